```python
import math
import jax, jax.numpy as jnp
from jax import lax
import numpy as np

D_MODEL = 1024
BATCH = 8
SEQ = 4096
DEPTH = 1

N_META = 16
SSD_HEAD_DIM = 64
SSD_INNER = D_MODEL
SSD_HEADS = SSD_INNER // SSD_HEAD_DIM
SSD_GROUPS = 2
SSD_STATE = 128
SSD_CONV = 4
CHUNK = 128
SB_HEAD_DIM = 64
SB_WIDTH = D_MODEL
SB_HEADS = SB_WIDTH // SB_HEAD_DIM
Q_BLOCK = 128
MIX_WIDTH = SSD_INNER + SB_WIDTH
D_FF = 256 * ((8 * D_MODEL // 3 + 255) // 256)
FFN_CONV = 3
EPS = 1e-6

XBC_WIDTH = SSD_INNER + 2 * SSD_GROUPS * SSD_STATE
OFF_Z = 0
OFF_XBC = OFF_Z + SSD_INNER
OFF_DT = OFF_XBC + XBC_WIDTH
OFF_Q = OFF_DT + SSD_HEADS
OFF_K = OFF_Q + SB_WIDTH
OFF_V = OFF_K + SB_WIDTH
IN_COLS = OFF_V + SB_WIDTH

kernel_name = "hymba_ssd_stickbreaking_convffn_layer"


def rms_norm(x, g):
    x32 = x.astype(jnp.float32)
    y = x32 * lax.rsqrt(jnp.mean(x32 * x32, axis=-1, keepdims=True) + EPS)
    return (y * g.astype(jnp.float32)).astype(x.dtype)


def causal_dwconv(x, w, b):
    K = w.shape[0]
    L = x.shape[1]
    xp = jnp.pad(x, ((0, 0), (K - 1, 0), (0, 0)))
    y = b
    for k in range(K):
        y = y + xp[:, k:k + L] * w[k]
    return y


def ssd_mixer(z, xbc, dt_raw, conv_w, conv_b, dt_bias, a_log, d_skip, norm_g):
    out_dtype = z.dtype
    Bsz, L, _ = xbc.shape
    H, P, G, N = SSD_HEADS, SSD_HEAD_DIM, SSD_GROUPS, SSD_STATE
    J = H // G
    f32 = jnp.float32
    xbc = jax.nn.silu(causal_dwconv(xbc, conv_w, conv_b)).astype(f32)
    xs = xbc[..., :SSD_INNER].reshape(Bsz, L, H, P)
    Bm = xbc[..., SSD_INNER:SSD_INNER + G * N].reshape(Bsz, L, G, N)
    Cm = xbc[..., SSD_INNER + G * N:].reshape(Bsz, L, G, N)
    dt = jax.nn.softplus(dt_raw.astype(f32) + dt_bias.astype(f32))
    A = -jnp.exp(a_log.astype(f32))

    pad = CHUNK - N_META
    Lp = L + pad
    nc = Lp // CHUNK

    def front_pad(t):
        return jnp.pad(t, ((0, 0), (pad, 0)) + ((0, 0),) * (t.ndim - 2))

    Xdt = front_pad(xs * dt[..., None]).reshape(Bsz, nc, CHUNK, G, J, P)
    Adt = front_pad(dt * A).reshape(Bsz, nc, CHUNK, G, J).transpose(0, 3, 4, 1, 2)
    Bc = front_pad(Bm).reshape(Bsz, nc, CHUNK, G, N)
    Cc = front_pad(Cm).reshape(Bsz, nc, CHUNK, G, N)

    Acs = jnp.cumsum(Adt, axis=-1)
    causal = jnp.tril(jnp.ones((CHUNK, CHUNK), dtype=bool))
    seg = Acs[..., :, None] - Acs[..., None, :]
    Ldec = jnp.where(causal, jnp.exp(jnp.where(causal, seg, 0.0)), 0.0)

    CB = jnp.einsum('bclgn,bcsgn->bgcls', Cc, Bc)
    y_diag = jnp.einsum('bgcls,bgjcls,bcsgjp->bclgjp', CB, Ldec, Xdt)

    decay_states = jnp.exp(Acs[..., -1:] - Acs)
    states = jnp.einsum('bclgn,bgjcl,bclgjp->bcgjpn', Bc, decay_states, Xdt)
    chunk_decay = jnp.exp(Acs[..., -1])

    def step(carry, inp):
        st, dec = inp
        return carry * dec[..., None, None] + st, carry

    init = jnp.zeros((Bsz, G, J, P, N), f32)
    _, prev = lax.scan(step, init, (states.transpose(1, 0, 2, 3, 4, 5),
                                    chunk_decay.transpose(3, 0, 1, 2)))
    prev = prev.transpose(1, 0, 2, 3, 4, 5)

    y_off = jnp.einsum('bclgn,bcgjpn,bgjcl->bclgjp', Cc, prev, jnp.exp(Acs))

    y = (y_diag + y_off).reshape(Bsz, Lp, H, P)[:, pad:]
    y = y + xs * d_skip.astype(f32)[:, None]
    y = y.reshape(Bsz, L, SSD_INNER) * jax.nn.silu(z.astype(f32))
    return rms_norm(y, norm_g).astype(out_dtype)


def stick_breaking_attention(q, k, v):
    out_dtype = q.dtype
    Bsz, L, H, D = q.shape
    f32 = jnp.float32
    scale = 1.0 / math.sqrt(D)
    pad = Q_BLOCK - N_META
    Lp = L + pad
    nb = Lp // Q_BLOCK
    padw = ((0, 0), (pad, 0), (0, 0), (0, 0))
    qp = jnp.pad(q.astype(f32), padw)
    kp = jnp.pad(k.astype(f32), padw)
    vp = jnp.pad(v.astype(f32), padw)
    qb = qp.reshape(Bsz, nb, Q_BLOCK, H, D).transpose(1, 0, 2, 3, 4)
    key_pos = jnp.arange(Lp)

    def block(args):
        i, qi = args
        q_pos = i * Q_BLOCK + jnp.arange(Q_BLOCK)
        valid = (key_pos[None, :] < q_pos[:, None]) & (key_pos[None, :] >= pad)
        zlog = jnp.einsum('bqhd,bkhd->bhqk', qi, kp) * scale
        log_beta = jax.nn.log_sigmoid(zlog)
        log_keep = jnp.where(valid, log_beta - zlog, 0.0)
        after = lax.cumsum(log_keep, axis=3, reverse=True) - log_keep
        w = jnp.where(valid, jnp.exp(log_beta + after), 0.0)
        return jnp.einsum('bhqk,bkhd->bqhd', w, vp)

    o = lax.map(block, (jnp.arange(nb), qb))
    o = o.transpose(1, 0, 2, 3, 4).reshape(Bsz, Lp, H, D)[:, pad:]
    return o.astype(out_dtype)


def _fwd_setup_inputs(seed: int = 0) -> dict:
    key = jax.random.key(seed)
    ks = jax.random.split(key, 20)
    f32 = jnp.float32

    def gain(k, n):
        return 1.0 + 0.05 * jax.random.normal(k, (DEPTH, n), f32)

    dt0 = jnp.exp(jax.random.uniform(ks[5], (DEPTH, SSD_HEADS), f32,
                                     math.log(1e-3), math.log(1e-1)))
    dt_bias = dt0 + jnp.log(-jnp.expm1(-dt0))
    return {
        "x": jax.random.normal(ks[0], (BATCH, SEQ, D_MODEL), f32),
        "meta_tokens": jax.random.normal(ks[1], (N_META, D_MODEL), f32),
        "mix_pre_g": gain(ks[2], D_MODEL),
        "w_in": jax.random.normal(ks[3], (DEPTH, D_MODEL, IN_COLS), f32) * D_MODEL ** -0.5,
        "ssd_conv_w": jax.random.normal(ks[4], (DEPTH, SSD_CONV, XBC_WIDTH), f32) * SSD_CONV ** -0.5,
        "ssd_conv_b": 0.01 * jax.random.normal(ks[6], (DEPTH, XBC_WIDTH), f32),
        "ssd_dt_bias": dt_bias,
        "ssd_a_log": jnp.log(jax.random.uniform(ks[7], (DEPTH, SSD_HEADS), f32, 1.0, 16.0)),
        "ssd_d": 1.0 + 0.1 * jax.random.normal(ks[8], (DEPTH, SSD_HEADS), f32),
        "ssd_norm_g": gain(ks[9], SSD_INNER),
        "sb_norm_g": gain(ks[10], SB_WIDTH),
        "w_out": jax.random.normal(ks[11], (DEPTH, MIX_WIDTH, D_MODEL), f32) * MIX_WIDTH ** -0.5,
        "mix_post_g": gain(ks[12], D_MODEL),
        "ffn_pre_g": gain(ks[13], D_MODEL),
        "w_up": jax.random.normal(ks[14], (DEPTH, D_MODEL, 2 * D_FF), f32) * D_MODEL ** -0.5,
        "ffn_conv_w": jax.random.normal(ks[15], (DEPTH, FFN_CONV, D_FF), f32) * FFN_CONV ** -0.5,
        "ffn_conv_b": 0.01 * jax.random.normal(ks[16], (DEPTH, D_FF), f32),
        "w_down": jax.random.normal(ks[17], (DEPTH, D_FF, D_MODEL), f32) * D_FF ** -0.5,
        "ffn_post_g": gain(ks[18], D_MODEL),
    }


def _fwd_reference(x, meta_tokens, mix_pre_g, w_in, ssd_conv_w, ssd_conv_b, ssd_dt_bias,
              ssd_a_log, ssd_d, ssd_norm_g, sb_norm_g, w_out, mix_post_g, ffn_pre_g,
              w_up, ffn_conv_w, ffn_conv_b, w_down, ffn_post_g):
    Bsz = x.shape[0]
    meta = jnp.broadcast_to(meta_tokens.astype(x.dtype)[None], (Bsz, N_META, x.shape[-1]))
    h = jnp.concatenate([meta, x], axis=1)
    L = h.shape[1]
    for l in range(DEPTH):
        xn = rms_norm(h, mix_pre_g[l])
        proj = xn @ w_in[l]
        z = proj[..., OFF_Z:OFF_XBC]
        xbc = proj[..., OFF_XBC:OFF_DT]
        dt_raw = proj[..., OFF_DT:OFF_Q]
        q = proj[..., OFF_Q:OFF_K].reshape(Bsz, L, SB_HEADS, SB_HEAD_DIM)
        k = proj[..., OFF_K:OFF_V].reshape(Bsz, L, SB_HEADS, SB_HEAD_DIM)
        v = proj[..., OFF_V:IN_COLS].reshape(Bsz, L, SB_HEADS, SB_HEAD_DIM)
        y_ssd = ssd_mixer(z, xbc, dt_raw, ssd_conv_w[l], ssd_conv_b[l], ssd_dt_bias[l],
                          ssd_a_log[l], ssd_d[l], ssd_norm_g[l])
        y_sb = rms_norm(stick_breaking_attention(q, k, v).reshape(Bsz, L, SB_WIDTH), sb_norm_g[l])
        mix = jnp.concatenate([y_ssd, y_sb], axis=-1) @ w_out[l]
        h = h + rms_norm(mix, mix_post_g[l])
        xn = rms_norm(h, ffn_pre_g[l])
        gu = xn @ w_up[l]
        g = causal_dwconv(gu[..., :D_FF], ffn_conv_w[l], ffn_conv_b[l])
        f = (jax.nn.gelu(g, approximate=True) * gu[..., D_FF:]) @ w_down[l]
        h = h + rms_norm(f, ffn_post_g[l])
    return h[:, N_META:]


import jax as _jax
import jax.numpy as _jnp

TWIN_FORMAT = 'train_step'
FWD_PARAMS = ['x', 'meta_tokens', 'mix_pre_g', 'w_in', 'ssd_conv_w', 'ssd_conv_b', 'ssd_dt_bias', 'ssd_a_log', 'ssd_d', 'ssd_norm_g', 'sb_norm_g', 'w_out', 'mix_post_g', 'ffn_pre_g', 'w_up', 'ffn_conv_w', 'ffn_conv_b', 'w_down', 'ffn_post_g']
TWIN_WEIGHTS = ['meta_tokens', 'mix_pre_g', 'w_in', 'ssd_conv_w', 'ssd_conv_b', 'ssd_dt_bias', 'ssd_a_log', 'ssd_d', 'ssd_norm_g', 'sb_norm_g', 'w_out', 'mix_post_g', 'ffn_pre_g', 'w_up', 'ffn_conv_w', 'ffn_conv_b', 'w_down', 'ffn_post_g']
TWIN_DIFF_INPUT = 'x'
TWIN_INPUTS = ['x', 'meta_tokens', 'mix_pre_g', 'w_in', 'ssd_conv_w', 'ssd_conv_b', 'ssd_dt_bias', 'ssd_a_log', 'ssd_d', 'ssd_norm_g', 'sb_norm_g', 'w_out', 'mix_post_g', 'ffn_pre_g', 'w_up', 'ffn_conv_w', 'ffn_conv_b', 'w_down', 'ffn_post_g', 'loss_target', 'm_meta_tokens', 'm_mix_pre_g', 'm_w_in', 'm_ssd_conv_w', 'm_ssd_conv_b', 'm_ssd_dt_bias', 'm_ssd_a_log', 'm_ssd_d', 'm_ssd_norm_g', 'm_sb_norm_g', 'm_w_out', 'm_mix_post_g', 'm_ffn_pre_g', 'm_w_up', 'm_ffn_conv_w', 'm_ffn_conv_b', 'm_w_down', 'm_ffn_post_g', 'v_meta_tokens', 'v_mix_pre_g', 'v_w_in', 'v_ssd_conv_w', 'v_ssd_conv_b', 'v_ssd_dt_bias', 'v_ssd_a_log', 'v_ssd_d', 'v_ssd_norm_g', 'v_sb_norm_g', 'v_w_out', 'v_mix_post_g', 'v_ffn_pre_g', 'v_w_up', 'v_ffn_conv_w', 'v_ffn_conv_b', 'v_w_down', 'v_ffn_post_g']
TWIN_OUTPUTS = ['loss', 'grad_x', 'grad_meta_tokens', 'grad_mix_pre_g', 'grad_w_in', 'grad_ssd_conv_w', 'grad_ssd_conv_b', 'grad_ssd_dt_bias', 'grad_ssd_a_log', 'grad_ssd_d', 'grad_ssd_norm_g', 'grad_sb_norm_g', 'grad_w_out', 'grad_mix_post_g', 'grad_ffn_pre_g', 'grad_w_up', 'grad_ffn_conv_w', 'grad_ffn_conv_b', 'grad_w_down', 'grad_ffn_post_g', 'delta_meta_tokens', 'delta_mix_pre_g', 'delta_w_in', 'delta_ssd_conv_w', 'delta_ssd_conv_b', 'delta_ssd_dt_bias', 'delta_ssd_a_log', 'delta_ssd_d', 'delta_ssd_norm_g', 'delta_sb_norm_g', 'delta_w_out', 'delta_mix_post_g', 'delta_ffn_pre_g', 'delta_w_up', 'delta_ffn_conv_w', 'delta_ffn_conv_b', 'delta_w_down', 'delta_ffn_post_g', 'new_m_meta_tokens', 'new_m_mix_pre_g', 'new_m_w_in', 'new_m_ssd_conv_w', 'new_m_ssd_conv_b', 'new_m_ssd_dt_bias', 'new_m_ssd_a_log', 'new_m_ssd_d', 'new_m_ssd_norm_g', 'new_m_sb_norm_g', 'new_m_w_out', 'new_m_mix_post_g', 'new_m_ffn_pre_g', 'new_m_w_up', 'new_m_ffn_conv_w', 'new_m_ffn_conv_b', 'new_m_w_down', 'new_m_ffn_post_g', 'new_v_meta_tokens', 'new_v_mix_pre_g', 'new_v_w_in', 'new_v_ssd_conv_w', 'new_v_ssd_conv_b', 'new_v_ssd_dt_bias', 'new_v_ssd_a_log', 'new_v_ssd_d', 'new_v_ssd_norm_g', 'new_v_sb_norm_g', 'new_v_w_out', 'new_v_mix_post_g', 'new_v_ffn_pre_g', 'new_v_w_up', 'new_v_ffn_conv_w', 'new_v_ffn_conv_b', 'new_v_w_down', 'new_v_ffn_post_g']
TWIN_LEAF_KINDS = {'loss': 'loss', 'grad_x': 'grad_x', 'grad_meta_tokens': 'grad_w', 'grad_mix_pre_g': 'grad_w', 'grad_w_in': 'grad_w', 'grad_ssd_conv_w': 'grad_w', 'grad_ssd_conv_b': 'grad_w', 'grad_ssd_dt_bias': 'grad_w', 'grad_ssd_a_log': 'grad_w', 'grad_ssd_d': 'grad_w', 'grad_ssd_norm_g': 'grad_w', 'grad_sb_norm_g': 'grad_w', 'grad_w_out': 'grad_w', 'grad_mix_post_g': 'grad_w', 'grad_ffn_pre_g': 'grad_w', 'grad_w_up': 'grad_w', 'grad_ffn_conv_w': 'grad_w', 'grad_ffn_conv_b': 'grad_w', 'grad_w_down': 'grad_w', 'grad_ffn_post_g': 'grad_w', 'delta_meta_tokens': 'delta_w', 'delta_mix_pre_g': 'delta_w', 'delta_w_in': 'delta_w', 'delta_ssd_conv_w': 'delta_w', 'delta_ssd_conv_b': 'delta_w', 'delta_ssd_dt_bias': 'delta_w', 'delta_ssd_a_log': 'delta_w', 'delta_ssd_d': 'delta_w', 'delta_ssd_norm_g': 'delta_w', 'delta_sb_norm_g': 'delta_w', 'delta_w_out': 'delta_w', 'delta_mix_post_g': 'delta_w', 'delta_ffn_pre_g': 'delta_w', 'delta_w_up': 'delta_w', 'delta_ffn_conv_w': 'delta_w', 'delta_ffn_conv_b': 'delta_w', 'delta_w_down': 'delta_w', 'delta_ffn_post_g': 'delta_w', 'new_m_meta_tokens': 'new_m', 'new_m_mix_pre_g': 'new_m', 'new_m_w_in': 'new_m', 'new_m_ssd_conv_w': 'new_m', 'new_m_ssd_conv_b': 'new_m', 'new_m_ssd_dt_bias': 'new_m', 'new_m_ssd_a_log': 'new_m', 'new_m_ssd_d': 'new_m', 'new_m_ssd_norm_g': 'new_m', 'new_m_sb_norm_g': 'new_m', 'new_m_w_out': 'new_m', 'new_m_mix_post_g': 'new_m', 'new_m_ffn_pre_g': 'new_m', 'new_m_w_up': 'new_m', 'new_m_ffn_conv_w': 'new_m', 'new_m_ffn_conv_b': 'new_m', 'new_m_w_down': 'new_m', 'new_m_ffn_post_g': 'new_m', 'new_v_meta_tokens': 'new_v', 'new_v_mix_pre_g': 'new_v', 'new_v_w_in': 'new_v', 'new_v_ssd_conv_w': 'new_v', 'new_v_ssd_conv_b': 'new_v', 'new_v_ssd_dt_bias': 'new_v', 'new_v_ssd_a_log': 'new_v', 'new_v_ssd_d': 'new_v', 'new_v_ssd_norm_g': 'new_v', 'new_v_sb_norm_g': 'new_v', 'new_v_w_out': 'new_v', 'new_v_mix_post_g': 'new_v', 'new_v_ffn_pre_g': 'new_v', 'new_v_w_up': 'new_v', 'new_v_ffn_conv_w': 'new_v', 'new_v_ffn_conv_b': 'new_v', 'new_v_w_down': 'new_v', 'new_v_ffn_post_g': 'new_v'}


def _forward(args):
    return _fwd_reference(*[args[k] for k in FWD_PARAMS])


def _output_shape():
    out = _jax.eval_shape(lambda: _forward(_fwd_setup_inputs(0)))
    return out.shape, out.dtype

N_MICROBATCH = 1
ADAM_LR = 0.001
ADAM_B1 = 0.9
ADAM_B2 = 0.999
ADAM_EPS = 1e-08
ADAM_WD = 0.01
ADAM_STEP = 10
PER_EXAMPLE_BATCH_AXIS = {'x': 0, 'loss_target': 0}
SHARED_INPUTS = []
_WEIGHT_DTYPES = {'meta_tokens': _jnp.float32, 'mix_pre_g': _jnp.float32, 'w_in': _jnp.float32, 'ssd_conv_w': _jnp.float32, 'ssd_conv_b': _jnp.float32, 'ssd_dt_bias': _jnp.float32, 'ssd_a_log': _jnp.float32, 'ssd_d': _jnp.float32, 'ssd_norm_g': _jnp.float32, 'sb_norm_g': _jnp.float32, 'w_out': _jnp.float32, 'mix_post_g': _jnp.float32, 'ffn_pre_g': _jnp.float32, 'w_up': _jnp.float32, 'ffn_conv_w': _jnp.float32, 'ffn_conv_b': _jnp.float32, 'w_down': _jnp.float32, 'ffn_post_g': _jnp.float32}
MOMENT_SCALE = {'meta_tokens': 1.347042e-02, 'mix_pre_g': 6.182420e-01, 'w_in': 2.765453e-01, 'ssd_conv_w': 3.404641e-01, 'ssd_conv_b': 7.526498e-01, 'ssd_dt_bias': 9.143069e-01, 'ssd_a_log': 1.973229e+00, 'ssd_d': 2.221781e+00, 'ssd_norm_g': 5.645338e-01, 'sb_norm_g': 4.005132e-01, 'w_out': 5.997283e-01, 'mix_post_g': 3.220777e+01, 'ffn_pre_g': 6.241228e-01, 'w_up': 2.757242e-01, 'ffn_conv_w': 2.871262e-01, 'ffn_conv_b': 6.123065e-01, 'w_down': 5.790915e-01, 'ffn_post_g': 3.201562e+01}


def _to_microbatches(a, axis):
    t = _jnp.moveaxis(a, axis, 0)
    t = t.reshape((N_MICROBATCH, t.shape[0] // N_MICROBATCH) + t.shape[1:])
    return _jnp.moveaxis(t, 1, axis + 1)


def setup_inputs(seed: int = 0) -> dict:
    inp = _fwd_setup_inputs(seed)
    key = _jax.random.fold_in(_jax.random.key(seed), 7919)
    shape, _ = _output_shape()
    out = dict(inp)
    out["loss_target"] = _jax.random.normal(_jax.random.fold_in(key, 0), shape, _jnp.float32)
    for i, name in enumerate(TWIN_WEIGHTS):
        w = inp[name].astype(_jnp.float32)
        if MOMENT_SCALE is None:
            s = _jnp.sqrt(_jnp.mean(_jnp.square(w)) + 1e-30)
        else:
            s = MOMENT_SCALE[name]
        km, kv = _jax.random.split(_jax.random.fold_in(key, i + 1))
        out[name] = w
        out["m_" + name] = s * _jax.random.normal(km, w.shape, _jnp.float32)
        out["v_" + name] = (s * s) * _jax.random.uniform(kv, w.shape, _jnp.float32, 0.5, 1.5)
    if N_MICROBATCH > 1:
        for name, axis in PER_EXAMPLE_BATCH_AXIS.items():
            out[name] = _to_microbatches(out[name], axis)
    return {'x': out['x'], 'meta_tokens': out['meta_tokens'], 'mix_pre_g': out['mix_pre_g'], 'w_in': out['w_in'], 'ssd_conv_w': out['ssd_conv_w'], 'ssd_conv_b': out['ssd_conv_b'], 'ssd_dt_bias': out['ssd_dt_bias'], 'ssd_a_log': out['ssd_a_log'], 'ssd_d': out['ssd_d'], 'ssd_norm_g': out['ssd_norm_g'], 'sb_norm_g': out['sb_norm_g'], 'w_out': out['w_out'], 'mix_post_g': out['mix_post_g'], 'ffn_pre_g': out['ffn_pre_g'], 'w_up': out['w_up'], 'ffn_conv_w': out['ffn_conv_w'], 'ffn_conv_b': out['ffn_conv_b'], 'w_down': out['w_down'], 'ffn_post_g': out['ffn_post_g'], 'loss_target': out['loss_target'], 'm_meta_tokens': out['m_meta_tokens'], 'm_mix_pre_g': out['m_mix_pre_g'], 'm_w_in': out['m_w_in'], 'm_ssd_conv_w': out['m_ssd_conv_w'], 'm_ssd_conv_b': out['m_ssd_conv_b'], 'm_ssd_dt_bias': out['m_ssd_dt_bias'], 'm_ssd_a_log': out['m_ssd_a_log'], 'm_ssd_d': out['m_ssd_d'], 'm_ssd_norm_g': out['m_ssd_norm_g'], 'm_sb_norm_g': out['m_sb_norm_g'], 'm_w_out': out['m_w_out'], 'm_mix_post_g': out['m_mix_post_g'], 'm_ffn_pre_g': out['m_ffn_pre_g'], 'm_w_up': out['m_w_up'], 'm_ffn_conv_w': out['m_ffn_conv_w'], 'm_ffn_conv_b': out['m_ffn_conv_b'], 'm_w_down': out['m_w_down'], 'm_ffn_post_g': out['m_ffn_post_g'], 'v_meta_tokens': out['v_meta_tokens'], 'v_mix_pre_g': out['v_mix_pre_g'], 'v_w_in': out['v_w_in'], 'v_ssd_conv_w': out['v_ssd_conv_w'], 'v_ssd_conv_b': out['v_ssd_conv_b'], 'v_ssd_dt_bias': out['v_ssd_dt_bias'], 'v_ssd_a_log': out['v_ssd_a_log'], 'v_ssd_d': out['v_ssd_d'], 'v_ssd_norm_g': out['v_ssd_norm_g'], 'v_sb_norm_g': out['v_sb_norm_g'], 'v_w_out': out['v_w_out'], 'v_mix_post_g': out['v_mix_post_g'], 'v_ffn_pre_g': out['v_ffn_pre_g'], 'v_w_up': out['v_w_up'], 'v_ffn_conv_w': out['v_ffn_conv_w'], 'v_ffn_conv_b': out['v_ffn_conv_b'], 'v_w_down': out['v_w_down'], 'v_ffn_post_g': out['v_ffn_post_g']}


def _loss(weights, diff, rest, loss_target):
    with _jax.named_scope("forward"):
        args = {**rest, TWIN_DIFF_INPUT: diff, **{k: w.astype(_WEIGHT_DTYPES[k]) for k, w in weights.items()}}
        y = _forward(args)
    with _jax.named_scope("loss_head"):
        err = _jnp.square(y.astype(_jnp.float32) - loss_target)
        return 0.5 * _jnp.sum(_jnp.mean(err, axis=-1)) if err.ndim else 0.5 * err


def _adamw(w, g, m, v):
    m = ADAM_B1 * m + (1.0 - ADAM_B1) * g
    v = ADAM_B2 * v + (1.0 - ADAM_B2) * _jnp.square(g)
    m_hat = m / (1.0 - ADAM_B1 ** ADAM_STEP)
    v_hat = v / (1.0 - ADAM_B2 ** ADAM_STEP)
    delta = -ADAM_LR * (m_hat / (_jnp.sqrt(v_hat) + ADAM_EPS) + ADAM_WD * w)
    return delta, m, v


def reference(x, meta_tokens, mix_pre_g, w_in, ssd_conv_w, ssd_conv_b, ssd_dt_bias, ssd_a_log, ssd_d, ssd_norm_g, sb_norm_g, w_out, mix_post_g, ffn_pre_g, w_up, ffn_conv_w, ffn_conv_b, w_down, ffn_post_g, loss_target, m_meta_tokens, m_mix_pre_g, m_w_in, m_ssd_conv_w, m_ssd_conv_b, m_ssd_dt_bias, m_ssd_a_log, m_ssd_d, m_ssd_norm_g, m_sb_norm_g, m_w_out, m_mix_post_g, m_ffn_pre_g, m_w_up, m_ffn_conv_w, m_ffn_conv_b, m_w_down, m_ffn_post_g, v_meta_tokens, v_mix_pre_g, v_w_in, v_ssd_conv_w, v_ssd_conv_b, v_ssd_dt_bias, v_ssd_a_log, v_ssd_d, v_ssd_norm_g, v_sb_norm_g, v_w_out, v_mix_post_g, v_ffn_pre_g, v_w_up, v_ffn_conv_w, v_ffn_conv_b, v_w_down, v_ffn_post_g):
    given = dict(x=x, meta_tokens=meta_tokens, mix_pre_g=mix_pre_g, w_in=w_in, ssd_conv_w=ssd_conv_w, ssd_conv_b=ssd_conv_b, ssd_dt_bias=ssd_dt_bias, ssd_a_log=ssd_a_log, ssd_d=ssd_d, ssd_norm_g=ssd_norm_g, sb_norm_g=sb_norm_g, w_out=w_out, mix_post_g=mix_post_g, ffn_pre_g=ffn_pre_g, w_up=w_up, ffn_conv_w=ffn_conv_w, ffn_conv_b=ffn_conv_b, w_down=w_down, ffn_post_g=ffn_post_g, loss_target=loss_target, m_meta_tokens=m_meta_tokens, m_mix_pre_g=m_mix_pre_g, m_w_in=m_w_in, m_ssd_conv_w=m_ssd_conv_w, m_ssd_conv_b=m_ssd_conv_b, m_ssd_dt_bias=m_ssd_dt_bias, m_ssd_a_log=m_ssd_a_log, m_ssd_d=m_ssd_d, m_ssd_norm_g=m_ssd_norm_g, m_sb_norm_g=m_sb_norm_g, m_w_out=m_w_out, m_mix_post_g=m_mix_post_g, m_ffn_pre_g=m_ffn_pre_g, m_w_up=m_w_up, m_ffn_conv_w=m_ffn_conv_w, m_ffn_conv_b=m_ffn_conv_b, m_w_down=m_w_down, m_ffn_post_g=m_ffn_post_g, v_meta_tokens=v_meta_tokens, v_mix_pre_g=v_mix_pre_g, v_w_in=v_w_in, v_ssd_conv_w=v_ssd_conv_w, v_ssd_conv_b=v_ssd_conv_b, v_ssd_dt_bias=v_ssd_dt_bias, v_ssd_a_log=v_ssd_a_log, v_ssd_d=v_ssd_d, v_ssd_norm_g=v_ssd_norm_g, v_sb_norm_g=v_sb_norm_g, v_w_out=v_w_out, v_mix_post_g=v_mix_post_g, v_ffn_pre_g=v_ffn_pre_g, v_w_up=v_w_up, v_ffn_conv_w=v_ffn_conv_w, v_ffn_conv_b=v_ffn_conv_b, v_w_down=v_w_down, v_ffn_post_g=v_ffn_post_g)
    weights = {n: given[n] for n in TWIN_WEIGHTS}
    shared = {n: given[n] for n in SHARED_INPUTS}
    per_example = {n: given[n] for n in ['x']}
    grad_fn = _jax.value_and_grad(_loss, argnums=(0, 1))

    def one_microbatch(ex, loss_target):
        ex = dict(ex)
        diff = ex.pop(TWIN_DIFF_INPUT)
        return grad_fn(weights, diff, {**shared, **ex}, loss_target)

    if N_MICROBATCH == 1:
        loss, (grad_w, grad_x) = one_microbatch(per_example, given["loss_target"])
    else:
        def body(carry, xs):
            loss_sum, grad_sum = carry
            l_k, (gw_k, gx_k) = one_microbatch(xs[0], xs[1])
            with _jax.named_scope("update"):
                return (loss_sum + l_k, _jax.tree.map(_jnp.add, grad_sum, gw_k)), gx_k

        init = (_jnp.zeros((), _jnp.float32), _jax.tree.map(_jnp.zeros_like, weights))
        (loss, grad_w), grad_x = _jax.lax.scan(body, init, (per_example, given["loss_target"]))
    with _jax.named_scope("update"):
        delta_w, new_m, new_v = {}, {}, {}
        for n in TWIN_WEIGHTS:
            delta_w[n], new_m[n], new_v[n] = _adamw(weights[n], grad_w[n], given["m_" + n], given["v_" + n])
    return (loss, grad_x, *[grad_w[n] for n in TWIN_WEIGHTS], *[delta_w[n] for n in TWIN_WEIGHTS],
            *[new_m[n] for n in TWIN_WEIGHTS], *[new_v[n] for n in TWIN_WEIGHTS])
```

```python
import math

import jax
import jax.numpy as jnp
from jax import lax
from jax.experimental import pallas as pl
from jax.experimental.pallas import tpu as pltpu

F32 = jnp.float32
BF16 = jnp.bfloat16
HI = lax.Precision.HIGHEST

D_MODEL = 1024
N_META = 16
BLK = 128
PAD = BLK - N_META
HEADS = 16
HEAD_DIM = 64
SSD_GROUPS = 2
SSD_STATE = 128
HEADS_PER_GROUP = HEADS // SSD_GROUPS
SSD_INNER = HEADS * HEAD_DIM
SSD_CONV = 4
XBC = SSD_INNER + 2 * SSD_GROUPS * SSD_STATE
OFF_B = SSD_INNER
OFF_C = SSD_INNER + SSD_GROUPS * SSD_STATE
D_FF = 2816
FFN_CONV = 3
EPS = 1e-6
SB_SCALE = 1.0 / math.sqrt(HEAD_DIM)
N_DEV = 8
LANES = 128
HALO = 8

ADAM_LR = 0.001
ADAM_B1 = 0.9
ADAM_B2 = 0.999
ADAM_EPS = 1e-08
ADAM_WD = 0.01
ADAM_STEP = 10

VMEM_FLOOR = 32 << 20
VMEM_CEIL = 60 << 20
MM_BUDGET = 20 << 20
ROW_BUDGET = 6 << 20

NN = (((1,), (0,)), ((), ()))
NT = (((1,), (1,)), ((), ()))
TN = (((0,), (0,)), ((), ()))


def _params(tile_bytes, sem=None):
    limit = int(min(max(2 * tile_bytes + (8 << 20), VMEM_FLOOR), VMEM_CEIL))
    return pltpu.CompilerParams(vmem_limit_bytes=limit, dimension_semantics=sem)


def _nbytes(shape, dtype):
    n = 1
    for s in shape:
        n *= s
    return n * jnp.dtype(dtype).itemsize


def _dot(a, b, dims=NN, precision=None):
    return lax.dot_general(a, b, dims, precision=precision, preferred_element_type=F32)


def _split_dot(x, u):
    hi = x.astype(BF16)
    lo = (x - hi.astype(F32)).astype(BF16)
    return _dot(hi, u) + _dot(lo, u)


def _softplus(x):
    return jnp.maximum(x, 0.0) + jnp.log1p(jnp.exp(-jnp.abs(x)))


def _rms(x, g):
    r = lax.rsqrt(jnp.mean(x * x, axis=-1, keepdims=True) + EPS)
    return x * r * g


def _rms_bwd(dy, x, g):
    r = lax.rsqrt(jnp.mean(x * x, axis=-1, keepdims=True) + EPS)
    xh = x * r
    u = dy * g
    dx = r * (u - xh * jnp.mean(xh * u, axis=-1, keepdims=True))
    return dx, jnp.sum(dy * xh, axis=0, keepdims=True)


def _gelu(x):
    c = math.sqrt(2.0 / math.pi)
    return 0.5 * x * (1.0 + jnp.tanh(c * (x + 0.044715 * x * x * x)))


def _gelu_grad(x):
    c = math.sqrt(2.0 / math.pi)
    t = jnp.tanh(c * (x + 0.044715 * x * x * x))
    return 0.5 * (1.0 + t) + 0.5 * x * (1.0 - t * t) * c * (1.0 + 3.0 * 0.044715 * x * x)


def _row_tile(rows, bytes_per_row):
    big = 384
    return big if rows % big == 0 and big * bytes_per_row <= ROW_BUDGET else BLK


def _mm(name, pairs, mode, out_dtype):
    a0, b0 = pairs[0]
    if mode == "tn":
        m, n = a0.shape[1], b0.shape[1]
    elif mode == "nt":
        m, n = a0.shape[0], b0.shape[0]
    else:
        m, n = a0.shape[0], b0.shape[1]
    dims = {"nn": NN, "nt": NT, "tn": TN}[mode]

    def tile_bytes(tm, tn):
        tot = tm * tn * jnp.dtype(out_dtype).itemsize
        for a, b in pairs:
            k = a.shape[0] if mode == "tn" else a.shape[1]
            tot += tm * k * a.dtype.itemsize + tn * k * b.dtype.itemsize
        return tot

    cands_m = [t for t in (1408, 1024, 512, 384, 256, 128) if m % t == 0] or [m]
    cands_n = [t for t in (1408, 1024, 768, 512, 256, 128) if n % t == 0] or [n]
    best = None
    for tm in cands_m:
        for tn in cands_n:
            if tile_bytes(tm, tn) <= MM_BUDGET and (best is None or tm * tn > best[0] * best[1]):
                best = (tm, tn)
    tm, tn = best if best is not None else (cands_m[-1], cands_n[-1])
    npairs = len(pairs)

    def body(*refs):
        o_ref = refs[2 * npairs]
        acc = None
        for p in range(npairs):
            part = _dot(refs[2 * p][...], refs[2 * p + 1][...], dims)
            acc = part if acc is None else acc + part
        o_ref[...] = acc.astype(o_ref.dtype)

    in_specs, args = [], []
    for a, b in pairs:
        if mode == "tn":
            k = a.shape[0]
            in_specs += [pl.BlockSpec((k, tm), lambda i, j: (0, i)), pl.BlockSpec((k, tn), lambda i, j: (0, j))]
        elif mode == "nt":
            k = a.shape[1]
            in_specs += [pl.BlockSpec((tm, k), lambda i, j: (i, 0)), pl.BlockSpec((tn, k), lambda i, j: (j, 0))]
        else:
            k = a.shape[1]
            in_specs += [pl.BlockSpec((tm, k), lambda i, j: (i, 0)), pl.BlockSpec((k, tn), lambda i, j: (0, j))]
        args += [a, b]
    return pl.pallas_call(
        body, name=name, grid=(m // tm, n // tn), in_specs=in_specs,
        out_specs=pl.BlockSpec((tm, tn), lambda i, j: (i, j)),
        out_shape=jax.ShapeDtypeStruct((m, n), out_dtype),
        compiler_params=_params(tile_bytes(tm, tn), ("parallel", "parallel")),
    )(*args)


def _rowcall(name, fn, rows=(), prevs=(), nexts=(), pars=(), out_rows=(), out_accs=()):
    rows, prevs, nexts, pars = list(rows), list(prevs), list(nexts), list(pars)
    n_rows = (rows + prevs + nexts)[0].shape[0]
    per_row = sum(a.shape[1] * a.dtype.itemsize for a in rows + prevs + nexts)
    per_row += sum(c * jnp.dtype(dt).itemsize for c, dt in out_rows) + sum(a.shape[1] * 4 for a in prevs + nexts)
    tm = _row_tile(n_rows, per_row)
    nt = n_rows // tm
    hb = tm // HALO
    nr, npv, nnx, npar, nor, noa = len(rows), len(prevs), len(nexts), len(pars), len(out_rows), len(out_accs)

    def body(*refs):
        i = pl.program_id(0)
        k = 0
        row_refs = refs[k:k + nr]; k += nr
        pc = refs[k:k + npv]; k += npv
        ph = refs[k:k + npv]; k += npv
        nc = refs[k:k + nnx]; k += nnx
        nh = refs[k:k + nnx]; k += nnx
        par_refs = refs[k:k + npar]; k += npar
        orow = refs[k:k + nor]; k += nor
        oacc = refs[k:k + noa]; k += noa
        pscr = refs[k:k + npv]; k += npv
        nscr = refs[k:k + nnx]
        for c_, h_, s_ in zip(pc, ph, pscr):
            s_[0:HALO, :] = h_[...] * (i > 0).astype(F32)
            s_[HALO:HALO + tm, :] = c_[...]
        for c_, h_, s_ in zip(nc, nh, nscr):
            s_[0:tm, :] = c_[...]
            s_[tm:tm + HALO, :] = h_[...] * (i < nt - 1).astype(F32)
        prev_fns = [(lambda s, s_=s_: s_[pl.ds(HALO - s, tm), :]) for s_ in pscr]
        next_fns = [(lambda s, s_=s_: s_[pl.ds(s, tm), :]) for s_ in nscr]
        row_vals, acc_vals = fn(i, tm, row_refs, prev_fns, next_fns, par_refs)
        for r_, v in zip(orow, row_vals):
            r_[...] = v.astype(r_.dtype)
        if noa:
            @pl.when(i == 0)
            def _():
                for r_ in oacc:
                    r_[...] = jnp.zeros_like(r_)
            for r_, v in zip(oacc, acc_vals):
                r_[...] += v

    def row_spec(a):
        return pl.BlockSpec((tm, a.shape[1]), lambda i: (i, 0))

    def whole(shape):
        return pl.BlockSpec(shape, lambda i: (0, 0))

    in_specs = [row_spec(a) for a in rows]
    in_specs += [row_spec(a) for a in prevs]
    in_specs += [pl.BlockSpec((HALO, a.shape[1]), lambda i: (jnp.maximum(i * hb - 1, 0), 0)) for a in prevs]
    in_specs += [row_spec(a) for a in nexts]
    in_specs += [pl.BlockSpec((HALO, a.shape[1]), lambda i: (jnp.minimum((i + 1) * hb, n_rows // HALO - 1), 0)) for a in nexts]
    in_specs += [whole(p.shape) for p in pars]
    out_specs = [pl.BlockSpec((tm, c), lambda i: (i, 0)) for c, _ in out_rows] + [whole(s) for s in out_accs]
    out_shape = [jax.ShapeDtypeStruct((n_rows, c), dt) for c, dt in out_rows]
    out_shape += [jax.ShapeDtypeStruct(s, F32) for s in out_accs]
    scratch = [pltpu.VMEM((tm + HALO, a.shape[1]), F32) for a in prevs + nexts]
    tile = tm * per_row
    res = pl.pallas_call(
        body, name=name, grid=(nt,), in_specs=in_specs, out_specs=out_specs, out_shape=out_shape,
        scratch_shapes=scratch, compiler_params=_params(2 * tile, ("arbitrary",)),
    )(*rows, *prevs, *prevs, *nexts, *nexts, *pars)
    return res


def _row_ids(i, tm):
    return i * tm + lax.broadcasted_iota(jnp.int32, (tm, 1), 0)


def _conv_taps(prev_fn, w_ref, b_ref, taps):
    acc = b_ref[...]
    for k in range(taps):
        acc = acc + prev_fn(taps - 1 - k) * w_ref[k:k + 1, :]
    return acc


def _rms_fwd(name, h, g):
    def fn(i, tm, rows, prevs, nexts, pars):
        return [_rms(rows[0][...], pars[0][...])], []
    return _rowcall(name, fn, rows=[h], pars=[g], out_rows=[(h.shape[1], BF16)])[0]


def _ssd_conv_fwd(xbc_raw, w, b):
    def fn(i, tm, rows, prevs, nexts, pars):
        c = _conv_taps(prevs[0], pars[0], pars[1], SSD_CONV)
        y = c * jax.nn.sigmoid(c)
        return [jnp.where(_row_ids(i, tm) >= PAD, y, 0.0)], []
    return _rowcall("ssd_conv_fwd", fn, prevs=[xbc_raw], pars=[w, b], out_rows=[(XBC, F32)])[0]


def _mix_post(mix, h0, g_post, g_pre):
    def fn(i, tm, rows, prevs, nexts, pars):
        h1 = rows[1][...] + _rms(rows[0][...], pars[0][...])
        return [h1, _rms(h1, pars[1][...])], []
    return _rowcall("mix_post", fn, rows=[mix, h0], pars=[g_post, g_pre],
                    out_rows=[(D_MODEL, F32), (D_MODEL, BF16)])


def _ffn_act(g_raw, u, w, b):
    def fn(i, tm, rows, prevs, nexts, pars):
        g = _conv_taps(prevs[0], pars[0], pars[1], FFN_CONV)
        return [_gelu(g) * rows[0][...]], []
    return _rowcall("ffn_act", fn, rows=[u], prevs=[g_raw], pars=[w, b], out_rows=[(D_FF, BF16)])[0]


def _loss_post(f, h1, target, g_post):
    def fn(i, tm, rows, prevs, nexts, pars):
        fv, g = rows[0][...], pars[0][...]
        h2 = rows[1][...] + _rms(fv, g)
        real = _row_ids(i, tm) >= BLK
        diff = jnp.where(real, h2 - rows[2][...], 0.0)
        loss = 0.5 * jnp.sum(jnp.mean(diff * diff, axis=-1, keepdims=True))
        dh2 = diff * (1.0 / D_MODEL)
        df, dg = _rms_bwd(dh2, fv, g)
        return [dh2, df], [jnp.zeros((1, LANES), F32) + loss, dg]
    return _rowcall("loss_post", fn, rows=[f, h1, target], pars=[g_post],
                    out_rows=[(D_MODEL, F32), (D_MODEL, BF16)], out_accs=[(1, LANES), (1, D_MODEL)])


def _ffn_bwd_act(dact, u, g_raw, w, b):
    def fn(i, tm, rows, prevs, nexts, pars):
        g = _conv_taps(prevs[0], pars[0], pars[1], FFN_CONV)
        da = rows[0][...]
        return [da * rows[1][...] * _gelu_grad(g), da * _gelu(g)], []
    return _rowcall("ffn_bwd_act", fn, rows=[dact, u], prevs=[g_raw], pars=[w, b],
                    out_rows=[(D_FF, F32), (D_FF, BF16)])


def _conv_bwd(name, dy, x, w, taps):
    width = x.shape[1]

    def fn(i, tm, rows, prevs, nexts, pars):
        dy0 = nexts[0](0)
        dx = None
        for k in range(taps):
            term = nexts[0](taps - 1 - k) * pars[0][k:k + 1, :]
            dx = term if dx is None else dx + term
        dws = [jnp.sum(dy0 * prevs[0](taps - 1 - k), axis=0, keepdims=True) for k in range(taps)]
        return [dx], dws + [jnp.sum(dy0, axis=0, keepdims=True)]
    return _rowcall(name, fn, prevs=[x], nexts=[dy], pars=[w], out_rows=[(width, BF16)],
                    out_accs=[(1, width)] * (taps + 1))


def _mid_bwd(dxn2, h1, dh2, mix, g_pre, g_post):
    def fn(i, tm, rows, prevs, nexts, pars):
        d1, dg_pre = _rms_bwd(rows[0][...], rows[1][...], pars[0][...])
        dh1 = rows[2][...] + d1
        dmix, dg_post = _rms_bwd(dh1, rows[3][...], pars[1][...])
        return [dh1, dmix], [dg_pre, dg_post]
    return _rowcall("mid_bwd", fn, rows=[dxn2, h1, dh2, mix], pars=[g_pre, g_post],
                    out_rows=[(D_MODEL, F32), (D_MODEL, BF16)], out_accs=[(1, D_MODEL), (1, D_MODEL)])


def _norm_bwd(name, dy, x, g):
    def fn(i, tm, rows, prevs, nexts, pars):
        dx, dg = _rms_bwd(rows[0][...], rows[1][...], pars[0][...])
        return [dx], [dg]
    return _rowcall(name, fn, rows=[dy, x], pars=[g], out_rows=[(x.shape[1], F32)], out_accs=[(1, x.shape[1])])


def _ssd_conv_bwd_act(dact, xbc_raw, w, b):
    def fn(i, tm, rows, prevs, nexts, pars):
        c = _conv_taps(prevs[0], pars[0], pars[1], SSD_CONV)
        s = jax.nn.sigmoid(c)
        dc = rows[0][...] * s * (1.0 + c * (1.0 - s))
        return [jnp.where(_row_ids(i, tm) >= PAD, dc, 0.0)], []
    return _rowcall("ssd_conv_bwd_act", fn, rows=[dact], prevs=[xbc_raw], pars=[w, b], out_rows=[(XBC, F32)])[0]


def _first_bwd(dxn1, h0, dh1, g):
    def fn(i, tm, rows, prevs, nexts, pars):
        d0, dg = _rms_bwd(rows[0][...], rows[1][...], pars[0][...])
        return [rows[2][...] + d0], [dg]
    return _rowcall("first_bwd", fn, rows=[dxn1, h0, dh1], pars=[g], out_rows=[(D_MODEL, F32)],
                    out_accs=[(1, D_MODEL)])


def _ssd_chunk_terms(c, dtr_ref, bias_ref, alog_ref):
    ri = lax.broadcasted_iota(jnp.int32, (BLK, BLK), 0)
    ci = lax.broadcasted_iota(jnp.int32, (BLK, BLK), 1)
    causal = ri >= ci
    tril = causal.astype(F32)
    triu = (ri <= ci).astype(F32)
    rowmask = ((c * BLK + lax.broadcasted_iota(jnp.int32, (BLK, 1), 0)) >= PAD).astype(F32)
    dt = _softplus(dtr_ref[...] + bias_ref[...]) * rowmask
    a_neg = -jnp.exp(alog_ref[...])
    a = dt * a_neg
    cs = _dot(tril, a, NN, HI)
    cs_t = _dot(a, triu, TN, HI)
    return causal, triu, rowmask, dt, a_neg, cs, cs_t


def _decay_matrix(causal, cs_h, cs_t_h):
    return jnp.where(causal, jnp.exp(jnp.where(causal, cs_h - cs_t_h, 0.0)), 0.0)


def _ssd_fwd(xbc, dtr, z, dt_bias, a_log, d_skip, norm_g):
    n_rows = xbc.shape[0]
    nb = n_rows // BLK

    def body(xbc_ref, dtr_ref, z_ref, bias_ref, alog_ref, d_ref, g_ref, ypre_ref, yssd_ref, st_ref, state):
        c = pl.program_id(0)

        @pl.when(c == 0)
        def _():
            state[...] = jnp.zeros_like(state)

        st_ref[...] = state[...]
        causal, _, _, dt, _, cs, cs_t = _ssd_chunk_terms(c, dtr_ref, bias_ref, alog_ref)
        cs_last = cs[BLK - 1:BLK, :]
        for g in range(SSD_GROUPS):
            b_g = xbc_ref[:, OFF_B + g * SSD_STATE:OFF_B + (g + 1) * SSD_STATE]
            c_g = xbc_ref[:, OFF_C + g * SSD_STATE:OFF_C + (g + 1) * SSD_STATE]
            c_b = c_g.astype(BF16)
            cb = _dot(c_b, b_g.astype(BF16), NT)
            for j in range(HEADS_PER_GROUP):
                h = g * HEADS_PER_GROUP + j
                sl = slice(h * HEAD_DIM, (h + 1) * HEAD_DIM)
                xs = xbc_ref[:, sl]
                x_b = (xs * dt[:, h:h + 1]).astype(BF16)
                cs_h = cs[:, h:h + 1]
                cl = cs_last[:, h:h + 1]
                decay = _decay_matrix(causal, cs_h, cs_t[h:h + 1, :])
                s_h = state[:, sl]
                y = _dot((cb * decay).astype(BF16), x_b)
                y = y + jnp.exp(cs_h) * _dot(c_b, s_h.astype(BF16))
                to_end = jnp.exp(cl - cs_h)
                state[:, sl] = jnp.exp(cl) * s_h + _dot((b_g * to_end).astype(BF16), x_b, TN)
                ypre_ref[:, sl] = y + d_ref[:, h:h + 1] * xs
        zz = z_ref[...]
        yg = ypre_ref[...] * (zz * jax.nn.sigmoid(zz))
        yssd_ref[...] = _rms(yg, g_ref[...]).astype(yssd_ref.dtype)

    blk = lambda w: pl.BlockSpec((BLK, w), lambda c: (c, 0))
    par = lambda a: pl.BlockSpec(a.shape, lambda c: (0, 0))
    return pl.pallas_call(
        body, name="ssd_fwd", grid=(nb,),
        in_specs=[blk(XBC), blk(LANES), blk(SSD_INNER), par(dt_bias), par(a_log), par(d_skip), par(norm_g)],
        out_specs=[blk(SSD_INNER), blk(SSD_INNER), pl.BlockSpec((None, SSD_STATE, SSD_INNER), lambda c: (c, 0, 0))],
        out_shape=[jax.ShapeDtypeStruct((n_rows, SSD_INNER), F32), jax.ShapeDtypeStruct((n_rows, SSD_INNER), BF16),
                   jax.ShapeDtypeStruct((nb, SSD_STATE, SSD_INNER), F32)],
        scratch_shapes=[pltpu.VMEM((SSD_STATE, SSD_INNER), F32)],
        compiler_params=_params(8 << 20, ("arbitrary",)),
    )(xbc, dtr, z, dt_bias, a_log, d_skip, norm_g)


def _ssd_bwd(dy, ypre, z, xbc, dtr, states, dt_bias, a_log, d_skip, norm_g):
    n_rows = xbc.shape[0]
    nb = n_rows // BLK

    def body(dy_ref, ypre_ref, z_ref, xbc_ref, dtr_ref, st_ref, bias_ref, alog_ref, d_ref, g_ref,
             dz_ref, dxbc_ref, ddtr_ref, dgn_ref, dd_ref, dal_ref, ddtb_ref, dstate, dyp):
        step = pl.program_id(0)
        c = nb - 1 - step

        @pl.when(step == 0)
        def _():
            dstate[...] = jnp.zeros_like(dstate)
            for r_ in (dgn_ref, dd_ref, dal_ref, ddtb_ref):
                r_[...] = jnp.zeros_like(r_)

        yp, zz = ypre_ref[...], z_ref[...]
        sz = jax.nn.sigmoid(zz)
        silu = zz * sz
        dyg, dgn = _rms_bwd(dy_ref[...], yp * silu, g_ref[...])
        dgn_ref[...] += dgn
        dz_ref[...] = (dyg * yp * (sz * (1.0 + zz * (1.0 - sz)))).astype(dz_ref.dtype)
        dyp[...] = dyg * silu

        causal, triu, rowmask, dt, a_neg, cs, cs_t = _ssd_chunk_terms(c, dtr_ref, bias_ref, alog_ref)
        cs_last = cs[BLK - 1:BLK, :]
        lane = lax.broadcasted_iota(jnp.int32, (1, LANES), 1)
        sub = lax.broadcasted_iota(jnp.int32, (BLK, 1), 0)
        last_row = (sub == BLK - 1).astype(F32)
        dcs_col = jnp.zeros((BLK, LANES), F32)
        dcs_row = jnp.zeros((LANES, BLK), F32)
        ddt_x = jnp.zeros((BLK, LANES), F32)
        dd_row = jnp.zeros((1, LANES), F32)
        for g in range(SSD_GROUPS):
            b_g = xbc_ref[:, OFF_B + g * SSD_STATE:OFF_B + (g + 1) * SSD_STATE]
            c_g = xbc_ref[:, OFF_C + g * SSD_STATE:OFF_C + (g + 1) * SSD_STATE]
            b_b, c_b = b_g.astype(BF16), c_g.astype(BF16)
            cb = _dot(c_b, b_b, NT)
            db_g = jnp.zeros((BLK, SSD_STATE), F32)
            dc_g = jnp.zeros((BLK, SSD_STATE), F32)
            for j in range(HEADS_PER_GROUP):
                h = g * HEADS_PER_GROUP + j
                sl = slice(h * HEAD_DIM, (h + 1) * HEAD_DIM)
                d_y = dyp[:, sl]
                xs = xbc_ref[:, sl]
                dt_h = dt[:, h:h + 1]
                x_dt = xs * dt_h
                x_b, dy_b = x_dt.astype(BF16), d_y.astype(BF16)
                cs_h = cs[:, h:h + 1]
                cl = cs_last[:, h:h + 1]
                decay = _decay_matrix(causal, cs_h, cs_t[h:h + 1, :])
                s_h, ds_h = st_ref[:, sl], dstate[:, sl]
                s_b, ds_b = s_h.astype(BF16), ds_h.astype(BF16)
                to_end = jnp.exp(cl - cs_h)
                from_start = jnp.exp(cs_h)
                chunk_decay = jnp.exp(cl)

                d_x = _dot((cb * decay).astype(BF16), dy_b, TN) + to_end * _dot(b_b, ds_b)
                gl = _dot(dy_b, x_b, NT) * decay
                gl_b = gl.astype(BF16)
                c_s = _dot(c_b, s_b)
                x_ds = _dot(x_b, ds_b, NT)
                dc_g = dc_g + _dot(gl_b, b_b) + from_start * _dot(dy_b, s_b, NT)
                db_g = db_g + _dot(gl_b, c_b, TN) + to_end * x_ds
                w_m = gl * cb
                d_end = jnp.sum(x_ds * b_g, axis=1, keepdims=True) * to_end
                at_last = jnp.sum(d_end) + chunk_decay * jnp.sum(ds_h * s_h)
                col = (jnp.sum(w_m, axis=1, keepdims=True)
                       + from_start * jnp.sum(d_y * c_s, axis=1, keepdims=True)
                       - d_end + last_row * at_last)
                onehot = (lane == h).astype(F32)
                dcs_col = dcs_col + col * onehot
                dcs_row = dcs_row + (sub == h).astype(F32) * jnp.sum(w_m, axis=0, keepdims=True)
                ddt_x = ddt_x + jnp.sum(d_x * xs, axis=1, keepdims=True) * onehot
                dd_row = dd_row + jnp.sum(d_y * xs) * onehot
                dxbc_ref[:, sl] = d_x * dt_h + d_ref[:, h:h + 1] * d_y
                dstate[:, sl] = chunk_decay * ds_h + _dot((c_g * from_start).astype(BF16), dy_b, TN)
            dxbc_ref[:, OFF_B + g * SSD_STATE:OFF_B + (g + 1) * SSD_STATE] = db_g
            dxbc_ref[:, OFF_C + g * SSD_STATE:OFF_C + (g + 1) * SSD_STATE] = dc_g
        da = _dot(triu, dcs_col, NN, HI) - _dot(triu, dcs_row, NT, HI)
        ddt = (da * a_neg + ddt_x) * rowmask
        ddtr = ddt * jax.nn.sigmoid(dtr_ref[...] + bias_ref[...]) * (lane < HEADS).astype(F32)
        ddtr_ref[...] = ddtr.astype(ddtr_ref.dtype)
        ddtb_ref[...] += jnp.sum(ddtr, axis=0, keepdims=True)
        dal_ref[...] += jnp.sum(da * dt, axis=0, keepdims=True) * a_neg
        dd_ref[...] += dd_row

    blk = lambda w: pl.BlockSpec((BLK, w), lambda s: (nb - 1 - s, 0))
    par = lambda a: pl.BlockSpec(a.shape, lambda s: (0, 0))
    acc = lambda w: pl.BlockSpec((1, w), lambda s: (0, 0))
    return pl.pallas_call(
        body, name="ssd_bwd", grid=(nb,),
        in_specs=[blk(SSD_INNER), blk(SSD_INNER), blk(SSD_INNER), blk(XBC), blk(LANES),
                  pl.BlockSpec((None, SSD_STATE, SSD_INNER), lambda s: (nb - 1 - s, 0, 0)),
                  par(dt_bias), par(a_log), par(d_skip), par(norm_g)],
        out_specs=[blk(SSD_INNER), blk(XBC), blk(LANES), acc(SSD_INNER), acc(LANES), acc(LANES), acc(LANES)],
        out_shape=[jax.ShapeDtypeStruct((n_rows, SSD_INNER), BF16), jax.ShapeDtypeStruct((n_rows, XBC), F32),
                   jax.ShapeDtypeStruct((n_rows, LANES), BF16), jax.ShapeDtypeStruct((1, SSD_INNER), F32),
                   jax.ShapeDtypeStruct((1, LANES), F32), jax.ShapeDtypeStruct((1, LANES), F32),
                   jax.ShapeDtypeStruct((1, LANES), F32)],
        scratch_shapes=[pltpu.VMEM((SSD_STATE, SSD_INNER), F32), pltpu.VMEM((BLK, SSD_INNER), F32)],
        compiler_params=_params(10 << 20, ("arbitrary",)),
    )(dy, ypre, z, xbc, dtr, states, dt_bias, a_log, d_skip, norm_g)


def _sb_block(q_i, k_j, i, j, ri, ci):
    zl = _dot(q_i, k_j, NT) * SB_SCALE
    sp = jnp.log1p(jnp.exp(-jnp.abs(zl)))
    log_beta = jnp.minimum(zl, 0.0) - sp
    key = j * BLK + ci
    valid = (key < i * BLK + ri) & (key >= PAD)
    log_keep = jnp.where(valid, -jnp.maximum(zl, 0.0) - sp, 0.0)
    return log_beta, log_keep, valid


def _attn_fwd(q, k, v):
    heads, n_rows, hd = q.shape
    nb = n_rows // BLK

    def body(q_ref, k_ref, v_ref, o_ref, tot_ref):
        ri = lax.broadcasted_iota(jnp.int32, (BLK, BLK), 0)
        ci = lax.broadcasted_iota(jnp.int32, (BLK, BLK), 1)
        later = (ri > ci).astype(BF16)

        def q_block(i, carry):
            r0 = pl.multiple_of(i * BLK, BLK)
            q_i = q_ref[pl.ds(r0, BLK), :]

            def k_block(jj, st):
                acc, run = st
                j = i - jj
                c0 = pl.multiple_of(j * BLK, BLK)
                log_beta, log_keep, valid = _sb_block(q_i, k_ref[pl.ds(c0, BLK), :], i, j, ri, ci)
                after = _split_dot(log_keep, later) + run
                w = jnp.where(valid, jnp.exp(log_beta + after), 0.0)
                acc = acc + _dot(w.astype(BF16), v_ref[pl.ds(c0, BLK), :])
                return acc, run + jnp.sum(log_keep, axis=1, keepdims=True)

            acc, run = lax.fori_loop(0, i + 1, k_block, (jnp.zeros((BLK, hd), F32), jnp.zeros((BLK, 1), F32)))
            o_ref[pl.ds(r0, BLK), :] = acc
            tot_ref[pl.ds(r0, BLK), :] = run
            return carry

        lax.fori_loop(0, nb, q_block, 0)

    spec = pl.BlockSpec((None, n_rows, hd), lambda h: (h, 0, 0))
    tot_spec = pl.BlockSpec((None, n_rows, 1), lambda h: (h, 0, 0))
    return pl.pallas_call(
        body, name="attn_fwd", grid=(heads,), in_specs=[spec, spec, spec], out_specs=[spec, tot_spec],
        out_shape=[jax.ShapeDtypeStruct((heads, n_rows, hd), F32), jax.ShapeDtypeStruct((heads, n_rows, 1), F32)],
        compiler_params=_params(10 * n_rows * LANES * 4, ("parallel",)),
    )(q, k, v)


def _attn_bwd(q, k, v, keep_total, do):
    heads, n_rows, hd = q.shape
    nb = n_rows // BLK

    def body(q_ref, k_ref, v_ref, tot_ref, do_ref, dq_ref, dk_ref, dv_ref, dk_acc, dv_acc):
        ri = lax.broadcasted_iota(jnp.int32, (BLK, BLK), 0)
        ci = lax.broadcasted_iota(jnp.int32, (BLK, BLK), 1)
        not_after = (ri <= ci).astype(BF16)
        before = (ri < ci).astype(BF16)
        dk_acc[...] = jnp.zeros_like(dk_acc)
        dv_acc[...] = jnp.zeros_like(dv_acc)

        def q_block(i, carry):
            r0 = pl.multiple_of(i * BLK, BLK)
            q_i = q_ref[pl.ds(r0, BLK), :]
            do_b = do_ref[pl.ds(r0, BLK), :].astype(BF16)
            total = tot_ref[pl.ds(r0, BLK), :]

            def k_block(j, st):
                dq, run, run_g = st
                c0 = pl.multiple_of(j * BLK, BLK)
                k_j = k_ref[pl.ds(c0, BLK), :]
                log_beta, log_keep, valid = _sb_block(q_i, k_j, i, j, ri, ci)
                after = total - run - _split_dot(log_keep, not_after)
                w = jnp.where(valid, jnp.exp(log_beta + after), 0.0)
                g = _dot(do_b, v_ref[pl.ds(c0, BLK), :], NT) * w
                g_before = run_g + _split_dot(g, before)
                beta = jnp.exp(log_beta)
                dz = jnp.where(valid, g * (1.0 - beta) - beta * g_before, 0.0) * SB_SCALE
                dz_b = dz.astype(BF16)
                dk_acc[pl.ds(c0, BLK), :] += _dot(dz_b, q_i, TN)
                dv_acc[pl.ds(c0, BLK), :] += _dot(w.astype(BF16), do_b, TN)
                return (dq + _dot(dz_b, k_j), run + jnp.sum(log_keep, axis=1, keepdims=True),
                        run_g + jnp.sum(g, axis=1, keepdims=True))

            zero = jnp.zeros((BLK, 1), F32)
            dq, _, _ = lax.fori_loop(0, i + 1, k_block, (jnp.zeros((BLK, hd), F32), zero, zero))
            dq_ref[pl.ds(r0, BLK), :] = dq.astype(dq_ref.dtype)
            return carry

        lax.fori_loop(0, nb, q_block, 0)
        dk_ref[...] = dk_acc[...].astype(dk_ref.dtype)
        dv_ref[...] = dv_acc[...].astype(dv_ref.dtype)

    spec = pl.BlockSpec((None, n_rows, hd), lambda h: (h, 0, 0))
    tot_spec = pl.BlockSpec((None, n_rows, 1), lambda h: (h, 0, 0))
    out = jax.ShapeDtypeStruct((heads, n_rows, hd), BF16)
    return pl.pallas_call(
        body, name="attn_bwd", grid=(heads,), in_specs=[spec, spec, spec, tot_spec, spec], out_specs=[spec] * 3,
        out_shape=[out] * 3,
        scratch_shapes=[pltpu.VMEM((n_rows, hd), F32), pltpu.VMEM((n_rows, hd), F32)],
        compiler_params=_params(12 * n_rows * LANES * 4, ("parallel",)),
    )(q, k, v, keep_total, do)


def _peer_exchange(name, arrays, gather):
    n = len(arrays)

    def body(*refs):
        ins, outs = refs[:n], refs[n:2 * n]
        send_sems, recv_sems, local_sems = refs[2 * n:]
        x, y, c = lax.axis_index("x"), lax.axis_index("y"), lax.axis_index("c")
        me = 4 * x + 2 * y + c

        def peer(r):
            px = 1 - x if r & 4 else x
            py = 1 - y if r & 2 else y
            pc = 1 - c if r & 1 else c
            return (px, py, pc), 4 * px + 2 * py + pc

        copies = []
        for a in range(n):
            own = pltpu.make_async_copy(ins[a] if gather else ins[a].at[me], outs[a].at[me if gather else 0],
                                        local_sems.at[a])
            own.start()
            copies.append(own)
        remote = []
        for r in range(1, N_DEV):
            dev, idx = peer(r)
            for a in range(n):
                cp = pltpu.make_async_remote_copy(
                    src_ref=ins[a] if gather else ins[a].at[idx],
                    dst_ref=outs[a].at[me if gather else r],
                    send_sem=send_sems.at[a, r - 1], recv_sem=recv_sems.at[a, r - 1],
                    device_id=dev, device_id_type=pl.DeviceIdType.MESH)
                cp.start()
                remote.append((cp, a, r, idx))
        for cp, a, r, idx in remote:
            pltpu.make_async_remote_copy(
                src_ref=ins[a] if gather else ins[a].at[idx],
                dst_ref=outs[a].at[idx if gather else r],
                send_sem=send_sems.at[a, r - 1], recv_sem=recv_sems.at[a, r - 1],
                device_id=peer(r)[0], device_id_type=pl.DeviceIdType.MESH).wait_recv()
        for cp, a, r, idx in remote:
            cp.wait_send()
        for cp in copies:
            cp.wait()

    any_spec = pl.BlockSpec(memory_space=pl.ANY)
    out_shape = [jax.ShapeDtypeStruct(((N_DEV,) + a.shape) if gather else a.shape, a.dtype) for a in arrays]
    return pl.pallas_call(
        body, name=name, in_specs=[any_spec] * n, out_specs=[any_spec] * n, out_shape=out_shape,
        scratch_shapes=[pltpu.SemaphoreType.DMA((n, N_DEV - 1)), pltpu.SemaphoreType.DMA((n, N_DEV - 1)),
                        pltpu.SemaphoreType.DMA((n,))],
    )(*arrays)


def _sum_slots(name, x):
    def body(x_ref, o_ref):
        acc = x_ref[0]
        for s in range(1, N_DEV):
            acc = acc + x_ref[s]
        o_ref[...] = acc
    return pl.pallas_call(body, name=name, out_shape=jax.ShapeDtypeStruct(x.shape[1:], F32))(x)


def _adamw(name, w, slots, m, v):
    n_slots, rows, cols = slots.shape
    tr = next((t for t in (256, 176, 128) if rows % t == 0 and rows > t), rows)

    def body(w_ref, s_ref, m_ref, v_ref, g_ref, d_ref, nm_ref, nv_ref):
        g = s_ref[0]
        for s in range(1, n_slots):
            g = g + s_ref[s]
        nm = ADAM_B1 * m_ref[...] + (1.0 - ADAM_B1) * g
        nv = ADAM_B2 * v_ref[...] + (1.0 - ADAM_B2) * (g * g)
        m_hat = nm / (1.0 - ADAM_B1 ** ADAM_STEP)
        v_hat = nv / (1.0 - ADAM_B2 ** ADAM_STEP)
        g_ref[...] = g
        d_ref[...] = -ADAM_LR * (m_hat / (jnp.sqrt(v_hat) + ADAM_EPS) + ADAM_WD * w_ref[...])
        nm_ref[...] = nm
        nv_ref[...] = nv

    spec = pl.BlockSpec((tr, cols), lambda i: (i, 0))
    out = jax.ShapeDtypeStruct((rows, cols), F32)
    return pl.pallas_call(
        body, name=name, grid=(rows // tr,),
        in_specs=[spec, pl.BlockSpec((n_slots, tr, cols), lambda i: (0, i, 0)), spec, spec],
        out_specs=[spec] * 4, out_shape=[out] * 4,
        compiler_params=_params((n_slots + 7) * tr * cols * 4, ("parallel",)),
    )(w, slots, m, v)


def _to_heads(a):
    return a.reshape(a.shape[0], HEADS, HEAD_DIM).transpose(1, 0, 2)


def _from_heads(a):
    return a.transpose(1, 0, 2).reshape(a.shape[1], HEADS * HEAD_DIM)


def _pad_lanes(a):
    return jnp.pad(a, ((0, 0), (0, LANES - a.shape[1])))


def kernel(x, meta_tokens, mix_pre_g, w_in, ssd_conv_w, ssd_conv_b, ssd_dt_bias, ssd_a_log, ssd_d, ssd_norm_g, sb_norm_g, w_out, mix_post_g, ffn_pre_g, w_up, ffn_conv_w, ffn_conv_b, w_down, ffn_post_g, loss_target, m_meta_tokens, m_mix_pre_g, m_w_in, m_ssd_conv_w, m_ssd_conv_b, m_ssd_dt_bias, m_ssd_a_log, m_ssd_d, m_ssd_norm_g, m_sb_norm_g, m_w_out, m_mix_post_g, m_ffn_pre_g, m_w_up, m_ffn_conv_w, m_ffn_conv_b, m_w_down, m_ffn_post_g, v_meta_tokens, v_mix_pre_g, v_w_in, v_ssd_conv_w, v_ssd_conv_b, v_ssd_dt_bias, v_ssd_a_log, v_ssd_d, v_ssd_norm_g, v_sb_norm_g, v_w_out, v_mix_post_g, v_ffn_pre_g, v_w_up, v_ffn_conv_w, v_ffn_conv_b, v_w_down, v_ffn_post_g):
    seq = x.shape[1]
    me = 4 * lax.axis_index("x") + 2 * lax.axis_index("y") + lax.axis_index("c")
    in_cols = w_in.shape[2]
    up_cols = w_up.shape[2]
    out_rows = w_out.shape[1]
    down_rows = w_down.shape[1]

    g_in, g_out, g_up, g_down, g_meta, g_scw, g_fcw = _peer_exchange(
        "gather_weights",
        [w_in[0].astype(BF16), w_out[0].astype(BF16), w_up[0].astype(BF16), w_down[0].astype(BF16),
         meta_tokens, ssd_conv_w[0], ffn_conv_w[0]], gather=True)
    w_in_full = g_in.transpose(1, 0, 2).reshape(D_MODEL, N_DEV * in_cols)
    off = [0, SSD_INNER, SSD_INNER + XBC, SSD_INNER + XBC + HEADS]
    w_z = w_in_full[:, off[0]:off[1]]
    w_xbc = w_in_full[:, off[1]:off[2]]
    w_dt = _pad_lanes(w_in_full[:, off[2]:off[3]])
    w_q = w_in_full[:, off[3]:off[3] + SSD_INNER]
    w_k = w_in_full[:, off[3] + SSD_INNER:off[3] + 2 * SSD_INNER]
    w_v = w_in_full[:, off[3] + 2 * SSD_INNER:off[3] + 3 * SSD_INNER]
    w_out_full = g_out.reshape(N_DEV * out_rows, D_MODEL)
    wo_ssd, wo_sb = w_out_full[:SSD_INNER], w_out_full[SSD_INNER:]
    w_up_full = g_up.transpose(1, 0, 2).reshape(D_MODEL, N_DEV * up_cols)
    w_gate, w_lin = w_up_full[:, :D_FF], w_up_full[:, D_FF:]
    w_down_full = g_down.reshape(N_DEV * down_rows, D_MODEL)
    meta_full = g_meta.transpose(1, 0, 2).reshape(N_META, D_MODEL)
    scw_full = g_scw.transpose(1, 0, 2).reshape(SSD_CONV, XBC)
    fcw_full = g_fcw.transpose(1, 0, 2).reshape(FFN_CONV, D_FF)

    dt_bias_p, a_log_p, d_p = _pad_lanes(ssd_dt_bias), _pad_lanes(ssd_a_log), _pad_lanes(ssd_d)

    h0 = jnp.concatenate([jnp.zeros((PAD, D_MODEL), F32), meta_full, x[0]], axis=0)
    target = jnp.concatenate([jnp.zeros((BLK, D_MODEL), F32), loss_target[0]], axis=0)
    xn1 = _rms_fwd("rms_pre_mix", h0, mix_pre_g)
    z = _mm("proj_z", [(xn1, w_z)], "nn", F32)
    xbc_raw = _mm("proj_xbc", [(xn1, w_xbc)], "nn", F32)
    dtr = _mm("proj_dt", [(xn1, w_dt)], "nn", F32)
    q = _to_heads(_mm("proj_q", [(xn1, w_q)], "nn", BF16))
    k = _to_heads(_mm("proj_k", [(xn1, w_k)], "nn", BF16))
    v = _to_heads(_mm("proj_v", [(xn1, w_v)], "nn", BF16))
    xbc_act = _ssd_conv_fwd(xbc_raw, scw_full, ssd_conv_b)
    ypre, y_ssd, states = _ssd_fwd(xbc_act, dtr, z, dt_bias_p, a_log_p, d_p, ssd_norm_g)
    o_heads, keep_total = _attn_fwd(q, k, v)
    o = _from_heads(o_heads)
    y_sb = _rms_fwd("rms_sb", o, sb_norm_g)
    mix = _mm("mix_out", [(y_ssd, wo_ssd), (y_sb, wo_sb)], "nn", F32)
    h1, xn2 = _mix_post(mix, h0, mix_post_g, ffn_pre_g)
    g_raw = _mm("ffn_gate", [(xn2, w_gate)], "nn", F32)
    u = _mm("ffn_lin", [(xn2, w_lin)], "nn", F32)
    act = _ffn_act(g_raw, u, fcw_full, ffn_conv_b)
    f = _mm("ffn_down", [(act, w_down_full)], "nn", F32)
    dh2, df, loss_row, dg_ffn_post = _loss_post(f, h1, target, ffn_post_g)
    loss = lax.psum(loss_row[0, 0], ("x", "y", "c"))

    dact = _mm("d_act", [(df, w_down_full)], "nt", F32)
    dw_down = _mm("dw_down", [(act, df)], "tn", F32)
    dg_conv, du = _ffn_bwd_act(dact, u, g_raw, fcw_full, ffn_conv_b)
    dg_raw, dfcw0, dfcw1, dfcw2, dfcb = _conv_bwd("ffn_conv_bwd", dg_conv, g_raw, fcw_full, FFN_CONV)
    dxn2 = _mm("d_xn2", [(dg_raw, w_gate), (du, w_lin)], "nt", F32)
    dw_gate = _mm("dw_gate", [(xn2, dg_raw)], "tn", F32)
    dw_lin = _mm("dw_lin", [(xn2, du)], "tn", F32)
    dh1, dmix, dg_ffn_pre, dg_mix_post = _mid_bwd(dxn2, h1, dh2, mix, ffn_pre_g, mix_post_g)

    dy_ssd = _mm("d_yssd", [(dmix, wo_ssd)], "nt", F32)
    dy_sb = _mm("d_ysb", [(dmix, wo_sb)], "nt", F32)
    dwo_ssd = _mm("dw_out_ssd", [(y_ssd, dmix)], "tn", F32)
    dwo_sb = _mm("dw_out_sb", [(y_sb, dmix)], "tn", F32)
    do, dg_sb = _norm_bwd("sb_norm_bwd", dy_sb, o, sb_norm_g)
    dq, dk, dv = _attn_bwd(q, k, v, keep_total, _to_heads(do))
    dq, dk, dv = _from_heads(dq), _from_heads(dk), _from_heads(dv)
    dz, dxbc_act, ddtr, dg_ssd_norm, dd_skip, da_log, ddt_bias = _ssd_bwd(
        dy_ssd, ypre, z, xbc_act, dtr, states, dt_bias_p, a_log_p, d_p, ssd_norm_g)
    dconv = _ssd_conv_bwd_act(dxbc_act, xbc_raw, scw_full, ssd_conv_b)
    dxbc_raw, dscw0, dscw1, dscw2, dscw3, dscb = _conv_bwd("ssd_conv_bwd", dconv, xbc_raw, scw_full, SSD_CONV)
    segs = [(dz, w_z), (dxbc_raw, w_xbc), (ddtr, w_dt), (dq, w_q), (dk, w_k), (dv, w_v)]
    dxn1 = _mm("d_xn1", segs, "nt", F32)
    dw_segs = [_mm("dw_in_%d" % s, [(xn1, d)], "tn", F32) for s, (d, _) in enumerate(segs)]
    dw_segs[2] = dw_segs[2][:, :HEADS]
    dw_in = jnp.concatenate(dw_segs, axis=1)
    dh0, dg_mix_pre = _first_bwd(dxn1, h0, dh1, mix_pre_g)
    grad_x = dh0[BLK:][None]

    slab_in = dw_in.reshape(D_MODEL, N_DEV, in_cols).transpose(1, 0, 2)
    slab_out = jnp.concatenate([dwo_ssd, dwo_sb], axis=0).reshape(N_DEV, out_rows, D_MODEL)
    half = N_DEV // 2
    slab_up = jnp.concatenate([dw_gate.reshape(D_MODEL, half, up_cols).transpose(1, 0, 2),
                               dw_lin.reshape(D_MODEL, half, up_cols).transpose(1, 0, 2)], axis=0)
    slab_down = dw_down.reshape(N_DEV, down_rows, D_MODEL)
    l_in, l_out, l_up, l_down = _peer_exchange("scatter_grads", [slab_in, slab_out, slab_up, slab_down], gather=False)

    small = [dg_mix_pre, dscb, ddt_bias, da_log, dd_skip, dg_ssd_norm, dg_sb, dg_mix_post, dg_ffn_pre, dfcb,
             dg_ffn_post, dh0[PAD:BLK].reshape(1, -1), dscw0, dscw1, dscw2, dscw3, dfcw0, dfcw1, dfcw2]
    sizes = [a.shape[1] for a in small]
    total = sum(sizes)
    rows_packed = -(-total // (LANES * HALO)) * HALO
    packed = jnp.pad(jnp.concatenate(small, axis=1), ((0, 0), (0, rows_packed * LANES - total)))
    (gathered,) = _peer_exchange("gather_small_grads", [packed.reshape(rows_packed, LANES)], gather=True)
    summed = _sum_slots("sum_small_grads", gathered).reshape(1, rows_packed * LANES)
    pieces, at = [], 0
    for s in sizes:
        pieces.append(summed[:, at:at + s])
        at += s
    (g_mix_pre, g_scb, g_dtb, g_alog, g_dskip, g_ssd_norm, g_sb, g_mix_post, g_ffn_pre, g_fcb, g_ffn_post,
     g_meta_flat, gs0, gs1, gs2, gs3, gf0, gf1, gf2) = pieces
    g_dtb, g_alog, g_dskip = g_dtb[:, :HEADS], g_alog[:, :HEADS], g_dskip[:, :HEADS]
    g_meta_full = g_meta_flat.reshape(N_META, D_MODEL)
    g_scw_full = jnp.concatenate([gs0, gs1, gs2, gs3], axis=0)
    g_fcw_full = jnp.concatenate([gf0, gf1, gf2], axis=0)
    meta_cols, scw_cols, fcw_cols = meta_tokens.shape[1], ssd_conv_w.shape[2], ffn_conv_w.shape[2]
    g_meta_mine = lax.dynamic_slice(g_meta_full, (0, me * meta_cols), (N_META, meta_cols))
    g_scw_mine = lax.dynamic_slice(g_scw_full, (0, me * scw_cols), (SSD_CONV, scw_cols))
    g_fcw_mine = lax.dynamic_slice(g_fcw_full, (0, me * fcw_cols), (FFN_CONV, fcw_cols))

    def lead(a):
        return a[None]

    upd = [
        _adamw("adamw_meta", meta_tokens, lead(g_meta_mine), m_meta_tokens, v_meta_tokens),
        _adamw("adamw_mix_pre_g", mix_pre_g, lead(g_mix_pre), m_mix_pre_g, v_mix_pre_g),
        [lead(a) for a in _adamw("adamw_w_in", w_in[0], l_in, m_w_in[0], v_w_in[0])],
        [lead(a) for a in _adamw("adamw_ssd_conv_w", ssd_conv_w[0], lead(g_scw_mine), m_ssd_conv_w[0], v_ssd_conv_w[0])],
        _adamw("adamw_ssd_conv_b", ssd_conv_b, lead(g_scb), m_ssd_conv_b, v_ssd_conv_b),
        _adamw("adamw_ssd_dt_bias", ssd_dt_bias, lead(g_dtb), m_ssd_dt_bias, v_ssd_dt_bias),
        _adamw("adamw_ssd_a_log", ssd_a_log, lead(g_alog), m_ssd_a_log, v_ssd_a_log),
        _adamw("adamw_ssd_d", ssd_d, lead(g_dskip), m_ssd_d, v_ssd_d),
        _adamw("adamw_ssd_norm_g", ssd_norm_g, lead(g_ssd_norm), m_ssd_norm_g, v_ssd_norm_g),
        _adamw("adamw_sb_norm_g", sb_norm_g, lead(g_sb), m_sb_norm_g, v_sb_norm_g),
        [lead(a) for a in _adamw("adamw_w_out", w_out[0], l_out, m_w_out[0], v_w_out[0])],
        _adamw("adamw_mix_post_g", mix_post_g, lead(g_mix_post), m_mix_post_g, v_mix_post_g),
        _adamw("adamw_ffn_pre_g", ffn_pre_g, lead(g_ffn_pre), m_ffn_pre_g, v_ffn_pre_g),
        [lead(a) for a in _adamw("adamw_w_up", w_up[0], l_up, m_w_up[0], v_w_up[0])],
        [lead(a) for a in _adamw("adamw_ffn_conv_w", ffn_conv_w[0], lead(g_fcw_mine), m_ffn_conv_w[0], v_ffn_conv_w[0])],
        _adamw("adamw_ffn_conv_b", ffn_conv_b, lead(g_fcb), m_ffn_conv_b, v_ffn_conv_b),
        [lead(a) for a in _adamw("adamw_w_down", w_down[0], l_down, m_w_down[0], v_w_down[0])],
        _adamw("adamw_ffn_post_g", ffn_post_g, lead(g_ffn_post), m_ffn_post_g, v_ffn_post_g),
    ]
    grads = [u_[0] for u_ in upd]
    deltas = [u_[1] for u_ in upd]
    new_m = [u_[2] for u_ in upd]
    new_v = [u_[3] for u_ in upd]
    return (loss, grad_x, *grads, *deltas, *new_m, *new_v)
```

```python
import math

import jax
import jax.numpy as jnp
from jax import lax
from jax.experimental import pallas as pl
from jax.experimental.pallas import tpu as pltpu

F32 = jnp.float32
BF16 = jnp.bfloat16
HI = lax.Precision.HIGHEST

D_MODEL = 1024
N_META = 16
BLK = 128
PAD = BLK - N_META
HEADS = 16
HEAD_DIM = 64
SSD_GROUPS = 2
SSD_STATE = 128
HEADS_PER_GROUP = HEADS // SSD_GROUPS
SSD_INNER = HEADS * HEAD_DIM
SSD_CONV = 4
XBC = SSD_INNER + 2 * SSD_GROUPS * SSD_STATE
OFF_B = SSD_INNER
OFF_C = SSD_INNER + SSD_GROUPS * SSD_STATE
D_FF = 2816
FFN_CONV = 3
EPS = 1e-6
SB_SCALE = 1.0 / math.sqrt(HEAD_DIM)
N_DEV = 8
LANES = 128
HALO = 8

ADAM_LR = 0.001
ADAM_B1 = 0.9
ADAM_B2 = 0.999
ADAM_EPS = 1e-08
ADAM_WD = 0.01
ADAM_STEP = 10

VMEM_FLOOR = 32 << 20
VMEM_CEIL = 60 << 20
MM_BUDGET = 20 << 20
ROW_BUDGET = 6 << 20

NN = (((1,), (0,)), ((), ()))
NT = (((1,), (1,)), ((), ()))
TN = (((0,), (0,)), ((), ()))


def _params(tile_bytes, sem=None):
    limit = int(min(max(2 * tile_bytes + (8 << 20), VMEM_FLOOR), VMEM_CEIL))
    return pltpu.CompilerParams(vmem_limit_bytes=limit, dimension_semantics=sem)


def _nbytes(shape, dtype):
    n = 1
    for s in shape:
        n *= s
    return n * jnp.dtype(dtype).itemsize


def _dot(a, b, dims=NN, precision=None):
    return lax.dot_general(a, b, dims, precision=precision, preferred_element_type=F32)


def _softplus(x):
    return jnp.maximum(x, 0.0) + jnp.log1p(jnp.exp(-jnp.abs(x)))


def _rms(x, g):
    r = lax.rsqrt(jnp.mean(x * x, axis=-1, keepdims=True) + EPS)
    return x * r * g


def _rms_bwd(dy, x, g):
    r = lax.rsqrt(jnp.mean(x * x, axis=-1, keepdims=True) + EPS)
    xh = x * r
    u = dy * g
    dx = r * (u - xh * jnp.mean(xh * u, axis=-1, keepdims=True))
    return dx, jnp.sum(dy * xh, axis=0, keepdims=True)


def _gelu(x):
    c = math.sqrt(2.0 / math.pi)
    return 0.5 * x * (1.0 + jnp.tanh(c * (x + 0.044715 * x * x * x)))


def _gelu_grad(x):
    c = math.sqrt(2.0 / math.pi)
    t = jnp.tanh(c * (x + 0.044715 * x * x * x))
    return 0.5 * (1.0 + t) + 0.5 * x * (1.0 - t * t) * c * (1.0 + 3.0 * 0.044715 * x * x)


def _row_tile(rows, bytes_per_row):
    big = 384
    return big if rows % big == 0 and big * bytes_per_row <= ROW_BUDGET else BLK


def _mm(name, pairs, mode, out_dtype):
    a0, b0 = pairs[0]
    if mode == "tn":
        m, n = a0.shape[1], b0.shape[1]
    elif mode == "nt":
        m, n = a0.shape[0], b0.shape[0]
    else:
        m, n = a0.shape[0], b0.shape[1]
    dims = {"nn": NN, "nt": NT, "tn": TN}[mode]

    def tile_bytes(tm, tn):
        tot = tm * tn * jnp.dtype(out_dtype).itemsize
        for a, b in pairs:
            k = a.shape[0] if mode == "tn" else a.shape[1]
            tot += tm * k * a.dtype.itemsize + tn * k * b.dtype.itemsize
        return tot

    cands_m = [t for t in (1408, 1024, 512, 384, 256, 128) if m % t == 0] or [m]
    cands_n = [t for t in (1408, 1024, 768, 512, 256, 128) if n % t == 0] or [n]
    best = None
    for tm in cands_m:
        for tn in cands_n:
            if tile_bytes(tm, tn) <= MM_BUDGET and (best is None or tm * tn > best[0] * best[1]):
                best = (tm, tn)
    tm, tn = best if best is not None else (cands_m[-1], cands_n[-1])
    npairs = len(pairs)

    def body(*refs):
        o_ref = refs[2 * npairs]
        acc = None
        for p in range(npairs):
            part = _dot(refs[2 * p][...], refs[2 * p + 1][...], dims)
            acc = part if acc is None else acc + part
        o_ref[...] = acc.astype(o_ref.dtype)

    in_specs, args = [], []
    for a, b in pairs:
        if mode == "tn":
            k = a.shape[0]
            in_specs += [pl.BlockSpec((k, tm), lambda i, j: (0, i)), pl.BlockSpec((k, tn), lambda i, j: (0, j))]
        elif mode == "nt":
            k = a.shape[1]
            in_specs += [pl.BlockSpec((tm, k), lambda i, j: (i, 0)), pl.BlockSpec((tn, k), lambda i, j: (j, 0))]
        else:
            k = a.shape[1]
            in_specs += [pl.BlockSpec((tm, k), lambda i, j: (i, 0)), pl.BlockSpec((k, tn), lambda i, j: (0, j))]
        args += [a, b]
    return pl.pallas_call(
        body, name=name, grid=(m // tm, n // tn), in_specs=in_specs,
        out_specs=pl.BlockSpec((tm, tn), lambda i, j: (i, j)),
        out_shape=jax.ShapeDtypeStruct((m, n), out_dtype),
        compiler_params=_params(tile_bytes(tm, tn), ("parallel", "parallel")),
    )(*args)


def _rowcall(name, fn, rows=(), prevs=(), nexts=(), pars=(), out_rows=(), out_accs=()):
    rows, prevs, nexts, pars = list(rows), list(prevs), list(nexts), list(pars)
    n_rows = (rows + prevs + nexts)[0].shape[0]
    per_row = sum(a.shape[1] * a.dtype.itemsize for a in rows + prevs + nexts)
    per_row += sum(c * jnp.dtype(dt).itemsize for c, dt in out_rows) + sum(a.shape[1] * 4 for a in prevs + nexts)
    tm = _row_tile(n_rows, per_row)
    nt = n_rows // tm
    hb = tm // HALO
    nr, npv, nnx, npar, nor, noa = len(rows), len(prevs), len(nexts), len(pars), len(out_rows), len(out_accs)

    def body(*refs):
        i = pl.program_id(0)
        k = 0
        row_refs = refs[k:k + nr]; k += nr
        pc = refs[k:k + npv]; k += npv
        ph = refs[k:k + npv]; k += npv
        nc = refs[k:k + nnx]; k += nnx
        nh = refs[k:k + nnx]; k += nnx
        par_refs = refs[k:k + npar]; k += npar
        orow = refs[k:k + nor]; k += nor
        oacc = refs[k:k + noa]; k += noa
        pscr = refs[k:k + npv]; k += npv
        nscr = refs[k:k + nnx]
        for c_, h_, s_ in zip(pc, ph, pscr):
            s_[0:HALO, :] = h_[...] * (i > 0).astype(F32)
            s_[HALO:HALO + tm, :] = c_[...]
        for c_, h_, s_ in zip(nc, nh, nscr):
            s_[0:tm, :] = c_[...]
            s_[tm:tm + HALO, :] = h_[...] * (i < nt - 1).astype(F32)
        prev_fns = [(lambda s, s_=s_: s_[pl.ds(HALO - s, tm), :]) for s_ in pscr]
        next_fns = [(lambda s, s_=s_: s_[pl.ds(s, tm), :]) for s_ in nscr]
        row_vals, acc_vals = fn(i, tm, row_refs, prev_fns, next_fns, par_refs)
        for r_, v in zip(orow, row_vals):
            r_[...] = v.astype(r_.dtype)
        if noa:
            @pl.when(i == 0)
            def _():
                for r_ in oacc:
                    r_[...] = jnp.zeros_like(r_)
            for r_, v in zip(oacc, acc_vals):
                r_[...] += v

    def row_spec(a):
        return pl.BlockSpec((tm, a.shape[1]), lambda i: (i, 0))

    def whole(shape):
        return pl.BlockSpec(shape, lambda i: (0, 0))

    in_specs = [row_spec(a) for a in rows]
    in_specs += [row_spec(a) for a in prevs]
    in_specs += [pl.BlockSpec((HALO, a.shape[1]), lambda i: (jnp.maximum(i * hb - 1, 0), 0)) for a in prevs]
    in_specs += [row_spec(a) for a in nexts]
    in_specs += [pl.BlockSpec((HALO, a.shape[1]), lambda i: (jnp.minimum((i + 1) * hb, n_rows // HALO - 1), 0)) for a in nexts]
    in_specs += [whole(p.shape) for p in pars]
    out_specs = [pl.BlockSpec((tm, c), lambda i: (i, 0)) for c, _ in out_rows] + [whole(s) for s in out_accs]
    out_shape = [jax.ShapeDtypeStruct((n_rows, c), dt) for c, dt in out_rows]
    out_shape += [jax.ShapeDtypeStruct(s, F32) for s in out_accs]
    scratch = [pltpu.VMEM((tm + HALO, a.shape[1]), F32) for a in prevs + nexts]
    tile = tm * per_row
    res = pl.pallas_call(
        body, name=name, grid=(nt,), in_specs=in_specs, out_specs=out_specs, out_shape=out_shape,
        scratch_shapes=scratch, compiler_params=_params(2 * tile, ("arbitrary",)),
    )(*rows, *prevs, *prevs, *nexts, *nexts, *pars)
    return res


def _row_ids(i, tm):
    return i * tm + lax.broadcasted_iota(jnp.int32, (tm, 1), 0)


def _conv_taps(prev_fn, w_ref, b_ref, taps):
    acc = b_ref[...]
    for k in range(taps):
        acc = acc + prev_fn(taps - 1 - k) * w_ref[k:k + 1, :]
    return acc


def _rms_fwd(name, h, g):
    def fn(i, tm, rows, prevs, nexts, pars):
        return [_rms(rows[0][...], pars[0][...])], []
    return _rowcall(name, fn, rows=[h], pars=[g], out_rows=[(h.shape[1], BF16)])[0]


def _ssd_conv_fwd(xbc_raw, w, b):
    def fn(i, tm, rows, prevs, nexts, pars):
        c = _conv_taps(prevs[0], pars[0], pars[1], SSD_CONV)
        y = c * jax.nn.sigmoid(c)
        return [jnp.where(_row_ids(i, tm) >= PAD, y, 0.0)], []
    return _rowcall("ssd_conv_fwd", fn, prevs=[xbc_raw], pars=[w, b], out_rows=[(XBC, F32)])[0]


def _mix_post(mix, h0, g_post, g_pre):
    def fn(i, tm, rows, prevs, nexts, pars):
        h1 = rows[1][...] + _rms(rows[0][...], pars[0][...])
        return [h1, _rms(h1, pars[1][...])], []
    return _rowcall("mix_post", fn, rows=[mix, h0], pars=[g_post, g_pre],
                    out_rows=[(D_MODEL, F32), (D_MODEL, BF16)])


def _ffn_act(g_raw, u, w, b):
    def fn(i, tm, rows, prevs, nexts, pars):
        g = _conv_taps(prevs[0], pars[0], pars[1], FFN_CONV)
        return [_gelu(g) * rows[0][...]], []
    return _rowcall("ffn_act", fn, rows=[u], prevs=[g_raw], pars=[w, b], out_rows=[(D_FF, BF16)])[0]


def _loss_post(f, h1, target, g_post):
    def fn(i, tm, rows, prevs, nexts, pars):
        fv, g = rows[0][...], pars[0][...]
        h2 = rows[1][...] + _rms(fv, g)
        real = _row_ids(i, tm) >= BLK
        diff = jnp.where(real, h2 - rows[2][...], 0.0)
        loss = 0.5 * jnp.sum(jnp.mean(diff * diff, axis=-1, keepdims=True))
        dh2 = diff * (1.0 / D_MODEL)
        df, dg = _rms_bwd(dh2, fv, g)
        return [dh2, df], [jnp.zeros((1, LANES), F32) + loss, dg]
    return _rowcall("loss_post", fn, rows=[f, h1, target], pars=[g_post],
                    out_rows=[(D_MODEL, F32), (D_MODEL, BF16)], out_accs=[(1, LANES), (1, D_MODEL)])


def _ffn_bwd_act(dact, u, g_raw, w, b):
    def fn(i, tm, rows, prevs, nexts, pars):
        g = _conv_taps(prevs[0], pars[0], pars[1], FFN_CONV)
        da = rows[0][...]
        return [da * rows[1][...] * _gelu_grad(g), da * _gelu(g)], []
    return _rowcall("ffn_bwd_act", fn, rows=[dact, u], prevs=[g_raw], pars=[w, b],
                    out_rows=[(D_FF, F32), (D_FF, BF16)])


def _conv_bwd(name, dy, x, w, taps):
    width = x.shape[1]

    def fn(i, tm, rows, prevs, nexts, pars):
        dy0 = nexts[0](0)
        dx = None
        for k in range(taps):
            term = nexts[0](taps - 1 - k) * pars[0][k:k + 1, :]
            dx = term if dx is None else dx + term
        dws = [jnp.sum(dy0 * prevs[0](taps - 1 - k), axis=0, keepdims=True) for k in range(taps)]
        return [dx], dws + [jnp.sum(dy0, axis=0, keepdims=True)]
    return _rowcall(name, fn, prevs=[x], nexts=[dy], pars=[w], out_rows=[(width, BF16)],
                    out_accs=[(1, width)] * (taps + 1))


def _mid_bwd(dxn2, h1, dh2, mix, g_pre, g_post):
    def fn(i, tm, rows, prevs, nexts, pars):
        d1, dg_pre = _rms_bwd(rows[0][...], rows[1][...], pars[0][...])
        dh1 = rows[2][...] + d1
        dmix, dg_post = _rms_bwd(dh1, rows[3][...], pars[1][...])
        return [dh1, dmix], [dg_pre, dg_post]
    return _rowcall("mid_bwd", fn, rows=[dxn2, h1, dh2, mix], pars=[g_pre, g_post],
                    out_rows=[(D_MODEL, F32), (D_MODEL, BF16)], out_accs=[(1, D_MODEL), (1, D_MODEL)])


def _norm_bwd(name, dy, x, g):
    def fn(i, tm, rows, prevs, nexts, pars):
        dx, dg = _rms_bwd(rows[0][...], rows[1][...], pars[0][...])
        return [dx], [dg]
    return _rowcall(name, fn, rows=[dy, x], pars=[g], out_rows=[(x.shape[1], BF16)], out_accs=[(1, x.shape[1])])


def _ssd_conv_bwd_act(dact, xbc_raw, w, b):
    def fn(i, tm, rows, prevs, nexts, pars):
        c = _conv_taps(prevs[0], pars[0], pars[1], SSD_CONV)
        s = jax.nn.sigmoid(c)
        dc = rows[0][...] * s * (1.0 + c * (1.0 - s))
        return [jnp.where(_row_ids(i, tm) >= PAD, dc, 0.0)], []
    return _rowcall("ssd_conv_bwd_act", fn, rows=[dact], prevs=[xbc_raw], pars=[w, b], out_rows=[(XBC, F32)])[0]


def _first_bwd(dxn1, h0, dh1, g):
    def fn(i, tm, rows, prevs, nexts, pars):
        d0, dg = _rms_bwd(rows[0][...], rows[1][...], pars[0][...])
        return [rows[2][...] + d0], [dg]
    return _rowcall("first_bwd", fn, rows=[dxn1, h0, dh1], pars=[g], out_rows=[(D_MODEL, F32)],
                    out_accs=[(1, D_MODEL)])


def _ssd_chunk_terms(c, dtr_ref, bias_ref, alog_ref):
    ri = lax.broadcasted_iota(jnp.int32, (BLK, BLK), 0)
    ci = lax.broadcasted_iota(jnp.int32, (BLK, BLK), 1)
    causal = ri >= ci
    tril = causal.astype(F32)
    triu = (ri <= ci).astype(F32)
    rowmask = ((c * BLK + lax.broadcasted_iota(jnp.int32, (BLK, 1), 0)) >= PAD).astype(F32)
    dt = _softplus(dtr_ref[...] + bias_ref[...]) * rowmask
    a_neg = -jnp.exp(alog_ref[...])
    a = dt * a_neg
    cs = _dot(tril, a, NN, HI)
    cs_t = _dot(a, triu, TN, HI)
    return causal, triu, rowmask, dt, a_neg, cs, cs_t


def _decay_matrix(causal, cs_h, cs_t_h):
    return jnp.where(causal, jnp.exp(jnp.where(causal, cs_h - cs_t_h, 0.0)), 0.0)


def _ssd_fwd(xbc, dtr, z, dt_bias, a_log, d_skip, norm_g):
    n_rows = xbc.shape[0]
    nb = n_rows // BLK

    def body(xbc_ref, dtr_ref, z_ref, bias_ref, alog_ref, d_ref, g_ref, ypre_ref, yssd_ref, st_ref, state):
        c = pl.program_id(0)

        @pl.when(c == 0)
        def _():
            state[...] = jnp.zeros_like(state)

        st_ref[...] = state[...]
        causal, _, _, dt, _, cs, cs_t = _ssd_chunk_terms(c, dtr_ref, bias_ref, alog_ref)
        cs_last = cs[BLK - 1:BLK, :]
        for g in range(SSD_GROUPS):
            b_g = xbc_ref[:, OFF_B + g * SSD_STATE:OFF_B + (g + 1) * SSD_STATE]
            c_g = xbc_ref[:, OFF_C + g * SSD_STATE:OFF_C + (g + 1) * SSD_STATE]
            c_b = c_g.astype(BF16)
            cb = _dot(c_b, b_g.astype(BF16), NT)
            for j in range(HEADS_PER_GROUP):
                h = g * HEADS_PER_GROUP + j
                sl = slice(h * HEAD_DIM, (h + 1) * HEAD_DIM)
                xs = xbc_ref[:, sl]
                x_b = (xs * dt[:, h:h + 1]).astype(BF16)
                cs_h = cs[:, h:h + 1]
                cl = cs_last[:, h:h + 1]
                decay = _decay_matrix(causal, cs_h, cs_t[h:h + 1, :])
                s_h = state[:, sl]
                y = _dot((cb * decay).astype(BF16), x_b)
                y = y + jnp.exp(cs_h) * _dot(c_b, s_h.astype(BF16))
                to_end = jnp.exp(cl - cs_h)
                state[:, sl] = jnp.exp(cl) * s_h + _dot((b_g * to_end).astype(BF16), x_b, TN)
                ypre_ref[:, sl] = y + d_ref[:, h:h + 1] * xs
        zz = z_ref[...]
        yg = ypre_ref[...] * (zz * jax.nn.sigmoid(zz))
        yssd_ref[...] = _rms(yg, g_ref[...]).astype(yssd_ref.dtype)

    blk = lambda w: pl.BlockSpec((BLK, w), lambda c: (c, 0))
    par = lambda a: pl.BlockSpec(a.shape, lambda c: (0, 0))
    return pl.pallas_call(
        body, name="ssd_fwd", grid=(nb,),
        in_specs=[blk(XBC), blk(LANES), blk(SSD_INNER), par(dt_bias), par(a_log), par(d_skip), par(norm_g)],
        out_specs=[blk(SSD_INNER), blk(SSD_INNER), pl.BlockSpec((None, SSD_STATE, SSD_INNER), lambda c: (c, 0, 0))],
        out_shape=[jax.ShapeDtypeStruct((n_rows, SSD_INNER), F32), jax.ShapeDtypeStruct((n_rows, SSD_INNER), BF16),
                   jax.ShapeDtypeStruct((nb, SSD_STATE, SSD_INNER), F32)],
        scratch_shapes=[pltpu.VMEM((SSD_STATE, SSD_INNER), F32)],
        compiler_params=_params(8 << 20, ("arbitrary",)),
    )(xbc, dtr, z, dt_bias, a_log, d_skip, norm_g)


def _ssd_bwd(dy, ypre, z, xbc, dtr, states, dt_bias, a_log, d_skip, norm_g):
    n_rows = xbc.shape[0]
    nb = n_rows // BLK

    def body(dy_ref, ypre_ref, z_ref, xbc_ref, dtr_ref, st_ref, bias_ref, alog_ref, d_ref, g_ref,
             dz_ref, dxbc_ref, ddtr_ref, dgn_ref, dd_ref, dal_ref, ddtb_ref, dstate, dyp):
        step = pl.program_id(0)
        c = nb - 1 - step

        @pl.when(step == 0)
        def _():
            dstate[...] = jnp.zeros_like(dstate)
            for r_ in (dgn_ref, dd_ref, dal_ref, ddtb_ref):
                r_[...] = jnp.zeros_like(r_)

        yp, zz = ypre_ref[...], z_ref[...]
        sz = jax.nn.sigmoid(zz)
        silu = zz * sz
        dyg, dgn = _rms_bwd(dy_ref[...], yp * silu, g_ref[...])
        dgn_ref[...] += dgn
        dz_ref[...] = (dyg * yp * (sz * (1.0 + zz * (1.0 - sz)))).astype(dz_ref.dtype)
        dyp[...] = dyg * silu

        causal, triu, rowmask, dt, a_neg, cs, cs_t = _ssd_chunk_terms(c, dtr_ref, bias_ref, alog_ref)
        cs_last = cs[BLK - 1:BLK, :]
        lane = lax.broadcasted_iota(jnp.int32, (1, LANES), 1)
        sub = lax.broadcasted_iota(jnp.int32, (BLK, 1), 0)
        last_row = (sub == BLK - 1).astype(F32)
        dcs_col = jnp.zeros((BLK, LANES), F32)
        dcs_row = jnp.zeros((LANES, BLK), F32)
        ddt_x = jnp.zeros((BLK, LANES), F32)
        dd_row = jnp.zeros((1, LANES), F32)
        for g in range(SSD_GROUPS):
            b_g = xbc_ref[:, OFF_B + g * SSD_STATE:OFF_B + (g + 1) * SSD_STATE]
            c_g = xbc_ref[:, OFF_C + g * SSD_STATE:OFF_C + (g + 1) * SSD_STATE]
            b_b, c_b = b_g.astype(BF16), c_g.astype(BF16)
            cb = _dot(c_b, b_b, NT)
            db_g = jnp.zeros((BLK, SSD_STATE), F32)
            dc_g = jnp.zeros((BLK, SSD_STATE), F32)
            for j in range(HEADS_PER_GROUP):
                h = g * HEADS_PER_GROUP + j
                sl = slice(h * HEAD_DIM, (h + 1) * HEAD_DIM)
                d_y = dyp[:, sl]
                xs = xbc_ref[:, sl]
                dt_h = dt[:, h:h + 1]
                x_dt = xs * dt_h
                x_b, dy_b = x_dt.astype(BF16), d_y.astype(BF16)
                cs_h = cs[:, h:h + 1]
                cl = cs_last[:, h:h + 1]
                decay = _decay_matrix(causal, cs_h, cs_t[h:h + 1, :])
                s_h, ds_h = st_ref[:, sl], dstate[:, sl]
                s_b, ds_b = s_h.astype(BF16), ds_h.astype(BF16)
                to_end = jnp.exp(cl - cs_h)
                from_start = jnp.exp(cs_h)
                chunk_decay = jnp.exp(cl)

                d_x = _dot((cb * decay).astype(BF16), dy_b, TN) + to_end * _dot(b_b, ds_b)
                gl = _dot(dy_b, x_b, NT) * decay
                gl_b = gl.astype(BF16)
                c_s = _dot(c_b, s_b)
                x_ds = _dot(x_b, ds_b, NT)
                dc_g = dc_g + _dot(gl_b, b_b) + from_start * _dot(dy_b, s_b, NT)
                db_g = db_g + _dot(gl_b, c_b, TN) + to_end * x_ds
                w_m = gl * cb
                d_end = jnp.sum(x_ds * b_g, axis=1, keepdims=True) * to_end
                at_last = jnp.sum(d_end) + chunk_decay * jnp.sum(ds_h * s_h)
                col = (jnp.sum(w_m, axis=1, keepdims=True)
                       + from_start * jnp.sum(d_y * c_s, axis=1, keepdims=True)
                       - d_end + last_row * at_last)
                onehot = (lane == h).astype(F32)
                dcs_col = dcs_col + col * onehot
                dcs_row = dcs_row + (sub == h).astype(F32) * jnp.sum(w_m, axis=0, keepdims=True)
                ddt_x = ddt_x + jnp.sum(d_x * xs, axis=1, keepdims=True) * onehot
                dd_row = dd_row + jnp.sum(d_y * xs) * onehot
                dxbc_ref[:, sl] = d_x * dt_h + d_ref[:, h:h + 1] * d_y
                dstate[:, sl] = chunk_decay * ds_h + _dot((c_g * from_start).astype(BF16), dy_b, TN)
            dxbc_ref[:, OFF_B + g * SSD_STATE:OFF_B + (g + 1) * SSD_STATE] = db_g
            dxbc_ref[:, OFF_C + g * SSD_STATE:OFF_C + (g + 1) * SSD_STATE] = dc_g
        da = _dot(triu, dcs_col, NN, HI) - _dot(triu, dcs_row, NT, HI)
        ddt = (da * a_neg + ddt_x) * rowmask
        ddtr = ddt * jax.nn.sigmoid(dtr_ref[...] + bias_ref[...]) * (lane < HEADS).astype(F32)
        ddtr_ref[...] = ddtr.astype(ddtr_ref.dtype)
        ddtb_ref[...] += jnp.sum(ddtr, axis=0, keepdims=True)
        dal_ref[...] += jnp.sum(da * dt, axis=0, keepdims=True) * a_neg
        dd_ref[...] += dd_row

    blk = lambda w: pl.BlockSpec((BLK, w), lambda s: (nb - 1 - s, 0))
    par = lambda a: pl.BlockSpec(a.shape, lambda s: (0, 0))
    acc = lambda w: pl.BlockSpec((1, w), lambda s: (0, 0))
    return pl.pallas_call(
        body, name="ssd_bwd", grid=(nb,),
        in_specs=[blk(SSD_INNER), blk(SSD_INNER), blk(SSD_INNER), blk(XBC), blk(LANES),
                  pl.BlockSpec((None, SSD_STATE, SSD_INNER), lambda s: (nb - 1 - s, 0, 0)),
                  par(dt_bias), par(a_log), par(d_skip), par(norm_g)],
        out_specs=[blk(SSD_INNER), blk(XBC), blk(LANES), acc(SSD_INNER), acc(LANES), acc(LANES), acc(LANES)],
        out_shape=[jax.ShapeDtypeStruct((n_rows, SSD_INNER), BF16), jax.ShapeDtypeStruct((n_rows, XBC), F32),
                   jax.ShapeDtypeStruct((n_rows, LANES), BF16), jax.ShapeDtypeStruct((1, SSD_INNER), F32),
                   jax.ShapeDtypeStruct((1, LANES), F32), jax.ShapeDtypeStruct((1, LANES), F32),
                   jax.ShapeDtypeStruct((1, LANES), F32)],
        scratch_shapes=[pltpu.VMEM((SSD_STATE, SSD_INNER), F32), pltpu.VMEM((BLK, SSD_INNER), F32)],
        compiler_params=_params(10 << 20, ("arbitrary",)),
    )(dy, ypre, z, xbc, dtr, states, dt_bias, a_log, d_skip, norm_g)


HEAD_GROUP = 4


def _sb_logits(zl, valid):
    log_keep = -(jnp.maximum(zl, 0.0) + jnp.log1p(jnp.exp(-jnp.abs(zl))))
    log_beta = log_keep + zl
    if valid is not None:
        log_keep = jnp.where(valid, log_keep, 0.0)
    return log_beta, log_keep


def _sums_dot(x, tri_and_ones):
    hi = x.astype(BF16)
    lo = (x - hi.astype(F32)).astype(BF16)
    both = _dot(jnp.concatenate([hi, lo], axis=1), tri_and_ones)
    return both[:, :BLK], both[:, BLK:]


def _tri_and_ones(tri):
    half = jnp.concatenate([tri, jnp.ones((BLK, BLK), F32)], axis=1)
    return jnp.concatenate([half, half], axis=0).astype(BF16)


def _tile_mask(i, j, ri, ci):
    key = j * BLK + ci
    return (key < i * BLK + ri) & (key >= PAD)


def _block_rows(j):
    return pl.ds(j * BLK if isinstance(j, int) else pl.multiple_of(j * BLK, BLK), BLK)


def _sweep(i, tile, leftwards):
    first, last = (i, 0) if leftwards else (0, i)
    tile(first, True)

    def mid(t, carry):
        tile(i - t if leftwards else t, False)
        return carry

    lax.fori_loop(1, i, mid, 0)

    @pl.when(i >= 1)
    def _():
        tile(last, True)


def _attn_fwd(q, k, v):
    n_rows, width = q.shape
    nb = n_rows // BLK
    gw = HEAD_GROUP * HEAD_DIM
    groups = width // gw

    def body(q_ref, k_ref, v_ref, o_ref, tot_ref, run_ref):
        ri = lax.broadcasted_iota(jnp.int32, (BLK, BLK), 0)
        ci = lax.broadcasted_iota(jnp.int32, (BLK, BLK), 1)
        sums = _tri_and_ones((ri > ci).astype(F32))

        def q_block(i, carry):
            r0 = pl.multiple_of(i * BLK, BLK)
            rows = pl.ds(r0, BLK)
            o_ref[rows, :] = jnp.zeros((BLK, gw), F32)
            run_ref[...] = jnp.zeros_like(run_ref)

            def tile(j, masked):
                cols = _block_rows(j)
                valid = _tile_mask(i, j, ri, ci) if masked else None
                q_i, k_j, v_j, o_i = q_ref[rows, :], k_ref[cols, :], v_ref[cols, :], o_ref[rows, :]
                heads = range(HEAD_GROUP)
                sls = [slice(h * HEAD_DIM, (h + 1) * HEAD_DIM) for h in heads]
                runs = [run_ref[h] for h in heads]
                zls = [_dot(q_i[:, sls[h]], k_j[:, sls[h]], NT) for h in heads]
                lgs = [_sb_logits(zls[h], valid) for h in heads]
                sms = [_sums_dot(lgs[h][1], sums) for h in heads]
                ws = [jnp.exp(lgs[h][0] + sms[h][0] + runs[h]) for h in heads]
                if masked:
                    ws = [jnp.where(valid, w, 0.0) for w in ws]
                outs = [_dot(ws[h].astype(BF16), v_j[:, sls[h]]) for h in heads]
                o_ref[rows, :] = o_i + jnp.concatenate(outs, axis=1)
                for h in heads:
                    run_ref[h] = runs[h] + sms[h][1]

            _sweep(i, tile, leftwards=True)
            for h in range(HEAD_GROUP):
                tot_ref[rows, h:h + 1] = run_ref[h][:, 0:1]
            return carry

        lax.fori_loop(0, nb, q_block, 0)

    spec = pl.BlockSpec((n_rows, gw), lambda g: (0, g))
    tot_spec = pl.BlockSpec((None, n_rows, HEAD_GROUP), lambda g: (g, 0, 0))
    return pl.pallas_call(
        body, name="attn_fwd", grid=(groups,), in_specs=[spec, spec, spec], out_specs=[spec, tot_spec],
        out_shape=[jax.ShapeDtypeStruct((n_rows, width), F32), jax.ShapeDtypeStruct((groups, n_rows, HEAD_GROUP), F32)],
        scratch_shapes=[pltpu.VMEM((HEAD_GROUP, BLK, BLK), F32)],
        compiler_params=_params(n_rows * (3 * gw * 2 + gw * 4 + LANES * 4), ("parallel",)),
    )(q, k, v)


def _attn_bwd(q, k, v, keep_total, do):
    n_rows, width = q.shape
    nb = n_rows // BLK
    gw = HEAD_GROUP * HEAD_DIM
    groups = width // gw

    def body(q_ref, k_ref, v_ref, tot_ref, do_ref, dq_ref, dk_ref, dv_ref, dq_acc, dk_acc, dv_acc, tot_b, run_ref, rung_ref):
        ri = lax.broadcasted_iota(jnp.int32, (BLK, BLK), 0)
        ci = lax.broadcasted_iota(jnp.int32, (BLK, BLK), 1)
        sums_keep = _tri_and_ones((ri <= ci).astype(F32))
        sums_g = _tri_and_ones((ri < ci).astype(F32))
        dk_acc[...] = jnp.zeros_like(dk_acc)
        dv_acc[...] = jnp.zeros_like(dv_acc)

        def q_block(i, carry):
            rows = _block_rows(i)
            dq_acc[...] = jnp.zeros_like(dq_acc)
            run_ref[...] = jnp.zeros_like(run_ref)
            rung_ref[...] = jnp.zeros_like(rung_ref)
            for h in range(HEAD_GROUP):
                tot_b[h] = jnp.broadcast_to(tot_ref[rows, h:h + 1], (BLK, BLK))

            def tile(j, masked):
                cols = _block_rows(j)
                valid = _tile_mask(i, j, ri, ci) if masked else None
                heads = range(HEAD_GROUP)
                sls = [slice(h * HEAD_DIM, (h + 1) * HEAD_DIM) for h in heads]
                q_i, do_i, k_j, v_j = q_ref[rows, :], do_ref[rows, :], k_ref[cols, :], v_ref[cols, :]
                zls = [_dot(q_i[:, sls[h]], k_j[:, sls[h]], NT) for h in heads]
                dws = [_dot(do_i[:, sls[h]], v_j[:, sls[h]], NT) for h in heads]
                lgs = [_sb_logits(zls[h], valid) for h in heads]
                sms = [_sums_dot(lgs[h][1], sums_keep) for h in heads]
                ws = [jnp.exp(lgs[h][0] + (tot_b[h] - run_ref[h] - sms[h][0])) for h in heads]
                if masked:
                    ws = [jnp.where(valid, w, 0.0) for w in ws]
                gs = [dws[h] * ws[h] for h in heads]
                gsm = [_sums_dot(gs[h], sums_g) for h in heads]
                w_bs = [w.astype(BF16) for w in ws]
                dvs = [_dot(w_bs[h], do_i[:, sls[h]], TN) for h in heads]
                dzs = []
                for h in heads:
                    beta = jnp.exp(lgs[h][0])
                    dz = gs[h] * (1.0 - beta) - beta * (gsm[h][0] + rung_ref[h])
                    dzs.append((jnp.where(valid, dz, 0.0) if masked else dz).astype(BF16))
                dqs = [_dot(dzs[h], k_j[:, sls[h]]) for h in heads]
                dks = [_dot(dzs[h], q_i[:, sls[h]], TN) for h in heads]
                dq_acc[...] += jnp.concatenate(dqs, axis=1)
                dk_acc[cols, :] += jnp.concatenate(dks, axis=1)
                dv_acc[cols, :] += jnp.concatenate(dvs, axis=1)
                for h in heads:
                    run_ref[h] += sms[h][1]
                    rung_ref[h] += gsm[h][1]

            _sweep(i, tile, leftwards=False)
            dq_ref[rows, :] = (dq_acc[...] * SB_SCALE).astype(dq_ref.dtype)
            return carry

        lax.fori_loop(0, nb, q_block, 0)
        dk_ref[...] = dk_acc[...].astype(dk_ref.dtype)
        dv_ref[...] = dv_acc[...].astype(dv_ref.dtype)

    spec = pl.BlockSpec((n_rows, gw), lambda g: (0, g))
    tot_spec = pl.BlockSpec((None, n_rows, HEAD_GROUP), lambda g: (g, 0, 0))
    out = jax.ShapeDtypeStruct((n_rows, width), BF16)
    tile_f32 = pltpu.VMEM((HEAD_GROUP, BLK, BLK), F32)
    return pl.pallas_call(
        body, name="attn_bwd", grid=(groups,), in_specs=[spec, spec, spec, tot_spec, spec], out_specs=[spec] * 3,
        out_shape=[out] * 3,
        scratch_shapes=[pltpu.VMEM((BLK, gw), F32), pltpu.VMEM((n_rows, gw), F32), pltpu.VMEM((n_rows, gw), F32),
                        tile_f32, tile_f32, tile_f32],
        compiler_params=_params(n_rows * (7 * gw * 2 + LANES * 4 + gw * 4), ("parallel",)),
    )(q, k, v, keep_total, do)


def _peer_exchange(name, arrays, gather):
    n = len(arrays)

    def body(*refs):
        ins, outs = refs[:n], refs[n:2 * n]
        send_sems, recv_sems, local_sems = refs[2 * n:]
        x, y, c = lax.axis_index("x"), lax.axis_index("y"), lax.axis_index("c")
        me = 4 * x + 2 * y + c

        def peer(r):
            px = 1 - x if r & 4 else x
            py = 1 - y if r & 2 else y
            pc = 1 - c if r & 1 else c
            return (px, py, pc), 4 * px + 2 * py + pc

        copies = []
        for a in range(n):
            own = pltpu.make_async_copy(ins[a] if gather else ins[a].at[me], outs[a].at[me if gather else 0],
                                        local_sems.at[a])
            own.start()
            copies.append(own)
        remote = []
        for r in range(1, N_DEV):
            dev, idx = peer(r)
            for a in range(n):
                cp = pltpu.make_async_remote_copy(
                    src_ref=ins[a] if gather else ins[a].at[idx],
                    dst_ref=outs[a].at[me if gather else r],
                    send_sem=send_sems.at[a, r - 1], recv_sem=recv_sems.at[a, r - 1],
                    device_id=dev, device_id_type=pl.DeviceIdType.MESH)
                cp.start()
                remote.append((cp, a, r, idx))
        for cp, a, r, idx in remote:
            pltpu.make_async_remote_copy(
                src_ref=ins[a] if gather else ins[a].at[idx],
                dst_ref=outs[a].at[idx if gather else r],
                send_sem=send_sems.at[a, r - 1], recv_sem=recv_sems.at[a, r - 1],
                device_id=peer(r)[0], device_id_type=pl.DeviceIdType.MESH).wait_recv()
        for cp, a, r, idx in remote:
            cp.wait_send()
        for cp in copies:
            cp.wait()

    any_spec = pl.BlockSpec(memory_space=pl.ANY)
    out_shape = [jax.ShapeDtypeStruct(((N_DEV,) + a.shape) if gather else a.shape, a.dtype) for a in arrays]
    return pl.pallas_call(
        body, name=name, in_specs=[any_spec] * n, out_specs=[any_spec] * n, out_shape=out_shape,
        scratch_shapes=[pltpu.SemaphoreType.DMA((n, N_DEV - 1)), pltpu.SemaphoreType.DMA((n, N_DEV - 1)),
                        pltpu.SemaphoreType.DMA((n,))],
    )(*arrays)


def _sum_slots(name, x):
    def body(x_ref, o_ref):
        acc = x_ref[0]
        for s in range(1, N_DEV):
            acc = acc + x_ref[s]
        o_ref[...] = acc
    return pl.pallas_call(body, name=name, out_shape=jax.ShapeDtypeStruct(x.shape[1:], F32))(x)


def _adamw(name, w, slots, m, v):
    n_slots, rows, cols = slots.shape
    tr = next((t for t in (256, 176, 128) if rows % t == 0 and rows > t), rows)

    def body(w_ref, s_ref, m_ref, v_ref, g_ref, d_ref, nm_ref, nv_ref):
        g = s_ref[0]
        for s in range(1, n_slots):
            g = g + s_ref[s]
        nm = ADAM_B1 * m_ref[...] + (1.0 - ADAM_B1) * g
        nv = ADAM_B2 * v_ref[...] + (1.0 - ADAM_B2) * (g * g)
        m_hat = nm / (1.0 - ADAM_B1 ** ADAM_STEP)
        v_hat = nv / (1.0 - ADAM_B2 ** ADAM_STEP)
        g_ref[...] = g
        d_ref[...] = -ADAM_LR * (m_hat / (jnp.sqrt(v_hat) + ADAM_EPS) + ADAM_WD * w_ref[...])
        nm_ref[...] = nm
        nv_ref[...] = nv

    spec = pl.BlockSpec((tr, cols), lambda i: (i, 0))
    out = jax.ShapeDtypeStruct((rows, cols), F32)
    return pl.pallas_call(
        body, name=name, grid=(rows // tr,),
        in_specs=[spec, pl.BlockSpec((n_slots, tr, cols), lambda i: (0, i, 0)), spec, spec],
        out_specs=[spec] * 4, out_shape=[out] * 4,
        compiler_params=_params((n_slots + 7) * tr * cols * 4, ("parallel",)),
    )(w, slots, m, v)


def _pad_lanes(a):
    return jnp.pad(a, ((0, 0), (0, LANES - a.shape[1])))


def kernel(x, meta_tokens, mix_pre_g, w_in, ssd_conv_w, ssd_conv_b, ssd_dt_bias, ssd_a_log, ssd_d, ssd_norm_g, sb_norm_g, w_out, mix_post_g, ffn_pre_g, w_up, ffn_conv_w, ffn_conv_b, w_down, ffn_post_g, loss_target, m_meta_tokens, m_mix_pre_g, m_w_in, m_ssd_conv_w, m_ssd_conv_b, m_ssd_dt_bias, m_ssd_a_log, m_ssd_d, m_ssd_norm_g, m_sb_norm_g, m_w_out, m_mix_post_g, m_ffn_pre_g, m_w_up, m_ffn_conv_w, m_ffn_conv_b, m_w_down, m_ffn_post_g, v_meta_tokens, v_mix_pre_g, v_w_in, v_ssd_conv_w, v_ssd_conv_b, v_ssd_dt_bias, v_ssd_a_log, v_ssd_d, v_ssd_norm_g, v_sb_norm_g, v_w_out, v_mix_post_g, v_ffn_pre_g, v_w_up, v_ffn_conv_w, v_ffn_conv_b, v_w_down, v_ffn_post_g):
    seq = x.shape[1]
    me = 4 * lax.axis_index("x") + 2 * lax.axis_index("y") + lax.axis_index("c")
    in_cols = w_in.shape[2]
    up_cols = w_up.shape[2]
    out_rows = w_out.shape[1]
    down_rows = w_down.shape[1]

    g_in, g_out, g_up, g_down, g_meta, g_scw, g_fcw = _peer_exchange(
        "gather_weights",
        [w_in[0].astype(BF16), w_out[0].astype(BF16), w_up[0].astype(BF16), w_down[0].astype(BF16),
         meta_tokens, ssd_conv_w[0], ffn_conv_w[0]], gather=True)
    w_in_full = g_in.transpose(1, 0, 2).reshape(D_MODEL, N_DEV * in_cols)
    off = [0, SSD_INNER, SSD_INNER + XBC, SSD_INNER + XBC + HEADS]
    w_z = w_in_full[:, off[0]:off[1]]
    w_xbc = w_in_full[:, off[1]:off[2]]
    w_dt = _pad_lanes(w_in_full[:, off[2]:off[3]])
    w_q = w_in_full[:, off[3]:off[3] + SSD_INNER]
    w_k = w_in_full[:, off[3] + SSD_INNER:off[3] + 2 * SSD_INNER]
    w_v = w_in_full[:, off[3] + 2 * SSD_INNER:off[3] + 3 * SSD_INNER]
    w_out_full = g_out.reshape(N_DEV * out_rows, D_MODEL)
    wo_ssd, wo_sb = w_out_full[:SSD_INNER], w_out_full[SSD_INNER:]
    w_up_full = g_up.transpose(1, 0, 2).reshape(D_MODEL, N_DEV * up_cols)
    w_gate, w_lin = w_up_full[:, :D_FF], w_up_full[:, D_FF:]
    w_down_full = g_down.reshape(N_DEV * down_rows, D_MODEL)
    meta_full = g_meta.transpose(1, 0, 2).reshape(N_META, D_MODEL)
    scw_full = g_scw.transpose(1, 0, 2).reshape(SSD_CONV, XBC)
    fcw_full = g_fcw.transpose(1, 0, 2).reshape(FFN_CONV, D_FF)

    dt_bias_p, a_log_p, d_p = _pad_lanes(ssd_dt_bias), _pad_lanes(ssd_a_log), _pad_lanes(ssd_d)

    h0 = jnp.concatenate([jnp.zeros((PAD, D_MODEL), F32), meta_full, x[0]], axis=0)
    target = jnp.concatenate([jnp.zeros((BLK, D_MODEL), F32), loss_target[0]], axis=0)
    xn1 = _rms_fwd("rms_pre_mix", h0, mix_pre_g)
    z = _mm("proj_z", [(xn1, w_z)], "nn", F32)
    xbc_raw = _mm("proj_xbc", [(xn1, w_xbc)], "nn", F32)
    dtr = _mm("proj_dt", [(xn1, w_dt)], "nn", F32)
    q = _mm("proj_q", [(xn1, w_q * SB_SCALE)], "nn", BF16)
    k = _mm("proj_k", [(xn1, w_k)], "nn", BF16)
    v = _mm("proj_v", [(xn1, w_v)], "nn", BF16)
    xbc_act = _ssd_conv_fwd(xbc_raw, scw_full, ssd_conv_b)
    ypre, y_ssd, states = _ssd_fwd(xbc_act, dtr, z, dt_bias_p, a_log_p, d_p, ssd_norm_g)
    o, keep_total = _attn_fwd(q, k, v)
    y_sb = _rms_fwd("rms_sb", o, sb_norm_g)
    mix = _mm("mix_out", [(y_ssd, wo_ssd), (y_sb, wo_sb)], "nn", F32)
    h1, xn2 = _mix_post(mix, h0, mix_post_g, ffn_pre_g)
    g_raw = _mm("ffn_gate", [(xn2, w_gate)], "nn", F32)
    u = _mm("ffn_lin", [(xn2, w_lin)], "nn", F32)
    act = _ffn_act(g_raw, u, fcw_full, ffn_conv_b)
    f = _mm("ffn_down", [(act, w_down_full)], "nn", F32)
    dh2, df, loss_row, dg_ffn_post = _loss_post(f, h1, target, ffn_post_g)
    loss = lax.psum(loss_row[0, 0], ("x", "y", "c"))

    dact = _mm("d_act", [(df, w_down_full)], "nt", F32)
    dw_down = _mm("dw_down", [(act, df)], "tn", F32)
    dg_conv, du = _ffn_bwd_act(dact, u, g_raw, fcw_full, ffn_conv_b)
    dg_raw, dfcw0, dfcw1, dfcw2, dfcb = _conv_bwd("ffn_conv_bwd", dg_conv, g_raw, fcw_full, FFN_CONV)
    dxn2 = _mm("d_xn2", [(dg_raw, w_gate), (du, w_lin)], "nt", F32)
    dw_gate = _mm("dw_gate", [(xn2, dg_raw)], "tn", F32)
    dw_lin = _mm("dw_lin", [(xn2, du)], "tn", F32)
    dh1, dmix, dg_ffn_pre, dg_mix_post = _mid_bwd(dxn2, h1, dh2, mix, ffn_pre_g, mix_post_g)

    dy_ssd = _mm("d_yssd", [(dmix, wo_ssd)], "nt", F32)
    dy_sb = _mm("d_ysb", [(dmix, wo_sb)], "nt", F32)
    dwo_ssd = _mm("dw_out_ssd", [(y_ssd, dmix)], "tn", F32)
    dwo_sb = _mm("dw_out_sb", [(y_sb, dmix)], "tn", F32)
    do, dg_sb = _norm_bwd("sb_norm_bwd", dy_sb, o, sb_norm_g)
    dq, dk, dv = _attn_bwd(q, k, v, keep_total, do)
    dz, dxbc_act, ddtr, dg_ssd_norm, dd_skip, da_log, ddt_bias = _ssd_bwd(
        dy_ssd, ypre, z, xbc_act, dtr, states, dt_bias_p, a_log_p, d_p, ssd_norm_g)
    dconv = _ssd_conv_bwd_act(dxbc_act, xbc_raw, scw_full, ssd_conv_b)
    dxbc_raw, dscw0, dscw1, dscw2, dscw3, dscb = _conv_bwd("ssd_conv_bwd", dconv, xbc_raw, scw_full, SSD_CONV)
    segs = [(dz, w_z), (dxbc_raw, w_xbc), (ddtr, w_dt), (dq, w_q), (dk, w_k), (dv, w_v)]
    dxn1 = _mm("d_xn1", segs, "nt", F32)
    dw_segs = [_mm("dw_in_%d" % s, [(xn1, d)], "tn", F32) for s, (d, _) in enumerate(segs)]
    dw_segs[2] = dw_segs[2][:, :HEADS]
    dw_in = jnp.concatenate(dw_segs, axis=1)
    dh0, dg_mix_pre = _first_bwd(dxn1, h0, dh1, mix_pre_g)
    grad_x = dh0[BLK:][None]

    slab_in = dw_in.reshape(D_MODEL, N_DEV, in_cols).transpose(1, 0, 2)
    slab_out = jnp.concatenate([dwo_ssd, dwo_sb], axis=0).reshape(N_DEV, out_rows, D_MODEL)
    half = N_DEV // 2
    slab_up = jnp.concatenate([dw_gate.reshape(D_MODEL, half, up_cols).transpose(1, 0, 2),
                               dw_lin.reshape(D_MODEL, half, up_cols).transpose(1, 0, 2)], axis=0)
    slab_down = dw_down.reshape(N_DEV, down_rows, D_MODEL)
    l_in, l_out, l_up, l_down = _peer_exchange("scatter_grads", [slab_in, slab_out, slab_up, slab_down], gather=False)

    small = [dg_mix_pre, dscb, ddt_bias, da_log, dd_skip, dg_ssd_norm, dg_sb, dg_mix_post, dg_ffn_pre, dfcb,
             dg_ffn_post, dh0[PAD:BLK].reshape(1, -1), dscw0, dscw1, dscw2, dscw3, dfcw0, dfcw1, dfcw2]
    sizes = [a.shape[1] for a in small]
    total = sum(sizes)
    rows_packed = -(-total // (LANES * HALO)) * HALO
    packed = jnp.pad(jnp.concatenate(small, axis=1), ((0, 0), (0, rows_packed * LANES - total)))
    (gathered,) = _peer_exchange("gather_small_grads", [packed.reshape(rows_packed, LANES)], gather=True)
    summed = _sum_slots("sum_small_grads", gathered).reshape(1, rows_packed * LANES)
    pieces, at = [], 0
    for s in sizes:
        pieces.append(summed[:, at:at + s])
        at += s
    (g_mix_pre, g_scb, g_dtb, g_alog, g_dskip, g_ssd_norm, g_sb, g_mix_post, g_ffn_pre, g_fcb, g_ffn_post,
     g_meta_flat, gs0, gs1, gs2, gs3, gf0, gf1, gf2) = pieces
    g_dtb, g_alog, g_dskip = g_dtb[:, :HEADS], g_alog[:, :HEADS], g_dskip[:, :HEADS]
    g_meta_full = g_meta_flat.reshape(N_META, D_MODEL)
    g_scw_full = jnp.concatenate([gs0, gs1, gs2, gs3], axis=0)
    g_fcw_full = jnp.concatenate([gf0, gf1, gf2], axis=0)
    meta_cols, scw_cols, fcw_cols = meta_tokens.shape[1], ssd_conv_w.shape[2], ffn_conv_w.shape[2]
    g_meta_mine = lax.dynamic_slice(g_meta_full, (0, me * meta_cols), (N_META, meta_cols))
    g_scw_mine = lax.dynamic_slice(g_scw_full, (0, me * scw_cols), (SSD_CONV, scw_cols))
    g_fcw_mine = lax.dynamic_slice(g_fcw_full, (0, me * fcw_cols), (FFN_CONV, fcw_cols))

    def lead(a):
        return a[None]

    upd = [
        _adamw("adamw_meta", meta_tokens, lead(g_meta_mine), m_meta_tokens, v_meta_tokens),
        _adamw("adamw_mix_pre_g", mix_pre_g, lead(g_mix_pre), m_mix_pre_g, v_mix_pre_g),
        [lead(a) for a in _adamw("adamw_w_in", w_in[0], l_in, m_w_in[0], v_w_in[0])],
        [lead(a) for a in _adamw("adamw_ssd_conv_w", ssd_conv_w[0], lead(g_scw_mine), m_ssd_conv_w[0], v_ssd_conv_w[0])],
        _adamw("adamw_ssd_conv_b", ssd_conv_b, lead(g_scb), m_ssd_conv_b, v_ssd_conv_b),
        _adamw("adamw_ssd_dt_bias", ssd_dt_bias, lead(g_dtb), m_ssd_dt_bias, v_ssd_dt_bias),
        _adamw("adamw_ssd_a_log", ssd_a_log, lead(g_alog), m_ssd_a_log, v_ssd_a_log),
        _adamw("adamw_ssd_d", ssd_d, lead(g_dskip), m_ssd_d, v_ssd_d),
        _adamw("adamw_ssd_norm_g", ssd_norm_g, lead(g_ssd_norm), m_ssd_norm_g, v_ssd_norm_g),
        _adamw("adamw_sb_norm_g", sb_norm_g, lead(g_sb), m_sb_norm_g, v_sb_norm_g),
        [lead(a) for a in _adamw("adamw_w_out", w_out[0], l_out, m_w_out[0], v_w_out[0])],
        _adamw("adamw_mix_post_g", mix_post_g, lead(g_mix_post), m_mix_post_g, v_mix_post_g),
        _adamw("adamw_ffn_pre_g", ffn_pre_g, lead(g_ffn_pre), m_ffn_pre_g, v_ffn_pre_g),
        [lead(a) for a in _adamw("adamw_w_up", w_up[0], l_up, m_w_up[0], v_w_up[0])],
        [lead(a) for a in _adamw("adamw_ffn_conv_w", ffn_conv_w[0], lead(g_fcw_mine), m_ffn_conv_w[0], v_ffn_conv_w[0])],
        _adamw("adamw_ffn_conv_b", ffn_conv_b, lead(g_fcb), m_ffn_conv_b, v_ffn_conv_b),
        [lead(a) for a in _adamw("adamw_w_down", w_down[0], l_down, m_w_down[0], v_w_down[0])],
        _adamw("adamw_ffn_post_g", ffn_post_g, lead(g_ffn_post), m_ffn_post_g, v_ffn_post_g),
    ]
    grads = [u_[0] for u_ in upd]
    deltas = [u_[1] for u_ in upd]
    new_m = [u_[2] for u_ in upd]
    new_v = [u_[3] for u_ in upd]
    return (loss, grad_x, *grads, *deltas, *new_m, *new_v)
```

```python
import math

import jax
import jax.numpy as jnp
from jax import lax
from jax.experimental import pallas as pl
from jax.experimental.pallas import tpu as pltpu

F32 = jnp.float32
BF16 = jnp.bfloat16
HI = lax.Precision.HIGHEST

D_MODEL = 1024
N_META = 16
BLK = 128
PAD = BLK - N_META
HEADS = 16
HEAD_DIM = 64
SSD_GROUPS = 2
SSD_STATE = 128
HEADS_PER_GROUP = HEADS // SSD_GROUPS
SSD_INNER = HEADS * HEAD_DIM
SSD_CONV = 4
XBC = SSD_INNER + 2 * SSD_GROUPS * SSD_STATE
OFF_B = SSD_INNER
OFF_C = SSD_INNER + SSD_GROUPS * SSD_STATE
D_FF = 2816
FFN_CONV = 3
EPS = 1e-6
SB_SCALE = 1.0 / math.sqrt(HEAD_DIM)
N_DEV = 8
LANES = 128
HALO = 8

ADAM_LR = 0.001
ADAM_B1 = 0.9
ADAM_B2 = 0.999
ADAM_EPS = 1e-08
ADAM_WD = 0.01
ADAM_STEP = 10

VMEM_FLOOR = 32 << 20
VMEM_CEIL = 60 << 20
MM_BUDGET = 20 << 20
ROW_BUDGET = 6 << 20

NN = (((1,), (0,)), ((), ()))
NT = (((1,), (1,)), ((), ()))
TN = (((0,), (0,)), ((), ()))


def _params(tile_bytes, sem=None):
    limit = int(min(max(2 * tile_bytes + (8 << 20), VMEM_FLOOR), VMEM_CEIL))
    return pltpu.CompilerParams(vmem_limit_bytes=limit, dimension_semantics=sem)


def _nbytes(shape, dtype):
    n = 1
    for s in shape:
        n *= s
    return n * jnp.dtype(dtype).itemsize


def _dot(a, b, dims=NN, precision=None):
    return lax.dot_general(a, b, dims, precision=precision, preferred_element_type=F32)


def _softplus(x):
    return jnp.maximum(x, 0.0) + jnp.log1p(jnp.exp(-jnp.abs(x)))


def _rms(x, g):
    r = lax.rsqrt(jnp.mean(x * x, axis=-1, keepdims=True) + EPS)
    return x * r * g


def _rms_bwd(dy, x, g):
    r = lax.rsqrt(jnp.mean(x * x, axis=-1, keepdims=True) + EPS)
    xh = x * r
    u = dy * g
    dx = r * (u - xh * jnp.mean(xh * u, axis=-1, keepdims=True))
    return dx, jnp.sum(dy * xh, axis=0, keepdims=True)


def _gelu(x):
    c = math.sqrt(2.0 / math.pi)
    return 0.5 * x * (1.0 + jnp.tanh(c * (x + 0.044715 * x * x * x)))


def _gelu_grad(x):
    c = math.sqrt(2.0 / math.pi)
    t = jnp.tanh(c * (x + 0.044715 * x * x * x))
    return 0.5 * (1.0 + t) + 0.5 * x * (1.0 - t * t) * c * (1.0 + 3.0 * 0.044715 * x * x)


def _row_tile(rows, bytes_per_row):
    big = 384
    return big if rows % big == 0 and big * bytes_per_row <= ROW_BUDGET else BLK


def _mm(name, pairs, mode, out_dtype):
    a0, b0 = pairs[0]
    if mode == "tn":
        m, n = a0.shape[1], b0.shape[1]
    elif mode == "nt":
        m, n = a0.shape[0], b0.shape[0]
    else:
        m, n = a0.shape[0], b0.shape[1]
    dims = {"nn": NN, "nt": NT, "tn": TN}[mode]

    def tile_bytes(tm, tn):
        tot = tm * tn * jnp.dtype(out_dtype).itemsize
        for a, b in pairs:
            k = a.shape[0] if mode == "tn" else a.shape[1]
            tot += tm * k * a.dtype.itemsize + tn * k * b.dtype.itemsize
        return tot

    cands_m = [t for t in (1408, 1024, 512, 384, 256, 128) if m % t == 0] or [m]
    cands_n = [t for t in (1408, 1024, 768, 512, 256, 128) if n % t == 0] or [n]
    best = None
    for tm in cands_m:
        for tn in cands_n:
            if tile_bytes(tm, tn) <= MM_BUDGET and (best is None or tm * tn > best[0] * best[1]):
                best = (tm, tn)
    tm, tn = best if best is not None else (cands_m[-1], cands_n[-1])
    npairs = len(pairs)

    def body(*refs):
        o_ref = refs[2 * npairs]
        acc = None
        for p in range(npairs):
            part = _dot(refs[2 * p][...], refs[2 * p + 1][...], dims)
            acc = part if acc is None else acc + part
        o_ref[...] = acc.astype(o_ref.dtype)

    in_specs, args = [], []
    for a, b in pairs:
        if mode == "tn":
            k = a.shape[0]
            in_specs += [pl.BlockSpec((k, tm), lambda i, j: (0, i)), pl.BlockSpec((k, tn), lambda i, j: (0, j))]
        elif mode == "nt":
            k = a.shape[1]
            in_specs += [pl.BlockSpec((tm, k), lambda i, j: (i, 0)), pl.BlockSpec((tn, k), lambda i, j: (j, 0))]
        else:
            k = a.shape[1]
            in_specs += [pl.BlockSpec((tm, k), lambda i, j: (i, 0)), pl.BlockSpec((k, tn), lambda i, j: (0, j))]
        args += [a, b]
    return pl.pallas_call(
        body, name=name, grid=(m // tm, n // tn), in_specs=in_specs,
        out_specs=pl.BlockSpec((tm, tn), lambda i, j: (i, j)),
        out_shape=jax.ShapeDtypeStruct((m, n), out_dtype),
        compiler_params=_params(tile_bytes(tm, tn), ("parallel", "parallel")),
    )(*args)


def _rowcall(name, fn, rows=(), prevs=(), nexts=(), pars=(), out_rows=(), out_accs=()):
    rows, prevs, nexts, pars = list(rows), list(prevs), list(nexts), list(pars)
    n_rows = (rows + prevs + nexts)[0].shape[0]
    per_row = sum(a.shape[1] * a.dtype.itemsize for a in rows + prevs + nexts)
    per_row += sum(c * jnp.dtype(dt).itemsize for c, dt in out_rows) + sum(a.shape[1] * 4 for a in prevs + nexts)
    tm = _row_tile(n_rows, per_row)
    nt = n_rows // tm
    hb = tm // HALO
    nr, npv, nnx, npar, nor, noa = len(rows), len(prevs), len(nexts), len(pars), len(out_rows), len(out_accs)

    def body(*refs):
        i = pl.program_id(0)
        k = 0
        row_refs = refs[k:k + nr]; k += nr
        pc = refs[k:k + npv]; k += npv
        ph = refs[k:k + npv]; k += npv
        nc = refs[k:k + nnx]; k += nnx
        nh = refs[k:k + nnx]; k += nnx
        par_refs = refs[k:k + npar]; k += npar
        orow = refs[k:k + nor]; k += nor
        oacc = refs[k:k + noa]; k += noa
        pscr = refs[k:k + npv]; k += npv
        nscr = refs[k:k + nnx]
        for c_, h_, s_ in zip(pc, ph, pscr):
            s_[0:HALO, :] = h_[...] * (i > 0).astype(F32)
            s_[HALO:HALO + tm, :] = c_[...]
        for c_, h_, s_ in zip(nc, nh, nscr):
            s_[0:tm, :] = c_[...]
            s_[tm:tm + HALO, :] = h_[...] * (i < nt - 1).astype(F32)
        prev_fns = [(lambda s, s_=s_: s_[pl.ds(HALO - s, tm), :]) for s_ in pscr]
        next_fns = [(lambda s, s_=s_: s_[pl.ds(s, tm), :]) for s_ in nscr]
        row_vals, acc_vals = fn(i, tm, row_refs, prev_fns, next_fns, par_refs)
        for r_, v in zip(orow, row_vals):
            r_[...] = v.astype(r_.dtype)
        if noa:
            @pl.when(i == 0)
            def _():
                for r_ in oacc:
                    r_[...] = jnp.zeros_like(r_)
            for r_, v in zip(oacc, acc_vals):
                r_[...] += v

    def row_spec(a):
        return pl.BlockSpec((tm, a.shape[1]), lambda i: (i, 0))

    def whole(shape):
        return pl.BlockSpec(shape, lambda i: (0, 0))

    in_specs = [row_spec(a) for a in rows]
    in_specs += [row_spec(a) for a in prevs]
    in_specs += [pl.BlockSpec((HALO, a.shape[1]), lambda i: (jnp.maximum(i * hb - 1, 0), 0)) for a in prevs]
    in_specs += [row_spec(a) for a in nexts]
    in_specs += [pl.BlockSpec((HALO, a.shape[1]), lambda i: (jnp.minimum((i + 1) * hb, n_rows // HALO - 1), 0)) for a in nexts]
    in_specs += [whole(p.shape) for p in pars]
    out_specs = [pl.BlockSpec((tm, c), lambda i: (i, 0)) for c, _ in out_rows] + [whole(s) for s in out_accs]
    out_shape = [jax.ShapeDtypeStruct((n_rows, c), dt) for c, dt in out_rows]
    out_shape += [jax.ShapeDtypeStruct(s, F32) for s in out_accs]
    scratch = [pltpu.VMEM((tm + HALO, a.shape[1]), F32) for a in prevs + nexts]
    tile = tm * per_row
    res = pl.pallas_call(
        body, name=name, grid=(nt,), in_specs=in_specs, out_specs=out_specs, out_shape=out_shape,
        scratch_shapes=scratch, compiler_params=_params(2 * tile, ("arbitrary",)),
    )(*rows, *prevs, *prevs, *nexts, *nexts, *pars)
    return res


def _row_ids(i, tm):
    return i * tm + lax.broadcasted_iota(jnp.int32, (tm, 1), 0)


def _conv_taps(prev_fn, w_ref, b_ref, taps):
    acc = b_ref[...]
    for k in range(taps):
        acc = acc + prev_fn(taps - 1 - k) * w_ref[k:k + 1, :]
    return acc


def _rms_fwd(name, h, g):
    def fn(i, tm, rows, prevs, nexts, pars):
        return [_rms(rows[0][...], pars[0][...])], []
    return _rowcall(name, fn, rows=[h], pars=[g], out_rows=[(h.shape[1], BF16)])[0]


def _ssd_conv_fwd(xbc_raw, w, b):
    def fn(i, tm, rows, prevs, nexts, pars):
        c = _conv_taps(prevs[0], pars[0], pars[1], SSD_CONV)
        y = c * jax.nn.sigmoid(c)
        return [jnp.where(_row_ids(i, tm) >= PAD, y, 0.0)], []
    return _rowcall("ssd_conv_fwd", fn, prevs=[xbc_raw], pars=[w, b], out_rows=[(XBC, F32)])[0]


def _mix_post(mix, h0, g_post, g_pre):
    def fn(i, tm, rows, prevs, nexts, pars):
        h1 = rows[1][...] + _rms(rows[0][...], pars[0][...])
        return [h1, _rms(h1, pars[1][...])], []
    return _rowcall("mix_post", fn, rows=[mix, h0], pars=[g_post, g_pre],
                    out_rows=[(D_MODEL, F32), (D_MODEL, BF16)])


def _ffn_act(g_raw, u, w, b):
    def fn(i, tm, rows, prevs, nexts, pars):
        g = _conv_taps(prevs[0], pars[0], pars[1], FFN_CONV)
        return [_gelu(g) * rows[0][...]], []
    return _rowcall("ffn_act", fn, rows=[u], prevs=[g_raw], pars=[w, b], out_rows=[(D_FF, BF16)])[0]


def _loss_post(f, h1, target, g_post):
    def fn(i, tm, rows, prevs, nexts, pars):
        fv, g = rows[0][...], pars[0][...]
        h2 = rows[1][...] + _rms(fv, g)
        real = _row_ids(i, tm) >= BLK
        diff = jnp.where(real, h2 - rows[2][...], 0.0)
        loss = 0.5 * jnp.sum(jnp.mean(diff * diff, axis=-1, keepdims=True))
        dh2 = diff * (1.0 / D_MODEL)
        df, dg = _rms_bwd(dh2, fv, g)
        return [dh2, df], [jnp.zeros((1, LANES), F32) + loss, dg]
    return _rowcall("loss_post", fn, rows=[f, h1, target], pars=[g_post],
                    out_rows=[(D_MODEL, F32), (D_MODEL, BF16)], out_accs=[(1, LANES), (1, D_MODEL)])


def _ffn_bwd_act(dact, u, g_raw, w, b):
    def fn(i, tm, rows, prevs, nexts, pars):
        g = _conv_taps(prevs[0], pars[0], pars[1], FFN_CONV)
        da = rows[0][...]
        return [da * rows[1][...] * _gelu_grad(g), da * _gelu(g)], []
    return _rowcall("ffn_bwd_act", fn, rows=[dact, u], prevs=[g_raw], pars=[w, b],
                    out_rows=[(D_FF, F32), (D_FF, BF16)])


def _conv_bwd(name, dy, x, w, taps):
    width = x.shape[1]

    def fn(i, tm, rows, prevs, nexts, pars):
        dy0 = nexts[0](0)
        dx = None
        for k in range(taps):
            term = nexts[0](taps - 1 - k) * pars[0][k:k + 1, :]
            dx = term if dx is None else dx + term
        dws = [jnp.sum(dy0 * prevs[0](taps - 1 - k), axis=0, keepdims=True) for k in range(taps)]
        return [dx], dws + [jnp.sum(dy0, axis=0, keepdims=True)]
    return _rowcall(name, fn, prevs=[x], nexts=[dy], pars=[w], out_rows=[(width, BF16)],
                    out_accs=[(1, width)] * (taps + 1))


def _mid_bwd(dxn2, h1, dh2, mix, g_pre, g_post):
    def fn(i, tm, rows, prevs, nexts, pars):
        d1, dg_pre = _rms_bwd(rows[0][...], rows[1][...], pars[0][...])
        dh1 = rows[2][...] + d1
        dmix, dg_post = _rms_bwd(dh1, rows[3][...], pars[1][...])
        return [dh1, dmix], [dg_pre, dg_post]
    return _rowcall("mid_bwd", fn, rows=[dxn2, h1, dh2, mix], pars=[g_pre, g_post],
                    out_rows=[(D_MODEL, F32), (D_MODEL, BF16)], out_accs=[(1, D_MODEL), (1, D_MODEL)])


def _norm_bwd(name, dy, x, g):
    def fn(i, tm, rows, prevs, nexts, pars):
        dx, dg = _rms_bwd(rows[0][...], rows[1][...], pars[0][...])
        return [dx], [dg]
    return _rowcall(name, fn, rows=[dy, x], pars=[g], out_rows=[(x.shape[1], BF16)], out_accs=[(1, x.shape[1])])


def _ssd_conv_bwd_act(dact, xbc_raw, w, b):
    def fn(i, tm, rows, prevs, nexts, pars):
        c = _conv_taps(prevs[0], pars[0], pars[1], SSD_CONV)
        s = jax.nn.sigmoid(c)
        dc = rows[0][...] * s * (1.0 + c * (1.0 - s))
        return [jnp.where(_row_ids(i, tm) >= PAD, dc, 0.0)], []
    return _rowcall("ssd_conv_bwd_act", fn, rows=[dact], prevs=[xbc_raw], pars=[w, b], out_rows=[(XBC, F32)])[0]


def _first_bwd(dxn1, h0, dh1, g):
    def fn(i, tm, rows, prevs, nexts, pars):
        d0, dg = _rms_bwd(rows[0][...], rows[1][...], pars[0][...])
        return [rows[2][...] + d0], [dg]
    return _rowcall("first_bwd", fn, rows=[dxn1, h0, dh1], pars=[g], out_rows=[(D_MODEL, F32)],
                    out_accs=[(1, D_MODEL)])


def _ssd_chunk_terms(c, dtr_ref, bias_ref, alog_ref):
    ri = lax.broadcasted_iota(jnp.int32, (BLK, BLK), 0)
    ci = lax.broadcasted_iota(jnp.int32, (BLK, BLK), 1)
    causal = ri >= ci
    tril = causal.astype(F32)
    triu = (ri <= ci).astype(F32)
    rowmask = ((c * BLK + lax.broadcasted_iota(jnp.int32, (BLK, 1), 0)) >= PAD).astype(F32)
    dt = _softplus(dtr_ref[...] + bias_ref[...]) * rowmask
    a_neg = -jnp.exp(alog_ref[...])
    a = dt * a_neg
    cs = _dot(tril, a, NN, HI)
    cs_t = _dot(a, triu, TN, HI)
    return causal, triu, rowmask, dt, a_neg, cs, cs_t


def _decay_matrix(causal, cs_h, cs_t_h):
    return jnp.where(causal, jnp.exp(jnp.where(causal, cs_h - cs_t_h, 0.0)), 0.0)


def _ssd_fwd(xbc, dtr, z, dt_bias, a_log, d_skip, norm_g):
    n_rows = xbc.shape[0]
    nb = n_rows // BLK

    def body(xbc_ref, dtr_ref, z_ref, bias_ref, alog_ref, d_ref, g_ref, ypre_ref, yssd_ref, st_ref, state):
        c = pl.program_id(0)

        @pl.when(c == 0)
        def _():
            state[...] = jnp.zeros_like(state)

        st_ref[...] = state[...]
        causal, _, _, dt, _, cs, cs_t = _ssd_chunk_terms(c, dtr_ref, bias_ref, alog_ref)
        cs_last = cs[BLK - 1:BLK, :]
        for g in range(SSD_GROUPS):
            b_g = xbc_ref[:, OFF_B + g * SSD_STATE:OFF_B + (g + 1) * SSD_STATE]
            c_g = xbc_ref[:, OFF_C + g * SSD_STATE:OFF_C + (g + 1) * SSD_STATE]
            c_b = c_g.astype(BF16)
            cb = _dot(c_b, b_g.astype(BF16), NT)
            for j in range(HEADS_PER_GROUP):
                h = g * HEADS_PER_GROUP + j
                sl = slice(h * HEAD_DIM, (h + 1) * HEAD_DIM)
                xs = xbc_ref[:, sl]
                x_b = (xs * dt[:, h:h + 1]).astype(BF16)
                cs_h = cs[:, h:h + 1]
                cl = cs_last[:, h:h + 1]
                decay = _decay_matrix(causal, cs_h, cs_t[h:h + 1, :])
                s_h = state[:, sl]
                y = _dot((cb * decay).astype(BF16), x_b)
                y = y + jnp.exp(cs_h) * _dot(c_b, s_h.astype(BF16))
                to_end = jnp.exp(cl - cs_h)
                state[:, sl] = jnp.exp(cl) * s_h + _dot((b_g * to_end).astype(BF16), x_b, TN)
                ypre_ref[:, sl] = y + d_ref[:, h:h + 1] * xs
        zz = z_ref[...]
        yg = ypre_ref[...] * (zz * jax.nn.sigmoid(zz))
        yssd_ref[...] = _rms(yg, g_ref[...]).astype(yssd_ref.dtype)

    blk = lambda w: pl.BlockSpec((BLK, w), lambda c: (c, 0))
    par = lambda a: pl.BlockSpec(a.shape, lambda c: (0, 0))
    return pl.pallas_call(
        body, name="ssd_fwd", grid=(nb,),
        in_specs=[blk(XBC), blk(LANES), blk(SSD_INNER), par(dt_bias), par(a_log), par(d_skip), par(norm_g)],
        out_specs=[blk(SSD_INNER), blk(SSD_INNER), pl.BlockSpec((None, SSD_STATE, SSD_INNER), lambda c: (c, 0, 0))],
        out_shape=[jax.ShapeDtypeStruct((n_rows, SSD_INNER), F32), jax.ShapeDtypeStruct((n_rows, SSD_INNER), BF16),
                   jax.ShapeDtypeStruct((nb, SSD_STATE, SSD_INNER), F32)],
        scratch_shapes=[pltpu.VMEM((SSD_STATE, SSD_INNER), F32)],
        compiler_params=_params(8 << 20, ("arbitrary",)),
    )(xbc, dtr, z, dt_bias, a_log, d_skip, norm_g)


def _ssd_bwd(dy, ypre, z, xbc, dtr, states, dt_bias, a_log, d_skip, norm_g):
    n_rows = xbc.shape[0]
    nb = n_rows // BLK

    def body(dy_ref, ypre_ref, z_ref, xbc_ref, dtr_ref, st_ref, bias_ref, alog_ref, d_ref, g_ref,
             dz_ref, dxbc_ref, ddtr_ref, dgn_ref, dd_ref, dal_ref, ddtb_ref, dstate, dyp):
        step = pl.program_id(0)
        c = nb - 1 - step

        @pl.when(step == 0)
        def _():
            dstate[...] = jnp.zeros_like(dstate)
            for r_ in (dgn_ref, dd_ref, dal_ref, ddtb_ref):
                r_[...] = jnp.zeros_like(r_)

        yp, zz = ypre_ref[...], z_ref[...]
        sz = jax.nn.sigmoid(zz)
        silu = zz * sz
        dyg, dgn = _rms_bwd(dy_ref[...], yp * silu, g_ref[...])
        dgn_ref[...] += dgn
        dz_ref[...] = (dyg * yp * (sz * (1.0 + zz * (1.0 - sz)))).astype(dz_ref.dtype)
        dyp[...] = dyg * silu

        causal, triu, rowmask, dt, a_neg, cs, cs_t = _ssd_chunk_terms(c, dtr_ref, bias_ref, alog_ref)
        cs_last = cs[BLK - 1:BLK, :]
        lane = lax.broadcasted_iota(jnp.int32, (1, LANES), 1)
        sub = lax.broadcasted_iota(jnp.int32, (BLK, 1), 0)
        last_row = (sub == BLK - 1).astype(F32)
        dcs_col = jnp.zeros((BLK, LANES), F32)
        dcs_row = jnp.zeros((LANES, BLK), F32)
        ddt_x = jnp.zeros((BLK, LANES), F32)
        dd_row = jnp.zeros((1, LANES), F32)
        for g in range(SSD_GROUPS):
            b_g = xbc_ref[:, OFF_B + g * SSD_STATE:OFF_B + (g + 1) * SSD_STATE]
            c_g = xbc_ref[:, OFF_C + g * SSD_STATE:OFF_C + (g + 1) * SSD_STATE]
            b_b, c_b = b_g.astype(BF16), c_g.astype(BF16)
            cb = _dot(c_b, b_b, NT)
            db_g = jnp.zeros((BLK, SSD_STATE), F32)
            dc_g = jnp.zeros((BLK, SSD_STATE), F32)
            for j in range(HEADS_PER_GROUP):
                h = g * HEADS_PER_GROUP + j
                sl = slice(h * HEAD_DIM, (h + 1) * HEAD_DIM)
                d_y = dyp[:, sl]
                xs = xbc_ref[:, sl]
                dt_h = dt[:, h:h + 1]
                x_dt = xs * dt_h
                x_b, dy_b = x_dt.astype(BF16), d_y.astype(BF16)
                cs_h = cs[:, h:h + 1]
                cl = cs_last[:, h:h + 1]
                decay = _decay_matrix(causal, cs_h, cs_t[h:h + 1, :])
                s_h, ds_h = st_ref[:, sl], dstate[:, sl]
                s_b, ds_b = s_h.astype(BF16), ds_h.astype(BF16)
                to_end = jnp.exp(cl - cs_h)
                from_start = jnp.exp(cs_h)
                chunk_decay = jnp.exp(cl)

                d_x = _dot((cb * decay).astype(BF16), dy_b, TN) + to_end * _dot(b_b, ds_b)
                gl = _dot(dy_b, x_b, NT) * decay
                gl_b = gl.astype(BF16)
                c_s = _dot(c_b, s_b)
                x_ds = _dot(x_b, ds_b, NT)
                dc_g = dc_g + _dot(gl_b, b_b) + from_start * _dot(dy_b, s_b, NT)
                db_g = db_g + _dot(gl_b, c_b, TN) + to_end * x_ds
                w_m = gl * cb
                d_end = jnp.sum(x_ds * b_g, axis=1, keepdims=True) * to_end
                at_last = jnp.sum(d_end) + chunk_decay * jnp.sum(ds_h * s_h)
                col = (jnp.sum(w_m, axis=1, keepdims=True)
                       + from_start * jnp.sum(d_y * c_s, axis=1, keepdims=True)
                       - d_end + last_row * at_last)
                onehot = (lane == h).astype(F32)
                dcs_col = dcs_col + col * onehot
                dcs_row = dcs_row + (sub == h).astype(F32) * jnp.sum(w_m, axis=0, keepdims=True)
                ddt_x = ddt_x + jnp.sum(d_x * xs, axis=1, keepdims=True) * onehot
                dd_row = dd_row + jnp.sum(d_y * xs) * onehot
                dxbc_ref[:, sl] = d_x * dt_h + d_ref[:, h:h + 1] * d_y
                dstate[:, sl] = chunk_decay * ds_h + _dot((c_g * from_start).astype(BF16), dy_b, TN)
            dxbc_ref[:, OFF_B + g * SSD_STATE:OFF_B + (g + 1) * SSD_STATE] = db_g
            dxbc_ref[:, OFF_C + g * SSD_STATE:OFF_C + (g + 1) * SSD_STATE] = dc_g
        da = _dot(triu, dcs_col, NN, HI) - _dot(triu, dcs_row, NT, HI)
        ddt = (da * a_neg + ddt_x) * rowmask
        ddtr = ddt * jax.nn.sigmoid(dtr_ref[...] + bias_ref[...]) * (lane < HEADS).astype(F32)
        ddtr_ref[...] = ddtr.astype(ddtr_ref.dtype)
        ddtb_ref[...] += jnp.sum(ddtr, axis=0, keepdims=True)
        dal_ref[...] += jnp.sum(da * dt, axis=0, keepdims=True) * a_neg
        dd_ref[...] += dd_row

    blk = lambda w: pl.BlockSpec((BLK, w), lambda s: (nb - 1 - s, 0))
    par = lambda a: pl.BlockSpec(a.shape, lambda s: (0, 0))
    acc = lambda w: pl.BlockSpec((1, w), lambda s: (0, 0))
    return pl.pallas_call(
        body, name="ssd_bwd", grid=(nb,),
        in_specs=[blk(SSD_INNER), blk(SSD_INNER), blk(SSD_INNER), blk(XBC), blk(LANES),
                  pl.BlockSpec((None, SSD_STATE, SSD_INNER), lambda s: (nb - 1 - s, 0, 0)),
                  par(dt_bias), par(a_log), par(d_skip), par(norm_g)],
        out_specs=[blk(SSD_INNER), blk(XBC), blk(LANES), acc(SSD_INNER), acc(LANES), acc(LANES), acc(LANES)],
        out_shape=[jax.ShapeDtypeStruct((n_rows, SSD_INNER), BF16), jax.ShapeDtypeStruct((n_rows, XBC), F32),
                   jax.ShapeDtypeStruct((n_rows, LANES), BF16), jax.ShapeDtypeStruct((1, SSD_INNER), F32),
                   jax.ShapeDtypeStruct((1, LANES), F32), jax.ShapeDtypeStruct((1, LANES), F32),
                   jax.ShapeDtypeStruct((1, LANES), F32)],
        scratch_shapes=[pltpu.VMEM((SSD_STATE, SSD_INNER), F32), pltpu.VMEM((BLK, SSD_INNER), F32)],
        compiler_params=_params(10 << 20, ("arbitrary",)),
    )(dy, ypre, z, xbc, dtr, states, dt_bias, a_log, d_skip, norm_g)


HEAD_GROUP = 4


def _sb_logits(zl, valid):
    log_keep = -(jnp.maximum(zl, 0.0) + jnp.log1p(jnp.exp(-jnp.abs(zl))))
    log_beta = log_keep + zl
    if valid is not None:
        log_keep = jnp.where(valid, log_keep, 0.0)
    return log_beta, log_keep


def _sums_dot(x, tri_and_ones):
    hi = x.astype(BF16)
    lo = (x - hi.astype(F32)).astype(BF16)
    both = _dot(jnp.concatenate([hi, lo], axis=1), tri_and_ones)
    return both[:, :BLK], both[:, BLK:]


def _tri_and_ones(tri):
    half = jnp.concatenate([tri, jnp.ones((BLK, BLK), F32)], axis=1)
    return jnp.concatenate([half, half], axis=0).astype(BF16)


def _tile_mask(i, j, ri, ci):
    key = j * BLK + ci
    return (key < i * BLK + ri) & (key >= PAD)


def _block_rows(j):
    return pl.ds(j * BLK if isinstance(j, int) else pl.multiple_of(j * BLK, BLK), BLK)


def _sweep(i, tile, leftwards):
    first, last = (i, 0) if leftwards else (0, i)
    tile(first, True)

    def mid(t, carry):
        tile(i - t if leftwards else t, False)
        return carry

    lax.fori_loop(1, i, mid, 0)

    @pl.when(i >= 1)
    def _():
        tile(last, True)


def _ride_along(ex, n_in, n_out, refs):
    k = 0
    parts = []
    for cnt in (n_in, ex.n, n_out, ex.n):
        parts.append(refs[k:k + cnt])
        k += cnt
    n_sems = len(ex.scratch)
    return (*parts, refs[k:len(refs) - n_sems], refs[len(refs) - n_sems:])


def _attn_fwd(q, k, v, ex, ex_arrays):
    n_rows, width = q.shape
    nb = n_rows // BLK
    gw = HEAD_GROUP * HEAD_DIM
    groups = width // gw

    def body(*refs):
        (q_ref, k_ref, v_ref), ex_in, (o_ref, tot_ref), ex_out, (run_ref,), sems = _ride_along(ex, 3, 2, refs)
        step = pl.program_id(0)

        @pl.when(step == 0)
        def _():
            ex.start(ex_in, ex_out, sems)

        ri = lax.broadcasted_iota(jnp.int32, (BLK, BLK), 0)
        ci = lax.broadcasted_iota(jnp.int32, (BLK, BLK), 1)
        sums = _tri_and_ones((ri > ci).astype(F32))

        def q_block(i, carry):
            r0 = pl.multiple_of(i * BLK, BLK)
            rows = pl.ds(r0, BLK)
            o_ref[rows, :] = jnp.zeros((BLK, gw), F32)
            run_ref[...] = jnp.zeros_like(run_ref)

            def tile(j, masked):
                cols = _block_rows(j)
                valid = _tile_mask(i, j, ri, ci) if masked else None
                q_i, k_j, v_j, o_i = q_ref[rows, :], k_ref[cols, :], v_ref[cols, :], o_ref[rows, :]
                heads = range(HEAD_GROUP)
                sls = [slice(h * HEAD_DIM, (h + 1) * HEAD_DIM) for h in heads]
                runs = [run_ref[h] for h in heads]
                zls = [_dot(q_i[:, sls[h]], k_j[:, sls[h]], NT) for h in heads]
                lgs = [_sb_logits(zls[h], valid) for h in heads]
                sms = [_sums_dot(lgs[h][1], sums) for h in heads]
                ws = [jnp.exp(lgs[h][0] + sms[h][0] + runs[h]) for h in heads]
                if masked:
                    ws = [jnp.where(valid, w, 0.0) for w in ws]
                outs = [_dot(ws[h].astype(BF16), v_j[:, sls[h]]) for h in heads]
                o_ref[rows, :] = o_i + jnp.concatenate(outs, axis=1)
                for h in heads:
                    run_ref[h] = runs[h] + sms[h][1]

            _sweep(i, tile, leftwards=True)
            for h in range(HEAD_GROUP):
                tot_ref[rows, h:h + 1] = run_ref[h][:, 0:1]
            return carry

        lax.fori_loop(0, nb, q_block, 0)

        @pl.when(step == groups - 1)
        def _():
            ex.wait(ex_in, ex_out, sems)

    spec = pl.BlockSpec((n_rows, gw), lambda g: (0, g))
    tot_spec = pl.BlockSpec((None, n_rows, HEAD_GROUP), lambda g: (g, 0, 0))
    res = pl.pallas_call(
        body, name="attn_fwd", grid=(groups,), in_specs=[spec, spec, spec] + ex.specs,
        out_specs=[spec, tot_spec] + ex.specs,
        out_shape=[jax.ShapeDtypeStruct((n_rows, width), F32),
                   jax.ShapeDtypeStruct((groups, n_rows, HEAD_GROUP), F32)] + ex.out_shape,
        scratch_shapes=[pltpu.VMEM((HEAD_GROUP, BLK, BLK), F32)] + ex.scratch,
        compiler_params=_params(n_rows * (3 * gw * 2 + gw * 4 + LANES * 4), ("arbitrary",)),
    )(q, k, v, *ex_arrays)
    return res[0], res[1], res[2:]


def _attn_bwd(q, k, v, keep_total, do, ex, ex_arrays):
    n_rows, width = q.shape
    nb = n_rows // BLK
    gw = HEAD_GROUP * HEAD_DIM
    groups = width // gw

    def body(*refs):
        ((q_ref, k_ref, v_ref, tot_ref, do_ref), ex_in, (dq_ref, dk_ref, dv_ref), ex_out,
         (dq_acc, dk_acc, dv_acc, tot_b, run_ref, rung_ref), sems) = _ride_along(ex, 5, 3, refs)
        step = pl.program_id(0)

        @pl.when(step == 0)
        def _():
            ex.start(ex_in, ex_out, sems)

        ri = lax.broadcasted_iota(jnp.int32, (BLK, BLK), 0)
        ci = lax.broadcasted_iota(jnp.int32, (BLK, BLK), 1)
        sums_keep = _tri_and_ones((ri <= ci).astype(F32))
        sums_g = _tri_and_ones((ri < ci).astype(F32))
        dk_acc[...] = jnp.zeros_like(dk_acc)
        dv_acc[...] = jnp.zeros_like(dv_acc)

        def q_block(i, carry):
            rows = _block_rows(i)
            dq_acc[...] = jnp.zeros_like(dq_acc)
            run_ref[...] = jnp.zeros_like(run_ref)
            rung_ref[...] = jnp.zeros_like(rung_ref)
            for h in range(HEAD_GROUP):
                tot_b[h] = jnp.broadcast_to(tot_ref[rows, h:h + 1], (BLK, BLK))

            def tile(j, masked):
                cols = _block_rows(j)
                valid = _tile_mask(i, j, ri, ci) if masked else None
                heads = range(HEAD_GROUP)
                sls = [slice(h * HEAD_DIM, (h + 1) * HEAD_DIM) for h in heads]
                q_i, do_i, k_j, v_j = q_ref[rows, :], do_ref[rows, :], k_ref[cols, :], v_ref[cols, :]
                zls = [_dot(q_i[:, sls[h]], k_j[:, sls[h]], NT) for h in heads]
                dws = [_dot(do_i[:, sls[h]], v_j[:, sls[h]], NT) for h in heads]
                lgs = [_sb_logits(zls[h], valid) for h in heads]
                sms = [_sums_dot(lgs[h][1], sums_keep) for h in heads]
                ws = [jnp.exp(lgs[h][0] + (tot_b[h] - run_ref[h] - sms[h][0])) for h in heads]
                if masked:
                    ws = [jnp.where(valid, w, 0.0) for w in ws]
                gs = [dws[h] * ws[h] for h in heads]
                gsm = [_sums_dot(gs[h], sums_g) for h in heads]
                w_bs = [w.astype(BF16) for w in ws]
                dvs = [_dot(w_bs[h], do_i[:, sls[h]], TN) for h in heads]
                dzs = []
                for h in heads:
                    beta = jnp.exp(lgs[h][0])
                    dz = gs[h] * (1.0 - beta) - beta * (gsm[h][0] + rung_ref[h])
                    dzs.append((jnp.where(valid, dz, 0.0) if masked else dz).astype(BF16))
                dqs = [_dot(dzs[h], k_j[:, sls[h]]) for h in heads]
                dks = [_dot(dzs[h], q_i[:, sls[h]], TN) for h in heads]
                dq_acc[...] += jnp.concatenate(dqs, axis=1)
                dk_acc[cols, :] += jnp.concatenate(dks, axis=1)
                dv_acc[cols, :] += jnp.concatenate(dvs, axis=1)
                for h in heads:
                    run_ref[h] += sms[h][1]
                    rung_ref[h] += gsm[h][1]

            _sweep(i, tile, leftwards=False)
            dq_ref[rows, :] = (dq_acc[...] * SB_SCALE).astype(dq_ref.dtype)
            return carry

        lax.fori_loop(0, nb, q_block, 0)
        dk_ref[...] = dk_acc[...].astype(dk_ref.dtype)
        dv_ref[...] = dv_acc[...].astype(dv_ref.dtype)

        @pl.when(step == groups - 1)
        def _():
            ex.wait(ex_in, ex_out, sems)

    spec = pl.BlockSpec((n_rows, gw), lambda g: (0, g))
    tot_spec = pl.BlockSpec((None, n_rows, HEAD_GROUP), lambda g: (g, 0, 0))
    out = jax.ShapeDtypeStruct((n_rows, width), BF16)
    tile_f32 = pltpu.VMEM((HEAD_GROUP, BLK, BLK), F32)
    res = pl.pallas_call(
        body, name="attn_bwd", grid=(groups,), in_specs=[spec, spec, spec, tot_spec, spec] + ex.specs,
        out_specs=[spec] * 3 + ex.specs, out_shape=[out] * 3 + ex.out_shape,
        scratch_shapes=[pltpu.VMEM((BLK, gw), F32), pltpu.VMEM((n_rows, gw), F32), pltpu.VMEM((n_rows, gw), F32),
                        tile_f32, tile_f32, tile_f32] + ex.scratch,
        compiler_params=_params(n_rows * (7 * gw * 2 + LANES * 4 + gw * 4), ("arbitrary",)),
    )(q, k, v, keep_total, do, *ex_arrays)
    return res[0], res[1], res[2], res[3:]


class _Exchange:
    def __init__(self, arrays, gather):
        self.n = len(arrays)
        self.gather = gather
        self.out_shape = [jax.ShapeDtypeStruct(((N_DEV,) + a.shape) if gather else a.shape, a.dtype) for a in arrays]
        self.scratch = [pltpu.SemaphoreType.DMA((self.n, N_DEV - 1)), pltpu.SemaphoreType.DMA((self.n, N_DEV - 1)),
                        pltpu.SemaphoreType.DMA((self.n,))]
        self.specs = [pl.BlockSpec(memory_space=pl.ANY)] * self.n

    def _copies(self, ins, outs, sems):
        send_sems, recv_sems, local_sems = sems
        x, y, c = lax.axis_index("x"), lax.axis_index("y"), lax.axis_index("c")
        me = 4 * x + 2 * y + c
        gather = self.gather
        local, sends, recvs = [], [], []
        for a in range(self.n):
            local.append(pltpu.make_async_copy(ins[a] if gather else ins[a].at[me], outs[a].at[me if gather else 0],
                                               local_sems.at[a]))
        for r in range(1, N_DEV):
            px = 1 - x if r & 4 else x
            py = 1 - y if r & 2 else y
            pc = 1 - c if r & 1 else c
            idx = 4 * px + 2 * py + pc
            for a in range(self.n):
                src = ins[a] if gather else ins[a].at[idx]
                pair = dict(send_sem=send_sems.at[a, r - 1], recv_sem=recv_sems.at[a, r - 1],
                            device_id=(px, py, pc), device_id_type=pl.DeviceIdType.MESH)
                sends.append(pltpu.make_async_remote_copy(src_ref=src, dst_ref=outs[a].at[me if gather else r], **pair))
                recvs.append(pltpu.make_async_remote_copy(src_ref=src, dst_ref=outs[a].at[idx if gather else r], **pair))
        return local, sends, recvs

    def start(self, ins, outs, sems):
        local, sends, _ = self._copies(ins, outs, sems)
        for cp in local + sends:
            cp.start()

    def wait(self, ins, outs, sems):
        local, sends, recvs = self._copies(ins, outs, sems)
        for cp in recvs:
            cp.wait_recv()
        for cp in sends:
            cp.wait_send()
        for cp in local:
            cp.wait()


def _peer_exchange(name, arrays, gather):
    ex = _Exchange(arrays, gather)
    n = ex.n

    def body(*refs):
        ins, outs, sems = refs[:n], refs[n:2 * n], refs[2 * n:]
        ex.start(ins, outs, sems)
        ex.wait(ins, outs, sems)

    return pl.pallas_call(body, name=name, in_specs=ex.specs, out_specs=ex.specs, out_shape=ex.out_shape,
                          scratch_shapes=ex.scratch)(*arrays)


def _sum_slots(name, x):
    def body(x_ref, o_ref):
        acc = x_ref[0]
        for s in range(1, N_DEV):
            acc = acc + x_ref[s]
        o_ref[...] = acc
    return pl.pallas_call(body, name=name, out_shape=jax.ShapeDtypeStruct(x.shape[1:], F32))(x)


def _adamw(name, w, slots, m, v):
    n_slots, rows, cols = slots.shape
    tr = next((t for t in (256, 176, 128) if rows % t == 0 and rows > t), rows)

    def body(w_ref, s_ref, m_ref, v_ref, g_ref, d_ref, nm_ref, nv_ref):
        g = s_ref[0].astype(F32)
        for s in range(1, n_slots):
            g = g + s_ref[s].astype(F32)
        nm = ADAM_B1 * m_ref[...] + (1.0 - ADAM_B1) * g
        nv = ADAM_B2 * v_ref[...] + (1.0 - ADAM_B2) * (g * g)
        m_hat = nm / (1.0 - ADAM_B1 ** ADAM_STEP)
        v_hat = nv / (1.0 - ADAM_B2 ** ADAM_STEP)
        g_ref[...] = g
        d_ref[...] = -ADAM_LR * (m_hat / (jnp.sqrt(v_hat) + ADAM_EPS) + ADAM_WD * w_ref[...])
        nm_ref[...] = nm
        nv_ref[...] = nv

    spec = pl.BlockSpec((tr, cols), lambda i: (i, 0))
    out = jax.ShapeDtypeStruct((rows, cols), F32)
    return pl.pallas_call(
        body, name=name, grid=(rows // tr,),
        in_specs=[spec, pl.BlockSpec((n_slots, tr, cols), lambda i: (0, i, 0)), spec, spec],
        out_specs=[spec] * 4, out_shape=[out] * 4,
        compiler_params=_params((n_slots + 7) * tr * cols * 4, ("parallel",)),
    )(w, slots, m, v)


def _pad_lanes(a):
    return jnp.pad(a, ((0, 0), (0, LANES - a.shape[1])))


def kernel(x, meta_tokens, mix_pre_g, w_in, ssd_conv_w, ssd_conv_b, ssd_dt_bias, ssd_a_log, ssd_d, ssd_norm_g, sb_norm_g, w_out, mix_post_g, ffn_pre_g, w_up, ffn_conv_w, ffn_conv_b, w_down, ffn_post_g, loss_target, m_meta_tokens, m_mix_pre_g, m_w_in, m_ssd_conv_w, m_ssd_conv_b, m_ssd_dt_bias, m_ssd_a_log, m_ssd_d, m_ssd_norm_g, m_sb_norm_g, m_w_out, m_mix_post_g, m_ffn_pre_g, m_w_up, m_ffn_conv_w, m_ffn_conv_b, m_w_down, m_ffn_post_g, v_meta_tokens, v_mix_pre_g, v_w_in, v_ssd_conv_w, v_ssd_conv_b, v_ssd_dt_bias, v_ssd_a_log, v_ssd_d, v_ssd_norm_g, v_sb_norm_g, v_w_out, v_mix_post_g, v_ffn_pre_g, v_w_up, v_ffn_conv_w, v_ffn_conv_b, v_w_down, v_ffn_post_g):
    seq = x.shape[1]
    me = 4 * lax.axis_index("x") + 2 * lax.axis_index("y") + lax.axis_index("c")
    in_cols = w_in.shape[2]
    up_cols = w_up.shape[2]
    out_rows = w_out.shape[1]
    down_rows = w_down.shape[1]

    g_in, g_meta, g_scw, g_fcw = _peer_exchange(
        "gather_w_in", [w_in[0].astype(BF16), meta_tokens, ssd_conv_w[0], ffn_conv_w[0]], gather=True)
    late_weights = [w_out[0].astype(BF16), w_up[0].astype(BF16), w_down[0].astype(BF16)]
    w_in_full = g_in.transpose(1, 0, 2).reshape(D_MODEL, N_DEV * in_cols)
    off = [0, SSD_INNER, SSD_INNER + XBC, SSD_INNER + XBC + HEADS]
    w_z = w_in_full[:, off[0]:off[1]]
    w_xbc = w_in_full[:, off[1]:off[2]]
    w_dt = _pad_lanes(w_in_full[:, off[2]:off[3]])
    w_q = w_in_full[:, off[3]:off[3] + SSD_INNER]
    w_k = w_in_full[:, off[3] + SSD_INNER:off[3] + 2 * SSD_INNER]
    w_v = w_in_full[:, off[3] + 2 * SSD_INNER:off[3] + 3 * SSD_INNER]
    meta_full = g_meta.transpose(1, 0, 2).reshape(N_META, D_MODEL)
    scw_full = g_scw.transpose(1, 0, 2).reshape(SSD_CONV, XBC)
    fcw_full = g_fcw.transpose(1, 0, 2).reshape(FFN_CONV, D_FF)

    dt_bias_p, a_log_p, d_p = _pad_lanes(ssd_dt_bias), _pad_lanes(ssd_a_log), _pad_lanes(ssd_d)

    h0 = jnp.concatenate([jnp.zeros((PAD, D_MODEL), F32), meta_full, x[0]], axis=0)
    target = jnp.concatenate([jnp.zeros((BLK, D_MODEL), F32), loss_target[0]], axis=0)
    xn1 = _rms_fwd("rms_pre_mix", h0, mix_pre_g)
    z = _mm("proj_z", [(xn1, w_z)], "nn", F32)
    xbc_raw = _mm("proj_xbc", [(xn1, w_xbc)], "nn", F32)
    dtr = _mm("proj_dt", [(xn1, w_dt)], "nn", F32)
    q = _mm("proj_q", [(xn1, w_q * SB_SCALE)], "nn", BF16)
    k = _mm("proj_k", [(xn1, w_k)], "nn", BF16)
    v = _mm("proj_v", [(xn1, w_v)], "nn", BF16)
    xbc_act = _ssd_conv_fwd(xbc_raw, scw_full, ssd_conv_b)
    ypre, y_ssd, states = _ssd_fwd(xbc_act, dtr, z, dt_bias_p, a_log_p, d_p, ssd_norm_g)
    o, keep_total, (g_out, g_up, g_down) = _attn_fwd(q, k, v, _Exchange(late_weights, gather=True), late_weights)
    w_out_full = g_out.reshape(N_DEV * out_rows, D_MODEL)
    wo_ssd, wo_sb = w_out_full[:SSD_INNER], w_out_full[SSD_INNER:]
    w_up_full = g_up.transpose(1, 0, 2).reshape(D_MODEL, N_DEV * up_cols)
    w_gate, w_lin = w_up_full[:, :D_FF], w_up_full[:, D_FF:]
    w_down_full = g_down.reshape(N_DEV * down_rows, D_MODEL)
    y_sb = _rms_fwd("rms_sb", o, sb_norm_g)
    mix = _mm("mix_out", [(y_ssd, wo_ssd), (y_sb, wo_sb)], "nn", F32)
    h1, xn2 = _mix_post(mix, h0, mix_post_g, ffn_pre_g)
    g_raw = _mm("ffn_gate", [(xn2, w_gate)], "nn", F32)
    u = _mm("ffn_lin", [(xn2, w_lin)], "nn", F32)
    act = _ffn_act(g_raw, u, fcw_full, ffn_conv_b)
    f = _mm("ffn_down", [(act, w_down_full)], "nn", F32)
    dh2, df, loss_row, dg_ffn_post = _loss_post(f, h1, target, ffn_post_g)
    loss = lax.psum(loss_row[0, 0], ("x", "y", "c"))

    dact = _mm("d_act", [(df, w_down_full)], "nt", F32)
    dw_down = _mm("dw_down", [(act, df)], "tn", F32)
    dg_conv, du = _ffn_bwd_act(dact, u, g_raw, fcw_full, ffn_conv_b)
    dg_raw, dfcw0, dfcw1, dfcw2, dfcb = _conv_bwd("ffn_conv_bwd", dg_conv, g_raw, fcw_full, FFN_CONV)
    dxn2 = _mm("d_xn2", [(dg_raw, w_gate), (du, w_lin)], "nt", F32)
    dw_gate = _mm("dw_gate", [(xn2, dg_raw)], "tn", F32)
    dw_lin = _mm("dw_lin", [(xn2, du)], "tn", F32)
    dh1, dmix, dg_ffn_pre, dg_mix_post = _mid_bwd(dxn2, h1, dh2, mix, ffn_pre_g, mix_post_g)

    dy_ssd = _mm("d_yssd", [(dmix, wo_ssd)], "nt", F32)
    dy_sb = _mm("d_ysb", [(dmix, wo_sb)], "nt", F32)
    dwo_ssd = _mm("dw_out_ssd", [(y_ssd, dmix)], "tn", F32)
    dwo_sb = _mm("dw_out_sb", [(y_sb, dmix)], "tn", F32)
    do, dg_sb = _norm_bwd("sb_norm_bwd", dy_sb, o, sb_norm_g)
    half = N_DEV // 2
    early_slabs = [
        jnp.concatenate([dwo_ssd, dwo_sb], axis=0).reshape(N_DEV, out_rows, D_MODEL),
        jnp.concatenate([dw_gate.reshape(D_MODEL, half, up_cols).transpose(1, 0, 2),
                         dw_lin.reshape(D_MODEL, half, up_cols).transpose(1, 0, 2)], axis=0),
        dw_down.reshape(N_DEV, down_rows, D_MODEL)]
    dq, dk, dv, (l_out, l_up, l_down) = _attn_bwd(q, k, v, keep_total, do, _Exchange(early_slabs, gather=False), early_slabs)
    dz, dxbc_act, ddtr, dg_ssd_norm, dd_skip, da_log, ddt_bias = _ssd_bwd(
        dy_ssd, ypre, z, xbc_act, dtr, states, dt_bias_p, a_log_p, d_p, ssd_norm_g)
    dconv = _ssd_conv_bwd_act(dxbc_act, xbc_raw, scw_full, ssd_conv_b)
    dxbc_raw, dscw0, dscw1, dscw2, dscw3, dscb = _conv_bwd("ssd_conv_bwd", dconv, xbc_raw, scw_full, SSD_CONV)
    segs = [(dz, w_z), (dxbc_raw, w_xbc), (ddtr, w_dt), (dq, w_q), (dk, w_k), (dv, w_v)]
    dxn1 = _mm("d_xn1", segs, "nt", F32)
    dw_segs = [_mm("dw_in_%d" % s, [(xn1, d)], "tn", BF16) for s, (d, _) in enumerate(segs)]
    dw_segs[2] = dw_segs[2][:, :HEADS]
    dw_in = jnp.concatenate(dw_segs, axis=1)
    dh0, dg_mix_pre = _first_bwd(dxn1, h0, dh1, mix_pre_g)
    grad_x = dh0[BLK:][None]

    slab_in = dw_in.reshape(D_MODEL, N_DEV, in_cols).transpose(1, 0, 2)
    (l_in,) = _peer_exchange("scatter_dw_in", [slab_in], gather=False)

    small = [dg_mix_pre, dscb, ddt_bias, da_log, dd_skip, dg_ssd_norm, dg_sb, dg_mix_post, dg_ffn_pre, dfcb,
             dg_ffn_post, dh0[PAD:BLK].reshape(1, -1), dscw0, dscw1, dscw2, dscw3, dfcw0, dfcw1, dfcw2]
    sizes = [a.shape[1] for a in small]
    total = sum(sizes)
    rows_packed = -(-total // (LANES * HALO)) * HALO
    packed = jnp.pad(jnp.concatenate(small, axis=1), ((0, 0), (0, rows_packed * LANES - total)))
    (gathered,) = _peer_exchange("gather_small_grads", [packed.reshape(rows_packed, LANES)], gather=True)
    summed = _sum_slots("sum_small_grads", gathered).reshape(1, rows_packed * LANES)
    pieces, at = [], 0
    for s in sizes:
        pieces.append(summed[:, at:at + s])
        at += s
    (g_mix_pre, g_scb, g_dtb, g_alog, g_dskip, g_ssd_norm, g_sb, g_mix_post, g_ffn_pre, g_fcb, g_ffn_post,
     g_meta_flat, gs0, gs1, gs2, gs3, gf0, gf1, gf2) = pieces
    g_dtb, g_alog, g_dskip = g_dtb[:, :HEADS], g_alog[:, :HEADS], g_dskip[:, :HEADS]
    g_meta_full = g_meta_flat.reshape(N_META, D_MODEL)
    g_scw_full = jnp.concatenate([gs0, gs1, gs2, gs3], axis=0)
    g_fcw_full = jnp.concatenate([gf0, gf1, gf2], axis=0)
    meta_cols, scw_cols, fcw_cols = meta_tokens.shape[1], ssd_conv_w.shape[2], ffn_conv_w.shape[2]
    g_meta_mine = lax.dynamic_slice(g_meta_full, (0, me * meta_cols), (N_META, meta_cols))
    g_scw_mine = lax.dynamic_slice(g_scw_full, (0, me * scw_cols), (SSD_CONV, scw_cols))
    g_fcw_mine = lax.dynamic_slice(g_fcw_full, (0, me * fcw_cols), (FFN_CONV, fcw_cols))

    def lead(a):
        return a[None]

    upd = [
        _adamw("adamw_meta", meta_tokens, lead(g_meta_mine), m_meta_tokens, v_meta_tokens),
        _adamw("adamw_mix_pre_g", mix_pre_g, lead(g_mix_pre), m_mix_pre_g, v_mix_pre_g),
        [lead(a) for a in _adamw("adamw_w_in", w_in[0], l_in, m_w_in[0], v_w_in[0])],
        [lead(a) for a in _adamw("adamw_ssd_conv_w", ssd_conv_w[0], lead(g_scw_mine), m_ssd_conv_w[0], v_ssd_conv_w[0])],
        _adamw("adamw_ssd_conv_b", ssd_conv_b, lead(g_scb), m_ssd_conv_b, v_ssd_conv_b),
        _adamw("adamw_ssd_dt_bias", ssd_dt_bias, lead(g_dtb), m_ssd_dt_bias, v_ssd_dt_bias),
        _adamw("adamw_ssd_a_log", ssd_a_log, lead(g_alog), m_ssd_a_log, v_ssd_a_log),
        _adamw("adamw_ssd_d", ssd_d, lead(g_dskip), m_ssd_d, v_ssd_d),
        _adamw("adamw_ssd_norm_g", ssd_norm_g, lead(g_ssd_norm), m_ssd_norm_g, v_ssd_norm_g),
        _adamw("adamw_sb_norm_g", sb_norm_g, lead(g_sb), m_sb_norm_g, v_sb_norm_g),
        [lead(a) for a in _adamw("adamw_w_out", w_out[0], l_out, m_w_out[0], v_w_out[0])],
        _adamw("adamw_mix_post_g", mix_post_g, lead(g_mix_post), m_mix_post_g, v_mix_post_g),
        _adamw("adamw_ffn_pre_g", ffn_pre_g, lead(g_ffn_pre), m_ffn_pre_g, v_ffn_pre_g),
        [lead(a) for a in _adamw("adamw_w_up", w_up[0], l_up, m_w_up[0], v_w_up[0])],
        [lead(a) for a in _adamw("adamw_ffn_conv_w", ffn_conv_w[0], lead(g_fcw_mine), m_ffn_conv_w[0], v_ffn_conv_w[0])],
        _adamw("adamw_ffn_conv_b", ffn_conv_b, lead(g_fcb), m_ffn_conv_b, v_ffn_conv_b),
        [lead(a) for a in _adamw("adamw_w_down", w_down[0], l_down, m_w_down[0], v_w_down[0])],
        _adamw("adamw_ffn_post_g", ffn_post_g, lead(g_ffn_post), m_ffn_post_g, v_ffn_post_g),
    ]
    grads = [u_[0] for u_ in upd]
    deltas = [u_[1] for u_ in upd]
    new_m = [u_[2] for u_ in upd]
    new_v = [u_[3] for u_ in upd]
    return (loss, grad_x, *grads, *deltas, *new_m, *new_v)
```

```python
import math

import jax
import jax.numpy as jnp
from jax import lax
from jax.experimental import pallas as pl
from jax.experimental.pallas import tpu as pltpu

F32 = jnp.float32
BF16 = jnp.bfloat16
HI = lax.Precision.HIGHEST

D_MODEL = 1024
N_META = 16
BLK = 128
PAD = BLK - N_META
HEADS = 16
HEAD_DIM = 64
SSD_GROUPS = 2
SSD_STATE = 128
HEADS_PER_GROUP = HEADS // SSD_GROUPS
SSD_INNER = HEADS * HEAD_DIM
SSD_CONV = 4
XBC = SSD_INNER + 2 * SSD_GROUPS * SSD_STATE
OFF_B = SSD_INNER
OFF_C = SSD_INNER + SSD_GROUPS * SSD_STATE
D_FF = 2816
FFN_CONV = 3
EPS = 1e-6
SB_SCALE = 1.0 / math.sqrt(HEAD_DIM)
N_DEV = 8
LANES = 128
HALO = 8

ADAM_LR = 0.001
ADAM_B1 = 0.9
ADAM_B2 = 0.999
ADAM_EPS = 1e-08
ADAM_WD = 0.01
ADAM_STEP = 10

VMEM_FLOOR = 32 << 20
VMEM_CEIL = 60 << 20
MM_BUDGET = 20 << 20
ROW_BUDGET = 6 << 20

NN = (((1,), (0,)), ((), ()))
NT = (((1,), (1,)), ((), ()))
TN = (((0,), (0,)), ((), ()))


def _params(tile_bytes, sem=None):
    limit = int(min(max(2 * tile_bytes + (8 << 20), VMEM_FLOOR), VMEM_CEIL))
    return pltpu.CompilerParams(vmem_limit_bytes=limit, dimension_semantics=sem)


def _nbytes(shape, dtype):
    n = 1
    for s in shape:
        n *= s
    return n * jnp.dtype(dtype).itemsize


def _dot(a, b, dims=NN, precision=None):
    return lax.dot_general(a, b, dims, precision=precision, preferred_element_type=F32)


def _softplus(x):
    return jnp.maximum(x, 0.0) + jnp.log1p(jnp.exp(-jnp.abs(x)))


def _rms(x, g):
    r = lax.rsqrt(jnp.mean(x * x, axis=-1, keepdims=True) + EPS)
    return x * r * g


def _rms_bwd(dy, x, g):
    r = lax.rsqrt(jnp.mean(x * x, axis=-1, keepdims=True) + EPS)
    xh = x * r
    u = dy * g
    dx = r * (u - xh * jnp.mean(xh * u, axis=-1, keepdims=True))
    return dx, jnp.sum(dy * xh, axis=0, keepdims=True)


def _gelu(x):
    c = math.sqrt(2.0 / math.pi)
    return 0.5 * x * (1.0 + jnp.tanh(c * (x + 0.044715 * x * x * x)))


def _gelu_grad(x):
    c = math.sqrt(2.0 / math.pi)
    t = jnp.tanh(c * (x + 0.044715 * x * x * x))
    return 0.5 * (1.0 + t) + 0.5 * x * (1.0 - t * t) * c * (1.0 + 3.0 * 0.044715 * x * x)


def _row_tile(rows, bytes_per_row):
    big = 384
    return big if rows % big == 0 and big * bytes_per_row <= ROW_BUDGET else BLK


def _mm(name, pairs, mode, out_dtype):
    a0, b0 = pairs[0]
    if mode == "tn":
        m, n = a0.shape[1], b0.shape[1]
    elif mode == "nt":
        m, n = a0.shape[0], b0.shape[0]
    else:
        m, n = a0.shape[0], b0.shape[1]
    dims = {"nn": NN, "nt": NT, "tn": TN}[mode]

    def tile_bytes(tm, tn):
        tot = tm * tn * jnp.dtype(out_dtype).itemsize
        for a, b in pairs:
            k = a.shape[0] if mode == "tn" else a.shape[1]
            tot += tm * k * a.dtype.itemsize + tn * k * b.dtype.itemsize
        return tot

    cands_m = [t for t in (1408, 1024, 512, 384, 256, 128) if m % t == 0] or [m]
    cands_n = [t for t in (1408, 1024, 768, 512, 256, 128) if n % t == 0] or [n]
    best = None
    for tm in cands_m:
        for tn in cands_n:
            if tile_bytes(tm, tn) <= MM_BUDGET and (best is None or tm * tn > best[0] * best[1]):
                best = (tm, tn)
    tm, tn = best if best is not None else (cands_m[-1], cands_n[-1])
    npairs = len(pairs)

    def body(*refs):
        o_ref = refs[2 * npairs]
        acc = None
        for p in range(npairs):
            part = _dot(refs[2 * p][...], refs[2 * p + 1][...], dims)
            acc = part if acc is None else acc + part
        o_ref[...] = acc.astype(o_ref.dtype)

    in_specs, args = [], []
    for a, b in pairs:
        if mode == "tn":
            k = a.shape[0]
            in_specs += [pl.BlockSpec((k, tm), lambda i, j: (0, i)), pl.BlockSpec((k, tn), lambda i, j: (0, j))]
        elif mode == "nt":
            k = a.shape[1]
            in_specs += [pl.BlockSpec((tm, k), lambda i, j: (i, 0)), pl.BlockSpec((tn, k), lambda i, j: (j, 0))]
        else:
            k = a.shape[1]
            in_specs += [pl.BlockSpec((tm, k), lambda i, j: (i, 0)), pl.BlockSpec((k, tn), lambda i, j: (0, j))]
        args += [a, b]
    return pl.pallas_call(
        body, name=name, grid=(m // tm, n // tn), in_specs=in_specs,
        out_specs=pl.BlockSpec((tm, tn), lambda i, j: (i, j)),
        out_shape=jax.ShapeDtypeStruct((m, n), out_dtype),
        compiler_params=_params(tile_bytes(tm, tn), ("parallel", "parallel")),
    )(*args)


def _rowcall(name, fn, rows=(), prevs=(), nexts=(), pars=(), out_rows=(), out_accs=()):
    rows, prevs, nexts, pars = list(rows), list(prevs), list(nexts), list(pars)
    n_rows = (rows + prevs + nexts)[0].shape[0]
    per_row = sum(a.shape[1] * a.dtype.itemsize for a in rows + prevs + nexts)
    per_row += sum(c * jnp.dtype(dt).itemsize for c, dt in out_rows) + sum(a.shape[1] * 4 for a in prevs + nexts)
    tm = _row_tile(n_rows, per_row)
    nt = n_rows // tm
    hb = tm // HALO
    nr, npv, nnx, npar, nor, noa = len(rows), len(prevs), len(nexts), len(pars), len(out_rows), len(out_accs)

    def body(*refs):
        i = pl.program_id(0)
        k = 0
        row_refs = refs[k:k + nr]; k += nr
        pc = refs[k:k + npv]; k += npv
        ph = refs[k:k + npv]; k += npv
        nc = refs[k:k + nnx]; k += nnx
        nh = refs[k:k + nnx]; k += nnx
        par_refs = refs[k:k + npar]; k += npar
        orow = refs[k:k + nor]; k += nor
        oacc = refs[k:k + noa]; k += noa
        pscr = refs[k:k + npv]; k += npv
        nscr = refs[k:k + nnx]
        for c_, h_, s_ in zip(pc, ph, pscr):
            s_[0:HALO, :] = h_[...] * (i > 0).astype(F32)
            s_[HALO:HALO + tm, :] = c_[...]
        for c_, h_, s_ in zip(nc, nh, nscr):
            s_[0:tm, :] = c_[...]
            s_[tm:tm + HALO, :] = h_[...] * (i < nt - 1).astype(F32)
        prev_fns = [(lambda s, s_=s_: s_[pl.ds(HALO - s, tm), :]) for s_ in pscr]
        next_fns = [(lambda s, s_=s_: s_[pl.ds(s, tm), :]) for s_ in nscr]
        row_vals, acc_vals = fn(i, tm, row_refs, prev_fns, next_fns, par_refs)
        for r_, v in zip(orow, row_vals):
            r_[...] = v.astype(r_.dtype)
        if noa:
            @pl.when(i == 0)
            def _():
                for r_ in oacc:
                    r_[...] = jnp.zeros_like(r_)
            for r_, v in zip(oacc, acc_vals):
                r_[...] += v

    def row_spec(a):
        return pl.BlockSpec((tm, a.shape[1]), lambda i: (i, 0))

    def whole(shape):
        return pl.BlockSpec(shape, lambda i: (0, 0))

    in_specs = [row_spec(a) for a in rows]
    in_specs += [row_spec(a) for a in prevs]
    in_specs += [pl.BlockSpec((HALO, a.shape[1]), lambda i: (jnp.maximum(i * hb - 1, 0), 0)) for a in prevs]
    in_specs += [row_spec(a) for a in nexts]
    in_specs += [pl.BlockSpec((HALO, a.shape[1]), lambda i: (jnp.minimum((i + 1) * hb, n_rows // HALO - 1), 0)) for a in nexts]
    in_specs += [whole(p.shape) for p in pars]
    out_specs = [pl.BlockSpec((tm, c), lambda i: (i, 0)) for c, _ in out_rows] + [whole(s) for s in out_accs]
    out_shape = [jax.ShapeDtypeStruct((n_rows, c), dt) for c, dt in out_rows]
    out_shape += [jax.ShapeDtypeStruct(s, F32) for s in out_accs]
    scratch = [pltpu.VMEM((tm + HALO, a.shape[1]), F32) for a in prevs + nexts]
    tile = tm * per_row
    res = pl.pallas_call(
        body, name=name, grid=(nt,), in_specs=in_specs, out_specs=out_specs, out_shape=out_shape,
        scratch_shapes=scratch, compiler_params=_params(2 * tile, ("arbitrary",)),
    )(*rows, *prevs, *prevs, *nexts, *nexts, *pars)
    return res


def _row_ids(i, tm):
    return i * tm + lax.broadcasted_iota(jnp.int32, (tm, 1), 0)


def _conv_taps(prev_fn, w_ref, b_ref, taps):
    acc = b_ref[...]
    for k in range(taps):
        acc = acc + prev_fn(taps - 1 - k) * w_ref[k:k + 1, :]
    return acc


def _rms_fwd(name, h, g):
    def fn(i, tm, rows, prevs, nexts, pars):
        return [_rms(rows[0][...], pars[0][...])], []
    return _rowcall(name, fn, rows=[h], pars=[g], out_rows=[(h.shape[1], BF16)])[0]


def _ssd_conv_fwd(xbc_raw, w, b):
    def fn(i, tm, rows, prevs, nexts, pars):
        c = _conv_taps(prevs[0], pars[0], pars[1], SSD_CONV)
        y = c * jax.nn.sigmoid(c)
        return [jnp.where(_row_ids(i, tm) >= PAD, y, 0.0)], []
    return _rowcall("ssd_conv_fwd", fn, prevs=[xbc_raw], pars=[w, b], out_rows=[(XBC, F32)])[0]


def _mix_post(mix, h0, g_post, g_pre):
    def fn(i, tm, rows, prevs, nexts, pars):
        h1 = rows[1][...] + _rms(rows[0][...], pars[0][...])
        return [h1, _rms(h1, pars[1][...])], []
    return _rowcall("mix_post", fn, rows=[mix, h0], pars=[g_post, g_pre],
                    out_rows=[(D_MODEL, F32), (D_MODEL, BF16)])


def _ffn_act(g_raw, u, w, b):
    def fn(i, tm, rows, prevs, nexts, pars):
        g = _conv_taps(prevs[0], pars[0], pars[1], FFN_CONV)
        return [_gelu(g) * rows[0][...]], []
    return _rowcall("ffn_act", fn, rows=[u], prevs=[g_raw], pars=[w, b], out_rows=[(D_FF, BF16)])[0]


def _loss_post(f, h1, target, g_post):
    def fn(i, tm, rows, prevs, nexts, pars):
        fv, g = rows[0][...], pars[0][...]
        h2 = rows[1][...] + _rms(fv, g)
        real = _row_ids(i, tm) >= BLK
        diff = jnp.where(real, h2 - rows[2][...], 0.0)
        loss = 0.5 * jnp.sum(jnp.mean(diff * diff, axis=-1, keepdims=True))
        dh2 = diff * (1.0 / D_MODEL)
        df, dg = _rms_bwd(dh2, fv, g)
        return [dh2, df], [jnp.zeros((1, LANES), F32) + loss, dg]
    return _rowcall("loss_post", fn, rows=[f, h1, target], pars=[g_post],
                    out_rows=[(D_MODEL, F32), (D_MODEL, BF16)], out_accs=[(1, LANES), (1, D_MODEL)])


def _ffn_bwd_act(dact, u, g_raw, w, b):
    def fn(i, tm, rows, prevs, nexts, pars):
        g = _conv_taps(prevs[0], pars[0], pars[1], FFN_CONV)
        da = rows[0][...]
        return [da * rows[1][...] * _gelu_grad(g), da * _gelu(g)], []
    return _rowcall("ffn_bwd_act", fn, rows=[dact, u], prevs=[g_raw], pars=[w, b],
                    out_rows=[(D_FF, F32), (D_FF, BF16)])


def _conv_bwd(name, dy, x, w, taps):
    width = x.shape[1]

    def fn(i, tm, rows, prevs, nexts, pars):
        dy0 = nexts[0](0)
        dx = None
        for k in range(taps):
            term = nexts[0](taps - 1 - k) * pars[0][k:k + 1, :]
            dx = term if dx is None else dx + term
        dws = [jnp.sum(dy0 * prevs[0](taps - 1 - k), axis=0, keepdims=True) for k in range(taps)]
        return [dx], dws + [jnp.sum(dy0, axis=0, keepdims=True)]
    return _rowcall(name, fn, prevs=[x], nexts=[dy], pars=[w], out_rows=[(width, BF16)],
                    out_accs=[(1, width)] * (taps + 1))


def _mid_bwd(dxn2, h1, dh2, mix, g_pre, g_post):
    def fn(i, tm, rows, prevs, nexts, pars):
        d1, dg_pre = _rms_bwd(rows[0][...], rows[1][...], pars[0][...])
        dh1 = rows[2][...] + d1
        dmix, dg_post = _rms_bwd(dh1, rows[3][...], pars[1][...])
        return [dh1, dmix], [dg_pre, dg_post]
    return _rowcall("mid_bwd", fn, rows=[dxn2, h1, dh2, mix], pars=[g_pre, g_post],
                    out_rows=[(D_MODEL, F32), (D_MODEL, BF16)], out_accs=[(1, D_MODEL), (1, D_MODEL)])


def _norm_bwd(name, dy, x, g):
    def fn(i, tm, rows, prevs, nexts, pars):
        dx, dg = _rms_bwd(rows[0][...], rows[1][...], pars[0][...])
        return [dx], [dg]
    return _rowcall(name, fn, rows=[dy, x], pars=[g], out_rows=[(x.shape[1], BF16)], out_accs=[(1, x.shape[1])])


def _ssd_conv_bwd_act(dact, xbc_raw, w, b):
    def fn(i, tm, rows, prevs, nexts, pars):
        c = _conv_taps(prevs[0], pars[0], pars[1], SSD_CONV)
        s = jax.nn.sigmoid(c)
        dc = rows[0][...] * s * (1.0 + c * (1.0 - s))
        return [jnp.where(_row_ids(i, tm) >= PAD, dc, 0.0)], []
    return _rowcall("ssd_conv_bwd_act", fn, rows=[dact], prevs=[xbc_raw], pars=[w, b], out_rows=[(XBC, F32)])[0]


def _first_bwd(dxn1, h0, dh1, g):
    def fn(i, tm, rows, prevs, nexts, pars):
        d0, dg = _rms_bwd(rows[0][...], rows[1][...], pars[0][...])
        return [rows[2][...] + d0], [dg]
    return _rowcall("first_bwd", fn, rows=[dxn1, h0, dh1], pars=[g], out_rows=[(D_MODEL, F32)],
                    out_accs=[(1, D_MODEL)])


def _ssd_chunk_terms(c, dtr_ref, bias_ref, alog_ref):
    ri = lax.broadcasted_iota(jnp.int32, (BLK, BLK), 0)
    ci = lax.broadcasted_iota(jnp.int32, (BLK, BLK), 1)
    causal = ri >= ci
    tril = causal.astype(F32)
    triu = (ri <= ci).astype(F32)
    rowmask = ((c * BLK + lax.broadcasted_iota(jnp.int32, (BLK, 1), 0)) >= PAD).astype(F32)
    dt = _softplus(dtr_ref[...] + bias_ref[...]) * rowmask
    a_neg = -jnp.exp(alog_ref[...])
    a = dt * a_neg
    cs = _dot(tril, a, NN, HI)
    cs_t = _dot(a, triu, TN, HI)
    return causal, triu, rowmask, dt, a_neg, cs, cs_t


def _decay_matrix(causal, cs_h, cs_t_h):
    return jnp.where(causal, jnp.exp(jnp.where(causal, cs_h - cs_t_h, 0.0)), 0.0)


def _ssd_fwd(xbc, dtr, z, dt_bias, a_log, d_skip, norm_g):
    n_rows = xbc.shape[0]
    nb = n_rows // BLK

    def body(xbc_ref, dtr_ref, z_ref, bias_ref, alog_ref, d_ref, g_ref, ypre_ref, yssd_ref, st_ref, state):
        c = pl.program_id(0)

        @pl.when(c == 0)
        def _():
            state[...] = jnp.zeros_like(state)

        st_ref[...] = state[...]
        causal, _, _, dt, _, cs, cs_t = _ssd_chunk_terms(c, dtr_ref, bias_ref, alog_ref)
        cs_last = cs[BLK - 1:BLK, :]
        for g in range(SSD_GROUPS):
            b_g = xbc_ref[:, OFF_B + g * SSD_STATE:OFF_B + (g + 1) * SSD_STATE]
            c_g = xbc_ref[:, OFF_C + g * SSD_STATE:OFF_C + (g + 1) * SSD_STATE]
            c_b = c_g.astype(BF16)
            cb = _dot(c_b, b_g.astype(BF16), NT)
            for j in range(HEADS_PER_GROUP):
                h = g * HEADS_PER_GROUP + j
                sl = slice(h * HEAD_DIM, (h + 1) * HEAD_DIM)
                xs = xbc_ref[:, sl]
                x_b = (xs * dt[:, h:h + 1]).astype(BF16)
                cs_h = cs[:, h:h + 1]
                cl = cs_last[:, h:h + 1]
                decay = _decay_matrix(causal, cs_h, cs_t[h:h + 1, :])
                s_h = state[:, sl]
                y = _dot((cb * decay).astype(BF16), x_b)
                y = y + jnp.exp(cs_h) * _dot(c_b, s_h.astype(BF16))
                to_end = jnp.exp(cl - cs_h)
                state[:, sl] = jnp.exp(cl) * s_h + _dot((b_g * to_end).astype(BF16), x_b, TN)
                ypre_ref[:, sl] = y + d_ref[:, h:h + 1] * xs
        zz = z_ref[...]
        yg = ypre_ref[...] * (zz * jax.nn.sigmoid(zz))
        yssd_ref[...] = _rms(yg, g_ref[...]).astype(yssd_ref.dtype)

    blk = lambda w: pl.BlockSpec((BLK, w), lambda c: (c, 0))
    par = lambda a: pl.BlockSpec(a.shape, lambda c: (0, 0))
    return pl.pallas_call(
        body, name="ssd_fwd", grid=(nb,),
        in_specs=[blk(XBC), blk(LANES), blk(SSD_INNER), par(dt_bias), par(a_log), par(d_skip), par(norm_g)],
        out_specs=[blk(SSD_INNER), blk(SSD_INNER), pl.BlockSpec((None, SSD_STATE, SSD_INNER), lambda c: (c, 0, 0))],
        out_shape=[jax.ShapeDtypeStruct((n_rows, SSD_INNER), F32), jax.ShapeDtypeStruct((n_rows, SSD_INNER), BF16),
                   jax.ShapeDtypeStruct((nb, SSD_STATE, SSD_INNER), F32)],
        scratch_shapes=[pltpu.VMEM((SSD_STATE, SSD_INNER), F32)],
        compiler_params=_params(8 << 20, ("arbitrary",)),
    )(xbc, dtr, z, dt_bias, a_log, d_skip, norm_g)


def _ssd_bwd(dy, ypre, z, xbc, dtr, states, dt_bias, a_log, d_skip, norm_g):
    n_rows = xbc.shape[0]
    nb = n_rows // BLK

    def body(dy_ref, ypre_ref, z_ref, xbc_ref, dtr_ref, st_ref, bias_ref, alog_ref, d_ref, g_ref,
             dz_ref, dxbc_ref, ddtr_ref, dgn_ref, dd_ref, dal_ref, ddtb_ref, dstate, dyp):
        step = pl.program_id(0)
        c = nb - 1 - step

        @pl.when(step == 0)
        def _():
            dstate[...] = jnp.zeros_like(dstate)
            for r_ in (dgn_ref, dd_ref, dal_ref, ddtb_ref):
                r_[...] = jnp.zeros_like(r_)

        yp, zz = ypre_ref[...], z_ref[...]
        sz = jax.nn.sigmoid(zz)
        silu = zz * sz
        dyg, dgn = _rms_bwd(dy_ref[...], yp * silu, g_ref[...])
        dgn_ref[...] += dgn
        dz_ref[...] = (dyg * yp * (sz * (1.0 + zz * (1.0 - sz)))).astype(dz_ref.dtype)
        dyp[...] = dyg * silu

        causal, triu, rowmask, dt, a_neg, cs, cs_t = _ssd_chunk_terms(c, dtr_ref, bias_ref, alog_ref)
        cs_last = cs[BLK - 1:BLK, :]
        lane = lax.broadcasted_iota(jnp.int32, (1, LANES), 1)
        sub = lax.broadcasted_iota(jnp.int32, (BLK, 1), 0)
        last_row = (sub == BLK - 1).astype(F32)
        dcs_col = jnp.zeros((BLK, LANES), F32)
        dcs_row = jnp.zeros((LANES, BLK), F32)
        ddt_x = jnp.zeros((BLK, LANES), F32)
        dd_row = jnp.zeros((1, LANES), F32)
        for g in range(SSD_GROUPS):
            b_g = xbc_ref[:, OFF_B + g * SSD_STATE:OFF_B + (g + 1) * SSD_STATE]
            c_g = xbc_ref[:, OFF_C + g * SSD_STATE:OFF_C + (g + 1) * SSD_STATE]
            b_b, c_b = b_g.astype(BF16), c_g.astype(BF16)
            cb = _dot(c_b, b_b, NT)
            db_g = jnp.zeros((BLK, SSD_STATE), F32)
            dc_g = jnp.zeros((BLK, SSD_STATE), F32)
            for j in range(HEADS_PER_GROUP):
                h = g * HEADS_PER_GROUP + j
                sl = slice(h * HEAD_DIM, (h + 1) * HEAD_DIM)
                d_y = dyp[:, sl]
                xs = xbc_ref[:, sl]
                dt_h = dt[:, h:h + 1]
                x_dt = xs * dt_h
                x_b, dy_b = x_dt.astype(BF16), d_y.astype(BF16)
                cs_h = cs[:, h:h + 1]
                cl = cs_last[:, h:h + 1]
                decay = _decay_matrix(causal, cs_h, cs_t[h:h + 1, :])
                s_h, ds_h = st_ref[:, sl], dstate[:, sl]
                s_b, ds_b = s_h.astype(BF16), ds_h.astype(BF16)
                to_end = jnp.exp(cl - cs_h)
                from_start = jnp.exp(cs_h)
                chunk_decay = jnp.exp(cl)

                d_x = _dot((cb * decay).astype(BF16), dy_b, TN) + to_end * _dot(b_b, ds_b)
                gl = _dot(dy_b, x_b, NT) * decay
                gl_b = gl.astype(BF16)
                c_s = _dot(c_b, s_b)
                x_ds = _dot(x_b, ds_b, NT)
                dc_g = dc_g + _dot(gl_b, b_b) + from_start * _dot(dy_b, s_b, NT)
                db_g = db_g + _dot(gl_b, c_b, TN) + to_end * x_ds
                w_m = gl * cb
                d_end = jnp.sum(x_ds * b_g, axis=1, keepdims=True) * to_end
                at_last = jnp.sum(d_end) + chunk_decay * jnp.sum(ds_h * s_h)
                col = (jnp.sum(w_m, axis=1, keepdims=True)
                       + from_start * jnp.sum(d_y * c_s, axis=1, keepdims=True)
                       - d_end + last_row * at_last)
                onehot = (lane == h).astype(F32)
                dcs_col = dcs_col + col * onehot
                dcs_row = dcs_row + (sub == h).astype(F32) * jnp.sum(w_m, axis=0, keepdims=True)
                ddt_x = ddt_x + jnp.sum(d_x * xs, axis=1, keepdims=True) * onehot
                dd_row = dd_row + jnp.sum(d_y * xs) * onehot
                dxbc_ref[:, sl] = d_x * dt_h + d_ref[:, h:h + 1] * d_y
                dstate[:, sl] = chunk_decay * ds_h + _dot((c_g * from_start).astype(BF16), dy_b, TN)
            dxbc_ref[:, OFF_B + g * SSD_STATE:OFF_B + (g + 1) * SSD_STATE] = db_g
            dxbc_ref[:, OFF_C + g * SSD_STATE:OFF_C + (g + 1) * SSD_STATE] = dc_g
        da = _dot(triu, dcs_col, NN, HI) - _dot(triu, dcs_row, NT, HI)
        ddt = (da * a_neg + ddt_x) * rowmask
        ddtr = ddt * jax.nn.sigmoid(dtr_ref[...] + bias_ref[...]) * (lane < HEADS).astype(F32)
        ddtr_ref[...] = ddtr.astype(ddtr_ref.dtype)
        ddtb_ref[...] += jnp.sum(ddtr, axis=0, keepdims=True)
        dal_ref[...] += jnp.sum(da * dt, axis=0, keepdims=True) * a_neg
        dd_ref[...] += dd_row

    blk = lambda w: pl.BlockSpec((BLK, w), lambda s: (nb - 1 - s, 0))
    par = lambda a: pl.BlockSpec(a.shape, lambda s: (0, 0))
    acc = lambda w: pl.BlockSpec((1, w), lambda s: (0, 0))
    return pl.pallas_call(
        body, name="ssd_bwd", grid=(nb,),
        in_specs=[blk(SSD_INNER), blk(SSD_INNER), blk(SSD_INNER), blk(XBC), blk(LANES),
                  pl.BlockSpec((None, SSD_STATE, SSD_INNER), lambda s: (nb - 1 - s, 0, 0)),
                  par(dt_bias), par(a_log), par(d_skip), par(norm_g)],
        out_specs=[blk(SSD_INNER), blk(XBC), blk(LANES), acc(SSD_INNER), acc(LANES), acc(LANES), acc(LANES)],
        out_shape=[jax.ShapeDtypeStruct((n_rows, SSD_INNER), BF16), jax.ShapeDtypeStruct((n_rows, XBC), F32),
                   jax.ShapeDtypeStruct((n_rows, LANES), BF16), jax.ShapeDtypeStruct((1, SSD_INNER), F32),
                   jax.ShapeDtypeStruct((1, LANES), F32), jax.ShapeDtypeStruct((1, LANES), F32),
                   jax.ShapeDtypeStruct((1, LANES), F32)],
        scratch_shapes=[pltpu.VMEM((SSD_STATE, SSD_INNER), F32), pltpu.VMEM((BLK, SSD_INNER), F32)],
        compiler_params=_params(10 << 20, ("arbitrary",)),
    )(dy, ypre, z, xbc, dtr, states, dt_bias, a_log, d_skip, norm_g)


HEAD_GROUP = 4


def _sb_logits(zl, valid):
    log_keep = -(jnp.maximum(zl, 0.0) + jnp.log(1.0 + jnp.exp(-jnp.abs(zl))))
    log_beta = log_keep + zl
    if valid is not None:
        log_keep = jnp.where(valid, log_keep, 0.0)
    return log_beta, log_keep


def _sums_dot(x, tri_and_ones):
    hi = x.astype(BF16)
    lo = (x - hi.astype(F32)).astype(BF16)
    both = _dot(jnp.concatenate([hi, lo], axis=1), tri_and_ones)
    return both[:, :BLK], both[:, BLK:]


def _tri_and_ones(tri):
    half = jnp.concatenate([tri, jnp.ones((BLK, BLK), F32)], axis=1)
    return jnp.concatenate([half, half], axis=0).astype(BF16)


def _tile_mask(i, j, ri, ci):
    key = j * BLK + ci
    return (key < i * BLK + ri) & (key >= PAD)


def _pair_lanes(p):
    return slice(p * 2 * HEAD_DIM, (p + 1) * 2 * HEAD_DIM)


def _head_stack(x_pair, first):
    zero = jnp.zeros_like(x_pair)
    return jnp.concatenate([jnp.where(first, x_pair, zero), jnp.where(first, zero, x_pair)], axis=0)


def _block_rows(j):
    return pl.ds(j * BLK if isinstance(j, int) else pl.multiple_of(j * BLK, BLK), BLK)


def _sweep(i, tile, leftwards):
    step = -1 if leftwards else 1
    first, last = (i, 0) if leftwards else (0, i)
    tile(first, True, first, jnp.maximum(i - 1, 0) if leftwards else jnp.minimum(1, i))

    def mid(t, carry):
        j = i - t if leftwards else t
        tile(j, False, j - step, j + step)
        return carry

    lax.fori_loop(1, i, mid, 0)

    @pl.when(i >= 1)
    def _():
        tile(last, True, last - step, last)


def _ride_along(ex, n_in, n_out, refs):
    k = 0
    parts = []
    for cnt in (n_in, ex.n, n_out, ex.n):
        parts.append(refs[k:k + cnt])
        k += cnt
    n_sems = len(ex.scratch)
    return (*parts, refs[k:len(refs) - n_sems], refs[len(refs) - n_sems:])


def _attn_fwd(q, k, v, ex, ex_arrays):
    n_rows, width = q.shape
    nb = n_rows // BLK
    gw = HEAD_GROUP * HEAD_DIM
    groups = width // gw

    def body(*refs):
        (q_ref, k_ref, v_ref), ex_in, (o_ref, tot_ref), ex_out, (run_ref, z_ref, w_ref), sems = _ride_along(ex, 3, 2, refs)
        step = pl.program_id(0)

        @pl.when(step == 0)
        def _():
            ex.start(ex_in, ex_out, sems)

        ri = lax.broadcasted_iota(jnp.int32, (BLK, BLK), 0)
        ci = lax.broadcasted_iota(jnp.int32, (BLK, BLK), 1)
        sums = _tri_and_ones((ri > ci).astype(F32))
        first = ci < HEAD_DIM
        heads, pairs = range(HEAD_GROUP), range(HEAD_GROUP // 2)

        def stack(ref, j, p):
            return _head_stack(ref[_block_rows(j), _pair_lanes(p)], first)

        def q_block(i, carry):
            rows = _block_rows(i)
            o_ref[rows, :] = jnp.zeros((BLK, gw), F32)
            run_ref[...] = jnp.zeros_like(run_ref)
            w_ref[...] = jnp.zeros_like(w_ref)
            for p in pairs:
                z_ref[p] = _dot(q_ref[rows, _pair_lanes(p)], stack(k_ref, i, p), NT)

            def tile(j, masked, j_prev, j_next):
                valid = _tile_mask(i, j, ri, ci) if masked else None
                q_i, o_i = q_ref[rows, :], o_ref[rows, :]
                outs = [_dot(w_ref[p], stack(v_ref, j_prev, p)) for p in pairs]
                z_next = [_dot(q_i[:, _pair_lanes(p)], stack(k_ref, j_next, p), NT) for p in pairs]
                runs = [run_ref[h] for h in heads]
                zl2 = [z_ref[p] for p in pairs]
                lgs = [_sb_logits(zl2[h // 2][:, (h % 2) * BLK:(h % 2 + 1) * BLK], valid) for h in heads]
                sms = [_sums_dot(lgs[h][1], sums) for h in heads]
                ws = [jnp.exp(lgs[h][0] + sms[h][0] + runs[h]) for h in heads]
                if masked:
                    ws = [jnp.where(valid, w, 0.0) for w in ws]
                for p in pairs:
                    w_ref[p] = jnp.concatenate([ws[2 * p].astype(BF16), ws[2 * p + 1].astype(BF16)], axis=1)
                    z_ref[p] = z_next[p]
                o_ref[rows, :] = o_i + jnp.concatenate(outs, axis=1)
                for h in heads:
                    run_ref[h] = runs[h] + sms[h][1]

            _sweep(i, tile, leftwards=True)
            o_ref[rows, :] += jnp.concatenate([_dot(w_ref[p], stack(v_ref, 0, p)) for p in pairs], axis=1)
            for h in heads:
                tot_ref[rows, h:h + 1] = run_ref[h][:, 0:1]
            return carry

        lax.fori_loop(0, nb, q_block, 0)

        @pl.when(step == groups - 1)
        def _():
            ex.wait(ex_in, ex_out, sems)

    spec = pl.BlockSpec((n_rows, gw), lambda g: (0, g))
    tot_spec = pl.BlockSpec((None, n_rows, HEAD_GROUP), lambda g: (g, 0, 0))
    res = pl.pallas_call(
        body, name="attn_fwd", grid=(groups,), in_specs=[spec, spec, spec] + ex.specs,
        out_specs=[spec, tot_spec] + ex.specs,
        out_shape=[jax.ShapeDtypeStruct((n_rows, width), F32),
                   jax.ShapeDtypeStruct((groups, n_rows, HEAD_GROUP), F32)] + ex.out_shape,
        scratch_shapes=[pltpu.VMEM((HEAD_GROUP, BLK, BLK), F32), pltpu.VMEM((HEAD_GROUP // 2, BLK, 2 * BLK), F32),
                        pltpu.VMEM((HEAD_GROUP // 2, BLK, 2 * BLK), BF16)] + ex.scratch,
        compiler_params=_params(n_rows * (3 * gw * 2 + gw * 4 + LANES * 4), ("arbitrary",)),
    )(q, k, v, *ex_arrays)
    return res[0], res[1], res[2:]


def _attn_bwd(q, k, v, keep_total, do, ex, ex_arrays):
    n_rows, width = q.shape
    nb = n_rows // BLK
    gw = HEAD_GROUP * HEAD_DIM
    groups = width // gw

    def body(*refs):
        ((q_ref, k_ref, v_ref, tot_ref, do_ref), ex_in, (dq_ref, dk_ref, dv_ref), ex_out,
         (dq_acc, dk_acc, dv_acc, tot_b, run_ref, rung_ref, z_ref, dw_ref, dz_ref, wb_ref, qst_ref, dost_ref),
         sems) = _ride_along(ex, 5, 3, refs)
        step = pl.program_id(0)

        @pl.when(step == 0)
        def _():
            ex.start(ex_in, ex_out, sems)

        ri = lax.broadcasted_iota(jnp.int32, (BLK, BLK), 0)
        ci = lax.broadcasted_iota(jnp.int32, (BLK, BLK), 1)
        sums_keep = _tri_and_ones((ri <= ci).astype(F32))
        sums_g = _tri_and_ones((ri < ci).astype(F32))
        first = ci < HEAD_DIM
        heads, pairs = range(HEAD_GROUP), range(HEAD_GROUP // 2)
        dk_acc[...] = jnp.zeros_like(dk_acc)
        dv_acc[...] = jnp.zeros_like(dv_acc)

        def stack(ref, j, p):
            return _head_stack(ref[_block_rows(j), _pair_lanes(p)], first)

        def owed(j_prev):
            dq = [_dot(jnp.concatenate([dz_ref[2 * p], dz_ref[2 * p + 1]], axis=1), stack(k_ref, j_prev, p)) for p in pairs]
            dk = [_dot(jnp.concatenate([dz_ref[2 * p], dz_ref[2 * p + 1]], axis=0), qst_ref[p], TN) for p in pairs]
            dv = [_dot(jnp.concatenate([wb_ref[2 * p], wb_ref[2 * p + 1]], axis=0), dost_ref[p], TN) for p in pairs]
            return dq, dk, dv

        def settle(j_prev, parts):
            dq, dk, dv = parts
            prev = _block_rows(j_prev)
            dq_acc[...] += jnp.concatenate(dq, axis=1)
            dk_acc[prev, :] += jnp.concatenate(dk, axis=1)
            dv_acc[prev, :] += jnp.concatenate(dv, axis=1)

        def q_block(i, carry):
            rows = _block_rows(i)
            dq_acc[...] = jnp.zeros_like(dq_acc)
            run_ref[...] = jnp.zeros_like(run_ref)
            rung_ref[...] = jnp.zeros_like(rung_ref)
            dz_ref[...] = jnp.zeros_like(dz_ref)
            wb_ref[...] = jnp.zeros_like(wb_ref)
            for h in heads:
                tot_b[h] = jnp.broadcast_to(tot_ref[rows, h:h + 1], (BLK, BLK))
            for p in pairs:
                qst_ref[p] = _head_stack(q_ref[rows, _pair_lanes(p)], first)
                dost_ref[p] = _head_stack(do_ref[rows, _pair_lanes(p)], first)
                z_ref[p] = _dot(q_ref[rows, _pair_lanes(p)], stack(k_ref, 0, p), NT)
                dw_ref[p] = _dot(do_ref[rows, _pair_lanes(p)], stack(v_ref, 0, p), NT)

            def tile(j, masked, j_prev, j_next):
                valid = _tile_mask(i, j, ri, ci) if masked else None
                half = lambda a, h: a[h // 2][:, (h % 2) * BLK:(h % 2 + 1) * BLK]
                q_i, do_i = q_ref[rows, :], do_ref[rows, :]
                parts = owed(j_prev)
                z_next = [_dot(q_i[:, _pair_lanes(p)], stack(k_ref, j_next, p), NT) for p in pairs]
                dw_next = [_dot(do_i[:, _pair_lanes(p)], stack(v_ref, j_next, p), NT) for p in pairs]
                zl2 = [z_ref[p] for p in pairs]
                dw2 = [dw_ref[p] for p in pairs]
                lgs = [_sb_logits(half(zl2, h), valid) for h in heads]
                sms = [_sums_dot(lgs[h][1], sums_keep) for h in heads]
                ws = [jnp.exp(lgs[h][0] + (tot_b[h] - run_ref[h] - sms[h][0])) for h in heads]
                if masked:
                    ws = [jnp.where(valid, w, 0.0) for w in ws]
                gs = [half(dw2, h) * ws[h] for h in heads]
                gsm = [_sums_dot(gs[h], sums_g) for h in heads]
                dzs = []
                for h in heads:
                    beta = jnp.exp(lgs[h][0])
                    dz = gs[h] * (1.0 - beta) - beta * (gsm[h][0] + rung_ref[h])
                    dzs.append(jnp.where(valid, dz, 0.0) if masked else dz)
                settle(j_prev, parts)
                for h in heads:
                    dz_ref[h] = dzs[h].astype(BF16)
                    wb_ref[h] = ws[h].astype(BF16)
                    run_ref[h] += sms[h][1]
                    rung_ref[h] += gsm[h][1]
                for p in pairs:
                    z_ref[p] = z_next[p]
                    dw_ref[p] = dw_next[p]

            _sweep(i, tile, leftwards=False)
            settle(i, owed(i))
            dq_ref[rows, :] = (dq_acc[...] * SB_SCALE).astype(dq_ref.dtype)
            return carry

        lax.fori_loop(0, nb, q_block, 0)
        dk_ref[...] = dk_acc[...].astype(dk_ref.dtype)
        dv_ref[...] = dv_acc[...].astype(dv_ref.dtype)

        @pl.when(step == groups - 1)
        def _():
            ex.wait(ex_in, ex_out, sems)

    spec = pl.BlockSpec((n_rows, gw), lambda g: (0, g))
    tot_spec = pl.BlockSpec((None, n_rows, HEAD_GROUP), lambda g: (g, 0, 0))
    out = jax.ShapeDtypeStruct((n_rows, width), BF16)
    tile_f32 = pltpu.VMEM((HEAD_GROUP, BLK, BLK), F32)
    tile_bf16 = pltpu.VMEM((HEAD_GROUP, BLK, BLK), BF16)
    pair_f32 = pltpu.VMEM((HEAD_GROUP // 2, BLK, 2 * BLK), F32)
    pair_stack = pltpu.VMEM((HEAD_GROUP // 2, 2 * BLK, BLK), BF16)
    res = pl.pallas_call(
        body, name="attn_bwd", grid=(groups,), in_specs=[spec, spec, spec, tot_spec, spec] + ex.specs,
        out_specs=[spec] * 3 + ex.specs, out_shape=[out] * 3 + ex.out_shape,
        scratch_shapes=[pltpu.VMEM((BLK, gw), F32), pltpu.VMEM((n_rows, gw), F32), pltpu.VMEM((n_rows, gw), F32),
                        tile_f32, tile_f32, tile_f32, pair_f32, pair_f32, tile_bf16, tile_bf16, pair_stack,
                        pair_stack] + ex.scratch,
        compiler_params=_params(n_rows * (7 * gw * 2 + LANES * 4 + gw * 4), ("arbitrary",)),
    )(q, k, v, keep_total, do, *ex_arrays)
    return res[0], res[1], res[2], res[3:]


class _Exchange:
    def __init__(self, arrays, gather):
        self.n = len(arrays)
        self.gather = gather
        self.out_shape = [jax.ShapeDtypeStruct(((N_DEV,) + a.shape) if gather else a.shape, a.dtype) for a in arrays]
        self.scratch = [pltpu.SemaphoreType.DMA((self.n, N_DEV - 1)), pltpu.SemaphoreType.DMA((self.n, N_DEV - 1)),
                        pltpu.SemaphoreType.DMA((self.n,))]
        self.specs = [pl.BlockSpec(memory_space=pl.ANY)] * self.n

    def _copies(self, ins, outs, sems):
        send_sems, recv_sems, local_sems = sems
        x, y, c = lax.axis_index("x"), lax.axis_index("y"), lax.axis_index("c")
        me = 4 * x + 2 * y + c
        gather = self.gather
        local, sends, recvs = [], [], []
        for a in range(self.n):
            local.append(pltpu.make_async_copy(ins[a] if gather else ins[a].at[me], outs[a].at[me if gather else 0],
                                               local_sems.at[a]))
        for r in range(1, N_DEV):
            px = 1 - x if r & 4 else x
            py = 1 - y if r & 2 else y
            pc = 1 - c if r & 1 else c
            idx = 4 * px + 2 * py + pc
            for a in range(self.n):
                src = ins[a] if gather else ins[a].at[idx]
                pair = dict(send_sem=send_sems.at[a, r - 1], recv_sem=recv_sems.at[a, r - 1],
                            device_id=(px, py, pc), device_id_type=pl.DeviceIdType.MESH)
                sends.append(pltpu.make_async_remote_copy(src_ref=src, dst_ref=outs[a].at[me if gather else r], **pair))
                recvs.append(pltpu.make_async_remote_copy(src_ref=src, dst_ref=outs[a].at[idx if gather else r], **pair))
        return local, sends, recvs

    def start(self, ins, outs, sems):
        local, sends, _ = self._copies(ins, outs, sems)
        for cp in local + sends:
            cp.start()

    def wait(self, ins, outs, sems):
        local, sends, recvs = self._copies(ins, outs, sems)
        for cp in recvs:
            cp.wait_recv()
        for cp in sends:
            cp.wait_send()
        for cp in local:
            cp.wait()


def _peer_exchange(name, arrays, gather):
    ex = _Exchange(arrays, gather)
    n = ex.n

    def body(*refs):
        ins, outs, sems = refs[:n], refs[n:2 * n], refs[2 * n:]
        ex.start(ins, outs, sems)
        ex.wait(ins, outs, sems)

    return pl.pallas_call(body, name=name, in_specs=ex.specs, out_specs=ex.specs, out_shape=ex.out_shape,
                          scratch_shapes=ex.scratch)(*arrays)


def _sum_slots(name, x):
    def body(x_ref, o_ref):
        acc = x_ref[0]
        for s in range(1, N_DEV):
            acc = acc + x_ref[s]
        o_ref[...] = acc
    return pl.pallas_call(body, name=name, out_shape=jax.ShapeDtypeStruct(x.shape[1:], F32))(x)


def _adamw(name, w, slots, m, v):
    n_slots, rows, cols = slots.shape
    tr = next((t for t in (256, 176, 128) if rows % t == 0 and rows > t), rows)

    def body(w_ref, s_ref, m_ref, v_ref, g_ref, d_ref, nm_ref, nv_ref):
        g = s_ref[0].astype(F32)
        for s in range(1, n_slots):
            g = g + s_ref[s].astype(F32)
        nm = ADAM_B1 * m_ref[...] + (1.0 - ADAM_B1) * g
        nv = ADAM_B2 * v_ref[...] + (1.0 - ADAM_B2) * (g * g)
        m_hat = nm / (1.0 - ADAM_B1 ** ADAM_STEP)
        v_hat = nv / (1.0 - ADAM_B2 ** ADAM_STEP)
        g_ref[...] = g
        d_ref[...] = -ADAM_LR * (m_hat / (jnp.sqrt(v_hat) + ADAM_EPS) + ADAM_WD * w_ref[...])
        nm_ref[...] = nm
        nv_ref[...] = nv

    spec = pl.BlockSpec((tr, cols), lambda i: (i, 0))
    out = jax.ShapeDtypeStruct((rows, cols), F32)
    return pl.pallas_call(
        body, name=name, grid=(rows // tr,),
        in_specs=[spec, pl.BlockSpec((n_slots, tr, cols), lambda i: (0, i, 0)), spec, spec],
        out_specs=[spec] * 4, out_shape=[out] * 4,
        compiler_params=_params((n_slots + 7) * tr * cols * 4, ("parallel",)),
    )(w, slots, m, v)


def _pad_lanes(a):
    return jnp.pad(a, ((0, 0), (0, LANES - a.shape[1])))


def kernel(x, meta_tokens, mix_pre_g, w_in, ssd_conv_w, ssd_conv_b, ssd_dt_bias, ssd_a_log, ssd_d, ssd_norm_g, sb_norm_g, w_out, mix_post_g, ffn_pre_g, w_up, ffn_conv_w, ffn_conv_b, w_down, ffn_post_g, loss_target, m_meta_tokens, m_mix_pre_g, m_w_in, m_ssd_conv_w, m_ssd_conv_b, m_ssd_dt_bias, m_ssd_a_log, m_ssd_d, m_ssd_norm_g, m_sb_norm_g, m_w_out, m_mix_post_g, m_ffn_pre_g, m_w_up, m_ffn_conv_w, m_ffn_conv_b, m_w_down, m_ffn_post_g, v_meta_tokens, v_mix_pre_g, v_w_in, v_ssd_conv_w, v_ssd_conv_b, v_ssd_dt_bias, v_ssd_a_log, v_ssd_d, v_ssd_norm_g, v_sb_norm_g, v_w_out, v_mix_post_g, v_ffn_pre_g, v_w_up, v_ffn_conv_w, v_ffn_conv_b, v_w_down, v_ffn_post_g):
    seq = x.shape[1]
    me = 4 * lax.axis_index("x") + 2 * lax.axis_index("y") + lax.axis_index("c")
    in_cols = w_in.shape[2]
    up_cols = w_up.shape[2]
    out_rows = w_out.shape[1]
    down_rows = w_down.shape[1]

    g_in, g_meta, g_scw, g_fcw = _peer_exchange(
        "gather_w_in", [w_in[0].astype(BF16), meta_tokens, ssd_conv_w[0], ffn_conv_w[0]], gather=True)
    late_weights = [w_out[0].astype(BF16), w_up[0].astype(BF16), w_down[0].astype(BF16)]
    w_in_full = g_in.transpose(1, 0, 2).reshape(D_MODEL, N_DEV * in_cols)
    off = [0, SSD_INNER, SSD_INNER + XBC, SSD_INNER + XBC + HEADS]
    w_z = w_in_full[:, off[0]:off[1]]
    w_xbc = w_in_full[:, off[1]:off[2]]
    w_dt = _pad_lanes(w_in_full[:, off[2]:off[3]])
    w_q = w_in_full[:, off[3]:off[3] + SSD_INNER]
    w_k = w_in_full[:, off[3] + SSD_INNER:off[3] + 2 * SSD_INNER]
    w_v = w_in_full[:, off[3] + 2 * SSD_INNER:off[3] + 3 * SSD_INNER]
    meta_full = g_meta.transpose(1, 0, 2).reshape(N_META, D_MODEL)
    scw_full = g_scw.transpose(1, 0, 2).reshape(SSD_CONV, XBC)
    fcw_full = g_fcw.transpose(1, 0, 2).reshape(FFN_CONV, D_FF)

    dt_bias_p, a_log_p, d_p = _pad_lanes(ssd_dt_bias), _pad_lanes(ssd_a_log), _pad_lanes(ssd_d)

    h0 = jnp.concatenate([jnp.zeros((PAD, D_MODEL), F32), meta_full, x[0]], axis=0)
    target = jnp.concatenate([jnp.zeros((BLK, D_MODEL), F32), loss_target[0]], axis=0)
    xn1 = _rms_fwd("rms_pre_mix", h0, mix_pre_g)
    z = _mm("proj_z", [(xn1, w_z)], "nn", F32)
    xbc_raw = _mm("proj_xbc", [(xn1, w_xbc)], "nn", F32)
    dtr = _mm("proj_dt", [(xn1, w_dt)], "nn", F32)
    q = _mm("proj_q", [(xn1, w_q * SB_SCALE)], "nn", BF16)
    k = _mm("proj_k", [(xn1, w_k)], "nn", BF16)
    v = _mm("proj_v", [(xn1, w_v)], "nn", BF16)
    xbc_act = _ssd_conv_fwd(xbc_raw, scw_full, ssd_conv_b)
    ypre, y_ssd, states = _ssd_fwd(xbc_act, dtr, z, dt_bias_p, a_log_p, d_p, ssd_norm_g)
    o, keep_total, (g_out, g_up, g_down) = _attn_fwd(q, k, v, _Exchange(late_weights, gather=True), late_weights)
    w_out_full = g_out.reshape(N_DEV * out_rows, D_MODEL)
    wo_ssd, wo_sb = w_out_full[:SSD_INNER], w_out_full[SSD_INNER:]
    w_up_full = g_up.transpose(1, 0, 2).reshape(D_MODEL, N_DEV * up_cols)
    w_gate, w_lin = w_up_full[:, :D_FF], w_up_full[:, D_FF:]
    w_down_full = g_down.reshape(N_DEV * down_rows, D_MODEL)
    y_sb = _rms_fwd("rms_sb", o, sb_norm_g)
    mix = _mm("mix_out", [(y_ssd, wo_ssd), (y_sb, wo_sb)], "nn", F32)
    h1, xn2 = _mix_post(mix, h0, mix_post_g, ffn_pre_g)
    g_raw = _mm("ffn_gate", [(xn2, w_gate)], "nn", F32)
    u = _mm("ffn_lin", [(xn2, w_lin)], "nn", F32)
    act = _ffn_act(g_raw, u, fcw_full, ffn_conv_b)
    f = _mm("ffn_down", [(act, w_down_full)], "nn", F32)
    dh2, df, loss_row, dg_ffn_post = _loss_post(f, h1, target, ffn_post_g)
    loss = lax.psum(loss_row[0, 0], ("x", "y", "c"))

    dact = _mm("d_act", [(df, w_down_full)], "nt", F32)
    dw_down = _mm("dw_down", [(act, df)], "tn", F32)
    dg_conv, du = _ffn_bwd_act(dact, u, g_raw, fcw_full, ffn_conv_b)
    dg_raw, dfcw0, dfcw1, dfcw2, dfcb = _conv_bwd("ffn_conv_bwd", dg_conv, g_raw, fcw_full, FFN_CONV)
    dxn2 = _mm("d_xn2", [(dg_raw, w_gate), (du, w_lin)], "nt", F32)
    dw_gate = _mm("dw_gate", [(xn2, dg_raw)], "tn", F32)
    dw_lin = _mm("dw_lin", [(xn2, du)], "tn", F32)
    dh1, dmix, dg_ffn_pre, dg_mix_post = _mid_bwd(dxn2, h1, dh2, mix, ffn_pre_g, mix_post_g)

    dy_ssd = _mm("d_yssd", [(dmix, wo_ssd)], "nt", F32)
    dy_sb = _mm("d_ysb", [(dmix, wo_sb)], "nt", F32)
    dwo_ssd = _mm("dw_out_ssd", [(y_ssd, dmix)], "tn", F32)
    dwo_sb = _mm("dw_out_sb", [(y_sb, dmix)], "tn", F32)
    do, dg_sb = _norm_bwd("sb_norm_bwd", dy_sb, o, sb_norm_g)
    half = N_DEV // 2
    early_slabs = [
        jnp.concatenate([dwo_ssd, dwo_sb], axis=0).reshape(N_DEV, out_rows, D_MODEL),
        jnp.concatenate([dw_gate.reshape(D_MODEL, half, up_cols).transpose(1, 0, 2),
                         dw_lin.reshape(D_MODEL, half, up_cols).transpose(1, 0, 2)], axis=0),
        dw_down.reshape(N_DEV, down_rows, D_MODEL)]
    dq, dk, dv, (l_out, l_up, l_down) = _attn_bwd(q, k, v, keep_total, do, _Exchange(early_slabs, gather=False), early_slabs)
    dz, dxbc_act, ddtr, dg_ssd_norm, dd_skip, da_log, ddt_bias = _ssd_bwd(
        dy_ssd, ypre, z, xbc_act, dtr, states, dt_bias_p, a_log_p, d_p, ssd_norm_g)
    dconv = _ssd_conv_bwd_act(dxbc_act, xbc_raw, scw_full, ssd_conv_b)
    dxbc_raw, dscw0, dscw1, dscw2, dscw3, dscb = _conv_bwd("ssd_conv_bwd", dconv, xbc_raw, scw_full, SSD_CONV)
    segs = [(dz, w_z), (dxbc_raw, w_xbc), (ddtr, w_dt), (dq, w_q), (dk, w_k), (dv, w_v)]
    dxn1 = _mm("d_xn1", segs, "nt", F32)
    dw_segs = [_mm("dw_in_%d" % s, [(xn1, d)], "tn", BF16) for s, (d, _) in enumerate(segs)]
    dw_segs[2] = dw_segs[2][:, :HEADS]
    dw_in = jnp.concatenate(dw_segs, axis=1)
    dh0, dg_mix_pre = _first_bwd(dxn1, h0, dh1, mix_pre_g)
    grad_x = dh0[BLK:][None]

    slab_in = dw_in.reshape(D_MODEL, N_DEV, in_cols).transpose(1, 0, 2)
    (l_in,) = _peer_exchange("scatter_dw_in", [slab_in], gather=False)

    small = [dg_mix_pre, dscb, ddt_bias, da_log, dd_skip, dg_ssd_norm, dg_sb, dg_mix_post, dg_ffn_pre, dfcb,
             dg_ffn_post, dh0[PAD:BLK].reshape(1, -1), dscw0, dscw1, dscw2, dscw3, dfcw0, dfcw1, dfcw2]
    sizes = [a.shape[1] for a in small]
    total = sum(sizes)
    rows_packed = -(-total // (LANES * HALO)) * HALO
    packed = jnp.pad(jnp.concatenate(small, axis=1), ((0, 0), (0, rows_packed * LANES - total)))
    (gathered,) = _peer_exchange("gather_small_grads", [packed.reshape(rows_packed, LANES)], gather=True)
    summed = _sum_slots("sum_small_grads", gathered).reshape(1, rows_packed * LANES)
    pieces, at = [], 0
    for s in sizes:
        pieces.append(summed[:, at:at + s])
        at += s
    (g_mix_pre, g_scb, g_dtb, g_alog, g_dskip, g_ssd_norm, g_sb, g_mix_post, g_ffn_pre, g_fcb, g_ffn_post,
     g_meta_flat, gs0, gs1, gs2, gs3, gf0, gf1, gf2) = pieces
    g_dtb, g_alog, g_dskip = g_dtb[:, :HEADS], g_alog[:, :HEADS], g_dskip[:, :HEADS]
    g_meta_full = g_meta_flat.reshape(N_META, D_MODEL)
    g_scw_full = jnp.concatenate([gs0, gs1, gs2, gs3], axis=0)
    g_fcw_full = jnp.concatenate([gf0, gf1, gf2], axis=0)
    meta_cols, scw_cols, fcw_cols = meta_tokens.shape[1], ssd_conv_w.shape[2], ffn_conv_w.shape[2]
    g_meta_mine = lax.dynamic_slice(g_meta_full, (0, me * meta_cols), (N_META, meta_cols))
    g_scw_mine = lax.dynamic_slice(g_scw_full, (0, me * scw_cols), (SSD_CONV, scw_cols))
    g_fcw_mine = lax.dynamic_slice(g_fcw_full, (0, me * fcw_cols), (FFN_CONV, fcw_cols))

    def lead(a):
        return a[None]

    upd = [
        _adamw("adamw_meta", meta_tokens, lead(g_meta_mine), m_meta_tokens, v_meta_tokens),
        _adamw("adamw_mix_pre_g", mix_pre_g, lead(g_mix_pre), m_mix_pre_g, v_mix_pre_g),
        [lead(a) for a in _adamw("adamw_w_in", w_in[0], l_in, m_w_in[0], v_w_in[0])],
        [lead(a) for a in _adamw("adamw_ssd_conv_w", ssd_conv_w[0], lead(g_scw_mine), m_ssd_conv_w[0], v_ssd_conv_w[0])],
        _adamw("adamw_ssd_conv_b", ssd_conv_b, lead(g_scb), m_ssd_conv_b, v_ssd_conv_b),
        _adamw("adamw_ssd_dt_bias", ssd_dt_bias, lead(g_dtb), m_ssd_dt_bias, v_ssd_dt_bias),
        _adamw("adamw_ssd_a_log", ssd_a_log, lead(g_alog), m_ssd_a_log, v_ssd_a_log),
        _adamw("adamw_ssd_d", ssd_d, lead(g_dskip), m_ssd_d, v_ssd_d),
        _adamw("adamw_ssd_norm_g", ssd_norm_g, lead(g_ssd_norm), m_ssd_norm_g, v_ssd_norm_g),
        _adamw("adamw_sb_norm_g", sb_norm_g, lead(g_sb), m_sb_norm_g, v_sb_norm_g),
        [lead(a) for a in _adamw("adamw_w_out", w_out[0], l_out, m_w_out[0], v_w_out[0])],
        _adamw("adamw_mix_post_g", mix_post_g, lead(g_mix_post), m_mix_post_g, v_mix_post_g),
        _adamw("adamw_ffn_pre_g", ffn_pre_g, lead(g_ffn_pre), m_ffn_pre_g, v_ffn_pre_g),
        [lead(a) for a in _adamw("adamw_w_up", w_up[0], l_up, m_w_up[0], v_w_up[0])],
        [lead(a) for a in _adamw("adamw_ffn_conv_w", ffn_conv_w[0], lead(g_fcw_mine), m_ffn_conv_w[0], v_ffn_conv_w[0])],
        _adamw("adamw_ffn_conv_b", ffn_conv_b, lead(g_fcb), m_ffn_conv_b, v_ffn_conv_b),
        [lead(a) for a in _adamw("adamw_w_down", w_down[0], l_down, m_w_down[0], v_w_down[0])],
        _adamw("adamw_ffn_post_g", ffn_post_g, lead(g_ffn_post), m_ffn_post_g, v_ffn_post_g),
    ]
    grads = [u_[0] for u_ in upd]
    deltas = [u_[1] for u_ in upd]
    new_m = [u_[2] for u_ in upd]
    new_v = [u_[3] for u_ in upd]
    return (loss, grad_x, *grads, *deltas, *new_m, *new_v)
```

```python
import math

import jax
import jax.numpy as jnp
from jax import lax
from jax.experimental import pallas as pl
from jax.experimental.pallas import tpu as pltpu

F32 = jnp.float32
BF16 = jnp.bfloat16
HI = lax.Precision.HIGHEST

D_MODEL = 1024
N_META = 16
BLK = 128
PAD = BLK - N_META
HEADS = 16
HEAD_DIM = 64
SSD_GROUPS = 2
SSD_STATE = 128
HEADS_PER_GROUP = HEADS // SSD_GROUPS
SSD_INNER = HEADS * HEAD_DIM
SSD_CONV = 4
XBC = SSD_INNER + 2 * SSD_GROUPS * SSD_STATE
OFF_B = SSD_INNER
OFF_C = SSD_INNER + SSD_GROUPS * SSD_STATE
D_FF = 2816
FFN_CONV = 3
EPS = 1e-6
SB_SCALE = 1.0 / math.sqrt(HEAD_DIM)
N_DEV = 8
LANES = 128
HALO = 8

ADAM_LR = 0.001
ADAM_B1 = 0.9
ADAM_B2 = 0.999
ADAM_EPS = 1e-08
ADAM_WD = 0.01
ADAM_STEP = 10

VMEM_FLOOR = 32 << 20
VMEM_CEIL = 60 << 20
MM_BUDGET = 20 << 20
ROW_BUDGET = 6 << 20

NN = (((1,), (0,)), ((), ()))
NT = (((1,), (1,)), ((), ()))
TN = (((0,), (0,)), ((), ()))


def _params(tile_bytes, sem=None):
    limit = int(min(max(2 * tile_bytes + (8 << 20), VMEM_FLOOR), VMEM_CEIL))
    return pltpu.CompilerParams(vmem_limit_bytes=limit, dimension_semantics=sem)


def _nbytes(shape, dtype):
    n = 1
    for s in shape:
        n *= s
    return n * jnp.dtype(dtype).itemsize


def _dot(a, b, dims=NN, precision=None):
    return lax.dot_general(a, b, dims, precision=precision, preferred_element_type=F32)


def _softplus(x):
    return jnp.maximum(x, 0.0) + jnp.log1p(jnp.exp(-jnp.abs(x)))


def _rms(x, g):
    r = lax.rsqrt(jnp.mean(x * x, axis=-1, keepdims=True) + EPS)
    return x * r * g


def _rms_bwd(dy, x, g):
    r = lax.rsqrt(jnp.mean(x * x, axis=-1, keepdims=True) + EPS)
    xh = x * r
    u = dy * g
    dx = r * (u - xh * jnp.mean(xh * u, axis=-1, keepdims=True))
    return dx, jnp.sum(dy * xh, axis=0, keepdims=True)


def _gelu(x):
    c = math.sqrt(2.0 / math.pi)
    return 0.5 * x * (1.0 + jnp.tanh(c * (x + 0.044715 * x * x * x)))


def _gelu_grad(x):
    c = math.sqrt(2.0 / math.pi)
    t = jnp.tanh(c * (x + 0.044715 * x * x * x))
    return 0.5 * (1.0 + t) + 0.5 * x * (1.0 - t * t) * c * (1.0 + 3.0 * 0.044715 * x * x)


def _row_tile(rows, bytes_per_row):
    big = 384
    return big if rows % big == 0 and big * bytes_per_row <= ROW_BUDGET else BLK


def _mm(name, pairs, mode, out_dtype):
    a0, b0 = pairs[0]
    if mode == "tn":
        m, n = a0.shape[1], b0.shape[1]
    elif mode == "nt":
        m, n = a0.shape[0], b0.shape[0]
    else:
        m, n = a0.shape[0], b0.shape[1]
    dims = {"nn": NN, "nt": NT, "tn": TN}[mode]

    def tile_bytes(tm, tn):
        tot = tm * tn * jnp.dtype(out_dtype).itemsize
        for a, b in pairs:
            k = a.shape[0] if mode == "tn" else a.shape[1]
            tot += tm * k * a.dtype.itemsize + tn * k * b.dtype.itemsize
        return tot

    cands_m = [t for t in (1408, 1024, 512, 384, 256, 128) if m % t == 0] or [m]
    cands_n = [t for t in (1408, 1024, 768, 512, 256, 128) if n % t == 0] or [n]
    best = None
    for tm in cands_m:
        for tn in cands_n:
            if tile_bytes(tm, tn) <= MM_BUDGET and (best is None or tm * tn > best[0] * best[1]):
                best = (tm, tn)
    tm, tn = best if best is not None else (cands_m[-1], cands_n[-1])
    npairs = len(pairs)

    def body(*refs):
        o_ref = refs[2 * npairs]
        acc = None
        for p in range(npairs):
            part = _dot(refs[2 * p][...], refs[2 * p + 1][...], dims)
            acc = part if acc is None else acc + part
        o_ref[...] = acc.astype(o_ref.dtype)

    in_specs, args = [], []
    for a, b in pairs:
        if mode == "tn":
            k = a.shape[0]
            in_specs += [pl.BlockSpec((k, tm), lambda i, j: (0, i)), pl.BlockSpec((k, tn), lambda i, j: (0, j))]
        elif mode == "nt":
            k = a.shape[1]
            in_specs += [pl.BlockSpec((tm, k), lambda i, j: (i, 0)), pl.BlockSpec((tn, k), lambda i, j: (j, 0))]
        else:
            k = a.shape[1]
            in_specs += [pl.BlockSpec((tm, k), lambda i, j: (i, 0)), pl.BlockSpec((k, tn), lambda i, j: (0, j))]
        args += [a, b]
    return pl.pallas_call(
        body, name=name, grid=(m // tm, n // tn), in_specs=in_specs,
        out_specs=pl.BlockSpec((tm, tn), lambda i, j: (i, j)),
        out_shape=jax.ShapeDtypeStruct((m, n), out_dtype),
        compiler_params=_params(tile_bytes(tm, tn), ("parallel", "parallel")),
    )(*args)


def _rowcall(name, fn, rows=(), prevs=(), nexts=(), pars=(), out_rows=(), out_accs=()):
    rows, prevs, nexts, pars = list(rows), list(prevs), list(nexts), list(pars)
    n_rows = (rows + prevs + nexts)[0].shape[0]
    per_row = sum(a.shape[1] * a.dtype.itemsize for a in rows + prevs + nexts)
    per_row += sum(c * jnp.dtype(dt).itemsize for c, dt in out_rows) + sum(a.shape[1] * 4 for a in prevs + nexts)
    tm = _row_tile(n_rows, per_row)
    nt = n_rows // tm
    hb = tm // HALO
    nr, npv, nnx, npar, nor, noa = len(rows), len(prevs), len(nexts), len(pars), len(out_rows), len(out_accs)

    def body(*refs):
        i = pl.program_id(0)
        k = 0
        row_refs = refs[k:k + nr]; k += nr
        pc = refs[k:k + npv]; k += npv
        ph = refs[k:k + npv]; k += npv
        nc = refs[k:k + nnx]; k += nnx
        nh = refs[k:k + nnx]; k += nnx
        par_refs = refs[k:k + npar]; k += npar
        orow = refs[k:k + nor]; k += nor
        oacc = refs[k:k + noa]; k += noa
        pscr = refs[k:k + npv]; k += npv
        nscr = refs[k:k + nnx]
        for c_, h_, s_ in zip(pc, ph, pscr):
            s_[0:HALO, :] = h_[...] * (i > 0).astype(F32)
            s_[HALO:HALO + tm, :] = c_[...]
        for c_, h_, s_ in zip(nc, nh, nscr):
            s_[0:tm, :] = c_[...]
            s_[tm:tm + HALO, :] = h_[...] * (i < nt - 1).astype(F32)
        prev_fns = [(lambda s, s_=s_: s_[pl.ds(HALO - s, tm), :]) for s_ in pscr]
        next_fns = [(lambda s, s_=s_: s_[pl.ds(s, tm), :]) for s_ in nscr]
        row_vals, acc_vals = fn(i, tm, row_refs, prev_fns, next_fns, par_refs)
        for r_, v in zip(orow, row_vals):
            r_[...] = v.astype(r_.dtype)
        if noa:
            @pl.when(i == 0)
            def _():
                for r_ in oacc:
                    r_[...] = jnp.zeros_like(r_)
            for r_, v in zip(oacc, acc_vals):
                r_[...] += v

    def row_spec(a):
        return pl.BlockSpec((tm, a.shape[1]), lambda i: (i, 0))

    def whole(shape):
        return pl.BlockSpec(shape, lambda i: (0, 0))

    in_specs = [row_spec(a) for a in rows]
    in_specs += [row_spec(a) for a in prevs]
    in_specs += [pl.BlockSpec((HALO, a.shape[1]), lambda i: (jnp.maximum(i * hb - 1, 0), 0)) for a in prevs]
    in_specs += [row_spec(a) for a in nexts]
    in_specs += [pl.BlockSpec((HALO, a.shape[1]), lambda i: (jnp.minimum((i + 1) * hb, n_rows // HALO - 1), 0)) for a in nexts]
    in_specs += [whole(p.shape) for p in pars]
    out_specs = [pl.BlockSpec((tm, c), lambda i: (i, 0)) for c, _ in out_rows] + [whole(s) for s in out_accs]
    out_shape = [jax.ShapeDtypeStruct((n_rows, c), dt) for c, dt in out_rows]
    out_shape += [jax.ShapeDtypeStruct(s, F32) for s in out_accs]
    scratch = [pltpu.VMEM((tm + HALO, a.shape[1]), F32) for a in prevs + nexts]
    tile = tm * per_row
    res = pl.pallas_call(
        body, name=name, grid=(nt,), in_specs=in_specs, out_specs=out_specs, out_shape=out_shape,
        scratch_shapes=scratch, compiler_params=_params(2 * tile, ("arbitrary",)),
    )(*rows, *prevs, *prevs, *nexts, *nexts, *pars)
    return res


def _row_ids(i, tm):
    return i * tm + lax.broadcasted_iota(jnp.int32, (tm, 1), 0)


def _conv_taps(prev_fn, w_ref, b_ref, taps):
    acc = b_ref[...]
    for k in range(taps):
        acc = acc + prev_fn(taps - 1 - k) * w_ref[k:k + 1, :]
    return acc


def _rms_fwd(name, h, g):
    def fn(i, tm, rows, prevs, nexts, pars):
        return [_rms(rows[0][...], pars[0][...])], []
    return _rowcall(name, fn, rows=[h], pars=[g], out_rows=[(h.shape[1], BF16)])[0]


def _ssd_conv_fwd(xbc_raw, w, b):
    def fn(i, tm, rows, prevs, nexts, pars):
        c = _conv_taps(prevs[0], pars[0], pars[1], SSD_CONV)
        y = c * jax.nn.sigmoid(c)
        return [jnp.where(_row_ids(i, tm) >= PAD, y, 0.0)], []
    return _rowcall("ssd_conv_fwd", fn, prevs=[xbc_raw], pars=[w, b], out_rows=[(XBC, F32)])[0]


def _mix_post(mix, h0, g_post, g_pre):
    def fn(i, tm, rows, prevs, nexts, pars):
        h1 = rows[1][...] + _rms(rows[0][...], pars[0][...])
        return [h1, _rms(h1, pars[1][...])], []
    return _rowcall("mix_post", fn, rows=[mix, h0], pars=[g_post, g_pre],
                    out_rows=[(D_MODEL, F32), (D_MODEL, BF16)])


def _ffn_act(g_raw, u, w, b):
    def fn(i, tm, rows, prevs, nexts, pars):
        g = _conv_taps(prevs[0], pars[0], pars[1], FFN_CONV)
        return [_gelu(g) * rows[0][...]], []
    return _rowcall("ffn_act", fn, rows=[u], prevs=[g_raw], pars=[w, b], out_rows=[(D_FF, BF16)])[0]


def _loss_post(f, h1, target, g_post):
    def fn(i, tm, rows, prevs, nexts, pars):
        fv, g = rows[0][...], pars[0][...]
        h2 = rows[1][...] + _rms(fv, g)
        real = _row_ids(i, tm) >= BLK
        diff = jnp.where(real, h2 - rows[2][...], 0.0)
        loss = 0.5 * jnp.sum(jnp.mean(diff * diff, axis=-1, keepdims=True))
        dh2 = diff * (1.0 / D_MODEL)
        df, dg = _rms_bwd(dh2, fv, g)
        return [dh2, df], [jnp.zeros((1, LANES), F32) + loss, dg]
    return _rowcall("loss_post", fn, rows=[f, h1, target], pars=[g_post],
                    out_rows=[(D_MODEL, F32), (D_MODEL, BF16)], out_accs=[(1, LANES), (1, D_MODEL)])


def _ffn_bwd_act(dact, u, g_raw, w, b):
    def fn(i, tm, rows, prevs, nexts, pars):
        g = _conv_taps(prevs[0], pars[0], pars[1], FFN_CONV)
        da = rows[0][...]
        return [da * rows[1][...] * _gelu_grad(g), da * _gelu(g)], []
    return _rowcall("ffn_bwd_act", fn, rows=[dact, u], prevs=[g_raw], pars=[w, b],
                    out_rows=[(D_FF, F32), (D_FF, BF16)])


def _conv_bwd(name, dy, x, w, taps):
    width = x.shape[1]

    def fn(i, tm, rows, prevs, nexts, pars):
        dy0 = nexts[0](0)
        dx = None
        for k in range(taps):
            term = nexts[0](taps - 1 - k) * pars[0][k:k + 1, :]
            dx = term if dx is None else dx + term
        dws = [jnp.sum(dy0 * prevs[0](taps - 1 - k), axis=0, keepdims=True) for k in range(taps)]
        return [dx], dws + [jnp.sum(dy0, axis=0, keepdims=True)]
    return _rowcall(name, fn, prevs=[x], nexts=[dy], pars=[w], out_rows=[(width, BF16)],
                    out_accs=[(1, width)] * (taps + 1))


def _mid_bwd(dxn2, h1, dh2, mix, g_pre, g_post):
    def fn(i, tm, rows, prevs, nexts, pars):
        d1, dg_pre = _rms_bwd(rows[0][...], rows[1][...], pars[0][...])
        dh1 = rows[2][...] + d1
        dmix, dg_post = _rms_bwd(dh1, rows[3][...], pars[1][...])
        return [dh1, dmix], [dg_pre, dg_post]
    return _rowcall("mid_bwd", fn, rows=[dxn2, h1, dh2, mix], pars=[g_pre, g_post],
                    out_rows=[(D_MODEL, F32), (D_MODEL, BF16)], out_accs=[(1, D_MODEL), (1, D_MODEL)])


def _norm_bwd(name, dy, x, g):
    def fn(i, tm, rows, prevs, nexts, pars):
        dx, dg = _rms_bwd(rows[0][...], rows[1][...], pars[0][...])
        return [dx], [dg]
    return _rowcall(name, fn, rows=[dy, x], pars=[g], out_rows=[(x.shape[1], BF16)], out_accs=[(1, x.shape[1])])


def _ssd_conv_bwd_act(dact, xbc_raw, w, b):
    def fn(i, tm, rows, prevs, nexts, pars):
        c = _conv_taps(prevs[0], pars[0], pars[1], SSD_CONV)
        s = jax.nn.sigmoid(c)
        dc = rows[0][...] * s * (1.0 + c * (1.0 - s))
        return [jnp.where(_row_ids(i, tm) >= PAD, dc, 0.0)], []
    return _rowcall("ssd_conv_bwd_act", fn, rows=[dact], prevs=[xbc_raw], pars=[w, b], out_rows=[(XBC, F32)])[0]


def _first_bwd(dxn1, h0, dh1, g):
    def fn(i, tm, rows, prevs, nexts, pars):
        d0, dg = _rms_bwd(rows[0][...], rows[1][...], pars[0][...])
        return [rows[2][...] + d0], [dg]
    return _rowcall("first_bwd", fn, rows=[dxn1, h0, dh1], pars=[g], out_rows=[(D_MODEL, F32)],
                    out_accs=[(1, D_MODEL)])


def _ssd_chunk_terms(c, dtr_ref, bias_ref, alog_ref):
    ri = lax.broadcasted_iota(jnp.int32, (BLK, BLK), 0)
    ci = lax.broadcasted_iota(jnp.int32, (BLK, BLK), 1)
    causal = ri >= ci
    tril = causal.astype(F32)
    triu = (ri <= ci).astype(F32)
    rowmask = ((c * BLK + lax.broadcasted_iota(jnp.int32, (BLK, 1), 0)) >= PAD).astype(F32)
    dt = _softplus(dtr_ref[...] + bias_ref[...]) * rowmask
    a_neg = -jnp.exp(alog_ref[...])
    a = dt * a_neg
    cs = _dot(tril, a, NN, HI)
    cs_t = _dot(a, triu, TN, HI)
    return causal, triu, rowmask, dt, a_neg, cs, cs_t


def _decay_matrix(causal, cs_h, cs_t_h):
    return jnp.where(causal, jnp.exp(jnp.where(causal, cs_h - cs_t_h, 0.0)), 0.0)


def _head_spread(width):
    ri = lax.broadcasted_iota(jnp.int32, (LANES, HEADS * width), 0)
    ci = lax.broadcasted_iota(jnp.int32, (LANES, HEADS * width), 1)
    return ((ri * width <= ci) & (ci < (ri + 1) * width)).astype(BF16)


def _three_terms(x):
    hi = x.astype(BF16)
    rest = x - hi.astype(F32)
    mid = rest.astype(BF16)
    lo = (rest - mid.astype(F32)).astype(BF16)
    return jnp.concatenate([hi, mid, lo], axis=1)


def _spread(x, sel):
    return _dot(_three_terms(x), jnp.concatenate([sel, sel, sel], axis=0))


def _lane_sums(y, sel):
    return _dot(_three_terms(y), jnp.concatenate([sel, sel, sel], axis=1), NT)


def _ssd_spreads(dt, cs, d_ref, sel64, sel128):
    dt64 = _spread(dt, sel64)
    cs64 = _spread(cs, sel64)
    cs128 = _spread(cs, sel128)
    d64 = _spread(jnp.broadcast_to(d_ref[...], (HALO, LANES)), sel64)[0:1, :]
    cl64 = cs64[BLK - 1:BLK, :]
    return dt64, cs128, d64, jnp.exp(cs64), jnp.exp(cl64 - cs64), jnp.exp(cl64)


def _ssd_pair_terms(p, xbc_ref, dt64, cs128, cs_t, causal):
    lanes = _pair_lanes(p)
    xs = xbc_ref[:, lanes]
    decays = [_decay_matrix(causal, cs128[:, h * LANES:(h + 1) * LANES], cs_t[h:h + 1, :]) for h in (2 * p, 2 * p + 1)]
    return xs, xs * dt64[:, lanes], decays


def _ssd_fwd(xbc, dtr, z, dt_bias, a_log, d_skip, norm_g):
    n_rows = xbc.shape[0]
    nb = n_rows // BLK

    def body(xbc_ref, dtr_ref, z_ref, bias_ref, alog_ref, d_ref, g_ref, s64_ref, s128_ref,
             ypre_ref, yssd_ref, st_ref, state):
        c = pl.program_id(0)

        @pl.when(c == 0)
        def _():
            state[...] = jnp.zeros_like(state)

        st_ref[...] = state[...]
        causal, _, _, dt, _, cs, cs_t = _ssd_chunk_terms(c, dtr_ref, bias_ref, alog_ref)
        first = lax.broadcasted_iota(jnp.int32, (BLK, LANES), 1) < HEAD_DIM
        dt64, cs128, d64, from_start, to_end, chunk_decay = _ssd_spreads(dt, cs, d_ref, s64_ref[...], s128_ref[...])
        for g in range(SSD_GROUPS):
            b_b = xbc_ref[:, OFF_B + g * SSD_STATE:OFF_B + (g + 1) * SSD_STATE].astype(BF16)
            c_b = xbc_ref[:, OFF_C + g * SSD_STATE:OFF_C + (g + 1) * SSD_STATE].astype(BF16)
            cb = _dot(c_b, b_b, NT)
            for j in range(HEADS_PER_GROUP // 2):
                p = g * (HEADS_PER_GROUP // 2) + j
                lanes = _pair_lanes(p)
                xs, x_dt, decays = _ssd_pair_terms(p, xbc_ref, dt64, cs128, cs_t, causal)
                ms = [(cb * d).astype(BF16) for d in decays]
                s_p = state[:, lanes]
                y = _dot(jnp.concatenate(ms, axis=1), _head_stack(x_dt.astype(BF16), first))
                y = y + from_start[:, lanes] * _dot(c_b, s_p.astype(BF16))
                state[:, lanes] = chunk_decay[:, lanes] * s_p + _dot(b_b, (to_end[:, lanes] * x_dt).astype(BF16), TN)
                ypre_ref[:, lanes] = y + d64[:, lanes] * xs
        zz = z_ref[...]
        yg = ypre_ref[...] * (zz * jax.nn.sigmoid(zz))
        yssd_ref[...] = _rms(yg, g_ref[...]).astype(yssd_ref.dtype)

    blk = lambda w: pl.BlockSpec((BLK, w), lambda c: (c, 0))
    par = lambda a: pl.BlockSpec(a.shape, lambda c: (0, 0))
    sel64, sel128 = _head_spread(HEAD_DIM), _head_spread(LANES)
    return pl.pallas_call(
        body, name="ssd_fwd", grid=(nb,),
        in_specs=[blk(XBC), blk(LANES), blk(SSD_INNER), par(dt_bias), par(a_log), par(d_skip), par(norm_g),
                  par(sel64), par(sel128)],
        out_specs=[blk(SSD_INNER), blk(SSD_INNER), pl.BlockSpec((None, SSD_STATE, SSD_INNER), lambda c: (c, 0, 0))],
        out_shape=[jax.ShapeDtypeStruct((n_rows, SSD_INNER), F32), jax.ShapeDtypeStruct((n_rows, SSD_INNER), BF16),
                   jax.ShapeDtypeStruct((nb, SSD_STATE, SSD_INNER), F32)],
        scratch_shapes=[pltpu.VMEM((SSD_STATE, SSD_INNER), F32)],
        compiler_params=_params(8 << 20, ("arbitrary",)),
    )(xbc, dtr, z, dt_bias, a_log, d_skip, norm_g, sel64, sel128)


def _ssd_bwd(dy, ypre, z, xbc, dtr, states, dt_bias, a_log, d_skip, norm_g):
    n_rows = xbc.shape[0]
    nb = n_rows // BLK

    def body(dy_ref, ypre_ref, z_ref, xbc_ref, dtr_ref, st_ref, bias_ref, alog_ref, d_ref, g_ref, s64_ref, s128_ref,
             dz_ref, dxbc_ref, ddtr_ref, dgn_ref, dd_ref, dal_ref, ddtb_ref, dstate, dyp, red):
        step = pl.program_id(0)
        c = nb - 1 - step

        @pl.when(step == 0)
        def _():
            dstate[...] = jnp.zeros_like(dstate)
            for r_ in (dgn_ref, dd_ref, dal_ref, ddtb_ref):
                r_[...] = jnp.zeros_like(r_)

        yp, zz = ypre_ref[...], z_ref[...]
        sz = jax.nn.sigmoid(zz)
        silu = zz * sz
        dyg, dgn = _rms_bwd(dy_ref[...], yp * silu, g_ref[...])
        dgn_ref[...] += dgn
        dz_ref[...] = (dyg * yp * (sz * (1.0 + zz * (1.0 - sz)))).astype(dz_ref.dtype)
        dyp[...] = dyg * silu

        causal, triu, rowmask, dt, a_neg, cs, cs_t = _ssd_chunk_terms(c, dtr_ref, bias_ref, alog_ref)
        lane = lax.broadcasted_iota(jnp.int32, (1, LANES), 1)
        last_row = (lax.broadcasted_iota(jnp.int32, (BLK, 1), 0) == BLK - 1).astype(F32)
        first = lax.broadcasted_iota(jnp.int32, (BLK, LANES), 1) < HEAD_DIM
        sel64 = s64_ref[...]
        dt64, cs128, d64, from_start, to_end, chunk_decay = _ssd_spreads(dt, cs, d_ref, sel64, s128_ref[...])
        for g in range(SSD_GROUPS):
            b_b = xbc_ref[:, OFF_B + g * SSD_STATE:OFF_B + (g + 1) * SSD_STATE].astype(BF16)
            c_b = xbc_ref[:, OFF_C + g * SSD_STATE:OFF_C + (g + 1) * SSD_STATE].astype(BF16)
            cb = _dot(c_b, b_b, NT)
            b_twice = jnp.concatenate([b_b, b_b], axis=0)
            c_twice = jnp.concatenate([c_b, c_b], axis=0)
            db_g = jnp.zeros((BLK, SSD_STATE), F32)
            dc_g = jnp.zeros((BLK, SSD_STATE), F32)
            for j in range(HEADS_PER_GROUP // 2):
                p = g * (HEADS_PER_GROUP // 2) + j
                lanes = _pair_lanes(p)
                xs, x_dt, decays = _ssd_pair_terms(p, xbc_ref, dt64, cs128, cs_t, causal)
                ms = [(cb * d).astype(BF16) for d in decays]
                d_y = dyp[:, lanes]
                s_p, ds_p = st_ref[:, lanes], dstate[:, lanes]
                s_b, ds_b = s_p.astype(BF16), ds_p.astype(BF16)
                fs, te = from_start[:, lanes], to_end[:, lanes]
                x_b, dy_b = x_dt.astype(BF16), d_y.astype(BF16)
                x_st = _head_stack(x_b, first)
                fs_dy = (fs * d_y).astype(BF16)

                y_diag = _dot(jnp.concatenate(ms, axis=1), x_st)
                y_off = fs * _dot(c_b, s_b)
                end_part = te * _dot(b_b, ds_b)
                dx_diag = _dot(jnp.concatenate(ms, axis=0), _head_stack(dy_b, first), TN)
                d_x = dx_diag + end_part
                g2 = _dot(dy_b, x_st, NT)
                gl = [(g2[:, k * BLK:(k + 1) * BLK] * decays[k]).astype(BF16) for k in range(2)]
                dc_g = dc_g + _dot(jnp.concatenate(gl, axis=1), b_twice) + _dot(fs_dy, s_b, NT)
                db_g = db_g + _dot(jnp.concatenate(gl, axis=0), c_twice, TN) + _dot((te * x_dt).astype(BF16), ds_b, NT)
                red[0:BLK, lanes] = (dy_b.astype(F32) * y_diag - x_b.astype(F32) * dx_diag) + (d_y * y_off - x_dt * end_part)
                red[BLK:2 * BLK, lanes] = x_dt * end_part
                red[2 * BLK:3 * BLK, lanes] = d_x * xs
                red[3 * BLK:3 * BLK + HALO, lanes] = jnp.broadcast_to(jnp.sum(ds_p * s_p, axis=0, keepdims=True), (HALO, LANES))
                red[3 * BLK + HALO:3 * BLK + 2 * HALO, lanes] = jnp.broadcast_to(
                    jnp.sum(d_y * xs, axis=0, keepdims=True), (HALO, LANES))
                dxbc_ref[:, lanes] = d_x * dt64[:, lanes] + d64[:, lanes] * d_y
                dstate[:, lanes] = chunk_decay[:, lanes] * ds_p + _dot(c_b, fs_dy, TN)
            dxbc_ref[:, OFF_B + g * SSD_STATE:OFF_B + (g + 1) * SSD_STATE] = db_g
            dxbc_ref[:, OFF_C + g * SSD_STATE:OFF_C + (g + 1) * SSD_STATE] = dc_g
        sums = _lane_sums(red[...], sel64)
        at_last = (jnp.sum(sums[BLK:2 * BLK], axis=0, keepdims=True)
                   + jnp.exp(cs[BLK - 1:BLK, :]) * sums[3 * BLK:3 * BLK + 1])
        dcs = sums[0:BLK] + last_row * at_last
        ddt_x = sums[2 * BLK:3 * BLK]
        dd_row = sums[3 * BLK + HALO:3 * BLK + HALO + 1]
        da = _dot(triu, dcs, NN, HI)
        ddt = (da * a_neg + ddt_x) * rowmask
        ddtr = ddt * jax.nn.sigmoid(dtr_ref[...] + bias_ref[...]) * (lane < HEADS).astype(F32)
        ddtr_ref[...] = ddtr.astype(ddtr_ref.dtype)
        ddtb_ref[...] += jnp.sum(ddtr, axis=0, keepdims=True)
        dal_ref[...] += jnp.sum(da * dt, axis=0, keepdims=True) * a_neg
        dd_ref[...] += dd_row

    blk = lambda w: pl.BlockSpec((BLK, w), lambda s: (nb - 1 - s, 0))
    par = lambda a: pl.BlockSpec(a.shape, lambda s: (0, 0))
    acc = lambda w: pl.BlockSpec((1, w), lambda s: (0, 0))
    sel64, sel128 = _head_spread(HEAD_DIM), _head_spread(LANES)
    return pl.pallas_call(
        body, name="ssd_bwd", grid=(nb,),
        in_specs=[blk(SSD_INNER), blk(SSD_INNER), blk(SSD_INNER), blk(XBC), blk(LANES),
                  pl.BlockSpec((None, SSD_STATE, SSD_INNER), lambda s: (nb - 1 - s, 0, 0)),
                  par(dt_bias), par(a_log), par(d_skip), par(norm_g), par(sel64), par(sel128)],
        out_specs=[blk(SSD_INNER), blk(XBC), blk(LANES), acc(SSD_INNER), acc(LANES), acc(LANES), acc(LANES)],
        out_shape=[jax.ShapeDtypeStruct((n_rows, SSD_INNER), BF16), jax.ShapeDtypeStruct((n_rows, XBC), F32),
                   jax.ShapeDtypeStruct((n_rows, LANES), BF16), jax.ShapeDtypeStruct((1, SSD_INNER), F32),
                   jax.ShapeDtypeStruct((1, LANES), F32), jax.ShapeDtypeStruct((1, LANES), F32),
                   jax.ShapeDtypeStruct((1, LANES), F32)],
        scratch_shapes=[pltpu.VMEM((SSD_STATE, SSD_INNER), F32), pltpu.VMEM((BLK, SSD_INNER), F32),
                        pltpu.VMEM((3 * BLK + 2 * HALO, SSD_INNER), F32)],
        compiler_params=_params(12 << 20, ("arbitrary",)),
    )(dy, ypre, z, xbc, dtr, states, dt_bias, a_log, d_skip, norm_g, sel64, sel128)


HEAD_GROUP = 4


def _sb_logits(zl, valid):
    log_keep = -(jnp.maximum(zl, 0.0) + jnp.log(1.0 + jnp.exp(-jnp.abs(zl))))
    log_beta = log_keep + zl
    if valid is not None:
        log_keep = jnp.where(valid, log_keep, 0.0)
    return log_beta, log_keep


def _sums_dot(x, tri_and_ones):
    hi = x.astype(BF16)
    lo = (x - hi.astype(F32)).astype(BF16)
    both = _dot(jnp.concatenate([hi, lo], axis=1), tri_and_ones)
    return both[:, :BLK], both[:, BLK:]


def _tri_and_ones(tri):
    half = jnp.concatenate([tri, jnp.ones((BLK, BLK), F32)], axis=1)
    return jnp.concatenate([half, half], axis=0).astype(BF16)


def _tile_mask(i, j, ri, ci):
    key = j * BLK + ci
    return (key < i * BLK + ri) & (key >= PAD)


def _pair_lanes(p):
    return slice(p * 2 * HEAD_DIM, (p + 1) * 2 * HEAD_DIM)


def _head_stack(x_pair, first):
    zero = jnp.zeros_like(x_pair)
    return jnp.concatenate([jnp.where(first, x_pair, zero), jnp.where(first, zero, x_pair)], axis=0)


def _block_rows(j):
    return pl.ds(j * BLK if isinstance(j, int) else pl.multiple_of(j * BLK, BLK), BLK)


def _sweep(i, tile, leftwards):
    step = -1 if leftwards else 1
    first, last = (i, 0) if leftwards else (0, i)
    tile(first, True, first, jnp.maximum(i - 1, 0) if leftwards else jnp.minimum(1, i))

    def mid(t, carry):
        j = i - t if leftwards else t
        tile(j, False, j - step, j + step)
        return carry

    lax.fori_loop(1, i, mid, 0)

    @pl.when(i >= 1)
    def _():
        tile(last, True, last - step, last)


def _ride_along(ex, n_in, n_out, refs):
    k = 0
    parts = []
    for cnt in (n_in, ex.n, n_out, ex.n):
        parts.append(refs[k:k + cnt])
        k += cnt
    n_sems = len(ex.scratch)
    return (*parts, refs[k:len(refs) - n_sems], refs[len(refs) - n_sems:])


def _attn_fwd(q, k, v, ex, ex_arrays):
    n_rows, width = q.shape
    nb = n_rows // BLK
    gw = HEAD_GROUP * HEAD_DIM
    groups = width // gw

    def body(*refs):
        (q_ref, k_ref, v_ref), ex_in, (o_ref, tot_ref), ex_out, (run_ref, z_ref, w_ref), sems = _ride_along(ex, 3, 2, refs)
        step = pl.program_id(0)

        @pl.when(step == 0)
        def _():
            ex.start(ex_in, ex_out, sems)

        ri = lax.broadcasted_iota(jnp.int32, (BLK, BLK), 0)
        ci = lax.broadcasted_iota(jnp.int32, (BLK, BLK), 1)
        sums = _tri_and_ones((ri > ci).astype(F32))
        first = ci < HEAD_DIM
        heads, pairs = range(HEAD_GROUP), range(HEAD_GROUP // 2)

        def stack(ref, j, p):
            return _head_stack(ref[_block_rows(j), _pair_lanes(p)], first)

        def q_block(i, carry):
            rows = _block_rows(i)
            o_ref[rows, :] = jnp.zeros((BLK, gw), F32)
            run_ref[...] = jnp.zeros_like(run_ref)
            w_ref[...] = jnp.zeros_like(w_ref)
            for p in pairs:
                z_ref[p] = _dot(q_ref[rows, _pair_lanes(p)], stack(k_ref, i, p), NT)

            def tile(j, masked, j_prev, j_next):
                valid = _tile_mask(i, j, ri, ci) if masked else None
                q_i, o_i = q_ref[rows, :], o_ref[rows, :]
                outs = [_dot(w_ref[p], stack(v_ref, j_prev, p)) for p in pairs]
                z_next = [_dot(q_i[:, _pair_lanes(p)], stack(k_ref, j_next, p), NT) for p in pairs]
                runs = [run_ref[h] for h in heads]
                zl2 = [z_ref[p] for p in pairs]
                lgs, sms = [], []
                for h in heads:
                    lgs.append(_sb_logits(zl2[h // 2][:, (h % 2) * BLK:(h % 2 + 1) * BLK], valid))
                    sms.append(_sums_dot(lgs[h][1], sums))
                ws = [jnp.exp(lgs[h][0] + sms[h][0] + runs[h]) for h in heads]
                if masked:
                    ws = [jnp.where(valid, w, 0.0) for w in ws]
                for p in pairs:
                    w_ref[p] = jnp.concatenate([ws[2 * p].astype(BF16), ws[2 * p + 1].astype(BF16)], axis=1)
                    z_ref[p] = z_next[p]
                o_ref[rows, :] = o_i + jnp.concatenate(outs, axis=1)
                for h in heads:
                    run_ref[h] = runs[h] + sms[h][1]

            _sweep(i, tile, leftwards=True)
            o_ref[rows, :] += jnp.concatenate([_dot(w_ref[p], stack(v_ref, 0, p)) for p in pairs], axis=1)
            for h in heads:
                tot_ref[rows, h:h + 1] = run_ref[h][:, 0:1]
            return carry

        lax.fori_loop(0, nb, q_block, 0)

        @pl.when(step == groups - 1)
        def _():
            ex.wait(ex_in, ex_out, sems)

    spec = pl.BlockSpec((n_rows, gw), lambda g: (0, g))
    tot_spec = pl.BlockSpec((None, n_rows, HEAD_GROUP), lambda g: (g, 0, 0))
    res = pl.pallas_call(
        body, name="attn_fwd", grid=(groups,), in_specs=[spec, spec, spec] + ex.specs,
        out_specs=[spec, tot_spec] + ex.specs,
        out_shape=[jax.ShapeDtypeStruct((n_rows, width), F32),
                   jax.ShapeDtypeStruct((groups, n_rows, HEAD_GROUP), F32)] + ex.out_shape,
        scratch_shapes=[pltpu.VMEM((HEAD_GROUP, BLK, BLK), F32), pltpu.VMEM((HEAD_GROUP // 2, BLK, 2 * BLK), F32),
                        pltpu.VMEM((HEAD_GROUP // 2, BLK, 2 * BLK), BF16)] + ex.scratch,
        compiler_params=_params(n_rows * (3 * gw * 2 + gw * 4 + LANES * 4), ("arbitrary",)),
    )(q, k, v, *ex_arrays)
    return res[0], res[1], res[2:]


def _attn_bwd(q, k, v, keep_total, do, ex, ex_arrays):
    n_rows, width = q.shape
    nb = n_rows // BLK
    gw = HEAD_GROUP * HEAD_DIM
    groups = width // gw

    def body(*refs):
        ((q_ref, k_ref, v_ref, tot_ref, do_ref), ex_in, (dq_ref, dk_ref, dv_ref), ex_out,
         (dq_acc, dk_acc, dv_acc, tot_b, run_ref, rung_ref, z_ref, dw_ref, dz_ref, wb_ref, qst_ref, dost_ref),
         sems) = _ride_along(ex, 5, 3, refs)
        step = pl.program_id(0)

        @pl.when(step == 0)
        def _():
            ex.start(ex_in, ex_out, sems)

        ri = lax.broadcasted_iota(jnp.int32, (BLK, BLK), 0)
        ci = lax.broadcasted_iota(jnp.int32, (BLK, BLK), 1)
        sums_keep = _tri_and_ones((ri <= ci).astype(F32))
        sums_g = _tri_and_ones((ri < ci).astype(F32))
        first = ci < HEAD_DIM
        heads, pairs = range(HEAD_GROUP), range(HEAD_GROUP // 2)
        dk_acc[...] = jnp.zeros_like(dk_acc)
        dv_acc[...] = jnp.zeros_like(dv_acc)

        def stack(ref, j, p):
            return _head_stack(ref[_block_rows(j), _pair_lanes(p)], first)

        def owed_dq(j_prev):
            return [_dot(jnp.concatenate([dz_ref[2 * p], dz_ref[2 * p + 1]], axis=1), stack(k_ref, j_prev, p)) for p in pairs]

        def owed_dk():
            return [_dot(jnp.concatenate([dz_ref[2 * p], dz_ref[2 * p + 1]], axis=0), qst_ref[p], TN) for p in pairs]

        def owed_dv():
            return [_dot(jnp.concatenate([wb_ref[2 * p], wb_ref[2 * p + 1]], axis=0), dost_ref[p], TN) for p in pairs]

        def owed(j_prev):
            return owed_dq(j_prev), owed_dk(), owed_dv()

        def settle(j_prev, parts):
            dq, dk, dv = parts
            prev = _block_rows(j_prev)
            dq_acc[...] += jnp.concatenate(dq, axis=1)
            dk_acc[prev, :] += jnp.concatenate(dk, axis=1)
            dv_acc[prev, :] += jnp.concatenate(dv, axis=1)

        def q_block(i, carry):
            rows = _block_rows(i)
            dq_acc[...] = jnp.zeros_like(dq_acc)
            run_ref[...] = jnp.zeros_like(run_ref)
            rung_ref[...] = jnp.zeros_like(rung_ref)
            dz_ref[...] = jnp.zeros_like(dz_ref)
            wb_ref[...] = jnp.zeros_like(wb_ref)
            for h in heads:
                tot_b[h] = jnp.broadcast_to(tot_ref[rows, h:h + 1], (BLK, BLK))
            for p in pairs:
                qst_ref[p] = _head_stack(q_ref[rows, _pair_lanes(p)], first)
                dost_ref[p] = _head_stack(do_ref[rows, _pair_lanes(p)], first)
                z_ref[p] = _dot(q_ref[rows, _pair_lanes(p)], stack(k_ref, 0, p), NT)
                dw_ref[p] = _dot(do_ref[rows, _pair_lanes(p)], stack(v_ref, 0, p), NT)

            def tile(j, masked, j_prev, j_next):
                valid = _tile_mask(i, j, ri, ci) if masked else None
                half = lambda a, h: a[h // 2][:, (h % 2) * BLK:(h % 2 + 1) * BLK]
                q_i, do_i = q_ref[rows, :], do_ref[rows, :]
                zl2 = [z_ref[p] for p in pairs]
                dw2 = [dw_ref[p] for p in pairs]
                lgs, sms, ws, gs, gsm = [], [], [], [], []
                for h in heads:
                    lgs.append(_sb_logits(half(zl2, h), valid))
                    sms.append(_sums_dot(lgs[h][1], sums_keep))
                    if h == 0:
                        part_dq = owed_dq(j_prev)
                    if h == 1:
                        part_dk = owed_dk()
                    if h == 2:
                        part_dv = owed_dv()
                    if h == 3:
                        z_next = [_dot(q_i[:, _pair_lanes(p)], stack(k_ref, j_next, p), NT) for p in pairs]
                parts = (part_dq, part_dk, part_dv)
                for h in heads:
                    w = jnp.exp(lgs[h][0] + (tot_b[h] - run_ref[h] - sms[h][0]))
                    ws.append(jnp.where(valid, w, 0.0) if masked else w)
                    gs.append(half(dw2, h) * ws[h])
                    gsm.append(_sums_dot(gs[h], sums_g))
                    if h == 0:
                        dw_next = [_dot(do_i[:, _pair_lanes(p)], stack(v_ref, j_next, p), NT) for p in pairs]
                dzs = []
                for h in heads:
                    beta = jnp.exp(lgs[h][0])
                    dz = gs[h] * (1.0 - beta) - beta * (gsm[h][0] + rung_ref[h])
                    dzs.append(jnp.where(valid, dz, 0.0) if masked else dz)
                settle(j_prev, parts)
                for h in heads:
                    dz_ref[h] = dzs[h].astype(BF16)
                    wb_ref[h] = ws[h].astype(BF16)
                    run_ref[h] += sms[h][1]
                    rung_ref[h] += gsm[h][1]
                for p in pairs:
                    z_ref[p] = z_next[p]
                    dw_ref[p] = dw_next[p]

            _sweep(i, tile, leftwards=False)
            settle(i, owed(i))
            dq_ref[rows, :] = (dq_acc[...] * SB_SCALE).astype(dq_ref.dtype)
            return carry

        lax.fori_loop(0, nb, q_block, 0)
        dk_ref[...] = dk_acc[...].astype(dk_ref.dtype)
        dv_ref[...] = dv_acc[...].astype(dv_ref.dtype)

        @pl.when(step == groups - 1)
        def _():
            ex.wait(ex_in, ex_out, sems)

    spec = pl.BlockSpec((n_rows, gw), lambda g: (0, g))
    tot_spec = pl.BlockSpec((None, n_rows, HEAD_GROUP), lambda g: (g, 0, 0))
    out = jax.ShapeDtypeStruct((n_rows, width), BF16)
    tile_f32 = pltpu.VMEM((HEAD_GROUP, BLK, BLK), F32)
    tile_bf16 = pltpu.VMEM((HEAD_GROUP, BLK, BLK), BF16)
    pair_f32 = pltpu.VMEM((HEAD_GROUP // 2, BLK, 2 * BLK), F32)
    pair_stack = pltpu.VMEM((HEAD_GROUP // 2, 2 * BLK, BLK), BF16)
    res = pl.pallas_call(
        body, name="attn_bwd", grid=(groups,), in_specs=[spec, spec, spec, tot_spec, spec] + ex.specs,
        out_specs=[spec] * 3 + ex.specs, out_shape=[out] * 3 + ex.out_shape,
        scratch_shapes=[pltpu.VMEM((BLK, gw), F32), pltpu.VMEM((n_rows, gw), F32), pltpu.VMEM((n_rows, gw), F32),
                        tile_f32, tile_f32, tile_f32, pair_f32, pair_f32, tile_bf16, tile_bf16, pair_stack,
                        pair_stack] + ex.scratch,
        compiler_params=_params(n_rows * (7 * gw * 2 + LANES * 4 + gw * 4), ("arbitrary",)),
    )(q, k, v, keep_total, do, *ex_arrays)
    return res[0], res[1], res[2], res[3:]


class _Exchange:
    def __init__(self, arrays, gather):
        self.n = len(arrays)
        self.gather = gather
        self.out_shape = [jax.ShapeDtypeStruct(((N_DEV,) + a.shape) if gather else a.shape, a.dtype) for a in arrays]
        self.scratch = [pltpu.SemaphoreType.DMA((self.n, N_DEV - 1)), pltpu.SemaphoreType.DMA((self.n, N_DEV - 1)),
                        pltpu.SemaphoreType.DMA((self.n,))]
        self.specs = [pl.BlockSpec(memory_space=pl.ANY)] * self.n

    def _copies(self, ins, outs, sems):
        send_sems, recv_sems, local_sems = sems
        x, y, c = lax.axis_index("x"), lax.axis_index("y"), lax.axis_index("c")
        me = 4 * x + 2 * y + c
        gather = self.gather
        local, sends, recvs = [], [], []
        for a in range(self.n):
            local.append(pltpu.make_async_copy(ins[a] if gather else ins[a].at[me], outs[a].at[me if gather else 0],
                                               local_sems.at[a]))
        for r in range(1, N_DEV):
            px = 1 - x if r & 4 else x
            py = 1 - y if r & 2 else y
            pc = 1 - c if r & 1 else c
            idx = 4 * px + 2 * py + pc
            for a in range(self.n):
                src = ins[a] if gather else ins[a].at[idx]
                pair = dict(send_sem=send_sems.at[a, r - 1], recv_sem=recv_sems.at[a, r - 1],
                            device_id=(px, py, pc), device_id_type=pl.DeviceIdType.MESH)
                sends.append(pltpu.make_async_remote_copy(src_ref=src, dst_ref=outs[a].at[me if gather else r], **pair))
                recvs.append(pltpu.make_async_remote_copy(src_ref=src, dst_ref=outs[a].at[idx if gather else r], **pair))
        return local, sends, recvs

    def start(self, ins, outs, sems):
        local, sends, _ = self._copies(ins, outs, sems)
        for cp in local + sends:
            cp.start()

    def wait(self, ins, outs, sems):
        local, sends, recvs = self._copies(ins, outs, sems)
        for cp in recvs:
            cp.wait_recv()
        for cp in sends:
            cp.wait_send()
        for cp in local:
            cp.wait()


def _peer_exchange(name, arrays, gather):
    ex = _Exchange(arrays, gather)
    n = ex.n

    def body(*refs):
        ins, outs, sems = refs[:n], refs[n:2 * n], refs[2 * n:]
        ex.start(ins, outs, sems)
        ex.wait(ins, outs, sems)

    return pl.pallas_call(body, name=name, in_specs=ex.specs, out_specs=ex.specs, out_shape=ex.out_shape,
                          scratch_shapes=ex.scratch)(*arrays)


def _sum_slots(name, x):
    def body(x_ref, o_ref):
        acc = x_ref[0]
        for s in range(1, N_DEV):
            acc = acc + x_ref[s]
        o_ref[...] = acc
    return pl.pallas_call(body, name=name, out_shape=jax.ShapeDtypeStruct(x.shape[1:], F32))(x)


def _adamw(name, w, slots, m, v):
    n_slots, rows, cols = slots.shape
    tr = next((t for t in (256, 176, 128) if rows % t == 0 and rows > t), rows)

    def body(w_ref, s_ref, m_ref, v_ref, g_ref, d_ref, nm_ref, nv_ref):
        g = s_ref[0].astype(F32)
        for s in range(1, n_slots):
            g = g + s_ref[s].astype(F32)
        nm = ADAM_B1 * m_ref[...] + (1.0 - ADAM_B1) * g
        nv = ADAM_B2 * v_ref[...] + (1.0 - ADAM_B2) * (g * g)
        m_hat = nm / (1.0 - ADAM_B1 ** ADAM_STEP)
        v_hat = nv / (1.0 - ADAM_B2 ** ADAM_STEP)
        g_ref[...] = g
        d_ref[...] = -ADAM_LR * (m_hat / (jnp.sqrt(v_hat) + ADAM_EPS) + ADAM_WD * w_ref[...])
        nm_ref[...] = nm
        nv_ref[...] = nv

    spec = pl.BlockSpec((tr, cols), lambda i: (i, 0))
    out = jax.ShapeDtypeStruct((rows, cols), F32)
    return pl.pallas_call(
        body, name=name, grid=(rows // tr,),
        in_specs=[spec, pl.BlockSpec((n_slots, tr, cols), lambda i: (0, i, 0)), spec, spec],
        out_specs=[spec] * 4, out_shape=[out] * 4,
        compiler_params=_params((n_slots + 7) * tr * cols * 4, ("parallel",)),
    )(w, slots, m, v)


def _pad_lanes(a):
    return jnp.pad(a, ((0, 0), (0, LANES - a.shape[1])))


def kernel(x, meta_tokens, mix_pre_g, w_in, ssd_conv_w, ssd_conv_b, ssd_dt_bias, ssd_a_log, ssd_d, ssd_norm_g, sb_norm_g, w_out, mix_post_g, ffn_pre_g, w_up, ffn_conv_w, ffn_conv_b, w_down, ffn_post_g, loss_target, m_meta_tokens, m_mix_pre_g, m_w_in, m_ssd_conv_w, m_ssd_conv_b, m_ssd_dt_bias, m_ssd_a_log, m_ssd_d, m_ssd_norm_g, m_sb_norm_g, m_w_out, m_mix_post_g, m_ffn_pre_g, m_w_up, m_ffn_conv_w, m_ffn_conv_b, m_w_down, m_ffn_post_g, v_meta_tokens, v_mix_pre_g, v_w_in, v_ssd_conv_w, v_ssd_conv_b, v_ssd_dt_bias, v_ssd_a_log, v_ssd_d, v_ssd_norm_g, v_sb_norm_g, v_w_out, v_mix_post_g, v_ffn_pre_g, v_w_up, v_ffn_conv_w, v_ffn_conv_b, v_w_down, v_ffn_post_g):
    seq = x.shape[1]
    me = 4 * lax.axis_index("x") + 2 * lax.axis_index("y") + lax.axis_index("c")
    in_cols = w_in.shape[2]
    up_cols = w_up.shape[2]
    out_rows = w_out.shape[1]
    down_rows = w_down.shape[1]

    g_in, g_meta, g_scw, g_fcw = _peer_exchange(
        "gather_w_in", [w_in[0].astype(BF16), meta_tokens, ssd_conv_w[0], ffn_conv_w[0]], gather=True)
    late_weights = [w_out[0].astype(BF16), w_up[0].astype(BF16), w_down[0].astype(BF16)]
    w_in_full = g_in.transpose(1, 0, 2).reshape(D_MODEL, N_DEV * in_cols)
    off = [0, SSD_INNER, SSD_INNER + XBC, SSD_INNER + XBC + HEADS]
    w_z = w_in_full[:, off[0]:off[1]]
    w_xbc = w_in_full[:, off[1]:off[2]]
    w_dt = _pad_lanes(w_in_full[:, off[2]:off[3]])
    w_q = w_in_full[:, off[3]:off[3] + SSD_INNER]
    w_k = w_in_full[:, off[3] + SSD_INNER:off[3] + 2 * SSD_INNER]
    w_v = w_in_full[:, off[3] + 2 * SSD_INNER:off[3] + 3 * SSD_INNER]
    meta_full = g_meta.transpose(1, 0, 2).reshape(N_META, D_MODEL)
    scw_full = g_scw.transpose(1, 0, 2).reshape(SSD_CONV, XBC)
    fcw_full = g_fcw.transpose(1, 0, 2).reshape(FFN_CONV, D_FF)

    dt_bias_p, a_log_p, d_p = _pad_lanes(ssd_dt_bias), _pad_lanes(ssd_a_log), _pad_lanes(ssd_d)

    h0 = jnp.concatenate([jnp.zeros((PAD, D_MODEL), F32), meta_full, x[0]], axis=0)
    target = jnp.concatenate([jnp.zeros((BLK, D_MODEL), F32), loss_target[0]], axis=0)
    xn1 = _rms_fwd("rms_pre_mix", h0, mix_pre_g)
    z = _mm("proj_z", [(xn1, w_z)], "nn", F32)
    xbc_raw = _mm("proj_xbc", [(xn1, w_xbc)], "nn", F32)
    dtr = _mm("proj_dt", [(xn1, w_dt)], "nn", F32)
    q = _mm("proj_q", [(xn1, w_q * SB_SCALE)], "nn", BF16)
    k = _mm("proj_k", [(xn1, w_k)], "nn", BF16)
    v = _mm("proj_v", [(xn1, w_v)], "nn", BF16)
    xbc_act = _ssd_conv_fwd(xbc_raw, scw_full, ssd_conv_b)
    ypre, y_ssd, states = _ssd_fwd(xbc_act, dtr, z, dt_bias_p, a_log_p, d_p, ssd_norm_g)
    o, keep_total, (g_out, g_up, g_down) = _attn_fwd(q, k, v, _Exchange(late_weights, gather=True), late_weights)
    w_out_full = g_out.reshape(N_DEV * out_rows, D_MODEL)
    wo_ssd, wo_sb = w_out_full[:SSD_INNER], w_out_full[SSD_INNER:]
    w_up_full = g_up.transpose(1, 0, 2).reshape(D_MODEL, N_DEV * up_cols)
    w_gate, w_lin = w_up_full[:, :D_FF], w_up_full[:, D_FF:]
    w_down_full = g_down.reshape(N_DEV * down_rows, D_MODEL)
    y_sb = _rms_fwd("rms_sb", o, sb_norm_g)
    mix = _mm("mix_out", [(y_ssd, wo_ssd), (y_sb, wo_sb)], "nn", F32)
    h1, xn2 = _mix_post(mix, h0, mix_post_g, ffn_pre_g)
    g_raw = _mm("ffn_gate", [(xn2, w_gate)], "nn", F32)
    u = _mm("ffn_lin", [(xn2, w_lin)], "nn", F32)
    act = _ffn_act(g_raw, u, fcw_full, ffn_conv_b)
    f = _mm("ffn_down", [(act, w_down_full)], "nn", F32)
    dh2, df, loss_row, dg_ffn_post = _loss_post(f, h1, target, ffn_post_g)
    loss = lax.psum(loss_row[0, 0], ("x", "y", "c"))

    dact = _mm("d_act", [(df, w_down_full)], "nt", F32)
    dw_down = _mm("dw_down", [(act, df)], "tn", F32)
    dg_conv, du = _ffn_bwd_act(dact, u, g_raw, fcw_full, ffn_conv_b)
    dg_raw, dfcw0, dfcw1, dfcw2, dfcb = _conv_bwd("ffn_conv_bwd", dg_conv, g_raw, fcw_full, FFN_CONV)
    dxn2 = _mm("d_xn2", [(dg_raw, w_gate), (du, w_lin)], "nt", F32)
    dw_gate = _mm("dw_gate", [(xn2, dg_raw)], "tn", F32)
    dw_lin = _mm("dw_lin", [(xn2, du)], "tn", F32)
    dh1, dmix, dg_ffn_pre, dg_mix_post = _mid_bwd(dxn2, h1, dh2, mix, ffn_pre_g, mix_post_g)

    dy_ssd = _mm("d_yssd", [(dmix, wo_ssd)], "nt", F32)
    dy_sb = _mm("d_ysb", [(dmix, wo_sb)], "nt", F32)
    dwo_ssd = _mm("dw_out_ssd", [(y_ssd, dmix)], "tn", F32)
    dwo_sb = _mm("dw_out_sb", [(y_sb, dmix)], "tn", F32)
    do, dg_sb = _norm_bwd("sb_norm_bwd", dy_sb, o, sb_norm_g)
    half = N_DEV // 2
    early_slabs = [
        jnp.concatenate([dwo_ssd, dwo_sb], axis=0).reshape(N_DEV, out_rows, D_MODEL),
        jnp.concatenate([dw_gate.reshape(D_MODEL, half, up_cols).transpose(1, 0, 2),
                         dw_lin.reshape(D_MODEL, half, up_cols).transpose(1, 0, 2)], axis=0),
        dw_down.reshape(N_DEV, down_rows, D_MODEL)]
    dq, dk, dv, (l_out, l_up, l_down) = _attn_bwd(q, k, v, keep_total, do, _Exchange(early_slabs, gather=False), early_slabs)
    dz, dxbc_act, ddtr, dg_ssd_norm, dd_skip, da_log, ddt_bias = _ssd_bwd(
        dy_ssd, ypre, z, xbc_act, dtr, states, dt_bias_p, a_log_p, d_p, ssd_norm_g)
    dconv = _ssd_conv_bwd_act(dxbc_act, xbc_raw, scw_full, ssd_conv_b)
    dxbc_raw, dscw0, dscw1, dscw2, dscw3, dscb = _conv_bwd("ssd_conv_bwd", dconv, xbc_raw, scw_full, SSD_CONV)
    segs = [(dz, w_z), (dxbc_raw, w_xbc), (ddtr, w_dt), (dq, w_q), (dk, w_k), (dv, w_v)]
    dxn1 = _mm("d_xn1", segs, "nt", F32)
    dw_segs = [_mm("dw_in_%d" % s, [(xn1, d)], "tn", BF16) for s, (d, _) in enumerate(segs)]
    dw_segs[2] = dw_segs[2][:, :HEADS]
    dw_in = jnp.concatenate(dw_segs, axis=1)
    dh0, dg_mix_pre = _first_bwd(dxn1, h0, dh1, mix_pre_g)
    grad_x = dh0[BLK:][None]

    slab_in = dw_in.reshape(D_MODEL, N_DEV, in_cols).transpose(1, 0, 2)
    (l_in,) = _peer_exchange("scatter_dw_in", [slab_in], gather=False)

    small = [dg_mix_pre, dscb, ddt_bias, da_log, dd_skip, dg_ssd_norm, dg_sb, dg_mix_post, dg_ffn_pre, dfcb,
             dg_ffn_post, dh0[PAD:BLK].reshape(1, -1), dscw0, dscw1, dscw2, dscw3, dfcw0, dfcw1, dfcw2]
    sizes = [a.shape[1] for a in small]
    total = sum(sizes)
    rows_packed = -(-total // (LANES * HALO)) * HALO
    packed = jnp.pad(jnp.concatenate(small, axis=1), ((0, 0), (0, rows_packed * LANES - total)))
    (gathered,) = _peer_exchange("gather_small_grads", [packed.reshape(rows_packed, LANES)], gather=True)
    summed = _sum_slots("sum_small_grads", gathered).reshape(1, rows_packed * LANES)
    pieces, at = [], 0
    for s in sizes:
        pieces.append(summed[:, at:at + s])
        at += s
    (g_mix_pre, g_scb, g_dtb, g_alog, g_dskip, g_ssd_norm, g_sb, g_mix_post, g_ffn_pre, g_fcb, g_ffn_post,
     g_meta_flat, gs0, gs1, gs2, gs3, gf0, gf1, gf2) = pieces
    g_dtb, g_alog, g_dskip = g_dtb[:, :HEADS], g_alog[:, :HEADS], g_dskip[:, :HEADS]
    g_meta_full = g_meta_flat.reshape(N_META, D_MODEL)
    g_scw_full = jnp.concatenate([gs0, gs1, gs2, gs3], axis=0)
    g_fcw_full = jnp.concatenate([gf0, gf1, gf2], axis=0)
    meta_cols, scw_cols, fcw_cols = meta_tokens.shape[1], ssd_conv_w.shape[2], ffn_conv_w.shape[2]
    g_meta_mine = lax.dynamic_slice(g_meta_full, (0, me * meta_cols), (N_META, meta_cols))
    g_scw_mine = lax.dynamic_slice(g_scw_full, (0, me * scw_cols), (SSD_CONV, scw_cols))
    g_fcw_mine = lax.dynamic_slice(g_fcw_full, (0, me * fcw_cols), (FFN_CONV, fcw_cols))

    def lead(a):
        return a[None]

    upd = [
        _adamw("adamw_meta", meta_tokens, lead(g_meta_mine), m_meta_tokens, v_meta_tokens),
        _adamw("adamw_mix_pre_g", mix_pre_g, lead(g_mix_pre), m_mix_pre_g, v_mix_pre_g),
        [lead(a) for a in _adamw("adamw_w_in", w_in[0], l_in, m_w_in[0], v_w_in[0])],
        [lead(a) for a in _adamw("adamw_ssd_conv_w", ssd_conv_w[0], lead(g_scw_mine), m_ssd_conv_w[0], v_ssd_conv_w[0])],
        _adamw("adamw_ssd_conv_b", ssd_conv_b, lead(g_scb), m_ssd_conv_b, v_ssd_conv_b),
        _adamw("adamw_ssd_dt_bias", ssd_dt_bias, lead(g_dtb), m_ssd_dt_bias, v_ssd_dt_bias),
        _adamw("adamw_ssd_a_log", ssd_a_log, lead(g_alog), m_ssd_a_log, v_ssd_a_log),
        _adamw("adamw_ssd_d", ssd_d, lead(g_dskip), m_ssd_d, v_ssd_d),
        _adamw("adamw_ssd_norm_g", ssd_norm_g, lead(g_ssd_norm), m_ssd_norm_g, v_ssd_norm_g),
        _adamw("adamw_sb_norm_g", sb_norm_g, lead(g_sb), m_sb_norm_g, v_sb_norm_g),
        [lead(a) for a in _adamw("adamw_w_out", w_out[0], l_out, m_w_out[0], v_w_out[0])],
        _adamw("adamw_mix_post_g", mix_post_g, lead(g_mix_post), m_mix_post_g, v_mix_post_g),
        _adamw("adamw_ffn_pre_g", ffn_pre_g, lead(g_ffn_pre), m_ffn_pre_g, v_ffn_pre_g),
        [lead(a) for a in _adamw("adamw_w_up", w_up[0], l_up, m_w_up[0], v_w_up[0])],
        [lead(a) for a in _adamw("adamw_ffn_conv_w", ffn_conv_w[0], lead(g_fcw_mine), m_ffn_conv_w[0], v_ffn_conv_w[0])],
        _adamw("adamw_ffn_conv_b", ffn_conv_b, lead(g_fcb), m_ffn_conv_b, v_ffn_conv_b),
        [lead(a) for a in _adamw("adamw_w_down", w_down[0], l_down, m_w_down[0], v_w_down[0])],
        _adamw("adamw_ffn_post_g", ffn_post_g, lead(g_ffn_post), m_ffn_post_g, v_ffn_post_g),
    ]
    grads = [u_[0] for u_ in upd]
    deltas = [u_[1] for u_ in upd]
    new_m = [u_[2] for u_ in upd]
    new_v = [u_[3] for u_ in upd]
    return (loss, grad_x, *grads, *deltas, *new_m, *new_v)
```

```python
import math

import jax
import jax.numpy as jnp
from jax import lax
from jax.experimental import pallas as pl
from jax.experimental.pallas import tpu as pltpu

F32 = jnp.float32
BF16 = jnp.bfloat16
HI = lax.Precision.HIGHEST

D_MODEL = 1024
N_META = 16
BLK = 128
PAD = BLK - N_META
HEADS = 16
HEAD_DIM = 64
SSD_GROUPS = 2
SSD_STATE = 128
HEADS_PER_GROUP = HEADS // SSD_GROUPS
SSD_INNER = HEADS * HEAD_DIM
SSD_CONV = 4
XBC = SSD_INNER + 2 * SSD_GROUPS * SSD_STATE
OFF_B = SSD_INNER
OFF_C = SSD_INNER + SSD_GROUPS * SSD_STATE
D_FF = 2816
FFN_CONV = 3
EPS = 1e-6
SB_SCALE = 1.0 / math.sqrt(HEAD_DIM)
N_DEV = 8
LANES = 128
HALO = 8

ADAM_LR = 0.001
ADAM_B1 = 0.9
ADAM_B2 = 0.999
ADAM_EPS = 1e-08
ADAM_WD = 0.01
ADAM_STEP = 10

VMEM_FLOOR = 32 << 20
VMEM_CEIL = 60 << 20
MM_BUDGET = 20 << 20
ROW_BUDGET = 6 << 20

NN = (((1,), (0,)), ((), ()))
NT = (((1,), (1,)), ((), ()))
TN = (((0,), (0,)), ((), ()))


def _params(tile_bytes, sem=None):
    limit = int(min(max(2 * tile_bytes + (8 << 20), VMEM_FLOOR), VMEM_CEIL))
    return pltpu.CompilerParams(vmem_limit_bytes=limit, dimension_semantics=sem)


def _nbytes(shape, dtype):
    n = 1
    for s in shape:
        n *= s
    return n * jnp.dtype(dtype).itemsize


def _dot(a, b, dims=NN, precision=None):
    return lax.dot_general(a, b, dims, precision=precision, preferred_element_type=F32)


def _softplus(x):
    return jnp.maximum(x, 0.0) + jnp.log1p(jnp.exp(-jnp.abs(x)))


def _rms(x, g):
    r = lax.rsqrt(jnp.mean(x * x, axis=-1, keepdims=True) + EPS)
    return x * r * g


def _rms_bwd(dy, x, g):
    r = lax.rsqrt(jnp.mean(x * x, axis=-1, keepdims=True) + EPS)
    xh = x * r
    u = dy * g
    dx = r * (u - xh * jnp.mean(xh * u, axis=-1, keepdims=True))
    return dx, jnp.sum(dy * xh, axis=0, keepdims=True)


def _gelu(x):
    c = math.sqrt(2.0 / math.pi)
    return 0.5 * x * (1.0 + jnp.tanh(c * (x + 0.044715 * x * x * x)))


def _gelu_grad(x):
    c = math.sqrt(2.0 / math.pi)
    t = jnp.tanh(c * (x + 0.044715 * x * x * x))
    return 0.5 * (1.0 + t) + 0.5 * x * (1.0 - t * t) * c * (1.0 + 3.0 * 0.044715 * x * x)


def _row_tile(rows, bytes_per_row):
    big = 384
    return big if rows % big == 0 and big * bytes_per_row <= ROW_BUDGET else BLK


def _mm(name, pairs, mode, out_dtype):
    a0, b0 = pairs[0]
    if mode == "tn":
        m, n = a0.shape[1], b0.shape[1]
    elif mode == "nt":
        m, n = a0.shape[0], b0.shape[0]
    else:
        m, n = a0.shape[0], b0.shape[1]
    dims = {"nn": NN, "nt": NT, "tn": TN}[mode]

    def tile_bytes(tm, tn):
        tot = tm * tn * jnp.dtype(out_dtype).itemsize
        for a, b in pairs:
            k = a.shape[0] if mode == "tn" else a.shape[1]
            tot += tm * k * a.dtype.itemsize + tn * k * b.dtype.itemsize
        return tot

    cands_m = [t for t in (1408, 1024, 512, 384, 256, 128) if m % t == 0] or [m]
    cands_n = [t for t in (1408, 1024, 768, 512, 256, 128) if n % t == 0] or [n]
    best = None
    for tm in cands_m:
        for tn in cands_n:
            if tile_bytes(tm, tn) <= MM_BUDGET and (best is None or tm * tn > best[0] * best[1]):
                best = (tm, tn)
    tm, tn = best if best is not None else (cands_m[-1], cands_n[-1])
    npairs = len(pairs)

    def body(*refs):
        o_ref = refs[2 * npairs]
        acc = None
        for p in range(npairs):
            part = _dot(refs[2 * p][...], refs[2 * p + 1][...], dims)
            acc = part if acc is None else acc + part
        o_ref[...] = acc.astype(o_ref.dtype)

    in_specs, args = [], []
    for a, b in pairs:
        if mode == "tn":
            k = a.shape[0]
            in_specs += [pl.BlockSpec((k, tm), lambda i, j: (0, i)), pl.BlockSpec((k, tn), lambda i, j: (0, j))]
        elif mode == "nt":
            k = a.shape[1]
            in_specs += [pl.BlockSpec((tm, k), lambda i, j: (i, 0)), pl.BlockSpec((tn, k), lambda i, j: (j, 0))]
        else:
            k = a.shape[1]
            in_specs += [pl.BlockSpec((tm, k), lambda i, j: (i, 0)), pl.BlockSpec((k, tn), lambda i, j: (0, j))]
        args += [a, b]
    return pl.pallas_call(
        body, name=name, grid=(m // tm, n // tn), in_specs=in_specs,
        out_specs=pl.BlockSpec((tm, tn), lambda i, j: (i, j)),
        out_shape=jax.ShapeDtypeStruct((m, n), out_dtype),
        compiler_params=_params(tile_bytes(tm, tn), ("parallel", "parallel")),
    )(*args)


def _rowcall(name, fn, rows=(), prevs=(), nexts=(), pars=(), out_rows=(), out_accs=()):
    rows, prevs, nexts, pars = list(rows), list(prevs), list(nexts), list(pars)
    n_rows = (rows + prevs + nexts)[0].shape[0]
    per_row = sum(a.shape[1] * a.dtype.itemsize for a in rows + prevs + nexts)
    per_row += sum(c * jnp.dtype(dt).itemsize for c, dt in out_rows) + sum(a.shape[1] * 4 for a in prevs + nexts)
    tm = _row_tile(n_rows, per_row)
    nt = n_rows // tm
    hb = tm // HALO
    nr, npv, nnx, npar, nor, noa = len(rows), len(prevs), len(nexts), len(pars), len(out_rows), len(out_accs)

    def body(*refs):
        i = pl.program_id(0)
        k = 0
        row_refs = refs[k:k + nr]; k += nr
        pc = refs[k:k + npv]; k += npv
        ph = refs[k:k + npv]; k += npv
        nc = refs[k:k + nnx]; k += nnx
        nh = refs[k:k + nnx]; k += nnx
        par_refs = refs[k:k + npar]; k += npar
        orow = refs[k:k + nor]; k += nor
        oacc = refs[k:k + noa]; k += noa
        pscr = refs[k:k + npv]; k += npv
        nscr = refs[k:k + nnx]
        for c_, h_, s_ in zip(pc, ph, pscr):
            s_[0:HALO, :] = h_[...] * (i > 0).astype(F32)
            s_[HALO:HALO + tm, :] = c_[...]
        for c_, h_, s_ in zip(nc, nh, nscr):
            s_[0:tm, :] = c_[...]
            s_[tm:tm + HALO, :] = h_[...] * (i < nt - 1).astype(F32)
        prev_fns = [(lambda s, s_=s_: s_[pl.ds(HALO - s, tm), :]) for s_ in pscr]
        next_fns = [(lambda s, s_=s_: s_[pl.ds(s, tm), :]) for s_ in nscr]
        row_vals, acc_vals = fn(i, tm, row_refs, prev_fns, next_fns, par_refs)
        for r_, v in zip(orow, row_vals):
            r_[...] = v.astype(r_.dtype)
        if noa:
            @pl.when(i == 0)
            def _():
                for r_ in oacc:
                    r_[...] = jnp.zeros_like(r_)
            for r_, v in zip(oacc, acc_vals):
                r_[...] += v

    def row_spec(a):
        return pl.BlockSpec((tm, a.shape[1]), lambda i: (i, 0))

    def whole(shape):
        return pl.BlockSpec(shape, lambda i: (0, 0))

    in_specs = [row_spec(a) for a in rows]
    in_specs += [row_spec(a) for a in prevs]
    in_specs += [pl.BlockSpec((HALO, a.shape[1]), lambda i: (jnp.maximum(i * hb - 1, 0), 0)) for a in prevs]
    in_specs += [row_spec(a) for a in nexts]
    in_specs += [pl.BlockSpec((HALO, a.shape[1]), lambda i: (jnp.minimum((i + 1) * hb, n_rows // HALO - 1), 0)) for a in nexts]
    in_specs += [whole(p.shape) for p in pars]
    out_specs = [pl.BlockSpec((tm, c), lambda i: (i, 0)) for c, _ in out_rows] + [whole(s) for s in out_accs]
    out_shape = [jax.ShapeDtypeStruct((n_rows, c), dt) for c, dt in out_rows]
    out_shape += [jax.ShapeDtypeStruct(s, F32) for s in out_accs]
    scratch = [pltpu.VMEM((tm + HALO, a.shape[1]), F32) for a in prevs + nexts]
    tile = tm * per_row
    res = pl.pallas_call(
        body, name=name, grid=(nt,), in_specs=in_specs, out_specs=out_specs, out_shape=out_shape,
        scratch_shapes=scratch, compiler_params=_params(2 * tile, ("arbitrary",)),
    )(*rows, *prevs, *prevs, *nexts, *nexts, *pars)
    return res


def _row_ids(i, tm):
    return i * tm + lax.broadcasted_iota(jnp.int32, (tm, 1), 0)


def _conv_taps(prev_fn, w_ref, b_ref, taps):
    acc = b_ref[...]
    for k in range(taps):
        acc = acc + prev_fn(taps - 1 - k) * w_ref[k:k + 1, :]
    return acc


def _rms_fwd(name, h, g):
    def fn(i, tm, rows, prevs, nexts, pars):
        return [_rms(rows[0][...], pars[0][...])], []
    return _rowcall(name, fn, rows=[h], pars=[g], out_rows=[(h.shape[1], BF16)])[0]


def _ssd_conv_fwd(xbc_raw, w, b):
    def fn(i, tm, rows, prevs, nexts, pars):
        c = _conv_taps(prevs[0], pars[0], pars[1], SSD_CONV)
        y = c * jax.nn.sigmoid(c)
        return [jnp.where(_row_ids(i, tm) >= PAD, y, 0.0)], []
    return _rowcall("ssd_conv_fwd", fn, prevs=[xbc_raw], pars=[w, b], out_rows=[(XBC, F32)])[0]


def _mix_post(mix, h0, g_post, g_pre):
    def fn(i, tm, rows, prevs, nexts, pars):
        h1 = rows[1][...] + _rms(rows[0][...], pars[0][...])
        return [h1, _rms(h1, pars[1][...])], []
    return _rowcall("mix_post", fn, rows=[mix, h0], pars=[g_post, g_pre],
                    out_rows=[(D_MODEL, F32), (D_MODEL, BF16)])


def _ffn_act(g_raw, u, w, b):
    def fn(i, tm, rows, prevs, nexts, pars):
        g = _conv_taps(prevs[0], pars[0], pars[1], FFN_CONV)
        return [_gelu(g) * rows[0][...]], []
    return _rowcall("ffn_act", fn, rows=[u], prevs=[g_raw], pars=[w, b], out_rows=[(D_FF, BF16)])[0]


def _loss_post(f, h1, target, g_post):
    def fn(i, tm, rows, prevs, nexts, pars):
        fv, g = rows[0][...], pars[0][...]
        h2 = rows[1][...] + _rms(fv, g)
        real = _row_ids(i, tm) >= BLK
        diff = jnp.where(real, h2 - rows[2][...], 0.0)
        loss = 0.5 * jnp.sum(jnp.mean(diff * diff, axis=-1, keepdims=True))
        dh2 = diff * (1.0 / D_MODEL)
        df, dg = _rms_bwd(dh2, fv, g)
        return [dh2, df], [jnp.zeros((1, LANES), F32) + loss, dg]
    return _rowcall("loss_post", fn, rows=[f, h1, target], pars=[g_post],
                    out_rows=[(D_MODEL, F32), (D_MODEL, BF16)], out_accs=[(1, LANES), (1, D_MODEL)])


def _ffn_bwd_act(dact, u, g_raw, w, b):
    def fn(i, tm, rows, prevs, nexts, pars):
        g = _conv_taps(prevs[0], pars[0], pars[1], FFN_CONV)
        da = rows[0][...]
        return [da * rows[1][...] * _gelu_grad(g), da * _gelu(g)], []
    return _rowcall("ffn_bwd_act", fn, rows=[dact, u], prevs=[g_raw], pars=[w, b],
                    out_rows=[(D_FF, F32), (D_FF, BF16)])


def _conv_bwd(name, dy, x, w, taps):
    width = x.shape[1]

    def fn(i, tm, rows, prevs, nexts, pars):
        dy0 = nexts[0](0)
        dx = None
        for k in range(taps):
            term = nexts[0](taps - 1 - k) * pars[0][k:k + 1, :]
            dx = term if dx is None else dx + term
        dws = [jnp.sum(dy0 * prevs[0](taps - 1 - k), axis=0, keepdims=True) for k in range(taps)]
        return [dx], dws + [jnp.sum(dy0, axis=0, keepdims=True)]
    return _rowcall(name, fn, prevs=[x], nexts=[dy], pars=[w], out_rows=[(width, BF16)],
                    out_accs=[(1, width)] * (taps + 1))


def _mid_bwd(dxn2, h1, dh2, mix, g_pre, g_post):
    def fn(i, tm, rows, prevs, nexts, pars):
        d1, dg_pre = _rms_bwd(rows[0][...], rows[1][...], pars[0][...])
        dh1 = rows[2][...] + d1
        dmix, dg_post = _rms_bwd(dh1, rows[3][...], pars[1][...])
        return [dh1, dmix], [dg_pre, dg_post]
    return _rowcall("mid_bwd", fn, rows=[dxn2, h1, dh2, mix], pars=[g_pre, g_post],
                    out_rows=[(D_MODEL, F32), (D_MODEL, BF16)], out_accs=[(1, D_MODEL), (1, D_MODEL)])


def _norm_bwd(name, dy, x, g):
    def fn(i, tm, rows, prevs, nexts, pars):
        dx, dg = _rms_bwd(rows[0][...], rows[1][...], pars[0][...])
        return [dx], [dg]
    return _rowcall(name, fn, rows=[dy, x], pars=[g], out_rows=[(x.shape[1], BF16)], out_accs=[(1, x.shape[1])])


def _ssd_conv_bwd_act(dact, xbc_raw, w, b):
    def fn(i, tm, rows, prevs, nexts, pars):
        c = _conv_taps(prevs[0], pars[0], pars[1], SSD_CONV)
        s = jax.nn.sigmoid(c)
        dc = rows[0][...] * s * (1.0 + c * (1.0 - s))
        return [jnp.where(_row_ids(i, tm) >= PAD, dc, 0.0)], []
    return _rowcall("ssd_conv_bwd_act", fn, rows=[dact], prevs=[xbc_raw], pars=[w, b], out_rows=[(XBC, F32)])[0]


def _first_bwd(dxn1, h0, dh1, g):
    def fn(i, tm, rows, prevs, nexts, pars):
        d0, dg = _rms_bwd(rows[0][...], rows[1][...], pars[0][...])
        return [rows[2][...] + d0], [dg]
    return _rowcall("first_bwd", fn, rows=[dxn1, h0, dh1], pars=[g], out_rows=[(D_MODEL, F32)],
                    out_accs=[(1, D_MODEL)])


def _ssd_chunk_terms(c, dtr_ref, bias_ref, alog_ref):
    ri = lax.broadcasted_iota(jnp.int32, (BLK, BLK), 0)
    ci = lax.broadcasted_iota(jnp.int32, (BLK, BLK), 1)
    causal = ri >= ci
    tril = causal.astype(F32)
    triu = (ri <= ci).astype(F32)
    rowmask = ((c * BLK + lax.broadcasted_iota(jnp.int32, (BLK, 1), 0)) >= PAD).astype(F32)
    dt = _softplus(dtr_ref[...] + bias_ref[...]) * rowmask
    a_neg = -jnp.exp(alog_ref[...])
    a = dt * a_neg
    cs = _dot(tril, a, NN, HI)
    cs_t = _dot(a, triu, TN, HI)
    return causal, triu, rowmask, dt, a_neg, cs, cs_t


def _decay_matrix(causal, cs_h, cs_t_h):
    return jnp.where(causal, jnp.exp(jnp.where(causal, cs_h - cs_t_h, 0.0)), 0.0)


def _head_spread(width):
    ri = lax.broadcasted_iota(jnp.int32, (LANES, HEADS * width), 0)
    ci = lax.broadcasted_iota(jnp.int32, (LANES, HEADS * width), 1)
    return ((ri * width <= ci) & (ci < (ri + 1) * width)).astype(BF16)


def _three_terms(x):
    hi = x.astype(BF16)
    rest = x - hi.astype(F32)
    mid = rest.astype(BF16)
    lo = (rest - mid.astype(F32)).astype(BF16)
    return jnp.concatenate([hi, mid, lo], axis=1)


def _spread(x, sel):
    return _dot(_three_terms(x), jnp.concatenate([sel, sel, sel], axis=0))


def _lane_sums(y, sel):
    return _dot(_three_terms(y), jnp.concatenate([sel, sel, sel], axis=1), NT)


def _ssd_spreads(dt, cs, d_ref, sel64, sel128):
    dt64 = _spread(dt, sel64)
    cs64 = _spread(cs, sel64)
    cs128 = _spread(cs, sel128)
    d64 = _spread(jnp.broadcast_to(d_ref[...], (HALO, LANES)), sel64)[0:1, :]
    cl64 = cs64[BLK - 1:BLK, :]
    return dt64, cs128, d64, jnp.exp(cs64), jnp.exp(cl64 - cs64), jnp.exp(cl64)


def _ssd_pair_terms(p, xbc_ref, dt64, cs128, cs_t, causal):
    lanes = _pair_lanes(p)
    xs = xbc_ref[:, lanes]
    decays = [_decay_matrix(causal, cs128[:, h * LANES:(h + 1) * LANES], cs_t[h:h + 1, :]) for h in (2 * p, 2 * p + 1)]
    return xs, xs * dt64[:, lanes], decays


def _ssd_fwd(xbc, dtr, z, dt_bias, a_log, d_skip, norm_g):
    n_rows = xbc.shape[0]
    nb = n_rows // BLK

    def body(xbc_ref, dtr_ref, z_ref, bias_ref, alog_ref, d_ref, g_ref, s64_ref, s128_ref,
             ypre_ref, yssd_ref, st_ref, state):
        c = pl.program_id(0)

        @pl.when(c == 0)
        def _():
            state[...] = jnp.zeros_like(state)

        st_ref[...] = state[...]
        causal, _, _, dt, _, cs, cs_t = _ssd_chunk_terms(c, dtr_ref, bias_ref, alog_ref)
        first = lax.broadcasted_iota(jnp.int32, (BLK, LANES), 1) < HEAD_DIM
        dt64, cs128, d64, from_start, to_end, chunk_decay = _ssd_spreads(dt, cs, d_ref, s64_ref[...], s128_ref[...])
        for g in range(SSD_GROUPS):
            b_b = xbc_ref[:, OFF_B + g * SSD_STATE:OFF_B + (g + 1) * SSD_STATE].astype(BF16)
            c_b = xbc_ref[:, OFF_C + g * SSD_STATE:OFF_C + (g + 1) * SSD_STATE].astype(BF16)
            cb = _dot(c_b, b_b, NT)
            for j in range(HEADS_PER_GROUP // 2):
                p = g * (HEADS_PER_GROUP // 2) + j
                lanes = _pair_lanes(p)
                xs, x_dt, decays = _ssd_pair_terms(p, xbc_ref, dt64, cs128, cs_t, causal)
                ms = [(cb * d).astype(BF16) for d in decays]
                s_p = state[:, lanes]
                y = _dot(jnp.concatenate(ms, axis=1), _head_stack(x_dt.astype(BF16), first))
                y = y + from_start[:, lanes] * _dot(c_b, s_p.astype(BF16))
                state[:, lanes] = chunk_decay[:, lanes] * s_p + _dot(b_b, (to_end[:, lanes] * x_dt).astype(BF16), TN)
                ypre_ref[:, lanes] = y + d64[:, lanes] * xs
        zz = z_ref[...]
        yg = ypre_ref[...] * (zz * jax.nn.sigmoid(zz))
        yssd_ref[...] = _rms(yg, g_ref[...]).astype(yssd_ref.dtype)

    blk = lambda w: pl.BlockSpec((BLK, w), lambda c: (c, 0))
    par = lambda a: pl.BlockSpec(a.shape, lambda c: (0, 0))
    sel64, sel128 = _head_spread(HEAD_DIM), _head_spread(LANES)
    return pl.pallas_call(
        body, name="ssd_fwd", grid=(nb,),
        in_specs=[blk(XBC), blk(LANES), blk(SSD_INNER), par(dt_bias), par(a_log), par(d_skip), par(norm_g),
                  par(sel64), par(sel128)],
        out_specs=[blk(SSD_INNER), blk(SSD_INNER), pl.BlockSpec((None, SSD_STATE, SSD_INNER), lambda c: (c, 0, 0))],
        out_shape=[jax.ShapeDtypeStruct((n_rows, SSD_INNER), F32), jax.ShapeDtypeStruct((n_rows, SSD_INNER), BF16),
                   jax.ShapeDtypeStruct((nb, SSD_STATE, SSD_INNER), F32)],
        scratch_shapes=[pltpu.VMEM((SSD_STATE, SSD_INNER), F32)],
        compiler_params=_params(8 << 20, ("arbitrary",)),
    )(xbc, dtr, z, dt_bias, a_log, d_skip, norm_g, sel64, sel128)


def _ssd_bwd(dy, ypre, z, xbc, dtr, states, dt_bias, a_log, d_skip, norm_g):
    n_rows = xbc.shape[0]
    nb = n_rows // BLK

    def body(dy_ref, ypre_ref, z_ref, xbc_ref, dtr_ref, st_ref, bias_ref, alog_ref, d_ref, g_ref, s64_ref, s128_ref,
             dz_ref, dxbc_ref, ddtr_ref, dgn_ref, dd_ref, dal_ref, ddtb_ref, dstate, dyp, red):
        step = pl.program_id(0)
        c = nb - 1 - step

        @pl.when(step == 0)
        def _():
            dstate[...] = jnp.zeros_like(dstate)
            for r_ in (dgn_ref, dd_ref, dal_ref, ddtb_ref):
                r_[...] = jnp.zeros_like(r_)

        yp, zz = ypre_ref[...], z_ref[...]
        sz = jax.nn.sigmoid(zz)
        silu = zz * sz
        dyg, dgn = _rms_bwd(dy_ref[...], yp * silu, g_ref[...])
        dgn_ref[...] += dgn
        dz_ref[...] = (dyg * yp * (sz * (1.0 + zz * (1.0 - sz)))).astype(dz_ref.dtype)
        dyp[...] = dyg * silu

        causal, triu, rowmask, dt, a_neg, cs, cs_t = _ssd_chunk_terms(c, dtr_ref, bias_ref, alog_ref)
        lane = lax.broadcasted_iota(jnp.int32, (1, LANES), 1)
        last_row = (lax.broadcasted_iota(jnp.int32, (BLK, 1), 0) == BLK - 1).astype(F32)
        first = lax.broadcasted_iota(jnp.int32, (BLK, LANES), 1) < HEAD_DIM
        sel64 = s64_ref[...]
        dt64, cs128, d64, from_start, to_end, chunk_decay = _ssd_spreads(dt, cs, d_ref, sel64, s128_ref[...])
        for g in range(SSD_GROUPS):
            b_b = xbc_ref[:, OFF_B + g * SSD_STATE:OFF_B + (g + 1) * SSD_STATE].astype(BF16)
            c_b = xbc_ref[:, OFF_C + g * SSD_STATE:OFF_C + (g + 1) * SSD_STATE].astype(BF16)
            cb = _dot(c_b, b_b, NT)
            b_twice = jnp.concatenate([b_b, b_b], axis=0)
            c_twice = jnp.concatenate([c_b, c_b], axis=0)
            db_g = jnp.zeros((BLK, SSD_STATE), F32)
            dc_g = jnp.zeros((BLK, SSD_STATE), F32)
            for j in range(HEADS_PER_GROUP // 2):
                p = g * (HEADS_PER_GROUP // 2) + j
                lanes = _pair_lanes(p)
                xs, x_dt, decays = _ssd_pair_terms(p, xbc_ref, dt64, cs128, cs_t, causal)
                ms = [(cb * d).astype(BF16) for d in decays]
                d_y = dyp[:, lanes]
                s_p, ds_p = st_ref[:, lanes], dstate[:, lanes]
                s_b, ds_b = s_p.astype(BF16), ds_p.astype(BF16)
                fs, te = from_start[:, lanes], to_end[:, lanes]
                x_b, dy_b = x_dt.astype(BF16), d_y.astype(BF16)
                x_st = _head_stack(x_b, first)
                fs_dy = (fs * d_y).astype(BF16)

                y_diag = _dot(jnp.concatenate(ms, axis=1), x_st)
                y_off = fs * _dot(c_b, s_b)
                end_part = te * _dot(b_b, ds_b)
                dx_diag = _dot(jnp.concatenate(ms, axis=0), _head_stack(dy_b, first), TN)
                d_x = dx_diag + end_part
                g2 = _dot(dy_b, x_st, NT)
                gl = [(g2[:, k * BLK:(k + 1) * BLK] * decays[k]).astype(BF16) for k in range(2)]
                dc_g = dc_g + _dot(jnp.concatenate(gl, axis=1), b_twice) + _dot(fs_dy, s_b, NT)
                db_g = db_g + _dot(jnp.concatenate(gl, axis=0), c_twice, TN) + _dot((te * x_dt).astype(BF16), ds_b, NT)
                red[0:BLK, lanes] = (dy_b.astype(F32) * y_diag - x_b.astype(F32) * dx_diag) + (d_y * y_off - x_dt * end_part)
                red[BLK:2 * BLK, lanes] = x_dt * end_part
                red[2 * BLK:3 * BLK, lanes] = d_x * xs
                red[3 * BLK:3 * BLK + HALO, lanes] = jnp.broadcast_to(jnp.sum(ds_p * s_p, axis=0, keepdims=True), (HALO, LANES))
                red[3 * BLK + HALO:3 * BLK + 2 * HALO, lanes] = jnp.broadcast_to(
                    jnp.sum(d_y * xs, axis=0, keepdims=True), (HALO, LANES))
                dxbc_ref[:, lanes] = d_x * dt64[:, lanes] + d64[:, lanes] * d_y
                dstate[:, lanes] = chunk_decay[:, lanes] * ds_p + _dot(c_b, fs_dy, TN)
            dxbc_ref[:, OFF_B + g * SSD_STATE:OFF_B + (g + 1) * SSD_STATE] = db_g
            dxbc_ref[:, OFF_C + g * SSD_STATE:OFF_C + (g + 1) * SSD_STATE] = dc_g
        sums = _lane_sums(red[...], sel64)
        at_last = (jnp.sum(sums[BLK:2 * BLK], axis=0, keepdims=True)
                   + jnp.exp(cs[BLK - 1:BLK, :]) * sums[3 * BLK:3 * BLK + 1])
        dcs = sums[0:BLK] + last_row * at_last
        ddt_x = sums[2 * BLK:3 * BLK]
        dd_row = sums[3 * BLK + HALO:3 * BLK + HALO + 1]
        da = _dot(triu, dcs, NN, HI)
        ddt = (da * a_neg + ddt_x) * rowmask
        ddtr = ddt * jax.nn.sigmoid(dtr_ref[...] + bias_ref[...]) * (lane < HEADS).astype(F32)
        ddtr_ref[...] = ddtr.astype(ddtr_ref.dtype)
        ddtb_ref[...] += jnp.sum(ddtr, axis=0, keepdims=True)
        dal_ref[...] += jnp.sum(da * dt, axis=0, keepdims=True) * a_neg
        dd_ref[...] += dd_row

    blk = lambda w: pl.BlockSpec((BLK, w), lambda s: (nb - 1 - s, 0))
    par = lambda a: pl.BlockSpec(a.shape, lambda s: (0, 0))
    acc = lambda w: pl.BlockSpec((1, w), lambda s: (0, 0))
    sel64, sel128 = _head_spread(HEAD_DIM), _head_spread(LANES)
    return pl.pallas_call(
        body, name="ssd_bwd", grid=(nb,),
        in_specs=[blk(SSD_INNER), blk(SSD_INNER), blk(SSD_INNER), blk(XBC), blk(LANES),
                  pl.BlockSpec((None, SSD_STATE, SSD_INNER), lambda s: (nb - 1 - s, 0, 0)),
                  par(dt_bias), par(a_log), par(d_skip), par(norm_g), par(sel64), par(sel128)],
        out_specs=[blk(SSD_INNER), blk(XBC), blk(LANES), acc(SSD_INNER), acc(LANES), acc(LANES), acc(LANES)],
        out_shape=[jax.ShapeDtypeStruct((n_rows, SSD_INNER), BF16), jax.ShapeDtypeStruct((n_rows, XBC), F32),
                   jax.ShapeDtypeStruct((n_rows, LANES), BF16), jax.ShapeDtypeStruct((1, SSD_INNER), F32),
                   jax.ShapeDtypeStruct((1, LANES), F32), jax.ShapeDtypeStruct((1, LANES), F32),
                   jax.ShapeDtypeStruct((1, LANES), F32)],
        scratch_shapes=[pltpu.VMEM((SSD_STATE, SSD_INNER), F32), pltpu.VMEM((BLK, SSD_INNER), F32),
                        pltpu.VMEM((3 * BLK + 2 * HALO, SSD_INNER), F32)],
        compiler_params=_params(12 << 20, ("arbitrary",)),
    )(dy, ypre, z, xbc, dtr, states, dt_bias, a_log, d_skip, norm_g, sel64, sel128)


HEAD_GROUP = 4


LOG2_E = 1.4426950408889634
SOFTPLUS_CLAMP = 80.0


def _sb_logits(zl, valid):
    z2 = zl * LOG2_E
    lost = jnp.maximum(jnp.log2(1.0 + jnp.exp2(jnp.minimum(z2, SOFTPLUS_CLAMP))), z2)
    log_beta = z2 - lost
    if valid is not None:
        lost = jnp.where(valid, lost, 0.0)
    return log_beta, lost


def _tri_and_ones(tri, sign=1.0):
    half = sign * jnp.concatenate([tri, jnp.ones((BLK, BLK), F32)], axis=1)
    return jnp.concatenate([half, half], axis=0).astype(BF16)


def _tile_mask(i, j, ri, ci):
    key = j * BLK + ci
    return (key < i * BLK + ri) & (key >= PAD)


def _pair_lanes(p):
    return slice(p * 2 * HEAD_DIM, (p + 1) * 2 * HEAD_DIM)


def _head_stack(x_pair, first):
    zero = jnp.zeros_like(x_pair)
    return jnp.concatenate([jnp.where(first, x_pair, zero), jnp.where(first, zero, x_pair)], axis=0)


def _block_rows(j):
    return pl.ds(j * BLK if isinstance(j, int) else pl.multiple_of(j * BLK, BLK), BLK)


def _sweep_pairs(i, tile):
    n_calls = (i + 2) // 2
    tile(0, True)

    def mid(t, carry):
        tile(t, False)
        return carry

    lax.fori_loop(1, n_calls - 1, mid, 0)

    @pl.when(n_calls >= 2)
    def _():
        tile(n_calls - 1, True)

    return n_calls


def _two_terms(x):
    hi = x.astype(BF16)
    return jnp.concatenate([hi, (x - hi.astype(F32)).astype(BF16)], axis=1)


def _ride_along(ex, n_in, n_out, refs):
    k = 0
    parts = []
    for cnt in (n_in, ex.n, n_out, ex.n):
        parts.append(refs[k:k + cnt])
        k += cnt
    n_sems = len(ex.scratch)
    return (*parts, refs[k:len(refs) - n_sems], refs[len(refs) - n_sems:])


def _attn_fwd(q, k, v, ex, ex_arrays):
    n_rows, width = q.shape
    nb = n_rows // BLK
    gw = HEAD_GROUP * HEAD_DIM
    groups = width // gw

    def body(*refs):
        (q_ref, k_ref, v_ref), ex_in, (o_ref, tot_ref), ex_out, (run_ref, z_ref, w_ref), sems = _ride_along(ex, 3, 2, refs)
        step = pl.program_id(0)

        @pl.when(step == 0)
        def _():
            ex.start(ex_in, ex_out, sems)

        ri = lax.broadcasted_iota(jnp.int32, (BLK, BLK), 0)
        ci = lax.broadcasted_iota(jnp.int32, (BLK, BLK), 1)
        sums = _tri_and_ones((ri > ci).astype(F32), -1.0)
        first = ci < HEAD_DIM
        heads, pairs = range(HEAD_GROUP), range(HEAD_GROUP // 2)

        def stacks(ref, blocks, p):
            return jnp.concatenate([_head_stack(ref[_block_rows(jnp.clip(j, 0, nb - 1)), _pair_lanes(p)], first)
                                    for j in blocks], axis=0)

        def q_block(i, carry):
            rows = _block_rows(i)
            o_ref[rows, :] = jnp.zeros((BLK, gw), F32)
            run_ref[...] = jnp.zeros_like(run_ref)
            w_ref[...] = jnp.zeros_like(w_ref)

            def blocks(t):
                return i - 2 * t, i - 2 * t - 1

            for p in pairs:
                z_ref[p] = _dot(q_ref[rows, _pair_lanes(p)], stacks(k_ref, blocks(0), p), NT)

            def tile(t, masked):
                q_i, o_i = q_ref[rows, :], o_ref[rows, :]
                outs = [_dot(w_ref[p], stacks(v_ref, blocks(jnp.maximum(t - 1, 0)), p)) for p in pairs]
                z_next = [_dot(q_i[:, _pair_lanes(p)], stacks(k_ref, blocks(t + 1), p), NT) for p in pairs]
                runs = [run_ref[h] for h in heads]
                lgs, res = [], []
                for s, j in enumerate(blocks(t)):
                    valid = _tile_mask(i, j, ri, ci) if masked else None
                    lgs.append([_sb_logits(z_ref[h // 2][:, (2 * s + h % 2) * BLK:(2 * s + h % 2 + 1) * BLK], valid)
                                for h in heads])
                    res.append(_dot(jnp.concatenate([_two_terms(lgs[s][h][1]) for h in heads], axis=0), sums))
                ws = []
                for s, j in enumerate(blocks(t)):
                    valid = _tile_mask(i, j, ri, ci) if masked else None
                    for h in heads:
                        part = res[s][h * BLK:(h + 1) * BLK]
                        w = jnp.exp2(lgs[s][h][0] + part[:, :BLK] + runs[h])
                        ws.append(jnp.where(valid, w, 0.0) if masked else w)
                        runs[h] = runs[h] + part[:, BLK:]
                for p in pairs:
                    w_ref[p] = jnp.concatenate([ws[s * HEAD_GROUP + 2 * p + e].astype(BF16) for s in range(2) for e in range(2)],
                                               axis=1)
                    z_ref[p] = z_next[p]
                o_ref[rows, :] = o_i + jnp.concatenate(outs, axis=1)
                for h in heads:
                    run_ref[h] = runs[h]

            n_calls = _sweep_pairs(i, tile)
            o_ref[rows, :] += jnp.concatenate([_dot(w_ref[p], stacks(v_ref, blocks(n_calls - 1), p)) for p in pairs], axis=1)
            for h in heads:
                tot_ref[rows, h:h + 1] = run_ref[h][:, 0:1]
            return carry

        lax.fori_loop(0, nb, q_block, 0)

        @pl.when(step == groups - 1)
        def _():
            ex.wait(ex_in, ex_out, sems)

    spec = pl.BlockSpec((n_rows, gw), lambda g: (0, g))
    tot_spec = pl.BlockSpec((None, n_rows, HEAD_GROUP), lambda g: (g, 0, 0))
    res = pl.pallas_call(
        body, name="attn_fwd", grid=(groups,), in_specs=[spec, spec, spec] + ex.specs,
        out_specs=[spec, tot_spec] + ex.specs,
        out_shape=[jax.ShapeDtypeStruct((n_rows, width), F32),
                   jax.ShapeDtypeStruct((groups, n_rows, HEAD_GROUP), F32)] + ex.out_shape,
        scratch_shapes=[pltpu.VMEM((HEAD_GROUP, BLK, BLK), F32), pltpu.VMEM((HEAD_GROUP // 2, BLK, 4 * BLK), F32),
                        pltpu.VMEM((HEAD_GROUP // 2, BLK, 4 * BLK), BF16)] + ex.scratch,
        compiler_params=_params(n_rows * (3 * gw * 2 + gw * 4 + LANES * 4), ("arbitrary",)),
    )(q, k, v, *ex_arrays)
    return res[0], res[1], res[2:]


def _attn_bwd(q, k, v, keep_total, do, ex, ex_arrays):
    n_rows, width = q.shape
    nb = n_rows // BLK
    gw = HEAD_GROUP * HEAD_DIM
    groups = width // gw

    def body(*refs):
        ((q_ref, k_ref, v_ref, tot_ref, do_ref), ex_in, (dq_ref, dk_ref, dv_ref), ex_out,
         (dq_acc, dk_acc, dv_acc, tot_b, run_ref, rung_ref, z_ref, dw_ref, dz_ref, wb_ref, qst_ref, dost_ref),
         sems) = _ride_along(ex, 5, 3, refs)
        step = pl.program_id(0)

        @pl.when(step == 0)
        def _():
            ex.start(ex_in, ex_out, sems)

        ri = lax.broadcasted_iota(jnp.int32, (BLK, BLK), 0)
        ci = lax.broadcasted_iota(jnp.int32, (BLK, BLK), 1)
        sums_keep = _tri_and_ones((ri <= ci).astype(F32), -1.0)
        sums_g = _tri_and_ones((ri < ci).astype(F32))
        first = ci < HEAD_DIM
        heads, pairs = range(HEAD_GROUP), range(HEAD_GROUP // 2)
        dk_acc[...] = jnp.zeros_like(dk_acc)
        dv_acc[...] = jnp.zeros_like(dv_acc)

        def clamp(j):
            return jnp.clip(j, 0, nb - 1)

        def stacks(ref, blocks, p):
            return jnp.concatenate([_head_stack(ref[_block_rows(clamp(j)), _pair_lanes(p)], first) for j in blocks], axis=0)

        def blocks(t):
            return 2 * t, 2 * t + 1

        def kept(ref, p, axis):
            tiles = [[ref[s * HEAD_GROUP + 2 * p + e] for e in range(2)] for s in range(2)]
            if axis == 1:
                return jnp.concatenate(tiles[0] + tiles[1], axis=1)
            return jnp.concatenate([jnp.concatenate(tiles[s], axis=0) for s in range(2)], axis=1)

        def owed_dq(t_prev):
            return [_dot(kept(dz_ref, p, 1), stacks(k_ref, blocks(t_prev), p)) for p in pairs]

        def owed_dk():
            return [_dot(kept(dz_ref, p, 0), qst_ref[p], TN) for p in pairs]

        def owed_dv():
            return [_dot(kept(wb_ref, p, 0), dost_ref[p], TN) for p in pairs]

        def settle(t_prev, parts):
            dq, dk, dv = parts
            dq_acc[...] += jnp.concatenate(dq, axis=1)
            for s, j in enumerate(blocks(t_prev)):
                cols = _block_rows(clamp(j))
                dk_acc[cols, :] += jnp.concatenate([d[s * BLK:(s + 1) * BLK] for d in dk], axis=1)
                dv_acc[cols, :] += jnp.concatenate([d[s * BLK:(s + 1) * BLK] for d in dv], axis=1)

        def q_block(i, carry):
            rows = _block_rows(i)
            dq_acc[...] = jnp.zeros_like(dq_acc)
            run_ref[...] = jnp.zeros_like(run_ref)
            rung_ref[...] = jnp.zeros_like(rung_ref)
            dz_ref[...] = jnp.zeros_like(dz_ref)
            wb_ref[...] = jnp.zeros_like(wb_ref)
            for h in heads:
                tot_b[h] = jnp.broadcast_to(tot_ref[rows, h:h + 1], (BLK, BLK))
            for p in pairs:
                qst_ref[p] = _head_stack(q_ref[rows, _pair_lanes(p)], first)
                dost_ref[p] = _head_stack(do_ref[rows, _pair_lanes(p)], first)
                z_ref[p] = _dot(q_ref[rows, _pair_lanes(p)], stacks(k_ref, blocks(0), p), NT)
                dw_ref[p] = _dot(do_ref[rows, _pair_lanes(p)], stacks(v_ref, blocks(0), p), NT)

            def tile(t, masked):
                q_i, do_i = q_ref[rows, :], do_ref[rows, :]
                t_prev = jnp.maximum(t - 1, 0)
                valids = [_tile_mask(i, j, ri, ci) if masked else None for j in blocks(t)]
                tile_of = lambda ref, s, h: ref[h // 2][:, (2 * s + h % 2) * BLK:(2 * s + h % 2 + 1) * BLK]
                runs = [run_ref[h] for h in heads]
                rungs = [rung_ref[h] for h in heads]
                lgs, keep, ws, gs, gsum, dzs = [], [], [], [], [], []
                for s in range(2):
                    lgs.append([_sb_logits(tile_of(z_ref, s, h), valids[s]) for h in heads])
                    keep.append(_dot(jnp.concatenate([_two_terms(lgs[s][h][1]) for h in heads], axis=0), sums_keep))
                    if s == 0:
                        part_dq = owed_dq(t_prev)
                    else:
                        part_dk = owed_dk()
                for s in range(2):
                    ws.append([])
                    gs.append([])
                    for h in heads:
                        part = keep[s][h * BLK:(h + 1) * BLK]
                        w = jnp.exp2(lgs[s][h][0] + (tot_b[h] - runs[h] - part[:, :BLK]))
                        ws[s].append(jnp.where(valids[s], w, 0.0) if masked else w)
                        runs[h] = runs[h] + part[:, BLK:]
                        gs[s].append(tile_of(dw_ref, s, h) * ws[s][h])
                    gsum.append(_dot(jnp.concatenate([_two_terms(gs[s][h]) for h in heads], axis=0), sums_g))
                    if s == 0:
                        part_dv = owed_dv()
                    else:
                        z_next = [_dot(q_i[:, _pair_lanes(p)], stacks(k_ref, blocks(t + 1), p), NT) for p in pairs]
                for s in range(2):
                    dzs.append([])
                    for h in heads:
                        part = gsum[s][h * BLK:(h + 1) * BLK]
                        beta = jnp.exp2(lgs[s][h][0])
                        dz = gs[s][h] * (1.0 - beta) - beta * (part[:, :BLK] + rungs[h])
                        dzs[s].append(jnp.where(valids[s], dz, 0.0) if masked else dz)
                        rungs[h] = rungs[h] + part[:, BLK:]
                    if s == 0:
                        dw_next = [_dot(do_i[:, _pair_lanes(p)], stacks(v_ref, blocks(t + 1), p), NT) for p in pairs]
                settle(t_prev, (part_dq, part_dk, part_dv))
                for s in range(2):
                    for h in heads:
                        dz_ref[s * HEAD_GROUP + h] = dzs[s][h].astype(BF16)
                        wb_ref[s * HEAD_GROUP + h] = ws[s][h].astype(BF16)
                for h in heads:
                    run_ref[h] = runs[h]
                    rung_ref[h] = rungs[h]
                for p in pairs:
                    z_ref[p] = z_next[p]
                    dw_ref[p] = dw_next[p]

            n_calls = _sweep_pairs(i, tile)
            settle(n_calls - 1, (owed_dq(n_calls - 1), owed_dk(), owed_dv()))
            dq_ref[rows, :] = (dq_acc[...] * SB_SCALE).astype(dq_ref.dtype)
            return carry

        lax.fori_loop(0, nb, q_block, 0)
        dk_ref[...] = dk_acc[...].astype(dk_ref.dtype)
        dv_ref[...] = dv_acc[...].astype(dv_ref.dtype)

        @pl.when(step == groups - 1)
        def _():
            ex.wait(ex_in, ex_out, sems)

    spec = pl.BlockSpec((n_rows, gw), lambda g: (0, g))
    tot_spec = pl.BlockSpec((None, n_rows, HEAD_GROUP), lambda g: (g, 0, 0))
    out = jax.ShapeDtypeStruct((n_rows, width), BF16)
    tile_f32 = pltpu.VMEM((HEAD_GROUP, BLK, BLK), F32)
    tile_bf16 = pltpu.VMEM((2 * HEAD_GROUP, BLK, BLK), BF16)
    pair_f32 = pltpu.VMEM((HEAD_GROUP // 2, BLK, 4 * BLK), F32)
    pair_stack = pltpu.VMEM((HEAD_GROUP // 2, 2 * BLK, BLK), BF16)
    res = pl.pallas_call(
        body, name="attn_bwd", grid=(groups,), in_specs=[spec, spec, spec, tot_spec, spec] + ex.specs,
        out_specs=[spec] * 3 + ex.specs, out_shape=[out] * 3 + ex.out_shape,
        scratch_shapes=[pltpu.VMEM((BLK, gw), F32), pltpu.VMEM((n_rows, gw), F32), pltpu.VMEM((n_rows, gw), F32),
                        tile_f32, tile_f32, tile_f32, pair_f32, pair_f32, tile_bf16, tile_bf16, pair_stack,
                        pair_stack] + ex.scratch,
        compiler_params=_params(n_rows * (7 * gw * 2 + LANES * 4 + gw * 4), ("arbitrary",)),
    )(q, k, v, keep_total, do, *ex_arrays)
    return res[0], res[1], res[2], res[3:]


class _Exchange:
    def __init__(self, arrays, gather):
        self.n = len(arrays)
        self.gather = gather
        self.out_shape = [jax.ShapeDtypeStruct(((N_DEV,) + a.shape) if gather else a.shape, a.dtype) for a in arrays]
        self.scratch = [pltpu.SemaphoreType.DMA((self.n, N_DEV - 1)), pltpu.SemaphoreType.DMA((self.n, N_DEV - 1)),
                        pltpu.SemaphoreType.DMA((self.n,))]
        self.specs = [pl.BlockSpec(memory_space=pl.ANY)] * self.n

    def _copies(self, ins, outs, sems):
        send_sems, recv_sems, local_sems = sems
        x, y, c = lax.axis_index("x"), lax.axis_index("y"), lax.axis_index("c")
        me = 4 * x + 2 * y + c
        gather = self.gather
        local, sends, recvs = [], [], []
        for a in range(self.n):
            local.append(pltpu.make_async_copy(ins[a] if gather else ins[a].at[me], outs[a].at[me if gather else 0],
                                               local_sems.at[a]))
        for r in range(1, N_DEV):
            px = 1 - x if r & 4 else x
            py = 1 - y if r & 2 else y
            pc = 1 - c if r & 1 else c
            idx = 4 * px + 2 * py + pc
            for a in range(self.n):
                src = ins[a] if gather else ins[a].at[idx]
                pair = dict(send_sem=send_sems.at[a, r - 1], recv_sem=recv_sems.at[a, r - 1],
                            device_id=(px, py, pc), device_id_type=pl.DeviceIdType.MESH)
                sends.append(pltpu.make_async_remote_copy(src_ref=src, dst_ref=outs[a].at[me if gather else r], **pair))
                recvs.append(pltpu.make_async_remote_copy(src_ref=src, dst_ref=outs[a].at[idx if gather else r], **pair))
        return local, sends, recvs

    def start(self, ins, outs, sems):
        local, sends, _ = self._copies(ins, outs, sems)
        for cp in local + sends:
            cp.start()

    def wait(self, ins, outs, sems):
        local, sends, recvs = self._copies(ins, outs, sems)
        for cp in recvs:
            cp.wait_recv()
        for cp in sends:
            cp.wait_send()
        for cp in local:
            cp.wait()


def _peer_exchange(name, arrays, gather):
    ex = _Exchange(arrays, gather)
    n = ex.n

    def body(*refs):
        ins, outs, sems = refs[:n], refs[n:2 * n], refs[2 * n:]
        ex.start(ins, outs, sems)
        ex.wait(ins, outs, sems)

    return pl.pallas_call(body, name=name, in_specs=ex.specs, out_specs=ex.specs, out_shape=ex.out_shape,
                          scratch_shapes=ex.scratch)(*arrays)


def _sum_slots(name, x):
    def body(x_ref, o_ref):
        acc = x_ref[0]
        for s in range(1, N_DEV):
            acc = acc + x_ref[s]
        o_ref[...] = acc
    return pl.pallas_call(body, name=name, out_shape=jax.ShapeDtypeStruct(x.shape[1:], F32))(x)


def _adamw(name, w, slots, m, v):
    n_slots, rows, cols = slots.shape
    tr = next((t for t in (256, 176, 128) if rows % t == 0 and rows > t), rows)

    def body(w_ref, s_ref, m_ref, v_ref, g_ref, d_ref, nm_ref, nv_ref):
        g = s_ref[0].astype(F32)
        for s in range(1, n_slots):
            g = g + s_ref[s].astype(F32)
        nm = ADAM_B1 * m_ref[...] + (1.0 - ADAM_B1) * g
        nv = ADAM_B2 * v_ref[...] + (1.0 - ADAM_B2) * (g * g)
        m_hat = nm / (1.0 - ADAM_B1 ** ADAM_STEP)
        v_hat = nv / (1.0 - ADAM_B2 ** ADAM_STEP)
        g_ref[...] = g
        d_ref[...] = -ADAM_LR * (m_hat / (jnp.sqrt(v_hat) + ADAM_EPS) + ADAM_WD * w_ref[...])
        nm_ref[...] = nm
        nv_ref[...] = nv

    spec = pl.BlockSpec((tr, cols), lambda i: (i, 0))
    out = jax.ShapeDtypeStruct((rows, cols), F32)
    return pl.pallas_call(
        body, name=name, grid=(rows // tr,),
        in_specs=[spec, pl.BlockSpec((n_slots, tr, cols), lambda i: (0, i, 0)), spec, spec],
        out_specs=[spec] * 4, out_shape=[out] * 4,
        compiler_params=_params((n_slots + 7) * tr * cols * 4, ("parallel",)),
    )(w, slots, m, v)


def _pad_lanes(a):
    return jnp.pad(a, ((0, 0), (0, LANES - a.shape[1])))


def kernel(x, meta_tokens, mix_pre_g, w_in, ssd_conv_w, ssd_conv_b, ssd_dt_bias, ssd_a_log, ssd_d, ssd_norm_g, sb_norm_g, w_out, mix_post_g, ffn_pre_g, w_up, ffn_conv_w, ffn_conv_b, w_down, ffn_post_g, loss_target, m_meta_tokens, m_mix_pre_g, m_w_in, m_ssd_conv_w, m_ssd_conv_b, m_ssd_dt_bias, m_ssd_a_log, m_ssd_d, m_ssd_norm_g, m_sb_norm_g, m_w_out, m_mix_post_g, m_ffn_pre_g, m_w_up, m_ffn_conv_w, m_ffn_conv_b, m_w_down, m_ffn_post_g, v_meta_tokens, v_mix_pre_g, v_w_in, v_ssd_conv_w, v_ssd_conv_b, v_ssd_dt_bias, v_ssd_a_log, v_ssd_d, v_ssd_norm_g, v_sb_norm_g, v_w_out, v_mix_post_g, v_ffn_pre_g, v_w_up, v_ffn_conv_w, v_ffn_conv_b, v_w_down, v_ffn_post_g):
    seq = x.shape[1]
    me = 4 * lax.axis_index("x") + 2 * lax.axis_index("y") + lax.axis_index("c")
    in_cols = w_in.shape[2]
    up_cols = w_up.shape[2]
    out_rows = w_out.shape[1]
    down_rows = w_down.shape[1]

    g_in, g_meta, g_scw, g_fcw = _peer_exchange(
        "gather_w_in", [w_in[0].astype(BF16), meta_tokens, ssd_conv_w[0], ffn_conv_w[0]], gather=True)
    late_weights = [w_out[0].astype(BF16), w_up[0].astype(BF16), w_down[0].astype(BF16)]
    w_in_full = g_in.transpose(1, 0, 2).reshape(D_MODEL, N_DEV * in_cols)
    off = [0, SSD_INNER, SSD_INNER + XBC, SSD_INNER + XBC + HEADS]
    w_z = w_in_full[:, off[0]:off[1]]
    w_xbc = w_in_full[:, off[1]:off[2]]
    w_dt = _pad_lanes(w_in_full[:, off[2]:off[3]])
    w_q = w_in_full[:, off[3]:off[3] + SSD_INNER]
    w_k = w_in_full[:, off[3] + SSD_INNER:off[3] + 2 * SSD_INNER]
    w_v = w_in_full[:, off[3] + 2 * SSD_INNER:off[3] + 3 * SSD_INNER]
    meta_full = g_meta.transpose(1, 0, 2).reshape(N_META, D_MODEL)
    scw_full = g_scw.transpose(1, 0, 2).reshape(SSD_CONV, XBC)
    fcw_full = g_fcw.transpose(1, 0, 2).reshape(FFN_CONV, D_FF)

    dt_bias_p, a_log_p, d_p = _pad_lanes(ssd_dt_bias), _pad_lanes(ssd_a_log), _pad_lanes(ssd_d)

    h0 = jnp.concatenate([jnp.zeros((PAD, D_MODEL), F32), meta_full, x[0]], axis=0)
    target = jnp.concatenate([jnp.zeros((BLK, D_MODEL), F32), loss_target[0]], axis=0)
    xn1 = _rms_fwd("rms_pre_mix", h0, mix_pre_g)
    z = _mm("proj_z", [(xn1, w_z)], "nn", F32)
    xbc_raw = _mm("proj_xbc", [(xn1, w_xbc)], "nn", F32)
    dtr = _mm("proj_dt", [(xn1, w_dt)], "nn", F32)
    q = _mm("proj_q", [(xn1, w_q * SB_SCALE)], "nn", BF16)
    k = _mm("proj_k", [(xn1, w_k)], "nn", BF16)
    v = _mm("proj_v", [(xn1, w_v)], "nn", BF16)
    xbc_act = _ssd_conv_fwd(xbc_raw, scw_full, ssd_conv_b)
    ypre, y_ssd, states = _ssd_fwd(xbc_act, dtr, z, dt_bias_p, a_log_p, d_p, ssd_norm_g)
    o, keep_total, (g_out, g_up, g_down) = _attn_fwd(q, k, v, _Exchange(late_weights, gather=True), late_weights)
    w_out_full = g_out.reshape(N_DEV * out_rows, D_MODEL)
    wo_ssd, wo_sb = w_out_full[:SSD_INNER], w_out_full[SSD_INNER:]
    w_up_full = g_up.transpose(1, 0, 2).reshape(D_MODEL, N_DEV * up_cols)
    w_gate, w_lin = w_up_full[:, :D_FF], w_up_full[:, D_FF:]
    w_down_full = g_down.reshape(N_DEV * down_rows, D_MODEL)
    y_sb = _rms_fwd("rms_sb", o, sb_norm_g)
    mix = _mm("mix_out", [(y_ssd, wo_ssd), (y_sb, wo_sb)], "nn", F32)
    h1, xn2 = _mix_post(mix, h0, mix_post_g, ffn_pre_g)
    g_raw = _mm("ffn_gate", [(xn2, w_gate)], "nn", F32)
    u = _mm("ffn_lin", [(xn2, w_lin)], "nn", F32)
    act = _ffn_act(g_raw, u, fcw_full, ffn_conv_b)
    f = _mm("ffn_down", [(act, w_down_full)], "nn", F32)
    dh2, df, loss_row, dg_ffn_post = _loss_post(f, h1, target, ffn_post_g)
    loss = lax.psum(loss_row[0, 0], ("x", "y", "c"))

    dact = _mm("d_act", [(df, w_down_full)], "nt", F32)
    dw_down = _mm("dw_down", [(act, df)], "tn", F32)
    dg_conv, du = _ffn_bwd_act(dact, u, g_raw, fcw_full, ffn_conv_b)
    dg_raw, dfcw0, dfcw1, dfcw2, dfcb = _conv_bwd("ffn_conv_bwd", dg_conv, g_raw, fcw_full, FFN_CONV)
    dxn2 = _mm("d_xn2", [(dg_raw, w_gate), (du, w_lin)], "nt", F32)
    dw_gate = _mm("dw_gate", [(xn2, dg_raw)], "tn", F32)
    dw_lin = _mm("dw_lin", [(xn2, du)], "tn", F32)
    dh1, dmix, dg_ffn_pre, dg_mix_post = _mid_bwd(dxn2, h1, dh2, mix, ffn_pre_g, mix_post_g)

    dy_ssd = _mm("d_yssd", [(dmix, wo_ssd)], "nt", F32)
    dy_sb = _mm("d_ysb", [(dmix, wo_sb)], "nt", F32)
    dwo_ssd = _mm("dw_out_ssd", [(y_ssd, dmix)], "tn", F32)
    dwo_sb = _mm("dw_out_sb", [(y_sb, dmix)], "tn", F32)
    do, dg_sb = _norm_bwd("sb_norm_bwd", dy_sb, o, sb_norm_g)
    half = N_DEV // 2
    early_slabs = [
        jnp.concatenate([dwo_ssd, dwo_sb], axis=0).reshape(N_DEV, out_rows, D_MODEL),
        jnp.concatenate([dw_gate.reshape(D_MODEL, half, up_cols).transpose(1, 0, 2),
                         dw_lin.reshape(D_MODEL, half, up_cols).transpose(1, 0, 2)], axis=0),
        dw_down.reshape(N_DEV, down_rows, D_MODEL)]
    dq, dk, dv, (l_out, l_up, l_down) = _attn_bwd(q, k, v, keep_total, do, _Exchange(early_slabs, gather=False), early_slabs)
    dz, dxbc_act, ddtr, dg_ssd_norm, dd_skip, da_log, ddt_bias = _ssd_bwd(
        dy_ssd, ypre, z, xbc_act, dtr, states, dt_bias_p, a_log_p, d_p, ssd_norm_g)
    dconv = _ssd_conv_bwd_act(dxbc_act, xbc_raw, scw_full, ssd_conv_b)
    dxbc_raw, dscw0, dscw1, dscw2, dscw3, dscb = _conv_bwd("ssd_conv_bwd", dconv, xbc_raw, scw_full, SSD_CONV)
    segs = [(dz, w_z), (dxbc_raw, w_xbc), (ddtr, w_dt), (dq, w_q), (dk, w_k), (dv, w_v)]
    dxn1 = _mm("d_xn1", segs, "nt", F32)
    dw_segs = [_mm("dw_in_%d" % s, [(xn1, d)], "tn", BF16) for s, (d, _) in enumerate(segs)]
    dw_segs[2] = dw_segs[2][:, :HEADS]
    dw_in = jnp.concatenate(dw_segs, axis=1)
    dh0, dg_mix_pre = _first_bwd(dxn1, h0, dh1, mix_pre_g)
    grad_x = dh0[BLK:][None]

    slab_in = dw_in.reshape(D_MODEL, N_DEV, in_cols).transpose(1, 0, 2)
    (l_in,) = _peer_exchange("scatter_dw_in", [slab_in], gather=False)

    small = [dg_mix_pre, dscb, ddt_bias, da_log, dd_skip, dg_ssd_norm, dg_sb, dg_mix_post, dg_ffn_pre, dfcb,
             dg_ffn_post, dh0[PAD:BLK].reshape(1, -1), dscw0, dscw1, dscw2, dscw3, dfcw0, dfcw1, dfcw2]
    sizes = [a.shape[1] for a in small]
    total = sum(sizes)
    rows_packed = -(-total // (LANES * HALO)) * HALO
    packed = jnp.pad(jnp.concatenate(small, axis=1), ((0, 0), (0, rows_packed * LANES - total)))
    (gathered,) = _peer_exchange("gather_small_grads", [packed.reshape(rows_packed, LANES)], gather=True)
    summed = _sum_slots("sum_small_grads", gathered).reshape(1, rows_packed * LANES)
    pieces, at = [], 0
    for s in sizes:
        pieces.append(summed[:, at:at + s])
        at += s
    (g_mix_pre, g_scb, g_dtb, g_alog, g_dskip, g_ssd_norm, g_sb, g_mix_post, g_ffn_pre, g_fcb, g_ffn_post,
     g_meta_flat, gs0, gs1, gs2, gs3, gf0, gf1, gf2) = pieces
    g_dtb, g_alog, g_dskip = g_dtb[:, :HEADS], g_alog[:, :HEADS], g_dskip[:, :HEADS]
    g_meta_full = g_meta_flat.reshape(N_META, D_MODEL)
    g_scw_full = jnp.concatenate([gs0, gs1, gs2, gs3], axis=0)
    g_fcw_full = jnp.concatenate([gf0, gf1, gf2], axis=0)
    meta_cols, scw_cols, fcw_cols = meta_tokens.shape[1], ssd_conv_w.shape[2], ffn_conv_w.shape[2]
    g_meta_mine = lax.dynamic_slice(g_meta_full, (0, me * meta_cols), (N_META, meta_cols))
    g_scw_mine = lax.dynamic_slice(g_scw_full, (0, me * scw_cols), (SSD_CONV, scw_cols))
    g_fcw_mine = lax.dynamic_slice(g_fcw_full, (0, me * fcw_cols), (FFN_CONV, fcw_cols))

    def lead(a):
        return a[None]

    upd = [
        _adamw("adamw_meta", meta_tokens, lead(g_meta_mine), m_meta_tokens, v_meta_tokens),
        _adamw("adamw_mix_pre_g", mix_pre_g, lead(g_mix_pre), m_mix_pre_g, v_mix_pre_g),
        [lead(a) for a in _adamw("adamw_w_in", w_in[0], l_in, m_w_in[0], v_w_in[0])],
        [lead(a) for a in _adamw("adamw_ssd_conv_w", ssd_conv_w[0], lead(g_scw_mine), m_ssd_conv_w[0], v_ssd_conv_w[0])],
        _adamw("adamw_ssd_conv_b", ssd_conv_b, lead(g_scb), m_ssd_conv_b, v_ssd_conv_b),
        _adamw("adamw_ssd_dt_bias", ssd_dt_bias, lead(g_dtb), m_ssd_dt_bias, v_ssd_dt_bias),
        _adamw("adamw_ssd_a_log", ssd_a_log, lead(g_alog), m_ssd_a_log, v_ssd_a_log),
        _adamw("adamw_ssd_d", ssd_d, lead(g_dskip), m_ssd_d, v_ssd_d),
        _adamw("adamw_ssd_norm_g", ssd_norm_g, lead(g_ssd_norm), m_ssd_norm_g, v_ssd_norm_g),
        _adamw("adamw_sb_norm_g", sb_norm_g, lead(g_sb), m_sb_norm_g, v_sb_norm_g),
        [lead(a) for a in _adamw("adamw_w_out", w_out[0], l_out, m_w_out[0], v_w_out[0])],
        _adamw("adamw_mix_post_g", mix_post_g, lead(g_mix_post), m_mix_post_g, v_mix_post_g),
        _adamw("adamw_ffn_pre_g", ffn_pre_g, lead(g_ffn_pre), m_ffn_pre_g, v_ffn_pre_g),
        [lead(a) for a in _adamw("adamw_w_up", w_up[0], l_up, m_w_up[0], v_w_up[0])],
        [lead(a) for a in _adamw("adamw_ffn_conv_w", ffn_conv_w[0], lead(g_fcw_mine), m_ffn_conv_w[0], v_ffn_conv_w[0])],
        _adamw("adamw_ffn_conv_b", ffn_conv_b, lead(g_fcb), m_ffn_conv_b, v_ffn_conv_b),
        [lead(a) for a in _adamw("adamw_w_down", w_down[0], l_down, m_w_down[0], v_w_down[0])],
        _adamw("adamw_ffn_post_g", ffn_post_g, lead(g_ffn_post), m_ffn_post_g, v_ffn_post_g),
    ]
    grads = [u_[0] for u_ in upd]
    deltas = [u_[1] for u_ in upd]
    new_m = [u_[2] for u_ in upd]
    new_v = [u_[3] for u_ in upd]
    return (loss, grad_x, *grads, *deltas, *new_m, *new_v)
```

```python
import math

import jax
import jax.numpy as jnp
from jax import lax
from jax.experimental import pallas as pl
from jax.experimental.pallas import tpu as pltpu

F32 = jnp.float32
BF16 = jnp.bfloat16
HI = lax.Precision.HIGHEST

D_MODEL = 1024
N_META = 16
BLK = 128
PAD = BLK - N_META
HEADS = 16
HEAD_DIM = 64
SSD_GROUPS = 2
SSD_STATE = 128
HEADS_PER_GROUP = HEADS // SSD_GROUPS
SSD_INNER = HEADS * HEAD_DIM
SSD_CONV = 4
XBC = SSD_INNER + 2 * SSD_GROUPS * SSD_STATE
OFF_B = SSD_INNER
OFF_C = SSD_INNER + SSD_GROUPS * SSD_STATE
D_FF = 2816
FFN_CONV = 3
EPS = 1e-6
SB_SCALE = 1.0 / math.sqrt(HEAD_DIM)
N_DEV = 8
LANES = 128
HALO = 8

ADAM_LR = 0.001
ADAM_B1 = 0.9
ADAM_B2 = 0.999
ADAM_EPS = 1e-08
ADAM_WD = 0.01
ADAM_STEP = 10

VMEM_FLOOR = 32 << 20
VMEM_CEIL = 60 << 20
MM_BUDGET = 20 << 20
ROW_BUDGET = 6 << 20

NN = (((1,), (0,)), ((), ()))
NT = (((1,), (1,)), ((), ()))
TN = (((0,), (0,)), ((), ()))


def _params(tile_bytes, sem=None):
    limit = int(min(max(2 * tile_bytes + (8 << 20), VMEM_FLOOR), VMEM_CEIL))
    return pltpu.CompilerParams(vmem_limit_bytes=limit, dimension_semantics=sem)


def _nbytes(shape, dtype):
    n = 1
    for s in shape:
        n *= s
    return n * jnp.dtype(dtype).itemsize


def _dot(a, b, dims=NN, precision=None):
    return lax.dot_general(a, b, dims, precision=precision, preferred_element_type=F32)


def _softplus(x):
    return jnp.maximum(x, 0.0) + jnp.log1p(jnp.exp(-jnp.abs(x)))


def _rms(x, g):
    r = lax.rsqrt(jnp.mean(x * x, axis=-1, keepdims=True) + EPS)
    return x * r * g


def _rms_bwd(dy, x, g):
    r = lax.rsqrt(jnp.mean(x * x, axis=-1, keepdims=True) + EPS)
    xh = x * r
    u = dy * g
    dx = r * (u - xh * jnp.mean(xh * u, axis=-1, keepdims=True))
    return dx, jnp.sum(dy * xh, axis=0, keepdims=True)


def _gelu(x):
    c = math.sqrt(2.0 / math.pi)
    return 0.5 * x * (1.0 + jnp.tanh(c * (x + 0.044715 * x * x * x)))


def _gelu_and_grad(x):
    c = math.sqrt(2.0 / math.pi)
    x2 = x * x
    t = jnp.tanh(c * (x + 0.044715 * x2 * x))
    half = 0.5 * (1.0 + t)
    return x * half, half + 0.5 * x * (1.0 - t * t) * c * (1.0 + 3.0 * 0.044715 * x2)


def _row_tile(rows, bytes_per_row):
    big = 384
    return big if rows % big == 0 and big * bytes_per_row <= ROW_BUDGET else BLK


def _mm(name, pairs, mode, out_dtype):
    a0, b0 = pairs[0]
    if mode == "tn":
        m, n = a0.shape[1], b0.shape[1]
    elif mode == "nt":
        m, n = a0.shape[0], b0.shape[0]
    else:
        m, n = a0.shape[0], b0.shape[1]
    dims = {"nn": NN, "nt": NT, "tn": TN}[mode]

    def tile_bytes(tm, tn):
        tot = tm * tn * jnp.dtype(out_dtype).itemsize
        for a, b in pairs:
            k = a.shape[0] if mode == "tn" else a.shape[1]
            tot += tm * k * a.dtype.itemsize + tn * k * b.dtype.itemsize
        return tot

    cands_m = [t for t in (1408, 1024, 512, 384, 256, 128) if m % t == 0] or [m]
    cands_n = [t for t in (1408, 1024, 768, 512, 256, 128) if n % t == 0] or [n]
    best = None
    for tm in cands_m:
        for tn in cands_n:
            if tile_bytes(tm, tn) <= MM_BUDGET and (best is None or tm * tn > best[0] * best[1]):
                best = (tm, tn)
    tm, tn = best if best is not None else (cands_m[-1], cands_n[-1])
    npairs = len(pairs)

    def body(*refs):
        o_ref = refs[2 * npairs]
        acc = None
        for p in range(npairs):
            part = _dot(refs[2 * p][...], refs[2 * p + 1][...], dims)
            acc = part if acc is None else acc + part
        o_ref[...] = acc.astype(o_ref.dtype)

    in_specs, args = [], []
    for a, b in pairs:
        if mode == "tn":
            k = a.shape[0]
            in_specs += [pl.BlockSpec((k, tm), lambda i, j: (0, i)), pl.BlockSpec((k, tn), lambda i, j: (0, j))]
        elif mode == "nt":
            k = a.shape[1]
            in_specs += [pl.BlockSpec((tm, k), lambda i, j: (i, 0)), pl.BlockSpec((tn, k), lambda i, j: (j, 0))]
        else:
            k = a.shape[1]
            in_specs += [pl.BlockSpec((tm, k), lambda i, j: (i, 0)), pl.BlockSpec((k, tn), lambda i, j: (0, j))]
        args += [a, b]
    return pl.pallas_call(
        body, name=name, grid=(m // tm, n // tn), in_specs=in_specs,
        out_specs=pl.BlockSpec((tm, tn), lambda i, j: (i, j)),
        out_shape=jax.ShapeDtypeStruct((m, n), out_dtype),
        compiler_params=_params(tile_bytes(tm, tn), ("parallel", "parallel")),
    )(*args)


class _Window:
    def __init__(self, ref, cols):
        self.ref, self.cols = ref, cols

    def __getitem__(self, idx):
        return self.ref[:, self.cols] if idx is Ellipsis else self.ref[idx[0], self.cols]


def _rowcall(name, fn, rows=(), prevs=(), nexts=(), pars=(), out_rows=(), out_accs=(), chunk=None):
    rows, prevs, nexts, pars = list(rows), list(prevs), list(nexts), list(pars)
    n_rows = (rows + prevs + nexts)[0].shape[0]
    per_row = sum(a.shape[1] * a.dtype.itemsize for a in rows + prevs + nexts)
    per_row += sum(c * jnp.dtype(dt).itemsize for c, dt in out_rows) + sum(a.shape[1] * 4 for a in prevs + nexts)
    tm = _row_tile(n_rows, per_row)
    nt = n_rows // tm
    hb = tm // HALO
    nr, npv, nnx, npar, nor, noa = len(rows), len(prevs), len(nexts), len(pars), len(out_rows), len(out_accs)

    def body(*refs):
        i = pl.program_id(0)
        k = 0
        row_refs = refs[k:k + nr]; k += nr
        pc = refs[k:k + npv]; k += npv
        ph = refs[k:k + npv]; k += npv
        nc = refs[k:k + nnx]; k += nnx
        nh = refs[k:k + nnx]; k += nnx
        par_refs = refs[k:k + npar]; k += npar
        orow = refs[k:k + nor]; k += nor
        oacc = refs[k:k + noa]; k += noa
        pscr = refs[k:k + npv]; k += npv
        nscr = refs[k:k + nnx]
        for c_, h_, s_ in zip(pc, ph, pscr):
            s_[0:HALO, :] = h_[...] * (i > 0).astype(F32)
            s_[HALO:HALO + tm, :] = c_[...]
        for c_, h_, s_ in zip(nc, nh, nscr):
            s_[0:tm, :] = c_[...]
            s_[tm:tm + HALO, :] = h_[...] * (i < nt - 1).astype(F32)
        if noa:
            @pl.when(i == 0)
            def _():
                for r_ in oacc:
                    r_[...] = jnp.zeros_like(r_)
        width = out_rows[0][0]
        windows = [slice(None)] if chunk is None else [slice(c0, c0 + chunk) for c0 in range(0, width, chunk)]
        for cols in windows:
            prev_fns = [(lambda s, s_=s_, cols=cols: s_[pl.ds(HALO - s, tm), cols]) for s_ in pscr]
            next_fns = [(lambda s, s_=s_, cols=cols: s_[pl.ds(s, tm), cols]) for s_ in nscr]
            row_vals, acc_vals = fn(i, tm, [_Window(r_, cols) for r_ in row_refs], prev_fns, next_fns,
                                    [_Window(r_, cols) for r_ in par_refs])
            for r_, v in zip(orow, row_vals):
                r_[:, cols] = v.astype(r_.dtype)
            for r_, v in zip(oacc, acc_vals):
                r_[:, cols] += v

    def row_spec(a):
        return pl.BlockSpec((tm, a.shape[1]), lambda i: (i, 0))

    def whole(shape):
        return pl.BlockSpec(shape, lambda i: (0, 0))

    in_specs = [row_spec(a) for a in rows]
    in_specs += [row_spec(a) for a in prevs]
    in_specs += [pl.BlockSpec((HALO, a.shape[1]), lambda i: (jnp.maximum(i * hb - 1, 0), 0)) for a in prevs]
    in_specs += [row_spec(a) for a in nexts]
    in_specs += [pl.BlockSpec((HALO, a.shape[1]), lambda i: (jnp.minimum((i + 1) * hb, n_rows // HALO - 1), 0)) for a in nexts]
    in_specs += [whole(p.shape) for p in pars]
    out_specs = [pl.BlockSpec((tm, c), lambda i: (i, 0)) for c, _ in out_rows] + [whole(s) for s in out_accs]
    out_shape = [jax.ShapeDtypeStruct((n_rows, c), dt) for c, dt in out_rows]
    out_shape += [jax.ShapeDtypeStruct(s, F32) for s in out_accs]
    scratch = [pltpu.VMEM((tm + HALO, a.shape[1]), F32) for a in prevs + nexts]
    tile = tm * per_row
    res = pl.pallas_call(
        body, name=name, grid=(nt,), in_specs=in_specs, out_specs=out_specs, out_shape=out_shape,
        scratch_shapes=scratch, compiler_params=_params(2 * tile, ("arbitrary",)),
    )(*rows, *prevs, *prevs, *nexts, *nexts, *pars)
    return res


def _row_ids(i, tm):
    return i * tm + lax.broadcasted_iota(jnp.int32, (tm, 1), 0)


def _conv_taps(prev_fn, w_ref, b_ref, taps):
    acc = b_ref[...]
    for k in range(taps):
        acc = acc + prev_fn(taps - 1 - k) * w_ref[k:k + 1, :]
    return acc


def _rms_fwd(name, h, g):
    def fn(i, tm, rows, prevs, nexts, pars):
        return [_rms(rows[0][...], pars[0][...])], []
    return _rowcall(name, fn, rows=[h], pars=[g], out_rows=[(h.shape[1], BF16)])[0]


def _ssd_conv_fwd(xbc_raw, w, b):
    def fn(i, tm, rows, prevs, nexts, pars):
        c = _conv_taps(prevs[0], pars[0], pars[1], SSD_CONV)
        y = c * jax.nn.sigmoid(c)
        return [jnp.where(_row_ids(i, tm) >= PAD, y, 0.0)], []
    return _rowcall("ssd_conv_fwd", fn, prevs=[xbc_raw], pars=[w, b], out_rows=[(XBC, F32)], chunk=LANES)[0]


def _mix_post(mix, h0, g_post, g_pre):
    def fn(i, tm, rows, prevs, nexts, pars):
        h1 = rows[1][...] + _rms(rows[0][...], pars[0][...])
        return [h1, _rms(h1, pars[1][...])], []
    return _rowcall("mix_post", fn, rows=[mix, h0], pars=[g_post, g_pre],
                    out_rows=[(D_MODEL, F32), (D_MODEL, BF16)])


def _ffn_act(g_raw, u, w, b):
    def fn(i, tm, rows, prevs, nexts, pars):
        g = _conv_taps(prevs[0], pars[0], pars[1], FFN_CONV)
        return [_gelu(g) * rows[0][...]], []
    return _rowcall("ffn_act", fn, rows=[u], prevs=[g_raw], pars=[w, b], out_rows=[(D_FF, BF16)], chunk=LANES)[0]


def _loss_post(f, h1, target, g_post):
    def fn(i, tm, rows, prevs, nexts, pars):
        fv, g = rows[0][...], pars[0][...]
        h2 = rows[1][...] + _rms(fv, g)
        real = _row_ids(i, tm) >= BLK
        diff = jnp.where(real, h2 - rows[2][...], 0.0)
        loss = 0.5 * jnp.sum(jnp.mean(diff * diff, axis=-1, keepdims=True))
        dh2 = diff * (1.0 / D_MODEL)
        df, dg = _rms_bwd(dh2, fv, g)
        return [dh2, df], [jnp.zeros((1, LANES), F32) + loss, dg]
    return _rowcall("loss_post", fn, rows=[f, h1, target], pars=[g_post],
                    out_rows=[(D_MODEL, F32), (D_MODEL, BF16)], out_accs=[(1, LANES), (1, D_MODEL)])


def _ffn_bwd_act(dact, u, g_raw, w, b):
    def fn(i, tm, rows, prevs, nexts, pars):
        g = _conv_taps(prevs[0], pars[0], pars[1], FFN_CONV)
        da = rows[0][...]
        gelu, grad = _gelu_and_grad(g)
        return [da * rows[1][...] * grad, da * gelu], []
    return _rowcall("ffn_bwd_act", fn, rows=[dact, u], prevs=[g_raw], pars=[w, b],
                    out_rows=[(D_FF, F32), (D_FF, BF16)], chunk=LANES)


def _conv_bwd(name, dy, x, w, taps):
    width = x.shape[1]

    def fn(i, tm, rows, prevs, nexts, pars):
        dy0 = nexts[0](0)
        dx = None
        for k in range(taps):
            term = nexts[0](taps - 1 - k) * pars[0][k:k + 1, :]
            dx = term if dx is None else dx + term
        dws = [jnp.sum(dy0 * prevs[0](taps - 1 - k), axis=0, keepdims=True) for k in range(taps)]
        return [dx], dws + [jnp.sum(dy0, axis=0, keepdims=True)]
    return _rowcall(name, fn, prevs=[x], nexts=[dy], pars=[w], out_rows=[(width, BF16)],
                    out_accs=[(1, width)] * (taps + 1), chunk=LANES)


def _mid_bwd(dxn2, h1, dh2, mix, g_pre, g_post):
    def fn(i, tm, rows, prevs, nexts, pars):
        d1, dg_pre = _rms_bwd(rows[0][...], rows[1][...], pars[0][...])
        dh1 = rows[2][...] + d1
        dmix, dg_post = _rms_bwd(dh1, rows[3][...], pars[1][...])
        return [dh1, dmix], [dg_pre, dg_post]
    return _rowcall("mid_bwd", fn, rows=[dxn2, h1, dh2, mix], pars=[g_pre, g_post],
                    out_rows=[(D_MODEL, F32), (D_MODEL, BF16)], out_accs=[(1, D_MODEL), (1, D_MODEL)])


def _norm_bwd(name, dy, x, g):
    def fn(i, tm, rows, prevs, nexts, pars):
        dx, dg = _rms_bwd(rows[0][...], rows[1][...], pars[0][...])
        return [dx], [dg]
    return _rowcall(name, fn, rows=[dy, x], pars=[g], out_rows=[(x.shape[1], BF16)], out_accs=[(1, x.shape[1])])


def _ssd_conv_bwd_act(dact, xbc_raw, w, b):
    def fn(i, tm, rows, prevs, nexts, pars):
        c = _conv_taps(prevs[0], pars[0], pars[1], SSD_CONV)
        s = jax.nn.sigmoid(c)
        dc = rows[0][...] * s * (1.0 + c * (1.0 - s))
        return [jnp.where(_row_ids(i, tm) >= PAD, dc, 0.0)], []
    return _rowcall("ssd_conv_bwd_act", fn, rows=[dact], prevs=[xbc_raw], pars=[w, b], out_rows=[(XBC, F32)],
                    chunk=LANES)[0]


def _first_bwd(dxn1, h0, dh1, g):
    def fn(i, tm, rows, prevs, nexts, pars):
        d0, dg = _rms_bwd(rows[0][...], rows[1][...], pars[0][...])
        return [rows[2][...] + d0], [dg]
    return _rowcall("first_bwd", fn, rows=[dxn1, h0, dh1], pars=[g], out_rows=[(D_MODEL, F32)],
                    out_accs=[(1, D_MODEL)])


def _ssd_chunk_terms(c, dtr_ref, bias_ref, alog_ref):
    ri = lax.broadcasted_iota(jnp.int32, (BLK, BLK), 0)
    ci = lax.broadcasted_iota(jnp.int32, (BLK, BLK), 1)
    causal = ri >= ci
    tril = causal.astype(F32)
    triu = (ri <= ci).astype(F32)
    rowmask = ((c * BLK + lax.broadcasted_iota(jnp.int32, (BLK, 1), 0)) >= PAD).astype(F32)
    dt = _softplus(dtr_ref[...] + bias_ref[...]) * rowmask
    a_neg = -jnp.exp(alog_ref[...])
    a = dt * a_neg
    cs = _dot(tril, a, NN, HI)
    cs_t = _dot(a, triu, TN, HI)
    return causal, triu, rowmask, dt, a_neg, cs, cs_t


def _decay_matrix(causal, cs_h, cs_t_h):
    return jnp.where(causal, jnp.exp(jnp.where(causal, cs_h - cs_t_h, 0.0)), 0.0)


def _head_spread(width):
    ri = lax.broadcasted_iota(jnp.int32, (LANES, HEADS * width), 0)
    ci = lax.broadcasted_iota(jnp.int32, (LANES, HEADS * width), 1)
    return ((ri * width <= ci) & (ci < (ri + 1) * width)).astype(BF16)


def _three_terms(x):
    hi = x.astype(BF16)
    rest = x - hi.astype(F32)
    mid = rest.astype(BF16)
    lo = (rest - mid.astype(F32)).astype(BF16)
    return jnp.concatenate([hi, mid, lo], axis=1)


def _spread(x, sel):
    return _dot(_three_terms(x), jnp.concatenate([sel, sel, sel], axis=0))


def _lane_sums(y, sel):
    return _dot(_three_terms(y), jnp.concatenate([sel, sel, sel], axis=1), NT)


def _ssd_spreads(dt, cs, d_ref, sel64, sel128):
    dt64 = _spread(dt, sel64)
    cs64 = _spread(cs, sel64)
    cs128 = _spread(cs, sel128)
    d64 = _spread(jnp.broadcast_to(d_ref[...], (HALO, LANES)), sel64)[0:1, :]
    cl64 = cs64[BLK - 1:BLK, :]
    return dt64, cs128, d64, jnp.exp(cs64), jnp.exp(cl64 - cs64), jnp.exp(cl64)


def _ssd_pair_terms(p, xbc_ref, dt64, cs128, cs_t, causal):
    lanes = _pair_lanes(p)
    xs = xbc_ref[:, lanes]
    decays = [_decay_matrix(causal, cs128[:, h * LANES:(h + 1) * LANES], cs_t[h:h + 1, :]) for h in (2 * p, 2 * p + 1)]
    return xs, xs * dt64[:, lanes], decays


def _ssd_fwd(xbc, dtr, z, dt_bias, a_log, d_skip, norm_g):
    n_rows = xbc.shape[0]
    nb = n_rows // BLK

    def body(xbc_ref, dtr_ref, z_ref, bias_ref, alog_ref, d_ref, g_ref, s64_ref, s128_ref,
             ypre_ref, yssd_ref, st_ref, state):
        c = pl.program_id(0)

        @pl.when(c == 0)
        def _():
            state[...] = jnp.zeros_like(state)

        st_ref[...] = state[...]
        causal, _, _, dt, _, cs, cs_t = _ssd_chunk_terms(c, dtr_ref, bias_ref, alog_ref)
        first = lax.broadcasted_iota(jnp.int32, (BLK, LANES), 1) < HEAD_DIM
        dt64, cs128, d64, from_start, to_end, chunk_decay = _ssd_spreads(dt, cs, d_ref, s64_ref[...], s128_ref[...])
        for g in range(SSD_GROUPS):
            b_b = xbc_ref[:, OFF_B + g * SSD_STATE:OFF_B + (g + 1) * SSD_STATE].astype(BF16)
            c_b = xbc_ref[:, OFF_C + g * SSD_STATE:OFF_C + (g + 1) * SSD_STATE].astype(BF16)
            cb = _dot(c_b, b_b, NT)
            for j in range(HEADS_PER_GROUP // 2):
                p = g * (HEADS_PER_GROUP // 2) + j
                lanes = _pair_lanes(p)
                xs, x_dt, decays = _ssd_pair_terms(p, xbc_ref, dt64, cs128, cs_t, causal)
                ms = [(cb * d).astype(BF16) for d in decays]
                s_p = state[:, lanes]
                y = _dot(jnp.concatenate(ms, axis=1), _head_stack(x_dt.astype(BF16), first))
                y = y + from_start[:, lanes] * _dot(c_b, s_p.astype(BF16))
                state[:, lanes] = chunk_decay[:, lanes] * s_p + _dot(b_b, (to_end[:, lanes] * x_dt).astype(BF16), TN)
                ypre_ref[:, lanes] = y + d64[:, lanes] * xs
        zz = z_ref[...]
        yg = ypre_ref[...] * (zz * jax.nn.sigmoid(zz))
        yssd_ref[...] = _rms(yg, g_ref[...]).astype(yssd_ref.dtype)

    blk = lambda w: pl.BlockSpec((BLK, w), lambda c: (c, 0))
    par = lambda a: pl.BlockSpec(a.shape, lambda c: (0, 0))
    sel64, sel128 = _head_spread(HEAD_DIM), _head_spread(LANES)
    return pl.pallas_call(
        body, name="ssd_fwd", grid=(nb,),
        in_specs=[blk(XBC), blk(LANES), blk(SSD_INNER), par(dt_bias), par(a_log), par(d_skip), par(norm_g),
                  par(sel64), par(sel128)],
        out_specs=[blk(SSD_INNER), blk(SSD_INNER), pl.BlockSpec((None, SSD_STATE, SSD_INNER), lambda c: (c, 0, 0))],
        out_shape=[jax.ShapeDtypeStruct((n_rows, SSD_INNER), F32), jax.ShapeDtypeStruct((n_rows, SSD_INNER), BF16),
                   jax.ShapeDtypeStruct((nb, SSD_STATE, SSD_INNER), F32)],
        scratch_shapes=[pltpu.VMEM((SSD_STATE, SSD_INNER), F32)],
        compiler_params=_params(8 << 20, ("arbitrary",)),
    )(xbc, dtr, z, dt_bias, a_log, d_skip, norm_g, sel64, sel128)


def _ssd_bwd(dy, ypre, z, xbc, dtr, states, dt_bias, a_log, d_skip, norm_g):
    n_rows = xbc.shape[0]
    nb = n_rows // BLK

    def body(dy_ref, ypre_ref, z_ref, xbc_ref, dtr_ref, st_ref, bias_ref, alog_ref, d_ref, g_ref, s64_ref, s128_ref,
             dz_ref, dxbc_ref, ddtr_ref, dgn_ref, dd_ref, dal_ref, ddtb_ref, dstate, dyp, red):
        step = pl.program_id(0)
        c = nb - 1 - step

        @pl.when(step == 0)
        def _():
            dstate[...] = jnp.zeros_like(dstate)
            for r_ in (dgn_ref, dd_ref, dal_ref, ddtb_ref):
                r_[...] = jnp.zeros_like(r_)

        yp, zz = ypre_ref[...], z_ref[...]
        sz = jax.nn.sigmoid(zz)
        silu = zz * sz
        dyg, dgn = _rms_bwd(dy_ref[...], yp * silu, g_ref[...])
        dgn_ref[...] += dgn
        dz_ref[...] = (dyg * yp * (sz * (1.0 + zz * (1.0 - sz)))).astype(dz_ref.dtype)
        dyp[...] = dyg * silu

        causal, triu, rowmask, dt, a_neg, cs, cs_t = _ssd_chunk_terms(c, dtr_ref, bias_ref, alog_ref)
        lane = lax.broadcasted_iota(jnp.int32, (1, LANES), 1)
        last_row = (lax.broadcasted_iota(jnp.int32, (BLK, 1), 0) == BLK - 1).astype(F32)
        first = lax.broadcasted_iota(jnp.int32, (BLK, LANES), 1) < HEAD_DIM
        sel64 = s64_ref[...]
        dt64, cs128, d64, from_start, to_end, chunk_decay = _ssd_spreads(dt, cs, d_ref, sel64, s128_ref[...])
        for g in range(SSD_GROUPS):
            b_b = xbc_ref[:, OFF_B + g * SSD_STATE:OFF_B + (g + 1) * SSD_STATE].astype(BF16)
            c_b = xbc_ref[:, OFF_C + g * SSD_STATE:OFF_C + (g + 1) * SSD_STATE].astype(BF16)
            cb = _dot(c_b, b_b, NT)
            b_twice = jnp.concatenate([b_b, b_b], axis=0)
            c_twice = jnp.concatenate([c_b, c_b], axis=0)
            db_g = jnp.zeros((BLK, SSD_STATE), F32)
            dc_g = jnp.zeros((BLK, SSD_STATE), F32)
            for j in range(HEADS_PER_GROUP // 2):
                p = g * (HEADS_PER_GROUP // 2) + j
                lanes = _pair_lanes(p)
                xs, x_dt, decays = _ssd_pair_terms(p, xbc_ref, dt64, cs128, cs_t, causal)
                ms = [(cb * d).astype(BF16) for d in decays]
                d_y = dyp[:, lanes]
                s_p, ds_p = st_ref[:, lanes], dstate[:, lanes]
                s_b, ds_b = s_p.astype(BF16), ds_p.astype(BF16)
                fs, te = from_start[:, lanes], to_end[:, lanes]
                x_b, dy_b = x_dt.astype(BF16), d_y.astype(BF16)
                x_st = _head_stack(x_b, first)
                fs_dy = (fs * d_y).astype(BF16)

                y_diag = _dot(jnp.concatenate(ms, axis=1), x_st)
                y_off = fs * _dot(c_b, s_b)
                end_part = te * _dot(b_b, ds_b)
                dx_diag = _dot(jnp.concatenate(ms, axis=0), _head_stack(dy_b, first), TN)
                d_x = dx_diag + end_part
                g2 = _dot(dy_b, x_st, NT)
                gl = [(g2[:, k * BLK:(k + 1) * BLK] * decays[k]).astype(BF16) for k in range(2)]
                dc_g = dc_g + _dot(jnp.concatenate(gl, axis=1), b_twice) + _dot(fs_dy, s_b, NT)
                db_g = db_g + _dot(jnp.concatenate(gl, axis=0), c_twice, TN) + _dot((te * x_dt).astype(BF16), ds_b, NT)
                red[0:BLK, lanes] = (dy_b.astype(F32) * y_diag - x_b.astype(F32) * dx_diag) + (d_y * y_off - x_dt * end_part)
                red[BLK:2 * BLK, lanes] = x_dt * end_part
                red[2 * BLK:3 * BLK, lanes] = d_x * xs
                red[3 * BLK:3 * BLK + HALO, lanes] = jnp.broadcast_to(jnp.sum(ds_p * s_p, axis=0, keepdims=True), (HALO, LANES))
                red[3 * BLK + HALO:3 * BLK + 2 * HALO, lanes] = jnp.broadcast_to(
                    jnp.sum(d_y * xs, axis=0, keepdims=True), (HALO, LANES))
                dxbc_ref[:, lanes] = d_x * dt64[:, lanes] + d64[:, lanes] * d_y
                dstate[:, lanes] = chunk_decay[:, lanes] * ds_p + _dot(c_b, fs_dy, TN)
            dxbc_ref[:, OFF_B + g * SSD_STATE:OFF_B + (g + 1) * SSD_STATE] = db_g
            dxbc_ref[:, OFF_C + g * SSD_STATE:OFF_C + (g + 1) * SSD_STATE] = dc_g
        sums = _lane_sums(red[...], sel64)
        at_last = (jnp.sum(sums[BLK:2 * BLK], axis=0, keepdims=True)
                   + jnp.exp(cs[BLK - 1:BLK, :]) * sums[3 * BLK:3 * BLK + 1])
        dcs = sums[0:BLK] + last_row * at_last
        ddt_x = sums[2 * BLK:3 * BLK]
        dd_row = sums[3 * BLK + HALO:3 * BLK + HALO + 1]
        da = _dot(triu, dcs, NN, HI)
        ddt = (da * a_neg + ddt_x) * rowmask
        ddtr = ddt * jax.nn.sigmoid(dtr_ref[...] + bias_ref[...]) * (lane < HEADS).astype(F32)
        ddtr_ref[...] = ddtr.astype(ddtr_ref.dtype)
        ddtb_ref[...] += jnp.sum(ddtr, axis=0, keepdims=True)
        dal_ref[...] += jnp.sum(da * dt, axis=0, keepdims=True) * a_neg
        dd_ref[...] += dd_row

    blk = lambda w: pl.BlockSpec((BLK, w), lambda s: (nb - 1 - s, 0))
    par = lambda a: pl.BlockSpec(a.shape, lambda s: (0, 0))
    acc = lambda w: pl.BlockSpec((1, w), lambda s: (0, 0))
    sel64, sel128 = _head_spread(HEAD_DIM), _head_spread(LANES)
    return pl.pallas_call(
        body, name="ssd_bwd", grid=(nb,),
        in_specs=[blk(SSD_INNER), blk(SSD_INNER), blk(SSD_INNER), blk(XBC), blk(LANES),
                  pl.BlockSpec((None, SSD_STATE, SSD_INNER), lambda s: (nb - 1 - s, 0, 0)),
                  par(dt_bias), par(a_log), par(d_skip), par(norm_g), par(sel64), par(sel128)],
        out_specs=[blk(SSD_INNER), blk(XBC), blk(LANES), acc(SSD_INNER), acc(LANES), acc(LANES), acc(LANES)],
        out_shape=[jax.ShapeDtypeStruct((n_rows, SSD_INNER), BF16), jax.ShapeDtypeStruct((n_rows, XBC), F32),
                   jax.ShapeDtypeStruct((n_rows, LANES), BF16), jax.ShapeDtypeStruct((1, SSD_INNER), F32),
                   jax.ShapeDtypeStruct((1, LANES), F32), jax.ShapeDtypeStruct((1, LANES), F32),
                   jax.ShapeDtypeStruct((1, LANES), F32)],
        scratch_shapes=[pltpu.VMEM((SSD_STATE, SSD_INNER), F32), pltpu.VMEM((BLK, SSD_INNER), F32),
                        pltpu.VMEM((3 * BLK + 2 * HALO, SSD_INNER), F32)],
        compiler_params=_params(12 << 20, ("arbitrary",)),
    )(dy, ypre, z, xbc, dtr, states, dt_bias, a_log, d_skip, norm_g, sel64, sel128)


HEAD_GROUP = 4


LOG2_E = 1.4426950408889634
SOFTPLUS_CLAMP = 80.0


def _sb_logits(zl, valid):
    z2 = zl * LOG2_E
    lost = jnp.maximum(jnp.log2(1.0 + jnp.exp2(jnp.minimum(z2, SOFTPLUS_CLAMP))), z2)
    log_beta = z2 - lost
    if valid is not None:
        lost = jnp.where(valid, lost, 0.0)
    return log_beta, lost


def _tri_and_ones(tri, sign=1.0):
    half = sign * jnp.concatenate([tri, jnp.ones((BLK, BLK), F32)], axis=1)
    return jnp.concatenate([half, half], axis=0).astype(BF16)


def _tile_mask(i, j, ri, ci):
    key = j * BLK + ci
    return (key < i * BLK + ri) & (key >= PAD)


def _pair_lanes(p):
    return slice(p * 2 * HEAD_DIM, (p + 1) * 2 * HEAD_DIM)


def _head_stack(x_pair, first):
    zero = jnp.zeros_like(x_pair)
    return jnp.concatenate([jnp.where(first, x_pair, zero), jnp.where(first, zero, x_pair)], axis=0)


def _block_rows(j):
    return pl.ds(j * BLK if isinstance(j, int) else pl.multiple_of(j * BLK, BLK), BLK)


def _sweep_pairs(i, tile):
    n_calls = (i + 2) // 2
    tile(0, True)

    def mid(t, carry):
        tile(t, False)
        return carry

    lax.fori_loop(1, n_calls - 1, mid, 0)

    @pl.when(n_calls >= 2)
    def _():
        tile(n_calls - 1, True)

    return n_calls


def _two_terms(x):
    hi = x.astype(BF16)
    return jnp.concatenate([hi, (x - hi.astype(F32)).astype(BF16)], axis=1)


def _ride_along(ex, n_in, n_out, refs):
    k = 0
    parts = []
    for cnt in (n_in, ex.n, n_out, ex.n):
        parts.append(refs[k:k + cnt])
        k += cnt
    n_sems = len(ex.scratch)
    return (*parts, refs[k:len(refs) - n_sems], refs[len(refs) - n_sems:])


def _attn_fwd(q, k, v, ex, ex_arrays):
    n_rows, width = q.shape
    nb = n_rows // BLK
    gw = HEAD_GROUP * HEAD_DIM
    groups = width // gw

    def body(*refs):
        (q_ref, k_ref, v_ref), ex_in, (o_ref, tot_ref), ex_out, (run_ref, z_ref, w_ref), sems = _ride_along(ex, 3, 2, refs)
        step = pl.program_id(0)

        @pl.when(step == 0)
        def _():
            ex.start(ex_in, ex_out, sems)

        ri = lax.broadcasted_iota(jnp.int32, (BLK, BLK), 0)
        ci = lax.broadcasted_iota(jnp.int32, (BLK, BLK), 1)
        sums = _tri_and_ones((ri > ci).astype(F32), -1.0)
        first = ci < HEAD_DIM
        heads, pairs = range(HEAD_GROUP), range(HEAD_GROUP // 2)

        def stacks(ref, blocks, p):
            return jnp.concatenate([_head_stack(ref[_block_rows(jnp.clip(j, 0, nb - 1)), _pair_lanes(p)], first)
                                    for j in blocks], axis=0)

        def q_block(i, carry):
            rows = _block_rows(i)
            o_ref[rows, :] = jnp.zeros((BLK, gw), F32)
            run_ref[...] = jnp.zeros_like(run_ref)
            w_ref[...] = jnp.zeros_like(w_ref)

            def blocks(t):
                return i - 2 * t, i - 2 * t - 1

            for p in pairs:
                z_ref[p] = _dot(q_ref[rows, _pair_lanes(p)], stacks(k_ref, blocks(0), p), NT)

            def tile(t, masked):
                q_i, o_i = q_ref[rows, :], o_ref[rows, :]
                outs = [_dot(w_ref[p], stacks(v_ref, blocks(jnp.maximum(t - 1, 0)), p)) for p in pairs]
                z_next = [_dot(q_i[:, _pair_lanes(p)], stacks(k_ref, blocks(t + 1), p), NT) for p in pairs]
                runs = [run_ref[h] for h in heads]
                lgs, res = [], []
                for s, j in enumerate(blocks(t)):
                    valid = _tile_mask(i, j, ri, ci) if masked else None
                    lgs.append([_sb_logits(z_ref[h // 2][:, (2 * s + h % 2) * BLK:(2 * s + h % 2 + 1) * BLK], valid)
                                for h in heads])
                    res.append(_dot(jnp.concatenate([_two_terms(lgs[s][h][1]) for h in heads], axis=0), sums))
                ws = []
                for s, j in enumerate(blocks(t)):
                    valid = _tile_mask(i, j, ri, ci) if masked else None
                    for h in heads:
                        part = res[s][h * BLK:(h + 1) * BLK]
                        w = jnp.exp2(lgs[s][h][0] + part[:, :BLK] + runs[h])
                        ws.append(jnp.where(valid, w, 0.0) if masked else w)
                        runs[h] = runs[h] + part[:, BLK:]
                for p in pairs:
                    w_ref[p] = jnp.concatenate([ws[s * HEAD_GROUP + 2 * p + e].astype(BF16) for s in range(2) for e in range(2)],
                                               axis=1)
                    z_ref[p] = z_next[p]
                o_ref[rows, :] = o_i + jnp.concatenate(outs, axis=1)
                for h in heads:
                    run_ref[h] = runs[h]

            n_calls = _sweep_pairs(i, tile)
            o_ref[rows, :] += jnp.concatenate([_dot(w_ref[p], stacks(v_ref, blocks(n_calls - 1), p)) for p in pairs], axis=1)
            for h in heads:
                tot_ref[rows, h:h + 1] = run_ref[h][:, 0:1]
            return carry

        lax.fori_loop(0, nb, q_block, 0)

        @pl.when(step == groups - 1)
        def _():
            ex.wait(ex_in, ex_out, sems)

    spec = pl.BlockSpec((n_rows, gw), lambda g: (0, g))
    tot_spec = pl.BlockSpec((None, n_rows, HEAD_GROUP), lambda g: (g, 0, 0))
    res = pl.pallas_call(
        body, name="attn_fwd", grid=(groups,), in_specs=[spec, spec, spec] + ex.specs,
        out_specs=[spec, tot_spec] + ex.specs,
        out_shape=[jax.ShapeDtypeStruct((n_rows, width), F32),
                   jax.ShapeDtypeStruct((groups, n_rows, HEAD_GROUP), F32)] + ex.out_shape,
        scratch_shapes=[pltpu.VMEM((HEAD_GROUP, BLK, BLK), F32), pltpu.VMEM((HEAD_GROUP // 2, BLK, 4 * BLK), F32),
                        pltpu.VMEM((HEAD_GROUP // 2, BLK, 4 * BLK), BF16)] + ex.scratch,
        compiler_params=_params(n_rows * (3 * gw * 2 + gw * 4 + LANES * 4), ("arbitrary",)),
    )(q, k, v, *ex_arrays)
    return res[0], res[1], res[2:]


def _attn_bwd(q, k, v, keep_total, do, ex, ex_arrays):
    n_rows, width = q.shape
    nb = n_rows // BLK
    gw = HEAD_GROUP * HEAD_DIM
    groups = width // gw

    def body(*refs):
        ((q_ref, k_ref, v_ref, tot_ref, do_ref), ex_in, (dq_ref, dk_ref, dv_ref), ex_out,
         (dq_acc, dk_acc, dv_acc, tot_b, run_ref, rung_ref, z_ref, dw_ref, dz_ref, wb_ref, qst_ref, dost_ref),
         sems) = _ride_along(ex, 5, 3, refs)
        step = pl.program_id(0)

        @pl.when(step == 0)
        def _():
            ex.start(ex_in, ex_out, sems)

        ri = lax.broadcasted_iota(jnp.int32, (BLK, BLK), 0)
        ci = lax.broadcasted_iota(jnp.int32, (BLK, BLK), 1)
        sums_keep = _tri_and_ones((ri <= ci).astype(F32), -1.0)
        sums_g = _tri_and_ones((ri < ci).astype(F32))
        first = ci < HEAD_DIM
        heads, pairs = range(HEAD_GROUP), range(HEAD_GROUP // 2)
        dk_acc[...] = jnp.zeros_like(dk_acc)
        dv_acc[...] = jnp.zeros_like(dv_acc)

        def clamp(j):
            return jnp.clip(j, 0, nb - 1)

        def stacks(ref, blocks, p):
            return jnp.concatenate([_head_stack(ref[_block_rows(clamp(j)), _pair_lanes(p)], first) for j in blocks], axis=0)

        def blocks(t):
            return 2 * t, 2 * t + 1

        def kept(ref, p, axis):
            tiles = [[ref[s * HEAD_GROUP + 2 * p + e] for e in range(2)] for s in range(2)]
            if axis == 1:
                return jnp.concatenate(tiles[0] + tiles[1], axis=1)
            return jnp.concatenate([jnp.concatenate(tiles[s], axis=0) for s in range(2)], axis=1)

        def owed_dq(t_prev):
            return [_dot(kept(dz_ref, p, 1), stacks(k_ref, blocks(t_prev), p)) for p in pairs]

        def owed_dk():
            return [_dot(kept(dz_ref, p, 0), qst_ref[p], TN) for p in pairs]

        def owed_dv():
            return [_dot(kept(wb_ref, p, 0), dost_ref[p], TN) for p in pairs]

        def settle(t_prev, parts):
            dq, dk, dv = parts
            dq_acc[...] += jnp.concatenate(dq, axis=1)
            for s, j in enumerate(blocks(t_prev)):
                cols = _block_rows(clamp(j))
                dk_acc[cols, :] += jnp.concatenate([d[s * BLK:(s + 1) * BLK] for d in dk], axis=1)
                dv_acc[cols, :] += jnp.concatenate([d[s * BLK:(s + 1) * BLK] for d in dv], axis=1)

        def q_block(i, carry):
            rows = _block_rows(i)
            dq_acc[...] = jnp.zeros_like(dq_acc)
            run_ref[...] = jnp.zeros_like(run_ref)
            rung_ref[...] = jnp.zeros_like(rung_ref)
            dz_ref[...] = jnp.zeros_like(dz_ref)
            wb_ref[...] = jnp.zeros_like(wb_ref)
            for h in heads:
                tot_b[h] = jnp.broadcast_to(tot_ref[rows, h:h + 1], (BLK, BLK))
            for p in pairs:
                qst_ref[p] = _head_stack(q_ref[rows, _pair_lanes(p)], first)
                dost_ref[p] = _head_stack(do_ref[rows, _pair_lanes(p)], first)
                z_ref[p] = _dot(q_ref[rows, _pair_lanes(p)], stacks(k_ref, blocks(0), p), NT)
                dw_ref[p] = _dot(do_ref[rows, _pair_lanes(p)], stacks(v_ref, blocks(0), p), NT)

            def tile(t, masked):
                q_i, do_i = q_ref[rows, :], do_ref[rows, :]
                t_prev = jnp.maximum(t - 1, 0)
                valids = [_tile_mask(i, j, ri, ci) if masked else None for j in blocks(t)]
                tile_of = lambda ref, s, h: ref[h // 2][:, (2 * s + h % 2) * BLK:(2 * s + h % 2 + 1) * BLK]
                runs = [run_ref[h] for h in heads]
                rungs = [rung_ref[h] for h in heads]
                lgs, keep, ws, gs, gsum, dzs = [], [], [], [], [], []
                for s in range(2):
                    lgs.append([_sb_logits(tile_of(z_ref, s, h), valids[s]) for h in heads])
                    keep.append(_dot(jnp.concatenate([_two_terms(lgs[s][h][1]) for h in heads], axis=0), sums_keep))
                    if s == 0:
                        part_dq = owed_dq(t_prev)
                    else:
                        part_dk = owed_dk()
                for s in range(2):
                    ws.append([])
                    gs.append([])
                    for h in heads:
                        part = keep[s][h * BLK:(h + 1) * BLK]
                        w = jnp.exp2(lgs[s][h][0] + (tot_b[h] - runs[h] - part[:, :BLK]))
                        ws[s].append(jnp.where(valids[s], w, 0.0) if masked else w)
                        runs[h] = runs[h] + part[:, BLK:]
                        gs[s].append(tile_of(dw_ref, s, h) * ws[s][h])
                    gsum.append(_dot(jnp.concatenate([_two_terms(gs[s][h]) for h in heads], axis=0), sums_g))
                    if s == 0:
                        part_dv = owed_dv()
                    else:
                        z_next = [_dot(q_i[:, _pair_lanes(p)], stacks(k_ref, blocks(t + 1), p), NT) for p in pairs]
                for s in range(2):
                    dzs.append([])
                    for h in heads:
                        part = gsum[s][h * BLK:(h + 1) * BLK]
                        beta = jnp.exp2(lgs[s][h][0])
                        dz = gs[s][h] * (1.0 - beta) - beta * (part[:, :BLK] + rungs[h])
                        dzs[s].append(jnp.where(valids[s], dz, 0.0) if masked else dz)
                        rungs[h] = rungs[h] + part[:, BLK:]
                    if s == 0:
                        dw_next = [_dot(do_i[:, _pair_lanes(p)], stacks(v_ref, blocks(t + 1), p), NT) for p in pairs]
                settle(t_prev, (part_dq, part_dk, part_dv))
                for s in range(2):
                    for h in heads:
                        dz_ref[s * HEAD_GROUP + h] = dzs[s][h].astype(BF16)
                        wb_ref[s * HEAD_GROUP + h] = ws[s][h].astype(BF16)
                for h in heads:
                    run_ref[h] = runs[h]
                    rung_ref[h] = rungs[h]
                for p in pairs:
                    z_ref[p] = z_next[p]
                    dw_ref[p] = dw_next[p]

            n_calls = _sweep_pairs(i, tile)
            settle(n_calls - 1, (owed_dq(n_calls - 1), owed_dk(), owed_dv()))
            dq_ref[rows, :] = (dq_acc[...] * SB_SCALE).astype(dq_ref.dtype)
            return carry

        lax.fori_loop(0, nb, q_block, 0)
        dk_ref[...] = dk_acc[...].astype(dk_ref.dtype)
        dv_ref[...] = dv_acc[...].astype(dv_ref.dtype)

        @pl.when(step == groups - 1)
        def _():
            ex.wait(ex_in, ex_out, sems)

    spec = pl.BlockSpec((n_rows, gw), lambda g: (0, g))
    tot_spec = pl.BlockSpec((None, n_rows, HEAD_GROUP), lambda g: (g, 0, 0))
    out = jax.ShapeDtypeStruct((n_rows, width), BF16)
    tile_f32 = pltpu.VMEM((HEAD_GROUP, BLK, BLK), F32)
    tile_bf16 = pltpu.VMEM((2 * HEAD_GROUP, BLK, BLK), BF16)
    pair_f32 = pltpu.VMEM((HEAD_GROUP // 2, BLK, 4 * BLK), F32)
    pair_stack = pltpu.VMEM((HEAD_GROUP // 2, 2 * BLK, BLK), BF16)
    res = pl.pallas_call(
        body, name="attn_bwd", grid=(groups,), in_specs=[spec, spec, spec, tot_spec, spec] + ex.specs,
        out_specs=[spec] * 3 + ex.specs, out_shape=[out] * 3 + ex.out_shape,
        scratch_shapes=[pltpu.VMEM((BLK, gw), F32), pltpu.VMEM((n_rows, gw), F32), pltpu.VMEM((n_rows, gw), F32),
                        tile_f32, tile_f32, tile_f32, pair_f32, pair_f32, tile_bf16, tile_bf16, pair_stack,
                        pair_stack] + ex.scratch,
        compiler_params=_params(n_rows * (7 * gw * 2 + LANES * 4 + gw * 4), ("arbitrary",)),
    )(q, k, v, keep_total, do, *ex_arrays)
    return res[0], res[1], res[2], res[3:]


class _Exchange:
    def __init__(self, arrays, gather):
        self.n = len(arrays)
        self.gather = gather
        self.out_shape = [jax.ShapeDtypeStruct(((N_DEV,) + a.shape) if gather else a.shape, a.dtype) for a in arrays]
        self.scratch = [pltpu.SemaphoreType.DMA((self.n, N_DEV - 1)), pltpu.SemaphoreType.DMA((self.n, N_DEV - 1)),
                        pltpu.SemaphoreType.DMA((self.n,))]
        self.specs = [pl.BlockSpec(memory_space=pl.ANY)] * self.n

    def _copies(self, ins, outs, sems):
        send_sems, recv_sems, local_sems = sems
        x, y, c = lax.axis_index("x"), lax.axis_index("y"), lax.axis_index("c")
        me = 4 * x + 2 * y + c
        gather = self.gather
        local, sends, recvs = [], [], []
        for a in range(self.n):
            local.append(pltpu.make_async_copy(ins[a] if gather else ins[a].at[me], outs[a].at[me if gather else 0],
                                               local_sems.at[a]))
        for r in range(1, N_DEV):
            px = 1 - x if r & 4 else x
            py = 1 - y if r & 2 else y
            pc = 1 - c if r & 1 else c
            idx = 4 * px + 2 * py + pc
            for a in range(self.n):
                src = ins[a] if gather else ins[a].at[idx]
                pair = dict(send_sem=send_sems.at[a, r - 1], recv_sem=recv_sems.at[a, r - 1],
                            device_id=(px, py, pc), device_id_type=pl.DeviceIdType.MESH)
                sends.append(pltpu.make_async_remote_copy(src_ref=src, dst_ref=outs[a].at[me if gather else r], **pair))
                recvs.append(pltpu.make_async_remote_copy(src_ref=src, dst_ref=outs[a].at[idx if gather else r], **pair))
        return local, sends, recvs

    def start(self, ins, outs, sems):
        local, sends, _ = self._copies(ins, outs, sems)
        for cp in local + sends:
            cp.start()

    def wait(self, ins, outs, sems):
        local, sends, recvs = self._copies(ins, outs, sems)
        for cp in recvs:
            cp.wait_recv()
        for cp in sends:
            cp.wait_send()
        for cp in local:
            cp.wait()


def _peer_exchange(name, arrays, gather):
    ex = _Exchange(arrays, gather)
    n = ex.n

    def body(*refs):
        ins, outs, sems = refs[:n], refs[n:2 * n], refs[2 * n:]
        ex.start(ins, outs, sems)
        ex.wait(ins, outs, sems)

    return pl.pallas_call(body, name=name, in_specs=ex.specs, out_specs=ex.specs, out_shape=ex.out_shape,
                          scratch_shapes=ex.scratch)(*arrays)


def _sum_slots(name, x):
    def body(x_ref, o_ref):
        acc = x_ref[0]
        for s in range(1, N_DEV):
            acc = acc + x_ref[s]
        o_ref[...] = acc
    return pl.pallas_call(body, name=name, out_shape=jax.ShapeDtypeStruct(x.shape[1:], F32))(x)


def _adamw(name, w, slots, m, v):
    n_slots, rows, cols = slots.shape
    tr = next((t for t in (256, 176, 128) if rows % t == 0 and rows > t), rows)

    def body(w_ref, s_ref, m_ref, v_ref, g_ref, d_ref, nm_ref, nv_ref):
        g = s_ref[0].astype(F32)
        for s in range(1, n_slots):
            g = g + s_ref[s].astype(F32)
        nm = ADAM_B1 * m_ref[...] + (1.0 - ADAM_B1) * g
        nv = ADAM_B2 * v_ref[...] + (1.0 - ADAM_B2) * (g * g)
        m_hat = nm / (1.0 - ADAM_B1 ** ADAM_STEP)
        v_hat = nv / (1.0 - ADAM_B2 ** ADAM_STEP)
        g_ref[...] = g
        d_ref[...] = -ADAM_LR * (m_hat / (jnp.sqrt(v_hat) + ADAM_EPS) + ADAM_WD * w_ref[...])
        nm_ref[...] = nm
        nv_ref[...] = nv

    spec = pl.BlockSpec((tr, cols), lambda i: (i, 0))
    out = jax.ShapeDtypeStruct((rows, cols), F32)
    return pl.pallas_call(
        body, name=name, grid=(rows // tr,),
        in_specs=[spec, pl.BlockSpec((n_slots, tr, cols), lambda i: (0, i, 0)), spec, spec],
        out_specs=[spec] * 4, out_shape=[out] * 4,
        compiler_params=_params((n_slots + 7) * tr * cols * 4, ("parallel",)),
    )(w, slots, m, v)


def _pad_lanes(a):
    return jnp.pad(a, ((0, 0), (0, LANES - a.shape[1])))


def kernel(x, meta_tokens, mix_pre_g, w_in, ssd_conv_w, ssd_conv_b, ssd_dt_bias, ssd_a_log, ssd_d, ssd_norm_g, sb_norm_g, w_out, mix_post_g, ffn_pre_g, w_up, ffn_conv_w, ffn_conv_b, w_down, ffn_post_g, loss_target, m_meta_tokens, m_mix_pre_g, m_w_in, m_ssd_conv_w, m_ssd_conv_b, m_ssd_dt_bias, m_ssd_a_log, m_ssd_d, m_ssd_norm_g, m_sb_norm_g, m_w_out, m_mix_post_g, m_ffn_pre_g, m_w_up, m_ffn_conv_w, m_ffn_conv_b, m_w_down, m_ffn_post_g, v_meta_tokens, v_mix_pre_g, v_w_in, v_ssd_conv_w, v_ssd_conv_b, v_ssd_dt_bias, v_ssd_a_log, v_ssd_d, v_ssd_norm_g, v_sb_norm_g, v_w_out, v_mix_post_g, v_ffn_pre_g, v_w_up, v_ffn_conv_w, v_ffn_conv_b, v_w_down, v_ffn_post_g):
    seq = x.shape[1]
    me = 4 * lax.axis_index("x") + 2 * lax.axis_index("y") + lax.axis_index("c")
    in_cols = w_in.shape[2]
    up_cols = w_up.shape[2]
    out_rows = w_out.shape[1]
    down_rows = w_down.shape[1]

    g_in, g_meta, g_scw, g_fcw = _peer_exchange(
        "gather_w_in", [w_in[0].astype(BF16), meta_tokens, ssd_conv_w[0], ffn_conv_w[0]], gather=True)
    late_weights = [w_out[0].astype(BF16), w_up[0].astype(BF16), w_down[0].astype(BF16)]
    w_in_full = g_in.transpose(1, 0, 2).reshape(D_MODEL, N_DEV * in_cols)
    off = [0, SSD_INNER, SSD_INNER + XBC, SSD_INNER + XBC + HEADS]
    w_z = w_in_full[:, off[0]:off[1]]
    w_xbc = w_in_full[:, off[1]:off[2]]
    w_dt = _pad_lanes(w_in_full[:, off[2]:off[3]])
    w_q = w_in_full[:, off[3]:off[3] + SSD_INNER]
    w_k = w_in_full[:, off[3] + SSD_INNER:off[3] + 2 * SSD_INNER]
    w_v = w_in_full[:, off[3] + 2 * SSD_INNER:off[3] + 3 * SSD_INNER]
    meta_full = g_meta.transpose(1, 0, 2).reshape(N_META, D_MODEL)
    scw_full = g_scw.transpose(1, 0, 2).reshape(SSD_CONV, XBC)
    fcw_full = g_fcw.transpose(1, 0, 2).reshape(FFN_CONV, D_FF)

    dt_bias_p, a_log_p, d_p = _pad_lanes(ssd_dt_bias), _pad_lanes(ssd_a_log), _pad_lanes(ssd_d)

    h0 = jnp.concatenate([jnp.zeros((PAD, D_MODEL), F32), meta_full, x[0]], axis=0)
    target = jnp.concatenate([jnp.zeros((BLK, D_MODEL), F32), loss_target[0]], axis=0)
    xn1 = _rms_fwd("rms_pre_mix", h0, mix_pre_g)
    z = _mm("proj_z", [(xn1, w_z)], "nn", F32)
    xbc_raw = _mm("proj_xbc", [(xn1, w_xbc)], "nn", F32)
    dtr = _mm("proj_dt", [(xn1, w_dt)], "nn", F32)
    q = _mm("proj_q", [(xn1, w_q * SB_SCALE)], "nn", BF16)
    k = _mm("proj_k", [(xn1, w_k)], "nn", BF16)
    v = _mm("proj_v", [(xn1, w_v)], "nn", BF16)
    xbc_act = _ssd_conv_fwd(xbc_raw, scw_full, ssd_conv_b)
    ypre, y_ssd, states = _ssd_fwd(xbc_act, dtr, z, dt_bias_p, a_log_p, d_p, ssd_norm_g)
    o, keep_total, (g_out, g_up, g_down) = _attn_fwd(q, k, v, _Exchange(late_weights, gather=True), late_weights)
    w_out_full = g_out.reshape(N_DEV * out_rows, D_MODEL)
    wo_ssd, wo_sb = w_out_full[:SSD_INNER], w_out_full[SSD_INNER:]
    w_up_full = g_up.transpose(1, 0, 2).reshape(D_MODEL, N_DEV * up_cols)
    w_gate, w_lin = w_up_full[:, :D_FF], w_up_full[:, D_FF:]
    w_down_full = g_down.reshape(N_DEV * down_rows, D_MODEL)
    y_sb = _rms_fwd("rms_sb", o, sb_norm_g)
    mix = _mm("mix_out", [(y_ssd, wo_ssd), (y_sb, wo_sb)], "nn", F32)
    h1, xn2 = _mix_post(mix, h0, mix_post_g, ffn_pre_g)
    g_raw = _mm("ffn_gate", [(xn2, w_gate)], "nn", F32)
    u = _mm("ffn_lin", [(xn2, w_lin)], "nn", F32)
    act = _ffn_act(g_raw, u, fcw_full, ffn_conv_b)
    f = _mm("ffn_down", [(act, w_down_full)], "nn", F32)
    dh2, df, loss_row, dg_ffn_post = _loss_post(f, h1, target, ffn_post_g)
    loss = lax.psum(loss_row[0, 0], ("x", "y", "c"))

    dact = _mm("d_act", [(df, w_down_full)], "nt", F32)
    dw_down = _mm("dw_down", [(act, df)], "tn", F32)
    dg_conv, du = _ffn_bwd_act(dact, u, g_raw, fcw_full, ffn_conv_b)
    dg_raw, dfcw0, dfcw1, dfcw2, dfcb = _conv_bwd("ffn_conv_bwd", dg_conv, g_raw, fcw_full, FFN_CONV)
    dxn2 = _mm("d_xn2", [(dg_raw, w_gate), (du, w_lin)], "nt", F32)
    dw_gate = _mm("dw_gate", [(xn2, dg_raw)], "tn", F32)
    dw_lin = _mm("dw_lin", [(xn2, du)], "tn", F32)
    dh1, dmix, dg_ffn_pre, dg_mix_post = _mid_bwd(dxn2, h1, dh2, mix, ffn_pre_g, mix_post_g)

    dy_ssd = _mm("d_yssd", [(dmix, wo_ssd)], "nt", F32)
    dy_sb = _mm("d_ysb", [(dmix, wo_sb)], "nt", F32)
    dwo_ssd = _mm("dw_out_ssd", [(y_ssd, dmix)], "tn", F32)
    dwo_sb = _mm("dw_out_sb", [(y_sb, dmix)], "tn", F32)
    do, dg_sb = _norm_bwd("sb_norm_bwd", dy_sb, o, sb_norm_g)
    half = N_DEV // 2
    early_slabs = [
        jnp.concatenate([dwo_ssd, dwo_sb], axis=0).reshape(N_DEV, out_rows, D_MODEL),
        jnp.concatenate([dw_gate.reshape(D_MODEL, half, up_cols).transpose(1, 0, 2),
                         dw_lin.reshape(D_MODEL, half, up_cols).transpose(1, 0, 2)], axis=0),
        dw_down.reshape(N_DEV, down_rows, D_MODEL)]
    dq, dk, dv, (l_out, l_up, l_down) = _attn_bwd(q, k, v, keep_total, do, _Exchange(early_slabs, gather=False), early_slabs)
    dz, dxbc_act, ddtr, dg_ssd_norm, dd_skip, da_log, ddt_bias = _ssd_bwd(
        dy_ssd, ypre, z, xbc_act, dtr, states, dt_bias_p, a_log_p, d_p, ssd_norm_g)
    dconv = _ssd_conv_bwd_act(dxbc_act, xbc_raw, scw_full, ssd_conv_b)
    dxbc_raw, dscw0, dscw1, dscw2, dscw3, dscb = _conv_bwd("ssd_conv_bwd", dconv, xbc_raw, scw_full, SSD_CONV)
    segs = [(dz, w_z), (dxbc_raw, w_xbc), (ddtr, w_dt), (dq, w_q), (dk, w_k), (dv, w_v)]
    dxn1 = _mm("d_xn1", segs, "nt", F32)
    dw_segs = [_mm("dw_in_%d" % s, [(xn1, d)], "tn", BF16) for s, (d, _) in enumerate(segs)]
    dw_segs[2] = dw_segs[2][:, :HEADS]
    dw_in = jnp.concatenate(dw_segs, axis=1)
    dh0, dg_mix_pre = _first_bwd(dxn1, h0, dh1, mix_pre_g)
    grad_x = dh0[BLK:][None]

    slab_in = dw_in.reshape(D_MODEL, N_DEV, in_cols).transpose(1, 0, 2)
    (l_in,) = _peer_exchange("scatter_dw_in", [slab_in], gather=False)

    small = [dg_mix_pre, dscb, ddt_bias, da_log, dd_skip, dg_ssd_norm, dg_sb, dg_mix_post, dg_ffn_pre, dfcb,
             dg_ffn_post, dh0[PAD:BLK].reshape(1, -1), dscw0, dscw1, dscw2, dscw3, dfcw0, dfcw1, dfcw2]
    sizes = [a.shape[1] for a in small]
    total = sum(sizes)
    rows_packed = -(-total // (LANES * HALO)) * HALO
    packed = jnp.pad(jnp.concatenate(small, axis=1), ((0, 0), (0, rows_packed * LANES - total)))
    (gathered,) = _peer_exchange("gather_small_grads", [packed.reshape(rows_packed, LANES)], gather=True)
    summed = _sum_slots("sum_small_grads", gathered).reshape(1, rows_packed * LANES)
    pieces, at = [], 0
    for s in sizes:
        pieces.append(summed[:, at:at + s])
        at += s
    (g_mix_pre, g_scb, g_dtb, g_alog, g_dskip, g_ssd_norm, g_sb, g_mix_post, g_ffn_pre, g_fcb, g_ffn_post,
     g_meta_flat, gs0, gs1, gs2, gs3, gf0, gf1, gf2) = pieces
    g_dtb, g_alog, g_dskip = g_dtb[:, :HEADS], g_alog[:, :HEADS], g_dskip[:, :HEADS]
    g_meta_full = g_meta_flat.reshape(N_META, D_MODEL)
    g_scw_full = jnp.concatenate([gs0, gs1, gs2, gs3], axis=0)
    g_fcw_full = jnp.concatenate([gf0, gf1, gf2], axis=0)
    meta_cols, scw_cols, fcw_cols = meta_tokens.shape[1], ssd_conv_w.shape[2], ffn_conv_w.shape[2]
    g_meta_mine = lax.dynamic_slice(g_meta_full, (0, me * meta_cols), (N_META, meta_cols))
    g_scw_mine = lax.dynamic_slice(g_scw_full, (0, me * scw_cols), (SSD_CONV, scw_cols))
    g_fcw_mine = lax.dynamic_slice(g_fcw_full, (0, me * fcw_cols), (FFN_CONV, fcw_cols))

    def lead(a):
        return a[None]

    upd = [
        _adamw("adamw_meta", meta_tokens, lead(g_meta_mine), m_meta_tokens, v_meta_tokens),
        _adamw("adamw_mix_pre_g", mix_pre_g, lead(g_mix_pre), m_mix_pre_g, v_mix_pre_g),
        [lead(a) for a in _adamw("adamw_w_in", w_in[0], l_in, m_w_in[0], v_w_in[0])],
        [lead(a) for a in _adamw("adamw_ssd_conv_w", ssd_conv_w[0], lead(g_scw_mine), m_ssd_conv_w[0], v_ssd_conv_w[0])],
        _adamw("adamw_ssd_conv_b", ssd_conv_b, lead(g_scb), m_ssd_conv_b, v_ssd_conv_b),
        _adamw("adamw_ssd_dt_bias", ssd_dt_bias, lead(g_dtb), m_ssd_dt_bias, v_ssd_dt_bias),
        _adamw("adamw_ssd_a_log", ssd_a_log, lead(g_alog), m_ssd_a_log, v_ssd_a_log),
        _adamw("adamw_ssd_d", ssd_d, lead(g_dskip), m_ssd_d, v_ssd_d),
        _adamw("adamw_ssd_norm_g", ssd_norm_g, lead(g_ssd_norm), m_ssd_norm_g, v_ssd_norm_g),
        _adamw("adamw_sb_norm_g", sb_norm_g, lead(g_sb), m_sb_norm_g, v_sb_norm_g),
        [lead(a) for a in _adamw("adamw_w_out", w_out[0], l_out, m_w_out[0], v_w_out[0])],
        _adamw("adamw_mix_post_g", mix_post_g, lead(g_mix_post), m_mix_post_g, v_mix_post_g),
        _adamw("adamw_ffn_pre_g", ffn_pre_g, lead(g_ffn_pre), m_ffn_pre_g, v_ffn_pre_g),
        [lead(a) for a in _adamw("adamw_w_up", w_up[0], l_up, m_w_up[0], v_w_up[0])],
        [lead(a) for a in _adamw("adamw_ffn_conv_w", ffn_conv_w[0], lead(g_fcw_mine), m_ffn_conv_w[0], v_ffn_conv_w[0])],
        _adamw("adamw_ffn_conv_b", ffn_conv_b, lead(g_fcb), m_ffn_conv_b, v_ffn_conv_b),
        [lead(a) for a in _adamw("adamw_w_down", w_down[0], l_down, m_w_down[0], v_w_down[0])],
        _adamw("adamw_ffn_post_g", ffn_post_g, lead(g_ffn_post), m_ffn_post_g, v_ffn_post_g),
    ]
    grads = [u_[0] for u_ in upd]
    deltas = [u_[1] for u_ in upd]
    new_m = [u_[2] for u_ in upd]
    new_v = [u_[3] for u_ in upd]
    return (loss, grad_x, *grads, *deltas, *new_m, *new_v)
```

```python
import math

import jax
import jax.numpy as jnp
from jax import lax
from jax.experimental import pallas as pl
from jax.experimental.pallas import tpu as pltpu

F32 = jnp.float32
BF16 = jnp.bfloat16
HI = lax.Precision.HIGHEST

D_MODEL = 1024
N_META = 16
BLK = 128
PAD = BLK - N_META
HEADS = 16
HEAD_DIM = 64
SSD_GROUPS = 2
SSD_STATE = 128
HEADS_PER_GROUP = HEADS // SSD_GROUPS
SSD_INNER = HEADS * HEAD_DIM
SSD_CONV = 4
XBC = SSD_INNER + 2 * SSD_GROUPS * SSD_STATE
OFF_B = SSD_INNER
OFF_C = SSD_INNER + SSD_GROUPS * SSD_STATE
D_FF = 2816
FFN_CONV = 3
EPS = 1e-6
SB_SCALE = 1.0 / math.sqrt(HEAD_DIM)
N_DEV = 8
LANES = 128
HALO = 8

ADAM_LR = 0.001
ADAM_B1 = 0.9
ADAM_B2 = 0.999
ADAM_EPS = 1e-08
ADAM_WD = 0.01
ADAM_STEP = 10

VMEM_FLOOR = 32 << 20
VMEM_CEIL = 60 << 20
MM_BUDGET = 20 << 20
ROW_BUDGET = 6 << 20

NN = (((1,), (0,)), ((), ()))
NT = (((1,), (1,)), ((), ()))
TN = (((0,), (0,)), ((), ()))


def _params(tile_bytes, sem=None):
    limit = int(min(max(2 * tile_bytes + (8 << 20), VMEM_FLOOR), VMEM_CEIL))
    return pltpu.CompilerParams(vmem_limit_bytes=limit, dimension_semantics=sem)


def _nbytes(shape, dtype):
    n = 1
    for s in shape:
        n *= s
    return n * jnp.dtype(dtype).itemsize


def _dot(a, b, dims=NN, precision=None):
    return lax.dot_general(a, b, dims, precision=precision, preferred_element_type=F32)


def _softplus(x):
    return jnp.maximum(x, 0.0) + jnp.log1p(jnp.exp(-jnp.abs(x)))


def _rms(x, g):
    r = lax.rsqrt(jnp.mean(x * x, axis=-1, keepdims=True) + EPS)
    return x * r * g


def _rms_bwd(dy, x, g):
    r = lax.rsqrt(jnp.mean(x * x, axis=-1, keepdims=True) + EPS)
    xh = x * r
    u = dy * g
    dx = r * (u - xh * jnp.mean(xh * u, axis=-1, keepdims=True))
    return dx, jnp.sum(dy * xh, axis=0, keepdims=True)


def _gelu(x):
    c = math.sqrt(2.0 / math.pi)
    return 0.5 * x * (1.0 + jnp.tanh(c * (x + 0.044715 * x * x * x)))


def _gelu_and_grad(x):
    c = math.sqrt(2.0 / math.pi)
    x2 = x * x
    t = jnp.tanh(c * (x + 0.044715 * x2 * x))
    half = 0.5 * (1.0 + t)
    return x * half, half + 0.5 * x * (1.0 - t * t) * c * (1.0 + 3.0 * 0.044715 * x2)


def _row_tile(rows, bytes_per_row):
    big = 384
    return big if rows % big == 0 and big * bytes_per_row <= ROW_BUDGET else BLK


def _mm(name, pairs, mode, out_dtype):
    a0, b0 = pairs[0]
    if mode == "tn":
        m, n = a0.shape[1], b0.shape[1]
    elif mode == "nt":
        m, n = a0.shape[0], b0.shape[0]
    else:
        m, n = a0.shape[0], b0.shape[1]
    dims = {"nn": NN, "nt": NT, "tn": TN}[mode]

    def tile_bytes(tm, tn):
        tot = tm * tn * jnp.dtype(out_dtype).itemsize
        for a, b in pairs:
            k = a.shape[0] if mode == "tn" else a.shape[1]
            tot += tm * k * a.dtype.itemsize + tn * k * b.dtype.itemsize
        return tot

    cands_m = [t for t in (1408, 1024, 512, 384, 256, 128) if m % t == 0] or [m]
    cands_n = [t for t in (1408, 1024, 768, 512, 256, 128) if n % t == 0] or [n]
    best = None
    for tm in cands_m:
        for tn in cands_n:
            if tile_bytes(tm, tn) <= MM_BUDGET and (best is None or tm * tn > best[0] * best[1]):
                best = (tm, tn)
    tm, tn = best if best is not None else (cands_m[-1], cands_n[-1])
    npairs = len(pairs)

    def body(*refs):
        o_ref = refs[2 * npairs]
        acc = None
        for p in range(npairs):
            part = _dot(refs[2 * p][...], refs[2 * p + 1][...], dims)
            acc = part if acc is None else acc + part
        o_ref[...] = acc.astype(o_ref.dtype)

    in_specs, args = [], []
    for a, b in pairs:
        if mode == "tn":
            k = a.shape[0]
            in_specs += [pl.BlockSpec((k, tm), lambda i, j: (0, i)), pl.BlockSpec((k, tn), lambda i, j: (0, j))]
        elif mode == "nt":
            k = a.shape[1]
            in_specs += [pl.BlockSpec((tm, k), lambda i, j: (i, 0)), pl.BlockSpec((tn, k), lambda i, j: (j, 0))]
        else:
            k = a.shape[1]
            in_specs += [pl.BlockSpec((tm, k), lambda i, j: (i, 0)), pl.BlockSpec((k, tn), lambda i, j: (0, j))]
        args += [a, b]
    return pl.pallas_call(
        body, name=name, grid=(m // tm, n // tn), in_specs=in_specs,
        out_specs=pl.BlockSpec((tm, tn), lambda i, j: (i, j)),
        out_shape=jax.ShapeDtypeStruct((m, n), out_dtype),
        compiler_params=_params(tile_bytes(tm, tn), ("parallel", "parallel")),
    )(*args)


class _Window:
    def __init__(self, ref, cols):
        self.ref, self.cols = ref, cols

    def __getitem__(self, idx):
        return self.ref[:, self.cols] if idx is Ellipsis else self.ref[idx[0], self.cols]


def _rowcall(name, fn, rows=(), prevs=(), nexts=(), pars=(), out_rows=(), out_accs=(), chunk=None, late=()):
    rows, prevs, nexts, pars, late = list(rows), list(prevs), list(nexts), list(pars), list(late)
    n_rows = (rows + prevs + nexts)[0].shape[0]
    per_row = sum(a.shape[1] * a.dtype.itemsize for a in rows + prevs + nexts + late)
    per_row += sum(c * jnp.dtype(dt).itemsize for c, dt in out_rows) + sum(a.shape[1] * 4 for a in prevs + nexts)
    tm = _row_tile(n_rows, per_row)
    nt = n_rows // tm
    hb = tm // HALO
    if late:
        tm = BLK
        nt, hb = n_rows // tm, tm // HALO
    rows = rows + late
    nr, npv, nnx, npar, nor, noa = len(rows), len(prevs), len(nexts), len(pars), len(out_rows), len(out_accs)

    def body(*refs):
        i = pl.program_id(0)
        k = 0
        row_refs = refs[k:k + nr]; k += nr
        pc = refs[k:k + npv]; k += npv
        ph = refs[k:k + npv]; k += npv
        nc = refs[k:k + nnx]; k += nnx
        nh = refs[k:k + nnx]; k += nnx
        par_refs = refs[k:k + npar]; k += npar
        orow = refs[k:k + nor]; k += nor
        oacc = refs[k:k + noa]; k += noa
        pscr = refs[k:k + npv]; k += npv
        nscr = refs[k:k + nnx]
        for c_, h_, s_ in zip(pc, ph, pscr):
            s_[0:HALO, :] = h_[...] * (i > 0).astype(F32)
            s_[HALO:HALO + tm, :] = c_[...]
        for c_, h_, s_ in zip(nc, nh, nscr):
            s_[0:tm, :] = c_[...]
            s_[tm:tm + HALO, :] = h_[...] * (i < nt - 1).astype(F32)
        if noa:
            @pl.when(i == 0)
            def _():
                for r_ in oacc:
                    r_[...] = jnp.zeros_like(r_)
        width = out_rows[0][0]
        windows = [slice(None)] if chunk is None else [slice(c0, c0 + chunk) for c0 in range(0, width, chunk)]
        for cols in windows:
            prev_fns = [(lambda s, s_=s_, cols=cols: s_[pl.ds(HALO - s, tm), cols]) for s_ in pscr]
            next_fns = [(lambda s, s_=s_, cols=cols: s_[pl.ds(s, tm), cols]) for s_ in nscr]
            row_vals, acc_vals = fn(i, tm, [_Window(r_, cols) for r_ in row_refs], prev_fns, next_fns,
                                    [_Window(r_, cols) for r_ in par_refs])
            for r_, v in zip(orow, row_vals):
                r_[:, cols] = v.astype(r_.dtype)
            for r_, v in zip(oacc, acc_vals):
                r_[:, cols] += v

    def row_spec(a):
        return pl.BlockSpec((tm, a.shape[1]), lambda i: (i, 0))

    def whole(shape):
        return pl.BlockSpec(shape, lambda i: (0, 0))

    in_specs = [row_spec(a) for a in rows[:nr - len(late)]]
    in_specs += [pl.BlockSpec((tm, a.shape[1]), lambda i: (jnp.maximum(i - 1, 0), 0)) for a in late]
    in_specs += [row_spec(a) for a in prevs]
    in_specs += [pl.BlockSpec((HALO, a.shape[1]), lambda i: (jnp.maximum(i * hb - 1, 0), 0)) for a in prevs]
    in_specs += [row_spec(a) for a in nexts]
    in_specs += [pl.BlockSpec((HALO, a.shape[1]), lambda i: (jnp.minimum((i + 1) * hb, n_rows // HALO - 1), 0)) for a in nexts]
    in_specs += [whole(p.shape) for p in pars]
    out_specs = [pl.BlockSpec((tm, c), lambda i: (i, 0)) for c, _ in out_rows] + [whole(s) for s in out_accs]
    out_shape = [jax.ShapeDtypeStruct((n_rows, c), dt) for c, dt in out_rows]
    out_shape += [jax.ShapeDtypeStruct(s, F32) for s in out_accs]
    scratch = [pltpu.VMEM((tm + HALO, a.shape[1]), F32) for a in prevs + nexts]
    tile = tm * per_row
    res = pl.pallas_call(
        body, name=name, grid=(nt,), in_specs=in_specs, out_specs=out_specs, out_shape=out_shape,
        scratch_shapes=scratch, compiler_params=_params(2 * tile, ("arbitrary",)),
    )(*rows, *prevs, *prevs, *nexts, *nexts, *pars)
    return res


def _row_ids(i, tm):
    return i * tm + lax.broadcasted_iota(jnp.int32, (tm, 1), 0)


def _conv_taps(prev_fn, w_ref, b_ref, taps):
    acc = b_ref[...]
    for k in range(taps):
        acc = acc + prev_fn(taps - 1 - k) * w_ref[k:k + 1, :]
    return acc


def _rms_fwd(name, h, g):
    def fn(i, tm, rows, prevs, nexts, pars):
        return [_rms(rows[0][...], pars[0][...])], []
    return _rowcall(name, fn, rows=[h], pars=[g], out_rows=[(h.shape[1], BF16)])[0]


def _ssd_conv_fwd(xbc_raw, w, b):
    def fn(i, tm, rows, prevs, nexts, pars):
        c = _conv_taps(prevs[0], pars[0], pars[1], SSD_CONV)
        y = c * jax.nn.sigmoid(c)
        return [jnp.where(_row_ids(i, tm) >= PAD, y, 0.0)], []
    return _rowcall("ssd_conv_fwd", fn, prevs=[xbc_raw], pars=[w, b], out_rows=[(XBC, F32)], chunk=LANES)[0]


def _mix_post(mix, h0, g_post, g_pre):
    def fn(i, tm, rows, prevs, nexts, pars):
        h1 = rows[1][...] + _rms(rows[0][...], pars[0][...])
        return [h1, _rms(h1, pars[1][...])], []
    return _rowcall("mix_post", fn, rows=[mix, h0], pars=[g_post, g_pre],
                    out_rows=[(D_MODEL, F32), (D_MODEL, BF16)])


def _ffn_act(g_raw, u, w, b):
    def fn(i, tm, rows, prevs, nexts, pars):
        g = _conv_taps(prevs[0], pars[0], pars[1], FFN_CONV)
        return [_gelu(g) * rows[0][...]], []
    return _rowcall("ffn_act", fn, rows=[u], prevs=[g_raw], pars=[w, b], out_rows=[(D_FF, BF16)], chunk=LANES)[0]


def _loss_post(f, h1, target, g_post):
    def fn(i, tm, rows, prevs, nexts, pars):
        fv, g = rows[0][...], pars[0][...]
        h2 = rows[1][...] + _rms(fv, g)
        real = _row_ids(i, tm) >= BLK
        diff = jnp.where(real, h2 - rows[2][...], 0.0)
        loss = 0.5 * jnp.sum(jnp.mean(diff * diff, axis=-1, keepdims=True))
        dh2 = diff * (1.0 / D_MODEL)
        df, dg = _rms_bwd(dh2, fv, g)
        return [dh2, df], [jnp.zeros((1, LANES), F32) + loss, dg]
    return _rowcall("loss_post", fn, rows=[f, h1], late=[target], pars=[g_post],
                    out_rows=[(D_MODEL, F32), (D_MODEL, BF16)], out_accs=[(1, LANES), (1, D_MODEL)])


def _ffn_bwd_act(dact, u, g_raw, w, b):
    def fn(i, tm, rows, prevs, nexts, pars):
        g = _conv_taps(prevs[0], pars[0], pars[1], FFN_CONV)
        da = rows[0][...]
        gelu, grad = _gelu_and_grad(g)
        return [da * rows[1][...] * grad, da * gelu], []
    return _rowcall("ffn_bwd_act", fn, rows=[dact, u], prevs=[g_raw], pars=[w, b],
                    out_rows=[(D_FF, F32), (D_FF, BF16)], chunk=LANES)


def _conv_bwd(name, dy, x, w, taps):
    width = x.shape[1]

    def fn(i, tm, rows, prevs, nexts, pars):
        dy0 = nexts[0](0)
        dx = None
        for k in range(taps):
            term = nexts[0](taps - 1 - k) * pars[0][k:k + 1, :]
            dx = term if dx is None else dx + term
        dws = [jnp.sum(dy0 * prevs[0](taps - 1 - k), axis=0, keepdims=True) for k in range(taps)]
        return [dx], dws + [jnp.sum(dy0, axis=0, keepdims=True)]
    return _rowcall(name, fn, prevs=[x], nexts=[dy], pars=[w], out_rows=[(width, BF16)],
                    out_accs=[(1, width)] * (taps + 1), chunk=LANES)


def _mid_bwd(dxn2, h1, dh2, mix, g_pre, g_post):
    def fn(i, tm, rows, prevs, nexts, pars):
        d1, dg_pre = _rms_bwd(rows[0][...], rows[1][...], pars[0][...])
        dh1 = rows[2][...] + d1
        dmix, dg_post = _rms_bwd(dh1, rows[3][...], pars[1][...])
        return [dh1, dmix], [dg_pre, dg_post]
    return _rowcall("mid_bwd", fn, rows=[dxn2, h1, dh2, mix], pars=[g_pre, g_post],
                    out_rows=[(D_MODEL, F32), (D_MODEL, BF16)], out_accs=[(1, D_MODEL), (1, D_MODEL)])


def _norm_bwd(name, dy, x, g):
    def fn(i, tm, rows, prevs, nexts, pars):
        dx, dg = _rms_bwd(rows[0][...], rows[1][...], pars[0][...])
        return [dx], [dg]
    return _rowcall(name, fn, rows=[dy, x], pars=[g], out_rows=[(x.shape[1], BF16)], out_accs=[(1, x.shape[1])])


def _ssd_conv_bwd_act(dact, xbc_raw, w, b):
    def fn(i, tm, rows, prevs, nexts, pars):
        c = _conv_taps(prevs[0], pars[0], pars[1], SSD_CONV)
        s = jax.nn.sigmoid(c)
        dc = rows[0][...] * s * (1.0 + c * (1.0 - s))
        return [jnp.where(_row_ids(i, tm) >= PAD, dc, 0.0)], []
    return _rowcall("ssd_conv_bwd_act", fn, rows=[dact], prevs=[xbc_raw], pars=[w, b], out_rows=[(XBC, F32)],
                    chunk=LANES)[0]


def _first_bwd(dxn1, h0, dh1, g):
    def fn(i, tm, rows, prevs, nexts, pars):
        d0, dg = _rms_bwd(rows[0][...], rows[1][...], pars[0][...])
        return [rows[2][...] + d0], [dg]
    return _rowcall("first_bwd", fn, rows=[dxn1, h0, dh1], pars=[g], out_rows=[(D_MODEL, F32)],
                    out_accs=[(1, D_MODEL)])


def _ssd_chunk_terms(c, dtr_ref, bias_ref, alog_ref):
    ri = lax.broadcasted_iota(jnp.int32, (BLK, BLK), 0)
    ci = lax.broadcasted_iota(jnp.int32, (BLK, BLK), 1)
    causal = ri >= ci
    tril = causal.astype(F32)
    triu = (ri <= ci).astype(F32)
    rowmask = ((c * BLK + lax.broadcasted_iota(jnp.int32, (BLK, 1), 0)) >= PAD).astype(F32)
    dt = _softplus(dtr_ref[...] + bias_ref[...]) * rowmask
    a_neg = -jnp.exp(alog_ref[...])
    a = dt * a_neg
    cs = _dot(tril, a, NN, HI)
    cs_t = _dot(a, triu, TN, HI)
    return causal, triu, rowmask, dt, a_neg, cs, cs_t


def _decay_matrix(causal, cs_h, cs_t_h):
    return jnp.where(causal, jnp.exp(jnp.where(causal, cs_h - cs_t_h, 0.0)), 0.0)


def _head_spread(width):
    ri = lax.broadcasted_iota(jnp.int32, (LANES, HEADS * width), 0)
    ci = lax.broadcasted_iota(jnp.int32, (LANES, HEADS * width), 1)
    return ((ri * width <= ci) & (ci < (ri + 1) * width)).astype(BF16)


def _three_terms(x):
    hi = x.astype(BF16)
    rest = x - hi.astype(F32)
    mid = rest.astype(BF16)
    lo = (rest - mid.astype(F32)).astype(BF16)
    return jnp.concatenate([hi, mid, lo], axis=1)


def _spread(x, sel):
    return _dot(_three_terms(x), jnp.concatenate([sel, sel, sel], axis=0))


def _lane_sums(y, sel):
    return _dot(_three_terms(y), jnp.concatenate([sel, sel, sel], axis=1), NT)


def _ssd_spreads(dt, cs, d_ref, sel64, sel128):
    dt64 = _spread(dt, sel64)
    cs64 = _spread(cs, sel64)
    cs128 = _spread(cs, sel128)
    d64 = _spread(jnp.broadcast_to(d_ref[...], (HALO, LANES)), sel64)[0:1, :]
    cl64 = cs64[BLK - 1:BLK, :]
    return dt64, cs128, d64, jnp.exp(cs64), jnp.exp(cl64 - cs64), jnp.exp(cl64)


def _ssd_pair_terms(p, xbc_ref, dt64, cs128, cs_t, causal):
    lanes = _pair_lanes(p)
    xs = xbc_ref[:, lanes]
    decays = [_decay_matrix(causal, cs128[:, h * LANES:(h + 1) * LANES], cs_t[h:h + 1, :]) for h in (2 * p, 2 * p + 1)]
    return xs, xs * dt64[:, lanes], decays


def _ssd_fwd(xbc, dtr, z, dt_bias, a_log, d_skip, norm_g):
    n_rows = xbc.shape[0]
    nb = n_rows // BLK

    def body(xbc_ref, dtr_ref, z_ref, bias_ref, alog_ref, d_ref, g_ref, s64_ref, s128_ref,
             ypre_ref, yssd_ref, st_ref, state):
        c = pl.program_id(0)

        @pl.when(c == 0)
        def _():
            state[...] = jnp.zeros_like(state)

        st_ref[...] = state[...]
        causal, _, _, dt, _, cs, cs_t = _ssd_chunk_terms(c, dtr_ref, bias_ref, alog_ref)
        first = lax.broadcasted_iota(jnp.int32, (BLK, LANES), 1) < HEAD_DIM
        dt64, cs128, d64, from_start, to_end, chunk_decay = _ssd_spreads(dt, cs, d_ref, s64_ref[...], s128_ref[...])
        for g in range(SSD_GROUPS):
            b_b = xbc_ref[:, OFF_B + g * SSD_STATE:OFF_B + (g + 1) * SSD_STATE].astype(BF16)
            c_b = xbc_ref[:, OFF_C + g * SSD_STATE:OFF_C + (g + 1) * SSD_STATE].astype(BF16)
            cb = _dot(c_b, b_b, NT)
            for j in range(HEADS_PER_GROUP // 2):
                p = g * (HEADS_PER_GROUP // 2) + j
                lanes = _pair_lanes(p)
                xs, x_dt, decays = _ssd_pair_terms(p, xbc_ref, dt64, cs128, cs_t, causal)
                ms = [(cb * d).astype(BF16) for d in decays]
                s_p = state[:, lanes]
                y = _dot(jnp.concatenate(ms, axis=1), _head_stack(x_dt.astype(BF16), first))
                y = y + from_start[:, lanes] * _dot(c_b, s_p.astype(BF16))
                state[:, lanes] = chunk_decay[:, lanes] * s_p + _dot(b_b, (to_end[:, lanes] * x_dt).astype(BF16), TN)
                ypre_ref[:, lanes] = y + d64[:, lanes] * xs
        zz = z_ref[...]
        yg = ypre_ref[...] * (zz * jax.nn.sigmoid(zz))
        yssd_ref[...] = _rms(yg, g_ref[...]).astype(yssd_ref.dtype)

    blk = lambda w: pl.BlockSpec((BLK, w), lambda c: (c, 0))
    par = lambda a: pl.BlockSpec(a.shape, lambda c: (0, 0))
    sel64, sel128 = _head_spread(HEAD_DIM), _head_spread(LANES)
    return pl.pallas_call(
        body, name="ssd_fwd", grid=(nb,),
        in_specs=[blk(XBC), blk(LANES), blk(SSD_INNER), par(dt_bias), par(a_log), par(d_skip), par(norm_g),
                  par(sel64), par(sel128)],
        out_specs=[blk(SSD_INNER), blk(SSD_INNER), pl.BlockSpec((None, SSD_STATE, SSD_INNER), lambda c: (c, 0, 0))],
        out_shape=[jax.ShapeDtypeStruct((n_rows, SSD_INNER), F32), jax.ShapeDtypeStruct((n_rows, SSD_INNER), BF16),
                   jax.ShapeDtypeStruct((nb, SSD_STATE, SSD_INNER), F32)],
        scratch_shapes=[pltpu.VMEM((SSD_STATE, SSD_INNER), F32)],
        compiler_params=_params(8 << 20, ("arbitrary",)),
    )(xbc, dtr, z, dt_bias, a_log, d_skip, norm_g, sel64, sel128)


def _ssd_bwd(dy, ypre, z, xbc, dtr, states, dt_bias, a_log, d_skip, norm_g):
    n_rows = xbc.shape[0]
    nb = n_rows // BLK

    def body(dy_ref, ypre_ref, z_ref, xbc_ref, dtr_ref, st_ref, bias_ref, alog_ref, d_ref, g_ref, s64_ref, s128_ref,
             dz_ref, dxbc_ref, ddtr_ref, dgn_ref, dd_ref, dal_ref, ddtb_ref, dstate, dyp, red):
        step = pl.program_id(0)
        c = nb - 1 - step

        @pl.when(step == 0)
        def _():
            dstate[...] = jnp.zeros_like(dstate)
            for r_ in (dgn_ref, dd_ref, dal_ref, ddtb_ref):
                r_[...] = jnp.zeros_like(r_)

        yp, zz = ypre_ref[...], z_ref[...]
        sz = jax.nn.sigmoid(zz)
        silu = zz * sz
        dyg, dgn = _rms_bwd(dy_ref[...], yp * silu, g_ref[...])
        dgn_ref[...] += dgn
        dz_ref[...] = (dyg * yp * (sz * (1.0 + zz * (1.0 - sz)))).astype(dz_ref.dtype)
        dyp[...] = dyg * silu

        causal, triu, rowmask, dt, a_neg, cs, cs_t = _ssd_chunk_terms(c, dtr_ref, bias_ref, alog_ref)
        lane = lax.broadcasted_iota(jnp.int32, (1, LANES), 1)
        last_row = (lax.broadcasted_iota(jnp.int32, (BLK, 1), 0) == BLK - 1).astype(F32)
        first = lax.broadcasted_iota(jnp.int32, (BLK, LANES), 1) < HEAD_DIM
        sel64 = s64_ref[...]
        dt64, cs128, d64, from_start, to_end, chunk_decay = _ssd_spreads(dt, cs, d_ref, sel64, s128_ref[...])
        for g in range(SSD_GROUPS):
            b_b = xbc_ref[:, OFF_B + g * SSD_STATE:OFF_B + (g + 1) * SSD_STATE].astype(BF16)
            c_b = xbc_ref[:, OFF_C + g * SSD_STATE:OFF_C + (g + 1) * SSD_STATE].astype(BF16)
            cb = _dot(c_b, b_b, NT)
            b_twice = jnp.concatenate([b_b, b_b], axis=0)
            c_twice = jnp.concatenate([c_b, c_b], axis=0)
            db_g = jnp.zeros((BLK, SSD_STATE), F32)
            dc_g = jnp.zeros((BLK, SSD_STATE), F32)
            for j in range(HEADS_PER_GROUP // 2):
                p = g * (HEADS_PER_GROUP // 2) + j
                lanes = _pair_lanes(p)
                xs, x_dt, decays = _ssd_pair_terms(p, xbc_ref, dt64, cs128, cs_t, causal)
                ms = [(cb * d).astype(BF16) for d in decays]
                d_y = dyp[:, lanes]
                s_p, ds_p = st_ref[:, lanes], dstate[:, lanes]
                s_b, ds_b = s_p.astype(BF16), ds_p.astype(BF16)
                fs, te = from_start[:, lanes], to_end[:, lanes]
                x_b, dy_b = x_dt.astype(BF16), d_y.astype(BF16)
                x_st = _head_stack(x_b, first)
                fs_dy = (fs * d_y).astype(BF16)

                y_diag = _dot(jnp.concatenate(ms, axis=1), x_st)
                y_off = fs * _dot(c_b, s_b)
                end_part = te * _dot(b_b, ds_b)
                dx_diag = _dot(jnp.concatenate(ms, axis=0), _head_stack(dy_b, first), TN)
                d_x = dx_diag + end_part
                g2 = _dot(dy_b, x_st, NT)
                gl = [(g2[:, k * BLK:(k + 1) * BLK] * decays[k]).astype(BF16) for k in range(2)]
                dc_g = dc_g + _dot(jnp.concatenate(gl, axis=1), b_twice) + _dot(fs_dy, s_b, NT)
                db_g = db_g + _dot(jnp.concatenate(gl, axis=0), c_twice, TN) + _dot((te * x_dt).astype(BF16), ds_b, NT)
                red[0:BLK, lanes] = (dy_b.astype(F32) * y_diag - x_b.astype(F32) * dx_diag) + (d_y * y_off - x_dt * end_part)
                red[BLK:2 * BLK, lanes] = x_dt * end_part
                red[2 * BLK:3 * BLK, lanes] = d_x * xs
                red[3 * BLK:3 * BLK + HALO, lanes] = jnp.broadcast_to(jnp.sum(ds_p * s_p, axis=0, keepdims=True), (HALO, LANES))
                red[3 * BLK + HALO:3 * BLK + 2 * HALO, lanes] = jnp.broadcast_to(
                    jnp.sum(d_y * xs, axis=0, keepdims=True), (HALO, LANES))
                dxbc_ref[:, lanes] = d_x * dt64[:, lanes] + d64[:, lanes] * d_y
                dstate[:, lanes] = chunk_decay[:, lanes] * ds_p + _dot(c_b, fs_dy, TN)
            dxbc_ref[:, OFF_B + g * SSD_STATE:OFF_B + (g + 1) * SSD_STATE] = db_g
            dxbc_ref[:, OFF_C + g * SSD_STATE:OFF_C + (g + 1) * SSD_STATE] = dc_g
        sums = _lane_sums(red[...], sel64)
        at_last = (jnp.sum(sums[BLK:2 * BLK], axis=0, keepdims=True)
                   + jnp.exp(cs[BLK - 1:BLK, :]) * sums[3 * BLK:3 * BLK + 1])
        dcs = sums[0:BLK] + last_row * at_last
        ddt_x = sums[2 * BLK:3 * BLK]
        dd_row = sums[3 * BLK + HALO:3 * BLK + HALO + 1]
        da = _dot(triu, dcs, NN, HI)
        ddt = (da * a_neg + ddt_x) * rowmask
        ddtr = ddt * jax.nn.sigmoid(dtr_ref[...] + bias_ref[...]) * (lane < HEADS).astype(F32)
        ddtr_ref[...] = ddtr.astype(ddtr_ref.dtype)
        ddtb_ref[...] += jnp.sum(ddtr, axis=0, keepdims=True)
        dal_ref[...] += jnp.sum(da * dt, axis=0, keepdims=True) * a_neg
        dd_ref[...] += dd_row

    blk = lambda w: pl.BlockSpec((BLK, w), lambda s: (nb - 1 - s, 0))
    par = lambda a: pl.BlockSpec(a.shape, lambda s: (0, 0))
    acc = lambda w: pl.BlockSpec((1, w), lambda s: (0, 0))
    sel64, sel128 = _head_spread(HEAD_DIM), _head_spread(LANES)
    return pl.pallas_call(
        body, name="ssd_bwd", grid=(nb,),
        in_specs=[blk(SSD_INNER), blk(SSD_INNER), blk(SSD_INNER), blk(XBC), blk(LANES),
                  pl.BlockSpec((None, SSD_STATE, SSD_INNER), lambda s: (nb - 1 - s, 0, 0)),
                  par(dt_bias), par(a_log), par(d_skip), par(norm_g), par(sel64), par(sel128)],
        out_specs=[blk(SSD_INNER), blk(XBC), blk(LANES), acc(SSD_INNER), acc(LANES), acc(LANES), acc(LANES)],
        out_shape=[jax.ShapeDtypeStruct((n_rows, SSD_INNER), BF16), jax.ShapeDtypeStruct((n_rows, XBC), F32),
                   jax.ShapeDtypeStruct((n_rows, LANES), BF16), jax.ShapeDtypeStruct((1, SSD_INNER), F32),
                   jax.ShapeDtypeStruct((1, LANES), F32), jax.ShapeDtypeStruct((1, LANES), F32),
                   jax.ShapeDtypeStruct((1, LANES), F32)],
        scratch_shapes=[pltpu.VMEM((SSD_STATE, SSD_INNER), F32), pltpu.VMEM((BLK, SSD_INNER), F32),
                        pltpu.VMEM((3 * BLK + 2 * HALO, SSD_INNER), F32)],
        compiler_params=_params(12 << 20, ("arbitrary",)),
    )(dy, ypre, z, xbc, dtr, states, dt_bias, a_log, d_skip, norm_g, sel64, sel128)


HEAD_GROUP = 4


LOG2_E = 1.4426950408889634
SOFTPLUS_CLAMP = 80.0


def _sb_logits(zl, valid):
    z2 = zl * LOG2_E
    lost = jnp.maximum(jnp.log2(1.0 + jnp.exp2(jnp.minimum(z2, SOFTPLUS_CLAMP))), z2)
    log_beta = z2 - lost
    if valid is not None:
        lost = jnp.where(valid, lost, 0.0)
    return log_beta, lost


def _tri_and_ones(tri, sign=1.0):
    half = sign * jnp.concatenate([tri, jnp.ones((BLK, BLK), F32)], axis=1)
    return jnp.concatenate([half, half], axis=0).astype(BF16)


def _tile_mask(i, j, ri, ci):
    key = j * BLK + ci
    return (key < i * BLK + ri) & (key >= PAD)


def _pair_lanes(p):
    return slice(p * 2 * HEAD_DIM, (p + 1) * 2 * HEAD_DIM)


def _head_stack(x_pair, first):
    zero = jnp.zeros_like(x_pair)
    return jnp.concatenate([jnp.where(first, x_pair, zero), jnp.where(first, zero, x_pair)], axis=0)


def _block_rows(j):
    return pl.ds(j * BLK if isinstance(j, int) else pl.multiple_of(j * BLK, BLK), BLK)


def _sweep_pairs(i, tile):
    n_calls = (i + 2) // 2
    tile(0, True)

    def mid(t, carry):
        tile(t, False)
        return carry

    lax.fori_loop(1, n_calls - 1, mid, 0)

    @pl.when(n_calls >= 2)
    def _():
        tile(n_calls - 1, True)

    return n_calls


def _two_terms(x):
    hi = x.astype(BF16)
    return jnp.concatenate([hi, (x - hi.astype(F32)).astype(BF16)], axis=1)


def _ride_along(ex, n_in, n_out, refs):
    k = 0
    parts = []
    for cnt in (n_in, ex.n, n_out, ex.n):
        parts.append(refs[k:k + cnt])
        k += cnt
    n_sems = len(ex.scratch)
    return (*parts, refs[k:len(refs) - n_sems], refs[len(refs) - n_sems:])


def _attn_fwd(q, k, v, ex, ex_arrays):
    n_rows, width = q.shape
    nb = n_rows // BLK
    gw = HEAD_GROUP * HEAD_DIM
    groups = width // gw

    def body(*refs):
        (q_ref, k_ref, v_ref), ex_in, (o_ref, tot_ref), ex_out, (run_ref, z_ref, w_ref), sems = _ride_along(ex, 3, 2, refs)
        step = pl.program_id(0)

        @pl.when(step == 0)
        def _():
            ex.start(ex_in, ex_out, sems)

        ri = lax.broadcasted_iota(jnp.int32, (BLK, BLK), 0)
        ci = lax.broadcasted_iota(jnp.int32, (BLK, BLK), 1)
        sums = _tri_and_ones((ri > ci).astype(F32), -1.0)
        first = ci < HEAD_DIM
        heads, pairs = range(HEAD_GROUP), range(HEAD_GROUP // 2)

        def stacks(ref, blocks, p):
            return jnp.concatenate([_head_stack(ref[_block_rows(jnp.clip(j, 0, nb - 1)), _pair_lanes(p)], first)
                                    for j in blocks], axis=0)

        def q_block(i, carry):
            rows = _block_rows(i)
            o_ref[rows, :] = jnp.zeros((BLK, gw), F32)
            run_ref[...] = jnp.zeros_like(run_ref)
            w_ref[...] = jnp.zeros_like(w_ref)

            def blocks(t):
                return i - 2 * t, i - 2 * t - 1

            for p in pairs:
                z_ref[p] = _dot(q_ref[rows, _pair_lanes(p)], stacks(k_ref, blocks(0), p), NT)

            def tile(t, masked):
                q_i, o_i = q_ref[rows, :], o_ref[rows, :]
                outs = [_dot(w_ref[p], stacks(v_ref, blocks(jnp.maximum(t - 1, 0)), p)) for p in pairs]
                z_next = [_dot(q_i[:, _pair_lanes(p)], stacks(k_ref, blocks(t + 1), p), NT) for p in pairs]
                runs = [run_ref[h] for h in heads]
                lgs, res = [], []
                for s, j in enumerate(blocks(t)):
                    valid = _tile_mask(i, j, ri, ci) if masked else None
                    lgs.append([_sb_logits(z_ref[h // 2][:, (2 * s + h % 2) * BLK:(2 * s + h % 2 + 1) * BLK], valid)
                                for h in heads])
                    res.append(_dot(jnp.concatenate([_two_terms(lgs[s][h][1]) for h in heads], axis=0), sums))
                ws = []
                for s, j in enumerate(blocks(t)):
                    valid = _tile_mask(i, j, ri, ci) if masked else None
                    for h in heads:
                        part = res[s][h * BLK:(h + 1) * BLK]
                        w = jnp.exp2(lgs[s][h][0] + part[:, :BLK] + runs[h])
                        ws.append(jnp.where(valid, w, 0.0) if masked else w)
                        runs[h] = runs[h] + part[:, BLK:]
                for p in pairs:
                    w_ref[p] = jnp.concatenate([ws[s * HEAD_GROUP + 2 * p + e].astype(BF16) for s in range(2) for e in range(2)],
                                               axis=1)
                    z_ref[p] = z_next[p]
                o_ref[rows, :] = o_i + jnp.concatenate(outs, axis=1)
                for h in heads:
                    run_ref[h] = runs[h]

            n_calls = _sweep_pairs(i, tile)
            o_ref[rows, :] += jnp.concatenate([_dot(w_ref[p], stacks(v_ref, blocks(n_calls - 1), p)) for p in pairs], axis=1)
            for h in heads:
                tot_ref[rows, h:h + 1] = run_ref[h][:, 0:1]
            return carry

        lax.fori_loop(0, nb, q_block, 0)

        @pl.when(step == groups - 1)
        def _():
            ex.wait(ex_in, ex_out, sems)

    spec = pl.BlockSpec((n_rows, gw), lambda g: (0, g))
    tot_spec = pl.BlockSpec((None, n_rows, HEAD_GROUP), lambda g: (g, 0, 0))
    res = pl.pallas_call(
        body, name="attn_fwd", grid=(groups,), in_specs=[spec, spec, spec] + ex.specs,
        out_specs=[spec, tot_spec] + ex.specs,
        out_shape=[jax.ShapeDtypeStruct((n_rows, width), F32),
                   jax.ShapeDtypeStruct((groups, n_rows, HEAD_GROUP), F32)] + ex.out_shape,
        scratch_shapes=[pltpu.VMEM((HEAD_GROUP, BLK, BLK), F32), pltpu.VMEM((HEAD_GROUP // 2, BLK, 4 * BLK), F32),
                        pltpu.VMEM((HEAD_GROUP // 2, BLK, 4 * BLK), BF16)] + ex.scratch,
        compiler_params=_params(n_rows * (3 * gw * 2 + gw * 4 + LANES * 4), ("arbitrary",)),
    )(q, k, v, *ex_arrays)
    return res[0], res[1], res[2:]


def _attn_bwd(q, k, v, keep_total, do, ex, ex_arrays):
    n_rows, width = q.shape
    nb = n_rows // BLK
    gw = HEAD_GROUP * HEAD_DIM
    groups = width // gw

    def body(*refs):
        ((q_ref, k_ref, v_ref, tot_ref, do_ref), ex_in, (dq_ref, dk_ref, dv_ref), ex_out,
         (dq_acc, dk_acc, dv_acc, tot_b, run_ref, rung_ref, z_ref, dw_ref, dz_ref, wb_ref, qst_ref, dost_ref),
         sems) = _ride_along(ex, 5, 3, refs)
        step = pl.program_id(0)

        @pl.when(step == 0)
        def _():
            ex.start(ex_in, ex_out, sems)

        ri = lax.broadcasted_iota(jnp.int32, (BLK, BLK), 0)
        ci = lax.broadcasted_iota(jnp.int32, (BLK, BLK), 1)
        sums_keep = _tri_and_ones((ri <= ci).astype(F32), -1.0)
        sums_g = _tri_and_ones((ri < ci).astype(F32))
        first = ci < HEAD_DIM
        heads, pairs = range(HEAD_GROUP), range(HEAD_GROUP // 2)
        dk_acc[...] = jnp.zeros_like(dk_acc)
        dv_acc[...] = jnp.zeros_like(dv_acc)

        def clamp(j):
            return jnp.clip(j, 0, nb - 1)

        def stacks(ref, blocks, p):
            return jnp.concatenate([_head_stack(ref[_block_rows(clamp(j)), _pair_lanes(p)], first) for j in blocks], axis=0)

        def blocks(t):
            return 2 * t, 2 * t + 1

        def kept(ref, p, axis):
            tiles = [[ref[s * HEAD_GROUP + 2 * p + e] for e in range(2)] for s in range(2)]
            if axis == 1:
                return jnp.concatenate(tiles[0] + tiles[1], axis=1)
            return jnp.concatenate([jnp.concatenate(tiles[s], axis=0) for s in range(2)], axis=1)

        def owed_dq(t_prev):
            return [_dot(kept(dz_ref, p, 1), stacks(k_ref, blocks(t_prev), p)) for p in pairs]

        def owed_dk():
            return [_dot(kept(dz_ref, p, 0), qst_ref[p], TN) for p in pairs]

        def owed_dv():
            return [_dot(kept(wb_ref, p, 0), dost_ref[p], TN) for p in pairs]

        def settle(t_prev, parts):
            dq, dk, dv = parts
            dq_acc[...] += jnp.concatenate(dq, axis=1)
            for s, j in enumerate(blocks(t_prev)):
                cols = _block_rows(clamp(j))
                dk_acc[cols, :] += jnp.concatenate([d[s * BLK:(s + 1) * BLK] for d in dk], axis=1)
                dv_acc[cols, :] += jnp.concatenate([d[s * BLK:(s + 1) * BLK] for d in dv], axis=1)

        def q_block(i, carry):
            rows = _block_rows(i)
            dq_acc[...] = jnp.zeros_like(dq_acc)
            run_ref[...] = jnp.zeros_like(run_ref)
            rung_ref[...] = jnp.zeros_like(rung_ref)
            dz_ref[...] = jnp.zeros_like(dz_ref)
            wb_ref[...] = jnp.zeros_like(wb_ref)
            for h in heads:
                tot_b[h] = jnp.broadcast_to(tot_ref[rows, h:h + 1], (BLK, BLK))
            for p in pairs:
                qst_ref[p] = _head_stack(q_ref[rows, _pair_lanes(p)], first)
                dost_ref[p] = _head_stack(do_ref[rows, _pair_lanes(p)], first)
                z_ref[p] = _dot(q_ref[rows, _pair_lanes(p)], stacks(k_ref, blocks(0), p), NT)
                dw_ref[p] = _dot(do_ref[rows, _pair_lanes(p)], stacks(v_ref, blocks(0), p), NT)

            def tile(t, masked):
                q_i, do_i = q_ref[rows, :], do_ref[rows, :]
                t_prev = jnp.maximum(t - 1, 0)
                valids = [_tile_mask(i, j, ri, ci) if masked else None for j in blocks(t)]
                tile_of = lambda ref, s, h: ref[h // 2][:, (2 * s + h % 2) * BLK:(2 * s + h % 2 + 1) * BLK]
                runs = [run_ref[h] for h in heads]
                rungs = [rung_ref[h] for h in heads]
                lgs, keep, ws, gs, gsum, dzs = [], [], [], [], [], []
                for s in range(2):
                    lgs.append([_sb_logits(tile_of(z_ref, s, h), valids[s]) for h in heads])
                    keep.append(_dot(jnp.concatenate([_two_terms(lgs[s][h][1]) for h in heads], axis=0), sums_keep))
                    if s == 0:
                        part_dq = owed_dq(t_prev)
                    else:
                        part_dk = owed_dk()
                for s in range(2):
                    ws.append([])
                    gs.append([])
                    for h in heads:
                        part = keep[s][h * BLK:(h + 1) * BLK]
                        w = jnp.exp2(lgs[s][h][0] + (tot_b[h] - runs[h] - part[:, :BLK]))
                        ws[s].append(jnp.where(valids[s], w, 0.0) if masked else w)
                        runs[h] = runs[h] + part[:, BLK:]
                        gs[s].append(tile_of(dw_ref, s, h) * ws[s][h])
                    gsum.append(_dot(jnp.concatenate([_two_terms(gs[s][h]) for h in heads], axis=0), sums_g))
                    if s == 0:
                        part_dv = owed_dv()
                    else:
                        z_next = [_dot(q_i[:, _pair_lanes(p)], stacks(k_ref, blocks(t + 1), p), NT) for p in pairs]
                for s in range(2):
                    dzs.append([])
                    for h in heads:
                        part = gsum[s][h * BLK:(h + 1) * BLK]
                        beta = jnp.exp2(lgs[s][h][0])
                        dz = gs[s][h] * (1.0 - beta) - beta * (part[:, :BLK] + rungs[h])
                        dzs[s].append(jnp.where(valids[s], dz, 0.0) if masked else dz)
                        rungs[h] = rungs[h] + part[:, BLK:]
                    if s == 0:
                        dw_next = [_dot(do_i[:, _pair_lanes(p)], stacks(v_ref, blocks(t + 1), p), NT) for p in pairs]
                settle(t_prev, (part_dq, part_dk, part_dv))
                for s in range(2):
                    for h in heads:
                        dz_ref[s * HEAD_GROUP + h] = dzs[s][h].astype(BF16)
                        wb_ref[s * HEAD_GROUP + h] = ws[s][h].astype(BF16)
                for h in heads:
                    run_ref[h] = runs[h]
                    rung_ref[h] = rungs[h]
                for p in pairs:
                    z_ref[p] = z_next[p]
                    dw_ref[p] = dw_next[p]

            n_calls = _sweep_pairs(i, tile)
            settle(n_calls - 1, (owed_dq(n_calls - 1), owed_dk(), owed_dv()))
            dq_ref[rows, :] = (dq_acc[...] * SB_SCALE).astype(dq_ref.dtype)
            return carry

        lax.fori_loop(0, nb, q_block, 0)
        dk_ref[...] = dk_acc[...].astype(dk_ref.dtype)
        dv_ref[...] = dv_acc[...].astype(dv_ref.dtype)

        @pl.when(step == groups - 1)
        def _():
            ex.wait(ex_in, ex_out, sems)

    spec = pl.BlockSpec((n_rows, gw), lambda g: (0, g))
    tot_spec = pl.BlockSpec((None, n_rows, HEAD_GROUP), lambda g: (g, 0, 0))
    out = jax.ShapeDtypeStruct((n_rows, width), BF16)
    tile_f32 = pltpu.VMEM((HEAD_GROUP, BLK, BLK), F32)
    tile_bf16 = pltpu.VMEM((2 * HEAD_GROUP, BLK, BLK), BF16)
    pair_f32 = pltpu.VMEM((HEAD_GROUP // 2, BLK, 4 * BLK), F32)
    pair_stack = pltpu.VMEM((HEAD_GROUP // 2, 2 * BLK, BLK), BF16)
    res = pl.pallas_call(
        body, name="attn_bwd", grid=(groups,), in_specs=[spec, spec, spec, tot_spec, spec] + ex.specs,
        out_specs=[spec] * 3 + ex.specs, out_shape=[out] * 3 + ex.out_shape,
        scratch_shapes=[pltpu.VMEM((BLK, gw), F32), pltpu.VMEM((n_rows, gw), F32), pltpu.VMEM((n_rows, gw), F32),
                        tile_f32, tile_f32, tile_f32, pair_f32, pair_f32, tile_bf16, tile_bf16, pair_stack,
                        pair_stack] + ex.scratch,
        compiler_params=_params(n_rows * (7 * gw * 2 + LANES * 4 + gw * 4), ("arbitrary",)),
    )(q, k, v, keep_total, do, *ex_arrays)
    return res[0], res[1], res[2], res[3:]


class _Exchange:
    def __init__(self, arrays, gather):
        self.n = len(arrays)
        self.gather = gather
        self.out_shape = [jax.ShapeDtypeStruct(((N_DEV,) + a.shape) if gather else a.shape, a.dtype) for a in arrays]
        self.scratch = [pltpu.SemaphoreType.DMA((self.n, N_DEV - 1)), pltpu.SemaphoreType.DMA((self.n, N_DEV - 1)),
                        pltpu.SemaphoreType.DMA((self.n,))]
        self.specs = [pl.BlockSpec(memory_space=pl.ANY)] * self.n

    def _copies(self, ins, outs, sems):
        send_sems, recv_sems, local_sems = sems
        x, y, c = lax.axis_index("x"), lax.axis_index("y"), lax.axis_index("c")
        me = 4 * x + 2 * y + c
        gather = self.gather
        local, sends, recvs = [], [], []
        for a in range(self.n):
            local.append(pltpu.make_async_copy(ins[a] if gather else ins[a].at[me], outs[a].at[me if gather else 0],
                                               local_sems.at[a]))
        for r in range(1, N_DEV):
            px = 1 - x if r & 4 else x
            py = 1 - y if r & 2 else y
            pc = 1 - c if r & 1 else c
            idx = 4 * px + 2 * py + pc
            for a in range(self.n):
                src = ins[a] if gather else ins[a].at[idx]
                pair = dict(send_sem=send_sems.at[a, r - 1], recv_sem=recv_sems.at[a, r - 1],
                            device_id=(px, py, pc), device_id_type=pl.DeviceIdType.MESH)
                sends.append(pltpu.make_async_remote_copy(src_ref=src, dst_ref=outs[a].at[me if gather else r], **pair))
                recvs.append(pltpu.make_async_remote_copy(src_ref=src, dst_ref=outs[a].at[idx if gather else r], **pair))
        return local, sends, recvs

    def start(self, ins, outs, sems):
        local, sends, _ = self._copies(ins, outs, sems)
        for cp in local + sends:
            cp.start()

    def wait(self, ins, outs, sems):
        local, sends, recvs = self._copies(ins, outs, sems)
        for cp in recvs:
            cp.wait_recv()
        for cp in sends:
            cp.wait_send()
        for cp in local:
            cp.wait()


def _peer_exchange(name, parts):
    exs = [_Exchange(arrays, gather) for arrays, gather in parts]
    n = sum(ex.n for ex in exs)
    n_sems = len(exs[0].scratch)

    def body(*refs):
        at, views = 0, []
        for k, ex in enumerate(exs):
            views.append((ex, refs[at:at + ex.n], refs[n + at:n + at + ex.n],
                          refs[2 * n + k * n_sems:2 * n + (k + 1) * n_sems]))
            at += ex.n
        for ex, ins, outs, sems in views:
            ex.start(ins, outs, sems)
        for ex, ins, outs, sems in views:
            ex.wait(ins, outs, sems)

    res = pl.pallas_call(
        body, name=name, in_specs=[s for ex in exs for s in ex.specs], out_specs=[s for ex in exs for s in ex.specs],
        out_shape=[s for ex in exs for s in ex.out_shape], scratch_shapes=[s for ex in exs for s in ex.scratch],
    )(*[a for arrays, _ in parts for a in arrays])
    out, at = [], 0
    for ex in exs:
        out.append(res[at:at + ex.n])
        at += ex.n
    return out


def _gather_two_level(name, arrays):
    n = len(arrays)

    def body(*refs):
        ins, outs = refs[:n], refs[n:2 * n]
        send_sems, recv_sems, local_sems = refs[2 * n:]
        x, y, c = lax.axis_index("x"), lax.axis_index("y"), lax.axis_index("c")
        sibling = (x, y, 1 - c)
        chips = [(1 - x, y), (x, 1 - y), (1 - x, 1 - y)]

        def slot(a, dev):
            return outs[a].at[4 * dev[0] + 2 * dev[1] + dev[2]]

        def copy(a, k, block, to, src=None):
            return pltpu.make_async_remote_copy(
                src_ref=slot(a, block) if src is None else src, dst_ref=slot(a, block),
                send_sem=send_sems.at[a, k], recv_sem=recv_sems.at[a, k], device_id=to, device_id_type=pl.DeviceIdType.MESH)

        me = (x, y, c)
        mine = [pltpu.make_async_copy(ins[a], slot(a, me), local_sems.at[a]) for a in range(n)]
        first = [copy(a, 0, me, sibling, src=ins[a]) for a in range(n)]
        first += [copy(a, 1 + j, me, (*chip, c), src=ins[a]) for j, chip in enumerate(chips) for a in range(n)]
        for cp in mine + first:
            cp.start()
        passed = []
        for j, chip in enumerate(chips):
            for a in range(n):
                copy(a, 1 + j, (*chip, c), me).wait_recv()
            for a in range(n):
                cp = copy(a, 4 + j, (*chip, c), sibling)
                cp.start()
                passed.append(cp)
        for a in range(n):
            copy(a, 0, sibling, me).wait_recv()
            for j, chip in enumerate(chips):
                copy(a, 4 + j, (*chip, 1 - c), me).wait_recv()
        for cp in first + passed:
            cp.wait_send()
        for cp in mine:
            cp.wait()

    any_spec = pl.BlockSpec(memory_space=pl.ANY)
    return pl.pallas_call(
        body, name=name, in_specs=[any_spec] * n, out_specs=[any_spec] * n,
        out_shape=[jax.ShapeDtypeStruct((N_DEV,) + a.shape, a.dtype) for a in arrays],
        scratch_shapes=[pltpu.SemaphoreType.DMA((n, N_DEV - 1)), pltpu.SemaphoreType.DMA((n, N_DEV - 1)),
                        pltpu.SemaphoreType.DMA((n,))],
    )(*arrays)


def _sum_slots(name, x):
    def body(x_ref, o_ref):
        acc = x_ref[0]
        for s in range(1, N_DEV):
            acc = acc + x_ref[s]
        o_ref[...] = acc
    return pl.pallas_call(body, name=name, out_shape=jax.ShapeDtypeStruct(x.shape[1:], F32))(x)


def _adamw(name, w, slots, m, v):
    n_slots, rows, cols = slots.shape
    tr = next((t for t in (256, 176, 128) if rows % t == 0 and rows > t), rows)

    def body(w_ref, s_ref, m_ref, v_ref, g_ref, d_ref, nm_ref, nv_ref):
        g = s_ref[0].astype(F32)
        for s in range(1, n_slots):
            g = g + s_ref[s].astype(F32)
        nm = ADAM_B1 * m_ref[...] + (1.0 - ADAM_B1) * g
        nv = ADAM_B2 * v_ref[...] + (1.0 - ADAM_B2) * (g * g)
        m_hat = nm / (1.0 - ADAM_B1 ** ADAM_STEP)
        v_hat = nv / (1.0 - ADAM_B2 ** ADAM_STEP)
        g_ref[...] = g
        d_ref[...] = -ADAM_LR * (m_hat / (jnp.sqrt(v_hat) + ADAM_EPS) + ADAM_WD * w_ref[...])
        nm_ref[...] = nm
        nv_ref[...] = nv

    spec = pl.BlockSpec((tr, cols), lambda i: (i, 0))
    out = jax.ShapeDtypeStruct((rows, cols), F32)
    return pl.pallas_call(
        body, name=name, grid=(rows // tr,),
        in_specs=[spec, pl.BlockSpec((n_slots, tr, cols), lambda i: (0, i, 0)), spec, spec],
        out_specs=[spec] * 4, out_shape=[out] * 4,
        compiler_params=_params((n_slots + 7) * tr * cols * 4, ("parallel",)),
    )(w, slots, m, v)


def _pad_lanes(a):
    return jnp.pad(a, ((0, 0), (0, LANES - a.shape[1])))


def kernel(x, meta_tokens, mix_pre_g, w_in, ssd_conv_w, ssd_conv_b, ssd_dt_bias, ssd_a_log, ssd_d, ssd_norm_g, sb_norm_g, w_out, mix_post_g, ffn_pre_g, w_up, ffn_conv_w, ffn_conv_b, w_down, ffn_post_g, loss_target, m_meta_tokens, m_mix_pre_g, m_w_in, m_ssd_conv_w, m_ssd_conv_b, m_ssd_dt_bias, m_ssd_a_log, m_ssd_d, m_ssd_norm_g, m_sb_norm_g, m_w_out, m_mix_post_g, m_ffn_pre_g, m_w_up, m_ffn_conv_w, m_ffn_conv_b, m_w_down, m_ffn_post_g, v_meta_tokens, v_mix_pre_g, v_w_in, v_ssd_conv_w, v_ssd_conv_b, v_ssd_dt_bias, v_ssd_a_log, v_ssd_d, v_ssd_norm_g, v_sb_norm_g, v_w_out, v_mix_post_g, v_ffn_pre_g, v_w_up, v_ffn_conv_w, v_ffn_conv_b, v_w_down, v_ffn_post_g):
    seq = x.shape[1]
    me = 4 * lax.axis_index("x") + 2 * lax.axis_index("y") + lax.axis_index("c")
    in_cols = w_in.shape[2]
    up_cols = w_up.shape[2]
    out_rows = w_out.shape[1]
    down_rows = w_down.shape[1]

    g_in, g_meta, g_scw, g_fcw = _gather_two_level(
        "gather_w_in", [w_in[0].astype(BF16), meta_tokens, ssd_conv_w[0], ffn_conv_w[0]])
    late_weights = [w_out[0].astype(BF16), w_up[0].astype(BF16), w_down[0].astype(BF16)]
    w_in_full = g_in.transpose(1, 0, 2).reshape(D_MODEL, N_DEV * in_cols)
    off = [0, SSD_INNER, SSD_INNER + XBC, SSD_INNER + XBC + HEADS]
    w_z = w_in_full[:, off[0]:off[1]]
    w_xbc = w_in_full[:, off[1]:off[2]]
    w_dt = _pad_lanes(w_in_full[:, off[2]:off[3]])
    w_q = w_in_full[:, off[3]:off[3] + SSD_INNER]
    w_k = w_in_full[:, off[3] + SSD_INNER:off[3] + 2 * SSD_INNER]
    w_v = w_in_full[:, off[3] + 2 * SSD_INNER:off[3] + 3 * SSD_INNER]
    meta_full = g_meta.transpose(1, 0, 2).reshape(N_META, D_MODEL)
    scw_full = g_scw.transpose(1, 0, 2).reshape(SSD_CONV, XBC)
    fcw_full = g_fcw.transpose(1, 0, 2).reshape(FFN_CONV, D_FF)

    dt_bias_p, a_log_p, d_p = _pad_lanes(ssd_dt_bias), _pad_lanes(ssd_a_log), _pad_lanes(ssd_d)

    h0 = jnp.concatenate([jnp.zeros((PAD, D_MODEL), F32), meta_full, x[0]], axis=0)
    target = loss_target[0]
    xn1 = _rms_fwd("rms_pre_mix", h0, mix_pre_g)
    z = _mm("proj_z", [(xn1, w_z)], "nn", F32)
    xbc_raw = _mm("proj_xbc", [(xn1, w_xbc)], "nn", F32)
    dtr = _mm("proj_dt", [(xn1, w_dt)], "nn", F32)
    q = _mm("proj_q", [(xn1, w_q * SB_SCALE)], "nn", BF16)
    k = _mm("proj_k", [(xn1, w_k)], "nn", BF16)
    v = _mm("proj_v", [(xn1, w_v)], "nn", BF16)
    xbc_act = _ssd_conv_fwd(xbc_raw, scw_full, ssd_conv_b)
    ypre, y_ssd, states = _ssd_fwd(xbc_act, dtr, z, dt_bias_p, a_log_p, d_p, ssd_norm_g)
    o, keep_total, (g_out, g_up, g_down) = _attn_fwd(q, k, v, _Exchange(late_weights, gather=True), late_weights)
    w_out_full = g_out.reshape(N_DEV * out_rows, D_MODEL)
    wo_ssd, wo_sb = w_out_full[:SSD_INNER], w_out_full[SSD_INNER:]
    w_up_full = g_up.transpose(1, 0, 2).reshape(D_MODEL, N_DEV * up_cols)
    w_gate, w_lin = w_up_full[:, :D_FF], w_up_full[:, D_FF:]
    w_down_full = g_down.reshape(N_DEV * down_rows, D_MODEL)
    y_sb = _rms_fwd("rms_sb", o, sb_norm_g)
    mix = _mm("mix_out", [(y_ssd, wo_ssd), (y_sb, wo_sb)], "nn", F32)
    h1, xn2 = _mix_post(mix, h0, mix_post_g, ffn_pre_g)
    g_raw = _mm("ffn_gate", [(xn2, w_gate)], "nn", F32)
    u = _mm("ffn_lin", [(xn2, w_lin)], "nn", F32)
    act = _ffn_act(g_raw, u, fcw_full, ffn_conv_b)
    f = _mm("ffn_down", [(act, w_down_full)], "nn", F32)
    dh2, df, loss_row, dg_ffn_post = _loss_post(f, h1, target, ffn_post_g)

    dact = _mm("d_act", [(df, w_down_full)], "nt", F32)
    dw_down = _mm("dw_down", [(act, df)], "tn", F32)
    dg_conv, du = _ffn_bwd_act(dact, u, g_raw, fcw_full, ffn_conv_b)
    dg_raw, dfcw0, dfcw1, dfcw2, dfcb = _conv_bwd("ffn_conv_bwd", dg_conv, g_raw, fcw_full, FFN_CONV)
    dxn2 = _mm("d_xn2", [(dg_raw, w_gate), (du, w_lin)], "nt", F32)
    dw_gate = _mm("dw_gate", [(xn2, dg_raw)], "tn", F32)
    dw_lin = _mm("dw_lin", [(xn2, du)], "tn", F32)
    dh1, dmix, dg_ffn_pre, dg_mix_post = _mid_bwd(dxn2, h1, dh2, mix, ffn_pre_g, mix_post_g)

    dy_ssd = _mm("d_yssd", [(dmix, wo_ssd)], "nt", F32)
    dy_sb = _mm("d_ysb", [(dmix, wo_sb)], "nt", F32)
    dwo_ssd = _mm("dw_out_ssd", [(y_ssd, dmix)], "tn", F32)
    dwo_sb = _mm("dw_out_sb", [(y_sb, dmix)], "tn", F32)
    do, dg_sb = _norm_bwd("sb_norm_bwd", dy_sb, o, sb_norm_g)
    half = N_DEV // 2
    early_slabs = [
        jnp.concatenate([dwo_ssd, dwo_sb], axis=0).reshape(N_DEV, out_rows, D_MODEL),
        jnp.concatenate([dw_gate.reshape(D_MODEL, half, up_cols).transpose(1, 0, 2),
                         dw_lin.reshape(D_MODEL, half, up_cols).transpose(1, 0, 2)], axis=0),
        dw_down.reshape(N_DEV, down_rows, D_MODEL)]
    dq, dk, dv, (l_out, l_up, l_down) = _attn_bwd(q, k, v, keep_total, do, _Exchange(early_slabs, gather=False), early_slabs)
    dz, dxbc_act, ddtr, dg_ssd_norm, dd_skip, da_log, ddt_bias = _ssd_bwd(
        dy_ssd, ypre, z, xbc_act, dtr, states, dt_bias_p, a_log_p, d_p, ssd_norm_g)
    dconv = _ssd_conv_bwd_act(dxbc_act, xbc_raw, scw_full, ssd_conv_b)
    dxbc_raw, dscw0, dscw1, dscw2, dscw3, dscb = _conv_bwd("ssd_conv_bwd", dconv, xbc_raw, scw_full, SSD_CONV)
    segs = [(dz, w_z), (dxbc_raw, w_xbc), (ddtr, w_dt), (dq, w_q), (dk, w_k), (dv, w_v)]
    dxn1 = _mm("d_xn1", segs, "nt", F32)
    dw_segs = [_mm("dw_in_%d" % s, [(xn1, d)], "tn", BF16) for s, (d, _) in enumerate(segs)]
    dw_segs[2] = dw_segs[2][:, :HEADS]
    dw_in = jnp.concatenate(dw_segs, axis=1)
    dh0, dg_mix_pre = _first_bwd(dxn1, h0, dh1, mix_pre_g)
    grad_x = dh0[BLK:][None]

    slab_in = dw_in.reshape(D_MODEL, N_DEV, in_cols).transpose(1, 0, 2)

    small = [dg_mix_pre, dscb, ddt_bias, da_log, dd_skip, dg_ssd_norm, dg_sb, dg_mix_post, dg_ffn_pre, dfcb,
             dg_ffn_post, dh0[PAD:BLK].reshape(1, -1), dscw0, dscw1, dscw2, dscw3, dfcw0, dfcw1, dfcw2, loss_row]
    sizes = [a.shape[1] for a in small]
    total = sum(sizes)
    rows_packed = -(-total // (LANES * HALO)) * HALO
    packed = jnp.pad(jnp.concatenate(small, axis=1), ((0, 0), (0, rows_packed * LANES - total)))
    (l_in,), (gathered,) = _peer_exchange(
        "final_exchange", [([slab_in], False), ([packed.reshape(rows_packed, LANES)], True)])
    summed = _sum_slots("sum_small_grads", gathered).reshape(1, rows_packed * LANES)
    pieces, at = [], 0
    for s in sizes:
        pieces.append(summed[:, at:at + s])
        at += s
    (g_mix_pre, g_scb, g_dtb, g_alog, g_dskip, g_ssd_norm, g_sb, g_mix_post, g_ffn_pre, g_fcb, g_ffn_post,
     g_meta_flat, gs0, gs1, gs2, gs3, gf0, gf1, gf2, loss_all) = pieces
    loss = loss_all[0, 0]
    g_dtb, g_alog, g_dskip = g_dtb[:, :HEADS], g_alog[:, :HEADS], g_dskip[:, :HEADS]
    g_meta_full = g_meta_flat.reshape(N_META, D_MODEL)
    g_scw_full = jnp.concatenate([gs0, gs1, gs2, gs3], axis=0)
    g_fcw_full = jnp.concatenate([gf0, gf1, gf2], axis=0)
    meta_cols, scw_cols, fcw_cols = meta_tokens.shape[1], ssd_conv_w.shape[2], ffn_conv_w.shape[2]
    g_meta_mine = lax.dynamic_slice(g_meta_full, (0, me * meta_cols), (N_META, meta_cols))
    g_scw_mine = lax.dynamic_slice(g_scw_full, (0, me * scw_cols), (SSD_CONV, scw_cols))
    g_fcw_mine = lax.dynamic_slice(g_fcw_full, (0, me * fcw_cols), (FFN_CONV, fcw_cols))

    def lead(a):
        return a[None]

    upd = [
        _adamw("adamw_meta", meta_tokens, lead(g_meta_mine), m_meta_tokens, v_meta_tokens),
        _adamw("adamw_mix_pre_g", mix_pre_g, lead(g_mix_pre), m_mix_pre_g, v_mix_pre_g),
        [lead(a) for a in _adamw("adamw_w_in", w_in[0], l_in, m_w_in[0], v_w_in[0])],
        [lead(a) for a in _adamw("adamw_ssd_conv_w", ssd_conv_w[0], lead(g_scw_mine), m_ssd_conv_w[0], v_ssd_conv_w[0])],
        _adamw("adamw_ssd_conv_b", ssd_conv_b, lead(g_scb), m_ssd_conv_b, v_ssd_conv_b),
        _adamw("adamw_ssd_dt_bias", ssd_dt_bias, lead(g_dtb), m_ssd_dt_bias, v_ssd_dt_bias),
        _adamw("adamw_ssd_a_log", ssd_a_log, lead(g_alog), m_ssd_a_log, v_ssd_a_log),
        _adamw("adamw_ssd_d", ssd_d, lead(g_dskip), m_ssd_d, v_ssd_d),
        _adamw("adamw_ssd_norm_g", ssd_norm_g, lead(g_ssd_norm), m_ssd_norm_g, v_ssd_norm_g),
        _adamw("adamw_sb_norm_g", sb_norm_g, lead(g_sb), m_sb_norm_g, v_sb_norm_g),
        [lead(a) for a in _adamw("adamw_w_out", w_out[0], l_out, m_w_out[0], v_w_out[0])],
        _adamw("adamw_mix_post_g", mix_post_g, lead(g_mix_post), m_mix_post_g, v_mix_post_g),
        _adamw("adamw_ffn_pre_g", ffn_pre_g, lead(g_ffn_pre), m_ffn_pre_g, v_ffn_pre_g),
        [lead(a) for a in _adamw("adamw_w_up", w_up[0], l_up, m_w_up[0], v_w_up[0])],
        [lead(a) for a in _adamw("adamw_ffn_conv_w", ffn_conv_w[0], lead(g_fcw_mine), m_ffn_conv_w[0], v_ffn_conv_w[0])],
        _adamw("adamw_ffn_conv_b", ffn_conv_b, lead(g_fcb), m_ffn_conv_b, v_ffn_conv_b),
        [lead(a) for a in _adamw("adamw_w_down", w_down[0], l_down, m_w_down[0], v_w_down[0])],
        _adamw("adamw_ffn_post_g", ffn_post_g, lead(g_ffn_post), m_ffn_post_g, v_ffn_post_g),
    ]
    grads = [u_[0] for u_ in upd]
    deltas = [u_[1] for u_ in upd]
    new_m = [u_[2] for u_ in upd]
    new_v = [u_[3] for u_ in upd]
    return (loss, grad_x, *grads, *deltas, *new_m, *new_v)
```

```python
import math

import jax
import jax.numpy as jnp
from jax import lax
from jax.experimental import pallas as pl
from jax.experimental.pallas import tpu as pltpu

F32 = jnp.float32
BF16 = jnp.bfloat16
HI = lax.Precision.HIGHEST

D_MODEL = 1024
N_META = 16
BLK = 128
PAD = BLK - N_META
HEADS = 16
HEAD_DIM = 64
SSD_GROUPS = 2
SSD_STATE = 128
HEADS_PER_GROUP = HEADS // SSD_GROUPS
SSD_INNER = HEADS * HEAD_DIM
SSD_CONV = 4
XBC = SSD_INNER + 2 * SSD_GROUPS * SSD_STATE
OFF_B = SSD_INNER
OFF_C = SSD_INNER + SSD_GROUPS * SSD_STATE
D_FF = 2816
FFN_CONV = 3
EPS = 1e-6
SB_SCALE = 1.0 / math.sqrt(HEAD_DIM)
N_DEV = 8
LANES = 128
HALO = 8

ADAM_LR = 0.001
ADAM_B1 = 0.9
ADAM_B2 = 0.999
ADAM_EPS = 1e-08
ADAM_WD = 0.01
ADAM_STEP = 10

VMEM_FLOOR = 32 << 20
VMEM_CEIL = 60 << 20
MM_BUDGET = 20 << 20
ROW_BUDGET = 6 << 20

NN = (((1,), (0,)), ((), ()))
NT = (((1,), (1,)), ((), ()))
TN = (((0,), (0,)), ((), ()))


def _params(tile_bytes, sem=None):
    limit = int(min(max(2 * tile_bytes + (8 << 20), VMEM_FLOOR), VMEM_CEIL))
    return pltpu.CompilerParams(vmem_limit_bytes=limit, dimension_semantics=sem)


def _nbytes(shape, dtype):
    n = 1
    for s in shape:
        n *= s
    return n * jnp.dtype(dtype).itemsize


def _dot(a, b, dims=NN, precision=None):
    return lax.dot_general(a, b, dims, precision=precision, preferred_element_type=F32)


def _softplus(x):
    return jnp.maximum(x, 0.0) + jnp.log1p(jnp.exp(-jnp.abs(x)))


def _rms(x, g):
    r = lax.rsqrt(jnp.mean(x * x, axis=-1, keepdims=True) + EPS)
    return x * r * g


def _rms_bwd(dy, x, g):
    r = lax.rsqrt(jnp.mean(x * x, axis=-1, keepdims=True) + EPS)
    xh = x * r
    u = dy * g
    dx = r * (u - xh * jnp.mean(xh * u, axis=-1, keepdims=True))
    return dx, jnp.sum(dy * xh, axis=0, keepdims=True)


def _gelu(x):
    c = math.sqrt(2.0 / math.pi)
    return 0.5 * x * (1.0 + jnp.tanh(c * (x + 0.044715 * x * x * x)))


def _gelu_and_grad(x):
    c = math.sqrt(2.0 / math.pi)
    x2 = x * x
    t = jnp.tanh(c * (x + 0.044715 * x2 * x))
    half = 0.5 * (1.0 + t)
    return x * half, half + 0.5 * x * (1.0 - t * t) * c * (1.0 + 3.0 * 0.044715 * x2)


def _row_tile(rows, bytes_per_row):
    big = 384
    return big if rows % big == 0 and big * bytes_per_row <= ROW_BUDGET else BLK


def _mm(name, pairs, mode, out_dtype, ex=None, ex_arrays=()):
    a0, b0 = pairs[0]
    if mode == "tn":
        m, n = a0.shape[1], b0.shape[1]
    elif mode == "nt":
        m, n = a0.shape[0], b0.shape[0]
    else:
        m, n = a0.shape[0], b0.shape[1]
    dims = {"nn": NN, "nt": NT, "tn": TN}[mode]

    def tile_bytes(tm, tn):
        tot = tm * tn * jnp.dtype(out_dtype).itemsize
        for a, b in pairs:
            k = a.shape[0] if mode == "tn" else a.shape[1]
            tot += tm * k * a.dtype.itemsize + tn * k * b.dtype.itemsize
        return tot

    cands_m = [t for t in (1408, 1024, 512, 384, 256, 128) if m % t == 0] or [m]
    cands_n = [t for t in (1408, 1024, 768, 512, 256, 128) if n % t == 0] or [n]
    best = None
    for tm in cands_m:
        for tn in cands_n:
            if tile_bytes(tm, tn) <= MM_BUDGET and (best is None or tm * tn > best[0] * best[1]):
                best = (tm, tn)
    tm, tn = best if best is not None else (cands_m[-1], cands_n[-1])
    npairs = len(pairs)

    grid = (m // tm, n // tn)

    def body(*refs):
        if ex is None:
            ins, o_ref = refs[:2 * npairs], refs[2 * npairs]
        else:
            ins, ex_in, (o_ref,), ex_out, _, sems = _ride_along(ex, 2 * npairs, 1, refs)
            step = pl.program_id(0) * grid[1] + pl.program_id(1)

            @pl.when(step == 0)
            def _():
                ex.start(ex_in, ex_out, sems)

        acc = None
        for p in range(npairs):
            part = _dot(ins[2 * p][...], ins[2 * p + 1][...], dims)
            acc = part if acc is None else acc + part
        o_ref[...] = acc.astype(o_ref.dtype)
        if ex is not None:
            @pl.when(step == grid[0] * grid[1] - 1)
            def _():
                ex.wait(ex_in, ex_out, sems)

    in_specs, args = [], []
    for a, b in pairs:
        if mode == "tn":
            k = a.shape[0]
            in_specs += [pl.BlockSpec((k, tm), lambda i, j: (0, i)), pl.BlockSpec((k, tn), lambda i, j: (0, j))]
        elif mode == "nt":
            k = a.shape[1]
            in_specs += [pl.BlockSpec((tm, k), lambda i, j: (i, 0)), pl.BlockSpec((tn, k), lambda i, j: (j, 0))]
        else:
            k = a.shape[1]
            in_specs += [pl.BlockSpec((tm, k), lambda i, j: (i, 0)), pl.BlockSpec((k, tn), lambda i, j: (0, j))]
        args += [a, b]
    out_spec = pl.BlockSpec((tm, tn), lambda i, j: (i, j))
    out_shape = jax.ShapeDtypeStruct((m, n), out_dtype)
    if ex is None:
        return pl.pallas_call(
            body, name=name, grid=grid, in_specs=in_specs, out_specs=out_spec, out_shape=out_shape,
            compiler_params=_params(tile_bytes(tm, tn), ("parallel", "parallel")),
        )(*args)
    res = pl.pallas_call(
        body, name=name, grid=grid, in_specs=in_specs + ex.specs, out_specs=[out_spec] + ex.specs,
        out_shape=[out_shape] + ex.out_shape, scratch_shapes=ex.scratch,
        compiler_params=_params(tile_bytes(tm, tn), ("arbitrary", "arbitrary")),
    )(*args, *ex_arrays)
    return res[0], res[1:]


class _Window:
    def __init__(self, ref, cols):
        self.ref, self.cols = ref, cols

    def __getitem__(self, idx):
        return self.ref[:, self.cols] if idx is Ellipsis else self.ref[idx[0], self.cols]


def _rowcall(name, fn, rows=(), prevs=(), nexts=(), pars=(), out_rows=(), out_accs=(), chunk=None, late=()):
    rows, prevs, nexts, pars, late = list(rows), list(prevs), list(nexts), list(pars), list(late)
    n_rows = (rows + prevs + nexts)[0].shape[0]
    per_row = sum(a.shape[1] * a.dtype.itemsize for a in rows + prevs + nexts + late)
    per_row += sum(c * jnp.dtype(dt).itemsize for c, dt in out_rows) + sum(a.shape[1] * 4 for a in prevs + nexts)
    tm = _row_tile(n_rows, per_row)
    nt = n_rows // tm
    hb = tm // HALO
    if late:
        tm = BLK
        nt, hb = n_rows // tm, tm // HALO
    rows = rows + late
    nr, npv, nnx, npar, nor, noa = len(rows), len(prevs), len(nexts), len(pars), len(out_rows), len(out_accs)

    def body(*refs):
        i = pl.program_id(0)
        k = 0
        row_refs = refs[k:k + nr]; k += nr
        pc = refs[k:k + npv]; k += npv
        ph = refs[k:k + npv]; k += npv
        nc = refs[k:k + nnx]; k += nnx
        nh = refs[k:k + nnx]; k += nnx
        par_refs = refs[k:k + npar]; k += npar
        orow = refs[k:k + nor]; k += nor
        oacc = refs[k:k + noa]; k += noa
        pscr = refs[k:k + npv]; k += npv
        nscr = refs[k:k + nnx]
        for c_, h_, s_ in zip(pc, ph, pscr):
            s_[0:HALO, :] = h_[...] * (i > 0).astype(F32)
            s_[HALO:HALO + tm, :] = c_[...]
        for c_, h_, s_ in zip(nc, nh, nscr):
            s_[0:tm, :] = c_[...]
            s_[tm:tm + HALO, :] = h_[...] * (i < nt - 1).astype(F32)
        if noa:
            @pl.when(i == 0)
            def _():
                for r_ in oacc:
                    r_[...] = jnp.zeros_like(r_)
        width = out_rows[0][0]
        windows = [slice(None)] if chunk is None else [slice(c0, c0 + chunk) for c0 in range(0, width, chunk)]
        for cols in windows:
            prev_fns = [(lambda s, s_=s_, cols=cols: s_[pl.ds(HALO - s, tm), cols]) for s_ in pscr]
            next_fns = [(lambda s, s_=s_, cols=cols: s_[pl.ds(s, tm), cols]) for s_ in nscr]
            row_vals, acc_vals = fn(i, tm, [_Window(r_, cols) for r_ in row_refs], prev_fns, next_fns,
                                    [_Window(r_, cols) for r_ in par_refs])
            for r_, v in zip(orow, row_vals):
                r_[:, cols] = v.astype(r_.dtype)
            for r_, v in zip(oacc, acc_vals):
                r_[:, cols] += v

    def row_spec(a):
        return pl.BlockSpec((tm, a.shape[1]), lambda i: (i, 0))

    def whole(shape):
        return pl.BlockSpec(shape, lambda i: (0, 0))

    in_specs = [row_spec(a) for a in rows[:nr - len(late)]]
    in_specs += [pl.BlockSpec((tm, a.shape[1]), lambda i: (jnp.maximum(i - 1, 0), 0)) for a in late]
    in_specs += [row_spec(a) for a in prevs]
    in_specs += [pl.BlockSpec((HALO, a.shape[1]), lambda i: (jnp.maximum(i * hb - 1, 0), 0)) for a in prevs]
    in_specs += [row_spec(a) for a in nexts]
    in_specs += [pl.BlockSpec((HALO, a.shape[1]), lambda i: (jnp.minimum((i + 1) * hb, n_rows // HALO - 1), 0)) for a in nexts]
    in_specs += [whole(p.shape) for p in pars]
    out_specs = [pl.BlockSpec((tm, c), lambda i: (i, 0)) for c, _ in out_rows] + [whole(s) for s in out_accs]
    out_shape = [jax.ShapeDtypeStruct((n_rows, c), dt) for c, dt in out_rows]
    out_shape += [jax.ShapeDtypeStruct(s, F32) for s in out_accs]
    scratch = [pltpu.VMEM((tm + HALO, a.shape[1]), F32) for a in prevs + nexts]
    tile = tm * per_row
    res = pl.pallas_call(
        body, name=name, grid=(nt,), in_specs=in_specs, out_specs=out_specs, out_shape=out_shape,
        scratch_shapes=scratch, compiler_params=_params(2 * tile, ("arbitrary",)),
    )(*rows, *prevs, *prevs, *nexts, *nexts, *pars)
    return res


def _row_ids(i, tm):
    return i * tm + lax.broadcasted_iota(jnp.int32, (tm, 1), 0)


def _conv_taps(prev_fn, w_ref, b_ref, taps):
    acc = b_ref[...]
    for k in range(taps):
        acc = acc + prev_fn(taps - 1 - k) * w_ref[k:k + 1, :]
    return acc


def _rms_fwd(name, h, g):
    def fn(i, tm, rows, prevs, nexts, pars):
        return [_rms(rows[0][...], pars[0][...])], []
    return _rowcall(name, fn, rows=[h], pars=[g], out_rows=[(h.shape[1], BF16)])[0]


def _ssd_conv_fwd(xbc_raw, w, b):
    def fn(i, tm, rows, prevs, nexts, pars):
        c = _conv_taps(prevs[0], pars[0], pars[1], SSD_CONV)
        y = c * jax.nn.sigmoid(c)
        return [jnp.where(_row_ids(i, tm) >= PAD, y, 0.0)], []
    return _rowcall("ssd_conv_fwd", fn, prevs=[xbc_raw], pars=[w, b], out_rows=[(XBC, F32)], chunk=LANES)[0]


def _mix_post(mix, h0, g_post, g_pre):
    def fn(i, tm, rows, prevs, nexts, pars):
        h1 = rows[1][...] + _rms(rows[0][...], pars[0][...])
        return [h1, _rms(h1, pars[1][...])], []
    return _rowcall("mix_post", fn, rows=[mix, h0], pars=[g_post, g_pre],
                    out_rows=[(D_MODEL, F32), (D_MODEL, BF16)])


def _ffn_act(g_raw, u, w, b):
    def fn(i, tm, rows, prevs, nexts, pars):
        g = _conv_taps(prevs[0], pars[0], pars[1], FFN_CONV)
        return [_gelu(g) * rows[0][...]], []
    return _rowcall("ffn_act", fn, rows=[u], prevs=[g_raw], pars=[w, b], out_rows=[(D_FF, BF16)], chunk=LANES)[0]


def _loss_post(f, h1, target, g_post):
    def fn(i, tm, rows, prevs, nexts, pars):
        fv, g = rows[0][...], pars[0][...]
        h2 = rows[1][...] + _rms(fv, g)
        real = _row_ids(i, tm) >= BLK
        diff = jnp.where(real, h2 - rows[2][...], 0.0)
        loss = 0.5 * jnp.sum(jnp.mean(diff * diff, axis=-1, keepdims=True))
        dh2 = diff * (1.0 / D_MODEL)
        df, dg = _rms_bwd(dh2, fv, g)
        return [dh2, df], [jnp.zeros((1, LANES), F32) + loss, dg]
    return _rowcall("loss_post", fn, rows=[f, h1], late=[target], pars=[g_post],
                    out_rows=[(D_MODEL, F32), (D_MODEL, BF16)], out_accs=[(1, LANES), (1, D_MODEL)])


def _ffn_bwd_act(dact, u, g_raw, w, b):
    def fn(i, tm, rows, prevs, nexts, pars):
        g = _conv_taps(prevs[0], pars[0], pars[1], FFN_CONV)
        da = rows[0][...]
        gelu, grad = _gelu_and_grad(g)
        return [da * rows[1][...] * grad, da * gelu], []
    return _rowcall("ffn_bwd_act", fn, rows=[dact, u], prevs=[g_raw], pars=[w, b],
                    out_rows=[(D_FF, F32), (D_FF, BF16)], chunk=LANES)


def _conv_bwd(name, dy, x, w, taps):
    width = x.shape[1]

    def fn(i, tm, rows, prevs, nexts, pars):
        dy0 = nexts[0](0)
        dx = None
        for k in range(taps):
            term = nexts[0](taps - 1 - k) * pars[0][k:k + 1, :]
            dx = term if dx is None else dx + term
        dws = [jnp.sum(dy0 * prevs[0](taps - 1 - k), axis=0, keepdims=True) for k in range(taps)]
        return [dx], dws + [jnp.sum(dy0, axis=0, keepdims=True)]
    return _rowcall(name, fn, prevs=[x], nexts=[dy], pars=[w], out_rows=[(width, BF16)],
                    out_accs=[(1, width)] * (taps + 1), chunk=LANES)


def _mid_bwd(dxn2, h1, dh2, mix, g_pre, g_post):
    def fn(i, tm, rows, prevs, nexts, pars):
        d1, dg_pre = _rms_bwd(rows[0][...], rows[1][...], pars[0][...])
        dh1 = rows[2][...] + d1
        dmix, dg_post = _rms_bwd(dh1, rows[3][...], pars[1][...])
        return [dh1, dmix], [dg_pre, dg_post]
    return _rowcall("mid_bwd", fn, rows=[dxn2, h1, dh2, mix], pars=[g_pre, g_post],
                    out_rows=[(D_MODEL, F32), (D_MODEL, BF16)], out_accs=[(1, D_MODEL), (1, D_MODEL)])


def _norm_bwd(name, dy, x, g):
    def fn(i, tm, rows, prevs, nexts, pars):
        dx, dg = _rms_bwd(rows[0][...], rows[1][...], pars[0][...])
        return [dx], [dg]
    return _rowcall(name, fn, rows=[dy, x], pars=[g], out_rows=[(x.shape[1], BF16)], out_accs=[(1, x.shape[1])])


def _ssd_conv_bwd_act(dact, xbc_raw, w, b):
    def fn(i, tm, rows, prevs, nexts, pars):
        c = _conv_taps(prevs[0], pars[0], pars[1], SSD_CONV)
        s = jax.nn.sigmoid(c)
        dc = rows[0][...] * s * (1.0 + c * (1.0 - s))
        return [jnp.where(_row_ids(i, tm) >= PAD, dc, 0.0)], []
    return _rowcall("ssd_conv_bwd_act", fn, rows=[dact], prevs=[xbc_raw], pars=[w, b], out_rows=[(XBC, F32)],
                    chunk=LANES)[0]


def _first_bwd(dxn1, h0, dh1, g):
    def fn(i, tm, rows, prevs, nexts, pars):
        d0, dg = _rms_bwd(rows[0][...], rows[1][...], pars[0][...])
        return [rows[2][...] + d0], [dg]
    return _rowcall("first_bwd", fn, rows=[dxn1, h0, dh1], pars=[g], out_rows=[(D_MODEL, F32)],
                    out_accs=[(1, D_MODEL)])


def _ssd_chunk_terms(c, dtr_ref, bias_ref, alog_ref):
    ri = lax.broadcasted_iota(jnp.int32, (BLK, BLK), 0)
    ci = lax.broadcasted_iota(jnp.int32, (BLK, BLK), 1)
    causal = ri >= ci
    tril = causal.astype(F32)
    triu = (ri <= ci).astype(F32)
    rowmask = ((c * BLK + lax.broadcasted_iota(jnp.int32, (BLK, 1), 0)) >= PAD).astype(F32)
    dt = _softplus(dtr_ref[...] + bias_ref[...]) * rowmask
    a_neg = -jnp.exp(alog_ref[...])
    a = dt * a_neg
    cs = _dot(tril, a, NN, HI)
    cs_t = _dot(a, triu, TN, HI)
    return causal, triu, rowmask, dt, a_neg, cs, cs_t


def _decay_matrix(causal, cs_h, cs_t_h):
    return jnp.where(causal, jnp.exp(jnp.where(causal, cs_h - cs_t_h, 0.0)), 0.0)


def _head_spread(width):
    ri = lax.broadcasted_iota(jnp.int32, (LANES, HEADS * width), 0)
    ci = lax.broadcasted_iota(jnp.int32, (LANES, HEADS * width), 1)
    return ((ri * width <= ci) & (ci < (ri + 1) * width)).astype(BF16)


def _three_terms(x):
    hi = x.astype(BF16)
    rest = x - hi.astype(F32)
    mid = rest.astype(BF16)
    lo = (rest - mid.astype(F32)).astype(BF16)
    return jnp.concatenate([hi, mid, lo], axis=1)


def _spread(x, sel):
    return _dot(_three_terms(x), jnp.concatenate([sel, sel, sel], axis=0))


def _lane_sums(y, sel):
    return _dot(_three_terms(y), jnp.concatenate([sel, sel, sel], axis=1), NT)


def _ssd_spreads(dt, cs, d_ref, sel64, sel128):
    dt64 = _spread(dt, sel64)
    cs64 = _spread(cs, sel64)
    cs128 = _spread(cs, sel128)
    d64 = _spread(jnp.broadcast_to(d_ref[...], (HALO, LANES)), sel64)[0:1, :]
    cl64 = cs64[BLK - 1:BLK, :]
    return dt64, cs128, d64, jnp.exp(cs64), jnp.exp(cl64 - cs64), jnp.exp(cl64)


def _ssd_pair_terms(p, xbc_ref, dt64, cs128, cs_t, causal):
    lanes = _pair_lanes(p)
    xs = xbc_ref[:, lanes]
    decays = [_decay_matrix(causal, cs128[:, h * LANES:(h + 1) * LANES], cs_t[h:h + 1, :]) for h in (2 * p, 2 * p + 1)]
    return xs, xs * dt64[:, lanes], decays


def _ssd_fwd(xbc, dtr, z, dt_bias, a_log, d_skip, norm_g):
    n_rows = xbc.shape[0]
    nb = n_rows // BLK

    def body(xbc_ref, dtr_ref, z_ref, bias_ref, alog_ref, d_ref, g_ref, s64_ref, s128_ref,
             ypre_ref, yssd_ref, st_ref, state):
        c = pl.program_id(0)

        @pl.when(c == 0)
        def _():
            state[...] = jnp.zeros_like(state)

        st_ref[...] = state[...]
        causal, _, _, dt, _, cs, cs_t = _ssd_chunk_terms(c, dtr_ref, bias_ref, alog_ref)
        first = lax.broadcasted_iota(jnp.int32, (BLK, LANES), 1) < HEAD_DIM
        dt64, cs128, d64, from_start, to_end, chunk_decay = _ssd_spreads(dt, cs, d_ref, s64_ref[...], s128_ref[...])
        for g in range(SSD_GROUPS):
            b_b = xbc_ref[:, OFF_B + g * SSD_STATE:OFF_B + (g + 1) * SSD_STATE].astype(BF16)
            c_b = xbc_ref[:, OFF_C + g * SSD_STATE:OFF_C + (g + 1) * SSD_STATE].astype(BF16)
            cb = _dot(c_b, b_b, NT)
            for j in range(HEADS_PER_GROUP // 2):
                p = g * (HEADS_PER_GROUP // 2) + j
                lanes = _pair_lanes(p)
                xs, x_dt, decays = _ssd_pair_terms(p, xbc_ref, dt64, cs128, cs_t, causal)
                ms = [(cb * d).astype(BF16) for d in decays]
                s_p = state[:, lanes]
                y = _dot(jnp.concatenate(ms, axis=1), _head_stack(x_dt.astype(BF16), first))
                y = y + from_start[:, lanes] * _dot(c_b, s_p.astype(BF16))
                state[:, lanes] = chunk_decay[:, lanes] * s_p + _dot(b_b, (to_end[:, lanes] * x_dt).astype(BF16), TN)
                ypre_ref[:, lanes] = y + d64[:, lanes] * xs
        zz = z_ref[...]
        yg = ypre_ref[...] * (zz * jax.nn.sigmoid(zz))
        yssd_ref[...] = _rms(yg, g_ref[...]).astype(yssd_ref.dtype)

    blk = lambda w: pl.BlockSpec((BLK, w), lambda c: (c, 0))
    par = lambda a: pl.BlockSpec(a.shape, lambda c: (0, 0))
    sel64, sel128 = _head_spread(HEAD_DIM), _head_spread(LANES)
    return pl.pallas_call(
        body, name="ssd_fwd", grid=(nb,),
        in_specs=[blk(XBC), blk(LANES), blk(SSD_INNER), par(dt_bias), par(a_log), par(d_skip), par(norm_g),
                  par(sel64), par(sel128)],
        out_specs=[blk(SSD_INNER), blk(SSD_INNER), pl.BlockSpec((None, SSD_STATE, SSD_INNER), lambda c: (c, 0, 0))],
        out_shape=[jax.ShapeDtypeStruct((n_rows, SSD_INNER), F32), jax.ShapeDtypeStruct((n_rows, SSD_INNER), BF16),
                   jax.ShapeDtypeStruct((nb, SSD_STATE, SSD_INNER), F32)],
        scratch_shapes=[pltpu.VMEM((SSD_STATE, SSD_INNER), F32)],
        compiler_params=_params(8 << 20, ("arbitrary",)),
    )(xbc, dtr, z, dt_bias, a_log, d_skip, norm_g, sel64, sel128)


def _ssd_bwd(dy, ypre, z, xbc, dtr, states, dt_bias, a_log, d_skip, norm_g):
    n_rows = xbc.shape[0]
    nb = n_rows // BLK

    def body(dy_ref, ypre_ref, z_ref, xbc_ref, dtr_ref, st_ref, bias_ref, alog_ref, d_ref, g_ref, s64_ref, s128_ref,
             dz_ref, dxbc_ref, ddtr_ref, dgn_ref, dd_ref, dal_ref, ddtb_ref, dstate, dyp, red):
        step = pl.program_id(0)
        c = nb - 1 - step

        @pl.when(step == 0)
        def _():
            dstate[...] = jnp.zeros_like(dstate)
            for r_ in (dgn_ref, dd_ref, dal_ref, ddtb_ref):
                r_[...] = jnp.zeros_like(r_)

        yp, zz = ypre_ref[...], z_ref[...]
        sz = jax.nn.sigmoid(zz)
        silu = zz * sz
        dyg, dgn = _rms_bwd(dy_ref[...], yp * silu, g_ref[...])
        dgn_ref[...] += dgn
        dz_ref[...] = (dyg * yp * (sz * (1.0 + zz * (1.0 - sz)))).astype(dz_ref.dtype)
        dyp[...] = dyg * silu

        causal, triu, rowmask, dt, a_neg, cs, cs_t = _ssd_chunk_terms(c, dtr_ref, bias_ref, alog_ref)
        lane = lax.broadcasted_iota(jnp.int32, (1, LANES), 1)
        last_row = (lax.broadcasted_iota(jnp.int32, (BLK, 1), 0) == BLK - 1).astype(F32)
        first = lax.broadcasted_iota(jnp.int32, (BLK, LANES), 1) < HEAD_DIM
        sel64 = s64_ref[...]
        dt64, cs128, d64, from_start, to_end, chunk_decay = _ssd_spreads(dt, cs, d_ref, sel64, s128_ref[...])
        for g in range(SSD_GROUPS):
            b_b = xbc_ref[:, OFF_B + g * SSD_STATE:OFF_B + (g + 1) * SSD_STATE].astype(BF16)
            c_b = xbc_ref[:, OFF_C + g * SSD_STATE:OFF_C + (g + 1) * SSD_STATE].astype(BF16)
            cb = _dot(c_b, b_b, NT)
            b_twice = jnp.concatenate([b_b, b_b], axis=0)
            c_twice = jnp.concatenate([c_b, c_b], axis=0)
            db_g = jnp.zeros((BLK, SSD_STATE), F32)
            dc_g = jnp.zeros((BLK, SSD_STATE), F32)
            for j in range(HEADS_PER_GROUP // 2):
                p = g * (HEADS_PER_GROUP // 2) + j
                lanes = _pair_lanes(p)
                xs, x_dt, decays = _ssd_pair_terms(p, xbc_ref, dt64, cs128, cs_t, causal)
                ms = [(cb * d).astype(BF16) for d in decays]
                d_y = dyp[:, lanes]
                s_p, ds_p = st_ref[:, lanes], dstate[:, lanes]
                s_b, ds_b = s_p.astype(BF16), ds_p.astype(BF16)
                fs, te = from_start[:, lanes], to_end[:, lanes]
                x_b, dy_b = x_dt.astype(BF16), d_y.astype(BF16)
                x_st = _head_stack(x_b, first)
                fs_dy = (fs * d_y).astype(BF16)

                y_diag = _dot(jnp.concatenate(ms, axis=1), x_st)
                y_off = fs * _dot(c_b, s_b)
                end_part = te * _dot(b_b, ds_b)
                dx_diag = _dot(jnp.concatenate(ms, axis=0), _head_stack(dy_b, first), TN)
                d_x = dx_diag + end_part
                g2 = _dot(dy_b, x_st, NT)
                gl = [(g2[:, k * BLK:(k + 1) * BLK] * decays[k]).astype(BF16) for k in range(2)]
                dc_g = dc_g + _dot(jnp.concatenate(gl, axis=1), b_twice) + _dot(fs_dy, s_b, NT)
                db_g = db_g + _dot(jnp.concatenate(gl, axis=0), c_twice, TN) + _dot((te * x_dt).astype(BF16), ds_b, NT)
                red[0:BLK, lanes] = (dy_b.astype(F32) * y_diag - x_b.astype(F32) * dx_diag) + (d_y * y_off - x_dt * end_part)
                red[BLK:2 * BLK, lanes] = x_dt * end_part
                red[2 * BLK:3 * BLK, lanes] = d_x * xs
                red[3 * BLK:3 * BLK + HALO, lanes] = jnp.broadcast_to(jnp.sum(ds_p * s_p, axis=0, keepdims=True), (HALO, LANES))
                red[3 * BLK + HALO:3 * BLK + 2 * HALO, lanes] = jnp.broadcast_to(
                    jnp.sum(d_y * xs, axis=0, keepdims=True), (HALO, LANES))
                dxbc_ref[:, lanes] = d_x * dt64[:, lanes] + d64[:, lanes] * d_y
                dstate[:, lanes] = chunk_decay[:, lanes] * ds_p + _dot(c_b, fs_dy, TN)
            dxbc_ref[:, OFF_B + g * SSD_STATE:OFF_B + (g + 1) * SSD_STATE] = db_g
            dxbc_ref[:, OFF_C + g * SSD_STATE:OFF_C + (g + 1) * SSD_STATE] = dc_g
        sums = _lane_sums(red[...], sel64)
        at_last = (jnp.sum(sums[BLK:2 * BLK], axis=0, keepdims=True)
                   + jnp.exp(cs[BLK - 1:BLK, :]) * sums[3 * BLK:3 * BLK + 1])
        dcs = sums[0:BLK] + last_row * at_last
        ddt_x = sums[2 * BLK:3 * BLK]
        dd_row = sums[3 * BLK + HALO:3 * BLK + HALO + 1]
        da = _dot(triu, dcs, NN, HI)
        ddt = (da * a_neg + ddt_x) * rowmask
        ddtr = ddt * jax.nn.sigmoid(dtr_ref[...] + bias_ref[...]) * (lane < HEADS).astype(F32)
        ddtr_ref[...] = ddtr.astype(ddtr_ref.dtype)
        ddtb_ref[...] += jnp.sum(ddtr, axis=0, keepdims=True)
        dal_ref[...] += jnp.sum(da * dt, axis=0, keepdims=True) * a_neg
        dd_ref[...] += dd_row

    blk = lambda w: pl.BlockSpec((BLK, w), lambda s: (nb - 1 - s, 0))
    par = lambda a: pl.BlockSpec(a.shape, lambda s: (0, 0))
    acc = lambda w: pl.BlockSpec((1, w), lambda s: (0, 0))
    sel64, sel128 = _head_spread(HEAD_DIM), _head_spread(LANES)
    return pl.pallas_call(
        body, name="ssd_bwd", grid=(nb,),
        in_specs=[blk(SSD_INNER), blk(SSD_INNER), blk(SSD_INNER), blk(XBC), blk(LANES),
                  pl.BlockSpec((None, SSD_STATE, SSD_INNER), lambda s: (nb - 1 - s, 0, 0)),
                  par(dt_bias), par(a_log), par(d_skip), par(norm_g), par(sel64), par(sel128)],
        out_specs=[blk(SSD_INNER), blk(XBC), blk(LANES), acc(SSD_INNER), acc(LANES), acc(LANES), acc(LANES)],
        out_shape=[jax.ShapeDtypeStruct((n_rows, SSD_INNER), BF16), jax.ShapeDtypeStruct((n_rows, XBC), F32),
                   jax.ShapeDtypeStruct((n_rows, LANES), BF16), jax.ShapeDtypeStruct((1, SSD_INNER), F32),
                   jax.ShapeDtypeStruct((1, LANES), F32), jax.ShapeDtypeStruct((1, LANES), F32),
                   jax.ShapeDtypeStruct((1, LANES), F32)],
        scratch_shapes=[pltpu.VMEM((SSD_STATE, SSD_INNER), F32), pltpu.VMEM((BLK, SSD_INNER), F32),
                        pltpu.VMEM((3 * BLK + 2 * HALO, SSD_INNER), F32)],
        compiler_params=_params(12 << 20, ("arbitrary",)),
    )(dy, ypre, z, xbc, dtr, states, dt_bias, a_log, d_skip, norm_g, sel64, sel128)


HEAD_GROUP = 4


LOG2_E = 1.4426950408889634
SOFTPLUS_CLAMP = 80.0


def _sb_logits(zl, valid):
    z2 = zl * LOG2_E
    lost = jnp.maximum(jnp.log2(1.0 + jnp.exp2(jnp.minimum(z2, SOFTPLUS_CLAMP))), z2)
    log_beta = z2 - lost
    if valid is not None:
        lost = jnp.where(valid, lost, 0.0)
    return log_beta, lost


def _tri_and_ones(tri, sign=1.0):
    half = sign * jnp.concatenate([tri, jnp.ones((BLK, BLK), F32)], axis=1)
    return jnp.concatenate([half, half], axis=0).astype(BF16)


def _tile_mask(i, j, ri, ci):
    key = j * BLK + ci
    return (key < i * BLK + ri) & (key >= PAD)


def _pair_lanes(p):
    return slice(p * 2 * HEAD_DIM, (p + 1) * 2 * HEAD_DIM)


def _head_stack(x_pair, first):
    zero = jnp.zeros_like(x_pair)
    return jnp.concatenate([jnp.where(first, x_pair, zero), jnp.where(first, zero, x_pair)], axis=0)


def _block_rows(j):
    return pl.ds(j * BLK if isinstance(j, int) else pl.multiple_of(j * BLK, BLK), BLK)


def _sweep_pairs(i, tile):
    n_calls = (i + 2) // 2
    tile(0, True)

    def mid(t, carry):
        tile(t, False)
        return carry

    lax.fori_loop(1, n_calls - 1, mid, 0)

    @pl.when(n_calls >= 2)
    def _():
        tile(n_calls - 1, True)

    return n_calls


def _two_terms(x):
    hi = x.astype(BF16)
    return jnp.concatenate([hi, (x - hi.astype(F32)).astype(BF16)], axis=1)


def _ride_along(ex, n_in, n_out, refs):
    k = 0
    parts = []
    for cnt in (n_in, ex.n, n_out, ex.n):
        parts.append(refs[k:k + cnt])
        k += cnt
    n_sems = len(ex.scratch)
    return (*parts, refs[k:len(refs) - n_sems], refs[len(refs) - n_sems:])


def _attn_fwd(q, k, v, ex, ex_arrays):
    n_rows, width = q.shape
    nb = n_rows // BLK
    gw = HEAD_GROUP * HEAD_DIM
    groups = width // gw

    def body(*refs):
        (q_ref, k_ref, v_ref), ex_in, (o_ref, tot_ref), ex_out, (run_ref, z_ref, w_ref), sems = _ride_along(ex, 3, 2, refs)
        step = pl.program_id(0)

        @pl.when(step == 0)
        def _():
            ex.start(ex_in, ex_out, sems)

        ri = lax.broadcasted_iota(jnp.int32, (BLK, BLK), 0)
        ci = lax.broadcasted_iota(jnp.int32, (BLK, BLK), 1)
        sums = _tri_and_ones((ri > ci).astype(F32), -1.0)
        first = ci < HEAD_DIM
        heads, pairs = range(HEAD_GROUP), range(HEAD_GROUP // 2)

        def stacks(ref, blocks, p):
            return jnp.concatenate([_head_stack(ref[_block_rows(jnp.clip(j, 0, nb - 1)), _pair_lanes(p)], first)
                                    for j in blocks], axis=0)

        def q_block(i, carry):
            rows = _block_rows(i)
            o_ref[rows, :] = jnp.zeros((BLK, gw), F32)
            run_ref[...] = jnp.zeros_like(run_ref)
            w_ref[...] = jnp.zeros_like(w_ref)

            def blocks(t):
                return i - 2 * t, i - 2 * t - 1

            for p in pairs:
                z_ref[p] = _dot(q_ref[rows, _pair_lanes(p)], stacks(k_ref, blocks(0), p), NT)

            def tile(t, masked):
                q_i, o_i = q_ref[rows, :], o_ref[rows, :]
                outs = [_dot(w_ref[p], stacks(v_ref, blocks(jnp.maximum(t - 1, 0)), p)) for p in pairs]
                z_next = [_dot(q_i[:, _pair_lanes(p)], stacks(k_ref, blocks(t + 1), p), NT) for p in pairs]
                runs = [run_ref[h] for h in heads]
                lgs, res = [], []
                for s, j in enumerate(blocks(t)):
                    valid = _tile_mask(i, j, ri, ci) if masked else None
                    lgs.append([_sb_logits(z_ref[h // 2][:, (2 * s + h % 2) * BLK:(2 * s + h % 2 + 1) * BLK], valid)
                                for h in heads])
                    res.append(_dot(jnp.concatenate([_two_terms(lgs[s][h][1]) for h in heads], axis=0), sums))
                ws = []
                for s, j in enumerate(blocks(t)):
                    valid = _tile_mask(i, j, ri, ci) if masked else None
                    for h in heads:
                        part = res[s][h * BLK:(h + 1) * BLK]
                        w = jnp.exp2(lgs[s][h][0] + part[:, :BLK] + runs[h])
                        ws.append(jnp.where(valid, w, 0.0) if masked else w)
                        runs[h] = runs[h] + part[:, BLK:]
                for p in pairs:
                    w_ref[p] = jnp.concatenate([ws[s * HEAD_GROUP + 2 * p + e].astype(BF16) for s in range(2) for e in range(2)],
                                               axis=1)
                    z_ref[p] = z_next[p]
                o_ref[rows, :] = o_i + jnp.concatenate(outs, axis=1)
                for h in heads:
                    run_ref[h] = runs[h]

            n_calls = _sweep_pairs(i, tile)
            o_ref[rows, :] += jnp.concatenate([_dot(w_ref[p], stacks(v_ref, blocks(n_calls - 1), p)) for p in pairs], axis=1)
            for h in heads:
                tot_ref[rows, h:h + 1] = run_ref[h][:, 0:1]
            return carry

        lax.fori_loop(0, nb, q_block, 0)

        @pl.when(step == groups - 1)
        def _():
            ex.wait(ex_in, ex_out, sems)

    spec = pl.BlockSpec((n_rows, gw), lambda g: (0, g))
    tot_spec = pl.BlockSpec((None, n_rows, HEAD_GROUP), lambda g: (g, 0, 0))
    res = pl.pallas_call(
        body, name="attn_fwd", grid=(groups,), in_specs=[spec, spec, spec] + ex.specs,
        out_specs=[spec, tot_spec] + ex.specs,
        out_shape=[jax.ShapeDtypeStruct((n_rows, width), F32),
                   jax.ShapeDtypeStruct((groups, n_rows, HEAD_GROUP), F32)] + ex.out_shape,
        scratch_shapes=[pltpu.VMEM((HEAD_GROUP, BLK, BLK), F32), pltpu.VMEM((HEAD_GROUP // 2, BLK, 4 * BLK), F32),
                        pltpu.VMEM((HEAD_GROUP // 2, BLK, 4 * BLK), BF16)] + ex.scratch,
        compiler_params=_params(n_rows * (3 * gw * 2 + gw * 4 + LANES * 4), ("arbitrary",)),
    )(q, k, v, *ex_arrays)
    return res[0], res[1], res[2:]


def _attn_bwd(q, k, v, keep_total, do, ex, ex_arrays):
    n_rows, width = q.shape
    nb = n_rows // BLK
    gw = HEAD_GROUP * HEAD_DIM
    groups = width // gw

    def body(*refs):
        ((q_ref, k_ref, v_ref, tot_ref, do_ref), ex_in, (dq_ref, dk_ref, dv_ref), ex_out,
         (dq_acc, dk_acc, dv_acc, tot_b, run_ref, rung_ref, z_ref, dw_ref, dz_ref, wb_ref, qst_ref, dost_ref),
         sems) = _ride_along(ex, 5, 3, refs)
        step = pl.program_id(0)

        @pl.when(step == 0)
        def _():
            ex.start(ex_in, ex_out, sems)

        ri = lax.broadcasted_iota(jnp.int32, (BLK, BLK), 0)
        ci = lax.broadcasted_iota(jnp.int32, (BLK, BLK), 1)
        sums_keep = _tri_and_ones((ri <= ci).astype(F32), -1.0)
        sums_g = _tri_and_ones((ri < ci).astype(F32))
        first = ci < HEAD_DIM
        heads, pairs = range(HEAD_GROUP), range(HEAD_GROUP // 2)
        dk_acc[...] = jnp.zeros_like(dk_acc)
        dv_acc[...] = jnp.zeros_like(dv_acc)

        def clamp(j):
            return jnp.clip(j, 0, nb - 1)

        def stacks(ref, blocks, p):
            return jnp.concatenate([_head_stack(ref[_block_rows(clamp(j)), _pair_lanes(p)], first) for j in blocks], axis=0)

        def blocks(t):
            return 2 * t, 2 * t + 1

        def kept(ref, p, axis):
            tiles = [[ref[s * HEAD_GROUP + 2 * p + e] for e in range(2)] for s in range(2)]
            if axis == 1:
                return jnp.concatenate(tiles[0] + tiles[1], axis=1)
            return jnp.concatenate([jnp.concatenate(tiles[s], axis=0) for s in range(2)], axis=1)

        def owed_dq(t_prev):
            return [_dot(kept(dz_ref, p, 1), stacks(k_ref, blocks(t_prev), p)) for p in pairs]

        def owed_dk():
            return [_dot(kept(dz_ref, p, 0), qst_ref[p], TN) for p in pairs]

        def owed_dv():
            return [_dot(kept(wb_ref, p, 0), dost_ref[p], TN) for p in pairs]

        def settle(t_prev, parts):
            dq, dk, dv = parts
            dq_acc[...] += jnp.concatenate(dq, axis=1)
            for s, j in enumerate(blocks(t_prev)):
                cols = _block_rows(clamp(j))
                dk_acc[cols, :] += jnp.concatenate([d[s * BLK:(s + 1) * BLK] for d in dk], axis=1)
                dv_acc[cols, :] += jnp.concatenate([d[s * BLK:(s + 1) * BLK] for d in dv], axis=1)

        def q_block(i, carry):
            rows = _block_rows(i)
            dq_acc[...] = jnp.zeros_like(dq_acc)
            run_ref[...] = jnp.zeros_like(run_ref)
            rung_ref[...] = jnp.zeros_like(rung_ref)
            dz_ref[...] = jnp.zeros_like(dz_ref)
            wb_ref[...] = jnp.zeros_like(wb_ref)
            for h in heads:
                tot_b[h] = jnp.broadcast_to(tot_ref[rows, h:h + 1], (BLK, BLK))
            for p in pairs:
                qst_ref[p] = _head_stack(q_ref[rows, _pair_lanes(p)], first)
                dost_ref[p] = _head_stack(do_ref[rows, _pair_lanes(p)], first)
                z_ref[p] = _dot(q_ref[rows, _pair_lanes(p)], stacks(k_ref, blocks(0), p), NT)
                dw_ref[p] = _dot(do_ref[rows, _pair_lanes(p)], stacks(v_ref, blocks(0), p), NT)

            def tile(t, masked):
                q_i, do_i = q_ref[rows, :], do_ref[rows, :]
                t_prev = jnp.maximum(t - 1, 0)
                valids = [_tile_mask(i, j, ri, ci) if masked else None for j in blocks(t)]
                tile_of = lambda ref, s, h: ref[h // 2][:, (2 * s + h % 2) * BLK:(2 * s + h % 2 + 1) * BLK]
                runs = [run_ref[h] for h in heads]
                rungs = [rung_ref[h] for h in heads]
                lgs, keep, ws, gs, gsum, dzs = [], [], [], [], [], []
                for s in range(2):
                    lgs.append([_sb_logits(tile_of(z_ref, s, h), valids[s]) for h in heads])
                    keep.append(_dot(jnp.concatenate([_two_terms(lgs[s][h][1]) for h in heads], axis=0), sums_keep))
                    if s == 0:
                        part_dq = owed_dq(t_prev)
                    else:
                        part_dk = owed_dk()
                for s in range(2):
                    ws.append([])
                    gs.append([])
                    for h in heads:
                        part = keep[s][h * BLK:(h + 1) * BLK]
                        w = jnp.exp2(lgs[s][h][0] + (tot_b[h] - runs[h] - part[:, :BLK]))
                        ws[s].append(jnp.where(valids[s], w, 0.0) if masked else w)
                        runs[h] = runs[h] + part[:, BLK:]
                        gs[s].append(tile_of(dw_ref, s, h) * ws[s][h])
                    gsum.append(_dot(jnp.concatenate([_two_terms(gs[s][h]) for h in heads], axis=0), sums_g))
                    if s == 0:
                        part_dv = owed_dv()
                    else:
                        z_next = [_dot(q_i[:, _pair_lanes(p)], stacks(k_ref, blocks(t + 1), p), NT) for p in pairs]
                for s in range(2):
                    dzs.append([])
                    for h in heads:
                        part = gsum[s][h * BLK:(h + 1) * BLK]
                        beta = jnp.exp2(lgs[s][h][0])
                        dz = gs[s][h] * (1.0 - beta) - beta * (part[:, :BLK] + rungs[h])
                        dzs[s].append(jnp.where(valids[s], dz, 0.0) if masked else dz)
                        rungs[h] = rungs[h] + part[:, BLK:]
                    if s == 0:
                        dw_next = [_dot(do_i[:, _pair_lanes(p)], stacks(v_ref, blocks(t + 1), p), NT) for p in pairs]
                settle(t_prev, (part_dq, part_dk, part_dv))
                for s in range(2):
                    for h in heads:
                        dz_ref[s * HEAD_GROUP + h] = dzs[s][h].astype(BF16)
                        wb_ref[s * HEAD_GROUP + h] = ws[s][h].astype(BF16)
                for h in heads:
                    run_ref[h] = runs[h]
                    rung_ref[h] = rungs[h]
                for p in pairs:
                    z_ref[p] = z_next[p]
                    dw_ref[p] = dw_next[p]

            n_calls = _sweep_pairs(i, tile)
            settle(n_calls - 1, (owed_dq(n_calls - 1), owed_dk(), owed_dv()))
            dq_ref[rows, :] = (dq_acc[...] * SB_SCALE).astype(dq_ref.dtype)
            return carry

        lax.fori_loop(0, nb, q_block, 0)
        dk_ref[...] = dk_acc[...].astype(dk_ref.dtype)
        dv_ref[...] = dv_acc[...].astype(dv_ref.dtype)

        @pl.when(step == groups - 1)
        def _():
            ex.wait(ex_in, ex_out, sems)

    spec = pl.BlockSpec((n_rows, gw), lambda g: (0, g))
    tot_spec = pl.BlockSpec((None, n_rows, HEAD_GROUP), lambda g: (g, 0, 0))
    out = jax.ShapeDtypeStruct((n_rows, width), BF16)
    tile_f32 = pltpu.VMEM((HEAD_GROUP, BLK, BLK), F32)
    tile_bf16 = pltpu.VMEM((2 * HEAD_GROUP, BLK, BLK), BF16)
    pair_f32 = pltpu.VMEM((HEAD_GROUP // 2, BLK, 4 * BLK), F32)
    pair_stack = pltpu.VMEM((HEAD_GROUP // 2, 2 * BLK, BLK), BF16)
    res = pl.pallas_call(
        body, name="attn_bwd", grid=(groups,), in_specs=[spec, spec, spec, tot_spec, spec] + ex.specs,
        out_specs=[spec] * 3 + ex.specs, out_shape=[out] * 3 + ex.out_shape,
        scratch_shapes=[pltpu.VMEM((BLK, gw), F32), pltpu.VMEM((n_rows, gw), F32), pltpu.VMEM((n_rows, gw), F32),
                        tile_f32, tile_f32, tile_f32, pair_f32, pair_f32, tile_bf16, tile_bf16, pair_stack,
                        pair_stack] + ex.scratch,
        compiler_params=_params(n_rows * (7 * gw * 2 + LANES * 4 + gw * 4), ("arbitrary",)),
    )(q, k, v, keep_total, do, *ex_arrays)
    return res[0], res[1], res[2], res[3:]


class _Exchange:
    def __init__(self, arrays, gather):
        self.n = len(arrays)
        self.gather = gather
        self.out_shape = [jax.ShapeDtypeStruct(((N_DEV,) + a.shape) if gather else a.shape, a.dtype) for a in arrays]
        self.scratch = [pltpu.SemaphoreType.DMA((self.n, N_DEV - 1)), pltpu.SemaphoreType.DMA((self.n, N_DEV - 1)),
                        pltpu.SemaphoreType.DMA((self.n,))]
        self.specs = [pl.BlockSpec(memory_space=pl.ANY)] * self.n

    def _copies(self, ins, outs, sems, with_receives):
        send_sems, recv_sems, local_sems = sems
        x, y, c = lax.axis_index("x"), lax.axis_index("y"), lax.axis_index("c")
        me = 4 * x + 2 * y + c
        gather = self.gather
        local, sends, recvs = [], [], []
        for a in range(self.n):
            local.append(pltpu.make_async_copy(ins[a] if gather else ins[a].at[me], outs[a].at[me if gather else 0],
                                               local_sems.at[a]))
        for r in range(1, N_DEV):
            px = 1 - x if r & 4 else x
            py = 1 - y if r & 2 else y
            pc = 1 - c if r & 1 else c
            idx = 4 * px + 2 * py + pc
            for a in range(self.n):
                src = ins[a] if gather else ins[a].at[idx]
                pair = dict(send_sem=send_sems.at[a, r - 1], recv_sem=recv_sems.at[a, r - 1],
                            device_id=(px, py, pc), device_id_type=pl.DeviceIdType.MESH)
                sends.append(pltpu.make_async_remote_copy(src_ref=src, dst_ref=outs[a].at[me if gather else r], **pair))
                if with_receives:
                    recvs.append(pltpu.make_async_remote_copy(src_ref=src, dst_ref=outs[a].at[idx if gather else r], **pair))
        return local, sends, recvs

    def start(self, ins, outs, sems):
        local, sends, _ = self._copies(ins, outs, sems, with_receives=False)
        for cp in local + sends:
            cp.start()

    def wait(self, ins, outs, sems):
        local, sends, recvs = self._copies(ins, outs, sems, with_receives=True)
        for cp in recvs:
            cp.wait_recv()
        for cp in sends:
            cp.wait_send()
        for cp in local:
            cp.wait()


def _peer_exchange(name, parts):
    exs = [_Exchange(arrays, gather) for arrays, gather in parts]
    n = sum(ex.n for ex in exs)
    n_sems = len(exs[0].scratch)

    def body(*refs):
        at, views = 0, []
        for k, ex in enumerate(exs):
            views.append((ex, refs[at:at + ex.n], refs[n + at:n + at + ex.n],
                          refs[2 * n + k * n_sems:2 * n + (k + 1) * n_sems]))
            at += ex.n
        for ex, ins, outs, sems in views:
            ex.start(ins, outs, sems)
        for ex, ins, outs, sems in views:
            ex.wait(ins, outs, sems)

    res = pl.pallas_call(
        body, name=name, in_specs=[s for ex in exs for s in ex.specs], out_specs=[s for ex in exs for s in ex.specs],
        out_shape=[s for ex in exs for s in ex.out_shape], scratch_shapes=[s for ex in exs for s in ex.scratch],
    )(*[a for arrays, _ in parts for a in arrays])
    out, at = [], 0
    for ex in exs:
        out.append(res[at:at + ex.n])
        at += ex.n
    return out


def _gather_two_level(name, arrays):
    n = len(arrays)

    def body(*refs):
        ins, outs = refs[:n], refs[n:2 * n]
        send_sems, recv_sems, local_sems = refs[2 * n:]
        x, y, c = lax.axis_index("x"), lax.axis_index("y"), lax.axis_index("c")
        sibling = (x, y, 1 - c)
        chips = [(1 - x, y), (x, 1 - y), (1 - x, 1 - y)]

        def slot(a, dev):
            return outs[a].at[4 * dev[0] + 2 * dev[1] + dev[2]]

        def copy(a, k, block, to, src=None):
            return pltpu.make_async_remote_copy(
                src_ref=slot(a, block) if src is None else src, dst_ref=slot(a, block),
                send_sem=send_sems.at[a, k], recv_sem=recv_sems.at[a, k], device_id=to, device_id_type=pl.DeviceIdType.MESH)

        me = (x, y, c)
        mine = [pltpu.make_async_copy(ins[a], slot(a, me), local_sems.at[a]) for a in range(n)]
        first = [copy(a, 0, me, sibling, src=ins[a]) for a in range(n)]
        first += [copy(a, 1 + j, me, (*chip, c), src=ins[a]) for j, chip in enumerate(chips) for a in range(n)]
        for cp in mine + first:
            cp.start()
        passed = []
        for j, chip in enumerate(chips):
            for a in range(n):
                copy(a, 1 + j, (*chip, c), me).wait_recv()
            for a in range(n):
                cp = copy(a, 4 + j, (*chip, c), sibling)
                cp.start()
                passed.append(cp)
        for a in range(n):
            copy(a, 0, sibling, me).wait_recv()
            for j, chip in enumerate(chips):
                copy(a, 4 + j, (*chip, 1 - c), me).wait_recv()
        for cp in first + passed:
            cp.wait_send()
        for cp in mine:
            cp.wait()

    any_spec = pl.BlockSpec(memory_space=pl.ANY)
    return pl.pallas_call(
        body, name=name, in_specs=[any_spec] * n, out_specs=[any_spec] * n,
        out_shape=[jax.ShapeDtypeStruct((N_DEV,) + a.shape, a.dtype) for a in arrays],
        scratch_shapes=[pltpu.SemaphoreType.DMA((n, N_DEV - 1)), pltpu.SemaphoreType.DMA((n, N_DEV - 1)),
                        pltpu.SemaphoreType.DMA((n,))],
    )(*arrays)


def _sum_slots(name, x):
    def body(x_ref, o_ref):
        acc = x_ref[0]
        for s in range(1, N_DEV):
            acc = acc + x_ref[s]
        o_ref[...] = acc
    return pl.pallas_call(body, name=name, out_shape=jax.ShapeDtypeStruct(x.shape[1:], F32))(x)


def _adamw(name, w, slots, m, v):
    n_slots, rows, cols = slots.shape
    tr = next((t for t in (256, 176, 128) if rows % t == 0 and rows > t), rows)

    def body(w_ref, s_ref, m_ref, v_ref, g_ref, d_ref, nm_ref, nv_ref):
        g = s_ref[0].astype(F32)
        for s in range(1, n_slots):
            g = g + s_ref[s].astype(F32)
        nm = ADAM_B1 * m_ref[...] + (1.0 - ADAM_B1) * g
        nv = ADAM_B2 * v_ref[...] + (1.0 - ADAM_B2) * (g * g)
        m_hat = nm / (1.0 - ADAM_B1 ** ADAM_STEP)
        v_hat = nv / (1.0 - ADAM_B2 ** ADAM_STEP)
        g_ref[...] = g
        d_ref[...] = -ADAM_LR * (m_hat / (jnp.sqrt(v_hat) + ADAM_EPS) + ADAM_WD * w_ref[...])
        nm_ref[...] = nm
        nv_ref[...] = nv

    spec = pl.BlockSpec((tr, cols), lambda i: (i, 0))
    out = jax.ShapeDtypeStruct((rows, cols), F32)
    return pl.pallas_call(
        body, name=name, grid=(rows // tr,),
        in_specs=[spec, pl.BlockSpec((n_slots, tr, cols), lambda i: (0, i, 0)), spec, spec],
        out_specs=[spec] * 4, out_shape=[out] * 4,
        compiler_params=_params((n_slots + 7) * tr * cols * 4, ("parallel",)),
    )(w, slots, m, v)


def _pad_lanes(a):
    return jnp.pad(a, ((0, 0), (0, LANES - a.shape[1])))


def kernel(x, meta_tokens, mix_pre_g, w_in, ssd_conv_w, ssd_conv_b, ssd_dt_bias, ssd_a_log, ssd_d, ssd_norm_g, sb_norm_g, w_out, mix_post_g, ffn_pre_g, w_up, ffn_conv_w, ffn_conv_b, w_down, ffn_post_g, loss_target, m_meta_tokens, m_mix_pre_g, m_w_in, m_ssd_conv_w, m_ssd_conv_b, m_ssd_dt_bias, m_ssd_a_log, m_ssd_d, m_ssd_norm_g, m_sb_norm_g, m_w_out, m_mix_post_g, m_ffn_pre_g, m_w_up, m_ffn_conv_w, m_ffn_conv_b, m_w_down, m_ffn_post_g, v_meta_tokens, v_mix_pre_g, v_w_in, v_ssd_conv_w, v_ssd_conv_b, v_ssd_dt_bias, v_ssd_a_log, v_ssd_d, v_ssd_norm_g, v_sb_norm_g, v_w_out, v_mix_post_g, v_ffn_pre_g, v_w_up, v_ffn_conv_w, v_ffn_conv_b, v_w_down, v_ffn_post_g):
    seq = x.shape[1]
    me = 4 * lax.axis_index("x") + 2 * lax.axis_index("y") + lax.axis_index("c")
    in_cols = w_in.shape[2]
    up_cols = w_up.shape[2]
    out_rows = w_out.shape[1]
    down_rows = w_down.shape[1]

    g_in, g_meta, g_scw, g_fcw = _gather_two_level(
        "gather_w_in", [w_in[0].astype(BF16), meta_tokens, ssd_conv_w[0], ffn_conv_w[0]])
    late_weights = [w_out[0].astype(BF16), w_up[0].astype(BF16), w_down[0].astype(BF16)]
    w_in_full = g_in.transpose(1, 0, 2).reshape(D_MODEL, N_DEV * in_cols)
    off = [0, SSD_INNER, SSD_INNER + XBC, SSD_INNER + XBC + HEADS]
    w_z = w_in_full[:, off[0]:off[1]]
    w_xbc = w_in_full[:, off[1]:off[2]]
    w_dt = _pad_lanes(w_in_full[:, off[2]:off[3]])
    w_q = w_in_full[:, off[3]:off[3] + SSD_INNER]
    w_k = w_in_full[:, off[3] + SSD_INNER:off[3] + 2 * SSD_INNER]
    w_v = w_in_full[:, off[3] + 2 * SSD_INNER:off[3] + 3 * SSD_INNER]
    meta_full = g_meta.transpose(1, 0, 2).reshape(N_META, D_MODEL)
    scw_full = g_scw.transpose(1, 0, 2).reshape(SSD_CONV, XBC)
    fcw_full = g_fcw.transpose(1, 0, 2).reshape(FFN_CONV, D_FF)

    dt_bias_p, a_log_p, d_p = _pad_lanes(ssd_dt_bias), _pad_lanes(ssd_a_log), _pad_lanes(ssd_d)

    h0 = jnp.concatenate([jnp.zeros((PAD, D_MODEL), F32), meta_full, x[0]], axis=0)
    target = loss_target[0]
    xn1 = _rms_fwd("rms_pre_mix", h0, mix_pre_g)
    z = _mm("proj_z", [(xn1, w_z)], "nn", F32)
    xbc_raw = _mm("proj_xbc", [(xn1, w_xbc)], "nn", F32)
    dtr = _mm("proj_dt", [(xn1, w_dt)], "nn", F32)
    q = _mm("proj_q", [(xn1, w_q * SB_SCALE)], "nn", BF16)
    k = _mm("proj_k", [(xn1, w_k)], "nn", BF16)
    v = _mm("proj_v", [(xn1, w_v)], "nn", BF16)
    xbc_act = _ssd_conv_fwd(xbc_raw, scw_full, ssd_conv_b)
    ypre, y_ssd, states = _ssd_fwd(xbc_act, dtr, z, dt_bias_p, a_log_p, d_p, ssd_norm_g)
    o, keep_total, (g_out, g_up, g_down) = _attn_fwd(q, k, v, _Exchange(late_weights, gather=True), late_weights)
    w_out_full = g_out.reshape(N_DEV * out_rows, D_MODEL)
    wo_ssd, wo_sb = w_out_full[:SSD_INNER], w_out_full[SSD_INNER:]
    w_up_full = g_up.transpose(1, 0, 2).reshape(D_MODEL, N_DEV * up_cols)
    w_gate, w_lin = w_up_full[:, :D_FF], w_up_full[:, D_FF:]
    w_down_full = g_down.reshape(N_DEV * down_rows, D_MODEL)
    y_sb = _rms_fwd("rms_sb", o, sb_norm_g)
    mix = _mm("mix_out", [(y_ssd, wo_ssd), (y_sb, wo_sb)], "nn", F32)
    h1, xn2 = _mix_post(mix, h0, mix_post_g, ffn_pre_g)
    g_raw = _mm("ffn_gate", [(xn2, w_gate)], "nn", F32)
    u = _mm("ffn_lin", [(xn2, w_lin)], "nn", F32)
    act = _ffn_act(g_raw, u, fcw_full, ffn_conv_b)
    f = _mm("ffn_down", [(act, w_down_full)], "nn", F32)
    dh2, df, loss_row, dg_ffn_post = _loss_post(f, h1, target, ffn_post_g)

    dact = _mm("d_act", [(df, w_down_full)], "nt", F32)
    dw_down = _mm("dw_down", [(act, df)], "tn", F32)
    dg_conv, du = _ffn_bwd_act(dact, u, g_raw, fcw_full, ffn_conv_b)
    dg_raw, dfcw0, dfcw1, dfcw2, dfcb = _conv_bwd("ffn_conv_bwd", dg_conv, g_raw, fcw_full, FFN_CONV)
    dxn2 = _mm("d_xn2", [(dg_raw, w_gate), (du, w_lin)], "nt", F32)
    dw_gate = _mm("dw_gate", [(xn2, dg_raw)], "tn", F32)
    dw_lin = _mm("dw_lin", [(xn2, du)], "tn", F32)
    dh1, dmix, dg_ffn_pre, dg_mix_post = _mid_bwd(dxn2, h1, dh2, mix, ffn_pre_g, mix_post_g)

    dy_ssd = _mm("d_yssd", [(dmix, wo_ssd)], "nt", F32)
    dy_sb = _mm("d_ysb", [(dmix, wo_sb)], "nt", F32)
    dwo_ssd = _mm("dw_out_ssd", [(y_ssd, dmix)], "tn", F32)
    dwo_sb = _mm("dw_out_sb", [(y_sb, dmix)], "tn", F32)
    do, dg_sb = _norm_bwd("sb_norm_bwd", dy_sb, o, sb_norm_g)
    half = N_DEV // 2
    early_slabs = [
        jnp.concatenate([dwo_ssd, dwo_sb], axis=0).reshape(N_DEV, out_rows, D_MODEL),
        jnp.concatenate([dw_gate.reshape(D_MODEL, half, up_cols).transpose(1, 0, 2),
                         dw_lin.reshape(D_MODEL, half, up_cols).transpose(1, 0, 2)], axis=0),
        dw_down.reshape(N_DEV, down_rows, D_MODEL)]
    dq, dk, dv, (l_out, l_up, l_down) = _attn_bwd(q, k, v, keep_total, do, _Exchange(early_slabs, gather=False), early_slabs)
    dz, dxbc_act, ddtr, dg_ssd_norm, dd_skip, da_log, ddt_bias = _ssd_bwd(
        dy_ssd, ypre, z, xbc_act, dtr, states, dt_bias_p, a_log_p, d_p, ssd_norm_g)
    dconv = _ssd_conv_bwd_act(dxbc_act, xbc_raw, scw_full, ssd_conv_b)
    dxbc_raw, dscw0, dscw1, dscw2, dscw3, dscb = _conv_bwd("ssd_conv_bwd", dconv, xbc_raw, scw_full, SSD_CONV)
    segs = [(dz, w_z), (dxbc_raw, w_xbc), (ddtr, w_dt), (dq, w_q), (dk, w_k), (dv, w_v)]
    dw_segs = [_mm("dw_in_%d" % s, [(xn1, d)], "tn", BF16) for s, (d, _) in enumerate(segs)]
    dw_segs[2] = dw_segs[2][:, :HEADS]
    dw_in = jnp.concatenate(dw_segs, axis=1)
    slab_in = dw_in.reshape(D_MODEL, N_DEV, in_cols).transpose(1, 0, 2)
    dxn1, (l_in,) = _mm("d_xn1", segs, "nt", F32, _Exchange([slab_in], gather=False), [slab_in])
    dh0, dg_mix_pre = _first_bwd(dxn1, h0, dh1, mix_pre_g)
    grad_x = dh0[BLK:][None]

    small = [dg_mix_pre, dscb, ddt_bias, da_log, dd_skip, dg_ssd_norm, dg_sb, dg_mix_post, dg_ffn_pre, dfcb,
             dg_ffn_post, dh0[PAD:BLK].reshape(1, -1), dscw0, dscw1, dscw2, dscw3, dfcw0, dfcw1, dfcw2, loss_row]
    sizes = [a.shape[1] for a in small]
    total = sum(sizes)
    rows_packed = -(-total // (LANES * HALO)) * HALO
    packed = jnp.pad(jnp.concatenate(small, axis=1), ((0, 0), (0, rows_packed * LANES - total)))
    ((gathered,),) = _peer_exchange("gather_small_grads", [([packed.reshape(rows_packed, LANES)], True)])
    summed = _sum_slots("sum_small_grads", gathered).reshape(1, rows_packed * LANES)
    pieces, at = [], 0
    for s in sizes:
        pieces.append(summed[:, at:at + s])
        at += s
    (g_mix_pre, g_scb, g_dtb, g_alog, g_dskip, g_ssd_norm, g_sb, g_mix_post, g_ffn_pre, g_fcb, g_ffn_post,
     g_meta_flat, gs0, gs1, gs2, gs3, gf0, gf1, gf2, loss_all) = pieces
    loss = loss_all[0, 0]
    g_dtb, g_alog, g_dskip = g_dtb[:, :HEADS], g_alog[:, :HEADS], g_dskip[:, :HEADS]
    g_meta_full = g_meta_flat.reshape(N_META, D_MODEL)
    g_scw_full = jnp.concatenate([gs0, gs1, gs2, gs3], axis=0)
    g_fcw_full = jnp.concatenate([gf0, gf1, gf2], axis=0)
    meta_cols, scw_cols, fcw_cols = meta_tokens.shape[1], ssd_conv_w.shape[2], ffn_conv_w.shape[2]
    g_meta_mine = lax.dynamic_slice(g_meta_full, (0, me * meta_cols), (N_META, meta_cols))
    g_scw_mine = lax.dynamic_slice(g_scw_full, (0, me * scw_cols), (SSD_CONV, scw_cols))
    g_fcw_mine = lax.dynamic_slice(g_fcw_full, (0, me * fcw_cols), (FFN_CONV, fcw_cols))

    def lead(a):
        return a[None]

    upd = [
        _adamw("adamw_meta", meta_tokens, lead(g_meta_mine), m_meta_tokens, v_meta_tokens),
        _adamw("adamw_mix_pre_g", mix_pre_g, lead(g_mix_pre), m_mix_pre_g, v_mix_pre_g),
        [lead(a) for a in _adamw("adamw_w_in", w_in[0], l_in, m_w_in[0], v_w_in[0])],
        [lead(a) for a in _adamw("adamw_ssd_conv_w", ssd_conv_w[0], lead(g_scw_mine), m_ssd_conv_w[0], v_ssd_conv_w[0])],
        _adamw("adamw_ssd_conv_b", ssd_conv_b, lead(g_scb), m_ssd_conv_b, v_ssd_conv_b),
        _adamw("adamw_ssd_dt_bias", ssd_dt_bias, lead(g_dtb), m_ssd_dt_bias, v_ssd_dt_bias),
        _adamw("adamw_ssd_a_log", ssd_a_log, lead(g_alog), m_ssd_a_log, v_ssd_a_log),
        _adamw("adamw_ssd_d", ssd_d, lead(g_dskip), m_ssd_d, v_ssd_d),
        _adamw("adamw_ssd_norm_g", ssd_norm_g, lead(g_ssd_norm), m_ssd_norm_g, v_ssd_norm_g),
        _adamw("adamw_sb_norm_g", sb_norm_g, lead(g_sb), m_sb_norm_g, v_sb_norm_g),
        [lead(a) for a in _adamw("adamw_w_out", w_out[0], l_out, m_w_out[0], v_w_out[0])],
        _adamw("adamw_mix_post_g", mix_post_g, lead(g_mix_post), m_mix_post_g, v_mix_post_g),
        _adamw("adamw_ffn_pre_g", ffn_pre_g, lead(g_ffn_pre), m_ffn_pre_g, v_ffn_pre_g),
        [lead(a) for a in _adamw("adamw_w_up", w_up[0], l_up, m_w_up[0], v_w_up[0])],
        [lead(a) for a in _adamw("adamw_ffn_conv_w", ffn_conv_w[0], lead(g_fcw_mine), m_ffn_conv_w[0], v_ffn_conv_w[0])],
        _adamw("adamw_ffn_conv_b", ffn_conv_b, lead(g_fcb), m_ffn_conv_b, v_ffn_conv_b),
        [lead(a) for a in _adamw("adamw_w_down", w_down[0], l_down, m_w_down[0], v_w_down[0])],
        _adamw("adamw_ffn_post_g", ffn_post_g, lead(g_ffn_post), m_ffn_post_g, v_ffn_post_g),
    ]
    grads = [u_[0] for u_ in upd]
    deltas = [u_[1] for u_ in upd]
    new_m = [u_[2] for u_ in upd]
    new_v = [u_[3] for u_ in upd]
    return (loss, grad_x, *grads, *deltas, *new_m, *new_v)
```

```python
import math

import jax
import jax.numpy as jnp
from jax import lax
from jax.experimental import pallas as pl
from jax.experimental.pallas import tpu as pltpu

F32 = jnp.float32
BF16 = jnp.bfloat16
HI = lax.Precision.HIGHEST

D_MODEL = 1024
N_META = 16
BLK = 128
PAD = BLK - N_META
HEADS = 16
HEAD_DIM = 64
SSD_GROUPS = 2
SSD_STATE = 128
HEADS_PER_GROUP = HEADS // SSD_GROUPS
SSD_INNER = HEADS * HEAD_DIM
SSD_CONV = 4
XBC = SSD_INNER + 2 * SSD_GROUPS * SSD_STATE
OFF_B = SSD_INNER
OFF_C = SSD_INNER + SSD_GROUPS * SSD_STATE
D_FF = 2816
FFN_CONV = 3
EPS = 1e-6
SB_SCALE = 1.0 / math.sqrt(HEAD_DIM)
N_DEV = 8
LANES = 128
HALO = 8

ADAM_LR = 0.001
ADAM_B1 = 0.9
ADAM_B2 = 0.999
ADAM_EPS = 1e-08
ADAM_WD = 0.01
ADAM_STEP = 10

VMEM_FLOOR = 32 << 20
VMEM_CEIL = 60 << 20
MM_BUDGET = 20 << 20
ROW_BUDGET = 6 << 20

NN = (((1,), (0,)), ((), ()))
NT = (((1,), (1,)), ((), ()))
TN = (((0,), (0,)), ((), ()))


def _params(tile_bytes, sem=None):
    limit = int(min(max(2 * tile_bytes + (8 << 20), VMEM_FLOOR), VMEM_CEIL))
    return pltpu.CompilerParams(vmem_limit_bytes=limit, dimension_semantics=sem)


def _nbytes(shape, dtype):
    n = 1
    for s in shape:
        n *= s
    return n * jnp.dtype(dtype).itemsize


def _dot(a, b, dims=NN, precision=None):
    return lax.dot_general(a, b, dims, precision=precision, preferred_element_type=F32)


def _softplus(x):
    return jnp.maximum(x, 0.0) + jnp.log1p(jnp.exp(-jnp.abs(x)))


def _rms(x, g):
    r = lax.rsqrt(jnp.mean(x * x, axis=-1, keepdims=True) + EPS)
    return x * r * g


def _rms_bwd(dy, x, g):
    r = lax.rsqrt(jnp.mean(x * x, axis=-1, keepdims=True) + EPS)
    xh = x * r
    u = dy * g
    dx = r * (u - xh * jnp.mean(xh * u, axis=-1, keepdims=True))
    return dx, jnp.sum(dy * xh, axis=0, keepdims=True)


def _gelu(x):
    c = math.sqrt(2.0 / math.pi)
    return 0.5 * x * (1.0 + jnp.tanh(c * (x + 0.044715 * x * x * x)))


def _gelu_and_grad(x):
    c = math.sqrt(2.0 / math.pi)
    x2 = x * x
    t = jnp.tanh(c * (x + 0.044715 * x2 * x))
    half = 0.5 * (1.0 + t)
    return x * half, half + 0.5 * x * (1.0 - t * t) * c * (1.0 + 3.0 * 0.044715 * x2)


def _row_tile(rows, bytes_per_row):
    big = 384
    return big if rows % big == 0 and big * bytes_per_row <= ROW_BUDGET else BLK


def _mm(name, pairs, mode, out_dtype, ex=None, ex_arrays=()):
    a0, b0 = pairs[0]
    if mode == "tn":
        m, n = a0.shape[1], b0.shape[1]
    elif mode == "nt":
        m, n = a0.shape[0], b0.shape[0]
    else:
        m, n = a0.shape[0], b0.shape[1]
    dims = {"nn": NN, "nt": NT, "tn": TN}[mode]

    def tile_bytes(tm, tn):
        tot = tm * tn * jnp.dtype(out_dtype).itemsize
        for a, b in pairs:
            k = a.shape[0] if mode == "tn" else a.shape[1]
            tot += tm * k * a.dtype.itemsize + tn * k * b.dtype.itemsize
        return tot

    cands_m = [t for t in (1408, 1024, 512, 384, 256, 128) if m % t == 0] or [m]
    cands_n = [t for t in (1408, 1024, 768, 512, 256, 128) if n % t == 0] or [n]
    best = None
    for tm in cands_m:
        for tn in cands_n:
            if tile_bytes(tm, tn) <= MM_BUDGET and (best is None or tm * tn > best[0] * best[1]):
                best = (tm, tn)
    tm, tn = best if best is not None else (cands_m[-1], cands_n[-1])
    npairs = len(pairs)

    grid = (m // tm, n // tn)

    def body(*refs):
        if ex is None:
            ins, o_ref = refs[:2 * npairs], refs[2 * npairs]
        else:
            ins, ex_in, (o_ref,), ex_out, _, sems = _ride_along(ex, 2 * npairs, 1, refs)
            step = pl.program_id(0) * grid[1] + pl.program_id(1)

            @pl.when(step == 0)
            def _():
                ex.start(ex_in, ex_out, sems)

        acc = None
        for p in range(npairs):
            part = _dot(ins[2 * p][...], ins[2 * p + 1][...], dims)
            acc = part if acc is None else acc + part
        o_ref[...] = acc.astype(o_ref.dtype)
        if ex is not None:
            @pl.when(step == grid[0] * grid[1] - 1)
            def _():
                ex.wait(ex_in, ex_out, sems)

    in_specs, args = [], []
    for a, b in pairs:
        if mode == "tn":
            k = a.shape[0]
            in_specs += [pl.BlockSpec((k, tm), lambda i, j: (0, i)), pl.BlockSpec((k, tn), lambda i, j: (0, j))]
        elif mode == "nt":
            k = a.shape[1]
            in_specs += [pl.BlockSpec((tm, k), lambda i, j: (i, 0)), pl.BlockSpec((tn, k), lambda i, j: (j, 0))]
        else:
            k = a.shape[1]
            in_specs += [pl.BlockSpec((tm, k), lambda i, j: (i, 0)), pl.BlockSpec((k, tn), lambda i, j: (0, j))]
        args += [a, b]
    out_spec = pl.BlockSpec((tm, tn), lambda i, j: (i, j))
    out_shape = jax.ShapeDtypeStruct((m, n), out_dtype)
    if ex is None:
        return pl.pallas_call(
            body, name=name, grid=grid, in_specs=in_specs, out_specs=out_spec, out_shape=out_shape,
            compiler_params=_params(tile_bytes(tm, tn), ("parallel", "parallel")),
        )(*args)
    res = pl.pallas_call(
        body, name=name, grid=grid, in_specs=in_specs + ex.specs, out_specs=[out_spec] + ex.specs,
        out_shape=[out_shape] + ex.out_shape, scratch_shapes=ex.scratch,
        compiler_params=_params(tile_bytes(tm, tn), ("arbitrary", "arbitrary")),
    )(*args, *ex_arrays)
    return res[0], res[1:]


class _Window:
    def __init__(self, ref, cols):
        self.ref, self.cols = ref, cols

    def __getitem__(self, idx):
        return self.ref[:, self.cols] if idx is Ellipsis else self.ref[idx[0], self.cols]


def _rowcall(name, fn, rows=(), prevs=(), nexts=(), pars=(), out_rows=(), out_accs=(), chunk=None, late=()):
    rows, prevs, nexts, pars, late = list(rows), list(prevs), list(nexts), list(pars), list(late)
    n_rows = (rows + prevs + nexts)[0].shape[0]
    per_row = sum(a.shape[1] * a.dtype.itemsize for a in rows + prevs + nexts + late)
    per_row += sum(c * jnp.dtype(dt).itemsize for c, dt in out_rows) + sum(a.shape[1] * 4 for a in prevs + nexts)
    tm = _row_tile(n_rows, per_row)
    nt = n_rows // tm
    hb = tm // HALO
    if late:
        tm = BLK
        nt, hb = n_rows // tm, tm // HALO
    rows = rows + late
    nr, npv, nnx, npar, nor, noa = len(rows), len(prevs), len(nexts), len(pars), len(out_rows), len(out_accs)

    def body(*refs):
        i = pl.program_id(0)
        k = 0
        row_refs = refs[k:k + nr]; k += nr
        pc = refs[k:k + npv]; k += npv
        ph = refs[k:k + npv]; k += npv
        nc = refs[k:k + nnx]; k += nnx
        nh = refs[k:k + nnx]; k += nnx
        par_refs = refs[k:k + npar]; k += npar
        orow = refs[k:k + nor]; k += nor
        oacc = refs[k:k + noa]; k += noa
        pscr = refs[k:k + npv]; k += npv
        nscr = refs[k:k + nnx]
        for c_, h_, s_ in zip(pc, ph, pscr):
            s_[0:HALO, :] = h_[...] * (i > 0).astype(F32)
            s_[HALO:HALO + tm, :] = c_[...]
        for c_, h_, s_ in zip(nc, nh, nscr):
            s_[0:tm, :] = c_[...]
            s_[tm:tm + HALO, :] = h_[...] * (i < nt - 1).astype(F32)
        if noa:
            @pl.when(i == 0)
            def _():
                for r_ in oacc:
                    r_[...] = jnp.zeros_like(r_)
        width = out_rows[0][0]
        windows = [slice(None)] if chunk is None else [slice(c0, c0 + chunk) for c0 in range(0, width, chunk)]
        for cols in windows:
            prev_fns = [(lambda s, s_=s_, cols=cols: s_[pl.ds(HALO - s, tm), cols]) for s_ in pscr]
            next_fns = [(lambda s, s_=s_, cols=cols: s_[pl.ds(s, tm), cols]) for s_ in nscr]
            row_vals, acc_vals = fn(i, tm, [_Window(r_, cols) for r_ in row_refs], prev_fns, next_fns,
                                    [_Window(r_, cols) for r_ in par_refs])
            for r_, v in zip(orow, row_vals):
                r_[:, cols] = v.astype(r_.dtype)
            for r_, v in zip(oacc, acc_vals):
                r_[:, cols] += v

    def row_spec(a):
        return pl.BlockSpec((tm, a.shape[1]), lambda i: (i, 0))

    def whole(shape):
        return pl.BlockSpec(shape, lambda i: (0, 0))

    in_specs = [row_spec(a) for a in rows[:nr - len(late)]]
    in_specs += [pl.BlockSpec((tm, a.shape[1]), lambda i: (jnp.maximum(i - 1, 0), 0)) for a in late]
    in_specs += [row_spec(a) for a in prevs]
    in_specs += [pl.BlockSpec((HALO, a.shape[1]), lambda i: (jnp.maximum(i * hb - 1, 0), 0)) for a in prevs]
    in_specs += [row_spec(a) for a in nexts]
    in_specs += [pl.BlockSpec((HALO, a.shape[1]), lambda i: (jnp.minimum((i + 1) * hb, n_rows // HALO - 1), 0)) for a in nexts]
    in_specs += [whole(p.shape) for p in pars]
    out_specs = [pl.BlockSpec((tm, c), lambda i: (i, 0)) for c, _ in out_rows] + [whole(s) for s in out_accs]
    out_shape = [jax.ShapeDtypeStruct((n_rows, c), dt) for c, dt in out_rows]
    out_shape += [jax.ShapeDtypeStruct(s, F32) for s in out_accs]
    scratch = [pltpu.VMEM((tm + HALO, a.shape[1]), F32) for a in prevs + nexts]
    tile = tm * per_row
    res = pl.pallas_call(
        body, name=name, grid=(nt,), in_specs=in_specs, out_specs=out_specs, out_shape=out_shape,
        scratch_shapes=scratch, compiler_params=_params(2 * tile, ("arbitrary",)),
    )(*rows, *prevs, *prevs, *nexts, *nexts, *pars)
    return res


def _row_ids(i, tm):
    return i * tm + lax.broadcasted_iota(jnp.int32, (tm, 1), 0)


def _conv_taps(prev_fn, w_ref, b_ref, taps):
    acc = b_ref[...]
    for k in range(taps):
        acc = acc + prev_fn(taps - 1 - k) * w_ref[k:k + 1, :]
    return acc


def _rms_fwd(name, h, g):
    def fn(i, tm, rows, prevs, nexts, pars):
        return [_rms(rows[0][...], pars[0][...])], []
    return _rowcall(name, fn, rows=[h], pars=[g], out_rows=[(h.shape[1], BF16)])[0]


def _ssd_conv_fwd(xbc_raw, w, b):
    def fn(i, tm, rows, prevs, nexts, pars):
        c = _conv_taps(prevs[0], pars[0], pars[1], SSD_CONV)
        y = c * jax.nn.sigmoid(c)
        return [jnp.where(_row_ids(i, tm) >= PAD, y, 0.0)], []
    return _rowcall("ssd_conv_fwd", fn, prevs=[xbc_raw], pars=[w, b], out_rows=[(XBC, F32)], chunk=LANES)[0]


def _mix_post(mix, h0, g_post, g_pre):
    def fn(i, tm, rows, prevs, nexts, pars):
        h1 = rows[1][...] + _rms(rows[0][...], pars[0][...])
        return [h1, _rms(h1, pars[1][...])], []
    return _rowcall("mix_post", fn, rows=[mix, h0], pars=[g_post, g_pre],
                    out_rows=[(D_MODEL, F32), (D_MODEL, BF16)])


def _ffn_act(g_raw, u, w, b):
    def fn(i, tm, rows, prevs, nexts, pars):
        g = _conv_taps(prevs[0], pars[0], pars[1], FFN_CONV)
        return [_gelu(g) * rows[0][...]], []
    return _rowcall("ffn_act", fn, rows=[u], prevs=[g_raw], pars=[w, b], out_rows=[(D_FF, BF16)], chunk=LANES)[0]


def _loss_post(f, h1, target, g_post):
    def fn(i, tm, rows, prevs, nexts, pars):
        fv, g = rows[0][...], pars[0][...]
        h2 = rows[1][...] + _rms(fv, g)
        real = _row_ids(i, tm) >= BLK
        diff = jnp.where(real, h2 - rows[2][...], 0.0)
        loss = 0.5 * jnp.sum(jnp.mean(diff * diff, axis=-1, keepdims=True))
        dh2 = diff * (1.0 / D_MODEL)
        df, dg = _rms_bwd(dh2, fv, g)
        return [dh2, df], [jnp.zeros((1, LANES), F32) + loss, dg]
    return _rowcall("loss_post", fn, rows=[f, h1], late=[target], pars=[g_post],
                    out_rows=[(D_MODEL, F32), (D_MODEL, BF16)], out_accs=[(1, LANES), (1, D_MODEL)])


def _ffn_bwd_act(dact, u, g_raw, w, b):
    def fn(i, tm, rows, prevs, nexts, pars):
        g = _conv_taps(prevs[0], pars[0], pars[1], FFN_CONV)
        da = rows[0][...]
        gelu, grad = _gelu_and_grad(g)
        return [da * rows[1][...] * grad, da * gelu], []
    return _rowcall("ffn_bwd_act", fn, rows=[dact, u], prevs=[g_raw], pars=[w, b],
                    out_rows=[(D_FF, F32), (D_FF, BF16)], chunk=LANES)


def _conv_bwd(name, dy, x, w, taps):
    width = x.shape[1]

    def fn(i, tm, rows, prevs, nexts, pars):
        dy0 = nexts[0](0)
        dx = None
        for k in range(taps):
            term = nexts[0](taps - 1 - k) * pars[0][k:k + 1, :]
            dx = term if dx is None else dx + term
        dws = [jnp.sum(dy0 * prevs[0](taps - 1 - k), axis=0, keepdims=True) for k in range(taps)]
        return [dx], dws + [jnp.sum(dy0, axis=0, keepdims=True)]
    return _rowcall(name, fn, prevs=[x], nexts=[dy], pars=[w], out_rows=[(width, BF16)],
                    out_accs=[(1, width)] * (taps + 1), chunk=LANES)


def _mid_bwd(dxn2, h1, dh2, mix, g_pre, g_post):
    def fn(i, tm, rows, prevs, nexts, pars):
        d1, dg_pre = _rms_bwd(rows[0][...], rows[1][...], pars[0][...])
        dh1 = rows[2][...] + d1
        dmix, dg_post = _rms_bwd(dh1, rows[3][...], pars[1][...])
        return [dh1, dmix], [dg_pre, dg_post]
    return _rowcall("mid_bwd", fn, rows=[dxn2, h1, dh2, mix], pars=[g_pre, g_post],
                    out_rows=[(D_MODEL, F32), (D_MODEL, BF16)], out_accs=[(1, D_MODEL), (1, D_MODEL)])


def _norm_bwd(name, dy, x, g):
    def fn(i, tm, rows, prevs, nexts, pars):
        dx, dg = _rms_bwd(rows[0][...], rows[1][...], pars[0][...])
        return [dx], [dg]
    return _rowcall(name, fn, rows=[dy, x], pars=[g], out_rows=[(x.shape[1], BF16)], out_accs=[(1, x.shape[1])])


def _ssd_conv_bwd_act(dact, xbc_raw, w, b):
    def fn(i, tm, rows, prevs, nexts, pars):
        c = _conv_taps(prevs[0], pars[0], pars[1], SSD_CONV)
        s = jax.nn.sigmoid(c)
        dc = rows[0][...] * s * (1.0 + c * (1.0 - s))
        return [jnp.where(_row_ids(i, tm) >= PAD, dc, 0.0)], []
    return _rowcall("ssd_conv_bwd_act", fn, rows=[dact], prevs=[xbc_raw], pars=[w, b], out_rows=[(XBC, F32)],
                    chunk=LANES)[0]


def _first_bwd(dxn1, h0, dh1, g):
    def fn(i, tm, rows, prevs, nexts, pars):
        d0, dg = _rms_bwd(rows[0][...], rows[1][...], pars[0][...])
        return [rows[2][...] + d0], [dg]
    return _rowcall("first_bwd", fn, rows=[dxn1, h0, dh1], pars=[g], out_rows=[(D_MODEL, F32)],
                    out_accs=[(1, D_MODEL)])


def _ssd_chunk_terms(c, dtr_ref, bias_ref, alog_ref):
    ri = lax.broadcasted_iota(jnp.int32, (BLK, BLK), 0)
    ci = lax.broadcasted_iota(jnp.int32, (BLK, BLK), 1)
    causal = ri >= ci
    tril = causal.astype(F32)
    triu = (ri <= ci).astype(F32)
    rowmask = ((c * BLK + lax.broadcasted_iota(jnp.int32, (BLK, 1), 0)) >= PAD).astype(F32)
    dt = _softplus(dtr_ref[...] + bias_ref[...]) * rowmask
    a_neg = -jnp.exp(alog_ref[...])
    a = dt * a_neg
    cs = _dot(tril, a, NN, HI)
    cs_t = _dot(a, triu, TN, HI)
    return causal, triu, rowmask, dt, a_neg, cs, cs_t


def _decay_matrix(causal, cs_h, cs_t_h):
    return jnp.where(causal, jnp.exp(jnp.where(causal, cs_h - cs_t_h, 0.0)), 0.0)


def _head_spread(width):
    ri = lax.broadcasted_iota(jnp.int32, (LANES, HEADS * width), 0)
    ci = lax.broadcasted_iota(jnp.int32, (LANES, HEADS * width), 1)
    return ((ri * width <= ci) & (ci < (ri + 1) * width)).astype(BF16)


def _three_terms(x):
    hi = x.astype(BF16)
    rest = x - hi.astype(F32)
    mid = rest.astype(BF16)
    lo = (rest - mid.astype(F32)).astype(BF16)
    return jnp.concatenate([hi, mid, lo], axis=1)


def _spread(x, sel):
    return _dot(_three_terms(x), jnp.concatenate([sel, sel, sel], axis=0))


def _lane_sums(y, sel):
    return _dot(_three_terms(y), jnp.concatenate([sel, sel, sel], axis=1), NT)


def _ssd_spreads(dt, cs, d_ref, sel64, sel128):
    dt64 = _spread(dt, sel64)
    cs64 = _spread(cs, sel64)
    cs128 = _spread(cs, sel128)
    d64 = _spread(jnp.broadcast_to(d_ref[...], (HALO, LANES)), sel64)[0:1, :]
    cl64 = cs64[BLK - 1:BLK, :]
    return dt64, cs128, d64, jnp.exp(cs64), jnp.exp(cl64 - cs64), jnp.exp(cl64)


def _ssd_pair_terms(p, xbc_ref, dt64, cs128, cs_t, causal):
    lanes = _pair_lanes(p)
    xs = xbc_ref[:, lanes]
    decays = [_decay_matrix(causal, cs128[:, h * LANES:(h + 1) * LANES], cs_t[h:h + 1, :]) for h in (2 * p, 2 * p + 1)]
    return xs, xs * dt64[:, lanes], decays


def _ssd_fwd(xbc, dtr, z, dt_bias, a_log, d_skip, norm_g):
    n_rows = xbc.shape[0]
    nb = n_rows // BLK

    def body(xbc_ref, dtr_ref, z_ref, bias_ref, alog_ref, d_ref, g_ref, s64_ref, s128_ref,
             ypre_ref, yssd_ref, st_ref, state):
        c = pl.program_id(0)

        @pl.when(c == 0)
        def _():
            state[...] = jnp.zeros_like(state)

        st_ref[...] = state[...]
        causal, _, _, dt, _, cs, cs_t = _ssd_chunk_terms(c, dtr_ref, bias_ref, alog_ref)
        first = lax.broadcasted_iota(jnp.int32, (BLK, LANES), 1) < HEAD_DIM
        dt64, cs128, d64, from_start, to_end, chunk_decay = _ssd_spreads(dt, cs, d_ref, s64_ref[...], s128_ref[...])
        for g in range(SSD_GROUPS):
            b_b = xbc_ref[:, OFF_B + g * SSD_STATE:OFF_B + (g + 1) * SSD_STATE].astype(BF16)
            c_b = xbc_ref[:, OFF_C + g * SSD_STATE:OFF_C + (g + 1) * SSD_STATE].astype(BF16)
            cb = _dot(c_b, b_b, NT)
            for j in range(HEADS_PER_GROUP // 2):
                p = g * (HEADS_PER_GROUP // 2) + j
                lanes = _pair_lanes(p)
                xs, x_dt, decays = _ssd_pair_terms(p, xbc_ref, dt64, cs128, cs_t, causal)
                ms = [(cb * d).astype(BF16) for d in decays]
                s_p = state[:, lanes]
                y = _dot(jnp.concatenate(ms, axis=1), _head_stack(x_dt.astype(BF16), first))
                y = y + from_start[:, lanes] * _dot(c_b, s_p.astype(BF16))
                state[:, lanes] = chunk_decay[:, lanes] * s_p + _dot(b_b, (to_end[:, lanes] * x_dt).astype(BF16), TN)
                ypre_ref[:, lanes] = y + d64[:, lanes] * xs
        zz = z_ref[...]
        yg = ypre_ref[...] * (zz * jax.nn.sigmoid(zz))
        yssd_ref[...] = _rms(yg, g_ref[...]).astype(yssd_ref.dtype)

    blk = lambda w: pl.BlockSpec((BLK, w), lambda c: (c, 0))
    par = lambda a: pl.BlockSpec(a.shape, lambda c: (0, 0))
    sel64, sel128 = _head_spread(HEAD_DIM), _head_spread(LANES)
    return pl.pallas_call(
        body, name="ssd_fwd", grid=(nb,),
        in_specs=[blk(XBC), blk(LANES), blk(SSD_INNER), par(dt_bias), par(a_log), par(d_skip), par(norm_g),
                  par(sel64), par(sel128)],
        out_specs=[blk(SSD_INNER), blk(SSD_INNER), pl.BlockSpec((None, SSD_STATE, SSD_INNER), lambda c: (c, 0, 0))],
        out_shape=[jax.ShapeDtypeStruct((n_rows, SSD_INNER), F32), jax.ShapeDtypeStruct((n_rows, SSD_INNER), BF16),
                   jax.ShapeDtypeStruct((nb, SSD_STATE, SSD_INNER), F32)],
        scratch_shapes=[pltpu.VMEM((SSD_STATE, SSD_INNER), F32)],
        compiler_params=_params(8 << 20, ("arbitrary",)),
    )(xbc, dtr, z, dt_bias, a_log, d_skip, norm_g, sel64, sel128)


def _ssd_bwd(dy, ypre, z, xbc, dtr, states, dt_bias, a_log, d_skip, norm_g):
    n_rows = xbc.shape[0]
    nb = n_rows // BLK

    def body(dy_ref, ypre_ref, z_ref, xbc_ref, dtr_ref, st_ref, bias_ref, alog_ref, d_ref, g_ref, s64_ref, s128_ref,
             dz_ref, dxbc_ref, ddtr_ref, dgn_ref, dd_ref, dal_ref, ddtb_ref, dstate, dyp, red):
        step = pl.program_id(0)
        c = nb - 1 - step

        @pl.when(step == 0)
        def _():
            dstate[...] = jnp.zeros_like(dstate)
            for r_ in (dgn_ref, dd_ref, dal_ref, ddtb_ref):
                r_[...] = jnp.zeros_like(r_)

        yp, zz = ypre_ref[...], z_ref[...]
        sz = jax.nn.sigmoid(zz)
        silu = zz * sz
        dyg, dgn = _rms_bwd(dy_ref[...], yp * silu, g_ref[...])
        dgn_ref[...] += dgn
        dz_ref[...] = (dyg * yp * (sz * (1.0 + zz * (1.0 - sz)))).astype(dz_ref.dtype)
        dyp[...] = dyg * silu

        causal, triu, rowmask, dt, a_neg, cs, cs_t = _ssd_chunk_terms(c, dtr_ref, bias_ref, alog_ref)
        lane = lax.broadcasted_iota(jnp.int32, (1, LANES), 1)
        last_row = (lax.broadcasted_iota(jnp.int32, (BLK, 1), 0) == BLK - 1).astype(F32)
        first = lax.broadcasted_iota(jnp.int32, (BLK, LANES), 1) < HEAD_DIM
        sel64 = s64_ref[...]
        dt64, cs128, d64, from_start, to_end, chunk_decay = _ssd_spreads(dt, cs, d_ref, sel64, s128_ref[...])
        for g in range(SSD_GROUPS):
            b_b = xbc_ref[:, OFF_B + g * SSD_STATE:OFF_B + (g + 1) * SSD_STATE].astype(BF16)
            c_b = xbc_ref[:, OFF_C + g * SSD_STATE:OFF_C + (g + 1) * SSD_STATE].astype(BF16)
            cb = _dot(c_b, b_b, NT)
            b_twice = jnp.concatenate([b_b, b_b], axis=0)
            c_twice = jnp.concatenate([c_b, c_b], axis=0)
            db_g = jnp.zeros((BLK, SSD_STATE), F32)
            dc_g = jnp.zeros((BLK, SSD_STATE), F32)
            for j in range(HEADS_PER_GROUP // 2):
                p = g * (HEADS_PER_GROUP // 2) + j
                lanes = _pair_lanes(p)
                xs, x_dt, decays = _ssd_pair_terms(p, xbc_ref, dt64, cs128, cs_t, causal)
                ms = [(cb * d).astype(BF16) for d in decays]
                d_y = dyp[:, lanes]
                s_p, ds_p = st_ref[:, lanes], dstate[:, lanes]
                s_b, ds_b = s_p.astype(BF16), ds_p.astype(BF16)
                fs, te = from_start[:, lanes], to_end[:, lanes]
                x_b, dy_b = x_dt.astype(BF16), d_y.astype(BF16)
                x_st = _head_stack(x_b, first)
                fs_dy = (fs * d_y).astype(BF16)

                y_diag = _dot(jnp.concatenate(ms, axis=1), x_st)
                y_off = fs * _dot(c_b, s_b)
                end_part = te * _dot(b_b, ds_b)
                dx_diag = _dot(jnp.concatenate(ms, axis=0), _head_stack(dy_b, first), TN)
                d_x = dx_diag + end_part
                g2 = _dot(dy_b, x_st, NT)
                gl = [(g2[:, k * BLK:(k + 1) * BLK] * decays[k]).astype(BF16) for k in range(2)]
                dc_g = dc_g + _dot(jnp.concatenate(gl, axis=1), b_twice) + _dot(fs_dy, s_b, NT)
                db_g = db_g + _dot(jnp.concatenate(gl, axis=0), c_twice, TN) + _dot((te * x_dt).astype(BF16), ds_b, NT)
                red[0:BLK, lanes] = (dy_b.astype(F32) * y_diag - x_b.astype(F32) * dx_diag) + (d_y * y_off - x_dt * end_part)
                red[BLK:2 * BLK, lanes] = x_dt * end_part
                red[2 * BLK:3 * BLK, lanes] = d_x * xs
                red[3 * BLK:3 * BLK + HALO, lanes] = jnp.broadcast_to(jnp.sum(ds_p * s_p, axis=0, keepdims=True), (HALO, LANES))
                red[3 * BLK + HALO:3 * BLK + 2 * HALO, lanes] = jnp.broadcast_to(
                    jnp.sum(d_y * xs, axis=0, keepdims=True), (HALO, LANES))
                dxbc_ref[:, lanes] = d_x * dt64[:, lanes] + d64[:, lanes] * d_y
                dstate[:, lanes] = chunk_decay[:, lanes] * ds_p + _dot(c_b, fs_dy, TN)
            dxbc_ref[:, OFF_B + g * SSD_STATE:OFF_B + (g + 1) * SSD_STATE] = db_g
            dxbc_ref[:, OFF_C + g * SSD_STATE:OFF_C + (g + 1) * SSD_STATE] = dc_g
        sums = _lane_sums(red[...], sel64)
        at_last = (jnp.sum(sums[BLK:2 * BLK], axis=0, keepdims=True)
                   + jnp.exp(cs[BLK - 1:BLK, :]) * sums[3 * BLK:3 * BLK + 1])
        dcs = sums[0:BLK] + last_row * at_last
        ddt_x = sums[2 * BLK:3 * BLK]
        dd_row = sums[3 * BLK + HALO:3 * BLK + HALO + 1]
        da = _dot(triu, dcs, NN, HI)
        ddt = (da * a_neg + ddt_x) * rowmask
        ddtr = ddt * jax.nn.sigmoid(dtr_ref[...] + bias_ref[...]) * (lane < HEADS).astype(F32)
        ddtr_ref[...] = ddtr.astype(ddtr_ref.dtype)
        ddtb_ref[...] += jnp.sum(ddtr, axis=0, keepdims=True)
        dal_ref[...] += jnp.sum(da * dt, axis=0, keepdims=True) * a_neg
        dd_ref[...] += dd_row

    blk = lambda w: pl.BlockSpec((BLK, w), lambda s: (nb - 1 - s, 0))
    par = lambda a: pl.BlockSpec(a.shape, lambda s: (0, 0))
    acc = lambda w: pl.BlockSpec((1, w), lambda s: (0, 0))
    sel64, sel128 = _head_spread(HEAD_DIM), _head_spread(LANES)
    return pl.pallas_call(
        body, name="ssd_bwd", grid=(nb,),
        in_specs=[blk(SSD_INNER), blk(SSD_INNER), blk(SSD_INNER), blk(XBC), blk(LANES),
                  pl.BlockSpec((None, SSD_STATE, SSD_INNER), lambda s: (nb - 1 - s, 0, 0)),
                  par(dt_bias), par(a_log), par(d_skip), par(norm_g), par(sel64), par(sel128)],
        out_specs=[blk(SSD_INNER), blk(XBC), blk(LANES), acc(SSD_INNER), acc(LANES), acc(LANES), acc(LANES)],
        out_shape=[jax.ShapeDtypeStruct((n_rows, SSD_INNER), BF16), jax.ShapeDtypeStruct((n_rows, XBC), F32),
                   jax.ShapeDtypeStruct((n_rows, LANES), BF16), jax.ShapeDtypeStruct((1, SSD_INNER), F32),
                   jax.ShapeDtypeStruct((1, LANES), F32), jax.ShapeDtypeStruct((1, LANES), F32),
                   jax.ShapeDtypeStruct((1, LANES), F32)],
        scratch_shapes=[pltpu.VMEM((SSD_STATE, SSD_INNER), F32), pltpu.VMEM((BLK, SSD_INNER), F32),
                        pltpu.VMEM((3 * BLK + 2 * HALO, SSD_INNER), F32)],
        compiler_params=_params(12 << 20, ("arbitrary",)),
    )(dy, ypre, z, xbc, dtr, states, dt_bias, a_log, d_skip, norm_g, sel64, sel128)


HEAD_GROUP = 4


LOG2_E = 1.4426950408889634
SOFTPLUS_CLAMP = 80.0


def _sb_logits(zl, valid):
    z2 = zl * LOG2_E
    lost = jnp.maximum(jnp.log2(1.0 + jnp.exp2(jnp.minimum(z2, SOFTPLUS_CLAMP))), z2)
    log_beta = z2 - lost
    if valid is not None:
        lost = jnp.where(valid, lost, 0.0)
    return log_beta, lost


def _tri_and_ones(tri, sign=1.0):
    half = sign * jnp.concatenate([tri, jnp.ones((BLK, BLK), F32)], axis=1)
    return jnp.concatenate([half, half], axis=0).astype(BF16)


def _tile_mask(i, j, ri, ci):
    key = j * BLK + ci
    return (key < i * BLK + ri) & (key >= PAD)


def _pair_lanes(p):
    return slice(p * 2 * HEAD_DIM, (p + 1) * 2 * HEAD_DIM)


def _head_stack(x_pair, first):
    zero = jnp.zeros_like(x_pair)
    return jnp.concatenate([jnp.where(first, x_pair, zero), jnp.where(first, zero, x_pair)], axis=0)


def _block_rows(j):
    return pl.ds(j * BLK if isinstance(j, int) else pl.multiple_of(j * BLK, BLK), BLK)


def _two_terms(x):
    hi = x.astype(BF16)
    return jnp.concatenate([hi, (x - hi.astype(F32)).astype(BF16)], axis=1)


def _ride_along(ex, n_in, n_out, refs):
    k = 0
    parts = []
    for cnt in (n_in, ex.n, n_out, ex.n):
        parts.append(refs[k:k + cnt])
        k += cnt
    n_sems = len(ex.scratch)
    return (*parts, refs[k:len(refs) - n_sems], refs[len(refs) - n_sems:])


def _attn_fwd(q, k, v, ex, ex_arrays):
    n_rows, width = q.shape
    nb = n_rows // BLK
    n_heads = HEAD_GROUP
    gw = n_heads * HEAD_DIM
    groups = width // gw

    def body(*refs):
        (q_ref, k_ref, v_ref), ex_in, (o_ref, tot_ref), ex_out, (run_ref, z_ref, w_ref), sems = _ride_along(ex, 3, 2, refs)
        step = pl.program_id(0)

        @pl.when(step == 0)
        def _():
            ex.start(ex_in, ex_out, sems)

        ri = lax.broadcasted_iota(jnp.int32, (BLK, BLK), 0)
        ci = lax.broadcasted_iota(jnp.int32, (BLK, BLK), 1)
        sums = _tri_and_ones((ri > ci).astype(F32), -1.0)
        first = ci < HEAD_DIM
        heads, pairs = range(n_heads), range(n_heads // 2)

        def stacks(ref, blocks, p):
            return jnp.concatenate([_head_stack(ref[_block_rows(jnp.clip(j, 0, nb - 1)), _pair_lanes(p)], first)
                                    for j in blocks], axis=0)

        def q_block(i, carry):
            rows = _block_rows(i)
            o_ref[rows, :] = jnp.zeros((BLK, gw), F32)
            run_ref[...] = jnp.zeros_like(run_ref)
            w_ref[...] = jnp.zeros_like(w_ref)

            odd = (i + 1) % 2

            def blocks(t):
                return i + odd - 2 * t, i + odd - 2 * t - 1

            for p in pairs:
                z_ref[p] = _dot(q_ref[rows, _pair_lanes(p)], stacks(k_ref, blocks(0), p), NT)

            def tile(t, masked, skip=None):
                q_i, o_i = q_ref[rows, :], o_ref[rows, :]
                outs = [_dot(w_ref[p], stacks(v_ref, blocks(jnp.maximum(t - 1, 0)), p)) for p in pairs]
                z_next = [_dot(q_i[:, _pair_lanes(p)], stacks(k_ref, blocks(t + 1), p), NT) for p in pairs]
                runs = [run_ref[h] for h in heads]
                lgs, res = {}, {}
                for s, j in enumerate(blocks(t)):
                    if s == skip:
                        continue
                    valid = _tile_mask(i, j, ri, ci) if masked else None
                    lgs[s] = [_sb_logits(z_ref[h // 2][:, (2 * s + h % 2) * BLK:(2 * s + h % 2 + 1) * BLK], valid)
                              for h in heads]
                    res[s] = _dot(jnp.concatenate([_two_terms(lgs[s][h][1]) for h in heads], axis=0), sums)
                ws = []
                for s, j in enumerate(blocks(t)):
                    valid = _tile_mask(i, j, ri, ci) if masked else None
                    for h in heads:
                        if s == skip:
                            ws.append(jnp.zeros((BLK, BLK), BF16))
                            continue
                        part = res[s][h * BLK:(h + 1) * BLK]
                        w = jnp.exp2(lgs[s][h][0] + part[:, :BLK] + runs[h])
                        ws.append((jnp.where(valid, w, 0.0) if masked else w).astype(BF16))
                        runs[h] = runs[h] + part[:, BLK:]
                for p in pairs:
                    w_ref[p] = jnp.concatenate([ws[s * n_heads + 2 * p + e] for s in range(2) for e in range(2)], axis=1)
                    z_ref[p] = z_next[p]
                o_ref[rows, :] = o_i + jnp.concatenate(outs, axis=1)
                for h in heads:
                    run_ref[h] = runs[h]

            n_calls = (i + 2) // 2

            @pl.when(odd == 1)
            def _():
                tile(0, True, skip=0)

            @pl.when(odd == 0)
            def _():
                tile(0, True)

            def mid(t, carry):
                tile(t, False)
                return carry

            lax.fori_loop(1, n_calls - 1, mid, 0)

            @pl.when(n_calls >= 2)
            def _():
                tile(n_calls - 1, True)

            o_ref[rows, :] += jnp.concatenate([_dot(w_ref[p], stacks(v_ref, blocks(n_calls - 1), p)) for p in pairs], axis=1)
            for h in heads:
                tot_ref[h // HEAD_GROUP, rows, h % HEAD_GROUP:h % HEAD_GROUP + 1] = run_ref[h][:, 0:1]
            return carry

        lax.fori_loop(0, nb, q_block, 0)

        @pl.when(step == groups - 1)
        def _():
            ex.wait(ex_in, ex_out, sems)

    spec = pl.BlockSpec((n_rows, gw), lambda g: (0, g))
    tot_spec = pl.BlockSpec((n_heads // HEAD_GROUP, n_rows, HEAD_GROUP), lambda g: (g, 0, 0))
    res = pl.pallas_call(
        body, name="attn_fwd", grid=(groups,), in_specs=[spec, spec, spec] + ex.specs,
        out_specs=[spec, tot_spec] + ex.specs,
        out_shape=[jax.ShapeDtypeStruct((n_rows, width), F32),
                   jax.ShapeDtypeStruct((width // (HEAD_GROUP * HEAD_DIM), n_rows, HEAD_GROUP), F32)] + ex.out_shape,
        scratch_shapes=[pltpu.VMEM((n_heads, BLK, BLK), F32), pltpu.VMEM((n_heads // 2, BLK, 4 * BLK), F32),
                        pltpu.VMEM((n_heads // 2, BLK, 4 * BLK), BF16)] + ex.scratch,
        compiler_params=_params(n_rows * (3 * gw * 2 + gw * 4 + LANES * 4), ("arbitrary",)),
    )(q, k, v, *ex_arrays)
    return res[0], res[1], res[2:]


def _attn_bwd(q, k, v, keep_total, do, ex, ex_arrays):
    n_rows, width = q.shape
    nb = n_rows // BLK
    gw = HEAD_GROUP * HEAD_DIM
    groups = width // gw

    def body(*refs):
        ((q_ref, k_ref, v_ref, tot_ref, do_ref), ex_in, (dq_ref, dk_ref, dv_ref), ex_out,
         (dq_acc, dk_acc, dv_acc, tot_b, run_ref, rung_ref, z_ref, dw_ref, dz_ref, wb_ref, qst_ref, dost_ref),
         sems) = _ride_along(ex, 5, 3, refs)
        step = pl.program_id(0)

        @pl.when(step == 0)
        def _():
            ex.start(ex_in, ex_out, sems)

        ri = lax.broadcasted_iota(jnp.int32, (BLK, BLK), 0)
        ci = lax.broadcasted_iota(jnp.int32, (BLK, BLK), 1)
        sums_keep = _tri_and_ones((ri <= ci).astype(F32), -1.0)
        sums_g = _tri_and_ones((ri < ci).astype(F32))
        first = ci < HEAD_DIM
        heads, pairs = range(HEAD_GROUP), range(HEAD_GROUP // 2)
        dk_acc[...] = jnp.zeros_like(dk_acc)
        dv_acc[...] = jnp.zeros_like(dv_acc)

        def clamp(j):
            return jnp.clip(j, 0, nb - 1)

        def stacks(ref, blocks, p):
            return jnp.concatenate([_head_stack(ref[_block_rows(clamp(j)), _pair_lanes(p)], first) for j in blocks], axis=0)

        def blocks(t):
            return 2 * t, 2 * t + 1

        def kept(ref, p, axis):
            tiles = [[ref[s * HEAD_GROUP + 2 * p + e] for e in range(2)] for s in range(2)]
            if axis == 1:
                return jnp.concatenate(tiles[0] + tiles[1], axis=1)
            return jnp.concatenate([jnp.concatenate(tiles[s], axis=0) for s in range(2)], axis=1)

        def owed_dq(t_prev):
            return [_dot(kept(dz_ref, p, 1), stacks(k_ref, blocks(t_prev), p)) for p in pairs]

        def owed_dk():
            return [_dot(kept(dz_ref, p, 0), qst_ref[p], TN) for p in pairs]

        def owed_dv():
            return [_dot(kept(wb_ref, p, 0), dost_ref[p], TN) for p in pairs]

        def settle(t_prev, parts):
            dq, dk, dv = parts
            dq_acc[...] += jnp.concatenate(dq, axis=1)
            for s, j in enumerate(blocks(t_prev)):
                cols = _block_rows(clamp(j))
                dk_acc[cols, :] += jnp.concatenate([d[s * BLK:(s + 1) * BLK] for d in dk], axis=1)
                dv_acc[cols, :] += jnp.concatenate([d[s * BLK:(s + 1) * BLK] for d in dv], axis=1)

        def q_block(i, carry):
            rows = _block_rows(i)
            dq_acc[...] = jnp.zeros_like(dq_acc)
            run_ref[...] = jnp.zeros_like(run_ref)
            rung_ref[...] = jnp.zeros_like(rung_ref)
            dz_ref[...] = jnp.zeros_like(dz_ref)
            wb_ref[...] = jnp.zeros_like(wb_ref)
            for h in heads:
                tot_b[h] = jnp.broadcast_to(tot_ref[rows, h:h + 1], (BLK, BLK))
            for p in pairs:
                qst_ref[p] = _head_stack(q_ref[rows, _pair_lanes(p)], first)
                dost_ref[p] = _head_stack(do_ref[rows, _pair_lanes(p)], first)
                z_ref[p] = _dot(q_ref[rows, _pair_lanes(p)], stacks(k_ref, blocks(0), p), NT)
                dw_ref[p] = _dot(do_ref[rows, _pair_lanes(p)], stacks(v_ref, blocks(0), p), NT)

            def tile(t, masked, skip=None):
                q_i, do_i = q_ref[rows, :], do_ref[rows, :]
                t_prev = jnp.maximum(t - 1, 0)
                valids = [_tile_mask(i, j, ri, ci) if masked else None for j in blocks(t)]
                tile_of = lambda ref, s, h: ref[h // 2][:, (2 * s + h % 2) * BLK:(2 * s + h % 2 + 1) * BLK]
                nothing = [jnp.zeros((BLK, BLK), F32)] * HEAD_GROUP
                runs = [run_ref[h] for h in heads]
                rungs = [rung_ref[h] for h in heads]
                lgs, keep, ws, gs, gsum, dzs = {}, {}, {}, {}, {}, {}
                for s in range(2):
                    if s != skip:
                        lgs[s] = [_sb_logits(tile_of(z_ref, s, h), valids[s]) for h in heads]
                        keep[s] = _dot(jnp.concatenate([_two_terms(lgs[s][h][1]) for h in heads], axis=0), sums_keep)
                    if s == 0:
                        part_dq = owed_dq(t_prev)
                    else:
                        part_dk = owed_dk()
                for s in range(2):
                    ws[s], gs[s] = [], []
                    if s != skip:
                        for h in heads:
                            part = keep[s][h * BLK:(h + 1) * BLK]
                            w = jnp.exp2(lgs[s][h][0] + (tot_b[h] - runs[h] - part[:, :BLK]))
                            ws[s].append(jnp.where(valids[s], w, 0.0) if masked else w)
                            runs[h] = runs[h] + part[:, BLK:]
                            gs[s].append(tile_of(dw_ref, s, h) * ws[s][h])
                        gsum[s] = _dot(jnp.concatenate([_two_terms(gs[s][h]) for h in heads], axis=0), sums_g)
                    else:
                        ws[s] = nothing
                    if s == 0:
                        part_dv = owed_dv()
                    else:
                        z_next = [_dot(q_i[:, _pair_lanes(p)], stacks(k_ref, blocks(t + 1), p), NT) for p in pairs]
                for s in range(2):
                    dzs[s] = []
                    if s != skip:
                        for h in heads:
                            part = gsum[s][h * BLK:(h + 1) * BLK]
                            beta = jnp.exp2(lgs[s][h][0])
                            dz = gs[s][h] * (1.0 - beta) - beta * (part[:, :BLK] + rungs[h])
                            dzs[s].append(jnp.where(valids[s], dz, 0.0) if masked else dz)
                            rungs[h] = rungs[h] + part[:, BLK:]
                    else:
                        dzs[s] = nothing
                    if s == 0:
                        dw_next = [_dot(do_i[:, _pair_lanes(p)], stacks(v_ref, blocks(t + 1), p), NT) for p in pairs]
                settle(t_prev, (part_dq, part_dk, part_dv))
                for s in range(2):
                    for h in heads:
                        dz_ref[s * HEAD_GROUP + h] = dzs[s][h].astype(BF16)
                        wb_ref[s * HEAD_GROUP + h] = ws[s][h].astype(BF16)
                for h in heads:
                    run_ref[h] = runs[h]
                    rung_ref[h] = rungs[h]
                for p in pairs:
                    z_ref[p] = z_next[p]
                    dw_ref[p] = dw_next[p]

            n_calls = (i + 2) // 2
            tile(0, True)

            def mid(t, carry):
                tile(t, False)
                return carry

            lax.fori_loop(1, n_calls - 1, mid, 0)

            @pl.when((n_calls >= 2) & (i % 2 == 1))
            def _():
                tile(n_calls - 1, True)

            @pl.when((n_calls >= 2) & (i % 2 == 0))
            def _():
                tile(n_calls - 1, True, skip=1)

            settle(n_calls - 1, (owed_dq(n_calls - 1), owed_dk(), owed_dv()))
            dq_ref[rows, :] = (dq_acc[...] * SB_SCALE).astype(dq_ref.dtype)
            return carry

        lax.fori_loop(0, nb, q_block, 0)
        dk_ref[...] = dk_acc[...].astype(dk_ref.dtype)
        dv_ref[...] = dv_acc[...].astype(dv_ref.dtype)

        @pl.when(step == groups - 1)
        def _():
            ex.wait(ex_in, ex_out, sems)

    spec = pl.BlockSpec((n_rows, gw), lambda g: (0, g))
    tot_spec = pl.BlockSpec((None, n_rows, HEAD_GROUP), lambda g: (g, 0, 0))
    out = jax.ShapeDtypeStruct((n_rows, width), BF16)
    tile_f32 = pltpu.VMEM((HEAD_GROUP, BLK, BLK), F32)
    tile_bf16 = pltpu.VMEM((2 * HEAD_GROUP, BLK, BLK), BF16)
    pair_f32 = pltpu.VMEM((HEAD_GROUP // 2, BLK, 4 * BLK), F32)
    pair_stack = pltpu.VMEM((HEAD_GROUP // 2, 2 * BLK, BLK), BF16)
    res = pl.pallas_call(
        body, name="attn_bwd", grid=(groups,), in_specs=[spec, spec, spec, tot_spec, spec] + ex.specs,
        out_specs=[spec] * 3 + ex.specs, out_shape=[out] * 3 + ex.out_shape,
        scratch_shapes=[pltpu.VMEM((BLK, gw), F32), pltpu.VMEM((n_rows, gw), F32), pltpu.VMEM((n_rows, gw), F32),
                        tile_f32, tile_f32, tile_f32, pair_f32, pair_f32, tile_bf16, tile_bf16, pair_stack,
                        pair_stack] + ex.scratch,
        compiler_params=_params(n_rows * (7 * gw * 2 + LANES * 4 + gw * 4), ("arbitrary",)),
    )(q, k, v, keep_total, do, *ex_arrays)
    return res[0], res[1], res[2], res[3:]


class _Exchange:
    def __init__(self, arrays, gather):
        self.n = len(arrays)
        self.gather = gather
        self.out_shape = [jax.ShapeDtypeStruct(((N_DEV,) + a.shape) if gather else a.shape, a.dtype) for a in arrays]
        self.scratch = [pltpu.SemaphoreType.DMA((self.n, N_DEV - 1)), pltpu.SemaphoreType.DMA((self.n, N_DEV - 1)),
                        pltpu.SemaphoreType.DMA((self.n,))]
        self.specs = [pl.BlockSpec(memory_space=pl.ANY)] * self.n

    def _copies(self, ins, outs, sems, with_receives):
        send_sems, recv_sems, local_sems = sems
        x, y, c = lax.axis_index("x"), lax.axis_index("y"), lax.axis_index("c")
        me = 4 * x + 2 * y + c
        gather = self.gather
        local, sends, recvs = [], [], []
        for a in range(self.n):
            local.append(pltpu.make_async_copy(ins[a] if gather else ins[a].at[me], outs[a].at[me if gather else 0],
                                               local_sems.at[a]))
        for r in range(1, N_DEV):
            px = 1 - x if r & 4 else x
            py = 1 - y if r & 2 else y
            pc = 1 - c if r & 1 else c
            idx = 4 * px + 2 * py + pc
            for a in range(self.n):
                src = ins[a] if gather else ins[a].at[idx]
                pair = dict(send_sem=send_sems.at[a, r - 1], recv_sem=recv_sems.at[a, r - 1],
                            device_id=(px, py, pc), device_id_type=pl.DeviceIdType.MESH)
                sends.append(pltpu.make_async_remote_copy(src_ref=src, dst_ref=outs[a].at[me if gather else r], **pair))
                if with_receives:
                    recvs.append(pltpu.make_async_remote_copy(src_ref=src, dst_ref=outs[a].at[idx if gather else r], **pair))
        return local, sends, recvs

    def start(self, ins, outs, sems):
        local, sends, _ = self._copies(ins, outs, sems, with_receives=False)
        for cp in local + sends:
            cp.start()

    def wait(self, ins, outs, sems):
        local, sends, recvs = self._copies(ins, outs, sems, with_receives=True)
        for cp in recvs:
            cp.wait_recv()
        for cp in sends:
            cp.wait_send()
        for cp in local:
            cp.wait()


def _peer_exchange(name, parts):
    exs = [_Exchange(arrays, gather) for arrays, gather in parts]
    n = sum(ex.n for ex in exs)
    n_sems = len(exs[0].scratch)

    def body(*refs):
        at, views = 0, []
        for k, ex in enumerate(exs):
            views.append((ex, refs[at:at + ex.n], refs[n + at:n + at + ex.n],
                          refs[2 * n + k * n_sems:2 * n + (k + 1) * n_sems]))
            at += ex.n
        for ex, ins, outs, sems in views:
            ex.start(ins, outs, sems)
        for ex, ins, outs, sems in views:
            ex.wait(ins, outs, sems)

    res = pl.pallas_call(
        body, name=name, in_specs=[s for ex in exs for s in ex.specs], out_specs=[s for ex in exs for s in ex.specs],
        out_shape=[s for ex in exs for s in ex.out_shape], scratch_shapes=[s for ex in exs for s in ex.scratch],
    )(*[a for arrays, _ in parts for a in arrays])
    out, at = [], 0
    for ex in exs:
        out.append(res[at:at + ex.n])
        at += ex.n
    return out


def _gather_two_level(name, arrays):
    n = len(arrays)

    def body(*refs):
        ins, outs = refs[:n], refs[n:2 * n]
        send_sems, recv_sems, local_sems = refs[2 * n:]
        x, y, c = lax.axis_index("x"), lax.axis_index("y"), lax.axis_index("c")
        sibling = (x, y, 1 - c)
        chips = [(1 - x, y), (x, 1 - y), (1 - x, 1 - y)]

        def slot(a, dev):
            return outs[a].at[4 * dev[0] + 2 * dev[1] + dev[2]]

        def copy(a, k, block, to, src=None):
            return pltpu.make_async_remote_copy(
                src_ref=slot(a, block) if src is None else src, dst_ref=slot(a, block),
                send_sem=send_sems.at[a, k], recv_sem=recv_sems.at[a, k], device_id=to, device_id_type=pl.DeviceIdType.MESH)

        me = (x, y, c)
        mine = [pltpu.make_async_copy(ins[a], slot(a, me), local_sems.at[a]) for a in range(n)]
        first = [copy(a, 0, me, sibling, src=ins[a]) for a in range(n)]
        first += [copy(a, 1 + j, me, (*chip, c), src=ins[a]) for j, chip in enumerate(chips) for a in range(n)]
        for cp in mine + first:
            cp.start()
        passed = []
        for j, chip in enumerate(chips):
            for a in range(n):
                copy(a, 1 + j, (*chip, c), me).wait_recv()
            for a in range(n):
                cp = copy(a, 4 + j, (*chip, c), sibling)
                cp.start()
                passed.append(cp)
        for a in range(n):
            copy(a, 0, sibling, me).wait_recv()
            for j, chip in enumerate(chips):
                copy(a, 4 + j, (*chip, 1 - c), me).wait_recv()
        for cp in first + passed:
            cp.wait_send()
        for cp in mine:
            cp.wait()

    any_spec = pl.BlockSpec(memory_space=pl.ANY)
    return pl.pallas_call(
        body, name=name, in_specs=[any_spec] * n, out_specs=[any_spec] * n,
        out_shape=[jax.ShapeDtypeStruct((N_DEV,) + a.shape, a.dtype) for a in arrays],
        scratch_shapes=[pltpu.SemaphoreType.DMA((n, N_DEV - 1)), pltpu.SemaphoreType.DMA((n, N_DEV - 1)),
                        pltpu.SemaphoreType.DMA((n,))],
    )(*arrays)


def _sum_slots(name, x):
    def body(x_ref, o_ref):
        acc = x_ref[0]
        for s in range(1, N_DEV):
            acc = acc + x_ref[s]
        o_ref[...] = acc
    return pl.pallas_call(body, name=name, out_shape=jax.ShapeDtypeStruct(x.shape[1:], F32))(x)


def _adamw(name, w, slots, m, v):
    n_slots, rows, cols = slots.shape
    tr = next((t for t in (256, 176, 128) if rows % t == 0 and rows > t), rows)

    def body(w_ref, s_ref, m_ref, v_ref, g_ref, d_ref, nm_ref, nv_ref):
        g = s_ref[0].astype(F32)
        for s in range(1, n_slots):
            g = g + s_ref[s].astype(F32)
        nm = ADAM_B1 * m_ref[...] + (1.0 - ADAM_B1) * g
        nv = ADAM_B2 * v_ref[...] + (1.0 - ADAM_B2) * (g * g)
        m_hat = nm / (1.0 - ADAM_B1 ** ADAM_STEP)
        v_hat = nv / (1.0 - ADAM_B2 ** ADAM_STEP)
        g_ref[...] = g
        d_ref[...] = -ADAM_LR * (m_hat / (jnp.sqrt(v_hat) + ADAM_EPS) + ADAM_WD * w_ref[...])
        nm_ref[...] = nm
        nv_ref[...] = nv

    spec = pl.BlockSpec((tr, cols), lambda i: (i, 0))
    out = jax.ShapeDtypeStruct((rows, cols), F32)
    return pl.pallas_call(
        body, name=name, grid=(rows // tr,),
        in_specs=[spec, pl.BlockSpec((n_slots, tr, cols), lambda i: (0, i, 0)), spec, spec],
        out_specs=[spec] * 4, out_shape=[out] * 4,
        compiler_params=_params((n_slots + 7) * tr * cols * 4, ("parallel",)),
    )(w, slots, m, v)


def _pad_lanes(a):
    return jnp.pad(a, ((0, 0), (0, LANES - a.shape[1])))


def kernel(x, meta_tokens, mix_pre_g, w_in, ssd_conv_w, ssd_conv_b, ssd_dt_bias, ssd_a_log, ssd_d, ssd_norm_g, sb_norm_g, w_out, mix_post_g, ffn_pre_g, w_up, ffn_conv_w, ffn_conv_b, w_down, ffn_post_g, loss_target, m_meta_tokens, m_mix_pre_g, m_w_in, m_ssd_conv_w, m_ssd_conv_b, m_ssd_dt_bias, m_ssd_a_log, m_ssd_d, m_ssd_norm_g, m_sb_norm_g, m_w_out, m_mix_post_g, m_ffn_pre_g, m_w_up, m_ffn_conv_w, m_ffn_conv_b, m_w_down, m_ffn_post_g, v_meta_tokens, v_mix_pre_g, v_w_in, v_ssd_conv_w, v_ssd_conv_b, v_ssd_dt_bias, v_ssd_a_log, v_ssd_d, v_ssd_norm_g, v_sb_norm_g, v_w_out, v_mix_post_g, v_ffn_pre_g, v_w_up, v_ffn_conv_w, v_ffn_conv_b, v_w_down, v_ffn_post_g):
    seq = x.shape[1]
    me = 4 * lax.axis_index("x") + 2 * lax.axis_index("y") + lax.axis_index("c")
    in_cols = w_in.shape[2]
    up_cols = w_up.shape[2]
    out_rows = w_out.shape[1]
    down_rows = w_down.shape[1]

    g_in, g_meta, g_scw, g_fcw = _gather_two_level(
        "gather_w_in", [w_in[0].astype(BF16), meta_tokens, ssd_conv_w[0], ffn_conv_w[0]])
    late_weights = [w_out[0].astype(BF16), w_up[0].astype(BF16), w_down[0].astype(BF16)]
    w_in_full = g_in.transpose(1, 0, 2).reshape(D_MODEL, N_DEV * in_cols)
    off = [0, SSD_INNER, SSD_INNER + XBC, SSD_INNER + XBC + HEADS]
    w_z = w_in_full[:, off[0]:off[1]]
    w_xbc = w_in_full[:, off[1]:off[2]]
    w_dt = _pad_lanes(w_in_full[:, off[2]:off[3]])
    w_q = w_in_full[:, off[3]:off[3] + SSD_INNER]
    w_k = w_in_full[:, off[3] + SSD_INNER:off[3] + 2 * SSD_INNER]
    w_v = w_in_full[:, off[3] + 2 * SSD_INNER:off[3] + 3 * SSD_INNER]
    meta_full = g_meta.transpose(1, 0, 2).reshape(N_META, D_MODEL)
    scw_full = g_scw.transpose(1, 0, 2).reshape(SSD_CONV, XBC)
    fcw_full = g_fcw.transpose(1, 0, 2).reshape(FFN_CONV, D_FF)

    dt_bias_p, a_log_p, d_p = _pad_lanes(ssd_dt_bias), _pad_lanes(ssd_a_log), _pad_lanes(ssd_d)

    h0 = jnp.concatenate([jnp.zeros((PAD, D_MODEL), F32), meta_full, x[0]], axis=0)
    target = loss_target[0]
    xn1 = _rms_fwd("rms_pre_mix", h0, mix_pre_g)
    z = _mm("proj_z", [(xn1, w_z)], "nn", F32)
    xbc_raw = _mm("proj_xbc", [(xn1, w_xbc)], "nn", F32)
    dtr = _mm("proj_dt", [(xn1, w_dt)], "nn", F32)
    q = _mm("proj_q", [(xn1, w_q * SB_SCALE)], "nn", BF16)
    k = _mm("proj_k", [(xn1, w_k)], "nn", BF16)
    v = _mm("proj_v", [(xn1, w_v)], "nn", BF16)
    xbc_act = _ssd_conv_fwd(xbc_raw, scw_full, ssd_conv_b)
    ypre, y_ssd, states = _ssd_fwd(xbc_act, dtr, z, dt_bias_p, a_log_p, d_p, ssd_norm_g)
    o, keep_total, (g_out, g_up, g_down) = _attn_fwd(q, k, v, _Exchange(late_weights, gather=True), late_weights)
    w_out_full = g_out.reshape(N_DEV * out_rows, D_MODEL)
    wo_ssd, wo_sb = w_out_full[:SSD_INNER], w_out_full[SSD_INNER:]
    w_up_full = g_up.transpose(1, 0, 2).reshape(D_MODEL, N_DEV * up_cols)
    w_gate, w_lin = w_up_full[:, :D_FF], w_up_full[:, D_FF:]
    w_down_full = g_down.reshape(N_DEV * down_rows, D_MODEL)
    y_sb = _rms_fwd("rms_sb", o, sb_norm_g)
    mix = _mm("mix_out", [(y_ssd, wo_ssd), (y_sb, wo_sb)], "nn", F32)
    h1, xn2 = _mix_post(mix, h0, mix_post_g, ffn_pre_g)
    g_raw = _mm("ffn_gate", [(xn2, w_gate)], "nn", F32)
    u = _mm("ffn_lin", [(xn2, w_lin)], "nn", F32)
    act = _ffn_act(g_raw, u, fcw_full, ffn_conv_b)
    f = _mm("ffn_down", [(act, w_down_full)], "nn", F32)
    dh2, df, loss_row, dg_ffn_post = _loss_post(f, h1, target, ffn_post_g)

    dact = _mm("d_act", [(df, w_down_full)], "nt", F32)
    dw_down = _mm("dw_down", [(act, df)], "tn", F32)
    dg_conv, du = _ffn_bwd_act(dact, u, g_raw, fcw_full, ffn_conv_b)
    dg_raw, dfcw0, dfcw1, dfcw2, dfcb = _conv_bwd("ffn_conv_bwd", dg_conv, g_raw, fcw_full, FFN_CONV)
    dxn2 = _mm("d_xn2", [(dg_raw, w_gate), (du, w_lin)], "nt", F32)
    dw_gate = _mm("dw_gate", [(xn2, dg_raw)], "tn", F32)
    dw_lin = _mm("dw_lin", [(xn2, du)], "tn", F32)
    dh1, dmix, dg_ffn_pre, dg_mix_post = _mid_bwd(dxn2, h1, dh2, mix, ffn_pre_g, mix_post_g)

    dy_ssd = _mm("d_yssd", [(dmix, wo_ssd)], "nt", F32)
    dy_sb = _mm("d_ysb", [(dmix, wo_sb)], "nt", F32)
    dwo_ssd = _mm("dw_out_ssd", [(y_ssd, dmix)], "tn", F32)
    dwo_sb = _mm("dw_out_sb", [(y_sb, dmix)], "tn", F32)
    do, dg_sb = _norm_bwd("sb_norm_bwd", dy_sb, o, sb_norm_g)
    half = N_DEV // 2
    early_slabs = [
        jnp.concatenate([dwo_ssd, dwo_sb], axis=0).reshape(N_DEV, out_rows, D_MODEL),
        jnp.concatenate([dw_gate.reshape(D_MODEL, half, up_cols).transpose(1, 0, 2),
                         dw_lin.reshape(D_MODEL, half, up_cols).transpose(1, 0, 2)], axis=0),
        dw_down.reshape(N_DEV, down_rows, D_MODEL)]
    dq, dk, dv, (l_out, l_up, l_down) = _attn_bwd(q, k, v, keep_total, do, _Exchange(early_slabs, gather=False), early_slabs)
    dz, dxbc_act, ddtr, dg_ssd_norm, dd_skip, da_log, ddt_bias = _ssd_bwd(
        dy_ssd, ypre, z, xbc_act, dtr, states, dt_bias_p, a_log_p, d_p, ssd_norm_g)
    dconv = _ssd_conv_bwd_act(dxbc_act, xbc_raw, scw_full, ssd_conv_b)
    dxbc_raw, dscw0, dscw1, dscw2, dscw3, dscb = _conv_bwd("ssd_conv_bwd", dconv, xbc_raw, scw_full, SSD_CONV)
    segs = [(dz, w_z), (dxbc_raw, w_xbc), (ddtr, w_dt), (dq, w_q), (dk, w_k), (dv, w_v)]
    dw_segs = [_mm("dw_in_%d" % s, [(xn1, d)], "tn", BF16) for s, (d, _) in enumerate(segs)]
    dw_segs[2] = dw_segs[2][:, :HEADS]
    dw_in = jnp.concatenate(dw_segs, axis=1)
    slab_in = dw_in.reshape(D_MODEL, N_DEV, in_cols).transpose(1, 0, 2)
    dxn1, (l_in,) = _mm("d_xn1", segs, "nt", F32, _Exchange([slab_in], gather=False), [slab_in])
    dh0, dg_mix_pre = _first_bwd(dxn1, h0, dh1, mix_pre_g)
    grad_x = dh0[BLK:][None]

    small = [dg_mix_pre, dscb, ddt_bias, da_log, dd_skip, dg_ssd_norm, dg_sb, dg_mix_post, dg_ffn_pre, dfcb,
             dg_ffn_post, dh0[PAD:BLK].reshape(1, -1), dscw0, dscw1, dscw2, dscw3, dfcw0, dfcw1, dfcw2, loss_row]
    sizes = [a.shape[1] for a in small]
    total = sum(sizes)
    rows_packed = -(-total // (LANES * HALO)) * HALO
    packed = jnp.pad(jnp.concatenate(small, axis=1), ((0, 0), (0, rows_packed * LANES - total)))
    ((gathered,),) = _peer_exchange("gather_small_grads", [([packed.reshape(rows_packed, LANES)], True)])
    summed = _sum_slots("sum_small_grads", gathered).reshape(1, rows_packed * LANES)
    pieces, at = [], 0
    for s in sizes:
        pieces.append(summed[:, at:at + s])
        at += s
    (g_mix_pre, g_scb, g_dtb, g_alog, g_dskip, g_ssd_norm, g_sb, g_mix_post, g_ffn_pre, g_fcb, g_ffn_post,
     g_meta_flat, gs0, gs1, gs2, gs3, gf0, gf1, gf2, loss_all) = pieces
    loss = loss_all[0, 0]
    g_dtb, g_alog, g_dskip = g_dtb[:, :HEADS], g_alog[:, :HEADS], g_dskip[:, :HEADS]
    g_meta_full = g_meta_flat.reshape(N_META, D_MODEL)
    g_scw_full = jnp.concatenate([gs0, gs1, gs2, gs3], axis=0)
    g_fcw_full = jnp.concatenate([gf0, gf1, gf2], axis=0)
    meta_cols, scw_cols, fcw_cols = meta_tokens.shape[1], ssd_conv_w.shape[2], ffn_conv_w.shape[2]
    g_meta_mine = lax.dynamic_slice(g_meta_full, (0, me * meta_cols), (N_META, meta_cols))
    g_scw_mine = lax.dynamic_slice(g_scw_full, (0, me * scw_cols), (SSD_CONV, scw_cols))
    g_fcw_mine = lax.dynamic_slice(g_fcw_full, (0, me * fcw_cols), (FFN_CONV, fcw_cols))

    def lead(a):
        return a[None]

    upd = [
        _adamw("adamw_meta", meta_tokens, lead(g_meta_mine), m_meta_tokens, v_meta_tokens),
        _adamw("adamw_mix_pre_g", mix_pre_g, lead(g_mix_pre), m_mix_pre_g, v_mix_pre_g),
        [lead(a) for a in _adamw("adamw_w_in", w_in[0], l_in, m_w_in[0], v_w_in[0])],
        [lead(a) for a in _adamw("adamw_ssd_conv_w", ssd_conv_w[0], lead(g_scw_mine), m_ssd_conv_w[0], v_ssd_conv_w[0])],
        _adamw("adamw_ssd_conv_b", ssd_conv_b, lead(g_scb), m_ssd_conv_b, v_ssd_conv_b),
        _adamw("adamw_ssd_dt_bias", ssd_dt_bias, lead(g_dtb), m_ssd_dt_bias, v_ssd_dt_bias),
        _adamw("adamw_ssd_a_log", ssd_a_log, lead(g_alog), m_ssd_a_log, v_ssd_a_log),
        _adamw("adamw_ssd_d", ssd_d, lead(g_dskip), m_ssd_d, v_ssd_d),
        _adamw("adamw_ssd_norm_g", ssd_norm_g, lead(g_ssd_norm), m_ssd_norm_g, v_ssd_norm_g),
        _adamw("adamw_sb_norm_g", sb_norm_g, lead(g_sb), m_sb_norm_g, v_sb_norm_g),
        [lead(a) for a in _adamw("adamw_w_out", w_out[0], l_out, m_w_out[0], v_w_out[0])],
        _adamw("adamw_mix_post_g", mix_post_g, lead(g_mix_post), m_mix_post_g, v_mix_post_g),
        _adamw("adamw_ffn_pre_g", ffn_pre_g, lead(g_ffn_pre), m_ffn_pre_g, v_ffn_pre_g),
        [lead(a) for a in _adamw("adamw_w_up", w_up[0], l_up, m_w_up[0], v_w_up[0])],
        [lead(a) for a in _adamw("adamw_ffn_conv_w", ffn_conv_w[0], lead(g_fcw_mine), m_ffn_conv_w[0], v_ffn_conv_w[0])],
        _adamw("adamw_ffn_conv_b", ffn_conv_b, lead(g_fcb), m_ffn_conv_b, v_ffn_conv_b),
        [lead(a) for a in _adamw("adamw_w_down", w_down[0], l_down, m_w_down[0], v_w_down[0])],
        _adamw("adamw_ffn_post_g", ffn_post_g, lead(g_ffn_post), m_ffn_post_g, v_ffn_post_g),
    ]
    grads = [u_[0] for u_ in upd]
    deltas = [u_[1] for u_ in upd]
    new_m = [u_[2] for u_ in upd]
    new_v = [u_[3] for u_ in upd]
    return (loss, grad_x, *grads, *deltas, *new_m, *new_v)
```

```python
import math

import jax
import jax.numpy as jnp
from jax import lax
from jax.experimental import pallas as pl
from jax.experimental.pallas import tpu as pltpu

F32 = jnp.float32
BF16 = jnp.bfloat16
HI = lax.Precision.HIGHEST

D_MODEL = 1024
N_META = 16
BLK = 128
PAD = BLK - N_META
HEADS = 16
HEAD_DIM = 64
SSD_GROUPS = 2
SSD_STATE = 128
HEADS_PER_GROUP = HEADS // SSD_GROUPS
SSD_INNER = HEADS * HEAD_DIM
SSD_CONV = 4
XBC = SSD_INNER + 2 * SSD_GROUPS * SSD_STATE
OFF_B = SSD_INNER
OFF_C = SSD_INNER + SSD_GROUPS * SSD_STATE
D_FF = 2816
FFN_CONV = 3
EPS = 1e-6
SB_SCALE = 1.0 / math.sqrt(HEAD_DIM)
N_DEV = 8
LANES = 128
HALO = 8

ADAM_LR = 0.001
ADAM_B1 = 0.9
ADAM_B2 = 0.999
ADAM_EPS = 1e-08
ADAM_WD = 0.01
ADAM_STEP = 10

VMEM_FLOOR = 32 << 20
VMEM_CEIL = 60 << 20
MM_BUDGET = 20 << 20
ROW_BUDGET = 6 << 20

NN = (((1,), (0,)), ((), ()))
NT = (((1,), (1,)), ((), ()))
TN = (((0,), (0,)), ((), ()))


def _params(tile_bytes, sem=None):
    limit = int(min(max(2 * tile_bytes + (8 << 20), VMEM_FLOOR), VMEM_CEIL))
    return pltpu.CompilerParams(vmem_limit_bytes=limit, dimension_semantics=sem)


def _nbytes(shape, dtype):
    n = 1
    for s in shape:
        n *= s
    return n * jnp.dtype(dtype).itemsize


def _dot(a, b, dims=NN, precision=None):
    return lax.dot_general(a, b, dims, precision=precision, preferred_element_type=F32)


def _softplus(x):
    return jnp.maximum(x, 0.0) + jnp.log1p(jnp.exp(-jnp.abs(x)))


def _rms(x, g):
    r = lax.rsqrt(jnp.mean(x * x, axis=-1, keepdims=True) + EPS)
    return x * r * g


def _rms_bwd(dy, x, g):
    r = lax.rsqrt(jnp.mean(x * x, axis=-1, keepdims=True) + EPS)
    xh = x * r
    u = dy * g
    dx = r * (u - xh * jnp.mean(xh * u, axis=-1, keepdims=True))
    return dx, jnp.sum(dy * xh, axis=0, keepdims=True)


def _gelu(x):
    c = math.sqrt(2.0 / math.pi)
    return 0.5 * x * (1.0 + jnp.tanh(c * (x + 0.044715 * x * x * x)))


def _gelu_and_grad(x):
    c = math.sqrt(2.0 / math.pi)
    x2 = x * x
    t = jnp.tanh(c * (x + 0.044715 * x2 * x))
    half = 0.5 * (1.0 + t)
    return x * half, half + 0.5 * x * (1.0 - t * t) * c * (1.0 + 3.0 * 0.044715 * x2)


def _row_tile(rows, bytes_per_row):
    big = 384
    return big if rows % big == 0 and big * bytes_per_row <= ROW_BUDGET else BLK


def _mm(name, pairs, mode, out_dtype, ex=None, ex_arrays=()):
    a0, b0 = pairs[0]
    if mode == "tn":
        m, n = a0.shape[1], b0.shape[1]
    elif mode == "nt":
        m, n = a0.shape[0], b0.shape[0]
    else:
        m, n = a0.shape[0], b0.shape[1]
    dims = {"nn": NN, "nt": NT, "tn": TN}[mode]

    def tile_bytes(tm, tn):
        tot = tm * tn * jnp.dtype(out_dtype).itemsize
        for a, b in pairs:
            k = a.shape[0] if mode == "tn" else a.shape[1]
            tot += tm * k * a.dtype.itemsize + tn * k * b.dtype.itemsize
        return tot

    cands_m = [t for t in (1408, 1024, 512, 384, 256, 128) if m % t == 0] or [m]
    cands_n = [t for t in (1408, 1024, 768, 512, 256, 128) if n % t == 0] or [n]
    best = None
    for tm in cands_m:
        for tn in cands_n:
            if tile_bytes(tm, tn) <= MM_BUDGET and (best is None or tm * tn > best[0] * best[1]):
                best = (tm, tn)
    tm, tn = best if best is not None else (cands_m[-1], cands_n[-1])
    npairs = len(pairs)

    grid = (m // tm, n // tn)

    def body(*refs):
        if ex is None:
            ins, o_ref = refs[:2 * npairs], refs[2 * npairs]
        else:
            ins, ex_in, (o_ref,), ex_out, _, sems = _ride_along(ex, 2 * npairs, 1, refs)
            step = pl.program_id(0) * grid[1] + pl.program_id(1)

            @pl.when(step == 0)
            def _():
                ex.start(ex_in, ex_out, sems)

        acc = None
        for p in range(npairs):
            part = _dot(ins[2 * p][...], ins[2 * p + 1][...], dims)
            acc = part if acc is None else acc + part
        o_ref[...] = acc.astype(o_ref.dtype)
        if ex is not None:
            @pl.when(step == grid[0] * grid[1] - 1)
            def _():
                ex.wait(ex_in, ex_out, sems)

    in_specs, args = [], []
    for a, b in pairs:
        if mode == "tn":
            k = a.shape[0]
            in_specs += [pl.BlockSpec((k, tm), lambda i, j: (0, i)), pl.BlockSpec((k, tn), lambda i, j: (0, j))]
        elif mode == "nt":
            k = a.shape[1]
            in_specs += [pl.BlockSpec((tm, k), lambda i, j: (i, 0)), pl.BlockSpec((tn, k), lambda i, j: (j, 0))]
        else:
            k = a.shape[1]
            in_specs += [pl.BlockSpec((tm, k), lambda i, j: (i, 0)), pl.BlockSpec((k, tn), lambda i, j: (0, j))]
        args += [a, b]
    out_spec = pl.BlockSpec((tm, tn), lambda i, j: (i, j))
    out_shape = jax.ShapeDtypeStruct((m, n), out_dtype)
    if ex is None:
        return pl.pallas_call(
            body, name=name, grid=grid, in_specs=in_specs, out_specs=out_spec, out_shape=out_shape,
            compiler_params=_params(tile_bytes(tm, tn), ("parallel", "parallel")),
        )(*args)
    res = pl.pallas_call(
        body, name=name, grid=grid, in_specs=in_specs + ex.specs, out_specs=[out_spec] + ex.specs,
        out_shape=[out_shape] + ex.out_shape, scratch_shapes=ex.scratch,
        compiler_params=_params(tile_bytes(tm, tn), ("arbitrary", "arbitrary")),
    )(*args, *ex_arrays)
    return res[0], res[1:]


class _Window:
    def __init__(self, ref, cols):
        self.ref, self.cols = ref, cols

    def __getitem__(self, idx):
        return self.ref[:, self.cols] if idx is Ellipsis else self.ref[idx[0], self.cols]


def _rowcall(name, fn, rows=(), prevs=(), nexts=(), pars=(), out_rows=(), out_accs=(), chunk=None, late=()):
    rows, prevs, nexts, pars, late = list(rows), list(prevs), list(nexts), list(pars), list(late)
    n_rows = (rows + prevs + nexts)[0].shape[0]
    per_row = sum(a.shape[1] * a.dtype.itemsize for a in rows + prevs + nexts + late)
    per_row += sum(c * jnp.dtype(dt).itemsize for c, dt in out_rows) + sum(a.shape[1] * 4 for a in prevs + nexts)
    tm = _row_tile(n_rows, per_row)
    nt = n_rows // tm
    hb = tm // HALO
    if late:
        tm = BLK
        nt, hb = n_rows // tm, tm // HALO
    rows = rows + late
    nr, npv, nnx, npar, nor, noa = len(rows), len(prevs), len(nexts), len(pars), len(out_rows), len(out_accs)

    def body(*refs):
        i = pl.program_id(0)
        k = 0
        row_refs = refs[k:k + nr]; k += nr
        pc = refs[k:k + npv]; k += npv
        ph = refs[k:k + npv]; k += npv
        nc = refs[k:k + nnx]; k += nnx
        nh = refs[k:k + nnx]; k += nnx
        par_refs = refs[k:k + npar]; k += npar
        orow = refs[k:k + nor]; k += nor
        oacc = refs[k:k + noa]; k += noa
        pscr = refs[k:k + npv]; k += npv
        nscr = refs[k:k + nnx]
        for c_, h_, s_ in zip(pc, ph, pscr):
            s_[0:HALO, :] = h_[...] * (i > 0).astype(F32)
            s_[HALO:HALO + tm, :] = c_[...]
        for c_, h_, s_ in zip(nc, nh, nscr):
            s_[0:tm, :] = c_[...]
            s_[tm:tm + HALO, :] = h_[...] * (i < nt - 1).astype(F32)
        if noa:
            @pl.when(i == 0)
            def _():
                for r_ in oacc:
                    r_[...] = jnp.zeros_like(r_)
        width = out_rows[0][0]
        windows = [slice(None)] if chunk is None else [slice(c0, c0 + chunk) for c0 in range(0, width, chunk)]
        for cols in windows:
            prev_fns = [(lambda s, s_=s_, cols=cols: s_[pl.ds(HALO, tm), cols] if s == 0 else
                         pltpu.roll(s_[:, cols], s, 0)[HALO:HALO + tm]) for s_ in pscr]
            next_fns = [(lambda s, s_=s_, cols=cols: s_[pl.ds(0, tm), cols] if s == 0 else
                         pltpu.roll(s_[:, cols], tm + HALO - s, 0)[0:tm]) for s_ in nscr]
            row_vals, acc_vals = fn(i, tm, [_Window(r_, cols) for r_ in row_refs], prev_fns, next_fns,
                                    [_Window(r_, cols) for r_ in par_refs])
            for r_, v in zip(orow, row_vals):
                r_[:, cols] = v.astype(r_.dtype)
            for r_, v in zip(oacc, acc_vals):
                r_[:, cols] += v

    def row_spec(a):
        return pl.BlockSpec((tm, a.shape[1]), lambda i: (i, 0))

    def whole(shape):
        return pl.BlockSpec(shape, lambda i: (0, 0))

    in_specs = [row_spec(a) for a in rows[:nr - len(late)]]
    in_specs += [pl.BlockSpec((tm, a.shape[1]), lambda i: (jnp.maximum(i - 1, 0), 0)) for a in late]
    in_specs += [row_spec(a) for a in prevs]
    in_specs += [pl.BlockSpec((HALO, a.shape[1]), lambda i: (jnp.maximum(i * hb - 1, 0), 0)) for a in prevs]
    in_specs += [row_spec(a) for a in nexts]
    in_specs += [pl.BlockSpec((HALO, a.shape[1]), lambda i: (jnp.minimum((i + 1) * hb, n_rows // HALO - 1), 0)) for a in nexts]
    in_specs += [whole(p.shape) for p in pars]
    out_specs = [pl.BlockSpec((tm, c), lambda i: (i, 0)) for c, _ in out_rows] + [whole(s) for s in out_accs]
    out_shape = [jax.ShapeDtypeStruct((n_rows, c), dt) for c, dt in out_rows]
    out_shape += [jax.ShapeDtypeStruct(s, F32) for s in out_accs]
    scratch = [pltpu.VMEM((tm + HALO, a.shape[1]), F32) for a in prevs + nexts]
    tile = tm * per_row
    res = pl.pallas_call(
        body, name=name, grid=(nt,), in_specs=in_specs, out_specs=out_specs, out_shape=out_shape,
        scratch_shapes=scratch, compiler_params=_params(2 * tile, ("arbitrary",)),
    )(*rows, *prevs, *prevs, *nexts, *nexts, *pars)
    return res


def _row_ids(i, tm):
    return i * tm + lax.broadcasted_iota(jnp.int32, (tm, 1), 0)


def _conv_taps(prev_fn, w_ref, b_ref, taps):
    acc = b_ref[...]
    for k in range(taps):
        acc = acc + prev_fn(taps - 1 - k) * w_ref[k:k + 1, :]
    return acc


def _rms_fwd(name, h, g):
    def fn(i, tm, rows, prevs, nexts, pars):
        return [_rms(rows[0][...], pars[0][...])], []
    return _rowcall(name, fn, rows=[h], pars=[g], out_rows=[(h.shape[1], BF16)])[0]


def _ssd_conv_fwd(xbc_raw, w, b):
    def fn(i, tm, rows, prevs, nexts, pars):
        c = _conv_taps(prevs[0], pars[0], pars[1], SSD_CONV)
        y = c * jax.nn.sigmoid(c)
        return [jnp.where(_row_ids(i, tm) >= PAD, y, 0.0)], []
    return _rowcall("ssd_conv_fwd", fn, prevs=[xbc_raw], pars=[w, b], out_rows=[(XBC, F32)], chunk=LANES)[0]


def _mix_post(mix, h0, g_post, g_pre):
    def fn(i, tm, rows, prevs, nexts, pars):
        h1 = rows[1][...] + _rms(rows[0][...], pars[0][...])
        return [h1, _rms(h1, pars[1][...])], []
    return _rowcall("mix_post", fn, rows=[mix, h0], pars=[g_post, g_pre],
                    out_rows=[(D_MODEL, F32), (D_MODEL, BF16)])


def _ffn_act(g_raw, u, w, b):
    def fn(i, tm, rows, prevs, nexts, pars):
        g = _conv_taps(prevs[0], pars[0], pars[1], FFN_CONV)
        return [_gelu(g) * rows[0][...]], []
    return _rowcall("ffn_act", fn, rows=[u], prevs=[g_raw], pars=[w, b], out_rows=[(D_FF, BF16)], chunk=LANES)[0]


def _loss_post(f, h1, target, g_post):
    def fn(i, tm, rows, prevs, nexts, pars):
        fv, g = rows[0][...], pars[0][...]
        h2 = rows[1][...] + _rms(fv, g)
        real = _row_ids(i, tm) >= BLK
        diff = jnp.where(real, h2 - rows[2][...], 0.0)
        loss = 0.5 * jnp.sum(jnp.mean(diff * diff, axis=-1, keepdims=True))
        dh2 = diff * (1.0 / D_MODEL)
        df, dg = _rms_bwd(dh2, fv, g)
        return [dh2, df], [jnp.zeros((1, LANES), F32) + loss, dg]
    return _rowcall("loss_post", fn, rows=[f, h1], late=[target], pars=[g_post],
                    out_rows=[(D_MODEL, F32), (D_MODEL, BF16)], out_accs=[(1, LANES), (1, D_MODEL)])


def _ffn_bwd_act(dact, u, g_raw, w, b):
    def fn(i, tm, rows, prevs, nexts, pars):
        g = _conv_taps(prevs[0], pars[0], pars[1], FFN_CONV)
        da = rows[0][...]
        gelu, grad = _gelu_and_grad(g)
        return [da * rows[1][...] * grad, da * gelu], []
    return _rowcall("ffn_bwd_act", fn, rows=[dact, u], prevs=[g_raw], pars=[w, b],
                    out_rows=[(D_FF, F32), (D_FF, BF16)], chunk=LANES)


def _conv_bwd(name, dy, x, w, taps):
    width = x.shape[1]

    def fn(i, tm, rows, prevs, nexts, pars):
        dy0 = nexts[0](0)
        dx = None
        for k in range(taps):
            term = nexts[0](taps - 1 - k) * pars[0][k:k + 1, :]
            dx = term if dx is None else dx + term
        dws = [jnp.sum(dy0 * prevs[0](taps - 1 - k), axis=0, keepdims=True) for k in range(taps)]
        return [dx], dws + [jnp.sum(dy0, axis=0, keepdims=True)]
    return _rowcall(name, fn, prevs=[x], nexts=[dy], pars=[w], out_rows=[(width, BF16)],
                    out_accs=[(1, width)] * (taps + 1), chunk=LANES)


def _mid_bwd(dxn2, h1, dh2, mix, g_pre, g_post):
    def fn(i, tm, rows, prevs, nexts, pars):
        d1, dg_pre = _rms_bwd(rows[0][...], rows[1][...], pars[0][...])
        dh1 = rows[2][...] + d1
        dmix, dg_post = _rms_bwd(dh1, rows[3][...], pars[1][...])
        return [dh1, dmix], [dg_pre, dg_post]
    return _rowcall("mid_bwd", fn, rows=[dxn2, h1, dh2, mix], pars=[g_pre, g_post],
                    out_rows=[(D_MODEL, F32), (D_MODEL, BF16)], out_accs=[(1, D_MODEL), (1, D_MODEL)])


def _norm_bwd(name, dy, x, g):
    def fn(i, tm, rows, prevs, nexts, pars):
        dx, dg = _rms_bwd(rows[0][...], rows[1][...], pars[0][...])
        return [dx], [dg]
    return _rowcall(name, fn, rows=[dy, x], pars=[g], out_rows=[(x.shape[1], BF16)], out_accs=[(1, x.shape[1])])


def _ssd_conv_bwd_act(dact, xbc_raw, w, b):
    def fn(i, tm, rows, prevs, nexts, pars):
        c = _conv_taps(prevs[0], pars[0], pars[1], SSD_CONV)
        s = jax.nn.sigmoid(c)
        dc = rows[0][...] * s * (1.0 + c * (1.0 - s))
        return [jnp.where(_row_ids(i, tm) >= PAD, dc, 0.0)], []
    return _rowcall("ssd_conv_bwd_act", fn, rows=[dact], prevs=[xbc_raw], pars=[w, b], out_rows=[(XBC, F32)],
                    chunk=LANES)[0]


def _first_bwd(dxn1, h0, dh1, g):
    def fn(i, tm, rows, prevs, nexts, pars):
        d0, dg = _rms_bwd(rows[0][...], rows[1][...], pars[0][...])
        return [rows[2][...] + d0], [dg]
    return _rowcall("first_bwd", fn, rows=[dxn1, h0, dh1], pars=[g], out_rows=[(D_MODEL, F32)],
                    out_accs=[(1, D_MODEL)])


def _ssd_chunk_terms(c, dtr_ref, bias_ref, alog_ref):
    ri = lax.broadcasted_iota(jnp.int32, (BLK, BLK), 0)
    ci = lax.broadcasted_iota(jnp.int32, (BLK, BLK), 1)
    causal = ri >= ci
    tril = causal.astype(F32)
    triu = (ri <= ci).astype(F32)
    rowmask = ((c * BLK + lax.broadcasted_iota(jnp.int32, (BLK, 1), 0)) >= PAD).astype(F32)
    dt = _softplus(dtr_ref[...] + bias_ref[...]) * rowmask
    a_neg = -jnp.exp(alog_ref[...])
    a = dt * a_neg
    cs = _dot(tril, a, NN, HI)
    cs_t = _dot(a, triu, TN, HI)
    return causal, triu, rowmask, dt, a_neg, cs, cs_t


def _decay_matrix(causal, cs_h, cs_t_h):
    return jnp.where(causal, jnp.exp(jnp.where(causal, cs_h - cs_t_h, 0.0)), 0.0)


def _head_spread(width):
    ri = lax.broadcasted_iota(jnp.int32, (LANES, HEADS * width), 0)
    ci = lax.broadcasted_iota(jnp.int32, (LANES, HEADS * width), 1)
    return ((ri * width <= ci) & (ci < (ri + 1) * width)).astype(BF16)


def _three_terms(x):
    hi = x.astype(BF16)
    rest = x - hi.astype(F32)
    mid = rest.astype(BF16)
    lo = (rest - mid.astype(F32)).astype(BF16)
    return jnp.concatenate([hi, mid, lo], axis=1)


def _spread(x, sel):
    return _dot(_three_terms(x), jnp.concatenate([sel, sel, sel], axis=0))


def _lane_sums(y, sel):
    return _dot(_three_terms(y), jnp.concatenate([sel, sel, sel], axis=1), NT)


def _ssd_spreads(dt, cs, d_ref, sel64, sel128):
    dt64 = _spread(dt, sel64)
    cs64 = _spread(cs, sel64)
    cs128 = _spread(cs, sel128)
    d64 = _spread(jnp.broadcast_to(d_ref[...], (HALO, LANES)), sel64)[0:1, :]
    cl64 = cs64[BLK - 1:BLK, :]
    return dt64, cs128, d64, jnp.exp(cs64), jnp.exp(cl64 - cs64), jnp.exp(cl64)


def _ssd_pair_terms(p, xbc_ref, dt64, cs128, cs_t, causal):
    lanes = _pair_lanes(p)
    xs = xbc_ref[:, lanes]
    decays = [_decay_matrix(causal, cs128[:, h * LANES:(h + 1) * LANES], cs_t[h:h + 1, :]) for h in (2 * p, 2 * p + 1)]
    return xs, xs * dt64[:, lanes], decays


def _ssd_fwd(xbc, dtr, z, dt_bias, a_log, d_skip, norm_g):
    n_rows = xbc.shape[0]
    nb = n_rows // BLK

    def body(xbc_ref, dtr_ref, z_ref, bias_ref, alog_ref, d_ref, g_ref, s64_ref, s128_ref,
             ypre_ref, yssd_ref, st_ref, state):
        c = pl.program_id(0)

        @pl.when(c == 0)
        def _():
            state[...] = jnp.zeros_like(state)

        st_ref[...] = state[...]
        causal, _, _, dt, _, cs, cs_t = _ssd_chunk_terms(c, dtr_ref, bias_ref, alog_ref)
        first = lax.broadcasted_iota(jnp.int32, (BLK, LANES), 1) < HEAD_DIM
        dt64, cs128, d64, from_start, to_end, chunk_decay = _ssd_spreads(dt, cs, d_ref, s64_ref[...], s128_ref[...])
        for g in range(SSD_GROUPS):
            b_b = xbc_ref[:, OFF_B + g * SSD_STATE:OFF_B + (g + 1) * SSD_STATE].astype(BF16)
            c_b = xbc_ref[:, OFF_C + g * SSD_STATE:OFF_C + (g + 1) * SSD_STATE].astype(BF16)
            cb = _dot(c_b, b_b, NT)
            for j in range(HEADS_PER_GROUP // 2):
                p = g * (HEADS_PER_GROUP // 2) + j
                lanes = _pair_lanes(p)
                xs, x_dt, decays = _ssd_pair_terms(p, xbc_ref, dt64, cs128, cs_t, causal)
                ms = [(cb * d).astype(BF16) for d in decays]
                s_p = state[:, lanes]
                y = _dot(jnp.concatenate(ms, axis=1), _head_stack(x_dt.astype(BF16), first))
                y = y + from_start[:, lanes] * _dot(c_b, s_p.astype(BF16))
                state[:, lanes] = chunk_decay[:, lanes] * s_p + _dot(b_b, (to_end[:, lanes] * x_dt).astype(BF16), TN)
                ypre_ref[:, lanes] = y + d64[:, lanes] * xs
        zz = z_ref[...]
        yg = ypre_ref[...] * (zz * jax.nn.sigmoid(zz))
        yssd_ref[...] = _rms(yg, g_ref[...]).astype(yssd_ref.dtype)

    blk = lambda w: pl.BlockSpec((BLK, w), lambda c: (c, 0))
    par = lambda a: pl.BlockSpec(a.shape, lambda c: (0, 0))
    sel64, sel128 = _head_spread(HEAD_DIM), _head_spread(LANES)
    return pl.pallas_call(
        body, name="ssd_fwd", grid=(nb,),
        in_specs=[blk(XBC), blk(LANES), blk(SSD_INNER), par(dt_bias), par(a_log), par(d_skip), par(norm_g),
                  par(sel64), par(sel128)],
        out_specs=[blk(SSD_INNER), blk(SSD_INNER), pl.BlockSpec((None, SSD_STATE, SSD_INNER), lambda c: (c, 0, 0))],
        out_shape=[jax.ShapeDtypeStruct((n_rows, SSD_INNER), F32), jax.ShapeDtypeStruct((n_rows, SSD_INNER), BF16),
                   jax.ShapeDtypeStruct((nb, SSD_STATE, SSD_INNER), F32)],
        scratch_shapes=[pltpu.VMEM((SSD_STATE, SSD_INNER), F32)],
        compiler_params=_params(8 << 20, ("arbitrary",)),
    )(xbc, dtr, z, dt_bias, a_log, d_skip, norm_g, sel64, sel128)


def _ssd_bwd(dy, ypre, z, xbc, dtr, states, dt_bias, a_log, d_skip, norm_g):
    n_rows = xbc.shape[0]
    nb = n_rows // BLK

    def body(dy_ref, ypre_ref, z_ref, xbc_ref, dtr_ref, st_ref, bias_ref, alog_ref, d_ref, g_ref, s64_ref, s128_ref,
             dz_ref, dxbc_ref, ddtr_ref, dgn_ref, dd_ref, dal_ref, ddtb_ref, dstate, dyp, red):
        step = pl.program_id(0)
        c = nb - 1 - step

        @pl.when(step == 0)
        def _():
            dstate[...] = jnp.zeros_like(dstate)
            for r_ in (dgn_ref, dd_ref, dal_ref, ddtb_ref):
                r_[...] = jnp.zeros_like(r_)

        yp, zz = ypre_ref[...], z_ref[...]
        sz = jax.nn.sigmoid(zz)
        silu = zz * sz
        dyg, dgn = _rms_bwd(dy_ref[...], yp * silu, g_ref[...])
        dgn_ref[...] += dgn
        dz_ref[...] = (dyg * yp * (sz * (1.0 + zz * (1.0 - sz)))).astype(dz_ref.dtype)
        dyp[...] = dyg * silu

        causal, triu, rowmask, dt, a_neg, cs, cs_t = _ssd_chunk_terms(c, dtr_ref, bias_ref, alog_ref)
        lane = lax.broadcasted_iota(jnp.int32, (1, LANES), 1)
        last_row = (lax.broadcasted_iota(jnp.int32, (BLK, 1), 0) == BLK - 1).astype(F32)
        first = lax.broadcasted_iota(jnp.int32, (BLK, LANES), 1) < HEAD_DIM
        sel64 = s64_ref[...]
        dt64, cs128, d64, from_start, to_end, chunk_decay = _ssd_spreads(dt, cs, d_ref, sel64, s128_ref[...])
        for g in range(SSD_GROUPS):
            b_b = xbc_ref[:, OFF_B + g * SSD_STATE:OFF_B + (g + 1) * SSD_STATE].astype(BF16)
            c_b = xbc_ref[:, OFF_C + g * SSD_STATE:OFF_C + (g + 1) * SSD_STATE].astype(BF16)
            cb = _dot(c_b, b_b, NT)
            b_twice = jnp.concatenate([b_b, b_b], axis=0)
            c_twice = jnp.concatenate([c_b, c_b], axis=0)
            db_g = jnp.zeros((BLK, SSD_STATE), F32)
            dc_g = jnp.zeros((BLK, SSD_STATE), F32)
            for j in range(HEADS_PER_GROUP // 2):
                p = g * (HEADS_PER_GROUP // 2) + j
                lanes = _pair_lanes(p)
                xs, x_dt, decays = _ssd_pair_terms(p, xbc_ref, dt64, cs128, cs_t, causal)
                ms = [(cb * d).astype(BF16) for d in decays]
                d_y = dyp[:, lanes]
                s_p, ds_p = st_ref[:, lanes], dstate[:, lanes]
                s_b, ds_b = s_p.astype(BF16), ds_p.astype(BF16)
                fs, te = from_start[:, lanes], to_end[:, lanes]
                x_b, dy_b = x_dt.astype(BF16), d_y.astype(BF16)
                x_st = _head_stack(x_b, first)
                fs_dy = (fs * d_y).astype(BF16)

                y_diag = _dot(jnp.concatenate(ms, axis=1), x_st)
                y_off = fs * _dot(c_b, s_b)
                end_part = te * _dot(b_b, ds_b)
                dx_diag = _dot(jnp.concatenate(ms, axis=0), _head_stack(dy_b, first), TN)
                d_x = dx_diag + end_part
                g2 = _dot(dy_b, x_st, NT)
                gl = [(g2[:, k * BLK:(k + 1) * BLK] * decays[k]).astype(BF16) for k in range(2)]
                dc_g = dc_g + _dot(jnp.concatenate(gl, axis=1), b_twice) + _dot(fs_dy, s_b, NT)
                db_g = db_g + _dot(jnp.concatenate(gl, axis=0), c_twice, TN) + _dot((te * x_dt).astype(BF16), ds_b, NT)
                red[0:BLK, lanes] = (dy_b.astype(F32) * y_diag - x_b.astype(F32) * dx_diag) + (d_y * y_off - x_dt * end_part)
                red[BLK:2 * BLK, lanes] = x_dt * end_part
                red[2 * BLK:3 * BLK, lanes] = d_x * xs
                red[3 * BLK:3 * BLK + HALO, lanes] = jnp.broadcast_to(jnp.sum(ds_p * s_p, axis=0, keepdims=True), (HALO, LANES))
                red[3 * BLK + HALO:3 * BLK + 2 * HALO, lanes] = jnp.broadcast_to(
                    jnp.sum(d_y * xs, axis=0, keepdims=True), (HALO, LANES))
                dxbc_ref[:, lanes] = d_x * dt64[:, lanes] + d64[:, lanes] * d_y
                dstate[:, lanes] = chunk_decay[:, lanes] * ds_p + _dot(c_b, fs_dy, TN)
            dxbc_ref[:, OFF_B + g * SSD_STATE:OFF_B + (g + 1) * SSD_STATE] = db_g
            dxbc_ref[:, OFF_C + g * SSD_STATE:OFF_C + (g + 1) * SSD_STATE] = dc_g
        sums = _lane_sums(red[...], sel64)
        at_last = (jnp.sum(sums[BLK:2 * BLK], axis=0, keepdims=True)
                   + jnp.exp(cs[BLK - 1:BLK, :]) * sums[3 * BLK:3 * BLK + 1])
        dcs = sums[0:BLK] + last_row * at_last
        ddt_x = sums[2 * BLK:3 * BLK]
        dd_row = sums[3 * BLK + HALO:3 * BLK + HALO + 1]
        da = _dot(triu, dcs, NN, HI)
        ddt = (da * a_neg + ddt_x) * rowmask
        ddtr = ddt * jax.nn.sigmoid(dtr_ref[...] + bias_ref[...]) * (lane < HEADS).astype(F32)
        ddtr_ref[...] = ddtr.astype(ddtr_ref.dtype)
        ddtb_ref[...] += jnp.sum(ddtr, axis=0, keepdims=True)
        dal_ref[...] += jnp.sum(da * dt, axis=0, keepdims=True) * a_neg
        dd_ref[...] += dd_row

    blk = lambda w: pl.BlockSpec((BLK, w), lambda s: (nb - 1 - s, 0))
    par = lambda a: pl.BlockSpec(a.shape, lambda s: (0, 0))
    acc = lambda w: pl.BlockSpec((1, w), lambda s: (0, 0))
    sel64, sel128 = _head_spread(HEAD_DIM), _head_spread(LANES)
    return pl.pallas_call(
        body, name="ssd_bwd", grid=(nb,),
        in_specs=[blk(SSD_INNER), blk(SSD_INNER), blk(SSD_INNER), blk(XBC), blk(LANES),
                  pl.BlockSpec((None, SSD_STATE, SSD_INNER), lambda s: (nb - 1 - s, 0, 0)),
                  par(dt_bias), par(a_log), par(d_skip), par(norm_g), par(sel64), par(sel128)],
        out_specs=[blk(SSD_INNER), blk(XBC), blk(LANES), acc(SSD_INNER), acc(LANES), acc(LANES), acc(LANES)],
        out_shape=[jax.ShapeDtypeStruct((n_rows, SSD_INNER), BF16), jax.ShapeDtypeStruct((n_rows, XBC), F32),
                   jax.ShapeDtypeStruct((n_rows, LANES), BF16), jax.ShapeDtypeStruct((1, SSD_INNER), F32),
                   jax.ShapeDtypeStruct((1, LANES), F32), jax.ShapeDtypeStruct((1, LANES), F32),
                   jax.ShapeDtypeStruct((1, LANES), F32)],
        scratch_shapes=[pltpu.VMEM((SSD_STATE, SSD_INNER), F32), pltpu.VMEM((BLK, SSD_INNER), F32),
                        pltpu.VMEM((3 * BLK + 2 * HALO, SSD_INNER), F32)],
        compiler_params=_params(12 << 20, ("arbitrary",)),
    )(dy, ypre, z, xbc, dtr, states, dt_bias, a_log, d_skip, norm_g, sel64, sel128)


HEAD_GROUP = 4


LOG2_E = 1.4426950408889634
SOFTPLUS_CLAMP = 80.0


def _sb_logits(zl, valid):
    z2 = zl * LOG2_E
    lost = jnp.maximum(jnp.log2(1.0 + jnp.exp2(jnp.minimum(z2, SOFTPLUS_CLAMP))), z2)
    log_beta = z2 - lost
    if valid is not None:
        lost = jnp.where(valid, lost, 0.0)
    return log_beta, lost


def _tri_and_ones(tri, sign=1.0):
    half = sign * jnp.concatenate([tri, jnp.ones((BLK, BLK), F32)], axis=1)
    return jnp.concatenate([half, half], axis=0).astype(BF16)


def _tile_mask(i, j, ri, ci):
    key = j * BLK + ci
    return (key < i * BLK + ri) & (key >= PAD)


def _pair_lanes(p):
    return slice(p * 2 * HEAD_DIM, (p + 1) * 2 * HEAD_DIM)


def _head_stack(x_pair, first):
    zero = jnp.zeros_like(x_pair)
    return jnp.concatenate([jnp.where(first, x_pair, zero), jnp.where(first, zero, x_pair)], axis=0)


def _block_rows(j):
    return pl.ds(j * BLK if isinstance(j, int) else pl.multiple_of(j * BLK, BLK), BLK)


def _two_terms(x):
    hi = x.astype(BF16)
    return jnp.concatenate([hi, (x - hi.astype(F32)).astype(BF16)], axis=1)


def _ride_along(ex, n_in, n_out, refs):
    k = 0
    parts = []
    for cnt in (n_in, ex.n, n_out, ex.n):
        parts.append(refs[k:k + cnt])
        k += cnt
    n_sems = len(ex.scratch)
    return (*parts, refs[k:len(refs) - n_sems], refs[len(refs) - n_sems:])


def _attn_fwd(q, k, v, ex, ex_arrays):
    n_rows, width = q.shape
    nb = n_rows // BLK
    n_heads = HEAD_GROUP
    gw = n_heads * HEAD_DIM
    groups = width // gw

    def body(*refs):
        (q_ref, k_ref, v_ref), ex_in, (o_ref, tot_ref), ex_out, (run_ref, z_ref, w_ref), sems = _ride_along(ex, 3, 2, refs)
        step = pl.program_id(0)

        @pl.when(step == 0)
        def _():
            ex.start(ex_in, ex_out, sems)

        ri = lax.broadcasted_iota(jnp.int32, (BLK, BLK), 0)
        ci = lax.broadcasted_iota(jnp.int32, (BLK, BLK), 1)
        sums = _tri_and_ones((ri > ci).astype(F32), -1.0)
        first = ci < HEAD_DIM
        heads, pairs = range(n_heads), range(n_heads // 2)

        def stacks(ref, blocks, p):
            return jnp.concatenate([_head_stack(ref[_block_rows(jnp.clip(j, 0, nb - 1)), _pair_lanes(p)], first)
                                    for j in blocks], axis=0)

        def q_block(i, carry):
            rows = _block_rows(i)
            o_ref[rows, :] = jnp.zeros((BLK, gw), F32)
            run_ref[...] = jnp.zeros_like(run_ref)
            w_ref[...] = jnp.zeros_like(w_ref)

            odd = (i + 1) % 2

            def blocks(t):
                return i + odd - 2 * t, i + odd - 2 * t - 1

            for p in pairs:
                z_ref[p] = _dot(q_ref[rows, _pair_lanes(p)], stacks(k_ref, blocks(0), p), NT)

            def tile(t, masked, skip=None):
                q_i, o_i = q_ref[rows, :], o_ref[rows, :]
                outs = [_dot(w_ref[p], stacks(v_ref, blocks(jnp.maximum(t - 1, 0)), p)) for p in pairs]
                z_next = [_dot(q_i[:, _pair_lanes(p)], stacks(k_ref, blocks(t + 1), p), NT) for p in pairs]
                runs = [run_ref[h] for h in heads]
                lgs, res = {}, {}
                for s, j in enumerate(blocks(t)):
                    if s == skip:
                        continue
                    valid = _tile_mask(i, j, ri, ci) if masked else None
                    lgs[s] = [_sb_logits(z_ref[h // 2][:, (2 * s + h % 2) * BLK:(2 * s + h % 2 + 1) * BLK], valid)
                              for h in heads]
                    res[s] = _dot(jnp.concatenate([_two_terms(lgs[s][h][1]) for h in heads], axis=0), sums)
                ws = []
                for s, j in enumerate(blocks(t)):
                    valid = _tile_mask(i, j, ri, ci) if masked else None
                    for h in heads:
                        if s == skip:
                            ws.append(jnp.zeros((BLK, BLK), BF16))
                            continue
                        part = res[s][h * BLK:(h + 1) * BLK]
                        w = jnp.exp2(lgs[s][h][0] + part[:, :BLK] + runs[h])
                        ws.append((jnp.where(valid, w, 0.0) if masked else w).astype(BF16))
                        runs[h] = runs[h] + part[:, BLK:]
                for p in pairs:
                    w_ref[p] = jnp.concatenate([ws[s * n_heads + 2 * p + e] for s in range(2) for e in range(2)], axis=1)
                    z_ref[p] = z_next[p]
                o_ref[rows, :] = o_i + jnp.concatenate(outs, axis=1)
                for h in heads:
                    run_ref[h] = runs[h]

            n_calls = (i + 2) // 2

            @pl.when(odd == 1)
            def _():
                tile(0, True, skip=0)

            @pl.when(odd == 0)
            def _():
                tile(0, True)

            def mid(t, carry):
                tile(t, False)
                return carry

            lax.fori_loop(1, n_calls - 1, mid, 0)

            @pl.when(n_calls >= 2)
            def _():
                tile(n_calls - 1, True)

            o_ref[rows, :] += jnp.concatenate([_dot(w_ref[p], stacks(v_ref, blocks(n_calls - 1), p)) for p in pairs], axis=1)
            for h in heads:
                tot_ref[h // HEAD_GROUP, rows, h % HEAD_GROUP:h % HEAD_GROUP + 1] = run_ref[h][:, 0:1]
            return carry

        lax.fori_loop(0, nb, q_block, 0)

        @pl.when(step == groups - 1)
        def _():
            ex.wait(ex_in, ex_out, sems)

    spec = pl.BlockSpec((n_rows, gw), lambda g: (0, g))
    tot_spec = pl.BlockSpec((n_heads // HEAD_GROUP, n_rows, HEAD_GROUP), lambda g: (g, 0, 0))
    res = pl.pallas_call(
        body, name="attn_fwd", grid=(groups,), in_specs=[spec, spec, spec] + ex.specs,
        out_specs=[spec, tot_spec] + ex.specs,
        out_shape=[jax.ShapeDtypeStruct((n_rows, width), F32),
                   jax.ShapeDtypeStruct((width // (HEAD_GROUP * HEAD_DIM), n_rows, HEAD_GROUP), F32)] + ex.out_shape,
        scratch_shapes=[pltpu.VMEM((n_heads, BLK, BLK), F32), pltpu.VMEM((n_heads // 2, BLK, 4 * BLK), F32),
                        pltpu.VMEM((n_heads // 2, BLK, 4 * BLK), BF16)] + ex.scratch,
        compiler_params=_params(n_rows * (3 * gw * 2 + gw * 4 + LANES * 4), ("arbitrary",)),
    )(q, k, v, *ex_arrays)
    return res[0], res[1], res[2:]


def _attn_bwd(q, k, v, keep_total, do, ex, ex_arrays):
    n_rows, width = q.shape
    nb = n_rows // BLK
    gw = HEAD_GROUP * HEAD_DIM
    groups = width // gw

    def body(*refs):
        ((q_ref, k_ref, v_ref, tot_ref, do_ref), ex_in, (dq_ref, dk_ref, dv_ref), ex_out,
         (dq_acc, dk_acc, dv_acc, tot_b, run_ref, rung_ref, z_ref, dw_ref, dz_ref, wb_ref, qst_ref, dost_ref),
         sems) = _ride_along(ex, 5, 3, refs)
        step = pl.program_id(0)

        @pl.when(step == 0)
        def _():
            ex.start(ex_in, ex_out, sems)

        ri = lax.broadcasted_iota(jnp.int32, (BLK, BLK), 0)
        ci = lax.broadcasted_iota(jnp.int32, (BLK, BLK), 1)
        sums_keep = _tri_and_ones((ri <= ci).astype(F32), -1.0)
        sums_g = _tri_and_ones((ri < ci).astype(F32))
        first = ci < HEAD_DIM
        heads, pairs = range(HEAD_GROUP), range(HEAD_GROUP // 2)
        dk_acc[...] = jnp.zeros_like(dk_acc)
        dv_acc[...] = jnp.zeros_like(dv_acc)

        def clamp(j):
            return jnp.clip(j, 0, nb - 1)

        def stacks(ref, blocks, p):
            return jnp.concatenate([_head_stack(ref[_block_rows(clamp(j)), _pair_lanes(p)], first) for j in blocks], axis=0)

        def blocks(t):
            return 2 * t, 2 * t + 1

        def kept(ref, p, axis):
            tiles = [[ref[s * HEAD_GROUP + 2 * p + e] for e in range(2)] for s in range(2)]
            if axis == 1:
                return jnp.concatenate(tiles[0] + tiles[1], axis=1)
            return jnp.concatenate([jnp.concatenate(tiles[s], axis=0) for s in range(2)], axis=1)

        def owed_dq(t_prev):
            return [_dot(kept(dz_ref, p, 1), stacks(k_ref, blocks(t_prev), p)) for p in pairs]

        def owed_dk():
            return [_dot(kept(dz_ref, p, 0), qst_ref[p], TN) for p in pairs]

        def owed_dv():
            return [_dot(kept(wb_ref, p, 0), dost_ref[p], TN) for p in pairs]

        def settle(t_prev, parts):
            dq, dk, dv = parts
            dq_acc[...] += jnp.concatenate(dq, axis=1)
            for s, j in enumerate(blocks(t_prev)):
                cols = _block_rows(clamp(j))
                dk_acc[cols, :] += jnp.concatenate([d[s * BLK:(s + 1) * BLK] for d in dk], axis=1)
                dv_acc[cols, :] += jnp.concatenate([d[s * BLK:(s + 1) * BLK] for d in dv], axis=1)

        def q_block(i, carry):
            rows = _block_rows(i)
            dq_acc[...] = jnp.zeros_like(dq_acc)
            run_ref[...] = jnp.zeros_like(run_ref)
            rung_ref[...] = jnp.zeros_like(rung_ref)
            dz_ref[...] = jnp.zeros_like(dz_ref)
            wb_ref[...] = jnp.zeros_like(wb_ref)
            for h in heads:
                tot_b[h] = jnp.broadcast_to(tot_ref[rows, h:h + 1], (BLK, BLK))
            for p in pairs:
                qst_ref[p] = _head_stack(q_ref[rows, _pair_lanes(p)], first)
                dost_ref[p] = _head_stack(do_ref[rows, _pair_lanes(p)], first)
                z_ref[p] = _dot(q_ref[rows, _pair_lanes(p)], stacks(k_ref, blocks(0), p), NT)
                dw_ref[p] = _dot(do_ref[rows, _pair_lanes(p)], stacks(v_ref, blocks(0), p), NT)

            def tile(t, masked, skip=None):
                q_i, do_i = q_ref[rows, :], do_ref[rows, :]
                t_prev = jnp.maximum(t - 1, 0)
                valids = [_tile_mask(i, j, ri, ci) if masked else None for j in blocks(t)]
                tile_of = lambda ref, s, h: ref[h // 2][:, (2 * s + h % 2) * BLK:(2 * s + h % 2 + 1) * BLK]
                nothing = [jnp.zeros((BLK, BLK), F32)] * HEAD_GROUP
                runs = [run_ref[h] for h in heads]
                rungs = [rung_ref[h] for h in heads]
                lgs, keep, ws, gs, gsum, dzs = {}, {}, {}, {}, {}, {}
                for s in range(2):
                    if s != skip:
                        lgs[s] = [_sb_logits(tile_of(z_ref, s, h), valids[s]) for h in heads]
                        keep[s] = _dot(jnp.concatenate([_two_terms(lgs[s][h][1]) for h in heads], axis=0), sums_keep)
                    if s == 0:
                        part_dq = owed_dq(t_prev)
                    else:
                        part_dk = owed_dk()
                for s in range(2):
                    ws[s], gs[s] = [], []
                    if s != skip:
                        for h in heads:
                            part = keep[s][h * BLK:(h + 1) * BLK]
                            w = jnp.exp2(lgs[s][h][0] + (tot_b[h] - runs[h] - part[:, :BLK]))
                            ws[s].append(jnp.where(valids[s], w, 0.0) if masked else w)
                            runs[h] = runs[h] + part[:, BLK:]
                            gs[s].append(tile_of(dw_ref, s, h) * ws[s][h])
                        gsum[s] = _dot(jnp.concatenate([_two_terms(gs[s][h]) for h in heads], axis=0), sums_g)
                    else:
                        ws[s] = nothing
                    if s == 0:
                        part_dv = owed_dv()
                    else:
                        z_next = [_dot(q_i[:, _pair_lanes(p)], stacks(k_ref, blocks(t + 1), p), NT) for p in pairs]
                for s in range(2):
                    dzs[s] = []
                    if s != skip:
                        for h in heads:
                            part = gsum[s][h * BLK:(h + 1) * BLK]
                            beta = jnp.exp2(lgs[s][h][0])
                            dz = gs[s][h] * (1.0 - beta) - beta * (part[:, :BLK] + rungs[h])
                            dzs[s].append(jnp.where(valids[s], dz, 0.0) if masked else dz)
                            rungs[h] = rungs[h] + part[:, BLK:]
                    else:
                        dzs[s] = nothing
                    if s == 0:
                        dw_next = [_dot(do_i[:, _pair_lanes(p)], stacks(v_ref, blocks(t + 1), p), NT) for p in pairs]
                settle(t_prev, (part_dq, part_dk, part_dv))
                for s in range(2):
                    for h in heads:
                        dz_ref[s * HEAD_GROUP + h] = dzs[s][h].astype(BF16)
                        wb_ref[s * HEAD_GROUP + h] = ws[s][h].astype(BF16)
                for h in heads:
                    run_ref[h] = runs[h]
                    rung_ref[h] = rungs[h]
                for p in pairs:
                    z_ref[p] = z_next[p]
                    dw_ref[p] = dw_next[p]

            n_calls = (i + 2) // 2
            tile(0, True)

            def mid(t, carry):
                tile(t, False)
                return carry

            lax.fori_loop(1, n_calls - 1, mid, 0)

            @pl.when((n_calls >= 2) & (i % 2 == 1))
            def _():
                tile(n_calls - 1, True)

            @pl.when((n_calls >= 2) & (i % 2 == 0))
            def _():
                tile(n_calls - 1, True, skip=1)

            settle(n_calls - 1, (owed_dq(n_calls - 1), owed_dk(), owed_dv()))
            dq_ref[rows, :] = (dq_acc[...] * SB_SCALE).astype(dq_ref.dtype)
            return carry

        lax.fori_loop(0, nb, q_block, 0)
        dk_ref[...] = dk_acc[...].astype(dk_ref.dtype)
        dv_ref[...] = dv_acc[...].astype(dv_ref.dtype)

        @pl.when(step == groups - 1)
        def _():
            ex.wait(ex_in, ex_out, sems)

    spec = pl.BlockSpec((n_rows, gw), lambda g: (0, g))
    tot_spec = pl.BlockSpec((None, n_rows, HEAD_GROUP), lambda g: (g, 0, 0))
    out = jax.ShapeDtypeStruct((n_rows, width), BF16)
    tile_f32 = pltpu.VMEM((HEAD_GROUP, BLK, BLK), F32)
    tile_bf16 = pltpu.VMEM((2 * HEAD_GROUP, BLK, BLK), BF16)
    pair_f32 = pltpu.VMEM((HEAD_GROUP // 2, BLK, 4 * BLK), F32)
    pair_stack = pltpu.VMEM((HEAD_GROUP // 2, 2 * BLK, BLK), BF16)
    res = pl.pallas_call(
        body, name="attn_bwd", grid=(groups,), in_specs=[spec, spec, spec, tot_spec, spec] + ex.specs,
        out_specs=[spec] * 3 + ex.specs, out_shape=[out] * 3 + ex.out_shape,
        scratch_shapes=[pltpu.VMEM((BLK, gw), F32), pltpu.VMEM((n_rows, gw), F32), pltpu.VMEM((n_rows, gw), F32),
                        tile_f32, tile_f32, tile_f32, pair_f32, pair_f32, tile_bf16, tile_bf16, pair_stack,
                        pair_stack] + ex.scratch,
        compiler_params=_params(n_rows * (7 * gw * 2 + LANES * 4 + gw * 4), ("arbitrary",)),
    )(q, k, v, keep_total, do, *ex_arrays)
    return res[0], res[1], res[2], res[3:]


class _Exchange:
    def __init__(self, arrays, gather):
        self.n = len(arrays)
        self.gather = gather
        self.out_shape = [jax.ShapeDtypeStruct(((N_DEV,) + a.shape) if gather else a.shape, a.dtype) for a in arrays]
        self.scratch = [pltpu.SemaphoreType.DMA((self.n, N_DEV - 1)), pltpu.SemaphoreType.DMA((self.n, N_DEV - 1)),
                        pltpu.SemaphoreType.DMA((self.n,))]
        self.specs = [pl.BlockSpec(memory_space=pl.ANY)] * self.n

    def _copies(self, ins, outs, sems, with_receives):
        send_sems, recv_sems, local_sems = sems
        x, y, c = lax.axis_index("x"), lax.axis_index("y"), lax.axis_index("c")
        me = 4 * x + 2 * y + c
        gather = self.gather
        local, sends, recvs = [], [], []
        for a in range(self.n):
            local.append(pltpu.make_async_copy(ins[a] if gather else ins[a].at[me], outs[a].at[me if gather else 0],
                                               local_sems.at[a]))
        for r in range(1, N_DEV):
            px = 1 - x if r & 4 else x
            py = 1 - y if r & 2 else y
            pc = 1 - c if r & 1 else c
            idx = 4 * px + 2 * py + pc
            for a in range(self.n):
                src = ins[a] if gather else ins[a].at[idx]
                pair = dict(send_sem=send_sems.at[a, r - 1], recv_sem=recv_sems.at[a, r - 1],
                            device_id=(px, py, pc), device_id_type=pl.DeviceIdType.MESH)
                sends.append(pltpu.make_async_remote_copy(src_ref=src, dst_ref=outs[a].at[me if gather else r], **pair))
                if with_receives:
                    recvs.append(pltpu.make_async_remote_copy(src_ref=src, dst_ref=outs[a].at[idx if gather else r], **pair))
        return local, sends, recvs

    def start(self, ins, outs, sems):
        local, sends, _ = self._copies(ins, outs, sems, with_receives=False)
        for cp in local + sends:
            cp.start()

    def wait(self, ins, outs, sems):
        local, sends, recvs = self._copies(ins, outs, sems, with_receives=True)
        for cp in recvs:
            cp.wait_recv()
        for cp in sends:
            cp.wait_send()
        for cp in local:
            cp.wait()


def _peer_exchange(name, parts):
    exs = [_Exchange(arrays, gather) for arrays, gather in parts]
    n = sum(ex.n for ex in exs)
    n_sems = len(exs[0].scratch)

    def body(*refs):
        at, views = 0, []
        for k, ex in enumerate(exs):
            views.append((ex, refs[at:at + ex.n], refs[n + at:n + at + ex.n],
                          refs[2 * n + k * n_sems:2 * n + (k + 1) * n_sems]))
            at += ex.n
        for ex, ins, outs, sems in views:
            ex.start(ins, outs, sems)
        for ex, ins, outs, sems in views:
            ex.wait(ins, outs, sems)

    res = pl.pallas_call(
        body, name=name, in_specs=[s for ex in exs for s in ex.specs], out_specs=[s for ex in exs for s in ex.specs],
        out_shape=[s for ex in exs for s in ex.out_shape], scratch_shapes=[s for ex in exs for s in ex.scratch],
    )(*[a for arrays, _ in parts for a in arrays])
    out, at = [], 0
    for ex in exs:
        out.append(res[at:at + ex.n])
        at += ex.n
    return out


def _gather_two_level(name, arrays):
    n = len(arrays)

    def body(*refs):
        ins, outs = refs[:n], refs[n:2 * n]
        send_sems, recv_sems, local_sems = refs[2 * n:]
        x, y, c = lax.axis_index("x"), lax.axis_index("y"), lax.axis_index("c")
        sibling = (x, y, 1 - c)
        chips = [(1 - x, y), (x, 1 - y), (1 - x, 1 - y)]

        def slot(a, dev):
            return outs[a].at[4 * dev[0] + 2 * dev[1] + dev[2]]

        def copy(a, k, block, to, src=None):
            return pltpu.make_async_remote_copy(
                src_ref=slot(a, block) if src is None else src, dst_ref=slot(a, block),
                send_sem=send_sems.at[a, k], recv_sem=recv_sems.at[a, k], device_id=to, device_id_type=pl.DeviceIdType.MESH)

        me = (x, y, c)
        mine = [pltpu.make_async_copy(ins[a], slot(a, me), local_sems.at[a]) for a in range(n)]
        first = [copy(a, 0, me, sibling, src=ins[a]) for a in range(n)]
        first += [copy(a, 1 + j, me, (*chip, c), src=ins[a]) for j, chip in enumerate(chips) for a in range(n)]
        for cp in mine + first:
            cp.start()
        passed = []
        for j, chip in enumerate(chips):
            for a in range(n):
                copy(a, 1 + j, (*chip, c), me).wait_recv()
            for a in range(n):
                cp = copy(a, 4 + j, (*chip, c), sibling)
                cp.start()
                passed.append(cp)
        for a in range(n):
            copy(a, 0, sibling, me).wait_recv()
            for j, chip in enumerate(chips):
                copy(a, 4 + j, (*chip, 1 - c), me).wait_recv()
        for cp in first + passed:
            cp.wait_send()
        for cp in mine:
            cp.wait()

    any_spec = pl.BlockSpec(memory_space=pl.ANY)
    return pl.pallas_call(
        body, name=name, in_specs=[any_spec] * n, out_specs=[any_spec] * n,
        out_shape=[jax.ShapeDtypeStruct((N_DEV,) + a.shape, a.dtype) for a in arrays],
        scratch_shapes=[pltpu.SemaphoreType.DMA((n, N_DEV - 1)), pltpu.SemaphoreType.DMA((n, N_DEV - 1)),
                        pltpu.SemaphoreType.DMA((n,))],
    )(*arrays)


def _sum_slots(name, x):
    def body(x_ref, o_ref):
        acc = x_ref[0]
        for s in range(1, N_DEV):
            acc = acc + x_ref[s]
        o_ref[...] = acc
    return pl.pallas_call(body, name=name, out_shape=jax.ShapeDtypeStruct(x.shape[1:], F32))(x)


def _adamw(name, w, slots, m, v):
    n_slots, rows, cols = slots.shape
    tr = next((t for t in (256, 176, 128) if rows % t == 0 and rows > t), rows)

    def body(w_ref, s_ref, m_ref, v_ref, g_ref, d_ref, nm_ref, nv_ref):
        g = s_ref[0].astype(F32)
        for s in range(1, n_slots):
            g = g + s_ref[s].astype(F32)
        nm = ADAM_B1 * m_ref[...] + (1.0 - ADAM_B1) * g
        nv = ADAM_B2 * v_ref[...] + (1.0 - ADAM_B2) * (g * g)
        m_hat = nm / (1.0 - ADAM_B1 ** ADAM_STEP)
        v_hat = nv / (1.0 - ADAM_B2 ** ADAM_STEP)
        g_ref[...] = g
        d_ref[...] = -ADAM_LR * (m_hat / (jnp.sqrt(v_hat) + ADAM_EPS) + ADAM_WD * w_ref[...])
        nm_ref[...] = nm
        nv_ref[...] = nv

    spec = pl.BlockSpec((tr, cols), lambda i: (i, 0))
    out = jax.ShapeDtypeStruct((rows, cols), F32)
    return pl.pallas_call(
        body, name=name, grid=(rows // tr,),
        in_specs=[spec, pl.BlockSpec((n_slots, tr, cols), lambda i: (0, i, 0)), spec, spec],
        out_specs=[spec] * 4, out_shape=[out] * 4,
        compiler_params=_params((n_slots + 7) * tr * cols * 4, ("parallel",)),
    )(w, slots, m, v)


def _pad_lanes(a):
    return jnp.pad(a, ((0, 0), (0, LANES - a.shape[1])))


def kernel(x, meta_tokens, mix_pre_g, w_in, ssd_conv_w, ssd_conv_b, ssd_dt_bias, ssd_a_log, ssd_d, ssd_norm_g, sb_norm_g, w_out, mix_post_g, ffn_pre_g, w_up, ffn_conv_w, ffn_conv_b, w_down, ffn_post_g, loss_target, m_meta_tokens, m_mix_pre_g, m_w_in, m_ssd_conv_w, m_ssd_conv_b, m_ssd_dt_bias, m_ssd_a_log, m_ssd_d, m_ssd_norm_g, m_sb_norm_g, m_w_out, m_mix_post_g, m_ffn_pre_g, m_w_up, m_ffn_conv_w, m_ffn_conv_b, m_w_down, m_ffn_post_g, v_meta_tokens, v_mix_pre_g, v_w_in, v_ssd_conv_w, v_ssd_conv_b, v_ssd_dt_bias, v_ssd_a_log, v_ssd_d, v_ssd_norm_g, v_sb_norm_g, v_w_out, v_mix_post_g, v_ffn_pre_g, v_w_up, v_ffn_conv_w, v_ffn_conv_b, v_w_down, v_ffn_post_g):
    seq = x.shape[1]
    me = 4 * lax.axis_index("x") + 2 * lax.axis_index("y") + lax.axis_index("c")
    in_cols = w_in.shape[2]
    up_cols = w_up.shape[2]
    out_rows = w_out.shape[1]
    down_rows = w_down.shape[1]

    g_in, g_meta, g_scw, g_fcw = _gather_two_level(
        "gather_w_in", [w_in[0].astype(BF16), meta_tokens, ssd_conv_w[0], ffn_conv_w[0]])
    late_weights = [w_out[0].astype(BF16), w_up[0].astype(BF16), w_down[0].astype(BF16)]
    w_in_full = g_in.transpose(1, 0, 2).reshape(D_MODEL, N_DEV * in_cols)
    off = [0, SSD_INNER, SSD_INNER + XBC, SSD_INNER + XBC + HEADS]
    w_z = w_in_full[:, off[0]:off[1]]
    w_xbc = w_in_full[:, off[1]:off[2]]
    w_dt = _pad_lanes(w_in_full[:, off[2]:off[3]])
    w_q = w_in_full[:, off[3]:off[3] + SSD_INNER]
    w_k = w_in_full[:, off[3] + SSD_INNER:off[3] + 2 * SSD_INNER]
    w_v = w_in_full[:, off[3] + 2 * SSD_INNER:off[3] + 3 * SSD_INNER]
    meta_full = g_meta.transpose(1, 0, 2).reshape(N_META, D_MODEL)
    scw_full = g_scw.transpose(1, 0, 2).reshape(SSD_CONV, XBC)
    fcw_full = g_fcw.transpose(1, 0, 2).reshape(FFN_CONV, D_FF)

    dt_bias_p, a_log_p, d_p = _pad_lanes(ssd_dt_bias), _pad_lanes(ssd_a_log), _pad_lanes(ssd_d)

    h0 = jnp.concatenate([jnp.zeros((PAD, D_MODEL), F32), meta_full, x[0]], axis=0)
    target = loss_target[0]
    xn1 = _rms_fwd("rms_pre_mix", h0, mix_pre_g)
    z = _mm("proj_z", [(xn1, w_z)], "nn", F32)
    xbc_raw = _mm("proj_xbc", [(xn1, w_xbc)], "nn", F32)
    dtr = _mm("proj_dt", [(xn1, w_dt)], "nn", F32)
    q = _mm("proj_q", [(xn1, w_q * SB_SCALE)], "nn", BF16)
    k = _mm("proj_k", [(xn1, w_k)], "nn", BF16)
    v = _mm("proj_v", [(xn1, w_v)], "nn", BF16)
    xbc_act = _ssd_conv_fwd(xbc_raw, scw_full, ssd_conv_b)
    ypre, y_ssd, states = _ssd_fwd(xbc_act, dtr, z, dt_bias_p, a_log_p, d_p, ssd_norm_g)
    o, keep_total, (g_out, g_up, g_down) = _attn_fwd(q, k, v, _Exchange(late_weights, gather=True), late_weights)
    w_out_full = g_out.reshape(N_DEV * out_rows, D_MODEL)
    wo_ssd, wo_sb = w_out_full[:SSD_INNER], w_out_full[SSD_INNER:]
    w_up_full = g_up.transpose(1, 0, 2).reshape(D_MODEL, N_DEV * up_cols)
    w_gate, w_lin = w_up_full[:, :D_FF], w_up_full[:, D_FF:]
    w_down_full = g_down.reshape(N_DEV * down_rows, D_MODEL)
    y_sb = _rms_fwd("rms_sb", o, sb_norm_g)
    mix = _mm("mix_out", [(y_ssd, wo_ssd), (y_sb, wo_sb)], "nn", F32)
    h1, xn2 = _mix_post(mix, h0, mix_post_g, ffn_pre_g)
    g_raw = _mm("ffn_gate", [(xn2, w_gate)], "nn", F32)
    u = _mm("ffn_lin", [(xn2, w_lin)], "nn", F32)
    act = _ffn_act(g_raw, u, fcw_full, ffn_conv_b)
    f = _mm("ffn_down", [(act, w_down_full)], "nn", F32)
    dh2, df, loss_row, dg_ffn_post = _loss_post(f, h1, target, ffn_post_g)

    dact = _mm("d_act", [(df, w_down_full)], "nt", F32)
    dw_down = _mm("dw_down", [(act, df)], "tn", F32)
    dg_conv, du = _ffn_bwd_act(dact, u, g_raw, fcw_full, ffn_conv_b)
    dg_raw, dfcw0, dfcw1, dfcw2, dfcb = _conv_bwd("ffn_conv_bwd", dg_conv, g_raw, fcw_full, FFN_CONV)
    dxn2 = _mm("d_xn2", [(dg_raw, w_gate), (du, w_lin)], "nt", F32)
    dw_gate = _mm("dw_gate", [(xn2, dg_raw)], "tn", F32)
    dw_lin = _mm("dw_lin", [(xn2, du)], "tn", F32)
    dh1, dmix, dg_ffn_pre, dg_mix_post = _mid_bwd(dxn2, h1, dh2, mix, ffn_pre_g, mix_post_g)

    dy_ssd = _mm("d_yssd", [(dmix, wo_ssd)], "nt", F32)
    dy_sb = _mm("d_ysb", [(dmix, wo_sb)], "nt", F32)
    dwo_ssd = _mm("dw_out_ssd", [(y_ssd, dmix)], "tn", F32)
    dwo_sb = _mm("dw_out_sb", [(y_sb, dmix)], "tn", F32)
    do, dg_sb = _norm_bwd("sb_norm_bwd", dy_sb, o, sb_norm_g)
    half = N_DEV // 2
    early_slabs = [
        jnp.concatenate([dwo_ssd, dwo_sb], axis=0).reshape(N_DEV, out_rows, D_MODEL),
        jnp.concatenate([dw_gate.reshape(D_MODEL, half, up_cols).transpose(1, 0, 2),
                         dw_lin.reshape(D_MODEL, half, up_cols).transpose(1, 0, 2)], axis=0),
        dw_down.reshape(N_DEV, down_rows, D_MODEL)]
    dq, dk, dv, (l_out, l_up, l_down) = _attn_bwd(q, k, v, keep_total, do, _Exchange(early_slabs, gather=False), early_slabs)
    dz, dxbc_act, ddtr, dg_ssd_norm, dd_skip, da_log, ddt_bias = _ssd_bwd(
        dy_ssd, ypre, z, xbc_act, dtr, states, dt_bias_p, a_log_p, d_p, ssd_norm_g)
    dconv = _ssd_conv_bwd_act(dxbc_act, xbc_raw, scw_full, ssd_conv_b)
    dxbc_raw, dscw0, dscw1, dscw2, dscw3, dscb = _conv_bwd("ssd_conv_bwd", dconv, xbc_raw, scw_full, SSD_CONV)
    segs = [(dz, w_z), (dxbc_raw, w_xbc), (ddtr, w_dt), (dq, w_q), (dk, w_k), (dv, w_v)]
    dw_segs = [_mm("dw_in_%d" % s, [(xn1, d)], "tn", BF16) for s, (d, _) in enumerate(segs)]
    dw_segs[2] = dw_segs[2][:, :HEADS]
    dw_in = jnp.concatenate(dw_segs, axis=1)
    slab_in = dw_in.reshape(D_MODEL, N_DEV, in_cols).transpose(1, 0, 2)
    dxn1, (l_in,) = _mm("d_xn1", segs, "nt", F32, _Exchange([slab_in], gather=False), [slab_in])
    dh0, dg_mix_pre = _first_bwd(dxn1, h0, dh1, mix_pre_g)
    grad_x = dh0[BLK:][None]

    small = [dg_mix_pre, dscb, ddt_bias, da_log, dd_skip, dg_ssd_norm, dg_sb, dg_mix_post, dg_ffn_pre, dfcb,
             dg_ffn_post, dh0[PAD:BLK].reshape(1, -1), dscw0, dscw1, dscw2, dscw3, dfcw0, dfcw1, dfcw2, loss_row]
    sizes = [a.shape[1] for a in small]
    total = sum(sizes)
    rows_packed = -(-total // (LANES * HALO)) * HALO
    packed = jnp.pad(jnp.concatenate(small, axis=1), ((0, 0), (0, rows_packed * LANES - total)))
    ((gathered,),) = _peer_exchange("gather_small_grads", [([packed.reshape(rows_packed, LANES)], True)])
    summed = _sum_slots("sum_small_grads", gathered).reshape(1, rows_packed * LANES)
    pieces, at = [], 0
    for s in sizes:
        pieces.append(summed[:, at:at + s])
        at += s
    (g_mix_pre, g_scb, g_dtb, g_alog, g_dskip, g_ssd_norm, g_sb, g_mix_post, g_ffn_pre, g_fcb, g_ffn_post,
     g_meta_flat, gs0, gs1, gs2, gs3, gf0, gf1, gf2, loss_all) = pieces
    loss = loss_all[0, 0]
    g_dtb, g_alog, g_dskip = g_dtb[:, :HEADS], g_alog[:, :HEADS], g_dskip[:, :HEADS]
    g_meta_full = g_meta_flat.reshape(N_META, D_MODEL)
    g_scw_full = jnp.concatenate([gs0, gs1, gs2, gs3], axis=0)
    g_fcw_full = jnp.concatenate([gf0, gf1, gf2], axis=0)
    meta_cols, scw_cols, fcw_cols = meta_tokens.shape[1], ssd_conv_w.shape[2], ffn_conv_w.shape[2]
    g_meta_mine = lax.dynamic_slice(g_meta_full, (0, me * meta_cols), (N_META, meta_cols))
    g_scw_mine = lax.dynamic_slice(g_scw_full, (0, me * scw_cols), (SSD_CONV, scw_cols))
    g_fcw_mine = lax.dynamic_slice(g_fcw_full, (0, me * fcw_cols), (FFN_CONV, fcw_cols))

    def lead(a):
        return a[None]

    upd = [
        _adamw("adamw_meta", meta_tokens, lead(g_meta_mine), m_meta_tokens, v_meta_tokens),
        _adamw("adamw_mix_pre_g", mix_pre_g, lead(g_mix_pre), m_mix_pre_g, v_mix_pre_g),
        [lead(a) for a in _adamw("adamw_w_in", w_in[0], l_in, m_w_in[0], v_w_in[0])],
        [lead(a) for a in _adamw("adamw_ssd_conv_w", ssd_conv_w[0], lead(g_scw_mine), m_ssd_conv_w[0], v_ssd_conv_w[0])],
        _adamw("adamw_ssd_conv_b", ssd_conv_b, lead(g_scb), m_ssd_conv_b, v_ssd_conv_b),
        _adamw("adamw_ssd_dt_bias", ssd_dt_bias, lead(g_dtb), m_ssd_dt_bias, v_ssd_dt_bias),
        _adamw("adamw_ssd_a_log", ssd_a_log, lead(g_alog), m_ssd_a_log, v_ssd_a_log),
        _adamw("adamw_ssd_d", ssd_d, lead(g_dskip), m_ssd_d, v_ssd_d),
        _adamw("adamw_ssd_norm_g", ssd_norm_g, lead(g_ssd_norm), m_ssd_norm_g, v_ssd_norm_g),
        _adamw("adamw_sb_norm_g", sb_norm_g, lead(g_sb), m_sb_norm_g, v_sb_norm_g),
        [lead(a) for a in _adamw("adamw_w_out", w_out[0], l_out, m_w_out[0], v_w_out[0])],
        _adamw("adamw_mix_post_g", mix_post_g, lead(g_mix_post), m_mix_post_g, v_mix_post_g),
        _adamw("adamw_ffn_pre_g", ffn_pre_g, lead(g_ffn_pre), m_ffn_pre_g, v_ffn_pre_g),
        [lead(a) for a in _adamw("adamw_w_up", w_up[0], l_up, m_w_up[0], v_w_up[0])],
        [lead(a) for a in _adamw("adamw_ffn_conv_w", ffn_conv_w[0], lead(g_fcw_mine), m_ffn_conv_w[0], v_ffn_conv_w[0])],
        _adamw("adamw_ffn_conv_b", ffn_conv_b, lead(g_fcb), m_ffn_conv_b, v_ffn_conv_b),
        [lead(a) for a in _adamw("adamw_w_down", w_down[0], l_down, m_w_down[0], v_w_down[0])],
        _adamw("adamw_ffn_post_g", ffn_post_g, lead(g_ffn_post), m_ffn_post_g, v_ffn_post_g),
    ]
    grads = [u_[0] for u_ in upd]
    deltas = [u_[1] for u_ in upd]
    new_m = [u_[2] for u_ in upd]
    new_v = [u_[3] for u_ in upd]
    return (loss, grad_x, *grads, *deltas, *new_m, *new_v)
```

```python
import math

import jax
import jax.numpy as jnp
from jax import lax
from jax.experimental import pallas as pl
from jax.experimental.pallas import tpu as pltpu

F32 = jnp.float32
BF16 = jnp.bfloat16
HI = lax.Precision.HIGHEST

D_MODEL = 1024
N_META = 16
BLK = 128
PAD = BLK - N_META
HEADS = 16
HEAD_DIM = 64
SSD_GROUPS = 2
SSD_STATE = 128
HEADS_PER_GROUP = HEADS // SSD_GROUPS
SSD_INNER = HEADS * HEAD_DIM
SSD_CONV = 4
XBC = SSD_INNER + 2 * SSD_GROUPS * SSD_STATE
OFF_B = SSD_INNER
OFF_C = SSD_INNER + SSD_GROUPS * SSD_STATE
D_FF = 2816
FFN_CONV = 3
EPS = 1e-6
SB_SCALE = 1.0 / math.sqrt(HEAD_DIM)
N_DEV = 8
LANES = 128
HALO = 8

ADAM_LR = 0.001
ADAM_B1 = 0.9
ADAM_B2 = 0.999
ADAM_EPS = 1e-08
ADAM_WD = 0.01
ADAM_STEP = 10

VMEM_FLOOR = 32 << 20
VMEM_CEIL = 60 << 20
MM_BUDGET = 20 << 20
ROW_BUDGET = 6 << 20

NN = (((1,), (0,)), ((), ()))
NT = (((1,), (1,)), ((), ()))
TN = (((0,), (0,)), ((), ()))


def _params(tile_bytes, sem=None):
    limit = int(min(max(2 * tile_bytes + (8 << 20), VMEM_FLOOR), VMEM_CEIL))
    return pltpu.CompilerParams(vmem_limit_bytes=limit, dimension_semantics=sem)


def _nbytes(shape, dtype):
    n = 1
    for s in shape:
        n *= s
    return n * jnp.dtype(dtype).itemsize


def _dot(a, b, dims=NN, precision=None):
    return lax.dot_general(a, b, dims, precision=precision, preferred_element_type=F32)


def _softplus(x):
    return jnp.maximum(x, 0.0) + jnp.log1p(jnp.exp(-jnp.abs(x)))


def _rms(x, g):
    r = lax.rsqrt(jnp.mean(x * x, axis=-1, keepdims=True) + EPS)
    return x * r * g


def _rms_bwd(dy, x, g):
    r = lax.rsqrt(jnp.mean(x * x, axis=-1, keepdims=True) + EPS)
    xh = x * r
    u = dy * g
    dx = r * (u - xh * jnp.mean(xh * u, axis=-1, keepdims=True))
    return dx, jnp.sum(dy * xh, axis=0, keepdims=True)


def _gelu(x):
    c = math.sqrt(2.0 / math.pi)
    return 0.5 * x * (1.0 + jnp.tanh(c * (x + 0.044715 * x * x * x)))


def _gelu_and_grad(x):
    c = math.sqrt(2.0 / math.pi)
    x2 = x * x
    t = jnp.tanh(c * (x + 0.044715 * x2 * x))
    half = 0.5 * (1.0 + t)
    return x * half, half + 0.5 * x * (1.0 - t * t) * c * (1.0 + 3.0 * 0.044715 * x2)


def _row_tile(rows, bytes_per_row):
    big = 384
    return big if rows % big == 0 and big * bytes_per_row <= ROW_BUDGET else BLK


def _mm(name, pairs, mode, out_dtype, ex=None, ex_arrays=()):
    a0, b0 = pairs[0]
    if mode == "tn":
        m, n = a0.shape[1], b0.shape[1]
    elif mode == "nt":
        m, n = a0.shape[0], b0.shape[0]
    else:
        m, n = a0.shape[0], b0.shape[1]
    dims = {"nn": NN, "nt": NT, "tn": TN}[mode]

    def tile_bytes(tm, tn):
        tot = tm * tn * jnp.dtype(out_dtype).itemsize
        for a, b in pairs:
            k = a.shape[0] if mode == "tn" else a.shape[1]
            tot += tm * k * a.dtype.itemsize + tn * k * b.dtype.itemsize
        return tot

    cands_m = [t for t in (1408, 1024, 512, 384, 256, 128) if m % t == 0] or [m]
    cands_n = [t for t in (1408, 1024, 768, 512, 256, 128) if n % t == 0] or [n]
    best = None
    for tm in cands_m:
        for tn in cands_n:
            if tile_bytes(tm, tn) <= MM_BUDGET and (best is None or tm * tn > best[0] * best[1]):
                best = (tm, tn)
    tm, tn = best if best is not None else (cands_m[-1], cands_n[-1])
    npairs = len(pairs)

    grid = (m // tm, n // tn)

    def body(*refs):
        if ex is None:
            ins, o_ref = refs[:2 * npairs], refs[2 * npairs]
        else:
            ins, ex_in, (o_ref,), ex_out, _, sems = _ride_along(ex, 2 * npairs, 1, refs)
            step = pl.program_id(0) * grid[1] + pl.program_id(1)

            @pl.when(step == 0)
            def _():
                ex.start(ex_in, ex_out, sems)

        acc = None
        for p in range(npairs):
            part = _dot(ins[2 * p][...], ins[2 * p + 1][...], dims)
            acc = part if acc is None else acc + part
        o_ref[...] = acc.astype(o_ref.dtype)
        if ex is not None:
            @pl.when(step == grid[0] * grid[1] - 1)
            def _():
                ex.wait(ex_in, ex_out, sems)

    in_specs, args = [], []
    for a, b in pairs:
        if mode == "tn":
            k = a.shape[0]
            in_specs += [pl.BlockSpec((k, tm), lambda i, j: (0, i)), pl.BlockSpec((k, tn), lambda i, j: (0, j))]
        elif mode == "nt":
            k = a.shape[1]
            in_specs += [pl.BlockSpec((tm, k), lambda i, j: (i, 0)), pl.BlockSpec((tn, k), lambda i, j: (j, 0))]
        else:
            k = a.shape[1]
            in_specs += [pl.BlockSpec((tm, k), lambda i, j: (i, 0)), pl.BlockSpec((k, tn), lambda i, j: (0, j))]
        args += [a, b]
    out_spec = pl.BlockSpec((tm, tn), lambda i, j: (i, j))
    out_shape = jax.ShapeDtypeStruct((m, n), out_dtype)
    if ex is None:
        return pl.pallas_call(
            body, name=name, grid=grid, in_specs=in_specs, out_specs=out_spec, out_shape=out_shape,
            compiler_params=_params(tile_bytes(tm, tn), ("parallel", "parallel")),
        )(*args)
    res = pl.pallas_call(
        body, name=name, grid=grid, in_specs=in_specs + ex.specs, out_specs=[out_spec] + ex.specs,
        out_shape=[out_shape] + ex.out_shape, scratch_shapes=ex.scratch,
        compiler_params=_params(tile_bytes(tm, tn), ("arbitrary", "arbitrary")),
    )(*args, *ex_arrays)
    return res[0], res[1:]


class _Window:
    def __init__(self, ref, cols):
        self.ref, self.cols = ref, cols

    def __getitem__(self, idx):
        return self.ref[:, self.cols] if idx is Ellipsis else self.ref[idx[0], self.cols]


def _rowcall(name, fn, rows=(), prevs=(), nexts=(), pars=(), out_rows=(), out_accs=(), chunk=None, late=()):
    rows, prevs, nexts, pars, late = list(rows), list(prevs), list(nexts), list(pars), list(late)
    n_rows = (rows + prevs + nexts)[0].shape[0]
    per_row = sum(a.shape[1] * a.dtype.itemsize for a in rows + prevs + nexts + late)
    per_row += sum(c * jnp.dtype(dt).itemsize for c, dt in out_rows) + sum(a.shape[1] * 4 for a in prevs + nexts)
    tm = _row_tile(n_rows, per_row)
    nt = n_rows // tm
    hb = tm // HALO
    if late:
        tm = BLK
        nt, hb = n_rows // tm, tm // HALO
    rows = rows + late
    nr, npv, nnx, npar, nor, noa = len(rows), len(prevs), len(nexts), len(pars), len(out_rows), len(out_accs)

    def body(*refs):
        i = pl.program_id(0)
        k = 0
        row_refs = refs[k:k + nr]; k += nr
        pc = refs[k:k + npv]; k += npv
        ph = refs[k:k + npv]; k += npv
        nc = refs[k:k + nnx]; k += nnx
        nh = refs[k:k + nnx]; k += nnx
        par_refs = refs[k:k + npar]; k += npar
        orow = refs[k:k + nor]; k += nor
        oacc = refs[k:k + noa]; k += noa
        pscr = refs[k:k + npv]; k += npv
        nscr = refs[k:k + nnx]
        for c_, h_, s_ in zip(pc, ph, pscr):
            s_[0:HALO, :] = h_[...] * (i > 0).astype(F32)
            s_[HALO:HALO + tm, :] = c_[...]
        for c_, h_, s_ in zip(nc, nh, nscr):
            s_[0:tm, :] = c_[...]
            s_[tm:tm + HALO, :] = h_[...] * (i < nt - 1).astype(F32)
        if noa:
            @pl.when(i == 0)
            def _():
                for r_ in oacc:
                    r_[...] = jnp.zeros_like(r_)
        width = out_rows[0][0]
        windows = [slice(None)] if chunk is None else [slice(c0, c0 + chunk) for c0 in range(0, width, chunk)]
        for cols in windows:
            prev_fns = [(lambda s, s_=s_, cols=cols: s_[pl.ds(HALO, tm), cols] if s == 0 else
                         pltpu.roll(s_[:, cols], s, 0)[HALO:HALO + tm]) for s_ in pscr]
            next_fns = [(lambda s, s_=s_, cols=cols: s_[pl.ds(0, tm), cols] if s == 0 else
                         pltpu.roll(s_[:, cols], tm + HALO - s, 0)[0:tm]) for s_ in nscr]
            row_vals, acc_vals = fn(i, tm, [_Window(r_, cols) for r_ in row_refs], prev_fns, next_fns,
                                    [_Window(r_, cols) for r_ in par_refs])
            for r_, v in zip(orow, row_vals):
                r_[:, cols] = v.astype(r_.dtype)
            for r_, v in zip(oacc, acc_vals):
                r_[:, cols] += v

    def row_spec(a):
        return pl.BlockSpec((tm, a.shape[1]), lambda i: (i, 0))

    def whole(shape):
        return pl.BlockSpec(shape, lambda i: (0, 0))

    in_specs = [row_spec(a) for a in rows[:nr - len(late)]]
    in_specs += [pl.BlockSpec((tm, a.shape[1]), lambda i: (jnp.maximum(i - 1, 0), 0)) for a in late]
    in_specs += [row_spec(a) for a in prevs]
    in_specs += [pl.BlockSpec((HALO, a.shape[1]), lambda i: (jnp.maximum(i * hb - 1, 0), 0)) for a in prevs]
    in_specs += [row_spec(a) for a in nexts]
    in_specs += [pl.BlockSpec((HALO, a.shape[1]), lambda i: (jnp.minimum((i + 1) * hb, n_rows // HALO - 1), 0)) for a in nexts]
    in_specs += [whole(p.shape) for p in pars]
    out_specs = [pl.BlockSpec((tm, c), lambda i: (i, 0)) for c, _ in out_rows] + [whole(s) for s in out_accs]
    out_shape = [jax.ShapeDtypeStruct((n_rows, c), dt) for c, dt in out_rows]
    out_shape += [jax.ShapeDtypeStruct(s, F32) for s in out_accs]
    scratch = [pltpu.VMEM((tm + HALO, a.shape[1]), F32) for a in prevs + nexts]
    tile = tm * per_row
    res = pl.pallas_call(
        body, name=name, grid=(nt,), in_specs=in_specs, out_specs=out_specs, out_shape=out_shape,
        scratch_shapes=scratch, compiler_params=_params(2 * tile, ("arbitrary",)),
    )(*rows, *prevs, *prevs, *nexts, *nexts, *pars)
    return res


def _row_ids(i, tm):
    return i * tm + lax.broadcasted_iota(jnp.int32, (tm, 1), 0)


def _conv_taps(prev_fn, w_ref, b_ref, taps):
    acc = b_ref[...]
    for k in range(taps):
        acc = acc + prev_fn(taps - 1 - k) * w_ref[k:k + 1, :]
    return acc


def _rms_fwd(name, h, g):
    def fn(i, tm, rows, prevs, nexts, pars):
        return [_rms(rows[0][...], pars[0][...])], []
    return _rowcall(name, fn, rows=[h], pars=[g], out_rows=[(h.shape[1], BF16)])[0]


def _ssd_conv_fwd(xbc_raw, w, b):
    def fn(i, tm, rows, prevs, nexts, pars):
        c = _conv_taps(prevs[0], pars[0], pars[1], SSD_CONV)
        y = c * jax.nn.sigmoid(c)
        return [jnp.where(_row_ids(i, tm) >= PAD, y, 0.0)], []
    return _rowcall("ssd_conv_fwd", fn, prevs=[xbc_raw], pars=[w, b], out_rows=[(XBC, F32)], chunk=LANES)[0]


def _mix_post(mix, h0, g_post, g_pre):
    def fn(i, tm, rows, prevs, nexts, pars):
        h1 = rows[1][...] + _rms(rows[0][...], pars[0][...])
        return [h1, _rms(h1, pars[1][...])], []
    return _rowcall("mix_post", fn, rows=[mix, h0], pars=[g_post, g_pre],
                    out_rows=[(D_MODEL, F32), (D_MODEL, BF16)])


def _ffn_act(g_raw, u, w, b):
    def fn(i, tm, rows, prevs, nexts, pars):
        g = _conv_taps(prevs[0], pars[0], pars[1], FFN_CONV)
        return [_gelu(g) * rows[0][...]], []
    return _rowcall("ffn_act", fn, rows=[u], prevs=[g_raw], pars=[w, b], out_rows=[(D_FF, BF16)], chunk=LANES)[0]


def _loss_post(f, h1, target, g_post):
    def fn(i, tm, rows, prevs, nexts, pars):
        fv, g = rows[0][...], pars[0][...]
        h2 = rows[1][...] + _rms(fv, g)
        real = _row_ids(i, tm) >= BLK
        diff = jnp.where(real, h2 - rows[2][...], 0.0)
        loss = 0.5 * jnp.sum(jnp.mean(diff * diff, axis=-1, keepdims=True))
        dh2 = diff * (1.0 / D_MODEL)
        df, dg = _rms_bwd(dh2, fv, g)
        return [dh2, df], [jnp.zeros((1, LANES), F32) + loss, dg]
    return _rowcall("loss_post", fn, rows=[f, h1], late=[target], pars=[g_post],
                    out_rows=[(D_MODEL, F32), (D_MODEL, BF16)], out_accs=[(1, LANES), (1, D_MODEL)])


def _ffn_bwd_act(dact, u, g_raw, w, b):
    def fn(i, tm, rows, prevs, nexts, pars):
        g = _conv_taps(prevs[0], pars[0], pars[1], FFN_CONV)
        da = rows[0][...]
        gelu, grad = _gelu_and_grad(g)
        return [da * rows[1][...] * grad, da * gelu], []
    return _rowcall("ffn_bwd_act", fn, rows=[dact, u], prevs=[g_raw], pars=[w, b],
                    out_rows=[(D_FF, F32), (D_FF, BF16)], chunk=LANES)


def _conv_bwd(name, dy, x, w, taps):
    width = x.shape[1]

    def fn(i, tm, rows, prevs, nexts, pars):
        dy0 = nexts[0](0)
        dx = None
        for k in range(taps):
            term = nexts[0](taps - 1 - k) * pars[0][k:k + 1, :]
            dx = term if dx is None else dx + term
        dws = [jnp.sum(dy0 * prevs[0](taps - 1 - k), axis=0, keepdims=True) for k in range(taps)]
        return [dx], dws + [jnp.sum(dy0, axis=0, keepdims=True)]
    return _rowcall(name, fn, prevs=[x], nexts=[dy], pars=[w], out_rows=[(width, BF16)],
                    out_accs=[(1, width)] * (taps + 1), chunk=LANES)


def _mid_bwd(dxn2, h1, dh2, mix, g_pre, g_post):
    def fn(i, tm, rows, prevs, nexts, pars):
        d1, dg_pre = _rms_bwd(rows[0][...], rows[1][...], pars[0][...])
        dh1 = rows[2][...] + d1
        dmix, dg_post = _rms_bwd(dh1, rows[3][...], pars[1][...])
        return [dh1, dmix], [dg_pre, dg_post]
    return _rowcall("mid_bwd", fn, rows=[dxn2, h1, dh2, mix], pars=[g_pre, g_post],
                    out_rows=[(D_MODEL, F32), (D_MODEL, BF16)], out_accs=[(1, D_MODEL), (1, D_MODEL)])


def _norm_bwd(name, dy, x, g):
    def fn(i, tm, rows, prevs, nexts, pars):
        dx, dg = _rms_bwd(rows[0][...], rows[1][...], pars[0][...])
        return [dx], [dg]
    return _rowcall(name, fn, rows=[dy, x], pars=[g], out_rows=[(x.shape[1], BF16)], out_accs=[(1, x.shape[1])])


def _ssd_conv_bwd_act(dact, xbc_raw, w, b):
    def fn(i, tm, rows, prevs, nexts, pars):
        c = _conv_taps(prevs[0], pars[0], pars[1], SSD_CONV)
        s = jax.nn.sigmoid(c)
        dc = rows[0][...] * s * (1.0 + c * (1.0 - s))
        return [jnp.where(_row_ids(i, tm) >= PAD, dc, 0.0)], []
    return _rowcall("ssd_conv_bwd_act", fn, rows=[dact], prevs=[xbc_raw], pars=[w, b], out_rows=[(XBC, F32)],
                    chunk=LANES)[0]


def _first_bwd(dxn1, h0, dh1, g):
    def fn(i, tm, rows, prevs, nexts, pars):
        d0, dg = _rms_bwd(rows[0][...], rows[1][...], pars[0][...])
        return [rows[2][...] + d0], [dg]
    return _rowcall("first_bwd", fn, rows=[dxn1, h0, dh1], pars=[g], out_rows=[(D_MODEL, F32)],
                    out_accs=[(1, D_MODEL)])


def _ssd_chunk_terms(c, dtr_ref, bias_ref, alog_ref):
    ri = lax.broadcasted_iota(jnp.int32, (BLK, BLK), 0)
    ci = lax.broadcasted_iota(jnp.int32, (BLK, BLK), 1)
    causal = ri >= ci
    tril = causal.astype(F32)
    triu = (ri <= ci).astype(F32)
    rowmask = ((c * BLK + lax.broadcasted_iota(jnp.int32, (BLK, 1), 0)) >= PAD).astype(F32)
    dt = _softplus(dtr_ref[...] + bias_ref[...]) * rowmask
    a_neg = -jnp.exp(alog_ref[...])
    a = dt * a_neg
    cs = _dot(tril, a, NN, HI)
    cs_t = _dot(a, triu, TN, HI)
    return causal, triu, rowmask, dt, a_neg, cs, cs_t


def _decay_matrix(causal, cs_h, cs_t_h):
    return jnp.where(causal, jnp.exp(jnp.where(causal, cs_h - cs_t_h, 0.0)), 0.0)


def _head_spread(width):
    ri = lax.broadcasted_iota(jnp.int32, (LANES, HEADS * width), 0)
    ci = lax.broadcasted_iota(jnp.int32, (LANES, HEADS * width), 1)
    return ((ri * width <= ci) & (ci < (ri + 1) * width)).astype(BF16)


def _three_terms(x):
    hi = x.astype(BF16)
    rest = x - hi.astype(F32)
    mid = rest.astype(BF16)
    lo = (rest - mid.astype(F32)).astype(BF16)
    return jnp.concatenate([hi, mid, lo], axis=1)


def _spread(x, sel):
    return _dot(_three_terms(x), jnp.concatenate([sel, sel, sel], axis=0))


def _lane_sums(y, sel):
    return _dot(_three_terms(y), jnp.concatenate([sel, sel, sel], axis=1), NT)


def _ssd_spreads(dt, cs, d_ref, sel64, sel128):
    dt64 = _spread(dt, sel64)
    cs64 = _spread(cs, sel64)
    cs128 = _spread(cs, sel128)
    d64 = _spread(jnp.broadcast_to(d_ref[...], (HALO, LANES)), sel64)[0:1, :]
    cl64 = cs64[BLK - 1:BLK, :]
    return dt64, cs128, d64, jnp.exp(cs64), jnp.exp(cl64 - cs64), jnp.exp(cl64)


def _ssd_pair_terms(p, xbc_ref, dt64, cs128, cs_t, causal):
    lanes = _pair_lanes(p)
    xs = xbc_ref[:, lanes]
    decays = [_decay_matrix(causal, cs128[:, h * LANES:(h + 1) * LANES], cs_t[h:h + 1, :]) for h in (2 * p, 2 * p + 1)]
    return xs, xs * dt64[:, lanes], decays


def _ssd_fwd(xbc, dtr, z, dt_bias, a_log, d_skip, norm_g):
    n_rows = xbc.shape[0]
    nb = n_rows // BLK

    def body(xbc_ref, dtr_ref, z_ref, bias_ref, alog_ref, d_ref, g_ref, s64_ref, s128_ref,
             ypre_ref, yssd_ref, st_ref, state):
        c = pl.program_id(0)

        @pl.when(c == 0)
        def _():
            state[...] = jnp.zeros_like(state)

        st_ref[...] = state[...]
        causal, _, _, dt, _, cs, cs_t = _ssd_chunk_terms(c, dtr_ref, bias_ref, alog_ref)
        first = lax.broadcasted_iota(jnp.int32, (BLK, LANES), 1) < HEAD_DIM
        dt64, cs128, d64, from_start, to_end, chunk_decay = _ssd_spreads(dt, cs, d_ref, s64_ref[...], s128_ref[...])
        for g in range(SSD_GROUPS):
            b_b = xbc_ref[:, OFF_B + g * SSD_STATE:OFF_B + (g + 1) * SSD_STATE].astype(BF16)
            c_b = xbc_ref[:, OFF_C + g * SSD_STATE:OFF_C + (g + 1) * SSD_STATE].astype(BF16)
            cb = _dot(c_b, b_b, NT)
            for j in range(HEADS_PER_GROUP // 2):
                p = g * (HEADS_PER_GROUP // 2) + j
                lanes = _pair_lanes(p)
                xs, x_dt, decays = _ssd_pair_terms(p, xbc_ref, dt64, cs128, cs_t, causal)
                ms = [(cb * d).astype(BF16) for d in decays]
                s_p = state[:, lanes]
                y = _dot(jnp.concatenate(ms, axis=1), _head_stack(x_dt.astype(BF16), first))
                y = y + from_start[:, lanes] * _dot(c_b, s_p.astype(BF16))
                state[:, lanes] = chunk_decay[:, lanes] * s_p + _dot(b_b, (to_end[:, lanes] * x_dt).astype(BF16), TN)
                ypre_ref[:, lanes] = y + d64[:, lanes] * xs
        zz = z_ref[...]
        yg = ypre_ref[...] * (zz * jax.nn.sigmoid(zz))
        yssd_ref[...] = _rms(yg, g_ref[...]).astype(yssd_ref.dtype)

    blk = lambda w: pl.BlockSpec((BLK, w), lambda c: (c, 0))
    par = lambda a: pl.BlockSpec(a.shape, lambda c: (0, 0))
    sel64, sel128 = _head_spread(HEAD_DIM), _head_spread(LANES)
    return pl.pallas_call(
        body, name="ssd_fwd", grid=(nb,),
        in_specs=[blk(XBC), blk(LANES), blk(SSD_INNER), par(dt_bias), par(a_log), par(d_skip), par(norm_g),
                  par(sel64), par(sel128)],
        out_specs=[blk(SSD_INNER), blk(SSD_INNER), pl.BlockSpec((None, SSD_STATE, SSD_INNER), lambda c: (c, 0, 0))],
        out_shape=[jax.ShapeDtypeStruct((n_rows, SSD_INNER), F32), jax.ShapeDtypeStruct((n_rows, SSD_INNER), BF16),
                   jax.ShapeDtypeStruct((nb, SSD_STATE, SSD_INNER), F32)],
        scratch_shapes=[pltpu.VMEM((SSD_STATE, SSD_INNER), F32)],
        compiler_params=_params(8 << 20, ("arbitrary",)),
    )(xbc, dtr, z, dt_bias, a_log, d_skip, norm_g, sel64, sel128)


def _ssd_bwd(dy, ypre, z, xbc, dtr, states, dt_bias, a_log, d_skip, norm_g):
    n_rows = xbc.shape[0]
    nb = n_rows // BLK

    def body(dy_ref, ypre_ref, z_ref, xbc_ref, dtr_ref, st_ref, bias_ref, alog_ref, d_ref, g_ref, s64_ref, s128_ref,
             dz_ref, dxbc_ref, ddtr_ref, dgn_ref, dd_ref, dal_ref, ddtb_ref, dstate, dyp, red):
        step = pl.program_id(0)
        c = nb - 1 - step

        @pl.when(step == 0)
        def _():
            dstate[...] = jnp.zeros_like(dstate)
            for r_ in (dgn_ref, dd_ref, dal_ref, ddtb_ref):
                r_[...] = jnp.zeros_like(r_)

        yp, zz = ypre_ref[...], z_ref[...]
        sz = jax.nn.sigmoid(zz)
        silu = zz * sz
        dyg, dgn = _rms_bwd(dy_ref[...], yp * silu, g_ref[...])
        dgn_ref[...] += dgn
        dz_ref[...] = (dyg * yp * (sz * (1.0 + zz * (1.0 - sz)))).astype(dz_ref.dtype)
        dyp[...] = dyg * silu

        causal, triu, rowmask, dt, a_neg, cs, cs_t = _ssd_chunk_terms(c, dtr_ref, bias_ref, alog_ref)
        lane = lax.broadcasted_iota(jnp.int32, (1, LANES), 1)
        last_row = (lax.broadcasted_iota(jnp.int32, (BLK, 1), 0) == BLK - 1).astype(F32)
        first = lax.broadcasted_iota(jnp.int32, (BLK, LANES), 1) < HEAD_DIM
        sel64 = s64_ref[...]
        dt64, cs128, d64, from_start, to_end, chunk_decay = _ssd_spreads(dt, cs, d_ref, sel64, s128_ref[...])
        for g in range(SSD_GROUPS):
            b_b = xbc_ref[:, OFF_B + g * SSD_STATE:OFF_B + (g + 1) * SSD_STATE].astype(BF16)
            c_b = xbc_ref[:, OFF_C + g * SSD_STATE:OFF_C + (g + 1) * SSD_STATE].astype(BF16)
            cb = _dot(c_b, b_b, NT)
            b_twice = jnp.concatenate([b_b, b_b], axis=0)
            c_twice = jnp.concatenate([c_b, c_b], axis=0)
            db_g = jnp.zeros((BLK, SSD_STATE), F32)
            dc_g = jnp.zeros((BLK, SSD_STATE), F32)
            for j in range(HEADS_PER_GROUP // 2):
                p = g * (HEADS_PER_GROUP // 2) + j
                lanes = _pair_lanes(p)
                xs, x_dt, decays = _ssd_pair_terms(p, xbc_ref, dt64, cs128, cs_t, causal)
                ms = [(cb * d).astype(BF16) for d in decays]
                d_y = dyp[:, lanes]
                s_p, ds_p = st_ref[:, lanes], dstate[:, lanes]
                s_b, ds_b = s_p.astype(BF16), ds_p.astype(BF16)
                fs, te = from_start[:, lanes], to_end[:, lanes]
                x_b, dy_b = x_dt.astype(BF16), d_y.astype(BF16)
                x_st = _head_stack(x_b, first)
                fs_dy = (fs * d_y).astype(BF16)

                y_diag = _dot(jnp.concatenate(ms, axis=1), x_st)
                y_off = fs * _dot(c_b, s_b)
                end_part = te * _dot(b_b, ds_b)
                dx_diag = _dot(jnp.concatenate(ms, axis=0), _head_stack(dy_b, first), TN)
                d_x = dx_diag + end_part
                g2 = _dot(dy_b, x_st, NT)
                gl = [(g2[:, k * BLK:(k + 1) * BLK] * decays[k]).astype(BF16) for k in range(2)]
                dc_g = dc_g + _dot(jnp.concatenate(gl, axis=1), b_twice) + _dot(fs_dy, s_b, NT)
                db_g = db_g + _dot(jnp.concatenate(gl, axis=0), c_twice, TN) + _dot((te * x_dt).astype(BF16), ds_b, NT)
                red[0:BLK, lanes] = (dy_b.astype(F32) * y_diag - x_b.astype(F32) * dx_diag) + (d_y * y_off - x_dt * end_part)
                red[BLK:2 * BLK, lanes] = x_dt * end_part
                red[2 * BLK:3 * BLK, lanes] = d_x * xs
                red[3 * BLK:3 * BLK + HALO, lanes] = jnp.broadcast_to(jnp.sum(ds_p * s_p, axis=0, keepdims=True), (HALO, LANES))
                red[3 * BLK + HALO:3 * BLK + 2 * HALO, lanes] = jnp.broadcast_to(
                    jnp.sum(d_y * xs, axis=0, keepdims=True), (HALO, LANES))
                dxbc_ref[:, lanes] = d_x * dt64[:, lanes] + d64[:, lanes] * d_y
                dstate[:, lanes] = chunk_decay[:, lanes] * ds_p + _dot(c_b, fs_dy, TN)
            dxbc_ref[:, OFF_B + g * SSD_STATE:OFF_B + (g + 1) * SSD_STATE] = db_g
            dxbc_ref[:, OFF_C + g * SSD_STATE:OFF_C + (g + 1) * SSD_STATE] = dc_g
        sums = _lane_sums(red[...], sel64)
        at_last = (jnp.sum(sums[BLK:2 * BLK], axis=0, keepdims=True)
                   + jnp.exp(cs[BLK - 1:BLK, :]) * sums[3 * BLK:3 * BLK + 1])
        dcs = sums[0:BLK] + last_row * at_last
        ddt_x = sums[2 * BLK:3 * BLK]
        dd_row = sums[3 * BLK + HALO:3 * BLK + HALO + 1]
        da = _dot(triu, dcs, NN, HI)
        ddt = (da * a_neg + ddt_x) * rowmask
        ddtr = ddt * jax.nn.sigmoid(dtr_ref[...] + bias_ref[...]) * (lane < HEADS).astype(F32)
        ddtr_ref[...] = ddtr.astype(ddtr_ref.dtype)
        ddtb_ref[...] += jnp.sum(ddtr, axis=0, keepdims=True)
        dal_ref[...] += jnp.sum(da * dt, axis=0, keepdims=True) * a_neg
        dd_ref[...] += dd_row

    blk = lambda w: pl.BlockSpec((BLK, w), lambda s: (nb - 1 - s, 0))
    par = lambda a: pl.BlockSpec(a.shape, lambda s: (0, 0))
    acc = lambda w: pl.BlockSpec((1, w), lambda s: (0, 0))
    sel64, sel128 = _head_spread(HEAD_DIM), _head_spread(LANES)
    return pl.pallas_call(
        body, name="ssd_bwd", grid=(nb,),
        in_specs=[blk(SSD_INNER), blk(SSD_INNER), blk(SSD_INNER), blk(XBC), blk(LANES),
                  pl.BlockSpec((None, SSD_STATE, SSD_INNER), lambda s: (nb - 1 - s, 0, 0)),
                  par(dt_bias), par(a_log), par(d_skip), par(norm_g), par(sel64), par(sel128)],
        out_specs=[blk(SSD_INNER), blk(XBC), blk(LANES), acc(SSD_INNER), acc(LANES), acc(LANES), acc(LANES)],
        out_shape=[jax.ShapeDtypeStruct((n_rows, SSD_INNER), BF16), jax.ShapeDtypeStruct((n_rows, XBC), F32),
                   jax.ShapeDtypeStruct((n_rows, LANES), BF16), jax.ShapeDtypeStruct((1, SSD_INNER), F32),
                   jax.ShapeDtypeStruct((1, LANES), F32), jax.ShapeDtypeStruct((1, LANES), F32),
                   jax.ShapeDtypeStruct((1, LANES), F32)],
        scratch_shapes=[pltpu.VMEM((SSD_STATE, SSD_INNER), F32), pltpu.VMEM((BLK, SSD_INNER), F32),
                        pltpu.VMEM((3 * BLK + 2 * HALO, SSD_INNER), F32)],
        compiler_params=_params(12 << 20, ("arbitrary",)),
    )(dy, ypre, z, xbc, dtr, states, dt_bias, a_log, d_skip, norm_g, sel64, sel128)


HEAD_GROUP = 4


LOG2_E = 1.4426950408889634
SOFTPLUS_CLAMP = 80.0


def _sb_logits(zl, valid):
    z2 = zl * LOG2_E
    lost = jnp.maximum(jnp.log2(1.0 + jnp.exp2(jnp.minimum(z2, SOFTPLUS_CLAMP))), z2)
    log_beta = z2 - lost
    if valid is not None:
        lost = jnp.where(valid, lost, 0.0)
    return log_beta, lost


def _tri_and_ones(tri, sign=1.0):
    half = sign * jnp.concatenate([tri, jnp.ones((BLK, BLK), F32)], axis=1)
    return jnp.concatenate([half, half], axis=0).astype(BF16)


def _tile_mask(i, j, ri, ci):
    key = j * BLK + ci
    return (key < i * BLK + ri) & (key >= PAD)


def _pair_lanes(p):
    return slice(p * 2 * HEAD_DIM, (p + 1) * 2 * HEAD_DIM)


def _head_stack(x_pair, first):
    zero = jnp.zeros_like(x_pair)
    return jnp.concatenate([jnp.where(first, x_pair, zero), jnp.where(first, zero, x_pair)], axis=0)


def _block_rows(j):
    return pl.ds(j * BLK if isinstance(j, int) else pl.multiple_of(j * BLK, BLK), BLK)


def _two_terms(x):
    hi = x.astype(BF16)
    return jnp.concatenate([hi, (x - hi.astype(F32)).astype(BF16)], axis=1)


def _ride_along(ex, n_in, n_out, refs):
    k = 0
    parts = []
    for cnt in (n_in, ex.n, n_out, ex.n):
        parts.append(refs[k:k + cnt])
        k += cnt
    n_sems = len(ex.scratch)
    return (*parts, refs[k:len(refs) - n_sems], refs[len(refs) - n_sems:])


def _attn_fwd(q, k, v, ex, ex_arrays):
    n_rows, width = q.shape
    nb = n_rows // BLK
    n_heads = HEAD_GROUP
    gw = n_heads * HEAD_DIM
    groups = width // gw

    def body(*refs):
        (q_ref, k_ref, v_ref), ex_in, (o_ref, tot_ref), ex_out, (run_ref, z_ref, w_ref), sems = _ride_along(ex, 3, 2, refs)
        step = pl.program_id(0)

        @pl.when(step == 0)
        def _():
            ex.start(ex_in, ex_out, sems)

        ri = lax.broadcasted_iota(jnp.int32, (BLK, BLK), 0)
        ci = lax.broadcasted_iota(jnp.int32, (BLK, BLK), 1)
        sums = _tri_and_ones((ri > ci).astype(F32), -1.0)
        first = ci < HEAD_DIM
        heads, pairs = range(n_heads), range(n_heads // 2)

        def stacks(ref, blocks, p):
            return jnp.concatenate([_head_stack(ref[_block_rows(jnp.clip(j, 0, nb - 1)), _pair_lanes(p)], first)
                                    for j in blocks], axis=0)

        def q_block(i, carry):
            rows = _block_rows(i)
            o_ref[rows, :] = jnp.zeros((BLK, gw), F32)
            run_ref[...] = jnp.zeros_like(run_ref)
            w_ref[...] = jnp.zeros_like(w_ref)

            odd = (i + 1) % 2

            def blocks(t):
                return i + odd - 2 * t, i + odd - 2 * t - 1

            for p in pairs:
                z_ref[p] = _dot(q_ref[rows, _pair_lanes(p)], stacks(k_ref, blocks(0), p), NT)

            def tile(t, masked, skip=None):
                q_i, o_i = q_ref[rows, :], o_ref[rows, :]
                outs = [_dot(w_ref[p], stacks(v_ref, blocks(jnp.maximum(t - 1, 0)), p)) for p in pairs]
                z_next = [_dot(q_i[:, _pair_lanes(p)], stacks(k_ref, blocks(t + 1), p), NT) for p in pairs]
                runs = [run_ref[h] for h in heads]
                lgs, res = {}, {}
                for s, j in enumerate(blocks(t)):
                    if s == skip:
                        continue
                    valid = _tile_mask(i, j, ri, ci) if masked else None
                    lgs[s] = [_sb_logits(z_ref[h // 2][:, (2 * s + h % 2) * BLK:(2 * s + h % 2 + 1) * BLK], valid)
                              for h in heads]
                    res[s] = _dot(jnp.concatenate([_two_terms(lgs[s][h][1]) for h in heads], axis=0), sums)
                ws = []
                for s, j in enumerate(blocks(t)):
                    valid = _tile_mask(i, j, ri, ci) if masked else None
                    for h in heads:
                        if s == skip:
                            ws.append(jnp.zeros((BLK, BLK), BF16))
                            continue
                        part = res[s][h * BLK:(h + 1) * BLK]
                        w = jnp.exp2(lgs[s][h][0] + part[:, :BLK] + runs[h])
                        ws.append((jnp.where(valid, w, 0.0) if masked else w).astype(BF16))
                        runs[h] = runs[h] + part[:, BLK:]
                for p in pairs:
                    w_ref[p] = jnp.concatenate([ws[s * n_heads + 2 * p + e] for s in range(2) for e in range(2)], axis=1)
                    z_ref[p] = z_next[p]
                o_ref[rows, :] = o_i + jnp.concatenate(outs, axis=1)
                for h in heads:
                    run_ref[h] = runs[h]

            n_calls = (i + 2) // 2

            @pl.when(odd == 1)
            def _():
                tile(0, True, skip=0)

            @pl.when(odd == 0)
            def _():
                tile(0, True)

            def mid(t, carry):
                tile(t, False)
                return carry

            lax.fori_loop(1, n_calls - 1, mid, 0)

            @pl.when(n_calls >= 2)
            def _():
                tile(n_calls - 1, True)

            o_ref[rows, :] += jnp.concatenate([_dot(w_ref[p], stacks(v_ref, blocks(n_calls - 1), p)) for p in pairs], axis=1)
            for h in heads:
                tot_ref[h // HEAD_GROUP, rows, h % HEAD_GROUP:h % HEAD_GROUP + 1] = run_ref[h][:, 0:1]
            return carry

        lax.fori_loop(0, nb, q_block, 0)

        @pl.when(step == groups - 1)
        def _():
            ex.wait(ex_in, ex_out, sems)

    spec = pl.BlockSpec((n_rows, gw), lambda g: (0, g))
    tot_spec = pl.BlockSpec((n_heads // HEAD_GROUP, n_rows, HEAD_GROUP), lambda g: (g, 0, 0))
    res = pl.pallas_call(
        body, name="attn_fwd", grid=(groups,), in_specs=[spec, spec, spec] + ex.specs,
        out_specs=[spec, tot_spec] + ex.specs,
        out_shape=[jax.ShapeDtypeStruct((n_rows, width), F32),
                   jax.ShapeDtypeStruct((width // (HEAD_GROUP * HEAD_DIM), n_rows, HEAD_GROUP), F32)] + ex.out_shape,
        scratch_shapes=[pltpu.VMEM((n_heads, BLK, BLK), F32), pltpu.VMEM((n_heads // 2, BLK, 4 * BLK), F32),
                        pltpu.VMEM((n_heads // 2, BLK, 4 * BLK), BF16)] + ex.scratch,
        compiler_params=_params(n_rows * (3 * gw * 2 + gw * 4 + LANES * 4), ("arbitrary",)),
    )(q, k, v, *ex_arrays)
    return res[0], res[1], res[2:]


def _attn_bwd(q, k, v, keep_total, do, ex, ex_arrays):
    n_rows, width = q.shape
    nb = n_rows // BLK
    gw = HEAD_GROUP * HEAD_DIM
    groups = width // gw

    def body(*refs):
        ((q_ref, k_ref, v_ref, tot_ref, do_ref), ex_in, (dq_ref, dk_ref, dv_ref), ex_out,
         (dq_acc, dk_acc, dv_acc, tot_b, run_ref, rung_ref, z_ref, dw_ref, dz_ref, wb_ref, qst_ref, dost_ref),
         sems) = _ride_along(ex, 5, 3, refs)
        step = pl.program_id(0)

        @pl.when(step == 0)
        def _():
            ex.start(ex_in, ex_out, sems)

        ri = lax.broadcasted_iota(jnp.int32, (BLK, BLK), 0)
        ci = lax.broadcasted_iota(jnp.int32, (BLK, BLK), 1)
        sums_keep = _tri_and_ones((ri <= ci).astype(F32), -1.0)
        sums_g = _tri_and_ones((ri < ci).astype(F32))
        first = ci < HEAD_DIM
        heads, pairs = range(HEAD_GROUP), range(HEAD_GROUP // 2)
        dk_acc[...] = jnp.zeros_like(dk_acc)
        dv_acc[...] = jnp.zeros_like(dv_acc)

        def clamp(j):
            return jnp.clip(j, 0, nb - 1)

        def stacks(ref, blocks, p):
            return jnp.concatenate([_head_stack(ref[_block_rows(clamp(j)), _pair_lanes(p)], first) for j in blocks], axis=0)

        def blocks(t):
            return 2 * t, 2 * t + 1

        def kept(ref, p, axis):
            tiles = [[ref[s * HEAD_GROUP + 2 * p + e] for e in range(2)] for s in range(2)]
            if axis == 1:
                return jnp.concatenate(tiles[0] + tiles[1], axis=1)
            return jnp.concatenate([jnp.concatenate(tiles[s], axis=0) for s in range(2)], axis=1)

        def owed_dq(t_prev):
            return [_dot(kept(dz_ref, p, 1), stacks(k_ref, blocks(t_prev), p)) for p in pairs]

        def owed_dk():
            return [_dot(kept(dz_ref, p, 0), qst_ref[p], TN) for p in pairs]

        def owed_dv():
            return [_dot(kept(wb_ref, p, 0), dost_ref[p], TN) for p in pairs]

        def settle(t_prev, parts):
            dq, dk, dv = parts
            dq_acc[...] += jnp.concatenate(dq, axis=1)
            for s, j in enumerate(blocks(t_prev)):
                cols = _block_rows(clamp(j))
                dk_acc[cols, :] += jnp.concatenate([d[s * BLK:(s + 1) * BLK] for d in dk], axis=1)
                dv_acc[cols, :] += jnp.concatenate([d[s * BLK:(s + 1) * BLK] for d in dv], axis=1)

        def q_block(i, carry):
            rows = _block_rows(i)
            dq_acc[...] = jnp.zeros_like(dq_acc)
            run_ref[...] = jnp.zeros_like(run_ref)
            rung_ref[...] = jnp.zeros_like(rung_ref)
            dz_ref[...] = jnp.zeros_like(dz_ref)
            wb_ref[...] = jnp.zeros_like(wb_ref)
            for h in heads:
                tot_b[h] = jnp.broadcast_to(tot_ref[rows, h:h + 1], (BLK, BLK))
            for p in pairs:
                qst_ref[p] = _head_stack(q_ref[rows, _pair_lanes(p)], first)
                dost_ref[p] = _head_stack(do_ref[rows, _pair_lanes(p)], first)
                z_ref[p] = _dot(q_ref[rows, _pair_lanes(p)], stacks(k_ref, blocks(0), p), NT)
                dw_ref[p] = _dot(do_ref[rows, _pair_lanes(p)], stacks(v_ref, blocks(0), p), NT)

            def tile(t, masked, skip=None):
                q_i, do_i = q_ref[rows, :], do_ref[rows, :]
                t_prev = jnp.maximum(t - 1, 0)
                valids = [_tile_mask(i, j, ri, ci) if masked else None for j in blocks(t)]
                tile_of = lambda ref, s, h: ref[h // 2][:, (2 * s + h % 2) * BLK:(2 * s + h % 2 + 1) * BLK]
                nothing = [jnp.zeros((BLK, BLK), F32)] * HEAD_GROUP
                runs = [run_ref[h] for h in heads]
                rungs = [rung_ref[h] for h in heads]
                lgs, keep, ws, gs, gsum, dzs = {}, {}, {}, {}, {}, {}
                for s in range(2):
                    if s != skip:
                        lgs[s] = [_sb_logits(tile_of(z_ref, s, h), valids[s]) for h in heads]
                        keep[s] = _dot(jnp.concatenate([_two_terms(lgs[s][h][1]) for h in heads], axis=0), sums_keep)
                    if s == 0:
                        part_dq = owed_dq(t_prev)
                    else:
                        part_dk = owed_dk()
                for s in range(2):
                    ws[s], gs[s] = [], []
                    if s != skip:
                        for h in heads:
                            part = keep[s][h * BLK:(h + 1) * BLK]
                            w = jnp.exp2(lgs[s][h][0] + (tot_b[h] - runs[h] - part[:, :BLK]))
                            ws[s].append(jnp.where(valids[s], w, 0.0) if masked else w)
                            runs[h] = runs[h] + part[:, BLK:]
                            gs[s].append(tile_of(dw_ref, s, h) * ws[s][h])
                        gsum[s] = _dot(jnp.concatenate([_two_terms(gs[s][h]) for h in heads], axis=0), sums_g)
                    else:
                        ws[s] = nothing
                    if s == 0:
                        part_dv = owed_dv()
                    else:
                        z_next = [_dot(q_i[:, _pair_lanes(p)], stacks(k_ref, blocks(t + 1), p), NT) for p in pairs]
                for s in range(2):
                    dzs[s] = []
                    if s != skip:
                        for h in heads:
                            part = gsum[s][h * BLK:(h + 1) * BLK]
                            beta = jnp.exp2(lgs[s][h][0])
                            dz = gs[s][h] * (1.0 - beta) - beta * (part[:, :BLK] + rungs[h])
                            dzs[s].append(jnp.where(valids[s], dz, 0.0) if masked else dz)
                            rungs[h] = rungs[h] + part[:, BLK:]
                    else:
                        dzs[s] = nothing
                    if s == 0:
                        dw_next = [_dot(do_i[:, _pair_lanes(p)], stacks(v_ref, blocks(t + 1), p), NT) for p in pairs]
                settle(t_prev, (part_dq, part_dk, part_dv))
                for s in range(2):
                    for h in heads:
                        dz_ref[s * HEAD_GROUP + h] = dzs[s][h].astype(BF16)
                        wb_ref[s * HEAD_GROUP + h] = ws[s][h].astype(BF16)
                for h in heads:
                    run_ref[h] = runs[h]
                    rung_ref[h] = rungs[h]
                for p in pairs:
                    z_ref[p] = z_next[p]
                    dw_ref[p] = dw_next[p]

            n_calls = (i + 2) // 2
            tile(0, True)

            def mid(t, carry):
                tile(t, False)
                return carry

            lax.fori_loop(1, n_calls - 1, mid, 0)

            @pl.when((n_calls >= 2) & (i % 2 == 1))
            def _():
                tile(n_calls - 1, True)

            @pl.when((n_calls >= 2) & (i % 2 == 0))
            def _():
                tile(n_calls - 1, True, skip=1)

            settle(n_calls - 1, (owed_dq(n_calls - 1), owed_dk(), owed_dv()))
            dq_ref[rows, :] = (dq_acc[...] * SB_SCALE).astype(dq_ref.dtype)
            return carry

        lax.fori_loop(0, nb, q_block, 0)
        dk_ref[...] = dk_acc[...].astype(dk_ref.dtype)
        dv_ref[...] = dv_acc[...].astype(dv_ref.dtype)

        @pl.when(step == groups - 1)
        def _():
            ex.wait(ex_in, ex_out, sems)

    spec = pl.BlockSpec((n_rows, gw), lambda g: (0, g))
    tot_spec = pl.BlockSpec((None, n_rows, HEAD_GROUP), lambda g: (g, 0, 0))
    out = jax.ShapeDtypeStruct((n_rows, width), BF16)
    tile_f32 = pltpu.VMEM((HEAD_GROUP, BLK, BLK), F32)
    tile_bf16 = pltpu.VMEM((2 * HEAD_GROUP, BLK, BLK), BF16)
    pair_f32 = pltpu.VMEM((HEAD_GROUP // 2, BLK, 4 * BLK), F32)
    pair_stack = pltpu.VMEM((HEAD_GROUP // 2, 2 * BLK, BLK), BF16)
    res = pl.pallas_call(
        body, name="attn_bwd", grid=(groups,), in_specs=[spec, spec, spec, tot_spec, spec] + ex.specs,
        out_specs=[spec] * 3 + ex.specs, out_shape=[out] * 3 + ex.out_shape,
        scratch_shapes=[pltpu.VMEM((BLK, gw), F32), pltpu.VMEM((n_rows, gw), F32), pltpu.VMEM((n_rows, gw), F32),
                        tile_f32, tile_f32, tile_f32, pair_f32, pair_f32, tile_bf16, tile_bf16, pair_stack,
                        pair_stack] + ex.scratch,
        compiler_params=_params(n_rows * (7 * gw * 2 + LANES * 4 + gw * 4), ("arbitrary",)),
    )(q, k, v, keep_total, do, *ex_arrays)
    return res[0], res[1], res[2], res[3:]


class _Exchange:
    def __init__(self, arrays, gather, same_core=False):
        self.n = len(arrays)
        self.gather = gather
        self.same_core = same_core
        self.out_shape = [jax.ShapeDtypeStruct(((N_DEV,) + a.shape) if gather else a.shape, a.dtype) for a in arrays]
        self.scratch = [pltpu.SemaphoreType.DMA((self.n, N_DEV - 1)), pltpu.SemaphoreType.DMA((self.n, N_DEV - 1)),
                        pltpu.SemaphoreType.DMA((self.n,))]
        self.specs = [pl.BlockSpec(memory_space=pl.ANY)] * self.n

    def _copies(self, ins, outs, sems, with_receives):
        send_sems, recv_sems, local_sems = sems
        x, y, c = lax.axis_index("x"), lax.axis_index("y"), lax.axis_index("c")
        gather, same_core = self.gather, self.same_core
        me = 2 * x + y if same_core else 4 * x + 2 * y + c
        local, sends, recvs = [], [], []
        for a in range(self.n):
            local.append(pltpu.make_async_copy(ins[a] if gather else ins[a].at[me], outs[a].at[me if gather else 0],
                                               local_sems.at[a]))
        for r in ((2, 4, 6) if same_core else range(1, N_DEV)):
            px = 1 - x if r & 4 else x
            py = 1 - y if r & 2 else y
            pc = 1 - c if r & 1 else c
            idx = 2 * px + py if same_core else 4 * px + 2 * py + pc
            slot = r // 2 if same_core else r
            for a in range(self.n):
                src = ins[a] if gather else ins[a].at[idx]
                pair = dict(send_sem=send_sems.at[a, r - 1], recv_sem=recv_sems.at[a, r - 1],
                            device_id=(px, py, pc), device_id_type=pl.DeviceIdType.MESH)
                sends.append(pltpu.make_async_remote_copy(src_ref=src, dst_ref=outs[a].at[me if gather else slot], **pair))
                if with_receives:
                    recvs.append(pltpu.make_async_remote_copy(src_ref=src, dst_ref=outs[a].at[idx if gather else slot], **pair))
        return local, sends, recvs

    def start(self, ins, outs, sems):
        local, sends, _ = self._copies(ins, outs, sems, with_receives=False)
        for cp in local + sends:
            cp.start()

    def wait(self, ins, outs, sems):
        local, sends, recvs = self._copies(ins, outs, sems, with_receives=True)
        for cp in recvs:
            cp.wait_recv()
        for cp in sends:
            cp.wait_send()
        for cp in local:
            cp.wait()


def _peer_exchange(name, parts):
    exs = [_Exchange(arrays, gather) for arrays, gather in parts]
    n = sum(ex.n for ex in exs)
    n_sems = len(exs[0].scratch)

    def body(*refs):
        at, views = 0, []
        for k, ex in enumerate(exs):
            views.append((ex, refs[at:at + ex.n], refs[n + at:n + at + ex.n],
                          refs[2 * n + k * n_sems:2 * n + (k + 1) * n_sems]))
            at += ex.n
        for ex, ins, outs, sems in views:
            ex.start(ins, outs, sems)
        for ex, ins, outs, sems in views:
            ex.wait(ins, outs, sems)

    res = pl.pallas_call(
        body, name=name, in_specs=[s for ex in exs for s in ex.specs], out_specs=[s for ex in exs for s in ex.specs],
        out_shape=[s for ex in exs for s in ex.out_shape], scratch_shapes=[s for ex in exs for s in ex.scratch],
    )(*[a for arrays, _ in parts for a in arrays])
    out, at = [], 0
    for ex in exs:
        out.append(res[at:at + ex.n])
        at += ex.n
    return out


def _gather_two_level(name, arrays):
    n = len(arrays)

    def body(*refs):
        ins, outs = refs[:n], refs[n:2 * n]
        send_sems, recv_sems, local_sems = refs[2 * n:]
        x, y, c = lax.axis_index("x"), lax.axis_index("y"), lax.axis_index("c")
        sibling = (x, y, 1 - c)
        chips = [(1 - x, y), (x, 1 - y), (1 - x, 1 - y)]

        def slot(a, dev):
            return outs[a].at[4 * dev[0] + 2 * dev[1] + dev[2]]

        def copy(a, k, block, to, src=None):
            return pltpu.make_async_remote_copy(
                src_ref=slot(a, block) if src is None else src, dst_ref=slot(a, block),
                send_sem=send_sems.at[a, k], recv_sem=recv_sems.at[a, k], device_id=to, device_id_type=pl.DeviceIdType.MESH)

        me = (x, y, c)
        mine = [pltpu.make_async_copy(ins[a], slot(a, me), local_sems.at[a]) for a in range(n)]
        first = [copy(a, 0, me, sibling, src=ins[a]) for a in range(n)]
        first += [copy(a, 1 + j, me, (*chip, c), src=ins[a]) for j, chip in enumerate(chips) for a in range(n)]
        for cp in mine + first:
            cp.start()
        passed = []
        for j, chip in enumerate(chips):
            for a in range(n):
                copy(a, 1 + j, (*chip, c), me).wait_recv()
            for a in range(n):
                cp = copy(a, 4 + j, (*chip, c), sibling)
                cp.start()
                passed.append(cp)
        for a in range(n):
            copy(a, 0, sibling, me).wait_recv()
            for j, chip in enumerate(chips):
                copy(a, 4 + j, (*chip, 1 - c), me).wait_recv()
        for cp in first + passed:
            cp.wait_send()
        for cp in mine:
            cp.wait()

    any_spec = pl.BlockSpec(memory_space=pl.ANY)
    return pl.pallas_call(
        body, name=name, in_specs=[any_spec] * n, out_specs=[any_spec] * n,
        out_shape=[jax.ShapeDtypeStruct((N_DEV,) + a.shape, a.dtype) for a in arrays],
        scratch_shapes=[pltpu.SemaphoreType.DMA((n, N_DEV - 1)), pltpu.SemaphoreType.DMA((n, N_DEV - 1)),
                        pltpu.SemaphoreType.DMA((n,))],
    )(*arrays)


def _sibling_swap(name, by_core):
    def body(x_ref, o_ref, send_sem, recv_sem):
        x, y, c = lax.axis_index("x"), lax.axis_index("y"), lax.axis_index("c")
        cp = pltpu.make_async_remote_copy(src_ref=x_ref.at[1 - c], dst_ref=o_ref, send_sem=send_sem, recv_sem=recv_sem,
                                          device_id=(x, y, 1 - c), device_id_type=pl.DeviceIdType.MESH)
        cp.start()
        cp.wait()

    any_spec = pl.BlockSpec(memory_space=pl.ANY)
    return pl.pallas_call(
        body, name=name, in_specs=[any_spec], out_specs=any_spec,
        out_shape=jax.ShapeDtypeStruct(by_core.shape[1:], by_core.dtype),
        scratch_shapes=[pltpu.SemaphoreType.DMA(()), pltpu.SemaphoreType.DMA(())],
    )(by_core)


def _pair_sum(name, a, b):
    rows, cols = a.shape
    tr = 512

    def body(a_ref, b_ref, o_ref):
        o_ref[...] = (a_ref[...].astype(F32) + b_ref[...].astype(F32)).astype(o_ref.dtype)

    spec = pl.BlockSpec((tr, cols), lambda i: (i, 0))
    return pl.pallas_call(body, name=name, grid=(rows // tr,), in_specs=[spec, spec], out_specs=spec,
                          out_shape=jax.ShapeDtypeStruct(a.shape, a.dtype),
                          compiler_params=_params(3 * tr * cols * 4, ("parallel",)))(a, b)


def _sum_slots(name, x):
    def body(x_ref, o_ref):
        acc = x_ref[0]
        for s in range(1, N_DEV):
            acc = acc + x_ref[s]
        o_ref[...] = acc
    return pl.pallas_call(body, name=name, out_shape=jax.ShapeDtypeStruct(x.shape[1:], F32))(x)


def _adamw(name, w, slots, m, v):
    n_slots, rows, cols = slots.shape
    tr = next((t for t in (256, 176, 128) if rows % t == 0 and rows > t), rows)

    def body(w_ref, s_ref, m_ref, v_ref, g_ref, d_ref, nm_ref, nv_ref):
        g = s_ref[0].astype(F32)
        for s in range(1, n_slots):
            g = g + s_ref[s].astype(F32)
        nm = ADAM_B1 * m_ref[...] + (1.0 - ADAM_B1) * g
        nv = ADAM_B2 * v_ref[...] + (1.0 - ADAM_B2) * (g * g)
        m_hat = nm / (1.0 - ADAM_B1 ** ADAM_STEP)
        v_hat = nv / (1.0 - ADAM_B2 ** ADAM_STEP)
        g_ref[...] = g
        d_ref[...] = -ADAM_LR * (m_hat / (jnp.sqrt(v_hat) + ADAM_EPS) + ADAM_WD * w_ref[...])
        nm_ref[...] = nm
        nv_ref[...] = nv

    spec = pl.BlockSpec((tr, cols), lambda i: (i, 0))
    out = jax.ShapeDtypeStruct((rows, cols), F32)
    return pl.pallas_call(
        body, name=name, grid=(rows // tr,),
        in_specs=[spec, pl.BlockSpec((n_slots, tr, cols), lambda i: (0, i, 0)), spec, spec],
        out_specs=[spec] * 4, out_shape=[out] * 4,
        compiler_params=_params((n_slots + 7) * tr * cols * 4, ("parallel",)),
    )(w, slots, m, v)


def _pad_lanes(a):
    return jnp.pad(a, ((0, 0), (0, LANES - a.shape[1])))


def kernel(x, meta_tokens, mix_pre_g, w_in, ssd_conv_w, ssd_conv_b, ssd_dt_bias, ssd_a_log, ssd_d, ssd_norm_g, sb_norm_g, w_out, mix_post_g, ffn_pre_g, w_up, ffn_conv_w, ffn_conv_b, w_down, ffn_post_g, loss_target, m_meta_tokens, m_mix_pre_g, m_w_in, m_ssd_conv_w, m_ssd_conv_b, m_ssd_dt_bias, m_ssd_a_log, m_ssd_d, m_ssd_norm_g, m_sb_norm_g, m_w_out, m_mix_post_g, m_ffn_pre_g, m_w_up, m_ffn_conv_w, m_ffn_conv_b, m_w_down, m_ffn_post_g, v_meta_tokens, v_mix_pre_g, v_w_in, v_ssd_conv_w, v_ssd_conv_b, v_ssd_dt_bias, v_ssd_a_log, v_ssd_d, v_ssd_norm_g, v_sb_norm_g, v_w_out, v_mix_post_g, v_ffn_pre_g, v_w_up, v_ffn_conv_w, v_ffn_conv_b, v_w_down, v_ffn_post_g):
    seq = x.shape[1]
    me = 4 * lax.axis_index("x") + 2 * lax.axis_index("y") + lax.axis_index("c")
    in_cols = w_in.shape[2]
    up_cols = w_up.shape[2]
    out_rows = w_out.shape[1]
    down_rows = w_down.shape[1]

    g_in, g_meta, g_scw, g_fcw = _gather_two_level(
        "gather_w_in", [w_in[0].astype(BF16), meta_tokens, ssd_conv_w[0], ffn_conv_w[0]])
    late_weights = [w_out[0].astype(BF16), w_up[0].astype(BF16), w_down[0].astype(BF16)]
    w_in_full = g_in.transpose(1, 0, 2).reshape(D_MODEL, N_DEV * in_cols)
    off = [0, SSD_INNER, SSD_INNER + XBC, SSD_INNER + XBC + HEADS]
    w_z = w_in_full[:, off[0]:off[1]]
    w_xbc = w_in_full[:, off[1]:off[2]]
    w_dt = _pad_lanes(w_in_full[:, off[2]:off[3]])
    w_q = w_in_full[:, off[3]:off[3] + SSD_INNER]
    w_k = w_in_full[:, off[3] + SSD_INNER:off[3] + 2 * SSD_INNER]
    w_v = w_in_full[:, off[3] + 2 * SSD_INNER:off[3] + 3 * SSD_INNER]
    meta_full = g_meta.transpose(1, 0, 2).reshape(N_META, D_MODEL)
    scw_full = g_scw.transpose(1, 0, 2).reshape(SSD_CONV, XBC)
    fcw_full = g_fcw.transpose(1, 0, 2).reshape(FFN_CONV, D_FF)

    dt_bias_p, a_log_p, d_p = _pad_lanes(ssd_dt_bias), _pad_lanes(ssd_a_log), _pad_lanes(ssd_d)

    h0 = jnp.concatenate([jnp.zeros((PAD, D_MODEL), F32), meta_full, x[0]], axis=0)
    target = loss_target[0]
    xn1 = _rms_fwd("rms_pre_mix", h0, mix_pre_g)
    z = _mm("proj_z", [(xn1, w_z)], "nn", F32)
    xbc_raw = _mm("proj_xbc", [(xn1, w_xbc)], "nn", F32)
    dtr = _mm("proj_dt", [(xn1, w_dt)], "nn", F32)
    q = _mm("proj_q", [(xn1, w_q * SB_SCALE)], "nn", BF16)
    k = _mm("proj_k", [(xn1, w_k)], "nn", BF16)
    v = _mm("proj_v", [(xn1, w_v)], "nn", BF16)
    xbc_act = _ssd_conv_fwd(xbc_raw, scw_full, ssd_conv_b)
    ypre, y_ssd, states = _ssd_fwd(xbc_act, dtr, z, dt_bias_p, a_log_p, d_p, ssd_norm_g)
    o, keep_total, (g_out, g_up, g_down) = _attn_fwd(q, k, v, _Exchange(late_weights, gather=True), late_weights)
    w_out_full = g_out.reshape(N_DEV * out_rows, D_MODEL)
    wo_ssd, wo_sb = w_out_full[:SSD_INNER], w_out_full[SSD_INNER:]
    w_up_full = g_up.transpose(1, 0, 2).reshape(D_MODEL, N_DEV * up_cols)
    w_gate, w_lin = w_up_full[:, :D_FF], w_up_full[:, D_FF:]
    w_down_full = g_down.reshape(N_DEV * down_rows, D_MODEL)
    y_sb = _rms_fwd("rms_sb", o, sb_norm_g)
    mix = _mm("mix_out", [(y_ssd, wo_ssd), (y_sb, wo_sb)], "nn", F32)
    h1, xn2 = _mix_post(mix, h0, mix_post_g, ffn_pre_g)
    g_raw = _mm("ffn_gate", [(xn2, w_gate)], "nn", F32)
    u = _mm("ffn_lin", [(xn2, w_lin)], "nn", F32)
    act = _ffn_act(g_raw, u, fcw_full, ffn_conv_b)
    f = _mm("ffn_down", [(act, w_down_full)], "nn", F32)
    dh2, df, loss_row, dg_ffn_post = _loss_post(f, h1, target, ffn_post_g)

    dact = _mm("d_act", [(df, w_down_full)], "nt", F32)
    dw_down = _mm("dw_down", [(act, df)], "tn", F32)
    dg_conv, du = _ffn_bwd_act(dact, u, g_raw, fcw_full, ffn_conv_b)
    dg_raw, dfcw0, dfcw1, dfcw2, dfcb = _conv_bwd("ffn_conv_bwd", dg_conv, g_raw, fcw_full, FFN_CONV)
    dxn2 = _mm("d_xn2", [(dg_raw, w_gate), (du, w_lin)], "nt", F32)
    dw_gate = _mm("dw_gate", [(xn2, dg_raw)], "tn", F32)
    dw_lin = _mm("dw_lin", [(xn2, du)], "tn", F32)
    dh1, dmix, dg_ffn_pre, dg_mix_post = _mid_bwd(dxn2, h1, dh2, mix, ffn_pre_g, mix_post_g)

    dy_ssd = _mm("d_yssd", [(dmix, wo_ssd)], "nt", F32)
    dy_sb = _mm("d_ysb", [(dmix, wo_sb)], "nt", F32)
    dwo_ssd = _mm("dw_out_ssd", [(y_ssd, dmix)], "tn", F32)
    dwo_sb = _mm("dw_out_sb", [(y_sb, dmix)], "tn", F32)
    do, dg_sb = _norm_bwd("sb_norm_bwd", dy_sb, o, sb_norm_g)
    half = N_DEV // 2
    early_slabs = [
        jnp.concatenate([dwo_ssd, dwo_sb], axis=0).reshape(N_DEV, out_rows, D_MODEL),
        jnp.concatenate([dw_gate.reshape(D_MODEL, half, up_cols).transpose(1, 0, 2),
                         dw_lin.reshape(D_MODEL, half, up_cols).transpose(1, 0, 2)], axis=0),
        dw_down.reshape(N_DEV, down_rows, D_MODEL)]
    dq, dk, dv, (l_out, l_up, l_down) = _attn_bwd(q, k, v, keep_total, do, _Exchange(early_slabs, gather=False), early_slabs)
    dz, dxbc_act, ddtr, dg_ssd_norm, dd_skip, da_log, ddt_bias = _ssd_bwd(
        dy_ssd, ypre, z, xbc_act, dtr, states, dt_bias_p, a_log_p, d_p, ssd_norm_g)
    dconv = _ssd_conv_bwd_act(dxbc_act, xbc_raw, scw_full, ssd_conv_b)
    dxbc_raw, dscw0, dscw1, dscw2, dscw3, dscb = _conv_bwd("ssd_conv_bwd", dconv, xbc_raw, scw_full, SSD_CONV)
    segs = [(dz, w_z), (dxbc_raw, w_xbc), (ddtr, w_dt), (dq, w_q), (dk, w_k), (dv, w_v)]
    dw_segs = [_mm("dw_in_%d" % s, [(xn1, d)], "tn", BF16) for s, (d, _) in enumerate(segs)]
    dw_segs[2] = dw_segs[2][:, :HEADS]
    dw_in = jnp.concatenate(dw_segs, axis=1)
    chips = N_DEV // 2
    by_core = dw_in.reshape(D_MODEL, chips, 2, in_cols).transpose(2, 1, 0, 3)
    c_me = lax.axis_index("c")
    from_sibling = _sibling_swap("swap_dw_in", by_core)
    mine = lax.dynamic_index_in_dim(by_core, c_me, 0, keepdims=False)
    pair = _pair_sum("pair_sum_dw_in", mine.reshape(chips * D_MODEL, in_cols),
                     from_sibling.reshape(chips * D_MODEL, in_cols)).reshape(chips, D_MODEL, in_cols)
    dxn1, (l_in,) = _mm("d_xn1", segs, "nt", F32, _Exchange([pair], gather=False, same_core=True), [pair])
    dh0, dg_mix_pre = _first_bwd(dxn1, h0, dh1, mix_pre_g)
    grad_x = dh0[BLK:][None]

    small = [dg_mix_pre, dscb, ddt_bias, da_log, dd_skip, dg_ssd_norm, dg_sb, dg_mix_post, dg_ffn_pre, dfcb,
             dg_ffn_post, dh0[PAD:BLK].reshape(1, -1), dscw0, dscw1, dscw2, dscw3, dfcw0, dfcw1, dfcw2, loss_row]
    sizes = [a.shape[1] for a in small]
    total = sum(sizes)
    rows_packed = -(-total // (LANES * HALO)) * HALO
    packed = jnp.pad(jnp.concatenate(small, axis=1), ((0, 0), (0, rows_packed * LANES - total)))
    ((gathered,),) = _peer_exchange("gather_small_grads", [([packed.reshape(rows_packed, LANES)], True)])
    summed = _sum_slots("sum_small_grads", gathered).reshape(1, rows_packed * LANES)
    pieces, at = [], 0
    for s in sizes:
        pieces.append(summed[:, at:at + s])
        at += s
    (g_mix_pre, g_scb, g_dtb, g_alog, g_dskip, g_ssd_norm, g_sb, g_mix_post, g_ffn_pre, g_fcb, g_ffn_post,
     g_meta_flat, gs0, gs1, gs2, gs3, gf0, gf1, gf2, loss_all) = pieces
    loss = loss_all[0, 0]
    g_dtb, g_alog, g_dskip = g_dtb[:, :HEADS], g_alog[:, :HEADS], g_dskip[:, :HEADS]
    g_meta_full = g_meta_flat.reshape(N_META, D_MODEL)
    g_scw_full = jnp.concatenate([gs0, gs1, gs2, gs3], axis=0)
    g_fcw_full = jnp.concatenate([gf0, gf1, gf2], axis=0)
    meta_cols, scw_cols, fcw_cols = meta_tokens.shape[1], ssd_conv_w.shape[2], ffn_conv_w.shape[2]
    g_meta_mine = lax.dynamic_slice(g_meta_full, (0, me * meta_cols), (N_META, meta_cols))
    g_scw_mine = lax.dynamic_slice(g_scw_full, (0, me * scw_cols), (SSD_CONV, scw_cols))
    g_fcw_mine = lax.dynamic_slice(g_fcw_full, (0, me * fcw_cols), (FFN_CONV, fcw_cols))

    def lead(a):
        return a[None]

    upd = [
        _adamw("adamw_meta", meta_tokens, lead(g_meta_mine), m_meta_tokens, v_meta_tokens),
        _adamw("adamw_mix_pre_g", mix_pre_g, lead(g_mix_pre), m_mix_pre_g, v_mix_pre_g),
        [lead(a) for a in _adamw("adamw_w_in", w_in[0], l_in, m_w_in[0], v_w_in[0])],
        [lead(a) for a in _adamw("adamw_ssd_conv_w", ssd_conv_w[0], lead(g_scw_mine), m_ssd_conv_w[0], v_ssd_conv_w[0])],
        _adamw("adamw_ssd_conv_b", ssd_conv_b, lead(g_scb), m_ssd_conv_b, v_ssd_conv_b),
        _adamw("adamw_ssd_dt_bias", ssd_dt_bias, lead(g_dtb), m_ssd_dt_bias, v_ssd_dt_bias),
        _adamw("adamw_ssd_a_log", ssd_a_log, lead(g_alog), m_ssd_a_log, v_ssd_a_log),
        _adamw("adamw_ssd_d", ssd_d, lead(g_dskip), m_ssd_d, v_ssd_d),
        _adamw("adamw_ssd_norm_g", ssd_norm_g, lead(g_ssd_norm), m_ssd_norm_g, v_ssd_norm_g),
        _adamw("adamw_sb_norm_g", sb_norm_g, lead(g_sb), m_sb_norm_g, v_sb_norm_g),
        [lead(a) for a in _adamw("adamw_w_out", w_out[0], l_out, m_w_out[0], v_w_out[0])],
        _adamw("adamw_mix_post_g", mix_post_g, lead(g_mix_post), m_mix_post_g, v_mix_post_g),
        _adamw("adamw_ffn_pre_g", ffn_pre_g, lead(g_ffn_pre), m_ffn_pre_g, v_ffn_pre_g),
        [lead(a) for a in _adamw("adamw_w_up", w_up[0], l_up, m_w_up[0], v_w_up[0])],
        [lead(a) for a in _adamw("adamw_ffn_conv_w", ffn_conv_w[0], lead(g_fcw_mine), m_ffn_conv_w[0], v_ffn_conv_w[0])],
        _adamw("adamw_ffn_conv_b", ffn_conv_b, lead(g_fcb), m_ffn_conv_b, v_ffn_conv_b),
        [lead(a) for a in _adamw("adamw_w_down", w_down[0], l_down, m_w_down[0], v_w_down[0])],
        _adamw("adamw_ffn_post_g", ffn_post_g, lead(g_ffn_post), m_ffn_post_g, v_ffn_post_g),
    ]
    grads = [u_[0] for u_ in upd]
    deltas = [u_[1] for u_ in upd]
    new_m = [u_[2] for u_ in upd]
    new_v = [u_[3] for u_ in upd]
    return (loss, grad_x, *grads, *deltas, *new_m, *new_v)
```

```python
import math

import jax
import jax.numpy as jnp
from jax import lax
from jax.experimental import pallas as pl
from jax.experimental.pallas import tpu as pltpu

F32 = jnp.float32
BF16 = jnp.bfloat16
HI = lax.Precision.HIGHEST

D_MODEL = 1024
N_META = 16
BLK = 128
PAD = BLK - N_META
HEADS = 16
HEAD_DIM = 64
SSD_GROUPS = 2
SSD_STATE = 128
HEADS_PER_GROUP = HEADS // SSD_GROUPS
SSD_INNER = HEADS * HEAD_DIM
SSD_CONV = 4
XBC = SSD_INNER + 2 * SSD_GROUPS * SSD_STATE
OFF_B = SSD_INNER
OFF_C = SSD_INNER + SSD_GROUPS * SSD_STATE
D_FF = 2816
FFN_CONV = 3
EPS = 1e-6
SB_SCALE = 1.0 / math.sqrt(HEAD_DIM)
N_DEV = 8
LANES = 128
HALO = 8

ADAM_LR = 0.001
ADAM_B1 = 0.9
ADAM_B2 = 0.999
ADAM_EPS = 1e-08
ADAM_WD = 0.01
ADAM_STEP = 10

VMEM_FLOOR = 32 << 20
VMEM_CEIL = 60 << 20
MM_BUDGET = 20 << 20
ROW_BUDGET = 6 << 20

NN = (((1,), (0,)), ((), ()))
NT = (((1,), (1,)), ((), ()))
TN = (((0,), (0,)), ((), ()))


def _params(tile_bytes, sem=None):
    limit = int(min(max(2 * tile_bytes + (8 << 20), VMEM_FLOOR), VMEM_CEIL))
    return pltpu.CompilerParams(vmem_limit_bytes=limit, dimension_semantics=sem)


def _nbytes(shape, dtype):
    n = 1
    for s in shape:
        n *= s
    return n * jnp.dtype(dtype).itemsize


def _dot(a, b, dims=NN, precision=None):
    return lax.dot_general(a, b, dims, precision=precision, preferred_element_type=F32)


def _softplus(x):
    return jnp.maximum(x, 0.0) + jnp.log1p(jnp.exp(-jnp.abs(x)))


def _rms(x, g):
    r = lax.rsqrt(jnp.mean(x * x, axis=-1, keepdims=True) + EPS)
    return x * r * g


def _rms_bwd(dy, x, g):
    r = lax.rsqrt(jnp.mean(x * x, axis=-1, keepdims=True) + EPS)
    xh = x * r
    u = dy * g
    dx = r * (u - xh * jnp.mean(xh * u, axis=-1, keepdims=True))
    return dx, jnp.sum(dy * xh, axis=0, keepdims=True)


def _gelu(x):
    c = math.sqrt(2.0 / math.pi)
    return 0.5 * x * (1.0 + jnp.tanh(c * (x + 0.044715 * x * x * x)))


def _gelu_and_grad(x):
    c = math.sqrt(2.0 / math.pi)
    x2 = x * x
    t = jnp.tanh(c * (x + 0.044715 * x2 * x))
    half = 0.5 * (1.0 + t)
    return x * half, half + 0.5 * x * (1.0 - t * t) * c * (1.0 + 3.0 * 0.044715 * x2)


def _row_tile(rows, bytes_per_row):
    big = 384
    return big if rows % big == 0 and big * bytes_per_row <= ROW_BUDGET else BLK


def _mm(name, pairs, mode, out_dtype, ex=None, ex_arrays=()):
    a0, b0 = pairs[0]
    if mode == "tn":
        m, n = a0.shape[1], b0.shape[1]
    elif mode == "nt":
        m, n = a0.shape[0], b0.shape[0]
    else:
        m, n = a0.shape[0], b0.shape[1]
    dims = {"nn": NN, "nt": NT, "tn": TN}[mode]

    def tile_bytes(tm, tn):
        tot = tm * tn * jnp.dtype(out_dtype).itemsize
        for a, b in pairs:
            k = a.shape[0] if mode == "tn" else a.shape[1]
            tot += tm * k * a.dtype.itemsize + tn * k * b.dtype.itemsize
        return tot

    cands_m = [t for t in (1408, 1024, 512, 384, 256, 128) if m % t == 0] or [m]
    cands_n = [t for t in (1408, 1024, 768, 512, 256, 128) if n % t == 0] or [n]
    best = None
    for tm in cands_m:
        for tn in cands_n:
            if tile_bytes(tm, tn) <= MM_BUDGET and (best is None or tm * tn > best[0] * best[1]):
                best = (tm, tn)
    tm, tn = best if best is not None else (cands_m[-1], cands_n[-1])
    npairs = len(pairs)

    grid = (m // tm, n // tn)

    def body(*refs):
        if ex is None:
            ins, o_ref = refs[:2 * npairs], refs[2 * npairs]
        else:
            ins, ex_in, (o_ref,), ex_out, _, sems = _ride_along(ex, 2 * npairs, 1, refs)
            step = pl.program_id(0) * grid[1] + pl.program_id(1)

            @pl.when(step == 0)
            def _():
                ex.start(ex_in, ex_out, sems)

        acc = None
        for p in range(npairs):
            part = _dot(ins[2 * p][...], ins[2 * p + 1][...], dims)
            acc = part if acc is None else acc + part
        o_ref[...] = acc.astype(o_ref.dtype)
        if ex is not None:
            @pl.when(step == grid[0] * grid[1] - 1)
            def _():
                ex.wait(ex_in, ex_out, sems)

    in_specs, args = [], []
    for a, b in pairs:
        if mode == "tn":
            k = a.shape[0]
            in_specs += [pl.BlockSpec((k, tm), lambda i, j: (0, i)), pl.BlockSpec((k, tn), lambda i, j: (0, j))]
        elif mode == "nt":
            k = a.shape[1]
            in_specs += [pl.BlockSpec((tm, k), lambda i, j: (i, 0)), pl.BlockSpec((tn, k), lambda i, j: (j, 0))]
        else:
            k = a.shape[1]
            in_specs += [pl.BlockSpec((tm, k), lambda i, j: (i, 0)), pl.BlockSpec((k, tn), lambda i, j: (0, j))]
        args += [a, b]
    out_spec = pl.BlockSpec((tm, tn), lambda i, j: (i, j))
    out_shape = jax.ShapeDtypeStruct((m, n), out_dtype)
    if ex is None:
        return pl.pallas_call(
            body, name=name, grid=grid, in_specs=in_specs, out_specs=out_spec, out_shape=out_shape,
            compiler_params=_params(tile_bytes(tm, tn), ("parallel", "parallel")),
        )(*args)
    res = pl.pallas_call(
        body, name=name, grid=grid, in_specs=in_specs + ex.specs, out_specs=[out_spec] + ex.specs,
        out_shape=[out_shape] + ex.out_shape, scratch_shapes=ex.scratch,
        compiler_params=_params(tile_bytes(tm, tn), ("arbitrary", "arbitrary")),
    )(*args, *ex_arrays)
    return res[0], res[1:]


class _Window:
    def __init__(self, ref, cols):
        self.ref, self.cols = ref, cols

    def __getitem__(self, idx):
        return self.ref[:, self.cols] if idx is Ellipsis else self.ref[idx[0], self.cols]


def _rowcall(name, fn, rows=(), prevs=(), nexts=(), pars=(), out_rows=(), out_accs=(), chunk=None, late=()):
    rows, prevs, nexts, pars, late = list(rows), list(prevs), list(nexts), list(pars), list(late)
    n_rows = (rows + prevs + nexts)[0].shape[0]
    per_row = sum(a.shape[1] * a.dtype.itemsize for a in rows + prevs + nexts + late)
    per_row += sum(c * jnp.dtype(dt).itemsize for c, dt in out_rows) + sum(a.shape[1] * 4 for a in prevs + nexts)
    tm = _row_tile(n_rows, per_row)
    nt = n_rows // tm
    hb = tm // HALO
    if late:
        tm = BLK
        nt, hb = n_rows // tm, tm // HALO
    rows = rows + late
    nr, npv, nnx, npar, nor, noa = len(rows), len(prevs), len(nexts), len(pars), len(out_rows), len(out_accs)

    def body(*refs):
        i = pl.program_id(0)
        k = 0
        row_refs = refs[k:k + nr]; k += nr
        pc = refs[k:k + npv]; k += npv
        ph = refs[k:k + npv]; k += npv
        nc = refs[k:k + nnx]; k += nnx
        nh = refs[k:k + nnx]; k += nnx
        par_refs = refs[k:k + npar]; k += npar
        orow = refs[k:k + nor]; k += nor
        oacc = refs[k:k + noa]; k += noa
        pscr = refs[k:k + npv]; k += npv
        nscr = refs[k:k + nnx]
        for c_, h_, s_ in zip(pc, ph, pscr):
            s_[0:HALO, :] = h_[...] * (i > 0).astype(F32)
            s_[HALO:HALO + tm, :] = c_[...]
        for c_, h_, s_ in zip(nc, nh, nscr):
            s_[0:tm, :] = c_[...]
            s_[tm:tm + HALO, :] = h_[...] * (i < nt - 1).astype(F32)
        if noa:
            @pl.when(i == 0)
            def _():
                for r_ in oacc:
                    r_[...] = jnp.zeros_like(r_)
        width = out_rows[0][0]
        windows = [slice(None)] if chunk is None else [slice(c0, c0 + chunk) for c0 in range(0, width, chunk)]
        for cols in windows:
            prev_fns = [(lambda s, s_=s_, cols=cols: s_[pl.ds(HALO, tm), cols] if s == 0 else
                         pltpu.roll(s_[:, cols], s, 0)[HALO:HALO + tm]) for s_ in pscr]
            next_fns = [(lambda s, s_=s_, cols=cols: s_[pl.ds(0, tm), cols] if s == 0 else
                         pltpu.roll(s_[:, cols], tm + HALO - s, 0)[0:tm]) for s_ in nscr]
            row_vals, acc_vals = fn(i, tm, [_Window(r_, cols) for r_ in row_refs], prev_fns, next_fns,
                                    [_Window(r_, cols) for r_ in par_refs])
            for r_, v in zip(orow, row_vals):
                r_[:, cols] = v.astype(r_.dtype)
            for r_, v in zip(oacc, acc_vals):
                r_[:, cols] += v

    def row_spec(a):
        return pl.BlockSpec((tm, a.shape[1]), lambda i: (i, 0))

    def whole(shape):
        return pl.BlockSpec(shape, lambda i: (0, 0))

    in_specs = [row_spec(a) for a in rows[:nr - len(late)]]
    in_specs += [pl.BlockSpec((tm, a.shape[1]), lambda i: (jnp.maximum(i - 1, 0), 0)) for a in late]
    in_specs += [row_spec(a) for a in prevs]
    in_specs += [pl.BlockSpec((HALO, a.shape[1]), lambda i: (jnp.maximum(i * hb - 1, 0), 0)) for a in prevs]
    in_specs += [row_spec(a) for a in nexts]
    in_specs += [pl.BlockSpec((HALO, a.shape[1]), lambda i: (jnp.minimum((i + 1) * hb, n_rows // HALO - 1), 0)) for a in nexts]
    in_specs += [whole(p.shape) for p in pars]
    out_specs = [pl.BlockSpec((tm, c), lambda i: (i, 0)) for c, _ in out_rows] + [whole(s) for s in out_accs]
    out_shape = [jax.ShapeDtypeStruct((n_rows, c), dt) for c, dt in out_rows]
    out_shape += [jax.ShapeDtypeStruct(s, F32) for s in out_accs]
    scratch = [pltpu.VMEM((tm + HALO, a.shape[1]), F32) for a in prevs + nexts]
    tile = tm * per_row
    res = pl.pallas_call(
        body, name=name, grid=(nt,), in_specs=in_specs, out_specs=out_specs, out_shape=out_shape,
        scratch_shapes=scratch, compiler_params=_params(2 * tile, ("arbitrary",)),
    )(*rows, *prevs, *prevs, *nexts, *nexts, *pars)
    return res


def _row_ids(i, tm):
    return i * tm + lax.broadcasted_iota(jnp.int32, (tm, 1), 0)


def _conv_taps(prev_fn, w_ref, b_ref, taps):
    acc = b_ref[...]
    for k in range(taps):
        acc = acc + prev_fn(taps - 1 - k) * w_ref[k:k + 1, :]
    return acc


def _rms_fwd(name, h, g):
    def fn(i, tm, rows, prevs, nexts, pars):
        return [_rms(rows[0][...], pars[0][...])], []
    return _rowcall(name, fn, rows=[h], pars=[g], out_rows=[(h.shape[1], BF16)])[0]


def _ssd_conv_fwd(xbc_raw, w, b):
    def fn(i, tm, rows, prevs, nexts, pars):
        c = _conv_taps(prevs[0], pars[0], pars[1], SSD_CONV)
        y = c * jax.nn.sigmoid(c)
        return [jnp.where(_row_ids(i, tm) >= PAD, y, 0.0)], []
    return _rowcall("ssd_conv_fwd", fn, prevs=[xbc_raw], pars=[w, b], out_rows=[(XBC, F32)], chunk=LANES)[0]


def _mix_post(mix, h0, g_post, g_pre):
    def fn(i, tm, rows, prevs, nexts, pars):
        h1 = rows[1][...] + _rms(rows[0][...], pars[0][...])
        return [h1, _rms(h1, pars[1][...])], []
    return _rowcall("mix_post", fn, rows=[mix, h0], pars=[g_post, g_pre],
                    out_rows=[(D_MODEL, F32), (D_MODEL, BF16)])


def _ffn_act(g_raw, u, w, b):
    def fn(i, tm, rows, prevs, nexts, pars):
        g = _conv_taps(prevs[0], pars[0], pars[1], FFN_CONV)
        return [_gelu(g) * rows[0][...]], []
    return _rowcall("ffn_act", fn, rows=[u], prevs=[g_raw], pars=[w, b], out_rows=[(D_FF, BF16)], chunk=LANES)[0]


def _loss_post(f, h1, target, g_post):
    def fn(i, tm, rows, prevs, nexts, pars):
        fv, g = rows[0][...], pars[0][...]
        h2 = rows[1][...] + _rms(fv, g)
        real = _row_ids(i, tm) >= BLK
        diff = jnp.where(real, h2 - rows[2][...], 0.0)
        loss = 0.5 * jnp.sum(jnp.mean(diff * diff, axis=-1, keepdims=True))
        dh2 = diff * (1.0 / D_MODEL)
        df, dg = _rms_bwd(dh2, fv, g)
        return [dh2, df], [jnp.zeros((1, LANES), F32) + loss, dg]
    return _rowcall("loss_post", fn, rows=[f, h1], late=[target], pars=[g_post],
                    out_rows=[(D_MODEL, F32), (D_MODEL, BF16)], out_accs=[(1, LANES), (1, D_MODEL)])


def _ffn_bwd_act(dact, u, g_raw, w, b):
    def fn(i, tm, rows, prevs, nexts, pars):
        g = _conv_taps(prevs[0], pars[0], pars[1], FFN_CONV)
        da = rows[0][...]
        gelu, grad = _gelu_and_grad(g)
        return [da * rows[1][...] * grad, da * gelu], []
    return _rowcall("ffn_bwd_act", fn, rows=[dact, u], prevs=[g_raw], pars=[w, b],
                    out_rows=[(D_FF, F32), (D_FF, BF16)], chunk=LANES)


def _conv_bwd(name, dy, x, w, taps):
    width = x.shape[1]

    def fn(i, tm, rows, prevs, nexts, pars):
        dy0 = nexts[0](0)
        dx = None
        for k in range(taps):
            term = nexts[0](taps - 1 - k) * pars[0][k:k + 1, :]
            dx = term if dx is None else dx + term
        dws = [jnp.sum(dy0 * prevs[0](taps - 1 - k), axis=0, keepdims=True) for k in range(taps)]
        return [dx], dws + [jnp.sum(dy0, axis=0, keepdims=True)]
    return _rowcall(name, fn, prevs=[x], nexts=[dy], pars=[w], out_rows=[(width, BF16)],
                    out_accs=[(1, width)] * (taps + 1), chunk=LANES)


def _mid_bwd(dxn2, h1, dh2, mix, g_pre, g_post):
    def fn(i, tm, rows, prevs, nexts, pars):
        d1, dg_pre = _rms_bwd(rows[0][...], rows[1][...], pars[0][...])
        dh1 = rows[2][...] + d1
        dmix, dg_post = _rms_bwd(dh1, rows[3][...], pars[1][...])
        return [dh1, dmix], [dg_pre, dg_post]
    return _rowcall("mid_bwd", fn, rows=[dxn2, h1, dh2, mix], pars=[g_pre, g_post],
                    out_rows=[(D_MODEL, F32), (D_MODEL, BF16)], out_accs=[(1, D_MODEL), (1, D_MODEL)])


def _norm_bwd(name, dy, x, g):
    def fn(i, tm, rows, prevs, nexts, pars):
        dx, dg = _rms_bwd(rows[0][...], rows[1][...], pars[0][...])
        return [dx], [dg]
    return _rowcall(name, fn, rows=[dy, x], pars=[g], out_rows=[(x.shape[1], BF16)], out_accs=[(1, x.shape[1])])


def _ssd_conv_bwd_act(dact, xbc_raw, w, b):
    def fn(i, tm, rows, prevs, nexts, pars):
        c = _conv_taps(prevs[0], pars[0], pars[1], SSD_CONV)
        s = jax.nn.sigmoid(c)
        dc = rows[0][...] * s * (1.0 + c * (1.0 - s))
        return [jnp.where(_row_ids(i, tm) >= PAD, dc, 0.0)], []
    return _rowcall("ssd_conv_bwd_act", fn, rows=[dact], prevs=[xbc_raw], pars=[w, b], out_rows=[(XBC, F32)],
                    chunk=LANES)[0]


def _first_bwd(dxn1, h0, dh1, g):
    def fn(i, tm, rows, prevs, nexts, pars):
        d0, dg = _rms_bwd(rows[0][...], rows[1][...], pars[0][...])
        return [rows[2][...] + d0], [dg]
    return _rowcall("first_bwd", fn, rows=[dxn1, h0, dh1], pars=[g], out_rows=[(D_MODEL, F32)],
                    out_accs=[(1, D_MODEL)])


def _ssd_chunk_terms(c, dtr_ref, bias_ref, alog_ref):
    ri = lax.broadcasted_iota(jnp.int32, (BLK, BLK), 0)
    ci = lax.broadcasted_iota(jnp.int32, (BLK, BLK), 1)
    causal = ri >= ci
    tril = causal.astype(F32)
    triu = (ri <= ci).astype(F32)
    rowmask = ((c * BLK + lax.broadcasted_iota(jnp.int32, (BLK, 1), 0)) >= PAD).astype(F32)
    dt = _softplus(dtr_ref[...] + bias_ref[...]) * rowmask
    a_neg = -jnp.exp(alog_ref[...])
    a = dt * a_neg
    cs = _dot(tril, a, NN, HI)
    cs_t = _dot(a, triu, TN, HI)
    return causal, triu, rowmask, dt, a_neg, cs, cs_t


def _decay_matrix(causal, cs_h, cs_t_h):
    return jnp.where(causal, jnp.exp(jnp.where(causal, cs_h - cs_t_h, 0.0)), 0.0)


def _head_spread(width):
    ri = lax.broadcasted_iota(jnp.int32, (LANES, HEADS * width), 0)
    ci = lax.broadcasted_iota(jnp.int32, (LANES, HEADS * width), 1)
    return ((ri * width <= ci) & (ci < (ri + 1) * width)).astype(BF16)


def _three_terms(x):
    hi = x.astype(BF16)
    rest = x - hi.astype(F32)
    mid = rest.astype(BF16)
    lo = (rest - mid.astype(F32)).astype(BF16)
    return jnp.concatenate([hi, mid, lo], axis=1)


def _spread(x, sel):
    return _dot(_three_terms(x), jnp.concatenate([sel, sel, sel], axis=0))


def _lane_sums(y, sel):
    return _dot(_three_terms(y), jnp.concatenate([sel, sel, sel], axis=1), NT)


def _ssd_spreads(dt, cs, d_ref, sel64, sel128):
    dt64 = _spread(dt, sel64)
    cs64 = _spread(cs, sel64)
    cs128 = _spread(cs, sel128)
    d64 = _spread(jnp.broadcast_to(d_ref[...], (HALO, LANES)), sel64)[0:1, :]
    cl64 = cs64[BLK - 1:BLK, :]
    return dt64, cs128, d64, jnp.exp(cs64), jnp.exp(cl64 - cs64), jnp.exp(cl64)


def _ssd_pair_terms(p, xbc_ref, dt64, cs128, cs_t, causal):
    lanes = _pair_lanes(p)
    xs = xbc_ref[:, lanes]
    decays = [_decay_matrix(causal, cs128[:, h * LANES:(h + 1) * LANES], cs_t[h:h + 1, :]) for h in (2 * p, 2 * p + 1)]
    return xs, xs * dt64[:, lanes], decays


def _ssd_fwd(xbc, dtr, z, dt_bias, a_log, d_skip, norm_g):
    n_rows = xbc.shape[0]
    nb = n_rows // BLK

    def body(xbc_ref, dtr_ref, z_ref, bias_ref, alog_ref, d_ref, g_ref, s64_ref, s128_ref,
             ypre_ref, yssd_ref, st_ref, state):
        c = pl.program_id(0)

        @pl.when(c == 0)
        def _():
            state[...] = jnp.zeros_like(state)

        st_ref[...] = state[...]
        causal, _, _, dt, _, cs, cs_t = _ssd_chunk_terms(c, dtr_ref, bias_ref, alog_ref)
        first = lax.broadcasted_iota(jnp.int32, (BLK, LANES), 1) < HEAD_DIM
        dt64, cs128, d64, from_start, to_end, chunk_decay = _ssd_spreads(dt, cs, d_ref, s64_ref[...], s128_ref[...])
        for g in range(SSD_GROUPS):
            b_b = xbc_ref[:, OFF_B + g * SSD_STATE:OFF_B + (g + 1) * SSD_STATE].astype(BF16)
            c_b = xbc_ref[:, OFF_C + g * SSD_STATE:OFF_C + (g + 1) * SSD_STATE].astype(BF16)
            cb = _dot(c_b, b_b, NT)
            for j in range(HEADS_PER_GROUP // 2):
                p = g * (HEADS_PER_GROUP // 2) + j
                lanes = _pair_lanes(p)
                xs, x_dt, decays = _ssd_pair_terms(p, xbc_ref, dt64, cs128, cs_t, causal)
                ms = [(cb * d).astype(BF16) for d in decays]
                s_p = state[:, lanes]
                y = _dot(jnp.concatenate(ms, axis=1), _head_stack(x_dt.astype(BF16), first))
                y = y + from_start[:, lanes] * _dot(c_b, s_p.astype(BF16))
                state[:, lanes] = chunk_decay[:, lanes] * s_p + _dot(b_b, (to_end[:, lanes] * x_dt).astype(BF16), TN)
                ypre_ref[:, lanes] = y + d64[:, lanes] * xs
        zz = z_ref[...]
        yg = ypre_ref[...] * (zz * jax.nn.sigmoid(zz))
        yssd_ref[...] = _rms(yg, g_ref[...]).astype(yssd_ref.dtype)

    blk = lambda w: pl.BlockSpec((BLK, w), lambda c: (c, 0))
    par = lambda a: pl.BlockSpec(a.shape, lambda c: (0, 0))
    sel64, sel128 = _head_spread(HEAD_DIM), _head_spread(LANES)
    return pl.pallas_call(
        body, name="ssd_fwd", grid=(nb,),
        in_specs=[blk(XBC), blk(LANES), blk(SSD_INNER), par(dt_bias), par(a_log), par(d_skip), par(norm_g),
                  par(sel64), par(sel128)],
        out_specs=[blk(SSD_INNER), blk(SSD_INNER), pl.BlockSpec((None, SSD_STATE, SSD_INNER), lambda c: (c, 0, 0))],
        out_shape=[jax.ShapeDtypeStruct((n_rows, SSD_INNER), F32), jax.ShapeDtypeStruct((n_rows, SSD_INNER), BF16),
                   jax.ShapeDtypeStruct((nb, SSD_STATE, SSD_INNER), F32)],
        scratch_shapes=[pltpu.VMEM((SSD_STATE, SSD_INNER), F32)],
        compiler_params=_params(8 << 20, ("arbitrary",)),
    )(xbc, dtr, z, dt_bias, a_log, d_skip, norm_g, sel64, sel128)


def _ssd_bwd(dy, ypre, z, xbc, dtr, states, dt_bias, a_log, d_skip, norm_g):
    n_rows = xbc.shape[0]
    nb = n_rows // BLK

    def body(dy_ref, ypre_ref, z_ref, xbc_ref, dtr_ref, st_ref, bias_ref, alog_ref, d_ref, g_ref, s64_ref, s128_ref,
             dz_ref, dxbc_ref, ddtr_ref, dgn_ref, dd_ref, dal_ref, ddtb_ref, dstate, dyp, red):
        step = pl.program_id(0)
        c = nb - 1 - step

        @pl.when(step == 0)
        def _():
            dstate[...] = jnp.zeros_like(dstate)
            for r_ in (dgn_ref, dd_ref, dal_ref, ddtb_ref):
                r_[...] = jnp.zeros_like(r_)

        yp, zz = ypre_ref[...], z_ref[...]
        sz = jax.nn.sigmoid(zz)
        silu = zz * sz
        dyg, dgn = _rms_bwd(dy_ref[...], yp * silu, g_ref[...])
        dgn_ref[...] += dgn
        dz_ref[...] = (dyg * yp * (sz * (1.0 + zz * (1.0 - sz)))).astype(dz_ref.dtype)
        dyp[...] = dyg * silu

        causal, triu, rowmask, dt, a_neg, cs, cs_t = _ssd_chunk_terms(c, dtr_ref, bias_ref, alog_ref)
        lane = lax.broadcasted_iota(jnp.int32, (1, LANES), 1)
        last_row = (lax.broadcasted_iota(jnp.int32, (BLK, 1), 0) == BLK - 1).astype(F32)
        first = lax.broadcasted_iota(jnp.int32, (BLK, LANES), 1) < HEAD_DIM
        sel64 = s64_ref[...]
        dt64, cs128, d64, from_start, to_end, chunk_decay = _ssd_spreads(dt, cs, d_ref, sel64, s128_ref[...])
        for g in range(SSD_GROUPS):
            b_b = xbc_ref[:, OFF_B + g * SSD_STATE:OFF_B + (g + 1) * SSD_STATE].astype(BF16)
            c_b = xbc_ref[:, OFF_C + g * SSD_STATE:OFF_C + (g + 1) * SSD_STATE].astype(BF16)
            cb = _dot(c_b, b_b, NT)
            b_twice = jnp.concatenate([b_b, b_b], axis=0)
            c_twice = jnp.concatenate([c_b, c_b], axis=0)
            db_g = jnp.zeros((BLK, SSD_STATE), F32)
            dc_g = jnp.zeros((BLK, SSD_STATE), F32)
            for j in range(HEADS_PER_GROUP // 2):
                p = g * (HEADS_PER_GROUP // 2) + j
                lanes = _pair_lanes(p)
                xs, x_dt, decays = _ssd_pair_terms(p, xbc_ref, dt64, cs128, cs_t, causal)
                ms = [(cb * d).astype(BF16) for d in decays]
                d_y = dyp[:, lanes]
                s_p, ds_p = st_ref[:, lanes], dstate[:, lanes]
                s_b, ds_b = s_p.astype(BF16), ds_p.astype(BF16)
                fs, te = from_start[:, lanes], to_end[:, lanes]
                x_b, dy_b = x_dt.astype(BF16), d_y.astype(BF16)
                x_st = _head_stack(x_b, first)
                fs_dy = (fs * d_y).astype(BF16)

                y_diag = _dot(jnp.concatenate(ms, axis=1), x_st)
                y_off = fs * _dot(c_b, s_b)
                end_part = te * _dot(b_b, ds_b)
                dx_diag = _dot(jnp.concatenate(ms, axis=0), _head_stack(dy_b, first), TN)
                d_x = dx_diag + end_part
                g2 = _dot(dy_b, x_st, NT)
                gl = [(g2[:, k * BLK:(k + 1) * BLK] * decays[k]).astype(BF16) for k in range(2)]
                dc_g = dc_g + _dot(jnp.concatenate(gl, axis=1), b_twice) + _dot(fs_dy, s_b, NT)
                db_g = db_g + _dot(jnp.concatenate(gl, axis=0), c_twice, TN) + _dot((te * x_dt).astype(BF16), ds_b, NT)
                red[0:BLK, lanes] = (dy_b.astype(F32) * y_diag - x_b.astype(F32) * dx_diag) + (d_y * y_off - x_dt * end_part)
                red[BLK:2 * BLK, lanes] = x_dt * end_part
                red[2 * BLK:3 * BLK, lanes] = d_x * xs
                red[3 * BLK:3 * BLK + HALO, lanes] = jnp.broadcast_to(jnp.sum(ds_p * s_p, axis=0, keepdims=True), (HALO, LANES))
                red[3 * BLK + HALO:3 * BLK + 2 * HALO, lanes] = jnp.broadcast_to(
                    jnp.sum(d_y * xs, axis=0, keepdims=True), (HALO, LANES))
                dxbc_ref[:, lanes] = d_x * dt64[:, lanes] + d64[:, lanes] * d_y
                dstate[:, lanes] = chunk_decay[:, lanes] * ds_p + _dot(c_b, fs_dy, TN)
            dxbc_ref[:, OFF_B + g * SSD_STATE:OFF_B + (g + 1) * SSD_STATE] = db_g
            dxbc_ref[:, OFF_C + g * SSD_STATE:OFF_C + (g + 1) * SSD_STATE] = dc_g
        sums = _lane_sums(red[...], sel64)
        at_last = (jnp.sum(sums[BLK:2 * BLK], axis=0, keepdims=True)
                   + jnp.exp(cs[BLK - 1:BLK, :]) * sums[3 * BLK:3 * BLK + 1])
        dcs = sums[0:BLK] + last_row * at_last
        ddt_x = sums[2 * BLK:3 * BLK]
        dd_row = sums[3 * BLK + HALO:3 * BLK + HALO + 1]
        da = _dot(triu, dcs, NN, HI)
        ddt = (da * a_neg + ddt_x) * rowmask
        ddtr = ddt * jax.nn.sigmoid(dtr_ref[...] + bias_ref[...]) * (lane < HEADS).astype(F32)
        ddtr_ref[...] = ddtr.astype(ddtr_ref.dtype)
        ddtb_ref[...] += jnp.sum(ddtr, axis=0, keepdims=True)
        dal_ref[...] += jnp.sum(da * dt, axis=0, keepdims=True) * a_neg
        dd_ref[...] += dd_row

    blk = lambda w: pl.BlockSpec((BLK, w), lambda s: (nb - 1 - s, 0))
    par = lambda a: pl.BlockSpec(a.shape, lambda s: (0, 0))
    acc = lambda w: pl.BlockSpec((1, w), lambda s: (0, 0))
    sel64, sel128 = _head_spread(HEAD_DIM), _head_spread(LANES)
    return pl.pallas_call(
        body, name="ssd_bwd", grid=(nb,),
        in_specs=[blk(SSD_INNER), blk(SSD_INNER), blk(SSD_INNER), blk(XBC), blk(LANES),
                  pl.BlockSpec((None, SSD_STATE, SSD_INNER), lambda s: (nb - 1 - s, 0, 0)),
                  par(dt_bias), par(a_log), par(d_skip), par(norm_g), par(sel64), par(sel128)],
        out_specs=[blk(SSD_INNER), blk(XBC), blk(LANES), acc(SSD_INNER), acc(LANES), acc(LANES), acc(LANES)],
        out_shape=[jax.ShapeDtypeStruct((n_rows, SSD_INNER), BF16), jax.ShapeDtypeStruct((n_rows, XBC), F32),
                   jax.ShapeDtypeStruct((n_rows, LANES), BF16), jax.ShapeDtypeStruct((1, SSD_INNER), F32),
                   jax.ShapeDtypeStruct((1, LANES), F32), jax.ShapeDtypeStruct((1, LANES), F32),
                   jax.ShapeDtypeStruct((1, LANES), F32)],
        scratch_shapes=[pltpu.VMEM((SSD_STATE, SSD_INNER), F32), pltpu.VMEM((BLK, SSD_INNER), F32),
                        pltpu.VMEM((3 * BLK + 2 * HALO, SSD_INNER), F32)],
        compiler_params=_params(12 << 20, ("arbitrary",)),
    )(dy, ypre, z, xbc, dtr, states, dt_bias, a_log, d_skip, norm_g, sel64, sel128)


HEAD_GROUP = 4


LOG2_E = 1.4426950408889634
SOFTPLUS_CLAMP = 80.0


def _sb_logits(zl, valid):
    z2 = zl * LOG2_E
    lost = jnp.maximum(jnp.log2(1.0 + jnp.exp2(jnp.minimum(z2, SOFTPLUS_CLAMP))), z2)
    log_beta = z2 - lost
    if valid is not None:
        lost = jnp.where(valid, lost, 0.0)
    return log_beta, lost


def _tri_and_ones(tri, sign=1.0):
    half = sign * jnp.concatenate([tri, jnp.ones((BLK, BLK), F32)], axis=1)
    return jnp.concatenate([half, half], axis=0).astype(BF16)


def _tile_mask(i, j, ri, ci):
    key = j * BLK + ci
    return (key < i * BLK + ri) & (key >= PAD)


def _pair_lanes(p):
    return slice(p * 2 * HEAD_DIM, (p + 1) * 2 * HEAD_DIM)


def _head_stack(x_pair, first):
    zero = jnp.zeros_like(x_pair)
    return jnp.concatenate([jnp.where(first, x_pair, zero), jnp.where(first, zero, x_pair)], axis=0)


def _block_rows(j):
    return pl.ds(j * BLK if isinstance(j, int) else pl.multiple_of(j * BLK, BLK), BLK)


def _two_terms(x):
    hi = x.astype(BF16)
    return jnp.concatenate([hi, (x - hi.astype(F32)).astype(BF16)], axis=1)


def _ride_along(ex, n_in, n_out, refs):
    k = 0
    parts = []
    for cnt in (n_in, ex.n, n_out, ex.n):
        parts.append(refs[k:k + cnt])
        k += cnt
    n_sems = len(ex.scratch)
    return (*parts, refs[k:len(refs) - n_sems], refs[len(refs) - n_sems:])


def _attn_fwd(q, k, v, ex, ex_arrays):
    n_rows, width = q.shape
    nb = n_rows // BLK
    n_heads = HEAD_GROUP
    gw = n_heads * HEAD_DIM
    groups = width // gw

    def body(*refs):
        (q_ref, k_ref, v_ref), ex_in, (o_ref, tot_ref), ex_out, (run_ref, z_ref, w_ref), sems = _ride_along(ex, 3, 2, refs)
        step = pl.program_id(0)

        @pl.when(step == 0)
        def _():
            ex.start(ex_in, ex_out, sems)

        ri = lax.broadcasted_iota(jnp.int32, (BLK, BLK), 0)
        ci = lax.broadcasted_iota(jnp.int32, (BLK, BLK), 1)
        sums = _tri_and_ones((ri > ci).astype(F32), -1.0)
        first = ci < HEAD_DIM
        heads, pairs = range(n_heads), range(n_heads // 2)

        def stacks(ref, blocks, p):
            return jnp.concatenate([_head_stack(ref[_block_rows(jnp.clip(j, 0, nb - 1)), _pair_lanes(p)], first)
                                    for j in blocks], axis=0)

        def q_block(i, carry):
            rows = _block_rows(i)
            o_ref[rows, :] = jnp.zeros((BLK, gw), F32)
            run_ref[...] = jnp.zeros_like(run_ref)
            w_ref[...] = jnp.zeros_like(w_ref)

            odd = (i + 1) % 2

            def blocks(t):
                return i + odd - 2 * t, i + odd - 2 * t - 1

            for p in pairs:
                z_ref[p] = _dot(q_ref[rows, _pair_lanes(p)], stacks(k_ref, blocks(0), p), NT)

            def tile(t, masked, skip=None):
                q_i, o_i = q_ref[rows, :], o_ref[rows, :]
                outs = [_dot(w_ref[p], stacks(v_ref, blocks(jnp.maximum(t - 1, 0)), p)) for p in pairs]
                z_next = [_dot(q_i[:, _pair_lanes(p)], stacks(k_ref, blocks(t + 1), p), NT) for p in pairs]
                runs = [run_ref[h] for h in heads]
                lgs, res = {}, {}
                for s, j in enumerate(blocks(t)):
                    if s == skip:
                        continue
                    valid = _tile_mask(i, j, ri, ci) if masked else None
                    lgs[s] = [_sb_logits(z_ref[h // 2][:, (2 * s + h % 2) * BLK:(2 * s + h % 2 + 1) * BLK], valid)
                              for h in heads]
                    res[s] = _dot(jnp.concatenate([_two_terms(lgs[s][h][1]) for h in heads], axis=0), sums)
                ws = []
                for s, j in enumerate(blocks(t)):
                    valid = _tile_mask(i, j, ri, ci) if masked else None
                    for h in heads:
                        if s == skip:
                            ws.append(jnp.zeros((BLK, BLK), BF16))
                            continue
                        part = res[s][h * BLK:(h + 1) * BLK]
                        w = jnp.exp2(lgs[s][h][0] + part[:, :BLK] + runs[h])
                        ws.append((jnp.where(valid, w, 0.0) if masked else w).astype(BF16))
                        runs[h] = runs[h] + part[:, BLK:]
                for p in pairs:
                    w_ref[p] = jnp.concatenate([ws[s * n_heads + 2 * p + e] for s in range(2) for e in range(2)], axis=1)
                    z_ref[p] = z_next[p]
                o_ref[rows, :] = o_i + jnp.concatenate(outs, axis=1)
                for h in heads:
                    run_ref[h] = runs[h]

            n_calls = (i + 2) // 2

            @pl.when(odd == 1)
            def _():
                tile(0, True, skip=0)

            @pl.when(odd == 0)
            def _():
                tile(0, True)

            def mid(t, carry):
                tile(t, False)
                return carry

            lax.fori_loop(1, n_calls - 1, mid, 0)

            @pl.when(n_calls >= 2)
            def _():
                tile(n_calls - 1, True)

            o_ref[rows, :] += jnp.concatenate([_dot(w_ref[p], stacks(v_ref, blocks(n_calls - 1), p)) for p in pairs], axis=1)
            for h in heads:
                tot_ref[h // HEAD_GROUP, rows, h % HEAD_GROUP:h % HEAD_GROUP + 1] = run_ref[h][:, 0:1]
            return carry

        lax.fori_loop(0, nb, q_block, 0)

        @pl.when(step == groups - 1)
        def _():
            ex.wait(ex_in, ex_out, sems)

    spec = pl.BlockSpec((n_rows, gw), lambda g: (0, g))
    tot_spec = pl.BlockSpec((n_heads // HEAD_GROUP, n_rows, HEAD_GROUP), lambda g: (g, 0, 0))
    res = pl.pallas_call(
        body, name="attn_fwd", grid=(groups,), in_specs=[spec, spec, spec] + ex.specs,
        out_specs=[spec, tot_spec] + ex.specs,
        out_shape=[jax.ShapeDtypeStruct((n_rows, width), F32),
                   jax.ShapeDtypeStruct((width // (HEAD_GROUP * HEAD_DIM), n_rows, HEAD_GROUP), F32)] + ex.out_shape,
        scratch_shapes=[pltpu.VMEM((n_heads, BLK, BLK), F32), pltpu.VMEM((n_heads // 2, BLK, 4 * BLK), F32),
                        pltpu.VMEM((n_heads // 2, BLK, 4 * BLK), BF16)] + ex.scratch,
        compiler_params=_params(n_rows * (3 * gw * 2 + gw * 4 + LANES * 4), ("arbitrary",)),
    )(q, k, v, *ex_arrays)
    return res[0], res[1], res[2:]


def _attn_bwd(q, k, v, keep_total, do, ex, ex_arrays):
    n_rows, width = q.shape
    nb = n_rows // BLK
    gw = HEAD_GROUP * HEAD_DIM
    groups = width // gw

    def body(*refs):
        ((q_ref, k_ref, v_ref, tot_ref, do_ref), ex_in, (dq_ref, dk_ref, dv_ref), ex_out,
         (dq_acc, dk_acc, dv_acc, tot_b, run_ref, rung_ref, z_ref, dw_ref, dz_ref, wb_ref, qst_ref, dost_ref),
         sems) = _ride_along(ex, 5, 3, refs)
        step = pl.program_id(0)

        @pl.when(step == 0)
        def _():
            ex.start(ex_in, ex_out, sems)

        ri = lax.broadcasted_iota(jnp.int32, (BLK, BLK), 0)
        ci = lax.broadcasted_iota(jnp.int32, (BLK, BLK), 1)
        sums_keep = _tri_and_ones((ri <= ci).astype(F32), -1.0)
        sums_g = _tri_and_ones((ri < ci).astype(F32))
        first = ci < HEAD_DIM
        heads, pairs = range(HEAD_GROUP), range(HEAD_GROUP // 2)
        dk_acc[...] = jnp.zeros_like(dk_acc)
        dv_acc[...] = jnp.zeros_like(dv_acc)

        def clamp(j):
            return jnp.clip(j, 0, nb - 1)

        def stacks(ref, blocks, p):
            return jnp.concatenate([_head_stack(ref[_block_rows(clamp(j)), _pair_lanes(p)], first) for j in blocks], axis=0)

        def blocks(t):
            return 2 * t, 2 * t + 1

        def kept(ref, p, axis):
            tiles = [[ref[s * HEAD_GROUP + 2 * p + e] for e in range(2)] for s in range(2)]
            if axis == 1:
                return jnp.concatenate(tiles[0] + tiles[1], axis=1)
            return jnp.concatenate([jnp.concatenate(tiles[s], axis=0) for s in range(2)], axis=1)

        def owed_dq(t_prev):
            return [_dot(kept(dz_ref, p, 1), stacks(k_ref, blocks(t_prev), p)) for p in pairs]

        def owed_dk():
            return [_dot(kept(dz_ref, p, 0), qst_ref[p], TN) for p in pairs]

        def owed_dv():
            return [_dot(kept(wb_ref, p, 0), dost_ref[p], TN) for p in pairs]

        def settle(t_prev, parts):
            dq, dk, dv = parts
            dq_acc[...] += jnp.concatenate(dq, axis=1)
            for s, j in enumerate(blocks(t_prev)):
                cols = _block_rows(clamp(j))
                dk_acc[cols, :] += jnp.concatenate([d[s * BLK:(s + 1) * BLK] for d in dk], axis=1)
                dv_acc[cols, :] += jnp.concatenate([d[s * BLK:(s + 1) * BLK] for d in dv], axis=1)

        def q_block(i, carry):
            rows = _block_rows(i)
            dq_acc[...] = jnp.zeros_like(dq_acc)
            run_ref[...] = jnp.zeros_like(run_ref)
            rung_ref[...] = jnp.zeros_like(rung_ref)
            dz_ref[...] = jnp.zeros_like(dz_ref)
            wb_ref[...] = jnp.zeros_like(wb_ref)
            for h in heads:
                tot_b[h] = jnp.broadcast_to(tot_ref[rows, h:h + 1], (BLK, BLK))
            for p in pairs:
                qst_ref[p] = _head_stack(q_ref[rows, _pair_lanes(p)], first)
                dost_ref[p] = _head_stack(do_ref[rows, _pair_lanes(p)], first)
                z_ref[p] = _dot(q_ref[rows, _pair_lanes(p)], stacks(k_ref, blocks(0), p), NT)
                dw_ref[p] = _dot(do_ref[rows, _pair_lanes(p)], stacks(v_ref, blocks(0), p), NT)

            def tile(t, masked, skip=None):
                q_i, do_i = q_ref[rows, :], do_ref[rows, :]
                t_prev = jnp.maximum(t - 1, 0)
                valids = [_tile_mask(i, j, ri, ci) if masked else None for j in blocks(t)]
                tile_of = lambda ref, s, h: ref[h // 2][:, (2 * s + h % 2) * BLK:(2 * s + h % 2 + 1) * BLK]
                nothing = [jnp.zeros((BLK, BLK), F32)] * HEAD_GROUP
                runs = [run_ref[h] for h in heads]
                rungs = [rung_ref[h] for h in heads]
                lgs, keep, ws, gs, gsum, dzs = {}, {}, {}, {}, {}, {}
                for s in range(2):
                    if s != skip:
                        lgs[s] = [_sb_logits(tile_of(z_ref, s, h), valids[s]) for h in heads]
                        keep[s] = _dot(jnp.concatenate([_two_terms(lgs[s][h][1]) for h in heads], axis=0), sums_keep)
                    if s == 0:
                        part_dq = owed_dq(t_prev)
                    else:
                        part_dk = owed_dk()
                for s in range(2):
                    ws[s], gs[s] = [], []
                    if s != skip:
                        for h in heads:
                            part = keep[s][h * BLK:(h + 1) * BLK]
                            w = jnp.exp2(lgs[s][h][0] + (tot_b[h] - runs[h] - part[:, :BLK]))
                            ws[s].append(jnp.where(valids[s], w, 0.0) if masked else w)
                            runs[h] = runs[h] + part[:, BLK:]
                            gs[s].append(tile_of(dw_ref, s, h) * ws[s][h])
                        gsum[s] = _dot(jnp.concatenate([_two_terms(gs[s][h]) for h in heads], axis=0), sums_g)
                    else:
                        ws[s] = nothing
                    if s == 0:
                        part_dv = owed_dv()
                    else:
                        z_next = [_dot(q_i[:, _pair_lanes(p)], stacks(k_ref, blocks(t + 1), p), NT) for p in pairs]
                for s in range(2):
                    dzs[s] = []
                    if s != skip:
                        for h in heads:
                            part = gsum[s][h * BLK:(h + 1) * BLK]
                            beta = jnp.exp2(lgs[s][h][0])
                            dz = gs[s][h] * (1.0 - beta) - beta * (part[:, :BLK] + rungs[h])
                            dzs[s].append(jnp.where(valids[s], dz, 0.0) if masked else dz)
                            rungs[h] = rungs[h] + part[:, BLK:]
                    else:
                        dzs[s] = nothing
                    if s == 0:
                        dw_next = [_dot(do_i[:, _pair_lanes(p)], stacks(v_ref, blocks(t + 1), p), NT) for p in pairs]
                settle(t_prev, (part_dq, part_dk, part_dv))
                for s in range(2):
                    for h in heads:
                        dz_ref[s * HEAD_GROUP + h] = dzs[s][h].astype(BF16)
                        wb_ref[s * HEAD_GROUP + h] = ws[s][h].astype(BF16)
                for h in heads:
                    run_ref[h] = runs[h]
                    rung_ref[h] = rungs[h]
                for p in pairs:
                    z_ref[p] = z_next[p]
                    dw_ref[p] = dw_next[p]

            n_calls = (i + 2) // 2
            tile(0, True)

            def mid(t, carry):
                tile(t, False)
                return carry

            lax.fori_loop(1, n_calls - 1, mid, 0)

            @pl.when((n_calls >= 2) & (i % 2 == 1))
            def _():
                tile(n_calls - 1, True)

            @pl.when((n_calls >= 2) & (i % 2 == 0))
            def _():
                tile(n_calls - 1, True, skip=1)

            settle(n_calls - 1, (owed_dq(n_calls - 1), owed_dk(), owed_dv()))
            dq_ref[rows, :] = (dq_acc[...] * SB_SCALE).astype(dq_ref.dtype)
            return carry

        lax.fori_loop(0, nb, q_block, 0)
        dk_ref[...] = dk_acc[...].astype(dk_ref.dtype)
        dv_ref[...] = dv_acc[...].astype(dv_ref.dtype)

        @pl.when(step == groups - 1)
        def _():
            ex.wait(ex_in, ex_out, sems)

    spec = pl.BlockSpec((n_rows, gw), lambda g: (0, g))
    tot_spec = pl.BlockSpec((None, n_rows, HEAD_GROUP), lambda g: (g, 0, 0))
    out = jax.ShapeDtypeStruct((n_rows, width), BF16)
    tile_f32 = pltpu.VMEM((HEAD_GROUP, BLK, BLK), F32)
    tile_bf16 = pltpu.VMEM((2 * HEAD_GROUP, BLK, BLK), BF16)
    pair_f32 = pltpu.VMEM((HEAD_GROUP // 2, BLK, 4 * BLK), F32)
    pair_stack = pltpu.VMEM((HEAD_GROUP // 2, 2 * BLK, BLK), BF16)
    res = pl.pallas_call(
        body, name="attn_bwd", grid=(groups,), in_specs=[spec, spec, spec, tot_spec, spec] + ex.specs,
        out_specs=[spec] * 3 + ex.specs, out_shape=[out] * 3 + ex.out_shape,
        scratch_shapes=[pltpu.VMEM((BLK, gw), F32), pltpu.VMEM((n_rows, gw), F32), pltpu.VMEM((n_rows, gw), F32),
                        tile_f32, tile_f32, tile_f32, pair_f32, pair_f32, tile_bf16, tile_bf16, pair_stack,
                        pair_stack] + ex.scratch,
        compiler_params=_params(n_rows * (7 * gw * 2 + LANES * 4 + gw * 4), ("arbitrary",)),
    )(q, k, v, keep_total, do, *ex_arrays)
    return res[0], res[1], res[2], res[3:]


class _Exchange:
    def __init__(self, arrays, gather, same_core=False):
        self.n = len(arrays)
        self.gather = gather
        self.same_core = same_core
        self.out_shape = [jax.ShapeDtypeStruct(((N_DEV,) + a.shape) if gather else a.shape, a.dtype) for a in arrays]
        self.scratch = [pltpu.SemaphoreType.DMA((self.n, N_DEV - 1)), pltpu.SemaphoreType.DMA((self.n, N_DEV - 1)),
                        pltpu.SemaphoreType.DMA((self.n,))]
        self.specs = [pl.BlockSpec(memory_space=pl.ANY)] * self.n

    def _copies(self, ins, outs, sems, with_receives):
        send_sems, recv_sems, local_sems = sems
        x, y, c = lax.axis_index("x"), lax.axis_index("y"), lax.axis_index("c")
        gather, same_core = self.gather, self.same_core
        me = 2 * x + y if same_core else 4 * x + 2 * y + c
        local, sends, recvs = [], [], []
        for a in range(self.n):
            local.append(pltpu.make_async_copy(ins[a] if gather else ins[a].at[me], outs[a].at[me if gather else 0],
                                               local_sems.at[a]))
        for r in ((2, 4, 6) if same_core else range(1, N_DEV)):
            px = 1 - x if r & 4 else x
            py = 1 - y if r & 2 else y
            pc = 1 - c if r & 1 else c
            idx = 2 * px + py if same_core else 4 * px + 2 * py + pc
            slot = r // 2 if same_core else r
            for a in range(self.n):
                src = ins[a] if gather else ins[a].at[idx]
                pair = dict(send_sem=send_sems.at[a, r - 1], recv_sem=recv_sems.at[a, r - 1],
                            device_id=(px, py, pc), device_id_type=pl.DeviceIdType.MESH)
                sends.append(pltpu.make_async_remote_copy(src_ref=src, dst_ref=outs[a].at[me if gather else slot], **pair))
                if with_receives:
                    recvs.append(pltpu.make_async_remote_copy(src_ref=src, dst_ref=outs[a].at[idx if gather else slot], **pair))
        return local, sends, recvs

    def start(self, ins, outs, sems):
        local, sends, _ = self._copies(ins, outs, sems, with_receives=False)
        for cp in local + sends:
            cp.start()

    def wait(self, ins, outs, sems):
        local, sends, recvs = self._copies(ins, outs, sems, with_receives=True)
        for cp in recvs:
            cp.wait_recv()
        for cp in sends:
            cp.wait_send()
        for cp in local:
            cp.wait()


def _peer_exchange(name, parts):
    exs = [_Exchange(arrays, gather) for arrays, gather in parts]
    n = sum(ex.n for ex in exs)
    n_sems = len(exs[0].scratch)

    def body(*refs):
        at, views = 0, []
        for k, ex in enumerate(exs):
            views.append((ex, refs[at:at + ex.n], refs[n + at:n + at + ex.n],
                          refs[2 * n + k * n_sems:2 * n + (k + 1) * n_sems]))
            at += ex.n
        for ex, ins, outs, sems in views:
            ex.start(ins, outs, sems)
        for ex, ins, outs, sems in views:
            ex.wait(ins, outs, sems)

    res = pl.pallas_call(
        body, name=name, in_specs=[s for ex in exs for s in ex.specs], out_specs=[s for ex in exs for s in ex.specs],
        out_shape=[s for ex in exs for s in ex.out_shape], scratch_shapes=[s for ex in exs for s in ex.scratch],
    )(*[a for arrays, _ in parts for a in arrays])
    out, at = [], 0
    for ex in exs:
        out.append(res[at:at + ex.n])
        at += ex.n
    return out


def _gather_two_level(name, arrays):
    n = len(arrays)

    def body(*refs):
        ins, outs = refs[:n], refs[n:2 * n]
        send_sems, recv_sems, local_sems = refs[2 * n:]
        x, y, c = lax.axis_index("x"), lax.axis_index("y"), lax.axis_index("c")
        sibling = (x, y, 1 - c)
        chips = [(1 - x, y), (x, 1 - y), (1 - x, 1 - y)]

        def slot(a, dev):
            return outs[a].at[4 * dev[0] + 2 * dev[1] + dev[2]]

        def copy(a, k, block, to, src=None):
            return pltpu.make_async_remote_copy(
                src_ref=slot(a, block) if src is None else src, dst_ref=slot(a, block),
                send_sem=send_sems.at[a, k], recv_sem=recv_sems.at[a, k], device_id=to, device_id_type=pl.DeviceIdType.MESH)

        me = (x, y, c)
        mine = [pltpu.make_async_copy(ins[a], slot(a, me), local_sems.at[a]) for a in range(n)]
        first = [copy(a, 0, me, sibling, src=ins[a]) for a in range(n)]
        first += [copy(a, 1 + j, me, (*chip, c), src=ins[a]) for j, chip in enumerate(chips) for a in range(n)]
        for cp in mine + first:
            cp.start()
        passed = []
        for j, chip in enumerate(chips):
            for a in range(n):
                copy(a, 1 + j, (*chip, c), me).wait_recv()
            for a in range(n):
                cp = copy(a, 4 + j, (*chip, c), sibling)
                cp.start()
                passed.append(cp)
        for a in range(n):
            copy(a, 0, sibling, me).wait_recv()
            for j, chip in enumerate(chips):
                copy(a, 4 + j, (*chip, 1 - c), me).wait_recv()
        for cp in first + passed:
            cp.wait_send()
        for cp in mine:
            cp.wait()

    any_spec = pl.BlockSpec(memory_space=pl.ANY)
    return pl.pallas_call(
        body, name=name, in_specs=[any_spec] * n, out_specs=[any_spec] * n,
        out_shape=[jax.ShapeDtypeStruct((N_DEV,) + a.shape, a.dtype) for a in arrays],
        scratch_shapes=[pltpu.SemaphoreType.DMA((n, N_DEV - 1)), pltpu.SemaphoreType.DMA((n, N_DEV - 1)),
                        pltpu.SemaphoreType.DMA((n,))],
    )(*arrays)


def _sibling_swap(name, slabs):
    chips = N_DEV // 2

    def body(x_ref, o_ref, send_sems, recv_sems):
        x, y, c = lax.axis_index("x"), lax.axis_index("y"), lax.axis_index("c")
        copies = [pltpu.make_async_remote_copy(
            src_ref=x_ref.at[2 * m + 1 - c], dst_ref=o_ref.at[m], send_sem=send_sems.at[m], recv_sem=recv_sems.at[m],
            device_id=(x, y, 1 - c), device_id_type=pl.DeviceIdType.MESH) for m in range(chips)]
        for cp in copies:
            cp.start()
        for cp in copies:
            cp.wait()

    any_spec = pl.BlockSpec(memory_space=pl.ANY)
    return pl.pallas_call(
        body, name=name, in_specs=[any_spec], out_specs=any_spec,
        out_shape=jax.ShapeDtypeStruct((chips,) + slabs.shape[1:], slabs.dtype),
        scratch_shapes=[pltpu.SemaphoreType.DMA((chips,)), pltpu.SemaphoreType.DMA((chips,))],
    )(slabs)


def _pair_sum(name, a, b):
    rows, cols = a.shape
    tr = 512

    def body(a_ref, b_ref, o_ref):
        o_ref[...] = (a_ref[...].astype(F32) + b_ref[...].astype(F32)).astype(o_ref.dtype)

    spec = pl.BlockSpec((tr, cols), lambda i: (i, 0))
    return pl.pallas_call(body, name=name, grid=(rows // tr,), in_specs=[spec, spec], out_specs=spec,
                          out_shape=jax.ShapeDtypeStruct(a.shape, a.dtype),
                          compiler_params=_params(3 * tr * cols * 4, ("parallel",)))(a, b)


def _sum_slots(name, x):
    def body(x_ref, o_ref):
        acc = x_ref[0]
        for s in range(1, N_DEV):
            acc = acc + x_ref[s]
        o_ref[...] = acc
    return pl.pallas_call(body, name=name, out_shape=jax.ShapeDtypeStruct(x.shape[1:], F32))(x)


def _adamw(name, w, slots, m, v):
    n_slots, rows, cols = slots.shape
    tr = next((t for t in (256, 176, 128) if rows % t == 0 and rows > t), rows)

    def body(w_ref, s_ref, m_ref, v_ref, g_ref, d_ref, nm_ref, nv_ref):
        g = s_ref[0].astype(F32)
        for s in range(1, n_slots):
            g = g + s_ref[s].astype(F32)
        nm = ADAM_B1 * m_ref[...] + (1.0 - ADAM_B1) * g
        nv = ADAM_B2 * v_ref[...] + (1.0 - ADAM_B2) * (g * g)
        m_hat = nm / (1.0 - ADAM_B1 ** ADAM_STEP)
        v_hat = nv / (1.0 - ADAM_B2 ** ADAM_STEP)
        g_ref[...] = g
        d_ref[...] = -ADAM_LR * (m_hat / (jnp.sqrt(v_hat) + ADAM_EPS) + ADAM_WD * w_ref[...])
        nm_ref[...] = nm
        nv_ref[...] = nv

    spec = pl.BlockSpec((tr, cols), lambda i: (i, 0))
    out = jax.ShapeDtypeStruct((rows, cols), F32)
    return pl.pallas_call(
        body, name=name, grid=(rows // tr,),
        in_specs=[spec, pl.BlockSpec((n_slots, tr, cols), lambda i: (0, i, 0)), spec, spec],
        out_specs=[spec] * 4, out_shape=[out] * 4,
        compiler_params=_params((n_slots + 7) * tr * cols * 4, ("parallel",)),
    )(w, slots, m, v)


def _pad_lanes(a):
    return jnp.pad(a, ((0, 0), (0, LANES - a.shape[1])))


def kernel(x, meta_tokens, mix_pre_g, w_in, ssd_conv_w, ssd_conv_b, ssd_dt_bias, ssd_a_log, ssd_d, ssd_norm_g, sb_norm_g, w_out, mix_post_g, ffn_pre_g, w_up, ffn_conv_w, ffn_conv_b, w_down, ffn_post_g, loss_target, m_meta_tokens, m_mix_pre_g, m_w_in, m_ssd_conv_w, m_ssd_conv_b, m_ssd_dt_bias, m_ssd_a_log, m_ssd_d, m_ssd_norm_g, m_sb_norm_g, m_w_out, m_mix_post_g, m_ffn_pre_g, m_w_up, m_ffn_conv_w, m_ffn_conv_b, m_w_down, m_ffn_post_g, v_meta_tokens, v_mix_pre_g, v_w_in, v_ssd_conv_w, v_ssd_conv_b, v_ssd_dt_bias, v_ssd_a_log, v_ssd_d, v_ssd_norm_g, v_sb_norm_g, v_w_out, v_mix_post_g, v_ffn_pre_g, v_w_up, v_ffn_conv_w, v_ffn_conv_b, v_w_down, v_ffn_post_g):
    seq = x.shape[1]
    me = 4 * lax.axis_index("x") + 2 * lax.axis_index("y") + lax.axis_index("c")
    in_cols = w_in.shape[2]
    up_cols = w_up.shape[2]
    out_rows = w_out.shape[1]
    down_rows = w_down.shape[1]

    g_in, g_meta, g_scw, g_fcw = _gather_two_level(
        "gather_w_in", [w_in[0].astype(BF16), meta_tokens, ssd_conv_w[0], ffn_conv_w[0]])
    late_weights = [w_out[0].astype(BF16), w_up[0].astype(BF16), w_down[0].astype(BF16)]
    w_in_full = g_in.transpose(1, 0, 2).reshape(D_MODEL, N_DEV * in_cols)
    off = [0, SSD_INNER, SSD_INNER + XBC, SSD_INNER + XBC + HEADS]
    w_z = w_in_full[:, off[0]:off[1]]
    w_xbc = w_in_full[:, off[1]:off[2]]
    w_dt = _pad_lanes(w_in_full[:, off[2]:off[3]])
    w_q = w_in_full[:, off[3]:off[3] + SSD_INNER]
    w_k = w_in_full[:, off[3] + SSD_INNER:off[3] + 2 * SSD_INNER]
    w_v = w_in_full[:, off[3] + 2 * SSD_INNER:off[3] + 3 * SSD_INNER]
    meta_full = g_meta.transpose(1, 0, 2).reshape(N_META, D_MODEL)
    scw_full = g_scw.transpose(1, 0, 2).reshape(SSD_CONV, XBC)
    fcw_full = g_fcw.transpose(1, 0, 2).reshape(FFN_CONV, D_FF)

    dt_bias_p, a_log_p, d_p = _pad_lanes(ssd_dt_bias), _pad_lanes(ssd_a_log), _pad_lanes(ssd_d)

    h0 = jnp.concatenate([jnp.zeros((PAD, D_MODEL), F32), meta_full, x[0]], axis=0)
    target = loss_target[0]
    xn1 = _rms_fwd("rms_pre_mix", h0, mix_pre_g)
    z = _mm("proj_z", [(xn1, w_z)], "nn", F32)
    xbc_raw = _mm("proj_xbc", [(xn1, w_xbc)], "nn", F32)
    dtr = _mm("proj_dt", [(xn1, w_dt)], "nn", F32)
    q = _mm("proj_q", [(xn1, w_q * SB_SCALE)], "nn", BF16)
    k = _mm("proj_k", [(xn1, w_k)], "nn", BF16)
    v = _mm("proj_v", [(xn1, w_v)], "nn", BF16)
    xbc_act = _ssd_conv_fwd(xbc_raw, scw_full, ssd_conv_b)
    ypre, y_ssd, states = _ssd_fwd(xbc_act, dtr, z, dt_bias_p, a_log_p, d_p, ssd_norm_g)
    o, keep_total, (g_out, g_up, g_down) = _attn_fwd(q, k, v, _Exchange(late_weights, gather=True), late_weights)
    w_out_full = g_out.reshape(N_DEV * out_rows, D_MODEL)
    wo_ssd, wo_sb = w_out_full[:SSD_INNER], w_out_full[SSD_INNER:]
    w_up_full = g_up.transpose(1, 0, 2).reshape(D_MODEL, N_DEV * up_cols)
    w_gate, w_lin = w_up_full[:, :D_FF], w_up_full[:, D_FF:]
    w_down_full = g_down.reshape(N_DEV * down_rows, D_MODEL)
    y_sb = _rms_fwd("rms_sb", o, sb_norm_g)
    mix = _mm("mix_out", [(y_ssd, wo_ssd), (y_sb, wo_sb)], "nn", F32)
    h1, xn2 = _mix_post(mix, h0, mix_post_g, ffn_pre_g)
    g_raw = _mm("ffn_gate", [(xn2, w_gate)], "nn", F32)
    u = _mm("ffn_lin", [(xn2, w_lin)], "nn", F32)
    act = _ffn_act(g_raw, u, fcw_full, ffn_conv_b)
    f = _mm("ffn_down", [(act, w_down_full)], "nn", F32)
    dh2, df, loss_row, dg_ffn_post = _loss_post(f, h1, target, ffn_post_g)

    dact = _mm("d_act", [(df, w_down_full)], "nt", F32)
    dw_down = _mm("dw_down", [(act, df)], "tn", F32)
    dg_conv, du = _ffn_bwd_act(dact, u, g_raw, fcw_full, ffn_conv_b)
    dg_raw, dfcw0, dfcw1, dfcw2, dfcb = _conv_bwd("ffn_conv_bwd", dg_conv, g_raw, fcw_full, FFN_CONV)
    dxn2 = _mm("d_xn2", [(dg_raw, w_gate), (du, w_lin)], "nt", F32)
    dw_gate = _mm("dw_gate", [(xn2, dg_raw)], "tn", F32)
    dw_lin = _mm("dw_lin", [(xn2, du)], "tn", F32)
    dh1, dmix, dg_ffn_pre, dg_mix_post = _mid_bwd(dxn2, h1, dh2, mix, ffn_pre_g, mix_post_g)

    dy_ssd = _mm("d_yssd", [(dmix, wo_ssd)], "nt", F32)
    dy_sb = _mm("d_ysb", [(dmix, wo_sb)], "nt", F32)
    dwo_ssd = _mm("dw_out_ssd", [(y_ssd, dmix)], "tn", F32)
    dwo_sb = _mm("dw_out_sb", [(y_sb, dmix)], "tn", F32)
    do, dg_sb = _norm_bwd("sb_norm_bwd", dy_sb, o, sb_norm_g)
    half = N_DEV // 2
    early_slabs = [
        jnp.concatenate([dwo_ssd, dwo_sb], axis=0).reshape(N_DEV, out_rows, D_MODEL),
        jnp.concatenate([dw_gate.reshape(D_MODEL, half, up_cols).transpose(1, 0, 2),
                         dw_lin.reshape(D_MODEL, half, up_cols).transpose(1, 0, 2)], axis=0),
        dw_down.reshape(N_DEV, down_rows, D_MODEL)]
    dq, dk, dv, (l_out, l_up, l_down) = _attn_bwd(q, k, v, keep_total, do, _Exchange(early_slabs, gather=False), early_slabs)
    dz, dxbc_act, ddtr, dg_ssd_norm, dd_skip, da_log, ddt_bias = _ssd_bwd(
        dy_ssd, ypre, z, xbc_act, dtr, states, dt_bias_p, a_log_p, d_p, ssd_norm_g)
    dconv = _ssd_conv_bwd_act(dxbc_act, xbc_raw, scw_full, ssd_conv_b)
    dxbc_raw, dscw0, dscw1, dscw2, dscw3, dscb = _conv_bwd("ssd_conv_bwd", dconv, xbc_raw, scw_full, SSD_CONV)
    segs = [(dz, w_z), (dxbc_raw, w_xbc), (ddtr, w_dt), (dq, w_q), (dk, w_k), (dv, w_v)]
    dw_segs = [_mm("dw_in_%d" % s, [(xn1, d)], "tn", BF16) for s, (d, _) in enumerate(segs)]
    dw_segs[2] = dw_segs[2][:, :HEADS]
    dw_in = jnp.concatenate(dw_segs, axis=1)
    chips = N_DEV // 2
    slab_in = dw_in.reshape(D_MODEL, N_DEV, in_cols).transpose(1, 0, 2)
    from_sibling = _sibling_swap("swap_dw_in", slab_in)
    mine = lax.dynamic_index_in_dim(slab_in.reshape(chips, 2, D_MODEL, in_cols), lax.axis_index("c"), 1, keepdims=False)
    pair = _pair_sum("pair_sum_dw_in", mine.reshape(chips * D_MODEL, in_cols),
                     from_sibling.reshape(chips * D_MODEL, in_cols)).reshape(chips, D_MODEL, in_cols)
    dxn1, (l_in,) = _mm("d_xn1", segs, "nt", F32, _Exchange([pair], gather=False, same_core=True), [pair])
    dh0, dg_mix_pre = _first_bwd(dxn1, h0, dh1, mix_pre_g)
    grad_x = dh0[BLK:][None]

    small = [dg_mix_pre, dscb, ddt_bias, da_log, dd_skip, dg_ssd_norm, dg_sb, dg_mix_post, dg_ffn_pre, dfcb,
             dg_ffn_post, dh0[PAD:BLK].reshape(1, -1), dscw0, dscw1, dscw2, dscw3, dfcw0, dfcw1, dfcw2, loss_row]
    sizes = [a.shape[1] for a in small]
    total = sum(sizes)
    rows_packed = -(-total // (LANES * HALO)) * HALO
    packed = jnp.pad(jnp.concatenate(small, axis=1), ((0, 0), (0, rows_packed * LANES - total)))
    ((gathered,),) = _peer_exchange("gather_small_grads", [([packed.reshape(rows_packed, LANES)], True)])
    summed = _sum_slots("sum_small_grads", gathered).reshape(1, rows_packed * LANES)
    pieces, at = [], 0
    for s in sizes:
        pieces.append(summed[:, at:at + s])
        at += s
    (g_mix_pre, g_scb, g_dtb, g_alog, g_dskip, g_ssd_norm, g_sb, g_mix_post, g_ffn_pre, g_fcb, g_ffn_post,
     g_meta_flat, gs0, gs1, gs2, gs3, gf0, gf1, gf2, loss_all) = pieces
    loss = loss_all[0, 0]
    g_dtb, g_alog, g_dskip = g_dtb[:, :HEADS], g_alog[:, :HEADS], g_dskip[:, :HEADS]
    g_meta_full = g_meta_flat.reshape(N_META, D_MODEL)
    g_scw_full = jnp.concatenate([gs0, gs1, gs2, gs3], axis=0)
    g_fcw_full = jnp.concatenate([gf0, gf1, gf2], axis=0)
    meta_cols, scw_cols, fcw_cols = meta_tokens.shape[1], ssd_conv_w.shape[2], ffn_conv_w.shape[2]
    g_meta_mine = lax.dynamic_slice(g_meta_full, (0, me * meta_cols), (N_META, meta_cols))
    g_scw_mine = lax.dynamic_slice(g_scw_full, (0, me * scw_cols), (SSD_CONV, scw_cols))
    g_fcw_mine = lax.dynamic_slice(g_fcw_full, (0, me * fcw_cols), (FFN_CONV, fcw_cols))

    def lead(a):
        return a[None]

    upd = [
        _adamw("adamw_meta", meta_tokens, lead(g_meta_mine), m_meta_tokens, v_meta_tokens),
        _adamw("adamw_mix_pre_g", mix_pre_g, lead(g_mix_pre), m_mix_pre_g, v_mix_pre_g),
        [lead(a) for a in _adamw("adamw_w_in", w_in[0], l_in, m_w_in[0], v_w_in[0])],
        [lead(a) for a in _adamw("adamw_ssd_conv_w", ssd_conv_w[0], lead(g_scw_mine), m_ssd_conv_w[0], v_ssd_conv_w[0])],
        _adamw("adamw_ssd_conv_b", ssd_conv_b, lead(g_scb), m_ssd_conv_b, v_ssd_conv_b),
        _adamw("adamw_ssd_dt_bias", ssd_dt_bias, lead(g_dtb), m_ssd_dt_bias, v_ssd_dt_bias),
        _adamw("adamw_ssd_a_log", ssd_a_log, lead(g_alog), m_ssd_a_log, v_ssd_a_log),
        _adamw("adamw_ssd_d", ssd_d, lead(g_dskip), m_ssd_d, v_ssd_d),
        _adamw("adamw_ssd_norm_g", ssd_norm_g, lead(g_ssd_norm), m_ssd_norm_g, v_ssd_norm_g),
        _adamw("adamw_sb_norm_g", sb_norm_g, lead(g_sb), m_sb_norm_g, v_sb_norm_g),
        [lead(a) for a in _adamw("adamw_w_out", w_out[0], l_out, m_w_out[0], v_w_out[0])],
        _adamw("adamw_mix_post_g", mix_post_g, lead(g_mix_post), m_mix_post_g, v_mix_post_g),
        _adamw("adamw_ffn_pre_g", ffn_pre_g, lead(g_ffn_pre), m_ffn_pre_g, v_ffn_pre_g),
        [lead(a) for a in _adamw("adamw_w_up", w_up[0], l_up, m_w_up[0], v_w_up[0])],
        [lead(a) for a in _adamw("adamw_ffn_conv_w", ffn_conv_w[0], lead(g_fcw_mine), m_ffn_conv_w[0], v_ffn_conv_w[0])],
        _adamw("adamw_ffn_conv_b", ffn_conv_b, lead(g_fcb), m_ffn_conv_b, v_ffn_conv_b),
        [lead(a) for a in _adamw("adamw_w_down", w_down[0], l_down, m_w_down[0], v_w_down[0])],
        _adamw("adamw_ffn_post_g", ffn_post_g, lead(g_ffn_post), m_ffn_post_g, v_ffn_post_g),
    ]
    grads = [u_[0] for u_ in upd]
    deltas = [u_[1] for u_ in upd]
    new_m = [u_[2] for u_ in upd]
    new_v = [u_[3] for u_ in upd]
    return (loss, grad_x, *grads, *deltas, *new_m, *new_v)
```

```python
import math

import jax
import jax.numpy as jnp
from jax import lax
from jax.experimental import pallas as pl
from jax.experimental.pallas import tpu as pltpu

F32 = jnp.float32
BF16 = jnp.bfloat16
HI = lax.Precision.HIGHEST

D_MODEL = 1024
N_META = 16
BLK = 128
PAD = BLK - N_META
HEADS = 16
HEAD_DIM = 64
SSD_GROUPS = 2
SSD_STATE = 128
HEADS_PER_GROUP = HEADS // SSD_GROUPS
SSD_INNER = HEADS * HEAD_DIM
SSD_CONV = 4
XBC = SSD_INNER + 2 * SSD_GROUPS * SSD_STATE
OFF_B = SSD_INNER
OFF_C = SSD_INNER + SSD_GROUPS * SSD_STATE
D_FF = 2816
FFN_CONV = 3
EPS = 1e-6
SB_SCALE = 1.0 / math.sqrt(HEAD_DIM)
N_DEV = 8
LANES = 128
HALO = 8

ADAM_LR = 0.001
ADAM_B1 = 0.9
ADAM_B2 = 0.999
ADAM_EPS = 1e-08
ADAM_WD = 0.01
ADAM_STEP = 10

VMEM_FLOOR = 32 << 20
VMEM_CEIL = 60 << 20
MM_BUDGET = 20 << 20
ROW_BUDGET = 6 << 20

NN = (((1,), (0,)), ((), ()))
NT = (((1,), (1,)), ((), ()))
TN = (((0,), (0,)), ((), ()))


def _params(tile_bytes, sem=None):
    limit = int(min(max(2 * tile_bytes + (8 << 20), VMEM_FLOOR), VMEM_CEIL))
    return pltpu.CompilerParams(vmem_limit_bytes=limit, dimension_semantics=sem)


def _nbytes(shape, dtype):
    n = 1
    for s in shape:
        n *= s
    return n * jnp.dtype(dtype).itemsize


def _dot(a, b, dims=NN, precision=None):
    return lax.dot_general(a, b, dims, precision=precision, preferred_element_type=F32)


def _softplus(x):
    return jnp.maximum(x, 0.0) + jnp.log1p(jnp.exp(-jnp.abs(x)))


def _rms(x, g):
    r = lax.rsqrt(jnp.mean(x * x, axis=-1, keepdims=True) + EPS)
    return x * r * g


def _rms_bwd(dy, x, g):
    r = lax.rsqrt(jnp.mean(x * x, axis=-1, keepdims=True) + EPS)
    xh = x * r
    u = dy * g
    dx = r * (u - xh * jnp.mean(xh * u, axis=-1, keepdims=True))
    return dx, jnp.sum(dy * xh, axis=0, keepdims=True)


def _gelu(x):
    c = math.sqrt(2.0 / math.pi)
    return 0.5 * x * (1.0 + jnp.tanh(c * (x + 0.044715 * x * x * x)))


def _gelu_and_grad(x):
    c = math.sqrt(2.0 / math.pi)
    x2 = x * x
    t = jnp.tanh(c * (x + 0.044715 * x2 * x))
    half = 0.5 * (1.0 + t)
    return x * half, half + 0.5 * x * (1.0 - t * t) * c * (1.0 + 3.0 * 0.044715 * x2)


def _row_tile(rows, bytes_per_row):
    big = 384
    return big if rows % big == 0 and big * bytes_per_row <= ROW_BUDGET else BLK


def _mm(name, pairs, mode, out_dtype, ex=None, ex_arrays=()):
    a0, b0 = pairs[0]
    if mode == "tn":
        m, n = a0.shape[1], b0.shape[1]
    elif mode == "nt":
        m, n = a0.shape[0], b0.shape[0]
    else:
        m, n = a0.shape[0], b0.shape[1]
    dims = {"nn": NN, "nt": NT, "tn": TN}[mode]

    def tile_bytes(tm, tn):
        tot = tm * tn * jnp.dtype(out_dtype).itemsize
        for a, b in pairs:
            k = a.shape[0] if mode == "tn" else a.shape[1]
            tot += tm * k * a.dtype.itemsize + tn * k * b.dtype.itemsize
        return tot

    cands_m = [t for t in (1408, 1024, 512, 384, 256, 128) if m % t == 0] or [m]
    cands_n = [t for t in (1408, 1024, 768, 512, 256, 128) if n % t == 0] or [n]
    best = None
    for tm in cands_m:
        for tn in cands_n:
            if tile_bytes(tm, tn) <= MM_BUDGET and (best is None or tm * tn > best[0] * best[1]):
                best = (tm, tn)
    tm, tn = best if best is not None else (cands_m[-1], cands_n[-1])
    npairs = len(pairs)

    grid = (m // tm, n // tn)

    def body(*refs):
        if ex is None:
            ins, o_ref = refs[:2 * npairs], refs[2 * npairs]
        else:
            ins, ex_in, (o_ref,), ex_out, _, sems = _ride_along(ex, 2 * npairs, 1, refs)
            step = pl.program_id(0) * grid[1] + pl.program_id(1)

            @pl.when(step == 0)
            def _():
                ex.start(ex_in, ex_out, sems)

        acc = None
        for p in range(npairs):
            part = _dot(ins[2 * p][...], ins[2 * p + 1][...], dims)
            acc = part if acc is None else acc + part
        o_ref[...] = acc.astype(o_ref.dtype)
        if ex is not None:
            @pl.when(step == grid[0] * grid[1] - 1)
            def _():
                ex.wait(ex_in, ex_out, sems)

    in_specs, args = [], []
    for a, b in pairs:
        if mode == "tn":
            k = a.shape[0]
            in_specs += [pl.BlockSpec((k, tm), lambda i, j: (0, i)), pl.BlockSpec((k, tn), lambda i, j: (0, j))]
        elif mode == "nt":
            k = a.shape[1]
            in_specs += [pl.BlockSpec((tm, k), lambda i, j: (i, 0)), pl.BlockSpec((tn, k), lambda i, j: (j, 0))]
        else:
            k = a.shape[1]
            in_specs += [pl.BlockSpec((tm, k), lambda i, j: (i, 0)), pl.BlockSpec((k, tn), lambda i, j: (0, j))]
        args += [a, b]
    out_spec = pl.BlockSpec((tm, tn), lambda i, j: (i, j))
    out_shape = jax.ShapeDtypeStruct((m, n), out_dtype)
    if ex is None:
        return pl.pallas_call(
            body, name=name, grid=grid, in_specs=in_specs, out_specs=out_spec, out_shape=out_shape,
            compiler_params=_params(tile_bytes(tm, tn), ("parallel", "parallel")),
        )(*args)
    res = pl.pallas_call(
        body, name=name, grid=grid, in_specs=in_specs + ex.specs, out_specs=[out_spec] + ex.specs,
        out_shape=[out_shape] + ex.out_shape, scratch_shapes=ex.scratch,
        compiler_params=_params(tile_bytes(tm, tn), ("arbitrary", "arbitrary")),
    )(*args, *ex_arrays)
    return res[0], res[1:]


class _Window:
    def __init__(self, ref, cols):
        self.ref, self.cols = ref, cols

    def __getitem__(self, idx):
        return self.ref[:, self.cols] if idx is Ellipsis else self.ref[idx[0], self.cols]


def _rowcall(name, fn, rows=(), prevs=(), nexts=(), pars=(), out_rows=(), out_accs=(), chunk=None, late=()):
    rows, prevs, nexts, pars, late = list(rows), list(prevs), list(nexts), list(pars), list(late)
    n_rows = (rows + prevs + nexts)[0].shape[0]
    per_row = sum(a.shape[1] * a.dtype.itemsize for a in rows + prevs + nexts + late)
    per_row += sum(c * jnp.dtype(dt).itemsize for c, dt in out_rows) + sum(a.shape[1] * 4 for a in prevs + nexts)
    tm = _row_tile(n_rows, per_row)
    nt = n_rows // tm
    hb = tm // HALO
    if late:
        tm = BLK
        nt, hb = n_rows // tm, tm // HALO
    rows = rows + late
    nr, npv, nnx, npar, nor, noa = len(rows), len(prevs), len(nexts), len(pars), len(out_rows), len(out_accs)

    def body(*refs):
        i = pl.program_id(0)
        k = 0
        row_refs = refs[k:k + nr]; k += nr
        pc = refs[k:k + npv]; k += npv
        ph = refs[k:k + npv]; k += npv
        nc = refs[k:k + nnx]; k += nnx
        nh = refs[k:k + nnx]; k += nnx
        par_refs = refs[k:k + npar]; k += npar
        orow = refs[k:k + nor]; k += nor
        oacc = refs[k:k + noa]; k += noa
        pscr = refs[k:k + npv]; k += npv
        nscr = refs[k:k + nnx]
        for c_, h_, s_ in zip(pc, ph, pscr):
            s_[0:HALO, :] = h_[...] * (i > 0).astype(F32)
            s_[HALO:HALO + tm, :] = c_[...]
        for c_, h_, s_ in zip(nc, nh, nscr):
            s_[0:tm, :] = c_[...]
            s_[tm:tm + HALO, :] = h_[...] * (i < nt - 1).astype(F32)
        if noa:
            @pl.when(i == 0)
            def _():
                for r_ in oacc:
                    r_[...] = jnp.zeros_like(r_)
        width = out_rows[0][0]
        windows = [slice(None)] if chunk is None else [slice(c0, c0 + chunk) for c0 in range(0, width, chunk)]
        for cols in windows:
            prev_fns = [(lambda s, s_=s_, cols=cols: s_[pl.ds(HALO, tm), cols] if s == 0 else
                         pltpu.roll(s_[:, cols], s, 0)[HALO:HALO + tm]) for s_ in pscr]
            next_fns = [(lambda s, s_=s_, cols=cols: s_[pl.ds(0, tm), cols] if s == 0 else
                         pltpu.roll(s_[:, cols], tm + HALO - s, 0)[0:tm]) for s_ in nscr]
            row_vals, acc_vals = fn(i, tm, [_Window(r_, cols) for r_ in row_refs], prev_fns, next_fns,
                                    [_Window(r_, cols) for r_ in par_refs])
            for r_, v in zip(orow, row_vals):
                r_[:, cols] = v.astype(r_.dtype)
            for r_, v in zip(oacc, acc_vals):
                r_[:, cols] += v

    def row_spec(a):
        return pl.BlockSpec((tm, a.shape[1]), lambda i: (i, 0))

    def whole(shape):
        return pl.BlockSpec(shape, lambda i: (0, 0))

    in_specs = [row_spec(a) for a in rows[:nr - len(late)]]
    in_specs += [pl.BlockSpec((tm, a.shape[1]), lambda i: (jnp.maximum(i - 1, 0), 0)) for a in late]
    in_specs += [row_spec(a) for a in prevs]
    in_specs += [pl.BlockSpec((HALO, a.shape[1]), lambda i: (jnp.maximum(i * hb - 1, 0), 0)) for a in prevs]
    in_specs += [row_spec(a) for a in nexts]
    in_specs += [pl.BlockSpec((HALO, a.shape[1]), lambda i: (jnp.minimum((i + 1) * hb, n_rows // HALO - 1), 0)) for a in nexts]
    in_specs += [whole(p.shape) for p in pars]
    out_specs = [pl.BlockSpec((tm, c), lambda i: (i, 0)) for c, _ in out_rows] + [whole(s) for s in out_accs]
    out_shape = [jax.ShapeDtypeStruct((n_rows, c), dt) for c, dt in out_rows]
    out_shape += [jax.ShapeDtypeStruct(s, F32) for s in out_accs]
    scratch = [pltpu.VMEM((tm + HALO, a.shape[1]), F32) for a in prevs + nexts]
    tile = tm * per_row
    res = pl.pallas_call(
        body, name=name, grid=(nt,), in_specs=in_specs, out_specs=out_specs, out_shape=out_shape,
        scratch_shapes=scratch, compiler_params=_params(2 * tile, ("arbitrary",)),
    )(*rows, *prevs, *prevs, *nexts, *nexts, *pars)
    return res


def _row_ids(i, tm):
    return i * tm + lax.broadcasted_iota(jnp.int32, (tm, 1), 0)


def _conv_taps(prev_fn, w_ref, b_ref, taps):
    acc = b_ref[...]
    for k in range(taps):
        acc = acc + prev_fn(taps - 1 - k) * w_ref[k:k + 1, :]
    return acc


def _rms_fwd(name, h, g):
    def fn(i, tm, rows, prevs, nexts, pars):
        return [_rms(rows[0][...], pars[0][...])], []
    return _rowcall(name, fn, rows=[h], pars=[g], out_rows=[(h.shape[1], BF16)])[0]


def _ssd_conv_fwd(xbc_raw, w, b):
    def fn(i, tm, rows, prevs, nexts, pars):
        c = _conv_taps(prevs[0], pars[0], pars[1], SSD_CONV)
        y = c * jax.nn.sigmoid(c)
        return [jnp.where(_row_ids(i, tm) >= PAD, y, 0.0)], []
    return _rowcall("ssd_conv_fwd", fn, prevs=[xbc_raw], pars=[w, b], out_rows=[(XBC, F32)], chunk=LANES)[0]


def _mix_post(mix, h0, g_post, g_pre):
    def fn(i, tm, rows, prevs, nexts, pars):
        h1 = rows[1][...] + _rms(rows[0][...], pars[0][...])
        return [h1, _rms(h1, pars[1][...])], []
    return _rowcall("mix_post", fn, rows=[mix, h0], pars=[g_post, g_pre],
                    out_rows=[(D_MODEL, F32), (D_MODEL, BF16)])


def _ffn_act(g_raw, u, w, b):
    def fn(i, tm, rows, prevs, nexts, pars):
        g = _conv_taps(prevs[0], pars[0], pars[1], FFN_CONV)
        return [_gelu(g) * rows[0][...]], []
    return _rowcall("ffn_act", fn, rows=[u], prevs=[g_raw], pars=[w, b], out_rows=[(D_FF, BF16)], chunk=LANES)[0]


def _loss_post(f, h1, target, g_post):
    def fn(i, tm, rows, prevs, nexts, pars):
        fv, g = rows[0][...], pars[0][...]
        h2 = rows[1][...] + _rms(fv, g)
        real = _row_ids(i, tm) >= BLK
        diff = jnp.where(real, h2 - rows[2][...], 0.0)
        loss = 0.5 * jnp.sum(jnp.mean(diff * diff, axis=-1, keepdims=True))
        dh2 = diff * (1.0 / D_MODEL)
        df, dg = _rms_bwd(dh2, fv, g)
        return [dh2, df], [jnp.zeros((1, LANES), F32) + loss, dg]
    return _rowcall("loss_post", fn, rows=[f, h1], late=[target], pars=[g_post],
                    out_rows=[(D_MODEL, F32), (D_MODEL, BF16)], out_accs=[(1, LANES), (1, D_MODEL)])


def _ffn_bwd_act(dact, u, g_raw, w, b):
    def fn(i, tm, rows, prevs, nexts, pars):
        g = _conv_taps(prevs[0], pars[0], pars[1], FFN_CONV)
        da = rows[0][...]
        gelu, grad = _gelu_and_grad(g)
        return [da * rows[1][...] * grad, da * gelu], []
    return _rowcall("ffn_bwd_act", fn, rows=[dact, u], prevs=[g_raw], pars=[w, b],
                    out_rows=[(D_FF, F32), (D_FF, BF16)], chunk=LANES)


def _conv_bwd(name, dy, x, w, taps):
    width = x.shape[1]

    def fn(i, tm, rows, prevs, nexts, pars):
        dy0 = nexts[0](0)
        dx = None
        for k in range(taps):
            term = nexts[0](taps - 1 - k) * pars[0][k:k + 1, :]
            dx = term if dx is None else dx + term
        dws = [jnp.sum(dy0 * prevs[0](taps - 1 - k), axis=0, keepdims=True) for k in range(taps)]
        return [dx], dws + [jnp.sum(dy0, axis=0, keepdims=True)]
    return _rowcall(name, fn, prevs=[x], nexts=[dy], pars=[w], out_rows=[(width, BF16)],
                    out_accs=[(1, width)] * (taps + 1), chunk=LANES)


def _mid_bwd(dxn2, h1, dh2, mix, g_pre, g_post):
    def fn(i, tm, rows, prevs, nexts, pars):
        d1, dg_pre = _rms_bwd(rows[0][...], rows[1][...], pars[0][...])
        dh1 = rows[2][...] + d1
        dmix, dg_post = _rms_bwd(dh1, rows[3][...], pars[1][...])
        return [dh1, dmix], [dg_pre, dg_post]
    return _rowcall("mid_bwd", fn, rows=[dxn2, h1, dh2, mix], pars=[g_pre, g_post],
                    out_rows=[(D_MODEL, F32), (D_MODEL, BF16)], out_accs=[(1, D_MODEL), (1, D_MODEL)])


def _norm_bwd(name, dy, x, g):
    def fn(i, tm, rows, prevs, nexts, pars):
        dx, dg = _rms_bwd(rows[0][...], rows[1][...], pars[0][...])
        return [dx], [dg]
    return _rowcall(name, fn, rows=[dy, x], pars=[g], out_rows=[(x.shape[1], BF16)], out_accs=[(1, x.shape[1])])


def _ssd_conv_bwd_act(dact, xbc_raw, w, b):
    def fn(i, tm, rows, prevs, nexts, pars):
        c = _conv_taps(prevs[0], pars[0], pars[1], SSD_CONV)
        s = jax.nn.sigmoid(c)
        dc = rows[0][...] * s * (1.0 + c * (1.0 - s))
        return [jnp.where(_row_ids(i, tm) >= PAD, dc, 0.0)], []
    return _rowcall("ssd_conv_bwd_act", fn, rows=[dact], prevs=[xbc_raw], pars=[w, b], out_rows=[(XBC, F32)],
                    chunk=LANES)[0]


def _first_bwd(dxn1, h0, dh1, g):
    def fn(i, tm, rows, prevs, nexts, pars):
        d0, dg = _rms_bwd(rows[0][...], rows[1][...], pars[0][...])
        return [rows[2][...] + d0], [dg]
    return _rowcall("first_bwd", fn, rows=[dxn1, h0, dh1], pars=[g], out_rows=[(D_MODEL, F32)],
                    out_accs=[(1, D_MODEL)])


def _ssd_chunk_terms(c, dtr_ref, bias_ref, alog_ref):
    ri = lax.broadcasted_iota(jnp.int32, (BLK, BLK), 0)
    ci = lax.broadcasted_iota(jnp.int32, (BLK, BLK), 1)
    causal = ri >= ci
    tril = causal.astype(F32)
    triu = (ri <= ci).astype(F32)
    rowmask = ((c * BLK + lax.broadcasted_iota(jnp.int32, (BLK, 1), 0)) >= PAD).astype(F32)
    dt = _softplus(dtr_ref[...] + bias_ref[...]) * rowmask
    a_neg = -jnp.exp(alog_ref[...])
    a = dt * a_neg
    cs = _dot(tril, a, NN, HI)
    cs_t = _dot(a, triu, TN, HI)
    return causal, triu, rowmask, dt, a_neg, cs, cs_t


def _decay_matrix(causal, cs_h, cs_t_h):
    return jnp.where(causal, jnp.exp(jnp.where(causal, cs_h - cs_t_h, 0.0)), 0.0)


def _head_spread(width):
    ri = lax.broadcasted_iota(jnp.int32, (LANES, HEADS * width), 0)
    ci = lax.broadcasted_iota(jnp.int32, (LANES, HEADS * width), 1)
    return ((ri * width <= ci) & (ci < (ri + 1) * width)).astype(BF16)


def _three_terms(x):
    hi = x.astype(BF16)
    rest = x - hi.astype(F32)
    mid = rest.astype(BF16)
    lo = (rest - mid.astype(F32)).astype(BF16)
    return jnp.concatenate([hi, mid, lo], axis=1)


def _spread(x, sel):
    return _dot(_three_terms(x), jnp.concatenate([sel, sel, sel], axis=0))


def _lane_sums(y, sel):
    return _dot(_three_terms(y), jnp.concatenate([sel, sel, sel], axis=1), NT)


def _ssd_spreads(dt, cs, d_ref, sel64, sel128):
    dt64 = _spread(dt, sel64)
    cs64 = _spread(cs, sel64)
    cs128 = _spread(cs, sel128)
    d64 = _spread(jnp.broadcast_to(d_ref[...], (HALO, LANES)), sel64)[0:1, :]
    cl64 = cs64[BLK - 1:BLK, :]
    return dt64, cs128, d64, jnp.exp(cs64), jnp.exp(cl64 - cs64), jnp.exp(cl64)


def _ssd_pair_terms(p, xbc_ref, dt64, cs128, cs_t, causal):
    lanes = _pair_lanes(p)
    xs = xbc_ref[:, lanes]
    decays = [_decay_matrix(causal, cs128[:, h * LANES:(h + 1) * LANES], cs_t[h:h + 1, :]) for h in (2 * p, 2 * p + 1)]
    return xs, xs * dt64[:, lanes], decays


def _ssd_fwd(xbc, dtr, z, dt_bias, a_log, d_skip, norm_g):
    n_rows = xbc.shape[0]
    nb = n_rows // BLK

    def body(xbc_ref, dtr_ref, z_ref, bias_ref, alog_ref, d_ref, g_ref, s64_ref, s128_ref,
             ypre_ref, yssd_ref, st_ref, state):
        c = pl.program_id(0)

        @pl.when(c == 0)
        def _():
            state[...] = jnp.zeros_like(state)

        st_ref[...] = state[...]
        causal, _, _, dt, _, cs, cs_t = _ssd_chunk_terms(c, dtr_ref, bias_ref, alog_ref)
        first = lax.broadcasted_iota(jnp.int32, (BLK, LANES), 1) < HEAD_DIM
        dt64, cs128, d64, from_start, to_end, chunk_decay = _ssd_spreads(dt, cs, d_ref, s64_ref[...], s128_ref[...])
        for g in range(SSD_GROUPS):
            b_b = xbc_ref[:, OFF_B + g * SSD_STATE:OFF_B + (g + 1) * SSD_STATE].astype(BF16)
            c_b = xbc_ref[:, OFF_C + g * SSD_STATE:OFF_C + (g + 1) * SSD_STATE].astype(BF16)
            cb = _dot(c_b, b_b, NT)
            for j in range(HEADS_PER_GROUP // 2):
                p = g * (HEADS_PER_GROUP // 2) + j
                lanes = _pair_lanes(p)
                xs, x_dt, decays = _ssd_pair_terms(p, xbc_ref, dt64, cs128, cs_t, causal)
                ms = [(cb * d).astype(BF16) for d in decays]
                s_p = state[:, lanes]
                y = _dot(jnp.concatenate(ms, axis=1), _head_stack(x_dt.astype(BF16), first))
                y = y + from_start[:, lanes] * _dot(c_b, s_p.astype(BF16))
                state[:, lanes] = chunk_decay[:, lanes] * s_p + _dot(b_b, (to_end[:, lanes] * x_dt).astype(BF16), TN)
                ypre_ref[:, lanes] = y + d64[:, lanes] * xs
        zz = z_ref[...]
        yg = ypre_ref[...] * (zz * jax.nn.sigmoid(zz))
        yssd_ref[...] = _rms(yg, g_ref[...]).astype(yssd_ref.dtype)

    blk = lambda w: pl.BlockSpec((BLK, w), lambda c: (c, 0))
    par = lambda a: pl.BlockSpec(a.shape, lambda c: (0, 0))
    sel64, sel128 = _head_spread(HEAD_DIM), _head_spread(LANES)
    return pl.pallas_call(
        body, name="ssd_fwd", grid=(nb,),
        in_specs=[blk(XBC), blk(LANES), blk(SSD_INNER), par(dt_bias), par(a_log), par(d_skip), par(norm_g),
                  par(sel64), par(sel128)],
        out_specs=[blk(SSD_INNER), blk(SSD_INNER), pl.BlockSpec((None, SSD_STATE, SSD_INNER), lambda c: (c, 0, 0))],
        out_shape=[jax.ShapeDtypeStruct((n_rows, SSD_INNER), F32), jax.ShapeDtypeStruct((n_rows, SSD_INNER), BF16),
                   jax.ShapeDtypeStruct((nb, SSD_STATE, SSD_INNER), F32)],
        scratch_shapes=[pltpu.VMEM((SSD_STATE, SSD_INNER), F32)],
        compiler_params=_params(8 << 20, ("arbitrary",)),
    )(xbc, dtr, z, dt_bias, a_log, d_skip, norm_g, sel64, sel128)


def _ssd_bwd(dy, ypre, z, xbc, dtr, states, dt_bias, a_log, d_skip, norm_g):
    n_rows = xbc.shape[0]
    nb = n_rows // BLK

    def body(dy_ref, ypre_ref, z_ref, xbc_ref, dtr_ref, st_ref, bias_ref, alog_ref, d_ref, g_ref, s64_ref, s128_ref,
             dz_ref, dxbc_ref, ddtr_ref, dgn_ref, dd_ref, dal_ref, ddtb_ref, dstate, dyp, red):
        step = pl.program_id(0)
        c = nb - 1 - step

        @pl.when(step == 0)
        def _():
            dstate[...] = jnp.zeros_like(dstate)
            for r_ in (dgn_ref, dd_ref, dal_ref, ddtb_ref):
                r_[...] = jnp.zeros_like(r_)

        yp, zz = ypre_ref[...], z_ref[...]
        sz = jax.nn.sigmoid(zz)
        silu = zz * sz
        dyg, dgn = _rms_bwd(dy_ref[...], yp * silu, g_ref[...])
        dgn_ref[...] += dgn
        dz_ref[...] = (dyg * yp * (sz * (1.0 + zz * (1.0 - sz)))).astype(dz_ref.dtype)
        dyp[...] = dyg * silu

        causal, triu, rowmask, dt, a_neg, cs, cs_t = _ssd_chunk_terms(c, dtr_ref, bias_ref, alog_ref)
        lane = lax.broadcasted_iota(jnp.int32, (1, LANES), 1)
        last_row = (lax.broadcasted_iota(jnp.int32, (BLK, 1), 0) == BLK - 1).astype(F32)
        first = lax.broadcasted_iota(jnp.int32, (BLK, LANES), 1) < HEAD_DIM
        sel64 = s64_ref[...]
        dt64, cs128, d64, from_start, to_end, chunk_decay = _ssd_spreads(dt, cs, d_ref, sel64, s128_ref[...])
        for g in range(SSD_GROUPS):
            b_b = xbc_ref[:, OFF_B + g * SSD_STATE:OFF_B + (g + 1) * SSD_STATE].astype(BF16)
            c_b = xbc_ref[:, OFF_C + g * SSD_STATE:OFF_C + (g + 1) * SSD_STATE].astype(BF16)
            cb = _dot(c_b, b_b, NT)
            b_twice = jnp.concatenate([b_b, b_b], axis=0)
            c_twice = jnp.concatenate([c_b, c_b], axis=0)
            db_g = jnp.zeros((BLK, SSD_STATE), F32)
            dc_g = jnp.zeros((BLK, SSD_STATE), F32)
            for j in range(HEADS_PER_GROUP // 2):
                p = g * (HEADS_PER_GROUP // 2) + j
                lanes = _pair_lanes(p)
                xs, x_dt, decays = _ssd_pair_terms(p, xbc_ref, dt64, cs128, cs_t, causal)
                ms = [(cb * d).astype(BF16) for d in decays]
                d_y = dyp[:, lanes]
                s_p, ds_p = st_ref[:, lanes], dstate[:, lanes]
                s_b, ds_b = s_p.astype(BF16), ds_p.astype(BF16)
                fs, te = from_start[:, lanes], to_end[:, lanes]
                x_b, dy_b = x_dt.astype(BF16), d_y.astype(BF16)
                x_st = _head_stack(x_b, first)
                fs_dy = (fs * d_y).astype(BF16)

                y_diag = _dot(jnp.concatenate(ms, axis=1), x_st)
                y_off = fs * _dot(c_b, s_b)
                end_part = te * _dot(b_b, ds_b)
                dx_diag = _dot(jnp.concatenate(ms, axis=0), _head_stack(dy_b, first), TN)
                d_x = dx_diag + end_part
                g2 = _dot(dy_b, x_st, NT)
                gl = [(g2[:, k * BLK:(k + 1) * BLK] * decays[k]).astype(BF16) for k in range(2)]
                dc_g = dc_g + _dot(jnp.concatenate(gl, axis=1), b_twice) + _dot(fs_dy, s_b, NT)
                db_g = db_g + _dot(jnp.concatenate(gl, axis=0), c_twice, TN) + _dot((te * x_dt).astype(BF16), ds_b, NT)
                red[0:BLK, lanes] = (dy_b.astype(F32) * y_diag - x_b.astype(F32) * dx_diag) + (d_y * y_off - x_dt * end_part)
                red[BLK:2 * BLK, lanes] = x_dt * end_part
                red[2 * BLK:3 * BLK, lanes] = d_x * xs
                red[3 * BLK:3 * BLK + HALO, lanes] = jnp.broadcast_to(jnp.sum(ds_p * s_p, axis=0, keepdims=True), (HALO, LANES))
                red[3 * BLK + HALO:3 * BLK + 2 * HALO, lanes] = jnp.broadcast_to(
                    jnp.sum(d_y * xs, axis=0, keepdims=True), (HALO, LANES))
                dxbc_ref[:, lanes] = d_x * dt64[:, lanes] + d64[:, lanes] * d_y
                dstate[:, lanes] = chunk_decay[:, lanes] * ds_p + _dot(c_b, fs_dy, TN)
            dxbc_ref[:, OFF_B + g * SSD_STATE:OFF_B + (g + 1) * SSD_STATE] = db_g
            dxbc_ref[:, OFF_C + g * SSD_STATE:OFF_C + (g + 1) * SSD_STATE] = dc_g
        sums = _lane_sums(red[...], sel64)
        at_last = (jnp.sum(sums[BLK:2 * BLK], axis=0, keepdims=True)
                   + jnp.exp(cs[BLK - 1:BLK, :]) * sums[3 * BLK:3 * BLK + 1])
        dcs = sums[0:BLK] + last_row * at_last
        ddt_x = sums[2 * BLK:3 * BLK]
        dd_row = sums[3 * BLK + HALO:3 * BLK + HALO + 1]
        da = _dot(triu, dcs, NN, HI)
        ddt = (da * a_neg + ddt_x) * rowmask
        ddtr = ddt * jax.nn.sigmoid(dtr_ref[...] + bias_ref[...]) * (lane < HEADS).astype(F32)
        ddtr_ref[...] = ddtr.astype(ddtr_ref.dtype)
        ddtb_ref[...] += jnp.sum(ddtr, axis=0, keepdims=True)
        dal_ref[...] += jnp.sum(da * dt, axis=0, keepdims=True) * a_neg
        dd_ref[...] += dd_row

    blk = lambda w: pl.BlockSpec((BLK, w), lambda s: (nb - 1 - s, 0))
    par = lambda a: pl.BlockSpec(a.shape, lambda s: (0, 0))
    acc = lambda w: pl.BlockSpec((1, w), lambda s: (0, 0))
    sel64, sel128 = _head_spread(HEAD_DIM), _head_spread(LANES)
    return pl.pallas_call(
        body, name="ssd_bwd", grid=(nb,),
        in_specs=[blk(SSD_INNER), blk(SSD_INNER), blk(SSD_INNER), blk(XBC), blk(LANES),
                  pl.BlockSpec((None, SSD_STATE, SSD_INNER), lambda s: (nb - 1 - s, 0, 0)),
                  par(dt_bias), par(a_log), par(d_skip), par(norm_g), par(sel64), par(sel128)],
        out_specs=[blk(SSD_INNER), blk(XBC), blk(LANES), acc(SSD_INNER), acc(LANES), acc(LANES), acc(LANES)],
        out_shape=[jax.ShapeDtypeStruct((n_rows, SSD_INNER), BF16), jax.ShapeDtypeStruct((n_rows, XBC), F32),
                   jax.ShapeDtypeStruct((n_rows, LANES), BF16), jax.ShapeDtypeStruct((1, SSD_INNER), F32),
                   jax.ShapeDtypeStruct((1, LANES), F32), jax.ShapeDtypeStruct((1, LANES), F32),
                   jax.ShapeDtypeStruct((1, LANES), F32)],
        scratch_shapes=[pltpu.VMEM((SSD_STATE, SSD_INNER), F32), pltpu.VMEM((BLK, SSD_INNER), F32),
                        pltpu.VMEM((3 * BLK + 2 * HALO, SSD_INNER), F32)],
        compiler_params=_params(12 << 20, ("arbitrary",)),
    )(dy, ypre, z, xbc, dtr, states, dt_bias, a_log, d_skip, norm_g, sel64, sel128)


HEAD_GROUP = 4


LOG2_E = 1.4426950408889634
SOFTPLUS_CLAMP = 80.0


def _sb_logits(zl, valid):
    z2 = zl * LOG2_E
    lost = jnp.maximum(jnp.log2(1.0 + jnp.exp2(jnp.minimum(z2, SOFTPLUS_CLAMP))), z2)
    log_beta = z2 - lost
    if valid is not None:
        lost = jnp.where(valid, lost, 0.0)
    return log_beta, lost


def _tri_and_ones(tri, sign=1.0):
    half = sign * jnp.concatenate([tri, jnp.ones((BLK, BLK), F32)], axis=1)
    return jnp.concatenate([half, half], axis=0).astype(BF16)


def _tile_mask(i, j, ri, ci):
    key = j * BLK + ci
    return (key < i * BLK + ri) & (key >= PAD)


def _pair_lanes(p):
    return slice(p * 2 * HEAD_DIM, (p + 1) * 2 * HEAD_DIM)


def _head_stack(x_pair, first):
    zero = jnp.zeros_like(x_pair)
    return jnp.concatenate([jnp.where(first, x_pair, zero), jnp.where(first, zero, x_pair)], axis=0)


def _block_rows(j):
    return pl.ds(j * BLK if isinstance(j, int) else pl.multiple_of(j * BLK, BLK), BLK)


def _two_terms(x):
    hi = x.astype(BF16)
    return jnp.concatenate([hi, (x - hi.astype(F32)).astype(BF16)], axis=1)


def _ride_along(ex, n_in, n_out, refs):
    k = 0
    parts = []
    for cnt in (n_in, ex.n, n_out, ex.n):
        parts.append(refs[k:k + cnt])
        k += cnt
    n_sems = len(ex.scratch)
    return (*parts, refs[k:len(refs) - n_sems], refs[len(refs) - n_sems:])


def _attn_fwd(q, k, v, ex, ex_arrays):
    n_rows, width = q.shape
    nb = n_rows // BLK
    n_heads = HEAD_GROUP
    gw = n_heads * HEAD_DIM
    groups = width // gw

    def body(*refs):
        (q_ref, k_ref, v_ref), ex_in, (o_ref, tot_ref), ex_out, (run_ref, z_ref, w_ref), sems = _ride_along(ex, 3, 2, refs)
        step = pl.program_id(0)

        @pl.when(step == 0)
        def _():
            ex.start(ex_in, ex_out, sems)

        ri = lax.broadcasted_iota(jnp.int32, (BLK, BLK), 0)
        ci = lax.broadcasted_iota(jnp.int32, (BLK, BLK), 1)
        sums = _tri_and_ones((ri > ci).astype(F32), -1.0)
        first = ci < HEAD_DIM
        heads, pairs = range(n_heads), range(n_heads // 2)

        def stacks(ref, blocks, p):
            return jnp.concatenate([_head_stack(ref[_block_rows(jnp.clip(j, 0, nb - 1)), _pair_lanes(p)], first)
                                    for j in blocks], axis=0)

        def q_block(i, carry):
            rows = _block_rows(i)
            o_ref[rows, :] = jnp.zeros((BLK, gw), F32)
            run_ref[...] = jnp.zeros_like(run_ref)
            w_ref[...] = jnp.zeros_like(w_ref)

            odd = (i + 1) % 2

            def blocks(t):
                return i + odd - 2 * t, i + odd - 2 * t - 1

            for p in pairs:
                z_ref[p] = _dot(q_ref[rows, _pair_lanes(p)], stacks(k_ref, blocks(0), p), NT)

            def tile(t, masked, skip=None):
                q_i, o_i = q_ref[rows, :], o_ref[rows, :]
                outs = [_dot(w_ref[p], stacks(v_ref, blocks(jnp.maximum(t - 1, 0)), p)) for p in pairs]
                z_next = [_dot(q_i[:, _pair_lanes(p)], stacks(k_ref, blocks(t + 1), p), NT) for p in pairs]
                runs = [run_ref[h] for h in heads]
                lgs, res = {}, {}
                for s, j in enumerate(blocks(t)):
                    if s == skip:
                        continue
                    valid = _tile_mask(i, j, ri, ci) if masked else None
                    lgs[s] = [_sb_logits(z_ref[h // 2][:, (2 * s + h % 2) * BLK:(2 * s + h % 2 + 1) * BLK], valid)
                              for h in heads]
                    res[s] = _dot(jnp.concatenate([_two_terms(lgs[s][h][1]) for h in heads], axis=0), sums)
                ws = []
                for s, j in enumerate(blocks(t)):
                    valid = _tile_mask(i, j, ri, ci) if masked else None
                    for h in heads:
                        if s == skip:
                            ws.append(jnp.zeros((BLK, BLK), BF16))
                            continue
                        part = res[s][h * BLK:(h + 1) * BLK]
                        w = jnp.exp2(lgs[s][h][0] + part[:, :BLK] + runs[h])
                        ws.append((jnp.where(valid, w, 0.0) if masked else w).astype(BF16))
                        runs[h] = runs[h] + part[:, BLK:]
                for p in pairs:
                    w_ref[p] = jnp.concatenate([ws[s * n_heads + 2 * p + e] for s in range(2) for e in range(2)], axis=1)
                    z_ref[p] = z_next[p]
                o_ref[rows, :] = o_i + jnp.concatenate(outs, axis=1)
                for h in heads:
                    run_ref[h] = runs[h]

            n_calls = (i + 2) // 2

            @pl.when(odd == 1)
            def _():
                tile(0, True, skip=0)

            @pl.when(odd == 0)
            def _():
                tile(0, True)

            def mid(t, carry):
                tile(t, False)
                return carry

            lax.fori_loop(1, n_calls - 1, mid, 0)

            @pl.when(n_calls >= 2)
            def _():
                tile(n_calls - 1, True)

            o_ref[rows, :] += jnp.concatenate([_dot(w_ref[p], stacks(v_ref, blocks(n_calls - 1), p)) for p in pairs], axis=1)
            for h in heads:
                tot_ref[h // HEAD_GROUP, rows, h % HEAD_GROUP:h % HEAD_GROUP + 1] = run_ref[h][:, 0:1]
            return carry

        lax.fori_loop(0, nb, q_block, 0)

        @pl.when(step == groups - 1)
        def _():
            ex.wait(ex_in, ex_out, sems)

    spec = pl.BlockSpec((n_rows, gw), lambda g: (0, g))
    tot_spec = pl.BlockSpec((n_heads // HEAD_GROUP, n_rows, HEAD_GROUP), lambda g: (g, 0, 0))
    res = pl.pallas_call(
        body, name="attn_fwd", grid=(groups,), in_specs=[spec, spec, spec] + ex.specs,
        out_specs=[spec, tot_spec] + ex.specs,
        out_shape=[jax.ShapeDtypeStruct((n_rows, width), F32),
                   jax.ShapeDtypeStruct((width // (HEAD_GROUP * HEAD_DIM), n_rows, HEAD_GROUP), F32)] + ex.out_shape,
        scratch_shapes=[pltpu.VMEM((n_heads, BLK, BLK), F32), pltpu.VMEM((n_heads // 2, BLK, 4 * BLK), F32),
                        pltpu.VMEM((n_heads // 2, BLK, 4 * BLK), BF16)] + ex.scratch,
        compiler_params=_params(n_rows * (3 * gw * 2 + gw * 4 + LANES * 4), ("arbitrary",)),
    )(q, k, v, *ex_arrays)
    return res[0], res[1], res[2:]


def _attn_bwd(q, k, v, keep_total, do, ex, ex_arrays):
    n_rows, width = q.shape
    nb = n_rows // BLK
    gw = HEAD_GROUP * HEAD_DIM
    groups = width // gw

    def body(*refs):
        ((q_ref, k_ref, v_ref, tot_ref, do_ref), ex_in, (dq_ref, dk_ref, dv_ref), ex_out,
         (dq_acc, dk_acc, dv_acc, tot_b, run_ref, rung_ref, z_ref, dw_ref, dz_ref, wb_ref, qst_ref, dost_ref),
         sems) = _ride_along(ex, 5, 3, refs)
        step = pl.program_id(0)

        @pl.when(step == 0)
        def _():
            ex.start(ex_in, ex_out, sems)

        ri = lax.broadcasted_iota(jnp.int32, (BLK, BLK), 0)
        ci = lax.broadcasted_iota(jnp.int32, (BLK, BLK), 1)
        sums_keep = _tri_and_ones((ri <= ci).astype(F32), -1.0)
        sums_g = _tri_and_ones((ri < ci).astype(F32))
        first = ci < HEAD_DIM
        heads, pairs = range(HEAD_GROUP), range(HEAD_GROUP // 2)
        dk_acc[...] = jnp.zeros_like(dk_acc)
        dv_acc[...] = jnp.zeros_like(dv_acc)

        def clamp(j):
            return jnp.clip(j, 0, nb - 1)

        def stacks(ref, blocks, p):
            return jnp.concatenate([_head_stack(ref[_block_rows(clamp(j)), _pair_lanes(p)], first) for j in blocks], axis=0)

        def blocks(t):
            return 2 * t, 2 * t + 1

        def kept(ref, p, axis):
            tiles = [[ref[s * HEAD_GROUP + 2 * p + e] for e in range(2)] for s in range(2)]
            if axis == 1:
                return jnp.concatenate(tiles[0] + tiles[1], axis=1)
            return jnp.concatenate([jnp.concatenate(tiles[s], axis=0) for s in range(2)], axis=1)

        def owed_dq(t_prev):
            return [_dot(kept(dz_ref, p, 1), stacks(k_ref, blocks(t_prev), p)) for p in pairs]

        def owed_dk():
            return [_dot(kept(dz_ref, p, 0), qst_ref[p], TN) for p in pairs]

        def owed_dv():
            return [_dot(kept(wb_ref, p, 0), dost_ref[p], TN) for p in pairs]

        def settle(t_prev, parts):
            dq, dk, dv = parts
            dq_acc[...] += jnp.concatenate(dq, axis=1)
            for s, j in enumerate(blocks(t_prev)):
                cols = _block_rows(clamp(j))
                dk_acc[cols, :] += jnp.concatenate([d[s * BLK:(s + 1) * BLK] for d in dk], axis=1)
                dv_acc[cols, :] += jnp.concatenate([d[s * BLK:(s + 1) * BLK] for d in dv], axis=1)

        def q_block(i, carry):
            rows = _block_rows(i)
            dq_acc[...] = jnp.zeros_like(dq_acc)
            run_ref[...] = jnp.zeros_like(run_ref)
            rung_ref[...] = jnp.zeros_like(rung_ref)
            dz_ref[...] = jnp.zeros_like(dz_ref)
            wb_ref[...] = jnp.zeros_like(wb_ref)
            for h in heads:
                tot_b[h] = jnp.broadcast_to(tot_ref[rows, h:h + 1], (BLK, BLK))
            for p in pairs:
                qst_ref[p] = _head_stack(q_ref[rows, _pair_lanes(p)], first)
                dost_ref[p] = _head_stack(do_ref[rows, _pair_lanes(p)], first)
                z_ref[p] = _dot(q_ref[rows, _pair_lanes(p)], stacks(k_ref, blocks(0), p), NT)
                dw_ref[p] = _dot(do_ref[rows, _pair_lanes(p)], stacks(v_ref, blocks(0), p), NT)

            def tile(t, masked, skip=None):
                q_i, do_i = q_ref[rows, :], do_ref[rows, :]
                t_prev = jnp.maximum(t - 1, 0)
                valids = [_tile_mask(i, j, ri, ci) if masked else None for j in blocks(t)]
                tile_of = lambda ref, s, h: ref[h // 2][:, (2 * s + h % 2) * BLK:(2 * s + h % 2 + 1) * BLK]
                nothing = [jnp.zeros((BLK, BLK), F32)] * HEAD_GROUP
                runs = [run_ref[h] for h in heads]
                rungs = [rung_ref[h] for h in heads]
                lgs, keep, ws, gs, gsum, dzs = {}, {}, {}, {}, {}, {}
                for s in range(2):
                    if s != skip:
                        lgs[s] = [_sb_logits(tile_of(z_ref, s, h), valids[s]) for h in heads]
                        keep[s] = _dot(jnp.concatenate([_two_terms(lgs[s][h][1]) for h in heads], axis=0), sums_keep)
                    if s == 0:
                        part_dq = owed_dq(t_prev)
                    else:
                        part_dk = owed_dk()
                for s in range(2):
                    ws[s], gs[s] = [], []
                    if s != skip:
                        for h in heads:
                            part = keep[s][h * BLK:(h + 1) * BLK]
                            w = jnp.exp2(lgs[s][h][0] + (tot_b[h] - runs[h] - part[:, :BLK]))
                            ws[s].append(jnp.where(valids[s], w, 0.0) if masked else w)
                            runs[h] = runs[h] + part[:, BLK:]
                            gs[s].append(tile_of(dw_ref, s, h) * ws[s][h])
                        gsum[s] = _dot(jnp.concatenate([_two_terms(gs[s][h]) for h in heads], axis=0), sums_g)
                    else:
                        ws[s] = nothing
                    if s == 0:
                        part_dv = owed_dv()
                    else:
                        z_next = [_dot(q_i[:, _pair_lanes(p)], stacks(k_ref, blocks(t + 1), p), NT) for p in pairs]
                for s in range(2):
                    dzs[s] = []
                    if s != skip:
                        for h in heads:
                            part = gsum[s][h * BLK:(h + 1) * BLK]
                            beta = jnp.exp2(lgs[s][h][0])
                            dz = gs[s][h] * (1.0 - beta) - beta * (part[:, :BLK] + rungs[h])
                            dzs[s].append(jnp.where(valids[s], dz, 0.0) if masked else dz)
                            rungs[h] = rungs[h] + part[:, BLK:]
                    else:
                        dzs[s] = nothing
                    if s == 0:
                        dw_next = [_dot(do_i[:, _pair_lanes(p)], stacks(v_ref, blocks(t + 1), p), NT) for p in pairs]
                settle(t_prev, (part_dq, part_dk, part_dv))
                for s in range(2):
                    for h in heads:
                        dz_ref[s * HEAD_GROUP + h] = dzs[s][h].astype(BF16)
                        wb_ref[s * HEAD_GROUP + h] = ws[s][h].astype(BF16)
                for h in heads:
                    run_ref[h] = runs[h]
                    rung_ref[h] = rungs[h]
                for p in pairs:
                    z_ref[p] = z_next[p]
                    dw_ref[p] = dw_next[p]

            n_calls = (i + 2) // 2
            tile(0, True)

            def mid(t, carry):
                tile(t, False)
                return carry

            lax.fori_loop(1, n_calls - 1, mid, 0)

            @pl.when((n_calls >= 2) & (i % 2 == 1))
            def _():
                tile(n_calls - 1, True)

            @pl.when((n_calls >= 2) & (i % 2 == 0))
            def _():
                tile(n_calls - 1, True, skip=1)

            settle(n_calls - 1, (owed_dq(n_calls - 1), owed_dk(), owed_dv()))
            dq_ref[rows, :] = (dq_acc[...] * SB_SCALE).astype(dq_ref.dtype)
            return carry

        lax.fori_loop(0, nb, q_block, 0)
        dk_ref[...] = dk_acc[...].astype(dk_ref.dtype)
        dv_ref[...] = dv_acc[...].astype(dv_ref.dtype)

        @pl.when(step == groups - 1)
        def _():
            ex.wait(ex_in, ex_out, sems)

    spec = pl.BlockSpec((n_rows, gw), lambda g: (0, g))
    tot_spec = pl.BlockSpec((None, n_rows, HEAD_GROUP), lambda g: (g, 0, 0))
    out = jax.ShapeDtypeStruct((n_rows, width), BF16)
    tile_f32 = pltpu.VMEM((HEAD_GROUP, BLK, BLK), F32)
    tile_bf16 = pltpu.VMEM((2 * HEAD_GROUP, BLK, BLK), BF16)
    pair_f32 = pltpu.VMEM((HEAD_GROUP // 2, BLK, 4 * BLK), F32)
    pair_stack = pltpu.VMEM((HEAD_GROUP // 2, 2 * BLK, BLK), BF16)
    res = pl.pallas_call(
        body, name="attn_bwd", grid=(groups,), in_specs=[spec, spec, spec, tot_spec, spec] + ex.specs,
        out_specs=[spec] * 3 + ex.specs, out_shape=[out] * 3 + ex.out_shape,
        scratch_shapes=[pltpu.VMEM((BLK, gw), F32), pltpu.VMEM((n_rows, gw), F32), pltpu.VMEM((n_rows, gw), F32),
                        tile_f32, tile_f32, tile_f32, pair_f32, pair_f32, tile_bf16, tile_bf16, pair_stack,
                        pair_stack] + ex.scratch,
        compiler_params=_params(n_rows * (7 * gw * 2 + LANES * 4 + gw * 4), ("arbitrary",)),
    )(q, k, v, keep_total, do, *ex_arrays)
    return res[0], res[1], res[2], res[3:]


class _Exchange:
    def __init__(self, arrays, gather, same_core=False):
        self.n = len(arrays)
        self.gather = gather
        self.same_core = same_core
        self.out_shape = [jax.ShapeDtypeStruct(((N_DEV,) + a.shape) if gather else a.shape, a.dtype) for a in arrays]
        self.scratch = [pltpu.SemaphoreType.DMA((self.n, N_DEV - 1)), pltpu.SemaphoreType.DMA((self.n, N_DEV - 1)),
                        pltpu.SemaphoreType.DMA((self.n,))]
        self.specs = [pl.BlockSpec(memory_space=pl.ANY)] * self.n

    def _copies(self, ins, outs, sems, with_receives):
        send_sems, recv_sems, local_sems = sems
        x, y, c = lax.axis_index("x"), lax.axis_index("y"), lax.axis_index("c")
        gather, same_core = self.gather, self.same_core
        me = 2 * x + y if same_core else 4 * x + 2 * y + c
        local, sends, recvs = [], [], []
        for a in range(self.n):
            local.append(pltpu.make_async_copy(ins[a] if gather else ins[a].at[me], outs[a].at[me if gather else 0],
                                               local_sems.at[a]))
        for r in ((2, 4, 6) if same_core else range(1, N_DEV)):
            px = 1 - x if r & 4 else x
            py = 1 - y if r & 2 else y
            pc = 1 - c if r & 1 else c
            idx = 2 * px + py if same_core else 4 * px + 2 * py + pc
            slot = r // 2 if same_core else r
            for a in range(self.n):
                src = ins[a] if gather else ins[a].at[idx]
                pair = dict(send_sem=send_sems.at[a, r - 1], recv_sem=recv_sems.at[a, r - 1],
                            device_id=(px, py, pc), device_id_type=pl.DeviceIdType.MESH)
                sends.append(pltpu.make_async_remote_copy(src_ref=src, dst_ref=outs[a].at[me if gather else slot], **pair))
                if with_receives:
                    recvs.append(pltpu.make_async_remote_copy(src_ref=src, dst_ref=outs[a].at[idx if gather else slot], **pair))
        return local, sends, recvs

    def start(self, ins, outs, sems):
        local, sends, _ = self._copies(ins, outs, sems, with_receives=False)
        for cp in local + sends:
            cp.start()

    def wait(self, ins, outs, sems):
        local, sends, recvs = self._copies(ins, outs, sems, with_receives=True)
        for cp in recvs:
            cp.wait_recv()
        for cp in sends:
            cp.wait_send()
        for cp in local:
            cp.wait()


def _peer_exchange(name, parts):
    exs = [_Exchange(arrays, gather) for arrays, gather in parts]
    n = sum(ex.n for ex in exs)
    n_sems = len(exs[0].scratch)

    def body(*refs):
        at, views = 0, []
        for k, ex in enumerate(exs):
            views.append((ex, refs[at:at + ex.n], refs[n + at:n + at + ex.n],
                          refs[2 * n + k * n_sems:2 * n + (k + 1) * n_sems]))
            at += ex.n
        for ex, ins, outs, sems in views:
            ex.start(ins, outs, sems)
        for ex, ins, outs, sems in views:
            ex.wait(ins, outs, sems)

    res = pl.pallas_call(
        body, name=name, in_specs=[s for ex in exs for s in ex.specs], out_specs=[s for ex in exs for s in ex.specs],
        out_shape=[s for ex in exs for s in ex.out_shape], scratch_shapes=[s for ex in exs for s in ex.scratch],
    )(*[a for arrays, _ in parts for a in arrays])
    out, at = [], 0
    for ex in exs:
        out.append(res[at:at + ex.n])
        at += ex.n
    return out


def _gather_two_level(name, arrays):
    n = len(arrays)

    def body(*refs):
        ins, outs = refs[:n], refs[n:2 * n]
        send_sems, recv_sems, local_sems = refs[2 * n:]
        x, y, c = lax.axis_index("x"), lax.axis_index("y"), lax.axis_index("c")
        sibling = (x, y, 1 - c)
        chips = [(1 - x, y), (x, 1 - y), (1 - x, 1 - y)]

        def slot(a, dev):
            return outs[a].at[4 * dev[0] + 2 * dev[1] + dev[2]]

        def copy(a, k, block, to, src=None):
            return pltpu.make_async_remote_copy(
                src_ref=slot(a, block) if src is None else src, dst_ref=slot(a, block),
                send_sem=send_sems.at[a, k], recv_sem=recv_sems.at[a, k], device_id=to, device_id_type=pl.DeviceIdType.MESH)

        me = (x, y, c)
        mine = [pltpu.make_async_copy(ins[a], slot(a, me), local_sems.at[a]) for a in range(n)]
        first = [copy(a, 0, me, sibling, src=ins[a]) for a in range(n)]
        first += [copy(a, 1 + j, me, (*chip, c), src=ins[a]) for j, chip in enumerate(chips) for a in range(n)]
        for cp in mine + first:
            cp.start()
        passed = []
        for j, chip in enumerate(chips):
            for a in range(n):
                copy(a, 1 + j, (*chip, c), me).wait_recv()
            for a in range(n):
                cp = copy(a, 4 + j, (*chip, c), sibling)
                cp.start()
                passed.append(cp)
        for a in range(n):
            copy(a, 0, sibling, me).wait_recv()
            for j, chip in enumerate(chips):
                copy(a, 4 + j, (*chip, 1 - c), me).wait_recv()
        for cp in first + passed:
            cp.wait_send()
        for cp in mine:
            cp.wait()

    any_spec = pl.BlockSpec(memory_space=pl.ANY)
    return pl.pallas_call(
        body, name=name, in_specs=[any_spec] * n, out_specs=[any_spec] * n,
        out_shape=[jax.ShapeDtypeStruct((N_DEV,) + a.shape, a.dtype) for a in arrays],
        scratch_shapes=[pltpu.SemaphoreType.DMA((n, N_DEV - 1)), pltpu.SemaphoreType.DMA((n, N_DEV - 1)),
                        pltpu.SemaphoreType.DMA((n,))],
    )(*arrays)


def _sibling_swap(name, slabs):
    chips = N_DEV // 2

    def body(x_ref, o_ref, send_sems, recv_sems):
        x, y, c = lax.axis_index("x"), lax.axis_index("y"), lax.axis_index("c")
        copies = [pltpu.make_async_remote_copy(
            src_ref=x_ref.at[2 * m + 1 - c], dst_ref=o_ref.at[m], send_sem=send_sems.at[m], recv_sem=recv_sems.at[m],
            device_id=(x, y, 1 - c), device_id_type=pl.DeviceIdType.MESH) for m in range(chips)]
        for cp in copies:
            cp.start()
        for cp in copies:
            cp.wait()

    any_spec = pl.BlockSpec(memory_space=pl.ANY)
    return pl.pallas_call(
        body, name=name, in_specs=[any_spec], out_specs=any_spec,
        out_shape=jax.ShapeDtypeStruct((chips,) + slabs.shape[1:], slabs.dtype),
        scratch_shapes=[pltpu.SemaphoreType.DMA((chips,)), pltpu.SemaphoreType.DMA((chips,))],
    )(slabs)


def _pair_sum(name, slabs, other, core):
    chips, rows, cols = other.shape
    tr = 512

    def body(core_ref, a_ref, b_ref, o_ref):
        o_ref[...] = (a_ref[...].astype(F32) + b_ref[...].astype(F32)).astype(o_ref.dtype)

    spec = pl.BlockSpec((None, tr, cols), lambda m, i, core_ref: (m, i, 0))
    mine = pl.BlockSpec((None, tr, cols), lambda m, i, core_ref: (2 * m + core_ref[0], i, 0))
    return pl.pallas_call(
        body, name=name,
        grid_spec=pltpu.PrefetchScalarGridSpec(num_scalar_prefetch=1, grid=(chips, rows // tr), in_specs=[mine, spec],
                                               out_specs=spec),
        out_shape=jax.ShapeDtypeStruct(other.shape, slabs.dtype),
        compiler_params=_params(3 * tr * cols * 4, ("parallel", "parallel")))(core, slabs, other)


def _sum_slots(name, x):
    def body(x_ref, o_ref):
        acc = x_ref[0]
        for s in range(1, N_DEV):
            acc = acc + x_ref[s]
        o_ref[...] = acc
    return pl.pallas_call(body, name=name, out_shape=jax.ShapeDtypeStruct(x.shape[1:], F32))(x)


def _adamw(name, w, slots, m, v):
    n_slots, rows, cols = slots.shape
    tr = next((t for t in (256, 176, 128) if rows % t == 0 and rows > t), rows)

    def body(w_ref, s_ref, m_ref, v_ref, g_ref, d_ref, nm_ref, nv_ref):
        g = s_ref[0].astype(F32)
        for s in range(1, n_slots):
            g = g + s_ref[s].astype(F32)
        nm = ADAM_B1 * m_ref[...] + (1.0 - ADAM_B1) * g
        nv = ADAM_B2 * v_ref[...] + (1.0 - ADAM_B2) * (g * g)
        m_hat = nm / (1.0 - ADAM_B1 ** ADAM_STEP)
        v_hat = nv / (1.0 - ADAM_B2 ** ADAM_STEP)
        g_ref[...] = g
        d_ref[...] = -ADAM_LR * (m_hat / (jnp.sqrt(v_hat) + ADAM_EPS) + ADAM_WD * w_ref[...])
        nm_ref[...] = nm
        nv_ref[...] = nv

    spec = pl.BlockSpec((tr, cols), lambda i: (i, 0))
    out = jax.ShapeDtypeStruct((rows, cols), F32)
    return pl.pallas_call(
        body, name=name, grid=(rows // tr,),
        in_specs=[spec, pl.BlockSpec((n_slots, tr, cols), lambda i: (0, i, 0)), spec, spec],
        out_specs=[spec] * 4, out_shape=[out] * 4,
        compiler_params=_params((n_slots + 7) * tr * cols * 4, ("parallel",)),
    )(w, slots, m, v)


def _pad_lanes(a):
    return jnp.pad(a, ((0, 0), (0, LANES - a.shape[1])))


def kernel(x, meta_tokens, mix_pre_g, w_in, ssd_conv_w, ssd_conv_b, ssd_dt_bias, ssd_a_log, ssd_d, ssd_norm_g, sb_norm_g, w_out, mix_post_g, ffn_pre_g, w_up, ffn_conv_w, ffn_conv_b, w_down, ffn_post_g, loss_target, m_meta_tokens, m_mix_pre_g, m_w_in, m_ssd_conv_w, m_ssd_conv_b, m_ssd_dt_bias, m_ssd_a_log, m_ssd_d, m_ssd_norm_g, m_sb_norm_g, m_w_out, m_mix_post_g, m_ffn_pre_g, m_w_up, m_ffn_conv_w, m_ffn_conv_b, m_w_down, m_ffn_post_g, v_meta_tokens, v_mix_pre_g, v_w_in, v_ssd_conv_w, v_ssd_conv_b, v_ssd_dt_bias, v_ssd_a_log, v_ssd_d, v_ssd_norm_g, v_sb_norm_g, v_w_out, v_mix_post_g, v_ffn_pre_g, v_w_up, v_ffn_conv_w, v_ffn_conv_b, v_w_down, v_ffn_post_g):
    seq = x.shape[1]
    me = 4 * lax.axis_index("x") + 2 * lax.axis_index("y") + lax.axis_index("c")
    in_cols = w_in.shape[2]
    up_cols = w_up.shape[2]
    out_rows = w_out.shape[1]
    down_rows = w_down.shape[1]

    g_in, g_meta, g_scw, g_fcw = _gather_two_level(
        "gather_w_in", [w_in[0].astype(BF16), meta_tokens, ssd_conv_w[0], ffn_conv_w[0]])
    late_weights = [w_out[0].astype(BF16), w_up[0].astype(BF16), w_down[0].astype(BF16)]
    w_in_full = g_in.transpose(1, 0, 2).reshape(D_MODEL, N_DEV * in_cols)
    off = [0, SSD_INNER, SSD_INNER + XBC, SSD_INNER + XBC + HEADS]
    w_z = w_in_full[:, off[0]:off[1]]
    w_xbc = w_in_full[:, off[1]:off[2]]
    w_dt = _pad_lanes(w_in_full[:, off[2]:off[3]])
    w_q = w_in_full[:, off[3]:off[3] + SSD_INNER]
    w_k = w_in_full[:, off[3] + SSD_INNER:off[3] + 2 * SSD_INNER]
    w_v = w_in_full[:, off[3] + 2 * SSD_INNER:off[3] + 3 * SSD_INNER]
    meta_full = g_meta.transpose(1, 0, 2).reshape(N_META, D_MODEL)
    scw_full = g_scw.transpose(1, 0, 2).reshape(SSD_CONV, XBC)
    fcw_full = g_fcw.transpose(1, 0, 2).reshape(FFN_CONV, D_FF)

    dt_bias_p, a_log_p, d_p = _pad_lanes(ssd_dt_bias), _pad_lanes(ssd_a_log), _pad_lanes(ssd_d)

    h0 = jnp.concatenate([jnp.zeros((PAD, D_MODEL), F32), meta_full, x[0]], axis=0)
    target = loss_target[0]
    xn1 = _rms_fwd("rms_pre_mix", h0, mix_pre_g)
    z = _mm("proj_z", [(xn1, w_z)], "nn", F32)
    xbc_raw = _mm("proj_xbc", [(xn1, w_xbc)], "nn", F32)
    dtr = _mm("proj_dt", [(xn1, w_dt)], "nn", F32)
    q = _mm("proj_q", [(xn1, w_q * SB_SCALE)], "nn", BF16)
    k = _mm("proj_k", [(xn1, w_k)], "nn", BF16)
    v = _mm("proj_v", [(xn1, w_v)], "nn", BF16)
    xbc_act = _ssd_conv_fwd(xbc_raw, scw_full, ssd_conv_b)
    ypre, y_ssd, states = _ssd_fwd(xbc_act, dtr, z, dt_bias_p, a_log_p, d_p, ssd_norm_g)
    o, keep_total, (g_out, g_up, g_down) = _attn_fwd(q, k, v, _Exchange(late_weights, gather=True), late_weights)
    w_out_full = g_out.reshape(N_DEV * out_rows, D_MODEL)
    wo_ssd, wo_sb = w_out_full[:SSD_INNER], w_out_full[SSD_INNER:]
    w_up_full = g_up.transpose(1, 0, 2).reshape(D_MODEL, N_DEV * up_cols)
    w_gate, w_lin = w_up_full[:, :D_FF], w_up_full[:, D_FF:]
    w_down_full = g_down.reshape(N_DEV * down_rows, D_MODEL)
    y_sb = _rms_fwd("rms_sb", o, sb_norm_g)
    mix = _mm("mix_out", [(y_ssd, wo_ssd), (y_sb, wo_sb)], "nn", F32)
    h1, xn2 = _mix_post(mix, h0, mix_post_g, ffn_pre_g)
    g_raw = _mm("ffn_gate", [(xn2, w_gate)], "nn", F32)
    u = _mm("ffn_lin", [(xn2, w_lin)], "nn", F32)
    act = _ffn_act(g_raw, u, fcw_full, ffn_conv_b)
    f = _mm("ffn_down", [(act, w_down_full)], "nn", F32)
    dh2, df, loss_row, dg_ffn_post = _loss_post(f, h1, target, ffn_post_g)

    dact = _mm("d_act", [(df, w_down_full)], "nt", F32)
    dw_down = _mm("dw_down", [(act, df)], "tn", F32)
    dg_conv, du = _ffn_bwd_act(dact, u, g_raw, fcw_full, ffn_conv_b)
    dg_raw, dfcw0, dfcw1, dfcw2, dfcb = _conv_bwd("ffn_conv_bwd", dg_conv, g_raw, fcw_full, FFN_CONV)
    dxn2 = _mm("d_xn2", [(dg_raw, w_gate), (du, w_lin)], "nt", F32)
    dw_gate = _mm("dw_gate", [(xn2, dg_raw)], "tn", F32)
    dw_lin = _mm("dw_lin", [(xn2, du)], "tn", F32)
    dh1, dmix, dg_ffn_pre, dg_mix_post = _mid_bwd(dxn2, h1, dh2, mix, ffn_pre_g, mix_post_g)

    dy_ssd = _mm("d_yssd", [(dmix, wo_ssd)], "nt", F32)
    dy_sb = _mm("d_ysb", [(dmix, wo_sb)], "nt", F32)
    dwo_ssd = _mm("dw_out_ssd", [(y_ssd, dmix)], "tn", F32)
    dwo_sb = _mm("dw_out_sb", [(y_sb, dmix)], "tn", F32)
    do, dg_sb = _norm_bwd("sb_norm_bwd", dy_sb, o, sb_norm_g)
    half = N_DEV // 2
    early_slabs = [
        jnp.concatenate([dwo_ssd, dwo_sb], axis=0).reshape(N_DEV, out_rows, D_MODEL),
        jnp.concatenate([dw_gate.reshape(D_MODEL, half, up_cols).transpose(1, 0, 2),
                         dw_lin.reshape(D_MODEL, half, up_cols).transpose(1, 0, 2)], axis=0),
        dw_down.reshape(N_DEV, down_rows, D_MODEL)]
    dq, dk, dv, (l_out, l_up, l_down) = _attn_bwd(q, k, v, keep_total, do, _Exchange(early_slabs, gather=False), early_slabs)
    dz, dxbc_act, ddtr, dg_ssd_norm, dd_skip, da_log, ddt_bias = _ssd_bwd(
        dy_ssd, ypre, z, xbc_act, dtr, states, dt_bias_p, a_log_p, d_p, ssd_norm_g)
    dconv = _ssd_conv_bwd_act(dxbc_act, xbc_raw, scw_full, ssd_conv_b)
    dxbc_raw, dscw0, dscw1, dscw2, dscw3, dscb = _conv_bwd("ssd_conv_bwd", dconv, xbc_raw, scw_full, SSD_CONV)
    segs = [(dz, w_z), (dxbc_raw, w_xbc), (ddtr, w_dt), (dq, w_q), (dk, w_k), (dv, w_v)]
    dw_segs = [_mm("dw_in_%d" % s, [(xn1, d)], "tn", BF16) for s, (d, _) in enumerate(segs)]
    dw_segs[2] = dw_segs[2][:, :HEADS]
    dw_in = jnp.concatenate(dw_segs, axis=1)
    chips = N_DEV // 2
    slab_in = dw_in.reshape(D_MODEL, N_DEV, in_cols).transpose(1, 0, 2)
    from_sibling = _sibling_swap("swap_dw_in", slab_in)
    pair = _pair_sum("pair_sum_dw_in", slab_in, from_sibling, lax.axis_index("c").astype(jnp.int32).reshape(1))
    dxn1, (l_in,) = _mm("d_xn1", segs, "nt", F32, _Exchange([pair], gather=False, same_core=True), [pair])
    dh0, dg_mix_pre = _first_bwd(dxn1, h0, dh1, mix_pre_g)
    grad_x = dh0[BLK:][None]

    small = [dg_mix_pre, dscb, ddt_bias, da_log, dd_skip, dg_ssd_norm, dg_sb, dg_mix_post, dg_ffn_pre, dfcb,
             dg_ffn_post, dh0[PAD:BLK].reshape(1, -1), dscw0, dscw1, dscw2, dscw3, dfcw0, dfcw1, dfcw2, loss_row]
    sizes = [a.shape[1] for a in small]
    total = sum(sizes)
    rows_packed = -(-total // (LANES * HALO)) * HALO
    packed = jnp.pad(jnp.concatenate(small, axis=1), ((0, 0), (0, rows_packed * LANES - total)))
    ((gathered,),) = _peer_exchange("gather_small_grads", [([packed.reshape(rows_packed, LANES)], True)])
    summed = _sum_slots("sum_small_grads", gathered).reshape(1, rows_packed * LANES)
    pieces, at = [], 0
    for s in sizes:
        pieces.append(summed[:, at:at + s])
        at += s
    (g_mix_pre, g_scb, g_dtb, g_alog, g_dskip, g_ssd_norm, g_sb, g_mix_post, g_ffn_pre, g_fcb, g_ffn_post,
     g_meta_flat, gs0, gs1, gs2, gs3, gf0, gf1, gf2, loss_all) = pieces
    loss = loss_all[0, 0]
    g_dtb, g_alog, g_dskip = g_dtb[:, :HEADS], g_alog[:, :HEADS], g_dskip[:, :HEADS]
    g_meta_full = g_meta_flat.reshape(N_META, D_MODEL)
    g_scw_full = jnp.concatenate([gs0, gs1, gs2, gs3], axis=0)
    g_fcw_full = jnp.concatenate([gf0, gf1, gf2], axis=0)
    meta_cols, scw_cols, fcw_cols = meta_tokens.shape[1], ssd_conv_w.shape[2], ffn_conv_w.shape[2]
    g_meta_mine = lax.dynamic_slice(g_meta_full, (0, me * meta_cols), (N_META, meta_cols))
    g_scw_mine = lax.dynamic_slice(g_scw_full, (0, me * scw_cols), (SSD_CONV, scw_cols))
    g_fcw_mine = lax.dynamic_slice(g_fcw_full, (0, me * fcw_cols), (FFN_CONV, fcw_cols))

    def lead(a):
        return a[None]

    upd = [
        _adamw("adamw_meta", meta_tokens, lead(g_meta_mine), m_meta_tokens, v_meta_tokens),
        _adamw("adamw_mix_pre_g", mix_pre_g, lead(g_mix_pre), m_mix_pre_g, v_mix_pre_g),
        [lead(a) for a in _adamw("adamw_w_in", w_in[0], l_in, m_w_in[0], v_w_in[0])],
        [lead(a) for a in _adamw("adamw_ssd_conv_w", ssd_conv_w[0], lead(g_scw_mine), m_ssd_conv_w[0], v_ssd_conv_w[0])],
        _adamw("adamw_ssd_conv_b", ssd_conv_b, lead(g_scb), m_ssd_conv_b, v_ssd_conv_b),
        _adamw("adamw_ssd_dt_bias", ssd_dt_bias, lead(g_dtb), m_ssd_dt_bias, v_ssd_dt_bias),
        _adamw("adamw_ssd_a_log", ssd_a_log, lead(g_alog), m_ssd_a_log, v_ssd_a_log),
        _adamw("adamw_ssd_d", ssd_d, lead(g_dskip), m_ssd_d, v_ssd_d),
        _adamw("adamw_ssd_norm_g", ssd_norm_g, lead(g_ssd_norm), m_ssd_norm_g, v_ssd_norm_g),
        _adamw("adamw_sb_norm_g", sb_norm_g, lead(g_sb), m_sb_norm_g, v_sb_norm_g),
        [lead(a) for a in _adamw("adamw_w_out", w_out[0], l_out, m_w_out[0], v_w_out[0])],
        _adamw("adamw_mix_post_g", mix_post_g, lead(g_mix_post), m_mix_post_g, v_mix_post_g),
        _adamw("adamw_ffn_pre_g", ffn_pre_g, lead(g_ffn_pre), m_ffn_pre_g, v_ffn_pre_g),
        [lead(a) for a in _adamw("adamw_w_up", w_up[0], l_up, m_w_up[0], v_w_up[0])],
        [lead(a) for a in _adamw("adamw_ffn_conv_w", ffn_conv_w[0], lead(g_fcw_mine), m_ffn_conv_w[0], v_ffn_conv_w[0])],
        _adamw("adamw_ffn_conv_b", ffn_conv_b, lead(g_fcb), m_ffn_conv_b, v_ffn_conv_b),
        [lead(a) for a in _adamw("adamw_w_down", w_down[0], l_down, m_w_down[0], v_w_down[0])],
        _adamw("adamw_ffn_post_g", ffn_post_g, lead(g_ffn_post), m_ffn_post_g, v_ffn_post_g),
    ]
    grads = [u_[0] for u_ in upd]
    deltas = [u_[1] for u_ in upd]
    new_m = [u_[2] for u_ in upd]
    new_v = [u_[3] for u_ in upd]
    return (loss, grad_x, *grads, *deltas, *new_m, *new_v)
```

```python
import math

import jax
import jax.numpy as jnp
from jax import lax
from jax.experimental import pallas as pl
from jax.experimental.pallas import tpu as pltpu

F32 = jnp.float32
BF16 = jnp.bfloat16
HI = lax.Precision.HIGHEST

D_MODEL = 1024
N_META = 16
BLK = 128
PAD = BLK - N_META
HEADS = 16
HEAD_DIM = 64
SSD_GROUPS = 2
SSD_STATE = 128
HEADS_PER_GROUP = HEADS // SSD_GROUPS
SSD_INNER = HEADS * HEAD_DIM
SSD_CONV = 4
XBC = SSD_INNER + 2 * SSD_GROUPS * SSD_STATE
OFF_B = SSD_INNER
OFF_C = SSD_INNER + SSD_GROUPS * SSD_STATE
D_FF = 2816
FFN_CONV = 3
EPS = 1e-6
SB_SCALE = 1.0 / math.sqrt(HEAD_DIM)
N_DEV = 8
LANES = 128
HALO = 8

ADAM_LR = 0.001
ADAM_B1 = 0.9
ADAM_B2 = 0.999
ADAM_EPS = 1e-08
ADAM_WD = 0.01
ADAM_STEP = 10

VMEM_FLOOR = 32 << 20
VMEM_CEIL = 60 << 20
MM_BUDGET = 20 << 20
ROW_BUDGET = 6 << 20

NN = (((1,), (0,)), ((), ()))
NT = (((1,), (1,)), ((), ()))
TN = (((0,), (0,)), ((), ()))


def _params(tile_bytes, sem=None):
    limit = int(min(max(2 * tile_bytes + (8 << 20), VMEM_FLOOR), VMEM_CEIL))
    return pltpu.CompilerParams(vmem_limit_bytes=limit, dimension_semantics=sem)


def _nbytes(shape, dtype):
    n = 1
    for s in shape:
        n *= s
    return n * jnp.dtype(dtype).itemsize


def _dot(a, b, dims=NN, precision=None):
    return lax.dot_general(a, b, dims, precision=precision, preferred_element_type=F32)


def _softplus(x):
    return jnp.maximum(x, 0.0) + jnp.log1p(jnp.exp(-jnp.abs(x)))


def _rms(x, g):
    r = lax.rsqrt(jnp.mean(x * x, axis=-1, keepdims=True) + EPS)
    return x * r * g


def _rms_bwd(dy, x, g):
    r = lax.rsqrt(jnp.mean(x * x, axis=-1, keepdims=True) + EPS)
    xh = x * r
    u = dy * g
    dx = r * (u - xh * jnp.mean(xh * u, axis=-1, keepdims=True))
    return dx, jnp.sum(dy * xh, axis=0, keepdims=True)


def _gelu(x):
    c = math.sqrt(2.0 / math.pi)
    return 0.5 * x * (1.0 + jnp.tanh(c * (x + 0.044715 * x * x * x)))


def _gelu_and_grad(x):
    c = math.sqrt(2.0 / math.pi)
    x2 = x * x
    t = jnp.tanh(c * (x + 0.044715 * x2 * x))
    half = 0.5 * (1.0 + t)
    return x * half, half + 0.5 * x * (1.0 - t * t) * c * (1.0 + 3.0 * 0.044715 * x2)


def _row_tile(rows, bytes_per_row):
    big = 384
    return big if rows % big == 0 and big * bytes_per_row <= ROW_BUDGET else BLK


def _mm(name, pairs, mode, out_dtype, ex=None, ex_arrays=()):
    a0, b0 = pairs[0]
    if mode == "tn":
        m, n = a0.shape[1], b0.shape[1]
    elif mode == "nt":
        m, n = a0.shape[0], b0.shape[0]
    else:
        m, n = a0.shape[0], b0.shape[1]
    dims = {"nn": NN, "nt": NT, "tn": TN}[mode]

    def tile_bytes(tm, tn):
        tot = tm * tn * jnp.dtype(out_dtype).itemsize
        for a, b in pairs:
            k = a.shape[0] if mode == "tn" else a.shape[1]
            tot += tm * k * a.dtype.itemsize + tn * k * b.dtype.itemsize
        return tot

    cands_m = [t for t in (1408, 1024, 512, 384, 256, 128) if m % t == 0] or [m]
    cands_n = [t for t in (1408, 1024, 768, 512, 256, 128) if n % t == 0] or [n]
    best = None
    for tm in cands_m:
        for tn in cands_n:
            if tile_bytes(tm, tn) <= MM_BUDGET and (best is None or tm * tn > best[0] * best[1]):
                best = (tm, tn)
    tm, tn = best if best is not None else (cands_m[-1], cands_n[-1])
    npairs = len(pairs)

    grid = (m // tm, n // tn)

    def body(*refs):
        if ex is None:
            ins, o_ref = refs[:2 * npairs], refs[2 * npairs]
        else:
            ins, ex_in, (o_ref,), ex_out, _, sems = _ride_along(ex, 2 * npairs, 1, refs)
            step = pl.program_id(0) * grid[1] + pl.program_id(1)

            @pl.when(step == 0)
            def _():
                ex.start(ex_in, ex_out, sems)

        acc = None
        for p in range(npairs):
            part = _dot(ins[2 * p][...], ins[2 * p + 1][...], dims)
            acc = part if acc is None else acc + part
        o_ref[...] = acc.astype(o_ref.dtype)
        if ex is not None:
            @pl.when(step == grid[0] * grid[1] - 1)
            def _():
                ex.wait(ex_in, ex_out, sems)

    in_specs, args = [], []
    for a, b in pairs:
        if mode == "tn":
            k = a.shape[0]
            in_specs += [pl.BlockSpec((k, tm), lambda i, j: (0, i)), pl.BlockSpec((k, tn), lambda i, j: (0, j))]
        elif mode == "nt":
            k = a.shape[1]
            in_specs += [pl.BlockSpec((tm, k), lambda i, j: (i, 0)), pl.BlockSpec((tn, k), lambda i, j: (j, 0))]
        else:
            k = a.shape[1]
            in_specs += [pl.BlockSpec((tm, k), lambda i, j: (i, 0)), pl.BlockSpec((k, tn), lambda i, j: (0, j))]
        args += [a, b]
    out_spec = pl.BlockSpec((tm, tn), lambda i, j: (i, j))
    out_shape = jax.ShapeDtypeStruct((m, n), out_dtype)
    if ex is None:
        return pl.pallas_call(
            body, name=name, grid=grid, in_specs=in_specs, out_specs=out_spec, out_shape=out_shape,
            compiler_params=_params(tile_bytes(tm, tn), ("parallel", "parallel")),
        )(*args)
    res = pl.pallas_call(
        body, name=name, grid=grid, in_specs=in_specs + ex.specs, out_specs=[out_spec] + ex.specs,
        out_shape=[out_shape] + ex.out_shape, scratch_shapes=ex.scratch,
        compiler_params=_params(tile_bytes(tm, tn), ("arbitrary", "arbitrary")),
    )(*args, *ex_arrays)
    return res[0], res[1:]


class _Window:
    def __init__(self, ref, cols):
        self.ref, self.cols = ref, cols

    def __getitem__(self, idx):
        return self.ref[:, self.cols] if idx is Ellipsis else self.ref[idx[0], self.cols]


def _rowcall(name, fn, rows=(), prevs=(), nexts=(), pars=(), out_rows=(), out_accs=(), chunk=None, late=()):
    rows, prevs, nexts, pars, late = list(rows), list(prevs), list(nexts), list(pars), list(late)
    n_rows = (rows + prevs + nexts)[0].shape[0]
    per_row = sum(a.shape[1] * a.dtype.itemsize for a in rows + prevs + nexts + late)
    per_row += sum(c * jnp.dtype(dt).itemsize for c, dt in out_rows) + sum(a.shape[1] * 4 for a in prevs + nexts)
    tm = _row_tile(n_rows, per_row)
    nt = n_rows // tm
    hb = tm // HALO
    if late:
        tm = BLK
        nt, hb = n_rows // tm, tm // HALO
    rows = rows + late
    nr, npv, nnx, npar, nor, noa = len(rows), len(prevs), len(nexts), len(pars), len(out_rows), len(out_accs)

    def body(*refs):
        i = pl.program_id(0)
        k = 0
        row_refs = refs[k:k + nr]; k += nr
        pc = refs[k:k + npv]; k += npv
        ph = refs[k:k + npv]; k += npv
        nc = refs[k:k + nnx]; k += nnx
        nh = refs[k:k + nnx]; k += nnx
        par_refs = refs[k:k + npar]; k += npar
        orow = refs[k:k + nor]; k += nor
        oacc = refs[k:k + noa]; k += noa
        pscr = refs[k:k + npv]; k += npv
        nscr = refs[k:k + nnx]
        for c_, h_, s_ in zip(pc, ph, pscr):
            s_[0:HALO, :] = h_[...] * (i > 0).astype(F32)
            s_[HALO:HALO + tm, :] = c_[...]
        for c_, h_, s_ in zip(nc, nh, nscr):
            s_[0:tm, :] = c_[...]
            s_[tm:tm + HALO, :] = h_[...] * (i < nt - 1).astype(F32)
        if noa:
            @pl.when(i == 0)
            def _():
                for r_ in oacc:
                    r_[...] = jnp.zeros_like(r_)
        width = out_rows[0][0]
        windows = [slice(None)] if chunk is None else [slice(c0, c0 + chunk) for c0 in range(0, width, chunk)]
        for cols in windows:
            prev_fns = [(lambda s, s_=s_, cols=cols: s_[pl.ds(HALO, tm), cols] if s == 0 else
                         pltpu.roll(s_[:, cols], s, 0)[HALO:HALO + tm]) for s_ in pscr]
            next_fns = [(lambda s, s_=s_, cols=cols: s_[pl.ds(0, tm), cols] if s == 0 else
                         pltpu.roll(s_[:, cols], tm + HALO - s, 0)[0:tm]) for s_ in nscr]
            row_vals, acc_vals = fn(i, tm, [_Window(r_, cols) for r_ in row_refs], prev_fns, next_fns,
                                    [_Window(r_, cols) for r_ in par_refs])
            for r_, v in zip(orow, row_vals):
                r_[:, cols] = v.astype(r_.dtype)
            for r_, v in zip(oacc, acc_vals):
                r_[:, cols] += v

    def row_spec(a):
        return pl.BlockSpec((tm, a.shape[1]), lambda i: (i, 0))

    def whole(shape):
        return pl.BlockSpec(shape, lambda i: (0, 0))

    in_specs = [row_spec(a) for a in rows[:nr - len(late)]]
    in_specs += [pl.BlockSpec((tm, a.shape[1]), lambda i: (jnp.maximum(i - 1, 0), 0)) for a in late]
    in_specs += [row_spec(a) for a in prevs]
    in_specs += [pl.BlockSpec((HALO, a.shape[1]), lambda i: (jnp.maximum(i * hb - 1, 0), 0)) for a in prevs]
    in_specs += [row_spec(a) for a in nexts]
    in_specs += [pl.BlockSpec((HALO, a.shape[1]), lambda i: (jnp.minimum((i + 1) * hb, n_rows // HALO - 1), 0)) for a in nexts]
    in_specs += [whole(p.shape) for p in pars]
    out_specs = [pl.BlockSpec((tm, c), lambda i: (i, 0)) for c, _ in out_rows] + [whole(s) for s in out_accs]
    out_shape = [jax.ShapeDtypeStruct((n_rows, c), dt) for c, dt in out_rows]
    out_shape += [jax.ShapeDtypeStruct(s, F32) for s in out_accs]
    scratch = [pltpu.VMEM((tm + HALO, a.shape[1]), F32) for a in prevs + nexts]
    tile = tm * per_row
    res = pl.pallas_call(
        body, name=name, grid=(nt,), in_specs=in_specs, out_specs=out_specs, out_shape=out_shape,
        scratch_shapes=scratch, compiler_params=_params(2 * tile, ("arbitrary",)),
    )(*rows, *prevs, *prevs, *nexts, *nexts, *pars)
    return res


def _row_ids(i, tm):
    return i * tm + lax.broadcasted_iota(jnp.int32, (tm, 1), 0)


def _conv_taps(prev_fn, w_ref, b_ref, taps):
    acc = b_ref[...]
    for k in range(taps):
        acc = acc + prev_fn(taps - 1 - k) * w_ref[k:k + 1, :]
    return acc


def _rms_fwd(name, h, g):
    def fn(i, tm, rows, prevs, nexts, pars):
        return [_rms(rows[0][...], pars[0][...])], []
    return _rowcall(name, fn, rows=[h], pars=[g], out_rows=[(h.shape[1], BF16)])[0]


def _ssd_conv_fwd(xbc_raw, w, b):
    def fn(i, tm, rows, prevs, nexts, pars):
        c = _conv_taps(prevs[0], pars[0], pars[1], SSD_CONV)
        y = c * jax.nn.sigmoid(c)
        return [jnp.where(_row_ids(i, tm) >= PAD, y, 0.0)], []
    return _rowcall("ssd_conv_fwd", fn, prevs=[xbc_raw], pars=[w, b], out_rows=[(XBC, F32)], chunk=LANES)[0]


def _mix_post(mix, h0, g_post, g_pre):
    def fn(i, tm, rows, prevs, nexts, pars):
        h1 = rows[1][...] + _rms(rows[0][...], pars[0][...])
        return [h1, _rms(h1, pars[1][...])], []
    return _rowcall("mix_post", fn, rows=[mix, h0], pars=[g_post, g_pre],
                    out_rows=[(D_MODEL, F32), (D_MODEL, BF16)])


def _ffn_act(g_raw, u, w, b):
    def fn(i, tm, rows, prevs, nexts, pars):
        g = _conv_taps(prevs[0], pars[0], pars[1], FFN_CONV)
        return [_gelu(g) * rows[0][...]], []
    return _rowcall("ffn_act", fn, rows=[u], prevs=[g_raw], pars=[w, b], out_rows=[(D_FF, BF16)], chunk=LANES)[0]


def _loss_post(f, h1, target, g_post):
    def fn(i, tm, rows, prevs, nexts, pars):
        fv, g = rows[0][...], pars[0][...]
        h2 = rows[1][...] + _rms(fv, g)
        real = _row_ids(i, tm) >= BLK
        diff = jnp.where(real, h2 - rows[2][...], 0.0)
        loss = 0.5 * jnp.sum(jnp.mean(diff * diff, axis=-1, keepdims=True))
        dh2 = diff * (1.0 / D_MODEL)
        df, dg = _rms_bwd(dh2, fv, g)
        return [dh2, df], [jnp.zeros((1, LANES), F32) + loss, dg]
    return _rowcall("loss_post", fn, rows=[f, h1], late=[target], pars=[g_post],
                    out_rows=[(D_MODEL, F32), (D_MODEL, BF16)], out_accs=[(1, LANES), (1, D_MODEL)])


def _ffn_bwd_act(dact, u, g_raw, w, b):
    def fn(i, tm, rows, prevs, nexts, pars):
        g = _conv_taps(prevs[0], pars[0], pars[1], FFN_CONV)
        da = rows[0][...]
        gelu, grad = _gelu_and_grad(g)
        return [da * rows[1][...] * grad, da * gelu], []
    return _rowcall("ffn_bwd_act", fn, rows=[dact, u], prevs=[g_raw], pars=[w, b],
                    out_rows=[(D_FF, F32), (D_FF, BF16)], chunk=LANES)


def _conv_bwd(name, dy, x, w, taps):
    width = x.shape[1]

    def fn(i, tm, rows, prevs, nexts, pars):
        dy0 = nexts[0](0)
        dx = None
        for k in range(taps):
            term = nexts[0](taps - 1 - k) * pars[0][k:k + 1, :]
            dx = term if dx is None else dx + term
        dws = [jnp.sum(dy0 * prevs[0](taps - 1 - k), axis=0, keepdims=True) for k in range(taps)]
        return [dx], dws + [jnp.sum(dy0, axis=0, keepdims=True)]
    return _rowcall(name, fn, prevs=[x], nexts=[dy], pars=[w], out_rows=[(width, BF16)],
                    out_accs=[(1, width)] * (taps + 1), chunk=LANES)


def _mid_bwd(dxn2, h1, dh2, mix, g_pre, g_post):
    def fn(i, tm, rows, prevs, nexts, pars):
        d1, dg_pre = _rms_bwd(rows[0][...], rows[1][...], pars[0][...])
        dh1 = rows[2][...] + d1
        dmix, dg_post = _rms_bwd(dh1, rows[3][...], pars[1][...])
        return [dh1, dmix], [dg_pre, dg_post]
    return _rowcall("mid_bwd", fn, rows=[dxn2, h1, dh2, mix], pars=[g_pre, g_post],
                    out_rows=[(D_MODEL, F32), (D_MODEL, BF16)], out_accs=[(1, D_MODEL), (1, D_MODEL)])


def _norm_bwd(name, dy, x, g):
    def fn(i, tm, rows, prevs, nexts, pars):
        dx, dg = _rms_bwd(rows[0][...], rows[1][...], pars[0][...])
        return [dx], [dg]
    return _rowcall(name, fn, rows=[dy, x], pars=[g], out_rows=[(x.shape[1], BF16)], out_accs=[(1, x.shape[1])])


def _ssd_conv_bwd_act(dact, xbc_raw, w, b):
    def fn(i, tm, rows, prevs, nexts, pars):
        c = _conv_taps(prevs[0], pars[0], pars[1], SSD_CONV)
        s = jax.nn.sigmoid(c)
        dc = rows[0][...] * s * (1.0 + c * (1.0 - s))
        return [jnp.where(_row_ids(i, tm) >= PAD, dc, 0.0)], []
    return _rowcall("ssd_conv_bwd_act", fn, rows=[dact], prevs=[xbc_raw], pars=[w, b], out_rows=[(XBC, F32)],
                    chunk=LANES)[0]


def _first_bwd(dxn1, h0, dh1, g):
    def fn(i, tm, rows, prevs, nexts, pars):
        d0, dg = _rms_bwd(rows[0][...], rows[1][...], pars[0][...])
        return [rows[2][...] + d0], [dg]
    return _rowcall("first_bwd", fn, rows=[dxn1, h0, dh1], pars=[g], out_rows=[(D_MODEL, F32)],
                    out_accs=[(1, D_MODEL)])


def _ssd_chunk_terms(c, dtr_ref, bias_ref, alog_ref):
    ri = lax.broadcasted_iota(jnp.int32, (BLK, BLK), 0)
    ci = lax.broadcasted_iota(jnp.int32, (BLK, BLK), 1)
    causal = ri >= ci
    tril = causal.astype(F32)
    triu = (ri <= ci).astype(F32)
    rowmask = ((c * BLK + lax.broadcasted_iota(jnp.int32, (BLK, 1), 0)) >= PAD).astype(F32)
    dt = _softplus(dtr_ref[...] + bias_ref[...]) * rowmask
    a_neg = -jnp.exp(alog_ref[...])
    a = dt * a_neg
    cs = _dot(tril, a, NN, HI)
    cs_t = _dot(a, triu, TN, HI)
    return causal, triu, rowmask, dt, a_neg, cs, cs_t


def _decay_matrix(causal, cs_h, cs_t_h):
    return jnp.where(causal, jnp.exp(jnp.where(causal, cs_h - cs_t_h, 0.0)), 0.0)


def _head_spread(width):
    ri = lax.broadcasted_iota(jnp.int32, (LANES, HEADS * width), 0)
    ci = lax.broadcasted_iota(jnp.int32, (LANES, HEADS * width), 1)
    return ((ri * width <= ci) & (ci < (ri + 1) * width)).astype(BF16)


def _three_terms(x):
    hi = x.astype(BF16)
    rest = x - hi.astype(F32)
    mid = rest.astype(BF16)
    lo = (rest - mid.astype(F32)).astype(BF16)
    return jnp.concatenate([hi, mid, lo], axis=1)


def _spread(x, sel):
    return _dot(_three_terms(x), jnp.concatenate([sel, sel, sel], axis=0))


def _lane_sums(y, sel):
    return _dot(_three_terms(y), jnp.concatenate([sel, sel, sel], axis=1), NT)


def _ssd_spreads(dt, cs, d_ref, sel64, sel128):
    dt64 = _spread(dt, sel64)
    cs64 = _spread(cs, sel64)
    cs128 = _spread(cs, sel128)
    d64 = _spread(jnp.broadcast_to(d_ref[...], (HALO, LANES)), sel64)[0:1, :]
    cl64 = cs64[BLK - 1:BLK, :]
    return dt64, cs128, d64, jnp.exp(cs64), jnp.exp(cl64 - cs64), jnp.exp(cl64)


def _ssd_pair_terms(p, xbc_ref, dt64, cs128, cs_t, causal):
    lanes = _pair_lanes(p)
    xs = xbc_ref[:, lanes]
    decays = [_decay_matrix(causal, cs128[:, h * LANES:(h + 1) * LANES], cs_t[h:h + 1, :]) for h in (2 * p, 2 * p + 1)]
    return xs, xs * dt64[:, lanes], decays


def _ssd_fwd(xbc, dtr, z, dt_bias, a_log, d_skip, norm_g):
    n_rows = xbc.shape[0]
    nb = n_rows // BLK

    def body(xbc_ref, dtr_ref, z_ref, bias_ref, alog_ref, d_ref, g_ref, s64_ref, s128_ref,
             ypre_ref, yssd_ref, st_ref, state):
        c = pl.program_id(0)

        @pl.when(c == 0)
        def _():
            state[...] = jnp.zeros_like(state)

        st_ref[...] = state[...]
        causal, _, _, dt, _, cs, cs_t = _ssd_chunk_terms(c, dtr_ref, bias_ref, alog_ref)
        first = lax.broadcasted_iota(jnp.int32, (BLK, LANES), 1) < HEAD_DIM
        dt64, cs128, d64, from_start, to_end, chunk_decay = _ssd_spreads(dt, cs, d_ref, s64_ref[...], s128_ref[...])
        for g in range(SSD_GROUPS):
            b_b = xbc_ref[:, OFF_B + g * SSD_STATE:OFF_B + (g + 1) * SSD_STATE].astype(BF16)
            c_b = xbc_ref[:, OFF_C + g * SSD_STATE:OFF_C + (g + 1) * SSD_STATE].astype(BF16)
            cb = _dot(c_b, b_b, NT)
            for j in range(HEADS_PER_GROUP // 2):
                p = g * (HEADS_PER_GROUP // 2) + j
                lanes = _pair_lanes(p)
                xs, x_dt, decays = _ssd_pair_terms(p, xbc_ref, dt64, cs128, cs_t, causal)
                ms = [(cb * d).astype(BF16) for d in decays]
                s_p = state[:, lanes]
                y = _dot(jnp.concatenate(ms, axis=1), _head_stack(x_dt.astype(BF16), first))
                y = y + from_start[:, lanes] * _dot(c_b, s_p.astype(BF16))
                state[:, lanes] = chunk_decay[:, lanes] * s_p + _dot(b_b, (to_end[:, lanes] * x_dt).astype(BF16), TN)
                ypre_ref[:, lanes] = y + d64[:, lanes] * xs
        zz = z_ref[...]
        yg = ypre_ref[...] * (zz * jax.nn.sigmoid(zz))
        yssd_ref[...] = _rms(yg, g_ref[...]).astype(yssd_ref.dtype)

    blk = lambda w: pl.BlockSpec((BLK, w), lambda c: (c, 0))
    par = lambda a: pl.BlockSpec(a.shape, lambda c: (0, 0))
    sel64, sel128 = _head_spread(HEAD_DIM), _head_spread(LANES)
    return pl.pallas_call(
        body, name="ssd_fwd", grid=(nb,),
        in_specs=[blk(XBC), blk(LANES), blk(SSD_INNER), par(dt_bias), par(a_log), par(d_skip), par(norm_g),
                  par(sel64), par(sel128)],
        out_specs=[blk(SSD_INNER), blk(SSD_INNER), pl.BlockSpec((None, SSD_STATE, SSD_INNER), lambda c: (c, 0, 0))],
        out_shape=[jax.ShapeDtypeStruct((n_rows, SSD_INNER), F32), jax.ShapeDtypeStruct((n_rows, SSD_INNER), BF16),
                   jax.ShapeDtypeStruct((nb, SSD_STATE, SSD_INNER), F32)],
        scratch_shapes=[pltpu.VMEM((SSD_STATE, SSD_INNER), F32)],
        compiler_params=_params(8 << 20, ("arbitrary",)),
    )(xbc, dtr, z, dt_bias, a_log, d_skip, norm_g, sel64, sel128)


def _ssd_bwd(dy, ypre, z, xbc, dtr, states, dt_bias, a_log, d_skip, norm_g):
    n_rows = xbc.shape[0]
    nb = n_rows // BLK

    def body(dy_ref, ypre_ref, z_ref, xbc_ref, dtr_ref, st_ref, bias_ref, alog_ref, d_ref, g_ref, s64_ref, s128_ref,
             dz_ref, dxbc_ref, ddtr_ref, dgn_ref, dd_ref, dal_ref, ddtb_ref, dstate, dyp, red):
        step = pl.program_id(0)
        c = nb - 1 - step

        @pl.when(step == 0)
        def _():
            dstate[...] = jnp.zeros_like(dstate)
            for r_ in (dgn_ref, dd_ref, dal_ref, ddtb_ref):
                r_[...] = jnp.zeros_like(r_)

        yp, zz = ypre_ref[...], z_ref[...]
        sz = jax.nn.sigmoid(zz)
        silu = zz * sz
        dyg, dgn = _rms_bwd(dy_ref[...], yp * silu, g_ref[...])
        dgn_ref[...] += dgn
        dz_ref[...] = (dyg * yp * (sz * (1.0 + zz * (1.0 - sz)))).astype(dz_ref.dtype)
        dyp[...] = dyg * silu

        causal, triu, rowmask, dt, a_neg, cs, cs_t = _ssd_chunk_terms(c, dtr_ref, bias_ref, alog_ref)
        lane = lax.broadcasted_iota(jnp.int32, (1, LANES), 1)
        last_row = (lax.broadcasted_iota(jnp.int32, (BLK, 1), 0) == BLK - 1).astype(F32)
        first = lax.broadcasted_iota(jnp.int32, (BLK, LANES), 1) < HEAD_DIM
        sel64 = s64_ref[...]
        dt64, cs128, d64, from_start, to_end, chunk_decay = _ssd_spreads(dt, cs, d_ref, sel64, s128_ref[...])
        for g in range(SSD_GROUPS):
            b_b = xbc_ref[:, OFF_B + g * SSD_STATE:OFF_B + (g + 1) * SSD_STATE].astype(BF16)
            c_b = xbc_ref[:, OFF_C + g * SSD_STATE:OFF_C + (g + 1) * SSD_STATE].astype(BF16)
            cb = _dot(c_b, b_b, NT)
            b_twice = jnp.concatenate([b_b, b_b], axis=0)
            c_twice = jnp.concatenate([c_b, c_b], axis=0)
            db_g = jnp.zeros((BLK, SSD_STATE), F32)
            dc_g = jnp.zeros((BLK, SSD_STATE), F32)
            for j in range(HEADS_PER_GROUP // 2):
                p = g * (HEADS_PER_GROUP // 2) + j
                lanes = _pair_lanes(p)
                xs, x_dt, decays = _ssd_pair_terms(p, xbc_ref, dt64, cs128, cs_t, causal)
                ms = [(cb * d).astype(BF16) for d in decays]
                d_y = dyp[:, lanes]
                s_p, ds_p = st_ref[:, lanes], dstate[:, lanes]
                s_b, ds_b = s_p.astype(BF16), ds_p.astype(BF16)
                fs, te = from_start[:, lanes], to_end[:, lanes]
                x_b, dy_b = x_dt.astype(BF16), d_y.astype(BF16)
                x_st = _head_stack(x_b, first)
                fs_dy = (fs * d_y).astype(BF16)

                y_diag = _dot(jnp.concatenate(ms, axis=1), x_st)
                y_off = fs * _dot(c_b, s_b)
                end_part = te * _dot(b_b, ds_b)
                dx_diag = _dot(jnp.concatenate(ms, axis=0), _head_stack(dy_b, first), TN)
                d_x = dx_diag + end_part
                g2 = _dot(dy_b, x_st, NT)
                gl = [(g2[:, k * BLK:(k + 1) * BLK] * decays[k]).astype(BF16) for k in range(2)]
                dc_g = dc_g + _dot(jnp.concatenate(gl, axis=1), b_twice) + _dot(fs_dy, s_b, NT)
                db_g = db_g + _dot(jnp.concatenate(gl, axis=0), c_twice, TN) + _dot((te * x_dt).astype(BF16), ds_b, NT)
                red[0:BLK, lanes] = (dy_b.astype(F32) * y_diag - x_b.astype(F32) * dx_diag) + (d_y * y_off - x_dt * end_part)
                red[BLK:2 * BLK, lanes] = x_dt * end_part
                red[2 * BLK:3 * BLK, lanes] = d_x * xs
                red[3 * BLK:3 * BLK + HALO, lanes] = jnp.broadcast_to(jnp.sum(ds_p * s_p, axis=0, keepdims=True), (HALO, LANES))
                red[3 * BLK + HALO:3 * BLK + 2 * HALO, lanes] = jnp.broadcast_to(
                    jnp.sum(d_y * xs, axis=0, keepdims=True), (HALO, LANES))
                dxbc_ref[:, lanes] = d_x * dt64[:, lanes] + d64[:, lanes] * d_y
                dstate[:, lanes] = chunk_decay[:, lanes] * ds_p + _dot(c_b, fs_dy, TN)
            dxbc_ref[:, OFF_B + g * SSD_STATE:OFF_B + (g + 1) * SSD_STATE] = db_g
            dxbc_ref[:, OFF_C + g * SSD_STATE:OFF_C + (g + 1) * SSD_STATE] = dc_g
        sums = _lane_sums(red[...], sel64)
        at_last = (jnp.sum(sums[BLK:2 * BLK], axis=0, keepdims=True)
                   + jnp.exp(cs[BLK - 1:BLK, :]) * sums[3 * BLK:3 * BLK + 1])
        dcs = sums[0:BLK] + last_row * at_last
        ddt_x = sums[2 * BLK:3 * BLK]
        dd_row = sums[3 * BLK + HALO:3 * BLK + HALO + 1]
        da = _dot(triu, dcs, NN, HI)
        ddt = (da * a_neg + ddt_x) * rowmask
        ddtr = ddt * jax.nn.sigmoid(dtr_ref[...] + bias_ref[...]) * (lane < HEADS).astype(F32)
        ddtr_ref[...] = ddtr.astype(ddtr_ref.dtype)
        ddtb_ref[...] += jnp.sum(ddtr, axis=0, keepdims=True)
        dal_ref[...] += jnp.sum(da * dt, axis=0, keepdims=True) * a_neg
        dd_ref[...] += dd_row

    blk = lambda w: pl.BlockSpec((BLK, w), lambda s: (nb - 1 - s, 0))
    par = lambda a: pl.BlockSpec(a.shape, lambda s: (0, 0))
    acc = lambda w: pl.BlockSpec((1, w), lambda s: (0, 0))
    sel64, sel128 = _head_spread(HEAD_DIM), _head_spread(LANES)
    return pl.pallas_call(
        body, name="ssd_bwd", grid=(nb,),
        in_specs=[blk(SSD_INNER), blk(SSD_INNER), blk(SSD_INNER), blk(XBC), blk(LANES),
                  pl.BlockSpec((None, SSD_STATE, SSD_INNER), lambda s: (nb - 1 - s, 0, 0)),
                  par(dt_bias), par(a_log), par(d_skip), par(norm_g), par(sel64), par(sel128)],
        out_specs=[blk(SSD_INNER), blk(XBC), blk(LANES), acc(SSD_INNER), acc(LANES), acc(LANES), acc(LANES)],
        out_shape=[jax.ShapeDtypeStruct((n_rows, SSD_INNER), BF16), jax.ShapeDtypeStruct((n_rows, XBC), F32),
                   jax.ShapeDtypeStruct((n_rows, LANES), BF16), jax.ShapeDtypeStruct((1, SSD_INNER), F32),
                   jax.ShapeDtypeStruct((1, LANES), F32), jax.ShapeDtypeStruct((1, LANES), F32),
                   jax.ShapeDtypeStruct((1, LANES), F32)],
        scratch_shapes=[pltpu.VMEM((SSD_STATE, SSD_INNER), F32), pltpu.VMEM((BLK, SSD_INNER), F32),
                        pltpu.VMEM((3 * BLK + 2 * HALO, SSD_INNER), F32)],
        compiler_params=_params(12 << 20, ("arbitrary",)),
    )(dy, ypre, z, xbc, dtr, states, dt_bias, a_log, d_skip, norm_g, sel64, sel128)


HEAD_GROUP = 4


LOG2_E = 1.4426950408889634
SOFTPLUS_CLAMP = 80.0


def _sb_logits(zl, valid):
    z2 = zl * LOG2_E
    lost = jnp.maximum(jnp.log2(1.0 + jnp.exp2(jnp.minimum(z2, SOFTPLUS_CLAMP))), z2)
    log_beta = z2 - lost
    if valid is not None:
        lost = jnp.where(valid, lost, 0.0)
    return log_beta, lost


def _tri_and_ones(tri, sign=1.0):
    half = sign * jnp.concatenate([tri, jnp.ones((BLK, BLK), F32)], axis=1)
    return jnp.concatenate([half, half], axis=0).astype(BF16)


def _tile_mask(i, j, ri, ci):
    key = j * BLK + ci
    return (key < i * BLK + ri) & (key >= PAD)


def _pair_lanes(p):
    return slice(p * 2 * HEAD_DIM, (p + 1) * 2 * HEAD_DIM)


def _head_stack(x_pair, first):
    zero = jnp.zeros_like(x_pair)
    return jnp.concatenate([jnp.where(first, x_pair, zero), jnp.where(first, zero, x_pair)], axis=0)


def _block_rows(j):
    return pl.ds(j * BLK if isinstance(j, int) else pl.multiple_of(j * BLK, BLK), BLK)


def _two_terms(x):
    hi = x.astype(BF16)
    return jnp.concatenate([hi, (x - hi.astype(F32)).astype(BF16)], axis=1)


def _ride_along(ex, n_in, n_out, refs):
    k = 0
    parts = []
    for cnt in (n_in, ex.n, n_out, ex.n):
        parts.append(refs[k:k + cnt])
        k += cnt
    n_sems = len(ex.scratch)
    return (*parts, refs[k:len(refs) - n_sems], refs[len(refs) - n_sems:])


def _qkv_specs(n_rows, gw):
    per = HEADS * HEAD_DIM // gw
    return [pl.BlockSpec((n_rows, gw), lambda g, o=o: (0, g + o * per)) for o in range(3)]


def _attn_fwd(q, k, v, ex, ex_arrays):
    n_rows, width = q.shape[0], HEADS * HEAD_DIM
    nb = n_rows // BLK
    n_heads = HEAD_GROUP
    gw = n_heads * HEAD_DIM
    groups = width // gw

    def body(*refs):
        (q_ref, k_ref, v_ref), ex_in, (o_ref, tot_ref), ex_out, (run_ref, z_ref, w_ref), sems = _ride_along(ex, 3, 2, refs)
        step = pl.program_id(0)

        @pl.when(step == 0)
        def _():
            ex.start(ex_in, ex_out, sems)

        ri = lax.broadcasted_iota(jnp.int32, (BLK, BLK), 0)
        ci = lax.broadcasted_iota(jnp.int32, (BLK, BLK), 1)
        sums = _tri_and_ones((ri > ci).astype(F32), -1.0)
        first = ci < HEAD_DIM
        heads, pairs = range(n_heads), range(n_heads // 2)

        def stacks(ref, blocks, p):
            return jnp.concatenate([_head_stack(ref[_block_rows(jnp.clip(j, 0, nb - 1)), _pair_lanes(p)], first)
                                    for j in blocks], axis=0)

        def q_block(i, carry):
            rows = _block_rows(i)
            o_ref[rows, :] = jnp.zeros((BLK, gw), F32)
            run_ref[...] = jnp.zeros_like(run_ref)
            w_ref[...] = jnp.zeros_like(w_ref)

            odd = (i + 1) % 2

            def blocks(t):
                return i + odd - 2 * t, i + odd - 2 * t - 1

            for p in pairs:
                z_ref[p] = _dot(q_ref[rows, _pair_lanes(p)], stacks(k_ref, blocks(0), p), NT)

            def tile(t, masked, skip=None):
                q_i, o_i = q_ref[rows, :], o_ref[rows, :]
                outs = [_dot(w_ref[p], stacks(v_ref, blocks(jnp.maximum(t - 1, 0)), p)) for p in pairs]
                z_next = [_dot(q_i[:, _pair_lanes(p)], stacks(k_ref, blocks(t + 1), p), NT) for p in pairs]
                runs = [run_ref[h] for h in heads]
                lgs, res = {}, {}
                for s, j in enumerate(blocks(t)):
                    if s == skip:
                        continue
                    valid = _tile_mask(i, j, ri, ci) if masked else None
                    lgs[s] = [_sb_logits(z_ref[h // 2][:, (2 * s + h % 2) * BLK:(2 * s + h % 2 + 1) * BLK], valid)
                              for h in heads]
                    res[s] = _dot(jnp.concatenate([_two_terms(lgs[s][h][1]) for h in heads], axis=0), sums)
                ws = []
                for s, j in enumerate(blocks(t)):
                    valid = _tile_mask(i, j, ri, ci) if masked else None
                    for h in heads:
                        if s == skip:
                            ws.append(jnp.zeros((BLK, BLK), BF16))
                            continue
                        part = res[s][h * BLK:(h + 1) * BLK]
                        w = jnp.exp2(lgs[s][h][0] + part[:, :BLK] + runs[h])
                        ws.append((jnp.where(valid, w, 0.0) if masked else w).astype(BF16))
                        runs[h] = runs[h] + part[:, BLK:]
                for p in pairs:
                    w_ref[p] = jnp.concatenate([ws[s * n_heads + 2 * p + e] for s in range(2) for e in range(2)], axis=1)
                    z_ref[p] = z_next[p]
                o_ref[rows, :] = o_i + jnp.concatenate(outs, axis=1)
                for h in heads:
                    run_ref[h] = runs[h]

            n_calls = (i + 2) // 2

            @pl.when(odd == 1)
            def _():
                tile(0, True, skip=0)

            @pl.when(odd == 0)
            def _():
                tile(0, True)

            def mid(t, carry):
                tile(t, False)
                return carry

            lax.fori_loop(1, n_calls - 1, mid, 0)

            @pl.when(n_calls >= 2)
            def _():
                tile(n_calls - 1, True)

            o_ref[rows, :] += jnp.concatenate([_dot(w_ref[p], stacks(v_ref, blocks(n_calls - 1), p)) for p in pairs], axis=1)
            for h in heads:
                tot_ref[h // HEAD_GROUP, rows, h % HEAD_GROUP:h % HEAD_GROUP + 1] = run_ref[h][:, 0:1]
            return carry

        lax.fori_loop(0, nb, q_block, 0)

        @pl.when(step == groups - 1)
        def _():
            ex.wait(ex_in, ex_out, sems)

    spec = pl.BlockSpec((n_rows, gw), lambda g: (0, g))
    tot_spec = pl.BlockSpec((n_heads // HEAD_GROUP, n_rows, HEAD_GROUP), lambda g: (g, 0, 0))
    res = pl.pallas_call(
        body, name="attn_fwd", grid=(groups,), in_specs=_qkv_specs(n_rows, gw) + ex.specs,
        out_specs=[spec, tot_spec] + ex.specs,
        out_shape=[jax.ShapeDtypeStruct((n_rows, width), F32),
                   jax.ShapeDtypeStruct((width // (HEAD_GROUP * HEAD_DIM), n_rows, HEAD_GROUP), F32)] + ex.out_shape,
        scratch_shapes=[pltpu.VMEM((n_heads, BLK, BLK), F32), pltpu.VMEM((n_heads // 2, BLK, 4 * BLK), F32),
                        pltpu.VMEM((n_heads // 2, BLK, 4 * BLK), BF16)] + ex.scratch,
        compiler_params=_params(n_rows * (3 * gw * 2 + gw * 4 + LANES * 4), ("arbitrary",)),
    )(q, k, v, *ex_arrays)
    return res[0], res[1], res[2:]


def _attn_bwd(q, k, v, keep_total, do, ex, ex_arrays):
    n_rows, width = q.shape[0], HEADS * HEAD_DIM
    nb = n_rows // BLK
    gw = HEAD_GROUP * HEAD_DIM
    groups = width // gw

    def body(*refs):
        ((q_ref, k_ref, v_ref, tot_ref, do_ref), ex_in, (dq_ref, dk_ref, dv_ref), ex_out,
         (dq_acc, dk_acc, dv_acc, tot_b, run_ref, rung_ref, z_ref, dw_ref, dz_ref, wb_ref, qst_ref, dost_ref),
         sems) = _ride_along(ex, 5, 3, refs)
        step = pl.program_id(0)

        @pl.when(step == 0)
        def _():
            ex.start(ex_in, ex_out, sems)

        ri = lax.broadcasted_iota(jnp.int32, (BLK, BLK), 0)
        ci = lax.broadcasted_iota(jnp.int32, (BLK, BLK), 1)
        sums_keep = _tri_and_ones((ri <= ci).astype(F32), -1.0)
        sums_g = _tri_and_ones((ri < ci).astype(F32))
        first = ci < HEAD_DIM
        heads, pairs = range(HEAD_GROUP), range(HEAD_GROUP // 2)
        dk_acc[...] = jnp.zeros_like(dk_acc)
        dv_acc[...] = jnp.zeros_like(dv_acc)

        def clamp(j):
            return jnp.clip(j, 0, nb - 1)

        def stacks(ref, blocks, p):
            return jnp.concatenate([_head_stack(ref[_block_rows(clamp(j)), _pair_lanes(p)], first) for j in blocks], axis=0)

        def blocks(t):
            return 2 * t, 2 * t + 1

        def kept(ref, p, axis):
            tiles = [[ref[s * HEAD_GROUP + 2 * p + e] for e in range(2)] for s in range(2)]
            if axis == 1:
                return jnp.concatenate(tiles[0] + tiles[1], axis=1)
            return jnp.concatenate([jnp.concatenate(tiles[s], axis=0) for s in range(2)], axis=1)

        def owed_dq(t_prev):
            return [_dot(kept(dz_ref, p, 1), stacks(k_ref, blocks(t_prev), p)) for p in pairs]

        def owed_dk():
            return [_dot(kept(dz_ref, p, 0), qst_ref[p], TN) for p in pairs]

        def owed_dv():
            return [_dot(kept(wb_ref, p, 0), dost_ref[p], TN) for p in pairs]

        def settle(t_prev, parts):
            dq, dk, dv = parts
            dq_acc[...] += jnp.concatenate(dq, axis=1)
            for s, j in enumerate(blocks(t_prev)):
                cols = _block_rows(clamp(j))
                dk_acc[cols, :] += jnp.concatenate([d[s * BLK:(s + 1) * BLK] for d in dk], axis=1)
                dv_acc[cols, :] += jnp.concatenate([d[s * BLK:(s + 1) * BLK] for d in dv], axis=1)

        def q_block(i, carry):
            rows = _block_rows(i)
            dq_acc[...] = jnp.zeros_like(dq_acc)
            run_ref[...] = jnp.zeros_like(run_ref)
            rung_ref[...] = jnp.zeros_like(rung_ref)
            dz_ref[...] = jnp.zeros_like(dz_ref)
            wb_ref[...] = jnp.zeros_like(wb_ref)
            for h in heads:
                tot_b[h] = jnp.broadcast_to(tot_ref[rows, h:h + 1], (BLK, BLK))
            for p in pairs:
                qst_ref[p] = _head_stack(q_ref[rows, _pair_lanes(p)], first)
                dost_ref[p] = _head_stack(do_ref[rows, _pair_lanes(p)], first)
                z_ref[p] = _dot(q_ref[rows, _pair_lanes(p)], stacks(k_ref, blocks(0), p), NT)
                dw_ref[p] = _dot(do_ref[rows, _pair_lanes(p)], stacks(v_ref, blocks(0), p), NT)

            def tile(t, masked, skip=None):
                q_i, do_i = q_ref[rows, :], do_ref[rows, :]
                t_prev = jnp.maximum(t - 1, 0)
                valids = [_tile_mask(i, j, ri, ci) if masked else None for j in blocks(t)]
                tile_of = lambda ref, s, h: ref[h // 2][:, (2 * s + h % 2) * BLK:(2 * s + h % 2 + 1) * BLK]
                nothing = [jnp.zeros((BLK, BLK), F32)] * HEAD_GROUP
                runs = [run_ref[h] for h in heads]
                rungs = [rung_ref[h] for h in heads]
                lgs, keep, ws, gs, gsum, dzs = {}, {}, {}, {}, {}, {}
                for s in range(2):
                    if s != skip:
                        lgs[s] = [_sb_logits(tile_of(z_ref, s, h), valids[s]) for h in heads]
                        keep[s] = _dot(jnp.concatenate([_two_terms(lgs[s][h][1]) for h in heads], axis=0), sums_keep)
                    if s == 0:
                        part_dq = owed_dq(t_prev)
                    else:
                        part_dk = owed_dk()
                for s in range(2):
                    ws[s], gs[s] = [], []
                    if s != skip:
                        for h in heads:
                            part = keep[s][h * BLK:(h + 1) * BLK]
                            w = jnp.exp2(lgs[s][h][0] + (tot_b[h] - runs[h] - part[:, :BLK]))
                            ws[s].append(jnp.where(valids[s], w, 0.0) if masked else w)
                            runs[h] = runs[h] + part[:, BLK:]
                            gs[s].append(tile_of(dw_ref, s, h) * ws[s][h])
                        gsum[s] = _dot(jnp.concatenate([_two_terms(gs[s][h]) for h in heads], axis=0), sums_g)
                    else:
                        ws[s] = nothing
                    if s == 0:
                        part_dv = owed_dv()
                    else:
                        z_next = [_dot(q_i[:, _pair_lanes(p)], stacks(k_ref, blocks(t + 1), p), NT) for p in pairs]
                for s in range(2):
                    dzs[s] = []
                    if s != skip:
                        for h in heads:
                            part = gsum[s][h * BLK:(h + 1) * BLK]
                            beta = jnp.exp2(lgs[s][h][0])
                            dz = gs[s][h] * (1.0 - beta) - beta * (part[:, :BLK] + rungs[h])
                            dzs[s].append(jnp.where(valids[s], dz, 0.0) if masked else dz)
                            rungs[h] = rungs[h] + part[:, BLK:]
                    else:
                        dzs[s] = nothing
                    if s == 0:
                        dw_next = [_dot(do_i[:, _pair_lanes(p)], stacks(v_ref, blocks(t + 1), p), NT) for p in pairs]
                settle(t_prev, (part_dq, part_dk, part_dv))
                for s in range(2):
                    for h in heads:
                        dz_ref[s * HEAD_GROUP + h] = dzs[s][h].astype(BF16)
                        wb_ref[s * HEAD_GROUP + h] = ws[s][h].astype(BF16)
                for h in heads:
                    run_ref[h] = runs[h]
                    rung_ref[h] = rungs[h]
                for p in pairs:
                    z_ref[p] = z_next[p]
                    dw_ref[p] = dw_next[p]

            n_calls = (i + 2) // 2
            tile(0, True)

            def mid(t, carry):
                tile(t, False)
                return carry

            lax.fori_loop(1, n_calls - 1, mid, 0)

            @pl.when((n_calls >= 2) & (i % 2 == 1))
            def _():
                tile(n_calls - 1, True)

            @pl.when((n_calls >= 2) & (i % 2 == 0))
            def _():
                tile(n_calls - 1, True, skip=1)

            settle(n_calls - 1, (owed_dq(n_calls - 1), owed_dk(), owed_dv()))
            dq_ref[rows, :] = (dq_acc[...] * SB_SCALE).astype(dq_ref.dtype)
            return carry

        lax.fori_loop(0, nb, q_block, 0)
        dk_ref[...] = dk_acc[...].astype(dk_ref.dtype)
        dv_ref[...] = dv_acc[...].astype(dv_ref.dtype)

        @pl.when(step == groups - 1)
        def _():
            ex.wait(ex_in, ex_out, sems)

    spec = pl.BlockSpec((n_rows, gw), lambda g: (0, g))
    tot_spec = pl.BlockSpec((None, n_rows, HEAD_GROUP), lambda g: (g, 0, 0))
    out = jax.ShapeDtypeStruct((n_rows, width), BF16)
    tile_f32 = pltpu.VMEM((HEAD_GROUP, BLK, BLK), F32)
    tile_bf16 = pltpu.VMEM((2 * HEAD_GROUP, BLK, BLK), BF16)
    pair_f32 = pltpu.VMEM((HEAD_GROUP // 2, BLK, 4 * BLK), F32)
    pair_stack = pltpu.VMEM((HEAD_GROUP // 2, 2 * BLK, BLK), BF16)
    res = pl.pallas_call(
        body, name="attn_bwd", grid=(groups,), in_specs=_qkv_specs(n_rows, gw) + [tot_spec, spec] + ex.specs,
        out_specs=[spec] * 3 + ex.specs, out_shape=[out] * 3 + ex.out_shape,
        scratch_shapes=[pltpu.VMEM((BLK, gw), F32), pltpu.VMEM((n_rows, gw), F32), pltpu.VMEM((n_rows, gw), F32),
                        tile_f32, tile_f32, tile_f32, pair_f32, pair_f32, tile_bf16, tile_bf16, pair_stack,
                        pair_stack] + ex.scratch,
        compiler_params=_params(n_rows * (7 * gw * 2 + LANES * 4 + gw * 4), ("arbitrary",)),
    )(q, k, v, keep_total, do, *ex_arrays)
    return res[0], res[1], res[2], res[3:]


class _Exchange:
    def __init__(self, arrays, gather, same_core=False):
        self.n = len(arrays)
        self.gather = gather
        self.same_core = same_core
        self.out_shape = [jax.ShapeDtypeStruct(((N_DEV,) + a.shape) if gather else a.shape, a.dtype) for a in arrays]
        self.scratch = [pltpu.SemaphoreType.DMA((self.n, N_DEV - 1)), pltpu.SemaphoreType.DMA((self.n, N_DEV - 1)),
                        pltpu.SemaphoreType.DMA((self.n,))]
        self.specs = [pl.BlockSpec(memory_space=pl.ANY)] * self.n

    def _copies(self, ins, outs, sems, with_receives):
        send_sems, recv_sems, local_sems = sems
        x, y, c = lax.axis_index("x"), lax.axis_index("y"), lax.axis_index("c")
        gather, same_core = self.gather, self.same_core
        me = 2 * x + y if same_core else 4 * x + 2 * y + c
        local, sends, recvs = [], [], []
        for a in range(self.n):
            local.append(pltpu.make_async_copy(ins[a] if gather else ins[a].at[me], outs[a].at[me if gather else 0],
                                               local_sems.at[a]))
        for r in ((2, 4, 6) if same_core else range(1, N_DEV)):
            px = 1 - x if r & 4 else x
            py = 1 - y if r & 2 else y
            pc = 1 - c if r & 1 else c
            idx = 2 * px + py if same_core else 4 * px + 2 * py + pc
            slot = r // 2 if same_core else r
            for a in range(self.n):
                src = ins[a] if gather else ins[a].at[idx]
                pair = dict(send_sem=send_sems.at[a, r - 1], recv_sem=recv_sems.at[a, r - 1],
                            device_id=(px, py, pc), device_id_type=pl.DeviceIdType.MESH)
                sends.append(pltpu.make_async_remote_copy(src_ref=src, dst_ref=outs[a].at[me if gather else slot], **pair))
                if with_receives:
                    recvs.append(pltpu.make_async_remote_copy(src_ref=src, dst_ref=outs[a].at[idx if gather else slot], **pair))
        return local, sends, recvs

    def start(self, ins, outs, sems):
        local, sends, _ = self._copies(ins, outs, sems, with_receives=False)
        for cp in local + sends:
            cp.start()

    def wait(self, ins, outs, sems):
        local, sends, recvs = self._copies(ins, outs, sems, with_receives=True)
        for cp in recvs:
            cp.wait_recv()
        for cp in sends:
            cp.wait_send()
        for cp in local:
            cp.wait()


def _peer_exchange(name, parts):
    exs = [_Exchange(arrays, gather) for arrays, gather in parts]
    n = sum(ex.n for ex in exs)
    n_sems = len(exs[0].scratch)

    def body(*refs):
        at, views = 0, []
        for k, ex in enumerate(exs):
            views.append((ex, refs[at:at + ex.n], refs[n + at:n + at + ex.n],
                          refs[2 * n + k * n_sems:2 * n + (k + 1) * n_sems]))
            at += ex.n
        for ex, ins, outs, sems in views:
            ex.start(ins, outs, sems)
        for ex, ins, outs, sems in views:
            ex.wait(ins, outs, sems)

    res = pl.pallas_call(
        body, name=name, in_specs=[s for ex in exs for s in ex.specs], out_specs=[s for ex in exs for s in ex.specs],
        out_shape=[s for ex in exs for s in ex.out_shape], scratch_shapes=[s for ex in exs for s in ex.scratch],
    )(*[a for arrays, _ in parts for a in arrays])
    out, at = [], 0
    for ex in exs:
        out.append(res[at:at + ex.n])
        at += ex.n
    return out


def _gather_two_level(name, arrays):
    n = len(arrays)

    def body(*refs):
        ins, outs = refs[:n], refs[n:2 * n]
        send_sems, recv_sems, local_sems = refs[2 * n:]
        x, y, c = lax.axis_index("x"), lax.axis_index("y"), lax.axis_index("c")
        sibling = (x, y, 1 - c)
        chips = [(1 - x, y), (x, 1 - y), (1 - x, 1 - y)]

        def slot(a, dev):
            return outs[a].at[4 * dev[0] + 2 * dev[1] + dev[2]]

        def copy(a, k, block, to, src=None):
            return pltpu.make_async_remote_copy(
                src_ref=slot(a, block) if src is None else src, dst_ref=slot(a, block),
                send_sem=send_sems.at[a, k], recv_sem=recv_sems.at[a, k], device_id=to, device_id_type=pl.DeviceIdType.MESH)

        me = (x, y, c)
        mine = [pltpu.make_async_copy(ins[a], slot(a, me), local_sems.at[a]) for a in range(n)]
        first = [copy(a, 0, me, sibling, src=ins[a]) for a in range(n)]
        first += [copy(a, 1 + j, me, (*chip, c), src=ins[a]) for j, chip in enumerate(chips) for a in range(n)]
        for cp in mine + first:
            cp.start()
        passed = []
        for j, chip in enumerate(chips):
            for a in range(n):
                copy(a, 1 + j, (*chip, c), me).wait_recv()
            for a in range(n):
                cp = copy(a, 4 + j, (*chip, c), sibling)
                cp.start()
                passed.append(cp)
        for a in range(n):
            copy(a, 0, sibling, me).wait_recv()
            for j, chip in enumerate(chips):
                copy(a, 4 + j, (*chip, 1 - c), me).wait_recv()
        for cp in first + passed:
            cp.wait_send()
        for cp in mine:
            cp.wait()

    any_spec = pl.BlockSpec(memory_space=pl.ANY)
    return pl.pallas_call(
        body, name=name, in_specs=[any_spec] * n, out_specs=[any_spec] * n,
        out_shape=[jax.ShapeDtypeStruct((N_DEV,) + a.shape, a.dtype) for a in arrays],
        scratch_shapes=[pltpu.SemaphoreType.DMA((n, N_DEV - 1)), pltpu.SemaphoreType.DMA((n, N_DEV - 1)),
                        pltpu.SemaphoreType.DMA((n,))],
    )(*arrays)


def _sibling_swap(name, slabs):
    chips = N_DEV // 2

    def body(x_ref, o_ref, send_sems, recv_sems):
        x, y, c = lax.axis_index("x"), lax.axis_index("y"), lax.axis_index("c")
        copies = [pltpu.make_async_remote_copy(
            src_ref=x_ref.at[2 * m + 1 - c], dst_ref=o_ref.at[m], send_sem=send_sems.at[m], recv_sem=recv_sems.at[m],
            device_id=(x, y, 1 - c), device_id_type=pl.DeviceIdType.MESH) for m in range(chips)]
        for cp in copies:
            cp.start()
        for cp in copies:
            cp.wait()

    any_spec = pl.BlockSpec(memory_space=pl.ANY)
    return pl.pallas_call(
        body, name=name, in_specs=[any_spec], out_specs=any_spec,
        out_shape=jax.ShapeDtypeStruct((chips,) + slabs.shape[1:], slabs.dtype),
        scratch_shapes=[pltpu.SemaphoreType.DMA((chips,)), pltpu.SemaphoreType.DMA((chips,))],
    )(slabs)


def _pair_sum(name, a, b):
    rows, cols = a.shape
    tr = 512

    def body(a_ref, b_ref, o_ref):
        o_ref[...] = (a_ref[...].astype(F32) + b_ref[...].astype(F32)).astype(o_ref.dtype)

    spec = pl.BlockSpec((tr, cols), lambda i: (i, 0))
    return pl.pallas_call(body, name=name, grid=(rows // tr,), in_specs=[spec, spec], out_specs=spec,
                          out_shape=jax.ShapeDtypeStruct(a.shape, a.dtype),
                          compiler_params=_params(3 * tr * cols * 4, ("parallel",)))(a, b)


def _sum_slots(name, x):
    def body(x_ref, o_ref):
        acc = x_ref[0]
        for s in range(1, N_DEV):
            acc = acc + x_ref[s]
        o_ref[...] = acc
    return pl.pallas_call(body, name=name, out_shape=jax.ShapeDtypeStruct(x.shape[1:], F32))(x)


def _adamw(name, w, slots, m, v):
    n_slots, rows, cols = slots.shape
    tr = next((t for t in (256, 176, 128) if rows % t == 0 and rows > t), rows)

    def body(w_ref, s_ref, m_ref, v_ref, g_ref, d_ref, nm_ref, nv_ref):
        g = s_ref[0].astype(F32)
        for s in range(1, n_slots):
            g = g + s_ref[s].astype(F32)
        nm = ADAM_B1 * m_ref[...] + (1.0 - ADAM_B1) * g
        nv = ADAM_B2 * v_ref[...] + (1.0 - ADAM_B2) * (g * g)
        m_hat = nm / (1.0 - ADAM_B1 ** ADAM_STEP)
        v_hat = nv / (1.0 - ADAM_B2 ** ADAM_STEP)
        g_ref[...] = g
        d_ref[...] = -ADAM_LR * (m_hat / (jnp.sqrt(v_hat) + ADAM_EPS) + ADAM_WD * w_ref[...])
        nm_ref[...] = nm
        nv_ref[...] = nv

    spec = pl.BlockSpec((tr, cols), lambda i: (i, 0))
    out = jax.ShapeDtypeStruct((rows, cols), F32)
    return pl.pallas_call(
        body, name=name, grid=(rows // tr,),
        in_specs=[spec, pl.BlockSpec((n_slots, tr, cols), lambda i: (0, i, 0)), spec, spec],
        out_specs=[spec] * 4, out_shape=[out] * 4,
        compiler_params=_params((n_slots + 7) * tr * cols * 4, ("parallel",)),
    )(w, slots, m, v)


def _pad_lanes(a):
    return jnp.pad(a, ((0, 0), (0, LANES - a.shape[1])))


def kernel(x, meta_tokens, mix_pre_g, w_in, ssd_conv_w, ssd_conv_b, ssd_dt_bias, ssd_a_log, ssd_d, ssd_norm_g, sb_norm_g, w_out, mix_post_g, ffn_pre_g, w_up, ffn_conv_w, ffn_conv_b, w_down, ffn_post_g, loss_target, m_meta_tokens, m_mix_pre_g, m_w_in, m_ssd_conv_w, m_ssd_conv_b, m_ssd_dt_bias, m_ssd_a_log, m_ssd_d, m_ssd_norm_g, m_sb_norm_g, m_w_out, m_mix_post_g, m_ffn_pre_g, m_w_up, m_ffn_conv_w, m_ffn_conv_b, m_w_down, m_ffn_post_g, v_meta_tokens, v_mix_pre_g, v_w_in, v_ssd_conv_w, v_ssd_conv_b, v_ssd_dt_bias, v_ssd_a_log, v_ssd_d, v_ssd_norm_g, v_sb_norm_g, v_w_out, v_mix_post_g, v_ffn_pre_g, v_w_up, v_ffn_conv_w, v_ffn_conv_b, v_w_down, v_ffn_post_g):
    seq = x.shape[1]
    me = 4 * lax.axis_index("x") + 2 * lax.axis_index("y") + lax.axis_index("c")
    in_cols = w_in.shape[2]
    up_cols = w_up.shape[2]
    out_rows = w_out.shape[1]
    down_rows = w_down.shape[1]

    g_in, g_meta, g_scw, g_fcw = _gather_two_level(
        "gather_w_in", [w_in[0].astype(BF16), meta_tokens, ssd_conv_w[0], ffn_conv_w[0]])
    late_weights = [w_out[0].astype(BF16), w_up[0].astype(BF16), w_down[0].astype(BF16)]
    w_in_full = g_in.transpose(1, 0, 2).reshape(D_MODEL, N_DEV * in_cols)
    off = [0, SSD_INNER, SSD_INNER + XBC, SSD_INNER + XBC + HEADS]
    w_z = w_in_full[:, off[0]:off[1]]
    w_xbc = w_in_full[:, off[1]:off[2]]
    w_dt = _pad_lanes(w_in_full[:, off[2]:off[3]])
    w_q = w_in_full[:, off[3]:off[3] + SSD_INNER]
    w_k = w_in_full[:, off[3] + SSD_INNER:off[3] + 2 * SSD_INNER]
    w_v = w_in_full[:, off[3] + 2 * SSD_INNER:off[3] + 3 * SSD_INNER]
    meta_full = g_meta.transpose(1, 0, 2).reshape(N_META, D_MODEL)
    scw_full = g_scw.transpose(1, 0, 2).reshape(SSD_CONV, XBC)
    fcw_full = g_fcw.transpose(1, 0, 2).reshape(FFN_CONV, D_FF)

    dt_bias_p, a_log_p, d_p = _pad_lanes(ssd_dt_bias), _pad_lanes(ssd_a_log), _pad_lanes(ssd_d)

    h0 = jnp.concatenate([jnp.zeros((PAD, D_MODEL), F32), meta_full, x[0]], axis=0)
    target = loss_target[0]
    xn1 = _rms_fwd("rms_pre_mix", h0, mix_pre_g)
    z = _mm("proj_z", [(xn1, w_z)], "nn", F32)
    xbc_raw = _mm("proj_xbc", [(xn1, w_xbc)], "nn", F32)
    dtr = _mm("proj_dt", [(xn1, w_dt)], "nn", F32)
    q = k = v = _mm("proj_qkv", [(xn1, jnp.concatenate([w_q * SB_SCALE, w_k, w_v], axis=1))], "nn", BF16)
    xbc_act = _ssd_conv_fwd(xbc_raw, scw_full, ssd_conv_b)
    ypre, y_ssd, states = _ssd_fwd(xbc_act, dtr, z, dt_bias_p, a_log_p, d_p, ssd_norm_g)
    o, keep_total, (g_out, g_up, g_down) = _attn_fwd(q, k, v, _Exchange(late_weights, gather=True), late_weights)
    w_out_full = g_out.reshape(N_DEV * out_rows, D_MODEL)
    wo_ssd, wo_sb = w_out_full[:SSD_INNER], w_out_full[SSD_INNER:]
    w_up_full = g_up.transpose(1, 0, 2).reshape(D_MODEL, N_DEV * up_cols)
    w_gate, w_lin = w_up_full[:, :D_FF], w_up_full[:, D_FF:]
    w_down_full = g_down.reshape(N_DEV * down_rows, D_MODEL)
    y_sb = _rms_fwd("rms_sb", o, sb_norm_g)
    mix = _mm("mix_out", [(y_ssd, wo_ssd), (y_sb, wo_sb)], "nn", F32)
    h1, xn2 = _mix_post(mix, h0, mix_post_g, ffn_pre_g)
    g_raw = _mm("ffn_gate", [(xn2, w_gate)], "nn", F32)
    u = _mm("ffn_lin", [(xn2, w_lin)], "nn", F32)
    act = _ffn_act(g_raw, u, fcw_full, ffn_conv_b)
    f = _mm("ffn_down", [(act, w_down_full)], "nn", F32)
    dh2, df, loss_row, dg_ffn_post = _loss_post(f, h1, target, ffn_post_g)

    dact = _mm("d_act", [(df, w_down_full)], "nt", F32)
    dw_down = _mm("dw_down", [(act, df)], "tn", F32)
    dg_conv, du = _ffn_bwd_act(dact, u, g_raw, fcw_full, ffn_conv_b)
    dg_raw, dfcw0, dfcw1, dfcw2, dfcb = _conv_bwd("ffn_conv_bwd", dg_conv, g_raw, fcw_full, FFN_CONV)
    dxn2 = _mm("d_xn2", [(dg_raw, w_gate), (du, w_lin)], "nt", F32)
    dw_gate = _mm("dw_gate", [(xn2, dg_raw)], "tn", F32)
    dw_lin = _mm("dw_lin", [(xn2, du)], "tn", F32)
    dh1, dmix, dg_ffn_pre, dg_mix_post = _mid_bwd(dxn2, h1, dh2, mix, ffn_pre_g, mix_post_g)

    dy_ssd = _mm("d_yssd", [(dmix, wo_ssd)], "nt", F32)
    dy_sb = _mm("d_ysb", [(dmix, wo_sb)], "nt", F32)
    dwo_ssd = _mm("dw_out_ssd", [(y_ssd, dmix)], "tn", F32)
    dwo_sb = _mm("dw_out_sb", [(y_sb, dmix)], "tn", F32)
    do, dg_sb = _norm_bwd("sb_norm_bwd", dy_sb, o, sb_norm_g)
    half = N_DEV // 2
    early_slabs = [
        jnp.concatenate([dwo_ssd, dwo_sb], axis=0).reshape(N_DEV, out_rows, D_MODEL),
        jnp.concatenate([dw_gate.reshape(D_MODEL, half, up_cols).transpose(1, 0, 2),
                         dw_lin.reshape(D_MODEL, half, up_cols).transpose(1, 0, 2)], axis=0),
        dw_down.reshape(N_DEV, down_rows, D_MODEL)]
    dq, dk, dv, (l_out, l_up, l_down) = _attn_bwd(q, k, v, keep_total, do, _Exchange(early_slabs, gather=False), early_slabs)
    dz, dxbc_act, ddtr, dg_ssd_norm, dd_skip, da_log, ddt_bias = _ssd_bwd(
        dy_ssd, ypre, z, xbc_act, dtr, states, dt_bias_p, a_log_p, d_p, ssd_norm_g)
    dconv = _ssd_conv_bwd_act(dxbc_act, xbc_raw, scw_full, ssd_conv_b)
    dxbc_raw, dscw0, dscw1, dscw2, dscw3, dscb = _conv_bwd("ssd_conv_bwd", dconv, xbc_raw, scw_full, SSD_CONV)
    segs = [(dz, w_z), (dxbc_raw, w_xbc), (ddtr, w_dt), (dq, w_q), (dk, w_k), (dv, w_v)]
    dw_segs = [_mm("dw_in_%d" % s, [(xn1, d)], "tn", BF16) for s, (d, _) in enumerate(segs)]
    dw_segs[2] = dw_segs[2][:, :HEADS]
    dw_in = jnp.concatenate(dw_segs, axis=1)
    chips = N_DEV // 2
    slab_in = dw_in.reshape(D_MODEL, N_DEV, in_cols).transpose(1, 0, 2)
    from_sibling = _sibling_swap("swap_dw_in", slab_in)
    mine = lax.dynamic_index_in_dim(slab_in.reshape(chips, 2, D_MODEL, in_cols), lax.axis_index("c"), 1, keepdims=False)
    pair = _pair_sum("pair_sum_dw_in", mine.reshape(chips * D_MODEL, in_cols),
                     from_sibling.reshape(chips * D_MODEL, in_cols)).reshape(chips, D_MODEL, in_cols)
    dxn1, (l_in,) = _mm("d_xn1", segs, "nt", F32, _Exchange([pair], gather=False, same_core=True), [pair])
    dh0, dg_mix_pre = _first_bwd(dxn1, h0, dh1, mix_pre_g)
    grad_x = dh0[BLK:][None]

    small = [dg_mix_pre, dscb, ddt_bias, da_log, dd_skip, dg_ssd_norm, dg_sb, dg_mix_post, dg_ffn_pre, dfcb,
             dg_ffn_post, dh0[PAD:BLK].reshape(1, -1), dscw0, dscw1, dscw2, dscw3, dfcw0, dfcw1, dfcw2, loss_row]
    sizes = [a.shape[1] for a in small]
    total = sum(sizes)
    rows_packed = -(-total // (LANES * HALO)) * HALO
    packed = jnp.pad(jnp.concatenate(small, axis=1), ((0, 0), (0, rows_packed * LANES - total)))
    ((gathered,),) = _peer_exchange("gather_small_grads", [([packed.reshape(rows_packed, LANES)], True)])
    summed = _sum_slots("sum_small_grads", gathered).reshape(1, rows_packed * LANES)
    pieces, at = [], 0
    for s in sizes:
        pieces.append(summed[:, at:at + s])
        at += s
    (g_mix_pre, g_scb, g_dtb, g_alog, g_dskip, g_ssd_norm, g_sb, g_mix_post, g_ffn_pre, g_fcb, g_ffn_post,
     g_meta_flat, gs0, gs1, gs2, gs3, gf0, gf1, gf2, loss_all) = pieces
    loss = loss_all[0, 0]
    g_dtb, g_alog, g_dskip = g_dtb[:, :HEADS], g_alog[:, :HEADS], g_dskip[:, :HEADS]
    g_meta_full = g_meta_flat.reshape(N_META, D_MODEL)
    g_scw_full = jnp.concatenate([gs0, gs1, gs2, gs3], axis=0)
    g_fcw_full = jnp.concatenate([gf0, gf1, gf2], axis=0)
    meta_cols, scw_cols, fcw_cols = meta_tokens.shape[1], ssd_conv_w.shape[2], ffn_conv_w.shape[2]
    g_meta_mine = lax.dynamic_slice(g_meta_full, (0, me * meta_cols), (N_META, meta_cols))
    g_scw_mine = lax.dynamic_slice(g_scw_full, (0, me * scw_cols), (SSD_CONV, scw_cols))
    g_fcw_mine = lax.dynamic_slice(g_fcw_full, (0, me * fcw_cols), (FFN_CONV, fcw_cols))

    def lead(a):
        return a[None]

    upd = [
        _adamw("adamw_meta", meta_tokens, lead(g_meta_mine), m_meta_tokens, v_meta_tokens),
        _adamw("adamw_mix_pre_g", mix_pre_g, lead(g_mix_pre), m_mix_pre_g, v_mix_pre_g),
        [lead(a) for a in _adamw("adamw_w_in", w_in[0], l_in, m_w_in[0], v_w_in[0])],
        [lead(a) for a in _adamw("adamw_ssd_conv_w", ssd_conv_w[0], lead(g_scw_mine), m_ssd_conv_w[0], v_ssd_conv_w[0])],
        _adamw("adamw_ssd_conv_b", ssd_conv_b, lead(g_scb), m_ssd_conv_b, v_ssd_conv_b),
        _adamw("adamw_ssd_dt_bias", ssd_dt_bias, lead(g_dtb), m_ssd_dt_bias, v_ssd_dt_bias),
        _adamw("adamw_ssd_a_log", ssd_a_log, lead(g_alog), m_ssd_a_log, v_ssd_a_log),
        _adamw("adamw_ssd_d", ssd_d, lead(g_dskip), m_ssd_d, v_ssd_d),
        _adamw("adamw_ssd_norm_g", ssd_norm_g, lead(g_ssd_norm), m_ssd_norm_g, v_ssd_norm_g),
        _adamw("adamw_sb_norm_g", sb_norm_g, lead(g_sb), m_sb_norm_g, v_sb_norm_g),
        [lead(a) for a in _adamw("adamw_w_out", w_out[0], l_out, m_w_out[0], v_w_out[0])],
        _adamw("adamw_mix_post_g", mix_post_g, lead(g_mix_post), m_mix_post_g, v_mix_post_g),
        _adamw("adamw_ffn_pre_g", ffn_pre_g, lead(g_ffn_pre), m_ffn_pre_g, v_ffn_pre_g),
        [lead(a) for a in _adamw("adamw_w_up", w_up[0], l_up, m_w_up[0], v_w_up[0])],
        [lead(a) for a in _adamw("adamw_ffn_conv_w", ffn_conv_w[0], lead(g_fcw_mine), m_ffn_conv_w[0], v_ffn_conv_w[0])],
        _adamw("adamw_ffn_conv_b", ffn_conv_b, lead(g_fcb), m_ffn_conv_b, v_ffn_conv_b),
        [lead(a) for a in _adamw("adamw_w_down", w_down[0], l_down, m_w_down[0], v_w_down[0])],
        _adamw("adamw_ffn_post_g", ffn_post_g, lead(g_ffn_post), m_ffn_post_g, v_ffn_post_g),
    ]
    grads = [u_[0] for u_ in upd]
    deltas = [u_[1] for u_ in upd]
    new_m = [u_[2] for u_ in upd]
    new_v = [u_[3] for u_ in upd]
    return (loss, grad_x, *grads, *deltas, *new_m, *new_v)
```
